```python
import jax, jax.numpy as jnp
from jax import lax
import numpy as np

D_MODEL = 1024
BATCH = 8
SEQ = 16384
DEPTH = 1

W_A = 3 * D_MODEL // 2
CONV_K = 3
CHUNK = 128
DH_B = 128
H_B = D_MODEL // DH_B
W_B = H_B * DH_B
D_FF = 4 * D_MODEL
N_PROJ = 3 * W_A + 2 * W_B + 2 * D_MODEL
LN_EPS = 1e-5
ALPHA = (2.0 * DEPTH) ** 0.25
BETA = (8.0 * DEPTH) ** -0.25

OFF_BA = 0
OFF_CA = OFF_BA + W_A
OFF_HA = OFF_CA + W_A
OFF_UB = OFF_HA + W_A
OFF_VB = OFF_UB + W_B
OFF_GA = OFF_VB + W_B
OFF_GB = OFF_GA + D_MODEL

kernel_name = "hybrid_shortconv_gmlp_deepnorm_encoder"


def _layernorm(x, g, b):
    xf = x.astype(jnp.float32)
    mu = jnp.mean(xf, axis=-1, keepdims=True)
    var = jnp.mean(jnp.square(xf - mu), axis=-1, keepdims=True)
    y = (xf - mu) * lax.rsqrt(var + LN_EPS) * g.astype(jnp.float32) + b.astype(jnp.float32)
    return y.astype(x.dtype)


def _centred_depthwise_conv(h, w):
    hp = jnp.pad(h, ((0, 0), (1, 1), (0, 0)))
    return hp[:, :-2, :] * w[0] + hp[:, 1:-1, :] * w[1] + hp[:, 2:, :] * w[2]


def _spatial_gate(u, v, g, b, w_s, b_s):
    bsz = v.shape[0]
    s = v.shape[1]
    n_chunks = s // CHUNK
    v = _layernorm(v, g, b)
    vc = v.reshape(bsz, n_chunks, CHUNK, H_B, DH_B)
    mixed = jnp.einsum('hij,bcjhd->bcihd', w_s, vc)
    mixed = mixed + jnp.transpose(b_s)[None, None, :, :, None]
    return u * mixed.reshape(bsz, s, W_B)


def _fwd_setup_inputs(seed: int = 0) -> dict:
    key = jax.random.key(seed)
    ks = jax.random.split(key, 20)
    L = DEPTH

    def nrm(k, shape, scale):
        return jax.random.normal(k, shape, jnp.float32) * scale

    return {
        "x": nrm(ks[0], (BATCH, SEQ, D_MODEL), 1.0),
        "w_in": nrm(ks[1], (L, D_MODEL, N_PROJ), D_MODEL ** -0.5),
        "b_gate": nrm(ks[2], (L, 2 * D_MODEL), 0.02),
        "conv_w": nrm(ks[3], (L, CONV_K, W_A), CONV_K ** -0.5),
        "v_norm_g": 1.0 + nrm(ks[4], (L, W_B), 0.02),
        "v_norm_b": nrm(ks[5], (L, W_B), 0.02),
        "w_s": nrm(ks[6], (L, H_B, CHUNK, CHUNK), CHUNK ** -0.5),
        "b_s": 1.0 + nrm(ks[7], (L, H_B, CHUNK), 0.02),
        "w_pa": nrm(ks[8], (L, W_A, D_MODEL), W_A ** -0.5),
        "w_pb": nrm(ks[9], (L, W_B, D_MODEL), W_B ** -0.5),
        "w_o": nrm(ks[10], (L, D_MODEL, D_MODEL), BETA * D_MODEL ** -0.5),
        "ln1_g": 1.0 + nrm(ks[11], (L, D_MODEL), 0.02),
        "ln1_b": nrm(ks[12], (L, D_MODEL), 0.02),
        "w_ff1": nrm(ks[13], (L, D_MODEL, D_FF), BETA * D_MODEL ** -0.5),
        "w_ff2": nrm(ks[14], (L, D_FF, D_MODEL), BETA * D_FF ** -0.5),
        "ln2_g": 1.0 + nrm(ks[15], (L, D_MODEL), 0.02),
        "ln2_b": nrm(ks[16], (L, D_MODEL), 0.02),
    }


def _fwd_reference(x, w_in, b_gate, conv_w, v_norm_g, v_norm_b, w_s, b_s, w_pa, w_pb, w_o,
              ln1_g, ln1_b, w_ff1, w_ff2, ln2_g, ln2_b):
    for l in range(DEPTH):
        p = jnp.einsum('bsd,dn->bsn', x, w_in[l])
        b_a = p[:, :, OFF_BA:OFF_CA]
        c_a = p[:, :, OFF_CA:OFF_HA]
        h_a = p[:, :, OFF_HA:OFF_UB]
        u_b = p[:, :, OFF_UB:OFF_VB]
        v_b = p[:, :, OFF_VB:OFF_GA]
        gates = jax.nn.sigmoid(p[:, :, OFF_GA:N_PROJ] + b_gate[l])
        g_a = gates[:, :, :D_MODEL]
        g_b = gates[:, :, D_MODEL:]
        a = b_a * _centred_depthwise_conv(c_a * h_a, conv_w[l])
        bb = _spatial_gate(jax.nn.gelu(u_b), jax.nn.gelu(v_b), v_norm_g[l], v_norm_b[l],
                           w_s[l], b_s[l])
        y_a = jnp.einsum('bsc,cd->bsd', a, w_pa[l])
        y_b = jnp.einsum('bsc,cd->bsd', bb, w_pb[l])
        mix = jnp.einsum('bsd,de->bse', g_a * y_a + g_b * y_b, w_o[l])
        x = _layernorm(ALPHA * x + mix, ln1_g[l], ln1_b[l])
        hid = jnp.square(jax.nn.relu(jnp.einsum('bsd,df->bsf', x, w_ff1[l])))
        ffn = jnp.einsum('bsf,fd->bsd', hid, w_ff2[l])
        x = _layernorm(ALPHA * x + ffn, ln2_g[l], ln2_b[l])
    return x


import jax as _jax
import jax.numpy as _jnp

TWIN_FORMAT = 'train_step'
FWD_PARAMS = ['x', 'w_in', 'b_gate', 'conv_w', 'v_norm_g', 'v_norm_b', 'w_s', 'b_s', 'w_pa', 'w_pb', 'w_o', 'ln1_g', 'ln1_b', 'w_ff1', 'w_ff2', 'ln2_g', 'ln2_b']
TWIN_WEIGHTS = ['w_in', 'b_gate', 'conv_w', 'v_norm_g', 'v_norm_b', 'w_s', 'b_s', 'w_pa', 'w_pb', 'w_o', 'ln1_g', 'ln1_b', 'w_ff1', 'w_ff2', 'ln2_g', 'ln2_b']
TWIN_DIFF_INPUT = 'x'
TWIN_INPUTS = ['x', 'w_in', 'b_gate', 'conv_w', 'v_norm_g', 'v_norm_b', 'w_s', 'b_s', 'w_pa', 'w_pb', 'w_o', 'ln1_g', 'ln1_b', 'w_ff1', 'w_ff2', 'ln2_g', 'ln2_b', 'loss_target', 'm_w_in', 'm_b_gate', 'm_conv_w', 'm_v_norm_g', 'm_v_norm_b', 'm_w_s', 'm_b_s', 'm_w_pa', 'm_w_pb', 'm_w_o', 'm_ln1_g', 'm_ln1_b', 'm_w_ff1', 'm_w_ff2', 'm_ln2_g', 'm_ln2_b', 'v_w_in', 'v_b_gate', 'v_conv_w', 'v_v_norm_g', 'v_v_norm_b', 'v_w_s', 'v_b_s', 'v_w_pa', 'v_w_pb', 'v_w_o', 'v_ln1_g', 'v_ln1_b', 'v_w_ff1', 'v_w_ff2', 'v_ln2_g', 'v_ln2_b']
TWIN_OUTPUTS = ['loss', 'grad_x', 'grad_w_in', 'grad_b_gate', 'grad_conv_w', 'grad_v_norm_g', 'grad_v_norm_b', 'grad_w_s', 'grad_b_s', 'grad_w_pa', 'grad_w_pb', 'grad_w_o', 'grad_ln1_g', 'grad_ln1_b', 'grad_w_ff1', 'grad_w_ff2', 'grad_ln2_g', 'grad_ln2_b', 'delta_w_in', 'delta_b_gate', 'delta_conv_w', 'delta_v_norm_g', 'delta_v_norm_b', 'delta_w_s', 'delta_b_s', 'delta_w_pa', 'delta_w_pb', 'delta_w_o', 'delta_ln1_g', 'delta_ln1_b', 'delta_w_ff1', 'delta_w_ff2', 'delta_ln2_g', 'delta_ln2_b', 'new_m_w_in', 'new_m_b_gate', 'new_m_conv_w', 'new_m_v_norm_g', 'new_m_v_norm_b', 'new_m_w_s', 'new_m_b_s', 'new_m_w_pa', 'new_m_w_pb', 'new_m_w_o', 'new_m_ln1_g', 'new_m_ln1_b', 'new_m_w_ff1', 'new_m_w_ff2', 'new_m_ln2_g', 'new_m_ln2_b', 'new_v_w_in', 'new_v_b_gate', 'new_v_conv_w', 'new_v_v_norm_g', 'new_v_v_norm_b', 'new_v_w_s', 'new_v_b_s', 'new_v_w_pa', 'new_v_w_pb', 'new_v_w_o', 'new_v_ln1_g', 'new_v_ln1_b', 'new_v_w_ff1', 'new_v_w_ff2', 'new_v_ln2_g', 'new_v_ln2_b']
TWIN_LEAF_KINDS = {'loss': 'loss', 'grad_x': 'grad_x', 'grad_w_in': 'grad_w', 'grad_b_gate': 'grad_w', 'grad_conv_w': 'grad_w', 'grad_v_norm_g': 'grad_w', 'grad_v_norm_b': 'grad_w', 'grad_w_s': 'grad_w', 'grad_b_s': 'grad_w', 'grad_w_pa': 'grad_w', 'grad_w_pb': 'grad_w', 'grad_w_o': 'grad_w', 'grad_ln1_g': 'grad_w', 'grad_ln1_b': 'grad_w', 'grad_w_ff1': 'grad_w', 'grad_w_ff2': 'grad_w', 'grad_ln2_g': 'grad_w', 'grad_ln2_b': 'grad_w', 'delta_w_in': 'delta_w', 'delta_b_gate': 'delta_w', 'delta_conv_w': 'delta_w', 'delta_v_norm_g': 'delta_w', 'delta_v_norm_b': 'delta_w', 'delta_w_s': 'delta_w', 'delta_b_s': 'delta_w', 'delta_w_pa': 'delta_w', 'delta_w_pb': 'delta_w', 'delta_w_o': 'delta_w', 'delta_ln1_g': 'delta_w', 'delta_ln1_b': 'delta_w', 'delta_w_ff1': 'delta_w', 'delta_w_ff2': 'delta_w', 'delta_ln2_g': 'delta_w', 'delta_ln2_b': 'delta_w', 'new_m_w_in': 'new_m', 'new_m_b_gate': 'new_m', 'new_m_conv_w': 'new_m', 'new_m_v_norm_g': 'new_m', 'new_m_v_norm_b': 'new_m', 'new_m_w_s': 'new_m', 'new_m_b_s': 'new_m', 'new_m_w_pa': 'new_m', 'new_m_w_pb': 'new_m', 'new_m_w_o': 'new_m', 'new_m_ln1_g': 'new_m', 'new_m_ln1_b': 'new_m', 'new_m_w_ff1': 'new_m', 'new_m_w_ff2': 'new_m', 'new_m_ln2_g': 'new_m', 'new_m_ln2_b': 'new_m', 'new_v_w_in': 'new_v', 'new_v_b_gate': 'new_v', 'new_v_conv_w': 'new_v', 'new_v_v_norm_g': 'new_v', 'new_v_v_norm_b': 'new_v', 'new_v_w_s': 'new_v', 'new_v_b_s': 'new_v', 'new_v_w_pa': 'new_v', 'new_v_w_pb': 'new_v', 'new_v_w_o': 'new_v', 'new_v_ln1_g': 'new_v', 'new_v_ln1_b': 'new_v', 'new_v_w_ff1': 'new_v', 'new_v_w_ff2': 'new_v', 'new_v_ln2_g': 'new_v', 'new_v_ln2_b': 'new_v'}


def _forward(args):
    return _fwd_reference(*[args[k] for k in FWD_PARAMS])


def _output_shape():
    def fwd():
        inp = _fwd_setup_inputs(0)
        return _fwd_reference(*[inp[k] for k in FWD_PARAMS])
    out = _jax.eval_shape(fwd)
    return out.shape, out.dtype

N_MICROBATCH = 1
ADAM_LR = 0.001
ADAM_B1 = 0.9
ADAM_B2 = 0.999
ADAM_EPS = 1e-08
ADAM_WD = 0.01
ADAM_STEP = 10
PER_EXAMPLE_BATCH_AXIS = {'x': 0, 'loss_target': 0}
SHARED_INPUTS = []
_WEIGHT_DTYPES = {'w_in': _jnp.float32, 'b_gate': _jnp.float32, 'conv_w': _jnp.float32, 'v_norm_g': _jnp.float32, 'v_norm_b': _jnp.float32, 'w_s': _jnp.float32, 'b_s': _jnp.float32, 'w_pa': _jnp.float32, 'w_pb': _jnp.float32, 'w_o': _jnp.float32, 'ln1_g': _jnp.float32, 'ln1_b': _jnp.float32, 'w_ff1': _jnp.float32, 'w_ff2': _jnp.float32, 'ln2_g': _jnp.float32, 'ln2_b': _jnp.float32}
MOMENT_SCALE = {'w_in': 6.757485e-02, 'b_gate': 3.986771e-02, 'conv_w': 7.816228e-02, 'v_norm_g': 6.015427e-02, 'v_norm_b': 6.087237e-02, 'w_s': 5.897094e-02, 'b_s': 6.440707e-02, 'w_pa': 9.069210e-02, 'w_pb': 1.123992e-01, 'w_o': 2.396200e-01, 'ln1_g': 4.209693e+00, 'ln1_b': 1.774789e+00, 'w_ff1': 7.349911e-02, 'w_ff2': 2.183442e-01, 'ln2_g': 1.279395e+02, 'ln2_b': 1.313346e+01}


def _to_microbatches(a, axis):
    t = _jnp.moveaxis(a, axis, 0)
    t = t.reshape((N_MICROBATCH, t.shape[0] // N_MICROBATCH) + t.shape[1:])
    return _jnp.moveaxis(t, 1, axis + 1)


def setup_inputs(seed: int = 0) -> dict:
    inp = _fwd_setup_inputs(seed)
    key = _jax.random.fold_in(_jax.random.key(seed), 7919)
    shape, _ = _output_shape()
    out = dict(inp)
    out["loss_target"] = _jax.random.normal(_jax.random.fold_in(key, 0), shape, _jnp.float32)
    for i, name in enumerate(TWIN_WEIGHTS):
        w = inp[name].astype(_jnp.float32)
        if MOMENT_SCALE is None:
            s = _jnp.sqrt(_jnp.mean(_jnp.square(w)) + 1e-30)
        else:
            s = MOMENT_SCALE[name]
        km, kv = _jax.random.split(_jax.random.fold_in(key, i + 1))
        out[name] = w
        out["m_" + name] = s * _jax.random.normal(km, w.shape, _jnp.float32)
        out["v_" + name] = (s * s) * _jax.random.uniform(kv, w.shape, _jnp.float32, 0.5, 1.5)
    if N_MICROBATCH > 1:
        for name, axis in PER_EXAMPLE_BATCH_AXIS.items():
            out[name] = _to_microbatches(out[name], axis)
    return {'x': out['x'], 'w_in': out['w_in'], 'b_gate': out['b_gate'], 'conv_w': out['conv_w'], 'v_norm_g': out['v_norm_g'], 'v_norm_b': out['v_norm_b'], 'w_s': out['w_s'], 'b_s': out['b_s'], 'w_pa': out['w_pa'], 'w_pb': out['w_pb'], 'w_o': out['w_o'], 'ln1_g': out['ln1_g'], 'ln1_b': out['ln1_b'], 'w_ff1': out['w_ff1'], 'w_ff2': out['w_ff2'], 'ln2_g': out['ln2_g'], 'ln2_b': out['ln2_b'], 'loss_target': out['loss_target'], 'm_w_in': out['m_w_in'], 'm_b_gate': out['m_b_gate'], 'm_conv_w': out['m_conv_w'], 'm_v_norm_g': out['m_v_norm_g'], 'm_v_norm_b': out['m_v_norm_b'], 'm_w_s': out['m_w_s'], 'm_b_s': out['m_b_s'], 'm_w_pa': out['m_w_pa'], 'm_w_pb': out['m_w_pb'], 'm_w_o': out['m_w_o'], 'm_ln1_g': out['m_ln1_g'], 'm_ln1_b': out['m_ln1_b'], 'm_w_ff1': out['m_w_ff1'], 'm_w_ff2': out['m_w_ff2'], 'm_ln2_g': out['m_ln2_g'], 'm_ln2_b': out['m_ln2_b'], 'v_w_in': out['v_w_in'], 'v_b_gate': out['v_b_gate'], 'v_conv_w': out['v_conv_w'], 'v_v_norm_g': out['v_v_norm_g'], 'v_v_norm_b': out['v_v_norm_b'], 'v_w_s': out['v_w_s'], 'v_b_s': out['v_b_s'], 'v_w_pa': out['v_w_pa'], 'v_w_pb': out['v_w_pb'], 'v_w_o': out['v_w_o'], 'v_ln1_g': out['v_ln1_g'], 'v_ln1_b': out['v_ln1_b'], 'v_w_ff1': out['v_w_ff1'], 'v_w_ff2': out['v_w_ff2'], 'v_ln2_g': out['v_ln2_g'], 'v_ln2_b': out['v_ln2_b']}


def _loss(weights, diff, rest, loss_target):
    with _jax.named_scope("forward"):
        args = {**rest, TWIN_DIFF_INPUT: diff, **{k: w.astype(_WEIGHT_DTYPES[k]) for k, w in weights.items()}}
        y = _forward(args)
    with _jax.named_scope("loss_head"):
        err = _jnp.square(y.astype(_jnp.float32) - loss_target)
        return 0.5 * _jnp.sum(_jnp.mean(err, axis=-1)) if err.ndim else 0.5 * err


def _adamw(w, g, m, v):
    m = ADAM_B1 * m + (1.0 - ADAM_B1) * g
    v = ADAM_B2 * v + (1.0 - ADAM_B2) * _jnp.square(g)
    m_hat = m / (1.0 - ADAM_B1 ** ADAM_STEP)
    v_hat = v / (1.0 - ADAM_B2 ** ADAM_STEP)
    delta = -ADAM_LR * (m_hat / (_jnp.sqrt(v_hat) + ADAM_EPS) + ADAM_WD * w)
    return delta, m, v


def reference(x, w_in, b_gate, conv_w, v_norm_g, v_norm_b, w_s, b_s, w_pa, w_pb, w_o, ln1_g, ln1_b, w_ff1, w_ff2, ln2_g, ln2_b, loss_target, m_w_in, m_b_gate, m_conv_w, m_v_norm_g, m_v_norm_b, m_w_s, m_b_s, m_w_pa, m_w_pb, m_w_o, m_ln1_g, m_ln1_b, m_w_ff1, m_w_ff2, m_ln2_g, m_ln2_b, v_w_in, v_b_gate, v_conv_w, v_v_norm_g, v_v_norm_b, v_w_s, v_b_s, v_w_pa, v_w_pb, v_w_o, v_ln1_g, v_ln1_b, v_w_ff1, v_w_ff2, v_ln2_g, v_ln2_b):
    given = dict(x=x, w_in=w_in, b_gate=b_gate, conv_w=conv_w, v_norm_g=v_norm_g, v_norm_b=v_norm_b, w_s=w_s, b_s=b_s, w_pa=w_pa, w_pb=w_pb, w_o=w_o, ln1_g=ln1_g, ln1_b=ln1_b, w_ff1=w_ff1, w_ff2=w_ff2, ln2_g=ln2_g, ln2_b=ln2_b, loss_target=loss_target, m_w_in=m_w_in, m_b_gate=m_b_gate, m_conv_w=m_conv_w, m_v_norm_g=m_v_norm_g, m_v_norm_b=m_v_norm_b, m_w_s=m_w_s, m_b_s=m_b_s, m_w_pa=m_w_pa, m_w_pb=m_w_pb, m_w_o=m_w_o, m_ln1_g=m_ln1_g, m_ln1_b=m_ln1_b, m_w_ff1=m_w_ff1, m_w_ff2=m_w_ff2, m_ln2_g=m_ln2_g, m_ln2_b=m_ln2_b, v_w_in=v_w_in, v_b_gate=v_b_gate, v_conv_w=v_conv_w, v_v_norm_g=v_v_norm_g, v_v_norm_b=v_v_norm_b, v_w_s=v_w_s, v_b_s=v_b_s, v_w_pa=v_w_pa, v_w_pb=v_w_pb, v_w_o=v_w_o, v_ln1_g=v_ln1_g, v_ln1_b=v_ln1_b, v_w_ff1=v_w_ff1, v_w_ff2=v_w_ff2, v_ln2_g=v_ln2_g, v_ln2_b=v_ln2_b)
    weights = {n: given[n] for n in TWIN_WEIGHTS}
    shared = {n: given[n] for n in SHARED_INPUTS}
    per_example = {n: given[n] for n in ['x']}
    grad_fn = _jax.value_and_grad(_loss, argnums=(0, 1))

    def one_microbatch(ex, loss_target):
        ex = dict(ex)
        diff = ex.pop(TWIN_DIFF_INPUT)
        return grad_fn(weights, diff, {**shared, **ex}, loss_target)

    if N_MICROBATCH == 1:
        loss, (grad_w, grad_x) = one_microbatch(per_example, given["loss_target"])
    else:
        def body(carry, xs):
            loss_sum, grad_sum = carry
            l_k, (gw_k, gx_k) = one_microbatch(xs[0], xs[1])
            with _jax.named_scope("update"):
                return (loss_sum + l_k, _jax.tree.map(_jnp.add, grad_sum, gw_k)), gx_k

        init = (_jnp.zeros((), _jnp.float32), _jax.tree.map(_jnp.zeros_like, weights))
        (loss, grad_w), grad_x = _jax.lax.scan(body, init, (per_example, given["loss_target"]))
    with _jax.named_scope("update"):
        delta_w, new_m, new_v = {}, {}, {}
        for n in TWIN_WEIGHTS:
            delta_w[n], new_m[n], new_v[n] = _adamw(weights[n], grad_w[n], given["m_" + n], given["v_" + n])
    return (loss, grad_x, *[grad_w[n] for n in TWIN_WEIGHTS], *[delta_w[n] for n in TWIN_WEIGHTS],
            *[new_m[n] for n in TWIN_WEIGHTS], *[new_v[n] for n in TWIN_WEIGHTS])
```

```python
import functools

import jax
import jax.numpy as jnp
from jax import lax
from jax.experimental import pallas as pl
from jax.experimental.pallas import tpu as pltpu

D = 1024
W_A = 1536
W_B = 1024
CHUNK = 128
N_HEAD = 8
D_FF = 4096
N_PROJ = 3 * W_A + 2 * W_B + 2 * D
OFF_CA, OFF_HA, OFF_UB, OFF_VB, OFF_GA, OFF_GB = 1536, 3072, 4608, 5632, 6656, 7680
LN_EPS = 1e-5
ALPHA = 2.0 ** 0.25
N_CHIP = 4
NP_SHARD = N_PROJ // N_CHIP
FF_SHARD = D_FF // N_CHIP
ADAM_LR, ADAM_B1, ADAM_B2, ADAM_EPS, ADAM_WD, ADAM_STEP = 0.001, 0.9, 0.999, 1e-08, 0.01, 10

PACK_ROWS = (("w_pa", W_A // N_CHIP), ("w_pb", W_B // N_CHIP), ("w_o", D // N_CHIP), ("w_ff1", D), ("w_ff2", FF_SHARD))
PACK_OFF = {}
_o = 0
for _n, _r in PACK_ROWS:
    PACK_OFF[_n] = (_o, _r)
    _o += _r
PACK_TOTAL = _o

V7X_VMEM_BYTES = 64 * 1024 * 1024
VMEM_LIMIT = 56 * 1024 * 1024
HALO = 8

BF = jnp.bfloat16
F32 = jnp.float32
MESH = pl.DeviceIdType.MESH


def _pc(body, **kw):
    return pl.pallas_call(body, **kw)


def _params(*sem):
    return pltpu.CompilerParams(dimension_semantics=sem, vmem_limit_bytes=VMEM_LIMIT)


def _resident(shape):
    n = len(shape)
    return pl.BlockSpec(shape, lambda *_: (0,) * n, pipeline_mode=pl.Buffered(1))


def _dot(a, b):
    return jnp.dot(a, b, preferred_element_type=F32)


def _dot_nt(a, b):
    return lax.dot_general(a, b, (((1,), (1,)), ((), ())), preferred_element_type=F32)


def _dot_tn(a, b):
    return lax.dot_general(a, b, (((0,), (0,)), ((), ())), preferred_element_type=F32)


def _gelu(x):
    t = jnp.tanh(0.7978845608028654 * (x + 0.044715 * (x * x * x)))
    return 0.5 * x * (1.0 + t), t


def _gelu_grad(x, t):
    return 0.5 * (1.0 + t) + 0.5 * x * (1.0 - t * t) * (0.7978845608028654 * (1.0 + 0.134145 * (x * x)))


def _ln_stats(r):
    mu = jnp.mean(r, axis=-1, keepdims=True)
    xc = r - mu
    var = jnp.mean(xc * xc, axis=-1, keepdims=True)
    rstd = lax.rsqrt(var + LN_EPS)
    return xc * rstd, rstd


def _ln_bwd(dy, g, xh, rstd):
    dxh = dy * g
    m1 = jnp.mean(dxh, axis=-1, keepdims=True)
    m2 = jnp.mean(dxh * xh, axis=-1, keepdims=True)
    return rstd * (dxh - m1 - xh * m2)


def _colsum(v):
    return jnp.sum(v, axis=0, keepdims=True)


def _proj_fwd(x, win4, tm):
    t = x.shape[0]

    def body(x_ref, w_ref, p_ref, xb_ref):
        @pl.when(pl.program_id(1) == 0)
        def _():
            xb_ref[...] = x_ref[...].astype(BF)

        p_ref[...] = _dot(xb_ref[...], w_ref[...])

    return _pc(
        body, name="proj_fwd", grid=(t // tm, N_CHIP),
        in_specs=[pl.BlockSpec((tm, D), lambda i, j: (i, 0)),
                  pl.BlockSpec((None, D, NP_SHARD), lambda i, j: (j, 0, 0))],
        out_specs=[pl.BlockSpec((tm, NP_SHARD), lambda i, j: (i, j)),
                   pl.BlockSpec((tm, D), lambda i, j: (i, 0))],
        out_shape=[jax.ShapeDtypeStruct((t, N_PROJ), F32), jax.ShapeDtypeStruct((t, D), BF)],
        compiler_params=_params("arbitrary", "arbitrary"),
    )(x, win4)


def _conv_fwd(p_ref, hal_ref, cw_ref, tm):
    ca = p_ref[:, OFF_CA:OFF_HA]
    ha = p_ref[:, OFF_HA:OFF_UB]
    ch = ca * ha
    ch_prev = hal_ref[HALO - 1:HALO, OFF_CA:OFF_HA] * hal_ref[HALO - 1:HALO, OFF_HA:OFF_UB]
    ch_next = hal_ref[HALO:HALO + 1, OFF_CA:OFF_HA] * hal_ref[HALO:HALO + 1, OFF_HA:OFF_UB]
    row = lax.broadcasted_iota(jnp.int32, (tm, W_A), 0)
    ch_m1 = jnp.where(row == 0, ch_prev, pltpu.roll(ch, 1, 0))
    ch_p1 = jnp.where(row == tm - 1, ch_next, pltpu.roll(ch, tm - 1, 0))
    cv = cw_ref[0:1, :] * ch_m1 + cw_ref[1:2, :] * ch + cw_ref[2:3, :] * ch_p1
    return ca, ha, ch, ch_m1, ch_p1, cv


def _spatial_fwd(p_ref, vg_ref, vb_ref, ws_ref, bsf_ref, vnb_ref, mixed_ref, tm):
    vb_pre = p_ref[:, OFF_VB:OFF_GA]
    gv, tv = _gelu(vb_pre)
    xhv, rstdv = _ln_stats(gv)
    vnb_ref[...] = (xhv * vg_ref[...] + vb_ref[...]).astype(BF)
    for c in range(tm // CHUNK):
        rows = slice(c * CHUNK, (c + 1) * CHUNK)
        for h in range(N_HEAD):
            cols = slice(h * CHUNK, (h + 1) * CHUNK)
            mixed_ref[rows, cols] = _dot(ws_ref[h], vnb_ref[rows, cols]) + bsf_ref[:, cols]
    return vb_pre, tv, xhv, rstdv


def _mix_fwd(p, hal, x, wpa, wpb, wo, wsb, bsf, bg, cw, vg, vb, tm):
    t = x.shape[0]
    nt = t // tm

    def body(p_ref, hal_ref, x_ref, wpa_ref, wpb_ref, wo_ref, ws_ref, bsf_ref, bg_ref, cw_ref, vg_ref, vb_ref,
             r1_ref, ya_ref, yb_ref, vnb_ref, mixed_ref):
        _, _, _, _, _, cv = _conv_fwd(p_ref, hal_ref, cw_ref, tm)
        a = p_ref[:, 0:OFF_CA] * cv
        ya = _dot(a.astype(BF), wpa_ref[...])
        ya_ref[...] = ya
        _spatial_fwd(p_ref, vg_ref, vb_ref, ws_ref, bsf_ref, vnb_ref, mixed_ref, tm)
        gu, _ = _gelu(p_ref[:, OFF_UB:OFF_VB])
        bb = gu * mixed_ref[...]
        yb = _dot(bb.astype(BF), wpb_ref[...])
        yb_ref[...] = yb
        ga = jax.nn.sigmoid(p_ref[:, OFF_GA:OFF_GB] + bg_ref[:, 0:D])
        gb = jax.nn.sigmoid(p_ref[:, OFF_GB:N_PROJ] + bg_ref[:, D:2 * D])
        z = ga * ya + gb * yb
        r1_ref[...] = ALPHA * x_ref[...] + _dot(z.astype(BF), wo_ref[...])

    tile = lambda w: pl.BlockSpec((tm, w), lambda i: (i, 0))
    return _pc(
        body, name="mix_fwd", grid=(nt,),
        in_specs=[tile(N_PROJ), pl.BlockSpec((None, 2 * HALO, N_PROJ), lambda i: (i, 0, 0)), tile(D),
                  _resident((W_A, D)), _resident((W_B, D)), _resident((D, D)), _resident((N_HEAD, CHUNK, CHUNK)),
                  _resident((CHUNK, W_B)), _resident((1, 2 * D)), _resident((3, W_A)), _resident((1, W_B)),
                  _resident((1, W_B))],
        out_specs=[tile(D), tile(D), tile(D)],
        out_shape=[jax.ShapeDtypeStruct((t, D), F32)] * 3,
        scratch_shapes=[pltpu.VMEM((tm, W_B), BF), pltpu.VMEM((tm, W_B), F32)],
        compiler_params=_params("arbitrary"),
    )(p, hal, x, wpa, wpb, wo, wsb, bsf, bg, cw, vg, vb)


def _ffn_fwd_bwd(r1, tgt, wff1, wff2, ln1g, ln1b, ln2g, ln2b, tm):
    t = r1.shape[0]

    def body(r1_ref, tgt_ref, w1_ref, w2_ref, g1_ref, b1_ref, g2_ref, b2_ref,
             dr1_ref, dr1b_ref, x1b_ref, hidb_ref, dh1b_ref, dr2b_ref, acc_ref, relu_ref):
        @pl.when(pl.program_id(0) == 0)
        def _():
            acc_ref[...] = jnp.zeros_like(acc_ref)

        xh1, rstd1 = _ln_stats(r1_ref[...])
        x1 = xh1 * g1_ref[...] + b1_ref[...]
        x1b_ref[...] = x1.astype(BF)
        ffn = jnp.zeros((tm, D), F32)
        for j in range(N_CHIP):
            cols = slice(j * FF_SHARD, (j + 1) * FF_SHARD)
            r = jnp.maximum(_dot(x1b_ref[...], w1_ref[j]), 0.0)
            relu_ref[:, cols] = r
            hidb_ref[:, cols] = (r * r).astype(BF)
            ffn = ffn + _dot(hidb_ref[:, cols], w2_ref[cols, :])
        xh2, rstd2 = _ln_stats(ALPHA * x1 + ffn)
        diff = xh2 * g2_ref[...] + b2_ref[...] - tgt_ref[...]
        acc_ref[4:5, :] += _colsum(diff * diff)
        dx2 = diff * (1.0 / D)
        acc_ref[2:3, :] += _colsum(dx2 * xh2)
        acc_ref[3:4, :] += _colsum(dx2)
        dr2 = _ln_bwd(dx2, g2_ref[...], xh2, rstd2)
        dr2b_ref[...] = dr2.astype(BF)
        dx1 = ALPHA * dr2
        for j in range(N_CHIP):
            cols = slice(j * FF_SHARD, (j + 1) * FF_SHARD)
            dhid = _dot_nt(dr2b_ref[...], w2_ref[cols, :])
            dh1b_ref[:, cols] = (dhid * (2.0 * relu_ref[:, cols])).astype(BF)
            dx1 = dx1 + _dot_nt(dh1b_ref[:, cols], w1_ref[j])
        acc_ref[0:1, :] += _colsum(dx1 * xh1)
        acc_ref[1:2, :] += _colsum(dx1)
        dr1 = _ln_bwd(dx1, g1_ref[...], xh1, rstd1)
        dr1_ref[...] = dr1
        dr1b_ref[...] = dr1.astype(BF)

    tile = lambda w: pl.BlockSpec((tm, w), lambda i: (i, 0))
    vec = _resident((1, D))
    return _pc(
        body, name="ffn_fwd_bwd", grid=(t // tm,),
        in_specs=[tile(D), tile(D), _resident((N_CHIP, D, FF_SHARD)), _resident((D_FF, D)), vec, vec, vec, vec],
        out_specs=[tile(D), tile(D), tile(D), tile(D_FF), tile(D_FF), tile(D), pl.BlockSpec((8, D), lambda i: (0, 0))],
        out_shape=[jax.ShapeDtypeStruct((t, D), F32), jax.ShapeDtypeStruct((t, D), BF), jax.ShapeDtypeStruct((t, D), BF),
                   jax.ShapeDtypeStruct((t, D_FF), BF), jax.ShapeDtypeStruct((t, D_FF), BF),
                   jax.ShapeDtypeStruct((t, D), BF), jax.ShapeDtypeStruct((8, D), F32)],
        scratch_shapes=[pltpu.VMEM((tm, D_FF), F32)],
        compiler_params=_params("arbitrary"),
    )(r1, tgt, wff1, wff2, ln1g, ln1b, ln2g, ln2b)


def _dw(a, b, nblk, am, bn, a_blocked, b_blocked, tk, name):
    t = a.shape[0]

    def body(a_ref, b_ref, o_ref):
        @pl.when(pl.program_id(1) == 0)
        def _():
            o_ref[...] = jnp.zeros_like(o_ref)

        o_ref[...] += _dot_tn(a_ref[...], b_ref[...])

    return _pc(
        body, name=name, grid=(nblk, t // tk),
        in_specs=[pl.BlockSpec((tk, am), (lambda j, k: (k, j)) if a_blocked else (lambda j, k: (k, 0))),
                  pl.BlockSpec((tk, bn), (lambda j, k: (k, j)) if b_blocked else (lambda j, k: (k, 0)))],
        out_specs=pl.BlockSpec((None, am, bn), lambda j, k: (j, 0, 0)),
        out_shape=jax.ShapeDtypeStruct((nblk, am, bn), F32),
        compiler_params=_params("arbitrary", "arbitrary"),
    )(a, b)


def _dx(dp, win4, dr1, tm):
    t = dp.shape[0]

    def body(dp_ref, w_ref, dr1_ref, dx_ref):
        j = pl.program_id(1)

        @pl.when(j == 0)
        def _():
            dx_ref[...] = ALPHA * dr1_ref[...]

        dx_ref[...] += _dot_nt(dp_ref[...], w_ref[...])

    return _pc(
        body, name="dx", grid=(t // tm, N_CHIP),
        in_specs=[pl.BlockSpec((tm, NP_SHARD), lambda i, j: (i, j)),
                  pl.BlockSpec((None, D, NP_SHARD), lambda i, j: (j, 0, 0)),
                  pl.BlockSpec((tm, D), lambda i, j: (i, 0))],
        out_specs=pl.BlockSpec((tm, D), lambda i, j: (i, 0)),
        out_shape=jax.ShapeDtypeStruct((t, D), F32),
        compiler_params=_params("arbitrary", "arbitrary"),
    )(dp, win4, dr1)


def _mix_bwd(p, hal, dr1, dr1h, ya, yb, wpa, wpb, wo, wsb, wstb, bsf, bg, cw, vg, vb, tm):
    t = p.shape[0]
    nt = t // tm
    te = tm + 2 * HALO
    mid = slice(HALO, HALO + tm)

    def body(p_ref, hal_ref, dr1_ref, dr1h_ref, ya_ref, yb_ref, wpa_ref, wpb_ref, wo_ref, ws_ref, wst_ref, bsf_ref,
             bg_ref, cw_ref, vg_ref, vb_ref,
             dp_ref, ab_ref, bbb_ref, zb_ref, dyab_ref, dybb_ref, dbg_ref, dcw_ref, dvgb_ref, dws_ref, dbs_ref,
             vnb_ref, mixed_ref, dmixb_ref, dvn_ref):
        @pl.when(pl.program_id(0) == 0)
        def _():
            for r in (dbg_ref, dcw_ref, dvgb_ref, dws_ref, dbs_ref):
                r[...] = jnp.zeros_like(r)

        ca, ha, ch, ch_m1, ch_p1, cv = _conv_fwd(p_ref, hal_ref, cw_ref, tm)
        ba = p_ref[:, 0:OFF_CA]
        ab_ref[...] = (ba * cv).astype(BF)
        vb_pre, tv, xhv, rstdv = _spatial_fwd(p_ref, vg_ref, vb_ref, ws_ref, bsf_ref, vnb_ref, mixed_ref, tm)
        ub = p_ref[:, OFF_UB:OFF_VB]
        gu, tu = _gelu(ub)
        bbb_ref[...] = (gu * mixed_ref[...]).astype(BF)
        bga = bg_ref[:, 0:D]
        ga = jax.nn.sigmoid(p_ref[:, OFF_GA:OFF_GB] + bga)
        gb = jax.nn.sigmoid(p_ref[:, OFF_GB:N_PROJ] + bg_ref[:, D:2 * D])
        ya = ya_ref[...]
        yb = yb_ref[...]
        zb_ref[...] = (ga * ya + gb * yb).astype(BF)

        dr1_ext = jnp.concatenate([dr1h_ref[0:HALO, :], dr1_ref[...], dr1h_ref[HALO:2 * HALO, :]], axis=0)
        dz_ext = _dot_nt(dr1_ext.astype(BF), wo_ref[...])
        ga_ext = jnp.concatenate([jax.nn.sigmoid(hal_ref[0:HALO, OFF_GA:OFF_GB] + bga), ga,
                                  jax.nn.sigmoid(hal_ref[HALO:2 * HALO, OFF_GA:OFF_GB] + bga)], axis=0)
        dya_ext = dz_ext * ga_ext
        dyab_ref[...] = dya_ext[mid].astype(BF)
        da_ext = _dot_nt(dya_ext.astype(BF), wpa_ref[...])
        ba_ext = jnp.concatenate([hal_ref[0:HALO, 0:OFF_CA], ba, hal_ref[HALO:2 * HALO, 0:OFF_CA]], axis=0)
        dcv_ext = da_ext * ba_ext
        dcv = dcv_ext[mid]
        dch = (cw_ref[0:1, :] * pltpu.roll(dcv_ext, te - 1, 0)[mid] + cw_ref[1:2, :] * dcv
               + cw_ref[2:3, :] * pltpu.roll(dcv_ext, 1, 0)[mid])
        dp_ref[:, 0:OFF_CA] = (da_ext[mid] * cv).astype(BF)
        dp_ref[:, OFF_CA:OFF_HA] = (dch * ha).astype(BF)
        dp_ref[:, OFF_HA:OFF_UB] = (dch * ca).astype(BF)
        dcw_ref[0:1, :] += _colsum(dcv * ch_m1)
        dcw_ref[1:2, :] += _colsum(dcv * ch)
        dcw_ref[2:3, :] += _colsum(dcv * ch_p1)

        dz = dz_ext[mid]
        dga = dz * ya * ga * (1.0 - ga)
        dgb = dz * yb * gb * (1.0 - gb)
        dp_ref[:, OFF_GA:OFF_GB] = dga.astype(BF)
        dp_ref[:, OFF_GB:N_PROJ] = dgb.astype(BF)
        dbg_ref[0:1, 0:D] += _colsum(dga)
        dbg_ref[0:1, D:2 * D] += _colsum(dgb)

        dybb_ref[...] = (dz * gb).astype(BF)
        dbb = _dot_nt(dybb_ref[...], wpb_ref[...])
        dp_ref[:, OFF_UB:OFF_VB] = (dbb * mixed_ref[...] * _gelu_grad(ub, tu)).astype(BF)
        dmixed = dbb * gu
        dmixb_ref[...] = dmixed.astype(BF)
        for c in range(tm // CHUNK):
            rows = slice(c * CHUNK, (c + 1) * CHUNK)
            dbs_ref[...] += dmixed[rows]
            for h in range(N_HEAD):
                cols = slice(h * CHUNK, (h + 1) * CHUNK)
                dws_ref[h] += _dot_nt(dmixb_ref[rows, cols], vnb_ref[rows, cols])
                dvn_ref[rows, cols] = _dot(wst_ref[h], dmixb_ref[rows, cols])
        dvn = dvn_ref[...]
        dvgb_ref[0:1, :] += _colsum(dvn * xhv)
        dvgb_ref[1:2, :] += _colsum(dvn)
        dgv = _ln_bwd(dvn, vg_ref[...], xhv, rstdv)
        dp_ref[:, OFF_VB:OFF_GA] = (dgv * _gelu_grad(vb_pre, tv)).astype(BF)

    tile = lambda w: pl.BlockSpec((tm, w), lambda i: (i, 0))
    halo = lambda w: pl.BlockSpec((None, 2 * HALO, w), lambda i: (i, 0, 0))
    acc = lambda *s: pl.BlockSpec(s, lambda i: (0,) * len(s))
    return _pc(
        body, name="mix_bwd", grid=(nt,),
        in_specs=[tile(N_PROJ), halo(N_PROJ), tile(D), halo(D), tile(D), tile(D),
                  _resident((W_A, D)), _resident((W_B, D)), _resident((D, D)), _resident((N_HEAD, CHUNK, CHUNK)),
                  _resident((N_HEAD, CHUNK, CHUNK)), _resident((CHUNK, W_B)), _resident((1, 2 * D)),
                  _resident((3, W_A)), _resident((1, W_B)), _resident((1, W_B))],
        out_specs=[tile(N_PROJ), tile(W_A), tile(W_B), tile(D), tile(D), tile(D),
                   acc(8, 2 * D), acc(8, W_A), acc(8, W_B), acc(N_HEAD, CHUNK, CHUNK), acc(CHUNK, W_B)],
        out_shape=[jax.ShapeDtypeStruct((t, N_PROJ), BF), jax.ShapeDtypeStruct((t, W_A), BF),
                   jax.ShapeDtypeStruct((t, W_B), BF), jax.ShapeDtypeStruct((t, D), BF), jax.ShapeDtypeStruct((t, D), BF),
                   jax.ShapeDtypeStruct((t, D), BF), jax.ShapeDtypeStruct((8, 2 * D), F32),
                   jax.ShapeDtypeStruct((8, W_A), F32), jax.ShapeDtypeStruct((8, W_B), F32),
                   jax.ShapeDtypeStruct((N_HEAD, CHUNK, CHUNK), F32), jax.ShapeDtypeStruct((CHUNK, W_B), F32)],
        scratch_shapes=[pltpu.VMEM((tm, W_B), BF), pltpu.VMEM((tm, W_B), F32), pltpu.VMEM((tm, W_B), BF),
                        pltpu.VMEM((tm, W_B), F32)],
        compiler_params=_params("arbitrary"),
    )(p, hal, dr1, dr1h, ya, yb, wpa, wpb, wo, wsb, wstb, bsf, bg, cw, vg, vb)


HBM_SPEC = pl.BlockSpec(memory_space=pltpu.HBM)
WIN_HALF = D // 2
PACK_HALF = PACK_TOTAL // 2
CONV_ROWS = 8


def _place():
    x, y, c = lax.axis_index("x"), lax.axis_index("y"), lax.axis_index("c")
    return x, y, c, 2 * x + y


def _flip(x, y, c, r):
    return (x ^ (r >> 1), y ^ (r & 1), c)


def _remote(src, dst, send_sems, recv_sems, k, peer):
    return pltpu.make_async_remote_copy(src_ref=src, dst_ref=dst, send_sem=send_sems.at[k], recv_sem=recv_sems.at[k],
                                        device_id=peer, device_id_type=MESH)


def _cast_shards(w_in, w_pa, w_pb, w_o, w_ff1, w_ff2):
    def body(win_ref, wpa_ref, wpb_ref, wo_ref, wff1_ref, wff2_ref, winb_ref, packb_ref):
        winb_ref[...] = win_ref[...].astype(BF)
        for name, ref in (("w_pa", wpa_ref), ("w_pb", wpb_ref), ("w_o", wo_ref), ("w_ff1", wff1_ref), ("w_ff2", wff2_ref)):
            off, rows = PACK_OFF[name]
            packb_ref[off:off + rows, :] = ref[...].astype(BF)

    return _pc(body, name="cast_shards",
               out_shape=[jax.ShapeDtypeStruct((D, NP_SHARD), BF), jax.ShapeDtypeStruct((PACK_TOTAL, D), BF)],
               compiler_params=pltpu.CompilerParams(vmem_limit_bytes=VMEM_LIMIT))(w_in, w_pa, w_pb, w_o, w_ff1, w_ff2)


def _gather_weights(winb, packb, conv8):
    halves = (WIN_HALF, PACK_HALF)

    def body(win_ref, pack_ref, conv_ref, win4_ref, pack4_ref, conv4_ref, send_sems, recv_sems, local_sems):
        x, y, c, j = _place()
        sibling = (x, y, 1 - c)
        srcs, outs = (win_ref, pack_ref), (win4_ref, pack4_ref)
        local = [pltpu.make_async_copy(srcs[a], outs[a].at[j], local_sems.at[a]) for a in range(2)]
        local.append(pltpu.make_async_copy(conv_ref, conv4_ref.at[j], local_sems.at[2]))
        for cp in local:
            cp.start()
        sends = []
        for a in range(2):
            mine = pl.ds(c * halves[a], halves[a])
            for r in (1, 2, 3):
                sends.append(_remote(srcs[a].at[mine], outs[a].at[j, mine], send_sems, recv_sems, 3 * a + r - 1,
                                     _flip(x, y, c, r)))
        for r in (1, 2, 3):
            sends.append(_remote(conv_ref, conv4_ref.at[j], send_sems, recv_sems, 5 + r, _flip(x, y, c, r)))
        for cp in sends:
            cp.start()
        passed = []
        for a in range(2):
            mine = pl.ds(c * halves[a], halves[a])
            for r in (1, 2, 3):
                landed = outs[a].at[j ^ r, mine]
                _remote(landed, landed, send_sems, recv_sems, 3 * a + r - 1, sibling).wait_recv()
                fwd = _remote(landed, landed, send_sems, recv_sems, 9 + 3 * a + r - 1, sibling)
                fwd.start()
                passed.append(fwd)
        for r in (1, 2, 3):
            _remote(conv_ref, conv4_ref.at[j ^ r], send_sems, recv_sems, 5 + r, sibling).wait_recv()
        for a in range(2):
            theirs = pl.ds((1 - c) * halves[a], halves[a])
            for r in (1, 2, 3):
                landed = outs[a].at[j ^ r, theirs]
                _remote(landed, landed, send_sems, recv_sems, 9 + 3 * a + r - 1, sibling).wait_recv()
        for cp in sends + passed:
            cp.wait_send()
        for cp in local:
            cp.wait()

    return _pc(
        body, name="gather_weights",
        in_specs=[HBM_SPEC] * 3, out_specs=[HBM_SPEC] * 3,
        out_shape=[jax.ShapeDtypeStruct((N_CHIP, D, NP_SHARD), BF), jax.ShapeDtypeStruct((N_CHIP, PACK_TOTAL, D), BF),
                   jax.ShapeDtypeStruct((N_CHIP, CONV_ROWS, W_A // N_CHIP), F32)],
        scratch_shapes=[pltpu.SemaphoreType.DMA((15,)), pltpu.SemaphoreType.DMA((15,)), pltpu.SemaphoreType.DMA((3,))],
    )(winb, packb, conv8)


def _exchange_cores(gin4, gpack4, small):
    def body(gin_ref, gpack_ref, small_ref, rin_ref, rpack_ref, rsmall_ref, send_sems, recv_sems):
        x, y, c, _ = _place()
        sibling = (x, y, 1 - c)
        cps = [_remote(gin_ref.at[:, pl.ds((1 - c) * WIN_HALF, WIN_HALF), :], rin_ref, send_sems, recv_sems, 0, sibling),
               _remote(gpack_ref.at[:, pl.ds((1 - c) * PACK_HALF, PACK_HALF), :], rpack_ref, send_sems, recv_sems, 1, sibling),
               _remote(small_ref, rsmall_ref, send_sems, recv_sems, 2, sibling)]
        for cp in cps:
            cp.start()
        for cp in cps:
            cp.wait()

    return _pc(
        body, name="exchange_cores", in_specs=[HBM_SPEC] * 3, out_specs=[HBM_SPEC] * 3,
        out_shape=[jax.ShapeDtypeStruct((N_CHIP, WIN_HALF, NP_SHARD), F32),
                   jax.ShapeDtypeStruct((N_CHIP, PACK_HALF, D), F32), jax.ShapeDtypeStruct(small.shape, F32)],
        scratch_shapes=[pltpu.SemaphoreType.DMA((3,)), pltpu.SemaphoreType.DMA((3,))],
    )(gin4, gpack4, small)


def _exchange_chips(sin4, spack4, csmall):
    def body(sin_ref, spack_ref, cs_ref, rin_ref, rpack_ref, rsm_ref, send_sems, recv_sems, local_sem):
        x, y, c, j = _place()
        own = pltpu.make_async_copy(cs_ref, rsm_ref.at[0], local_sem)
        own.start()
        cps = []
        for r in (1, 2, 3):
            peer = _flip(x, y, c, r)
            cps.append(_remote(sin_ref.at[j ^ r], rin_ref.at[r - 1], send_sems, recv_sems, 3 * (r - 1), peer))
            cps.append(_remote(spack_ref.at[j ^ r], rpack_ref.at[r - 1], send_sems, recv_sems, 3 * (r - 1) + 1, peer))
            cps.append(_remote(cs_ref, rsm_ref.at[r], send_sems, recv_sems, 3 * (r - 1) + 2, peer))
        for cp in cps:
            cp.start()
        for cp in cps:
            cp.wait()
        own.wait()

    return _pc(
        body, name="exchange_chips", in_specs=[HBM_SPEC] * 3, out_specs=[HBM_SPEC] * 3,
        out_shape=[jax.ShapeDtypeStruct((3, WIN_HALF, NP_SHARD), F32), jax.ShapeDtypeStruct((3, PACK_HALF, D), F32),
                   jax.ShapeDtypeStruct((N_CHIP,) + csmall.shape, F32)],
        scratch_shapes=[pltpu.SemaphoreType.DMA((9,)), pltpu.SemaphoreType.DMA((9,)), pltpu.SemaphoreType.DMA],
    )(sin4, spack4, csmall)


def _join_cores(fin, fpack):
    def body(fin_ref, fpack_ref, gin_ref, gpack_ref, send_sems, recv_sems, local_sems):
        x, y, c, _ = _place()
        sibling = (x, y, 1 - c)
        srcs, outs, halves = (fin_ref, fpack_ref), (gin_ref, gpack_ref), (WIN_HALF, PACK_HALF)
        cps = []
        for a in range(2):
            mine = outs[a].at[pl.ds(c * halves[a], halves[a]), :]
            cps.append(pltpu.make_async_copy(srcs[a], mine, local_sems.at[a]))
            cps.append(_remote(srcs[a], mine, send_sems, recv_sems, a, sibling))
        for cp in cps:
            cp.start()
        for cp in cps:
            cp.wait()

    return _pc(
        body, name="join_cores", in_specs=[HBM_SPEC] * 2, out_specs=[HBM_SPEC] * 2,
        out_shape=[jax.ShapeDtypeStruct((D, NP_SHARD), F32), jax.ShapeDtypeStruct((PACK_TOTAL, D), F32)],
        scratch_shapes=[pltpu.SemaphoreType.DMA((2,)), pltpu.SemaphoreType.DMA((2,)), pltpu.SemaphoreType.DMA((2,))],
    )(fin, fpack)


def _add_own_half(full4, recv4, core, rb, name):
    n, rh, cols = recv4.shape
    nb = rh // rb

    def body(c_ref, a_ref, b_ref, o_ref):
        o_ref[...] = a_ref[...] + b_ref[...]

    blk = (None, rb, cols)
    return _pc(
        body, name=name,
        grid_spec=pltpu.PrefetchScalarGridSpec(
            num_scalar_prefetch=1, grid=(n, nb),
            in_specs=[pl.BlockSpec(blk, lambda k, i, c: (k, c[0] * nb + i, 0)), pl.BlockSpec(blk, lambda k, i, c: (k, i, 0))],
            out_specs=pl.BlockSpec(blk, lambda k, i, c: (k, i, 0))),
        out_shape=jax.ShapeDtypeStruct(recv4.shape, F32),
        compiler_params=_params("arbitrary", "arbitrary"),
    )(core, full4, recv4)


def _add_chips(s4, r3, chip, rb, name):
    _, rh, cols = r3.shape

    def body(j_ref, s_ref, r_ref, o_ref):
        o_ref[...] = ((s_ref[...] + r_ref[0]) + r_ref[1]) + r_ref[2]

    return _pc(
        body, name=name,
        grid_spec=pltpu.PrefetchScalarGridSpec(
            num_scalar_prefetch=1, grid=(rh // rb,),
            in_specs=[pl.BlockSpec((None, rb, cols), lambda i, j: (j[0], i, 0)), pl.BlockSpec((3, rb, cols), lambda i, j: (0, i, 0))],
            out_specs=pl.BlockSpec((rb, cols), lambda i, j: (i, 0))),
        out_shape=jax.ShapeDtypeStruct((rh, cols), F32),
        compiler_params=_params("arbitrary"),
    )(chip, s4, r3)


def _add_small(a, b):
    def body(a_ref, b_ref, o_ref):
        o_ref[...] = a_ref[...] + b_ref[...]

    return _pc(body, name="add_small_cores", out_shape=jax.ShapeDtypeStruct(a.shape, F32))(a, b)


def _sum_small_chips(slots, chip):
    def body(j_ref, s_ref, o_ref):
        j = j_ref[0]
        o_ref[...] = ((s_ref[j] + s_ref[j ^ 1]) + s_ref[j ^ 2]) + s_ref[j ^ 3]

    return _pc(
        body, name="sum_small_chips",
        in_specs=[pl.BlockSpec(memory_space=pltpu.SMEM), pl.BlockSpec(memory_space=pltpu.VMEM)],
        out_specs=pl.BlockSpec(memory_space=pltpu.VMEM),
        out_shape=jax.ShapeDtypeStruct(slots.shape[1:], F32))(chip, slots)


def _adamw(w, g, m, v, rb, name):
    rows, cols = w.shape

    def body(w_ref, g_ref, m_ref, v_ref, d_ref, m2_ref, v2_ref):
        g_ = g_ref[...]
        m2 = ADAM_B1 * m_ref[...] + (1.0 - ADAM_B1) * g_
        v2 = ADAM_B2 * v_ref[...] + (1.0 - ADAM_B2) * (g_ * g_)
        m_hat = m2 / (1.0 - ADAM_B1 ** ADAM_STEP)
        v_hat = v2 / (1.0 - ADAM_B2 ** ADAM_STEP)
        d_ref[...] = -ADAM_LR * (m_hat / (jnp.sqrt(v_hat) + ADAM_EPS) + ADAM_WD * w_ref[...])
        m2_ref[...] = m2
        v2_ref[...] = v2

    blk = pl.BlockSpec((rb, cols), lambda i: (i, 0))
    return _pc(body, name=name, grid=(rows // rb,), in_specs=[blk] * 4, out_specs=[blk] * 3,
               out_shape=[jax.ShapeDtypeStruct((rows, cols), F32)] * 3, compiler_params=_params("arbitrary"))(w, g, m, v)


LANES = 128
SMALL_GRADS = (("b_gate", 2 * D), ("conv_w", 3 * W_A), ("v_norm_g", W_B), ("v_norm_b", W_B),
               ("w_s", N_HEAD * CHUNK * CHUNK), ("b_s", N_HEAD * CHUNK), ("ln1_g", D), ("ln1_b", D), ("ln2_g", D), ("ln2_b", D))


def _pack_rows(parts):
    rows = []
    for a in parts:
        a = a.reshape(-1)
        a = jnp.pad(a, (0, (-a.shape[0]) % LANES))
        rows.append(a.reshape(-1, LANES))
    out = jnp.concatenate(rows, axis=0)
    return jnp.pad(out, ((0, (-out.shape[0]) % 8), (0, 0)))


def _unpack_rows(buf, sizes):
    out, r = [], 0
    for n in sizes:
        nr = -(-n // LANES)
        out.append(buf[r:r + nr].reshape(-1)[:n])
        r += nr
    return out


def _halo(a, tm):
    t, w = a.shape
    a3 = a.reshape(t // tm, tm, w)
    zero = jnp.zeros((1, HALO, w), a.dtype)
    prev = jnp.concatenate([zero, a3[:-1, tm - HALO:, :]], axis=0)
    nxt = jnp.concatenate([a3[1:, :HALO, :], zero], axis=0)
    return jnp.concatenate([prev, nxt], axis=1)


TM_PROJ = 512
TM_MIX = 256
TM_FFN = 256
TM_DX = 1024
TK_DW = 2048
TK_DW_IN = 1024
RB_ADD = 256
RB_ADD_PACK = PACK_HALF // 2
RB_ADAM = 128


def kernel(x, w_in, b_gate, conv_w, v_norm_g, v_norm_b, w_s, b_s, w_pa, w_pb, w_o, ln1_g, ln1_b, w_ff1, w_ff2, ln2_g, ln2_b, loss_target, m_w_in, m_b_gate, m_conv_w, m_v_norm_g, m_v_norm_b, m_w_s, m_b_s, m_w_pa, m_w_pb, m_w_o, m_ln1_g, m_ln1_b, m_w_ff1, m_w_ff2, m_ln2_g, m_ln2_b, v_w_in, v_b_gate, v_conv_w, v_v_norm_g, v_v_norm_b, v_w_s, v_b_s, v_w_pa, v_w_pb, v_w_o, v_ln1_g, v_ln1_b, v_w_ff1, v_w_ff2, v_ln2_g, v_ln2_b):
    t = x.shape[1]
    core = lax.axis_index("c").astype(jnp.int32).reshape(1)
    chip_idx = 2 * lax.axis_index("x") + lax.axis_index("y")
    chip = chip_idx.astype(jnp.int32).reshape(1)
    x2 = x.reshape(t, D)
    tgt = loss_target.reshape(t, D)

    winb, packb = _cast_shards(w_in[0], w_pa[0], w_pb[0], w_o[0], w_ff1[0], w_ff2[0])
    conv8 = jnp.pad(conv_w[0], ((0, CONV_ROWS - 3), (0, 0)))
    win4, pack4, conv4 = _gather_weights(winb, packb, conv8)

    def full(name, rows_total):
        off, rows = PACK_OFF[name]
        return pack4[:, off:off + rows, :].reshape(rows_total, D)

    wpa, wpb, wo, wff2 = full("w_pa", W_A), full("w_pb", W_B), full("w_o", D), full("w_ff2", D_FF)
    off1, rows1 = PACK_OFF["w_ff1"]
    wff1_4 = pack4[:, off1:off1 + rows1, :]
    cw = jnp.transpose(conv4[:, :3, :], (1, 0, 2)).reshape(3, W_A)
    wsb = w_s[0].astype(BF)
    wstb = jnp.swapaxes(w_s[0], 1, 2).astype(BF)
    bsf = jnp.repeat(jnp.transpose(b_s[0]), CHUNK, axis=1)

    p, xb = _proj_fwd(x2, win4, TM_PROJ)
    hal = _halo(p, TM_MIX)
    r1, ya, yb = _mix_fwd(p, hal, x2, wpa, wpb, wo, wsb, bsf, b_gate, cw, v_norm_g, v_norm_b, TM_MIX)
    dr1, dr1b, x1b, hidb, dh1b, dr2b, acc = _ffn_fwd_bwd(r1, tgt, wff1_4, wff2, ln1_g, ln1_b, ln2_g, ln2_b, TM_FFN)
    dwff1_4 = _dw(x1b, dh1b, N_CHIP, D, FF_SHARD, False, True, TK_DW, "dw_ff1")
    dwff2_4 = _dw(hidb, dr2b, N_CHIP, FF_SHARD, D, True, False, TK_DW, "dw_ff2")
    dr1h = _halo(dr1, TM_MIX)
    dp, ab, bbb, zb, dyab, dybb, dbg, dcw, dvgb, dws, dbs_sum = _mix_bwd(
        p, hal, dr1, dr1h, ya, yb, wpa, wpb, wo, wsb, wstb, bsf, b_gate, cw, v_norm_g, v_norm_b, TM_MIX)
    dx = _dx(dp, win4, dr1, TM_DX)
    dwin4 = _dw(xb, dp, N_CHIP, D, NP_SHARD, False, True, TK_DW_IN, "dw_in")
    dwpa = _dw(ab, dyab, 1, W_A, D, False, False, TK_DW, "dw_pa")
    dwpb = _dw(bbb, dybb, 1, W_B, D, False, False, TK_DW, "dw_pb")
    dwo = _dw(zb, dr1b, 1, D, D, False, False, TK_DW, "dw_o")
    gpack4 = jnp.concatenate([dwpa.reshape(N_CHIP, -1, D), dwpb.reshape(N_CHIP, -1, D), dwo.reshape(N_CHIP, -1, D),
                              dwff1_4, dwff2_4], axis=1)
    dbs = jnp.transpose(jnp.sum(dbs_sum.reshape(CHUNK, N_HEAD, CHUNK), axis=-1))
    small = _pack_rows([dbg[0], dcw[0:3], dvgb[0], dvgb[1], dws, dbs, acc[0], acc[1], acc[2], acc[3]])
    loss = lax.psum(0.5 * jnp.sum(acc[4]) / D, ("x", "y", "c"))

    rin, rpack, rsmall = _exchange_cores(dwin4, gpack4, small)
    sin4 = _add_own_half(dwin4, rin, core, RB_ADD, "add_cores_in")
    spack4 = _add_own_half(gpack4, rpack, core, RB_ADD_PACK, "add_cores_pack")
    csmall = _add_small(small, rsmall)
    cin, cpack, slots = _exchange_chips(sin4, spack4, csmall)
    fin = _add_chips(sin4, cin, chip, RB_ADD, "add_chips_in")
    fpack = _add_chips(spack4, cpack, chip, RB_ADD_PACK, "add_chips_pack")
    gsmall = _sum_small_chips(slots, chip)
    g_in, g_pack = _join_cores(fin, fpack)

    grads = {"w_in": g_in}
    for name, _ in PACK_ROWS:
        off, rows = PACK_OFF[name]
        grads[name] = g_pack[off:off + rows, :]
    for (name, n), flat in zip(SMALL_GRADS, _unpack_rows(gsmall, [n for _, n in SMALL_GRADS])):
        grads[name] = flat
    grads["conv_w"] = lax.dynamic_slice(grads["conv_w"].reshape(3, W_A), (0, chip_idx * (W_A // N_CHIP)), (3, W_A // N_CHIP))

    weights = dict(w_in=w_in, b_gate=b_gate, conv_w=conv_w, v_norm_g=v_norm_g, v_norm_b=v_norm_b, w_s=w_s, b_s=b_s,
                   w_pa=w_pa, w_pb=w_pb, w_o=w_o, ln1_g=ln1_g, ln1_b=ln1_b, w_ff1=w_ff1, w_ff2=w_ff2, ln2_g=ln2_g, ln2_b=ln2_b)
    mom1 = dict(w_in=m_w_in, b_gate=m_b_gate, conv_w=m_conv_w, v_norm_g=m_v_norm_g, v_norm_b=m_v_norm_b, w_s=m_w_s,
                b_s=m_b_s, w_pa=m_w_pa, w_pb=m_w_pb, w_o=m_w_o, ln1_g=m_ln1_g, ln1_b=m_ln1_b, w_ff1=m_w_ff1,
                w_ff2=m_w_ff2, ln2_g=m_ln2_g, ln2_b=m_ln2_b)
    mom2 = dict(w_in=v_w_in, b_gate=v_b_gate, conv_w=v_conv_w, v_norm_g=v_v_norm_g, v_norm_b=v_v_norm_b, w_s=v_w_s,
                b_s=v_b_s, w_pa=v_w_pa, w_pb=v_w_pb, w_o=v_w_o, ln1_g=v_ln1_g, ln1_b=v_ln1_b, w_ff1=v_w_ff1,
                w_ff2=v_w_ff2, ln2_g=v_ln2_g, ln2_b=v_ln2_b)
    order = list(weights)
    big = ("w_in", "w_pa", "w_pb", "w_o", "w_ff1", "w_ff2")
    delta, new_m, new_v = {}, {}, {}
    for name in big:
        w2 = weights[name][0]
        delta[name], new_m[name], new_v[name] = _adamw(w2, grads[name], mom1[name][0], mom2[name][0], RB_ADAM, "adamw_" + name)
    little = [n for n in order if n not in big]
    sizes = [weights[n].size for n in little]
    wsmall = _pack_rows([weights[n] for n in little])
    ds, ms, vs = _adamw(wsmall, _pack_rows([grads[n] for n in little]), _pack_rows([mom1[n] for n in little]),
                        _pack_rows([mom2[n] for n in little]), wsmall.shape[0], "adamw_small")
    for name, d_, m_, v_ in zip(little, _unpack_rows(ds, sizes), _unpack_rows(ms, sizes), _unpack_rows(vs, sizes)):
        delta[name], new_m[name], new_v[name] = d_, m_, v_

    shaped = lambda d: [d[n].reshape(weights[n].shape) for n in order]
    return (loss, dx.reshape(x.shape), *shaped(grads), *shaped(delta), *shaped(new_m), *shaped(new_v))
```

```python
import functools

import jax
import jax.numpy as jnp
from jax import lax
from jax.experimental import pallas as pl
from jax.experimental.pallas import tpu as pltpu

D = 1024
W_A = 1536
W_B = 1024
CHUNK = 128
N_HEAD = 8
D_FF = 4096
N_PROJ = 3 * W_A + 2 * W_B + 2 * D
OFF_CA, OFF_HA, OFF_UB, OFF_VB, OFF_GA, OFF_GB = 1536, 3072, 4608, 5632, 6656, 7680
LN_EPS = 1e-5
ALPHA = 2.0 ** 0.25
N_CHIP = 4
NP_SHARD = N_PROJ // N_CHIP
FF_SHARD = D_FF // N_CHIP
ADAM_LR, ADAM_B1, ADAM_B2, ADAM_EPS, ADAM_WD, ADAM_STEP = 0.001, 0.9, 0.999, 1e-08, 0.01, 10

PACK_ROWS = (("w_pa", W_A // N_CHIP), ("w_pb", W_B // N_CHIP), ("w_o", D // N_CHIP), ("w_ff1", D), ("w_ff2", FF_SHARD))
PACK_OFF = {}
_o = 0
for _n, _r in PACK_ROWS:
    PACK_OFF[_n] = (_o, _r)
    _o += _r
PACK_TOTAL = _o

V7X_VMEM_BYTES = 64 * 1024 * 1024
VMEM_LIMIT = 56 * 1024 * 1024
HALO = 8

BF = jnp.bfloat16
F32 = jnp.float32
MESH = pl.DeviceIdType.MESH


def _pc(body, **kw):
    return pl.pallas_call(body, **kw)


def _params(*sem):
    return pltpu.CompilerParams(dimension_semantics=sem, vmem_limit_bytes=VMEM_LIMIT)


def _resident(shape):
    n = len(shape)
    return pl.BlockSpec(shape, lambda *_: (0,) * n, pipeline_mode=pl.Buffered(1))


def _dot(a, b):
    return jnp.dot(a, b, preferred_element_type=F32)


def _dot_nt(a, b):
    return lax.dot_general(a, b, (((1,), (1,)), ((), ())), preferred_element_type=F32)


def _dot_tn(a, b):
    return lax.dot_general(a, b, (((0,), (0,)), ((), ())), preferred_element_type=F32)


def _gelu(x):
    t = jnp.tanh(0.7978845608028654 * (x + 0.044715 * (x * x * x)))
    return 0.5 * x * (1.0 + t), t


def _gelu_grad(x, t):
    return 0.5 * (1.0 + t) + 0.5 * x * (1.0 - t * t) * (0.7978845608028654 * (1.0 + 0.134145 * (x * x)))


def _ln_stats(r):
    mu = jnp.mean(r, axis=-1, keepdims=True)
    xc = r - mu
    var = jnp.mean(xc * xc, axis=-1, keepdims=True)
    rstd = lax.rsqrt(var + LN_EPS)
    return xc * rstd, rstd


def _ln_bwd(dy, g, xh, rstd):
    dxh = dy * g
    m1 = jnp.mean(dxh, axis=-1, keepdims=True)
    m2 = jnp.mean(dxh * xh, axis=-1, keepdims=True)
    return rstd * (dxh - m1 - xh * m2)


def _colsum(v):
    return jnp.sum(v, axis=0, keepdims=True)


def _proj_fwd(x, win4, tm):
    t = x.shape[0]

    def body(x_ref, w_ref, p_ref, xb_ref):
        @pl.when(pl.program_id(1) == 0)
        def _():
            xb_ref[...] = x_ref[...].astype(BF)

        p_ref[...] = _dot(xb_ref[...], w_ref[...])

    return _pc(
        body, name="proj_fwd", grid=(t // tm, N_CHIP),
        in_specs=[pl.BlockSpec((tm, D), lambda i, j: (i, 0)),
                  pl.BlockSpec((None, D, NP_SHARD), lambda i, j: (j, 0, 0))],
        out_specs=[pl.BlockSpec((tm, NP_SHARD), lambda i, j: (i, j)),
                   pl.BlockSpec((tm, D), lambda i, j: (i, 0))],
        out_shape=[jax.ShapeDtypeStruct((t, N_PROJ), F32), jax.ShapeDtypeStruct((t, D), BF)],
        compiler_params=_params("arbitrary", "arbitrary"),
    )(x, win4)


def _conv_fwd(p_ref, hal_ref, cw_ref, tm):
    ca = p_ref[:, OFF_CA:OFF_HA]
    ha = p_ref[:, OFF_HA:OFF_UB]
    ch = ca * ha
    ch_prev = hal_ref[HALO - 1:HALO, OFF_CA:OFF_HA] * hal_ref[HALO - 1:HALO, OFF_HA:OFF_UB]
    ch_next = hal_ref[HALO:HALO + 1, OFF_CA:OFF_HA] * hal_ref[HALO:HALO + 1, OFF_HA:OFF_UB]
    row = lax.broadcasted_iota(jnp.int32, (tm, W_A), 0)
    ch_m1 = jnp.where(row == 0, ch_prev, pltpu.roll(ch, 1, 0))
    ch_p1 = jnp.where(row == tm - 1, ch_next, pltpu.roll(ch, tm - 1, 0))
    cv = cw_ref[0:1, :] * ch_m1 + cw_ref[1:2, :] * ch + cw_ref[2:3, :] * ch_p1
    return ca, ha, ch, ch_m1, ch_p1, cv


def _spatial_fwd(p_ref, vg_ref, vb_ref, ws_ref, bsf_ref, vnb_ref, mixed_ref, tm):
    vb_pre = p_ref[:, OFF_VB:OFF_GA]
    gv, tv = _gelu(vb_pre)
    xhv, rstdv = _ln_stats(gv)
    vnb_ref[...] = (xhv * vg_ref[...] + vb_ref[...]).astype(BF)
    for c in range(tm // CHUNK):
        rows = slice(c * CHUNK, (c + 1) * CHUNK)
        for h in range(N_HEAD):
            cols = slice(h * CHUNK, (h + 1) * CHUNK)
            mixed_ref[rows, cols] = _dot(ws_ref[h], vnb_ref[rows, cols]) + bsf_ref[:, cols]
    return vb_pre, tv, xhv, rstdv


def _mix_fwd(p, hal, x, wpa, wpb, wo, wsb, bsf, bg, cw, vg, vb, tm):
    t = x.shape[0]
    nt = t // tm

    def body(p_ref, hal_ref, x_ref, wpa_ref, wpb_ref, wo_ref, ws_ref, bsf_ref, bg_ref, cw_ref, vg_ref, vb_ref,
             r1_ref, ya_ref, yb_ref, vnb_ref, mixed_ref):
        _, _, _, _, _, cv = _conv_fwd(p_ref, hal_ref, cw_ref, tm)
        a = p_ref[:, 0:OFF_CA] * cv
        ya = _dot(a.astype(BF), wpa_ref[...])
        ya_ref[...] = ya
        _spatial_fwd(p_ref, vg_ref, vb_ref, ws_ref, bsf_ref, vnb_ref, mixed_ref, tm)
        gu, _ = _gelu(p_ref[:, OFF_UB:OFF_VB])
        bb = gu * mixed_ref[...]
        yb = _dot(bb.astype(BF), wpb_ref[...])
        yb_ref[...] = yb
        ga = jax.nn.sigmoid(p_ref[:, OFF_GA:OFF_GB] + bg_ref[:, 0:D])
        gb = jax.nn.sigmoid(p_ref[:, OFF_GB:N_PROJ] + bg_ref[:, D:2 * D])
        z = ga * ya + gb * yb
        r1_ref[...] = ALPHA * x_ref[...] + _dot(z.astype(BF), wo_ref[...])

    tile = lambda w: pl.BlockSpec((tm, w), lambda i: (i, 0))
    return _pc(
        body, name="mix_fwd", grid=(nt,),
        in_specs=[tile(N_PROJ), pl.BlockSpec((None, 2 * HALO, N_PROJ), lambda i: (i, 0, 0)), tile(D),
                  _resident((W_A, D)), _resident((W_B, D)), _resident((D, D)), _resident((N_HEAD, CHUNK, CHUNK)),
                  _resident((CHUNK, W_B)), _resident((1, 2 * D)), _resident((3, W_A)), _resident((1, W_B)),
                  _resident((1, W_B))],
        out_specs=[tile(D), tile(D), tile(D)],
        out_shape=[jax.ShapeDtypeStruct((t, D), F32)] * 3,
        scratch_shapes=[pltpu.VMEM((tm, W_B), BF), pltpu.VMEM((tm, W_B), F32)],
        compiler_params=_params("arbitrary"),
    )(p, hal, x, wpa, wpb, wo, wsb, bsf, bg, cw, vg, vb)


def _ffn_fwd_bwd(r1, tgt, wff1, wff2, ln1g, ln1b, ln2g, ln2b, tm):
    t = r1.shape[0]

    def body(r1_ref, tgt_ref, w1_ref, w2_ref, g1_ref, b1_ref, g2_ref, b2_ref,
             dr1_ref, dr1b_ref, x1b_ref, hidb_ref, dh1b_ref, dr2b_ref, acc_ref, relu_ref):
        @pl.when(pl.program_id(0) == 0)
        def _():
            acc_ref[...] = jnp.zeros_like(acc_ref)

        xh1, rstd1 = _ln_stats(r1_ref[...])
        x1 = xh1 * g1_ref[...] + b1_ref[...]
        x1b_ref[...] = x1.astype(BF)
        ffn = jnp.zeros((tm, D), F32)
        for j in range(N_CHIP):
            cols = slice(j * FF_SHARD, (j + 1) * FF_SHARD)
            r = jnp.maximum(_dot(x1b_ref[...], w1_ref[j]), 0.0)
            relu_ref[:, cols] = r
            hidb_ref[:, cols] = (r * r).astype(BF)
            ffn = ffn + _dot(hidb_ref[:, cols], w2_ref[cols, :])
        xh2, rstd2 = _ln_stats(ALPHA * x1 + ffn)
        diff = xh2 * g2_ref[...] + b2_ref[...] - tgt_ref[...]
        acc_ref[4:5, :] += _colsum(diff * diff)
        dx2 = diff * (1.0 / D)
        acc_ref[2:3, :] += _colsum(dx2 * xh2)
        acc_ref[3:4, :] += _colsum(dx2)
        dr2 = _ln_bwd(dx2, g2_ref[...], xh2, rstd2)
        dr2b_ref[...] = dr2.astype(BF)
        dx1 = ALPHA * dr2
        for j in range(N_CHIP):
            cols = slice(j * FF_SHARD, (j + 1) * FF_SHARD)
            dhid = _dot_nt(dr2b_ref[...], w2_ref[cols, :])
            dh1b_ref[:, cols] = (dhid * (2.0 * relu_ref[:, cols])).astype(BF)
            dx1 = dx1 + _dot_nt(dh1b_ref[:, cols], w1_ref[j])
        acc_ref[0:1, :] += _colsum(dx1 * xh1)
        acc_ref[1:2, :] += _colsum(dx1)
        dr1 = _ln_bwd(dx1, g1_ref[...], xh1, rstd1)
        dr1_ref[...] = dr1
        dr1b_ref[...] = dr1.astype(BF)

    tile = lambda w: pl.BlockSpec((tm, w), lambda i: (i, 0))
    vec = _resident((1, D))
    return _pc(
        body, name="ffn_fwd_bwd", grid=(t // tm,),
        in_specs=[tile(D), tile(D), _resident((N_CHIP, D, FF_SHARD)), _resident((D_FF, D)), vec, vec, vec, vec],
        out_specs=[tile(D), tile(D), tile(D), tile(D_FF), tile(D_FF), tile(D), pl.BlockSpec((8, D), lambda i: (0, 0))],
        out_shape=[jax.ShapeDtypeStruct((t, D), F32), jax.ShapeDtypeStruct((t, D), BF), jax.ShapeDtypeStruct((t, D), BF),
                   jax.ShapeDtypeStruct((t, D_FF), BF), jax.ShapeDtypeStruct((t, D_FF), BF),
                   jax.ShapeDtypeStruct((t, D), BF), jax.ShapeDtypeStruct((8, D), F32)],
        scratch_shapes=[pltpu.VMEM((tm, D_FF), F32)],
        compiler_params=_params("arbitrary"),
    )(r1, tgt, wff1, wff2, ln1g, ln1b, ln2g, ln2b)


def _dw(a, b, nblk, am, bn, a_blocked, b_blocked, tk, name):
    t = a.shape[0]

    def body(a_ref, b_ref, o_ref):
        @pl.when(pl.program_id(1) == 0)
        def _():
            o_ref[...] = jnp.zeros_like(o_ref)

        o_ref[...] += _dot_tn(a_ref[...], b_ref[...])

    return _pc(
        body, name=name, grid=(nblk, t // tk),
        in_specs=[pl.BlockSpec((tk, am), (lambda j, k: (k, j)) if a_blocked else (lambda j, k: (k, 0))),
                  pl.BlockSpec((tk, bn), (lambda j, k: (k, j)) if b_blocked else (lambda j, k: (k, 0)))],
        out_specs=pl.BlockSpec((None, am, bn), lambda j, k: (j, 0, 0)),
        out_shape=jax.ShapeDtypeStruct((nblk, am, bn), F32),
        compiler_params=_params("arbitrary", "arbitrary"),
    )(a, b)


def _dx(dp, win4, dr1, tm):
    t = dp.shape[0]

    def body(dp_ref, w_ref, dr1_ref, dx_ref):
        j = pl.program_id(1)

        @pl.when(j == 0)
        def _():
            dx_ref[...] = ALPHA * dr1_ref[...]

        dx_ref[...] += _dot_nt(dp_ref[...], w_ref[...])

    return _pc(
        body, name="dx", grid=(t // tm, N_CHIP),
        in_specs=[pl.BlockSpec((tm, NP_SHARD), lambda i, j: (i, j)),
                  pl.BlockSpec((None, D, NP_SHARD), lambda i, j: (j, 0, 0)),
                  pl.BlockSpec((tm, D), lambda i, j: (i, 0))],
        out_specs=pl.BlockSpec((tm, D), lambda i, j: (i, 0)),
        out_shape=jax.ShapeDtypeStruct((t, D), F32),
        compiler_params=_params("arbitrary", "arbitrary"),
    )(dp, win4, dr1)


def _mix_bwd(p, hal, dr1, dr1h, ya, yb, wpa, wpb, wo, wsb, wstb, bsf, bg, cw, vg, vb, tm):
    t = p.shape[0]
    nt = t // tm
    te = tm + 2 * HALO
    mid = slice(HALO, HALO + tm)

    def body(p_ref, hal_ref, dr1_ref, dr1h_ref, ya_ref, yb_ref, wpa_ref, wpb_ref, wo_ref, ws_ref, wst_ref, bsf_ref,
             bg_ref, cw_ref, vg_ref, vb_ref,
             dp_ref, ab_ref, bbb_ref, zb_ref, dyab_ref, dybb_ref, dbg_ref, dcw_ref, dvgb_ref, dws_ref, dbs_ref,
             vnb_ref, mixed_ref, dmixb_ref, dvn_ref):
        @pl.when(pl.program_id(0) == 0)
        def _():
            for r in (dbg_ref, dcw_ref, dvgb_ref, dws_ref, dbs_ref):
                r[...] = jnp.zeros_like(r)

        ca, ha, ch, ch_m1, ch_p1, cv = _conv_fwd(p_ref, hal_ref, cw_ref, tm)
        ba = p_ref[:, 0:OFF_CA]
        ab_ref[...] = (ba * cv).astype(BF)
        vb_pre, tv, xhv, rstdv = _spatial_fwd(p_ref, vg_ref, vb_ref, ws_ref, bsf_ref, vnb_ref, mixed_ref, tm)
        ub = p_ref[:, OFF_UB:OFF_VB]
        gu, tu = _gelu(ub)
        bbb_ref[...] = (gu * mixed_ref[...]).astype(BF)
        bga = bg_ref[:, 0:D]
        ga = jax.nn.sigmoid(p_ref[:, OFF_GA:OFF_GB] + bga)
        gb = jax.nn.sigmoid(p_ref[:, OFF_GB:N_PROJ] + bg_ref[:, D:2 * D])
        ya = ya_ref[...]
        yb = yb_ref[...]
        zb_ref[...] = (ga * ya + gb * yb).astype(BF)

        dr1_ext = jnp.concatenate([dr1h_ref[0:HALO, :], dr1_ref[...], dr1h_ref[HALO:2 * HALO, :]], axis=0)
        dz_ext = _dot_nt(dr1_ext.astype(BF), wo_ref[...])
        ga_ext = jnp.concatenate([jax.nn.sigmoid(hal_ref[0:HALO, OFF_GA:OFF_GB] + bga), ga,
                                  jax.nn.sigmoid(hal_ref[HALO:2 * HALO, OFF_GA:OFF_GB] + bga)], axis=0)
        dya_ext = dz_ext * ga_ext
        dyab_ref[...] = dya_ext[mid].astype(BF)
        da_ext = _dot_nt(dya_ext.astype(BF), wpa_ref[...])
        ba_ext = jnp.concatenate([hal_ref[0:HALO, 0:OFF_CA], ba, hal_ref[HALO:2 * HALO, 0:OFF_CA]], axis=0)
        dcv_ext = da_ext * ba_ext
        dcv = dcv_ext[mid]
        dch = (cw_ref[0:1, :] * pltpu.roll(dcv_ext, te - 1, 0)[mid] + cw_ref[1:2, :] * dcv
               + cw_ref[2:3, :] * pltpu.roll(dcv_ext, 1, 0)[mid])
        dp_ref[:, 0:OFF_CA] = (da_ext[mid] * cv).astype(BF)
        dp_ref[:, OFF_CA:OFF_HA] = (dch * ha).astype(BF)
        dp_ref[:, OFF_HA:OFF_UB] = (dch * ca).astype(BF)
        dcw_ref[0:1, :] += _colsum(dcv * ch_m1)
        dcw_ref[1:2, :] += _colsum(dcv * ch)
        dcw_ref[2:3, :] += _colsum(dcv * ch_p1)

        dz = dz_ext[mid]
        dga = dz * ya * ga * (1.0 - ga)
        dgb = dz * yb * gb * (1.0 - gb)
        dp_ref[:, OFF_GA:OFF_GB] = dga.astype(BF)
        dp_ref[:, OFF_GB:N_PROJ] = dgb.astype(BF)
        dbg_ref[0:1, 0:D] += _colsum(dga)
        dbg_ref[0:1, D:2 * D] += _colsum(dgb)

        dybb_ref[...] = (dz * gb).astype(BF)
        dbb = _dot_nt(dybb_ref[...], wpb_ref[...])
        dp_ref[:, OFF_UB:OFF_VB] = (dbb * mixed_ref[...] * _gelu_grad(ub, tu)).astype(BF)
        dmixed = dbb * gu
        dmixb_ref[...] = dmixed.astype(BF)
        for c in range(tm // CHUNK):
            rows = slice(c * CHUNK, (c + 1) * CHUNK)
            dbs_ref[...] += dmixed[rows]
            for h in range(N_HEAD):
                cols = slice(h * CHUNK, (h + 1) * CHUNK)
                dws_ref[h] += _dot_nt(dmixb_ref[rows, cols], vnb_ref[rows, cols])
                dvn_ref[rows, cols] = _dot(wst_ref[h], dmixb_ref[rows, cols])
        dvn = dvn_ref[...]
        dvgb_ref[0:1, :] += _colsum(dvn * xhv)
        dvgb_ref[1:2, :] += _colsum(dvn)
        dgv = _ln_bwd(dvn, vg_ref[...], xhv, rstdv)
        dp_ref[:, OFF_VB:OFF_GA] = (dgv * _gelu_grad(vb_pre, tv)).astype(BF)

    tile = lambda w: pl.BlockSpec((tm, w), lambda i: (i, 0))
    halo = lambda w: pl.BlockSpec((None, 2 * HALO, w), lambda i: (i, 0, 0))
    acc = lambda *s: pl.BlockSpec(s, lambda i: (0,) * len(s))
    return _pc(
        body, name="mix_bwd", grid=(nt,),
        in_specs=[tile(N_PROJ), halo(N_PROJ), tile(D), halo(D), tile(D), tile(D),
                  _resident((W_A, D)), _resident((W_B, D)), _resident((D, D)), _resident((N_HEAD, CHUNK, CHUNK)),
                  _resident((N_HEAD, CHUNK, CHUNK)), _resident((CHUNK, W_B)), _resident((1, 2 * D)),
                  _resident((3, W_A)), _resident((1, W_B)), _resident((1, W_B))],
        out_specs=[tile(N_PROJ), tile(W_A), tile(W_B), tile(D), tile(D), tile(D),
                   acc(8, 2 * D), acc(8, W_A), acc(8, W_B), acc(N_HEAD, CHUNK, CHUNK), acc(CHUNK, W_B)],
        out_shape=[jax.ShapeDtypeStruct((t, N_PROJ), BF), jax.ShapeDtypeStruct((t, W_A), BF),
                   jax.ShapeDtypeStruct((t, W_B), BF), jax.ShapeDtypeStruct((t, D), BF), jax.ShapeDtypeStruct((t, D), BF),
                   jax.ShapeDtypeStruct((t, D), BF), jax.ShapeDtypeStruct((8, 2 * D), F32),
                   jax.ShapeDtypeStruct((8, W_A), F32), jax.ShapeDtypeStruct((8, W_B), F32),
                   jax.ShapeDtypeStruct((N_HEAD, CHUNK, CHUNK), F32), jax.ShapeDtypeStruct((CHUNK, W_B), F32)],
        scratch_shapes=[pltpu.VMEM((tm, W_B), BF), pltpu.VMEM((tm, W_B), F32), pltpu.VMEM((tm, W_B), BF),
                        pltpu.VMEM((tm, W_B), F32)],
        compiler_params=_params("arbitrary"),
    )(p, hal, dr1, dr1h, ya, yb, wpa, wpb, wo, wsb, wstb, bsf, bg, cw, vg, vb)


HBM_SPEC = pl.BlockSpec(memory_space=pltpu.HBM)
WIN_HALF = D // 2
PACK_HALF = PACK_TOTAL // 2
CONV_ROWS = 8


def _place():
    x, y, c = lax.axis_index("x"), lax.axis_index("y"), lax.axis_index("c")
    return x, y, c, 2 * x + y


def _flip(x, y, c, r):
    return (x ^ (r >> 1), y ^ (r & 1), c)


def _remote(src, dst, send_sems, recv_sems, k, peer):
    return pltpu.make_async_remote_copy(src_ref=src, dst_ref=dst, send_sem=send_sems.at[k], recv_sem=recv_sems.at[k],
                                        device_id=peer, device_id_type=MESH)


def _cast_shards(w_in, w_pa, w_pb, w_o, w_ff1, w_ff2, chip):
    def body(j_ref, win_ref, wpa_ref, wpb_ref, wo_ref, wff1_ref, wff2_ref, win4_ref, pack4_ref):
        win4_ref[...] = win_ref[...].astype(BF)
        for name, ref in (("w_pa", wpa_ref), ("w_pb", wpb_ref), ("w_o", wo_ref), ("w_ff1", wff1_ref), ("w_ff2", wff2_ref)):
            off, rows = PACK_OFF[name]
            pack4_ref[off:off + rows, :] = ref[...].astype(BF)

    whole = lambda a: pl.BlockSpec(a.shape, lambda i, j: (0, 0), pipeline_mode=pl.Buffered(1))
    slot = lambda rows, cols: pl.BlockSpec((None, rows, cols), lambda i, j: (j[0], 0, 0))
    ws = (w_in, w_pa, w_pb, w_o, w_ff1, w_ff2)
    return _pc(
        body, name="cast_shards",
        grid_spec=pltpu.PrefetchScalarGridSpec(num_scalar_prefetch=1, grid=(1,), in_specs=[whole(w) for w in ws],
                                               out_specs=[slot(D, NP_SHARD), slot(PACK_TOTAL, D)]),
        out_shape=[jax.ShapeDtypeStruct((N_CHIP, D, NP_SHARD), BF), jax.ShapeDtypeStruct((N_CHIP, PACK_TOTAL, D), BF)],
        compiler_params=_params("arbitrary"))(chip, *ws)


def _gather_weights(win4, pack4, conv8):
    halves = (WIN_HALF, PACK_HALF)

    def body(win_in, pack_in, conv_ref, win4_ref, pack4_ref, conv4_ref, send_sems, recv_sems, local_sem):
        x, y, c, j = _place()
        sibling = (x, y, 1 - c)
        outs = (win4_ref, pack4_ref)
        own_conv = pltpu.make_async_copy(conv_ref, conv4_ref.at[j], local_sem)
        own_conv.start()
        sends = []
        for a in range(2):
            mine = outs[a].at[j, pl.ds(c * halves[a], halves[a])]
            for r in (1, 2, 3):
                sends.append(_remote(mine, mine, send_sems, recv_sems, 3 * a + r - 1, _flip(x, y, c, r)))
        for r in (1, 2, 3):
            sends.append(_remote(conv_ref, conv4_ref.at[j], send_sems, recv_sems, 5 + r, _flip(x, y, c, r)))
        for cp in sends:
            cp.start()
        passed = []
        for a in range(2):
            for r in (1, 2, 3):
                landed = outs[a].at[j ^ r, pl.ds(c * halves[a], halves[a])]
                _remote(landed, landed, send_sems, recv_sems, 3 * a + r - 1, sibling).wait_recv()
                fwd = _remote(landed, landed, send_sems, recv_sems, 9 + 3 * a + r - 1, sibling)
                fwd.start()
                passed.append(fwd)
        for r in (1, 2, 3):
            _remote(conv_ref, conv4_ref.at[j ^ r], send_sems, recv_sems, 5 + r, sibling).wait_recv()
        for a in range(2):
            for r in (1, 2, 3):
                landed = outs[a].at[j ^ r, pl.ds((1 - c) * halves[a], halves[a])]
                _remote(landed, landed, send_sems, recv_sems, 9 + 3 * a + r - 1, sibling).wait_recv()
        for cp in sends + passed:
            cp.wait_send()
        own_conv.wait()

    return _pc(
        body, name="gather_weights",
        in_specs=[HBM_SPEC] * 3, out_specs=[HBM_SPEC] * 3, input_output_aliases={0: 0, 1: 1},
        out_shape=[jax.ShapeDtypeStruct(win4.shape, BF), jax.ShapeDtypeStruct(pack4.shape, BF),
                   jax.ShapeDtypeStruct((N_CHIP, CONV_ROWS, W_A // N_CHIP), F32)],
        scratch_shapes=[pltpu.SemaphoreType.DMA((15,)), pltpu.SemaphoreType.DMA((15,)), pltpu.SemaphoreType.DMA],
    )(win4, pack4, conv8)


def _exchange_cores(gin4, gpack4, small):
    def body(gin_ref, gpack_ref, small_ref, rin_ref, rpack_ref, rsmall_ref, send_sems, recv_sems):
        x, y, c, _ = _place()
        sibling = (x, y, 1 - c)
        cps = [_remote(gin_ref.at[:, pl.ds((1 - c) * WIN_HALF, WIN_HALF), :], rin_ref, send_sems, recv_sems, 0, sibling),
               _remote(gpack_ref.at[:, pl.ds((1 - c) * PACK_HALF, PACK_HALF), :], rpack_ref, send_sems, recv_sems, 1, sibling),
               _remote(small_ref, rsmall_ref, send_sems, recv_sems, 2, sibling)]
        for cp in cps:
            cp.start()
        for cp in cps:
            cp.wait()

    return _pc(
        body, name="exchange_cores", in_specs=[HBM_SPEC] * 3, out_specs=[HBM_SPEC] * 3,
        out_shape=[jax.ShapeDtypeStruct((N_CHIP, WIN_HALF, NP_SHARD), F32),
                   jax.ShapeDtypeStruct((N_CHIP, PACK_HALF, D), F32), jax.ShapeDtypeStruct(small.shape, F32)],
        scratch_shapes=[pltpu.SemaphoreType.DMA((3,)), pltpu.SemaphoreType.DMA((3,))],
    )(gin4, gpack4, small)


def _exchange_chips(sb_in4, sb_pack4, csmall):
    def body(sin_ref, spack_ref, cs_ref, rin_ref, rpack_ref, rsm_ref, send_sems, recv_sems):
        x, y, c, j = _place()
        cps = []
        for r in (1, 2, 3):
            peer = _flip(x, y, c, r)
            cps.append(_remote(sin_ref.at[j ^ r], rin_ref.at[r - 1], send_sems, recv_sems, 3 * (r - 1), peer))
            cps.append(_remote(spack_ref.at[j ^ r], rpack_ref.at[r - 1], send_sems, recv_sems, 3 * (r - 1) + 1, peer))
            cps.append(_remote(cs_ref, rsm_ref.at[r - 1], send_sems, recv_sems, 3 * (r - 1) + 2, peer))
        for cp in cps:
            cp.start()
        for cp in cps:
            cp.wait()

    return _pc(
        body, name="exchange_chips", in_specs=[HBM_SPEC] * 3, out_specs=[HBM_SPEC] * 3,
        out_shape=[jax.ShapeDtypeStruct((3, WIN_HALF, NP_SHARD), BF), jax.ShapeDtypeStruct((3, PACK_HALF, D), BF),
                   jax.ShapeDtypeStruct((3,) + csmall.shape, F32)],
        scratch_shapes=[pltpu.SemaphoreType.DMA((9,)), pltpu.SemaphoreType.DMA((9,))],
    )(sb_in4, sb_pack4, csmall)


def _join_cores(g_in, g_pack):
    def body(gin_in, gpack_in, gin_ref, gpack_ref, send_sems, recv_sems):
        x, y, c, _ = _place()
        sibling = (x, y, 1 - c)
        cps = []
        for a, (ref, half) in enumerate(((gin_ref, WIN_HALF), (gpack_ref, PACK_HALF))):
            mine = ref.at[pl.ds(c * half, half), :]
            cps.append(_remote(mine, mine, send_sems, recv_sems, a, sibling))
        for cp in cps:
            cp.start()
        for a, (ref, half) in enumerate(((gin_ref, WIN_HALF), (gpack_ref, PACK_HALF))):
            theirs = ref.at[pl.ds((1 - c) * half, half), :]
            _remote(theirs, theirs, send_sems, recv_sems, a, sibling).wait_recv()
        for cp in cps:
            cp.wait_send()

    return _pc(
        body, name="join_cores", in_specs=[HBM_SPEC] * 2, out_specs=[HBM_SPEC] * 2, input_output_aliases={0: 0, 1: 1},
        out_shape=[jax.ShapeDtypeStruct(g_in.shape, F32), jax.ShapeDtypeStruct(g_pack.shape, F32)],
        scratch_shapes=[pltpu.SemaphoreType.DMA((2,)), pltpu.SemaphoreType.DMA((2,))],
    )(g_in, g_pack)


def _add_own_half(full4, recv4, core, rb, name):
    n, rh, cols = recv4.shape
    nb = rh // rb

    def body(c_ref, a_ref, b_ref, o_ref, ob_ref):
        s = a_ref[...] + b_ref[...]
        o_ref[...] = s
        ob_ref[...] = s.astype(BF)

    blk = (None, rb, cols)
    same = pl.BlockSpec(blk, lambda k, i, c: (k, i, 0))
    return _pc(
        body, name=name,
        grid_spec=pltpu.PrefetchScalarGridSpec(
            num_scalar_prefetch=1, grid=(n, nb),
            in_specs=[pl.BlockSpec(blk, lambda k, i, c: (k, c[0] * nb + i, 0)), same], out_specs=[same, same]),
        out_shape=[jax.ShapeDtypeStruct(recv4.shape, F32), jax.ShapeDtypeStruct(recv4.shape, BF)],
        compiler_params=_params("arbitrary", "arbitrary"),
    )(core, full4, recv4)


def _add_chips(s4, r3, place, rb, name):
    _, rh, cols = r3.shape
    nb = rh // rb

    def body(pl_ref, s_ref, r_ref, o_ref):
        o_ref[...] = ((s_ref[...] + r_ref[0].astype(F32)) + r_ref[1].astype(F32)) + r_ref[2].astype(F32)

    return _pc(
        body, name=name,
        grid_spec=pltpu.PrefetchScalarGridSpec(
            num_scalar_prefetch=1, grid=(nb,),
            in_specs=[pl.BlockSpec((None, rb, cols), lambda i, s: (s[0], i, 0)), pl.BlockSpec((3, rb, cols), lambda i, s: (0, i, 0))],
            out_specs=pl.BlockSpec((rb, cols), lambda i, s: (s[1] * nb + i, 0))),
        out_shape=jax.ShapeDtypeStruct((2 * rh, cols), F32),
        compiler_params=_params("arbitrary"),
    )(place, s4, r3)


def _add_small(a, b):
    def body(a_ref, b_ref, o_ref):
        o_ref[...] = a_ref[...] + b_ref[...]

    return _pc(body, name="add_small_cores", out_shape=jax.ShapeDtypeStruct(a.shape, F32))(a, b)


def _sum_small_chips(own, slots, place):
    def body(pl_ref, own_ref, s_ref, o_ref):
        j = pl_ref[0]

        def term(k):
            return jnp.where(j == k, own_ref[...], s_ref[jnp.maximum((j ^ k) - 1, 0)])

        o_ref[...] = ((term(0) + term(1)) + term(2)) + term(3)

    vmem = pl.BlockSpec(memory_space=pltpu.VMEM)
    return _pc(body, name="sum_small_chips", in_specs=[pl.BlockSpec(memory_space=pltpu.SMEM), vmem, vmem], out_specs=vmem,
               out_shape=jax.ShapeDtypeStruct(own.shape, F32))(place, own, slots)


def _adamw(w, g, m, v, rb, name):
    rows, cols = w.shape

    def body(w_ref, g_ref, m_ref, v_ref, d_ref, m2_ref, v2_ref):
        g_ = g_ref[...]
        m2 = ADAM_B1 * m_ref[...] + (1.0 - ADAM_B1) * g_
        v2 = ADAM_B2 * v_ref[...] + (1.0 - ADAM_B2) * (g_ * g_)
        m_hat = m2 / (1.0 - ADAM_B1 ** ADAM_STEP)
        v_hat = v2 / (1.0 - ADAM_B2 ** ADAM_STEP)
        d_ref[...] = -ADAM_LR * (m_hat / (jnp.sqrt(v_hat) + ADAM_EPS) + ADAM_WD * w_ref[...])
        m2_ref[...] = m2
        v2_ref[...] = v2

    blk = pl.BlockSpec((rb, cols), lambda i: (i, 0))
    return _pc(body, name=name, grid=(rows // rb,), in_specs=[blk] * 4, out_specs=[blk] * 3,
               out_shape=[jax.ShapeDtypeStruct((rows, cols), F32)] * 3, compiler_params=_params("arbitrary"))(w, g, m, v)


LANES = 128
SMALL_GRADS = (("b_gate", 2 * D), ("conv_w", 3 * W_A), ("v_norm_g", W_B), ("v_norm_b", W_B),
               ("w_s", N_HEAD * CHUNK * CHUNK), ("b_s", N_HEAD * CHUNK), ("ln1_g", D), ("ln1_b", D), ("ln2_g", D), ("ln2_b", D))


def _pack_rows(parts):
    rows = []
    for a in parts:
        a = a.reshape(-1)
        a = jnp.pad(a, (0, (-a.shape[0]) % LANES))
        rows.append(a.reshape(-1, LANES))
    out = jnp.concatenate(rows, axis=0)
    return jnp.pad(out, ((0, (-out.shape[0]) % 8), (0, 0)))


def _unpack_rows(buf, sizes):
    out, r = [], 0
    for n in sizes:
        nr = -(-n // LANES)
        out.append(buf[r:r + nr].reshape(-1)[:n])
        r += nr
    return out


def _halo(a, tm):
    t, w = a.shape
    a3 = a.reshape(t // tm, tm, w)
    zero = jnp.zeros((1, HALO, w), a.dtype)
    prev = jnp.concatenate([zero, a3[:-1, tm - HALO:, :]], axis=0)
    nxt = jnp.concatenate([a3[1:, :HALO, :], zero], axis=0)
    return jnp.concatenate([prev, nxt], axis=1)


TM_PROJ = 1024
TM_MIX = 256
TM_FFN = 256
TM_DX = 1024
TK_DW = 2048
TK_DW_IN = 1024
RB_ADD = 256
RB_ADD_PACK = PACK_HALF // 2
RB_ADAM = 128


def kernel(x, w_in, b_gate, conv_w, v_norm_g, v_norm_b, w_s, b_s, w_pa, w_pb, w_o, ln1_g, ln1_b, w_ff1, w_ff2, ln2_g, ln2_b, loss_target, m_w_in, m_b_gate, m_conv_w, m_v_norm_g, m_v_norm_b, m_w_s, m_b_s, m_w_pa, m_w_pb, m_w_o, m_ln1_g, m_ln1_b, m_w_ff1, m_w_ff2, m_ln2_g, m_ln2_b, v_w_in, v_b_gate, v_conv_w, v_v_norm_g, v_v_norm_b, v_w_s, v_b_s, v_w_pa, v_w_pb, v_w_o, v_ln1_g, v_ln1_b, v_w_ff1, v_w_ff2, v_ln2_g, v_ln2_b):
    t = x.shape[1]
    core = lax.axis_index("c").astype(jnp.int32).reshape(1)
    chip_idx = 2 * lax.axis_index("x") + lax.axis_index("y")
    chip = chip_idx.astype(jnp.int32).reshape(1)
    place = jnp.concatenate([chip, core])
    x2 = x.reshape(t, D)
    tgt = loss_target.reshape(t, D)

    win4, pack4 = _cast_shards(w_in[0], w_pa[0], w_pb[0], w_o[0], w_ff1[0], w_ff2[0], chip)
    conv8 = jnp.pad(conv_w[0], ((0, CONV_ROWS - 3), (0, 0)))
    win4, pack4, conv4 = _gather_weights(win4, pack4, conv8)

    def full(name, rows_total):
        off, rows = PACK_OFF[name]
        return pack4[:, off:off + rows, :].reshape(rows_total, D)

    wpa, wpb, wo, wff2 = full("w_pa", W_A), full("w_pb", W_B), full("w_o", D), full("w_ff2", D_FF)
    off1, rows1 = PACK_OFF["w_ff1"]
    wff1_4 = pack4[:, off1:off1 + rows1, :]
    cw = jnp.transpose(conv4[:, :3, :], (1, 0, 2)).reshape(3, W_A)
    wsb = w_s[0].astype(BF)
    wstb = jnp.swapaxes(w_s[0], 1, 2).astype(BF)
    bsf = jnp.repeat(jnp.transpose(b_s[0]), CHUNK, axis=1)

    p, xb = _proj_fwd(x2, win4, TM_PROJ)
    hal = _halo(p, TM_MIX)
    r1, ya, yb = _mix_fwd(p, hal, x2, wpa, wpb, wo, wsb, bsf, b_gate, cw, v_norm_g, v_norm_b, TM_MIX)
    dr1, dr1b, x1b, hidb, dh1b, dr2b, acc = _ffn_fwd_bwd(r1, tgt, wff1_4, wff2, ln1_g, ln1_b, ln2_g, ln2_b, TM_FFN)
    dwff1_4 = _dw(x1b, dh1b, N_CHIP, D, FF_SHARD, False, True, TK_DW, "dw_ff1")
    dwff2_4 = _dw(hidb, dr2b, N_CHIP, FF_SHARD, D, True, False, TK_DW, "dw_ff2")
    dr1h = _halo(dr1, TM_MIX)
    dp, ab, bbb, zb, dyab, dybb, dbg, dcw, dvgb, dws, dbs_sum = _mix_bwd(
        p, hal, dr1, dr1h, ya, yb, wpa, wpb, wo, wsb, wstb, bsf, b_gate, cw, v_norm_g, v_norm_b, TM_MIX)
    dx = _dx(dp, win4, dr1, TM_DX)
    dwin4 = _dw(xb, dp, N_CHIP, D, NP_SHARD, False, True, TK_DW_IN, "dw_in")
    dwpa = _dw(ab, dyab, 1, W_A, D, False, False, TK_DW, "dw_pa")
    dwpb = _dw(bbb, dybb, 1, W_B, D, False, False, TK_DW, "dw_pb")
    dwo = _dw(zb, dr1b, 1, D, D, False, False, TK_DW, "dw_o")
    gpack4 = jnp.concatenate([dwpa.reshape(N_CHIP, -1, D), dwpb.reshape(N_CHIP, -1, D), dwo.reshape(N_CHIP, -1, D),
                              dwff1_4, dwff2_4], axis=1)
    dbs = jnp.transpose(jnp.sum(dbs_sum.reshape(CHUNK, N_HEAD, CHUNK), axis=-1))
    small = _pack_rows([dbg[0], dcw[0:3], dvgb[0], dvgb[1], dws, dbs, acc[0], acc[1], acc[2], acc[3]])
    loss = lax.psum(0.5 * jnp.sum(acc[4]) / D, ("x", "y", "c"))

    rin, rpack, rsmall = _exchange_cores(dwin4, gpack4, small)
    sin4, sbin4 = _add_own_half(dwin4, rin, core, RB_ADD, "add_cores_in")
    spack4, sbpack4 = _add_own_half(gpack4, rpack, core, RB_ADD_PACK, "add_cores_pack")
    csmall = _add_small(small, rsmall)
    cin, cpack, slots = _exchange_chips(sbin4, sbpack4, csmall)
    gsmall = _sum_small_chips(csmall, slots, place)
    g_in, g_pack = _join_cores(_add_chips(sin4, cin, place, RB_ADD, "add_chips_in"),
                               _add_chips(spack4, cpack, place, RB_ADD_PACK, "add_chips_pack"))

    grads = {"w_in": g_in}
    for name, _ in PACK_ROWS:
        off, rows = PACK_OFF[name]
        grads[name] = g_pack[off:off + rows, :]
    for (name, n), flat in zip(SMALL_GRADS, _unpack_rows(gsmall, [n for _, n in SMALL_GRADS])):
        grads[name] = flat
    grads["conv_w"] = lax.dynamic_slice(grads["conv_w"].reshape(3, W_A), (0, chip_idx * (W_A // N_CHIP)), (3, W_A // N_CHIP))

    weights = dict(w_in=w_in, b_gate=b_gate, conv_w=conv_w, v_norm_g=v_norm_g, v_norm_b=v_norm_b, w_s=w_s, b_s=b_s,
                   w_pa=w_pa, w_pb=w_pb, w_o=w_o, ln1_g=ln1_g, ln1_b=ln1_b, w_ff1=w_ff1, w_ff2=w_ff2, ln2_g=ln2_g, ln2_b=ln2_b)
    mom1 = dict(w_in=m_w_in, b_gate=m_b_gate, conv_w=m_conv_w, v_norm_g=m_v_norm_g, v_norm_b=m_v_norm_b, w_s=m_w_s,
                b_s=m_b_s, w_pa=m_w_pa, w_pb=m_w_pb, w_o=m_w_o, ln1_g=m_ln1_g, ln1_b=m_ln1_b, w_ff1=m_w_ff1,
                w_ff2=m_w_ff2, ln2_g=m_ln2_g, ln2_b=m_ln2_b)
    mom2 = dict(w_in=v_w_in, b_gate=v_b_gate, conv_w=v_conv_w, v_norm_g=v_v_norm_g, v_norm_b=v_v_norm_b, w_s=v_w_s,
                b_s=v_b_s, w_pa=v_w_pa, w_pb=v_w_pb, w_o=v_w_o, ln1_g=v_ln1_g, ln1_b=v_ln1_b, w_ff1=v_w_ff1,
                w_ff2=v_w_ff2, ln2_g=v_ln2_g, ln2_b=v_ln2_b)
    order = list(weights)
    big = ("w_in", "w_pa", "w_pb", "w_o", "w_ff1", "w_ff2")
    delta, new_m, new_v = {}, {}, {}
    for name in big:
        w2 = weights[name][0]
        delta[name], new_m[name], new_v[name] = _adamw(w2, grads[name], mom1[name][0], mom2[name][0], RB_ADAM, "adamw_" + name)
    little = [n for n in order if n not in big]
    sizes = [weights[n].size for n in little]
    wsmall = _pack_rows([weights[n] for n in little])
    ds, ms, vs = _adamw(wsmall, _pack_rows([grads[n] for n in little]), _pack_rows([mom1[n] for n in little]),
                        _pack_rows([mom2[n] for n in little]), wsmall.shape[0], "adamw_small")
    for name, d_, m_, v_ in zip(little, _unpack_rows(ds, sizes), _unpack_rows(ms, sizes), _unpack_rows(vs, sizes)):
        delta[name], new_m[name], new_v[name] = d_, m_, v_

    shaped = lambda d: [d[n].reshape(weights[n].shape) for n in order]
    return (loss, dx.reshape(x.shape), *shaped(grads), *shaped(delta), *shaped(new_m), *shaped(new_v))
```

```python
import functools
from typing import NamedTuple

import jax
import jax.numpy as jnp
from jax import lax
from jax.experimental import pallas as pl
from jax.experimental.pallas import tpu as pltpu

D = 1024
W_A = 1536
W_B = 1024
CHUNK = 128
N_HEAD = 8
D_FF = 4096
N_PROJ = 3 * W_A + 2 * W_B + 2 * D
OFF_CA, OFF_HA, OFF_UB, OFF_VB, OFF_GA, OFF_GB = 1536, 3072, 4608, 5632, 6656, 7680
LN_EPS = 1e-5
ALPHA = 2.0 ** 0.25
N_CHIP = 4
NP_SHARD = N_PROJ // N_CHIP
FF_SHARD = D_FF // N_CHIP
ADAM_LR, ADAM_B1, ADAM_B2, ADAM_EPS, ADAM_WD, ADAM_STEP = 0.001, 0.9, 0.999, 1e-08, 0.01, 10

PROJ_ROWS = (("w_pa", W_A // N_CHIP), ("w_pb", W_B // N_CHIP), ("w_o", D // N_CHIP))
PROJ_OFF = {}
_o = 0
for _n, _r in PROJ_ROWS:
    PROJ_OFF[_n] = (_o, _r)
    _o += _r
PROJ_TOTAL = _o

V7X_VMEM_BYTES = 64 * 1024 * 1024
VMEM_LIMIT = 56 * 1024 * 1024
HALO = 8

BF = jnp.bfloat16
F32 = jnp.float32
MESH = pl.DeviceIdType.MESH
HBM_SPEC = pl.BlockSpec(memory_space=pltpu.HBM)


def _pc(body, **kw):
    return pl.pallas_call(body, **kw)


def _params(*sem):
    return pltpu.CompilerParams(dimension_semantics=sem, vmem_limit_bytes=VMEM_LIMIT)


def _resident(shape):
    n = len(shape)
    return pl.BlockSpec(shape, lambda *_: (0,) * n, pipeline_mode=pl.Buffered(1))


def _dot(a, b):
    return jnp.dot(a, b, preferred_element_type=F32)


def _dot_nt(a, b):
    return lax.dot_general(a, b, (((1,), (1,)), ((), ())), preferred_element_type=F32)


def _dot_tn(a, b):
    return lax.dot_general(a, b, (((0,), (0,)), ((), ())), preferred_element_type=F32)


def _gelu(x):
    t = jnp.tanh(0.7978845608028654 * (x + 0.044715 * (x * x * x)))
    return 0.5 * x * (1.0 + t), t


def _gelu_grad(x, t):
    return 0.5 * (1.0 + t) + 0.5 * x * (1.0 - t * t) * (0.7978845608028654 * (1.0 + 0.134145 * (x * x)))


def _ln_stats(r):
    mu = jnp.mean(r, axis=-1, keepdims=True)
    xc = r - mu
    var = jnp.mean(xc * xc, axis=-1, keepdims=True)
    rstd = lax.rsqrt(var + LN_EPS)
    return xc * rstd, rstd


def _ln_bwd(dy, g, xh, rstd):
    dxh = dy * g
    m1 = jnp.mean(dxh, axis=-1, keepdims=True)
    m2 = jnp.mean(dxh * xh, axis=-1, keepdims=True)
    return rstd * (dxh - m1 - xh * m2)


def _colsum(v):
    return jnp.sum(v, axis=0, keepdims=True)


class _Comm(NamedTuple):
    args: tuple
    out_shape: tuple
    aliases: dict
    n_sems: int
    stages: tuple


def _place():
    x, y, c = lax.axis_index("x"), lax.axis_index("y"), lax.axis_index("c")
    return x, y, c, 2 * x + y


def _flip(x, y, c, r):
    return (x ^ (r >> 1), y ^ (r & 1), c)


def _remote(src, dst, send_sems, recv_sems, k, peer):
    return pltpu.make_async_remote_copy(src_ref=src, dst_ref=dst, send_sem=send_sems.at[k], recv_sem=recv_sems.at[k],
                                        device_id=peer, device_id_type=MESH)


def _host_call(body, comm, *, name, grid, in_specs, out_specs, out_shape, args, scratch_shapes=(), aliases=None):
    sem = ("arbitrary",) * len(grid)
    aliases = dict(aliases or {})
    if comm is None:
        return _pc(body, name=name, grid=grid, in_specs=list(in_specs), out_specs=list(out_specs), out_shape=list(out_shape),
                   scratch_shapes=list(scratch_shapes), input_output_aliases=aliases, compiler_params=_params(*sem))(*args)
    n_in, n_out, n_scr = len(in_specs), len(out_specs), len(scratch_shapes)
    c_in, c_out = len(comm.args), len(comm.out_shape)
    steps = {"first": (0,) * len(grid), "late": (max(grid[0] - 2, 0),) + (0,) * (len(grid) - 1),
             "last": tuple(g - 1 for g in grid)}

    def wrapped(*refs):
        own_in, cin = refs[:n_in], refs[n_in:n_in + c_in]
        o0 = n_in + c_in
        own_out, cout = refs[o0:o0 + n_out], refs[o0 + n_out:o0 + n_out + c_out]
        s0 = o0 + n_out + c_out
        scr, (send_sems, recv_sems) = refs[s0:s0 + n_scr], refs[s0 + n_scr:]

        def run(phase):
            step = steps[phase]
            cond = pl.program_id(0) == step[0]
            for d in range(1, len(grid)):
                cond = jnp.logical_and(cond, pl.program_id(d) == step[d])
            for ph, fn in comm.stages:
                if ph == phase:
                    pl.when(cond)(functools.partial(fn, cin, cout, send_sems, recv_sems))

        run("first")
        body(*own_in, *own_out, *scr)
        run("late")
        run("last")

    aliases.update({n_in + i: n_out + o for i, o in comm.aliases.items()})
    outs = _pc(
        wrapped, name=name, grid=grid, in_specs=list(in_specs) + [HBM_SPEC] * c_in,
        out_specs=list(out_specs) + [HBM_SPEC] * c_out, out_shape=list(out_shape) + list(comm.out_shape),
        scratch_shapes=list(scratch_shapes) + [pltpu.SemaphoreType.DMA((comm.n_sems,))] * 2,
        input_output_aliases=aliases, compiler_params=_params(*sem))(*args, *comm.args)
    return outs[:n_out], outs[n_out:]


def _comm_call(comm, name):
    def body(*refs):
        c_in, c_out = len(comm.args), len(comm.out_shape)
        cin, cout, (send_sems, recv_sems) = refs[:c_in], refs[c_in:c_in + c_out], refs[c_in + c_out:]
        for phase in ("first", "late", "last"):
            for ph, fn in comm.stages:
                if ph == phase:
                    fn(cin, cout, send_sems, recv_sems)

    return _pc(body, name=name, in_specs=[HBM_SPEC] * len(comm.args), out_specs=[HBM_SPEC] * len(comm.out_shape),
               out_shape=list(comm.out_shape), scratch_shapes=[pltpu.SemaphoreType.DMA((comm.n_sems,))] * 2,
               input_output_aliases=dict(comm.aliases))(*comm.args)


def _gather_comm(bufs, whole=None):
    n = len(bufs)
    halves = [b.shape[1] // 2 for b in bufs]
    k_ici = lambda a, r: 3 * a + r - 1
    k_d2d = lambda a, r: 3 * n + 3 * a + r - 1
    k_whole = lambda r: 6 * n + r - 1

    def half(ref, slot, c, a):
        return ref.at[slot, pl.ds(c * halves[a], halves[a])]

    def send(cin, cout, ss, rs):
        x, y, c, j = _place()
        for a in range(n):
            mine = half(cout[a], j, c, a)
            for r in (1, 2, 3):
                _remote(mine, mine, ss, rs, k_ici(a, r), _flip(x, y, c, r)).start()
        if whole is not None:
            for r in (1, 2, 3):
                _remote(cout[n].at[j], cout[n].at[j], ss, rs, k_whole(r), _flip(x, y, c, r)).start()

    def forward(cin, cout, ss, rs):
        x, y, c, j = _place()
        sibling = (x, y, 1 - c)
        for a in range(n):
            for r in (1, 2, 3):
                landed = half(cout[a], j ^ r, c, a)
                _remote(landed, landed, ss, rs, k_ici(a, r), sibling).wait_recv()
                _remote(landed, landed, ss, rs, k_d2d(a, r), sibling).start()

    def finish(cin, cout, ss, rs):
        x, y, c, j = _place()
        sibling = (x, y, 1 - c)
        for a in range(n):
            for r in (1, 2, 3):
                theirs = half(cout[a], j ^ r, 1 - c, a)
                _remote(theirs, theirs, ss, rs, k_d2d(a, r), sibling).wait_recv()
        for a in range(n):
            mine = half(cout[a], j, c, a)
            for r in (1, 2, 3):
                _remote(mine, mine, ss, rs, k_ici(a, r), sibling).wait_send()
                landed = half(cout[a], j ^ r, c, a)
                _remote(landed, landed, ss, rs, k_d2d(a, r), sibling).wait_send()
        if whole is not None:
            for r in (1, 2, 3):
                cp = _remote(cout[n].at[j ^ r], cout[n].at[j ^ r], ss, rs, k_whole(r), sibling)
                cp.wait_recv()
                cp.wait_send()

    args = tuple(bufs) + ((whole,) if whole is not None else ())
    out_shape = tuple(jax.ShapeDtypeStruct(b.shape, b.dtype) for b in args)
    aliases = {a: a for a in range(len(args))}
    return _Comm(args, out_shape, aliases, 6 * n + 3, (("first", send), ("late", forward), ("last", finish)))


def _pair_comm(grads, small=None):
    n = len(grads)
    halves = [g.shape[1] // 2 for g in grads]

    def copies(cin, cout, ss, rs):
        x, y, c, _ = _place()
        sibling = (x, y, 1 - c)
        cps = [_remote(cin[a].at[:, pl.ds((1 - c) * halves[a], halves[a]), :], cout[a], ss, rs, a, sibling) for a in range(n)]
        if small is not None:
            cps.append(_remote(cin[n], cout[n], ss, rs, n, sibling))
        return cps

    def start(cin, cout, ss, rs):
        for cp in copies(cin, cout, ss, rs):
            cp.start()

    def finish(cin, cout, ss, rs):
        for cp in copies(cin, cout, ss, rs):
            cp.wait()

    args = tuple(grads) + ((small,) if small is not None else ())
    out_shape = tuple(jax.ShapeDtypeStruct((N_CHIP, h, g.shape[2]), F32) for g, h in zip(grads, halves))
    out_shape += (jax.ShapeDtypeStruct(small.shape, F32),) if small is not None else ()
    return _Comm(args, out_shape, {}, n + 1, (("first", start), ("last", finish)))


def _chips_comm(sums_bf, small=None):
    n = len(sums_bf)

    def copies(cin, cout, ss, rs):
        x, y, c, j = _place()
        cps = []
        for r in (1, 2, 3):
            peer = _flip(x, y, c, r)
            for a in range(n):
                cps.append(_remote(cin[a].at[j ^ r], cout[a].at[r - 1], ss, rs, (n + 1) * (r - 1) + a, peer))
            if small is not None:
                cps.append(_remote(cin[n], cout[n].at[r - 1], ss, rs, (n + 1) * (r - 1) + n, peer))
        return cps

    def start(cin, cout, ss, rs):
        for cp in copies(cin, cout, ss, rs):
            cp.start()

    def finish(cin, cout, ss, rs):
        for cp in copies(cin, cout, ss, rs):
            cp.wait()

    args = tuple(sums_bf) + ((small,) if small is not None else ())
    out_shape = tuple(jax.ShapeDtypeStruct((3,) + s.shape[1:], BF) for s in sums_bf)
    out_shape += (jax.ShapeDtypeStruct((3,) + small.shape, F32),) if small is not None else ()
    return _Comm(args, out_shape, {}, 3 * (n + 1), (("first", start), ("last", finish)))


def _join_comm(shards):
    n = len(shards)
    halves = [s.shape[0] // 2 for s in shards]

    def start(cin, cout, ss, rs):
        x, y, c, _ = _place()
        for a in range(n):
            mine = cout[a].at[pl.ds(c * halves[a], halves[a]), :]
            _remote(mine, mine, ss, rs, a, (x, y, 1 - c)).start()

    def finish(cin, cout, ss, rs):
        x, y, c, _ = _place()
        for a in range(n):
            theirs = cout[a].at[pl.ds((1 - c) * halves[a], halves[a]), :]
            cp = _remote(theirs, theirs, ss, rs, a, (x, y, 1 - c))
            cp.wait_recv()
            cp.wait_send()

    out_shape = tuple(jax.ShapeDtypeStruct(s.shape, F32) for s in shards)
    return _Comm(tuple(shards), out_shape, {a: a for a in range(n)}, n, (("first", start), ("last", finish)))


def _cast_shards(w_in, w_pa, w_pb, w_o, w_ff1, w_ff2, chip):
    def body(j_ref, win_ref, wpa_ref, wpb_ref, wo_ref, wff1_ref, wff2_ref, win4_ref, proj4_ref, ff14_ref, ff24_ref):
        win4_ref[...] = win_ref[...].astype(BF)
        for name, ref in (("w_pa", wpa_ref), ("w_pb", wpb_ref), ("w_o", wo_ref)):
            off, rows = PROJ_OFF[name]
            proj4_ref[off:off + rows, :] = ref[...].astype(BF)
        ff14_ref[...] = wff1_ref[...].astype(BF)
        ff24_ref[...] = wff2_ref[...].astype(BF)

    whole = lambda a: pl.BlockSpec(a.shape, lambda i, j: (0, 0), pipeline_mode=pl.Buffered(1))
    slot = lambda rows, cols: pl.BlockSpec((None, rows, cols), lambda i, j: (j[0], 0, 0))
    ws = (w_in, w_pa, w_pb, w_o, w_ff1, w_ff2)
    shapes = ((D, NP_SHARD), (PROJ_TOTAL, D), (D, FF_SHARD), (FF_SHARD, D))
    return _pc(
        body, name="cast_shards",
        grid_spec=pltpu.PrefetchScalarGridSpec(num_scalar_prefetch=1, grid=(1,), in_specs=[whole(w) for w in ws],
                                               out_specs=[slot(*s) for s in shapes]),
        out_shape=[jax.ShapeDtypeStruct((N_CHIP,) + s, BF) for s in shapes],
        compiler_params=_params("arbitrary"))(chip, *ws)


def _proj_fwd(x, win4, tm, tmix, comm):
    t = x.shape[0]
    sub = tm // tmix

    def body(x_ref, w_ref, p_ref, xb_ref, edge_ref):
        @pl.when(pl.program_id(1) == 0)
        def _():
            xb_ref[...] = x_ref[...].astype(BF)

        p_ref[...] = _dot(xb_ref[...], w_ref[...])
        for s in range(sub):
            edge_ref[s, 0:HALO, :] = p_ref[s * tmix:s * tmix + HALO, :]
            edge_ref[s, HALO:2 * HALO, :] = p_ref[(s + 1) * tmix - HALO:(s + 1) * tmix, :]

    return _host_call(
        body, comm, name="proj_fwd", grid=(t // tm, N_CHIP),
        in_specs=[pl.BlockSpec((tm, D), lambda i, j: (i, 0)), pl.BlockSpec((None, D, NP_SHARD), lambda i, j: (j, 0, 0))],
        out_specs=[pl.BlockSpec((tm, NP_SHARD), lambda i, j: (i, j)), pl.BlockSpec((tm, D), lambda i, j: (i, 0)),
                   pl.BlockSpec((sub, 2 * HALO, NP_SHARD), lambda i, j: (i, 0, j))],
        out_shape=[jax.ShapeDtypeStruct((t, N_PROJ), F32), jax.ShapeDtypeStruct((t, D), BF),
                   jax.ShapeDtypeStruct((t // tmix, 2 * HALO, N_PROJ), F32)],
        args=(x, win4))


def _edge_specs(nt, w):
    return [pl.BlockSpec((None, HALO, w), lambda i: (jnp.maximum(i - 1, 0), 1, 0)),
            pl.BlockSpec((None, HALO, w), lambda i: (jnp.minimum(i + 1, nt - 1), 0, 0))]


def _end_masks(nt):
    i = pl.program_id(0)
    return (i > 0).astype(F32), (i < nt - 1).astype(F32)


def _conv_fwd(p_ref, prev_ref, next_ref, cw_ref, tm, has_prev, has_next):
    ca = p_ref[:, OFF_CA:OFF_HA]
    ha = p_ref[:, OFF_HA:OFF_UB]
    ch = ca * ha
    ch_prev = prev_ref[HALO - 1:HALO, OFF_CA:OFF_HA] * prev_ref[HALO - 1:HALO, OFF_HA:OFF_UB] * has_prev
    ch_next = next_ref[0:1, OFF_CA:OFF_HA] * next_ref[0:1, OFF_HA:OFF_UB] * has_next
    row = lax.broadcasted_iota(jnp.int32, (tm, W_A), 0)
    ch_m1 = jnp.where(row == 0, ch_prev, pltpu.roll(ch, 1, 0))
    ch_p1 = jnp.where(row == tm - 1, ch_next, pltpu.roll(ch, tm - 1, 0))
    cv = cw_ref[0:1, :] * ch_m1 + cw_ref[1:2, :] * ch + cw_ref[2:3, :] * ch_p1
    return ca, ha, ch, ch_m1, ch_p1, cv


def _spatial_fwd(p_ref, vg_ref, vb_ref, ws_ref, bsf_ref, vnb_ref, mixed_ref, tm):
    vb_pre = p_ref[:, OFF_VB:OFF_GA]
    gv, tv = _gelu(vb_pre)
    xhv, rstdv = _ln_stats(gv)
    vnb_ref[...] = (xhv * vg_ref[...] + vb_ref[...]).astype(BF)
    for c in range(tm // CHUNK):
        rows = slice(c * CHUNK, (c + 1) * CHUNK)
        for h in range(N_HEAD):
            cols = slice(h * CHUNK, (h + 1) * CHUNK)
            mixed_ref[rows, cols] = _dot(ws_ref[h], vnb_ref[rows, cols]) + bsf_ref[:, cols]
    return vb_pre, tv, xhv, rstdv


def _mix_fwd(p, pedge, x, wpa, wpb, wo, wsb, bsf, bg, cw, vg, vb, tm):
    t = x.shape[0]
    nt = t // tm

    def body(p_ref, prev_ref, next_ref, x_ref, wpa_ref, wpb_ref, wo_ref, ws_ref, bsf_ref, bg_ref, cw_ref, vg_ref, vb_ref,
             r1_ref, ya_ref, yb_ref, vnb_ref, mixed_ref):
        has_prev, has_next = _end_masks(nt)
        _, _, _, _, _, cv = _conv_fwd(p_ref, prev_ref, next_ref, cw_ref, tm, has_prev, has_next)
        a = p_ref[:, 0:OFF_CA] * cv
        ya = _dot(a.astype(BF), wpa_ref[...])
        ya_ref[...] = ya
        _spatial_fwd(p_ref, vg_ref, vb_ref, ws_ref, bsf_ref, vnb_ref, mixed_ref, tm)
        gu, _ = _gelu(p_ref[:, OFF_UB:OFF_VB])
        bb = gu * mixed_ref[...]
        yb = _dot(bb.astype(BF), wpb_ref[...])
        yb_ref[...] = yb
        ga = jax.nn.sigmoid(p_ref[:, OFF_GA:OFF_GB] + bg_ref[:, 0:D])
        gb = jax.nn.sigmoid(p_ref[:, OFF_GB:N_PROJ] + bg_ref[:, D:2 * D])
        z = ga * ya + gb * yb
        r1_ref[...] = ALPHA * x_ref[...] + _dot(z.astype(BF), wo_ref[...])

    tile = lambda w: pl.BlockSpec((tm, w), lambda i: (i, 0))
    return _pc(
        body, name="mix_fwd", grid=(nt,),
        in_specs=[tile(N_PROJ), *_edge_specs(nt, N_PROJ), tile(D),
                  _resident((W_A, D)), _resident((W_B, D)), _resident((D, D)), _resident((N_HEAD, CHUNK, CHUNK)),
                  _resident((CHUNK, W_B)), _resident((1, 2 * D)), _resident((3, W_A)), _resident((1, W_B)),
                  _resident((1, W_B))],
        out_specs=[tile(D), tile(D), tile(D)],
        out_shape=[jax.ShapeDtypeStruct((t, D), F32)] * 3,
        scratch_shapes=[pltpu.VMEM((tm, W_B), BF), pltpu.VMEM((tm, W_B), F32)],
        compiler_params=_params("arbitrary"),
    )(p, pedge, pedge, x, wpa, wpb, wo, wsb, bsf, bg, cw, vg, vb)


def _ffn_fwd_bwd(r1, tgt, wff1, wff2, ln1g, ln1b, ln2g, ln2b, tm):
    t = r1.shape[0]

    def body(r1_ref, tgt_ref, w1_ref, w2_ref, g1_ref, b1_ref, g2_ref, b2_ref,
             dr1_ref, dr1b_ref, dedge_ref, x1b_ref, hidb_ref, dh1b_ref, dr2b_ref, acc_ref, relu_ref):
        @pl.when(pl.program_id(0) == 0)
        def _():
            acc_ref[...] = jnp.zeros_like(acc_ref)

        xh1, rstd1 = _ln_stats(r1_ref[...])
        x1 = xh1 * g1_ref[...] + b1_ref[...]
        x1b_ref[...] = x1.astype(BF)
        ffn = jnp.zeros((tm, D), F32)
        for j in range(N_CHIP):
            cols = slice(j * FF_SHARD, (j + 1) * FF_SHARD)
            r = jnp.maximum(_dot(x1b_ref[...], w1_ref[j]), 0.0)
            relu_ref[:, cols] = r
            hidb_ref[:, cols] = (r * r).astype(BF)
            ffn = ffn + _dot(hidb_ref[:, cols], w2_ref[cols, :])
        xh2, rstd2 = _ln_stats(ALPHA * x1 + ffn)
        diff = xh2 * g2_ref[...] + b2_ref[...] - tgt_ref[...]
        acc_ref[4:5, :] += _colsum(diff * diff)
        dx2 = diff * (1.0 / D)
        acc_ref[2:3, :] += _colsum(dx2 * xh2)
        acc_ref[3:4, :] += _colsum(dx2)
        dr2 = _ln_bwd(dx2, g2_ref[...], xh2, rstd2)
        dr2b_ref[...] = dr2.astype(BF)
        dx1 = ALPHA * dr2
        for j in range(N_CHIP):
            cols = slice(j * FF_SHARD, (j + 1) * FF_SHARD)
            dhid = _dot_nt(dr2b_ref[...], w2_ref[cols, :])
            dh1b_ref[:, cols] = (dhid * (2.0 * relu_ref[:, cols])).astype(BF)
            dx1 = dx1 + _dot_nt(dh1b_ref[:, cols], w1_ref[j])
        acc_ref[0:1, :] += _colsum(dx1 * xh1)
        acc_ref[1:2, :] += _colsum(dx1)
        dr1 = _ln_bwd(dx1, g1_ref[...], xh1, rstd1)
        dr1_ref[...] = dr1
        dr1b_ref[...] = dr1.astype(BF)
        dedge_ref[0:HALO, :] = dr1_ref[0:HALO, :]
        dedge_ref[HALO:2 * HALO, :] = dr1_ref[tm - HALO:tm, :]

    tile = lambda w: pl.BlockSpec((tm, w), lambda i: (i, 0))
    vec = _resident((1, D))
    return _pc(
        body, name="ffn_fwd_bwd", grid=(t // tm,),
        in_specs=[tile(D), tile(D), _resident((N_CHIP, D, FF_SHARD)), _resident((D_FF, D)), vec, vec, vec, vec],
        out_specs=[tile(D), tile(D), pl.BlockSpec((None, 2 * HALO, D), lambda i: (i, 0, 0)), tile(D), tile(D_FF),
                   tile(D_FF), tile(D), pl.BlockSpec((8, D), lambda i: (0, 0))],
        out_shape=[jax.ShapeDtypeStruct((t, D), F32), jax.ShapeDtypeStruct((t, D), BF),
                   jax.ShapeDtypeStruct((t // tm, 2 * HALO, D), F32), jax.ShapeDtypeStruct((t, D), BF),
                   jax.ShapeDtypeStruct((t, D_FF), BF), jax.ShapeDtypeStruct((t, D_FF), BF),
                   jax.ShapeDtypeStruct((t, D), BF), jax.ShapeDtypeStruct((8, D), F32)],
        scratch_shapes=[pltpu.VMEM((tm, D_FF), F32)],
        compiler_params=_params("arbitrary"),
    )(r1, tgt, wff1, wff2, ln1g, ln1b, ln2g, ln2b)


def _dw(a, b, nblk, am, bn, a_blocked, b_blocked, tk, name, comm=None):
    t = a.shape[0]

    def body(a_ref, b_ref, o_ref):
        @pl.when(pl.program_id(1) == 0)
        def _():
            o_ref[...] = jnp.zeros_like(o_ref)

        o_ref[...] += _dot_tn(a_ref[...], b_ref[...])

    outs = _host_call(
        body, comm, name=name, grid=(nblk, t // tk),
        in_specs=[pl.BlockSpec((tk, am), (lambda j, k: (k, j)) if a_blocked else (lambda j, k: (k, 0))),
                  pl.BlockSpec((tk, bn), (lambda j, k: (k, j)) if b_blocked else (lambda j, k: (k, 0)))],
        out_specs=[pl.BlockSpec((None, am, bn), lambda j, k: (j, 0, 0))],
        out_shape=[jax.ShapeDtypeStruct((nblk, am, bn), F32)], args=(a, b))
    return outs[0] if comm is None else (outs[0][0], outs[1])


def _dx(dp, win4, dr1, tm, blk0, nblk, filled, name, comm):
    t = dp.shape[0]

    def body(dp_ref, w_ref, dr1_ref, *rest):
        dx_ref = rest[-1]

        @pl.when(pl.program_id(1) == 0)
        def _():
            dx_ref[...] = ALPHA * dr1_ref[...]

        dx_ref[...] += _dot_nt(dp_ref[...], w_ref[...])

    in_specs = [pl.BlockSpec((tm, NP_SHARD), lambda i, j: (i + blk0, j)),
                pl.BlockSpec((None, D, NP_SHARD), lambda i, j: (j, 0, 0)),
                pl.BlockSpec((tm, D), lambda i, j: (i + blk0, 0))]
    args = (dp, win4, dr1)
    aliases = None
    if filled is not None:
        in_specs.append(pl.BlockSpec(memory_space=pl.ANY))
        args += (filled,)
        aliases = {3: 0}
    outs, got = _host_call(
        body, comm, name=name, grid=(nblk, N_CHIP), in_specs=in_specs,
        out_specs=[pl.BlockSpec((tm, D), lambda i, j: (i + blk0, 0))],
        out_shape=[jax.ShapeDtypeStruct((t, D), F32)], args=args, aliases=aliases)
    return outs[0], got


def _mix_bwd(p, pedge, dr1, dedge, ya, yb, wpa, wpb, wo, wsb, wstb, bsf, bg, cw, vg, vb, tm, comm):
    t = p.shape[0]
    nt = t // tm
    te = tm + 2 * HALO
    mid = slice(HALO, HALO + tm)

    def body(p_ref, prev_ref, next_ref, dr1_ref, dprev_ref, dnext_ref, ya_ref, yb_ref, wpa_ref, wpb_ref, wo_ref,
             ws_ref, wst_ref, bsf_ref, bg_ref, cw_ref, vg_ref, vb_ref,
             dp_ref, ab_ref, bbb_ref, zb_ref, dyab_ref, dybb_ref, dbg_ref, dcw_ref, dvgb_ref, dws_ref, dbs_ref,
             vnb_ref, mixed_ref, dmixb_ref, dvn_ref):
        @pl.when(pl.program_id(0) == 0)
        def _():
            for r in (dbg_ref, dcw_ref, dvgb_ref, dws_ref, dbs_ref):
                r[...] = jnp.zeros_like(r)

        has_prev, has_next = _end_masks(nt)
        ca, ha, ch, ch_m1, ch_p1, cv = _conv_fwd(p_ref, prev_ref, next_ref, cw_ref, tm, has_prev, has_next)
        ba = p_ref[:, 0:OFF_CA]
        ab_ref[...] = (ba * cv).astype(BF)
        vb_pre, tv, xhv, rstdv = _spatial_fwd(p_ref, vg_ref, vb_ref, ws_ref, bsf_ref, vnb_ref, mixed_ref, tm)
        ub = p_ref[:, OFF_UB:OFF_VB]
        gu, tu = _gelu(ub)
        bbb_ref[...] = (gu * mixed_ref[...]).astype(BF)
        bga = bg_ref[:, 0:D]
        ga = jax.nn.sigmoid(p_ref[:, OFF_GA:OFF_GB] + bga)
        gb = jax.nn.sigmoid(p_ref[:, OFF_GB:N_PROJ] + bg_ref[:, D:2 * D])
        ya = ya_ref[...]
        yb = yb_ref[...]
        zb_ref[...] = (ga * ya + gb * yb).astype(BF)

        dr1_ext = jnp.concatenate([dprev_ref[...] * has_prev, dr1_ref[...], dnext_ref[...] * has_next], axis=0)
        dz_ext = _dot_nt(dr1_ext.astype(BF), wo_ref[...])
        ga_ext = jnp.concatenate([jax.nn.sigmoid(prev_ref[:, OFF_GA:OFF_GB] + bga), ga,
                                  jax.nn.sigmoid(next_ref[:, OFF_GA:OFF_GB] + bga)], axis=0)
        dya_ext = dz_ext * ga_ext
        dyab_ref[...] = dya_ext[mid].astype(BF)
        da_ext = _dot_nt(dya_ext.astype(BF), wpa_ref[...])
        ba_ext = jnp.concatenate([prev_ref[:, 0:OFF_CA], ba, next_ref[:, 0:OFF_CA]], axis=0)
        dcv_ext = da_ext * ba_ext
        dcv = dcv_ext[mid]
        dch = (cw_ref[0:1, :] * pltpu.roll(dcv_ext, te - 1, 0)[mid] + cw_ref[1:2, :] * dcv
               + cw_ref[2:3, :] * pltpu.roll(dcv_ext, 1, 0)[mid])
        dp_ref[:, 0:OFF_CA] = (da_ext[mid] * cv).astype(BF)
        dp_ref[:, OFF_CA:OFF_HA] = (dch * ha).astype(BF)
        dp_ref[:, OFF_HA:OFF_UB] = (dch * ca).astype(BF)
        dcw_ref[0:1, :] += _colsum(dcv * ch_m1)
        dcw_ref[1:2, :] += _colsum(dcv * ch)
        dcw_ref[2:3, :] += _colsum(dcv * ch_p1)

        dz = dz_ext[mid]
        dga = dz * ya * ga * (1.0 - ga)
        dgb = dz * yb * gb * (1.0 - gb)
        dp_ref[:, OFF_GA:OFF_GB] = dga.astype(BF)
        dp_ref[:, OFF_GB:N_PROJ] = dgb.astype(BF)
        dbg_ref[0:1, 0:D] += _colsum(dga)
        dbg_ref[0:1, D:2 * D] += _colsum(dgb)

        dybb_ref[...] = (dz * gb).astype(BF)
        dbb = _dot_nt(dybb_ref[...], wpb_ref[...])
        dp_ref[:, OFF_UB:OFF_VB] = (dbb * mixed_ref[...] * _gelu_grad(ub, tu)).astype(BF)
        dmixed = dbb * gu
        dmixb_ref[...] = dmixed.astype(BF)
        for c in range(tm // CHUNK):
            rows = slice(c * CHUNK, (c + 1) * CHUNK)
            dbs_ref[...] += dmixed[rows]
            for h in range(N_HEAD):
                cols = slice(h * CHUNK, (h + 1) * CHUNK)
                dws_ref[h] += _dot_nt(dmixb_ref[rows, cols], vnb_ref[rows, cols])
                dvn_ref[rows, cols] = _dot(wst_ref[h], dmixb_ref[rows, cols])
        dvn = dvn_ref[...]
        dvgb_ref[0:1, :] += _colsum(dvn * xhv)
        dvgb_ref[1:2, :] += _colsum(dvn)
        dgv = _ln_bwd(dvn, vg_ref[...], xhv, rstdv)
        dp_ref[:, OFF_VB:OFF_GA] = (dgv * _gelu_grad(vb_pre, tv)).astype(BF)

    tile = lambda w: pl.BlockSpec((tm, w), lambda i: (i, 0))
    acc = lambda *s: pl.BlockSpec(s, lambda i: (0,) * len(s))
    return _host_call(
        body, comm, name="mix_bwd", grid=(nt,),
        in_specs=[tile(N_PROJ), *_edge_specs(nt, N_PROJ), tile(D), *_edge_specs(nt, D), tile(D), tile(D),
                  _resident((W_A, D)), _resident((W_B, D)), _resident((D, D)), _resident((N_HEAD, CHUNK, CHUNK)),
                  _resident((N_HEAD, CHUNK, CHUNK)), _resident((CHUNK, W_B)), _resident((1, 2 * D)),
                  _resident((3, W_A)), _resident((1, W_B)), _resident((1, W_B))],
        out_specs=[tile(N_PROJ), tile(W_A), tile(W_B), tile(D), tile(D), tile(D),
                   acc(8, 2 * D), acc(8, W_A), acc(8, W_B), acc(N_HEAD, CHUNK, CHUNK), acc(CHUNK, W_B)],
        out_shape=[jax.ShapeDtypeStruct((t, N_PROJ), BF), jax.ShapeDtypeStruct((t, W_A), BF),
                   jax.ShapeDtypeStruct((t, W_B), BF), jax.ShapeDtypeStruct((t, D), BF), jax.ShapeDtypeStruct((t, D), BF),
                   jax.ShapeDtypeStruct((t, D), BF), jax.ShapeDtypeStruct((8, 2 * D), F32),
                   jax.ShapeDtypeStruct((8, W_A), F32), jax.ShapeDtypeStruct((8, W_B), F32),
                   jax.ShapeDtypeStruct((N_HEAD, CHUNK, CHUNK), F32), jax.ShapeDtypeStruct((CHUNK, W_B), F32)],
        scratch_shapes=[pltpu.VMEM((tm, W_B), BF), pltpu.VMEM((tm, W_B), F32), pltpu.VMEM((tm, W_B), BF),
                        pltpu.VMEM((tm, W_B), F32)],
        args=(p, pedge, pedge, dr1, dedge, dedge, ya, yb, wpa, wpb, wo, wsb, wstb, bsf, bg, cw, vg, vb))


def _add_own_half(full4, recv4, core, rb, name):
    n, rh, cols = recv4.shape
    nb = rh // rb

    def body(c_ref, a_ref, b_ref, o_ref, ob_ref):
        s = a_ref[...] + b_ref[...]
        o_ref[...] = s
        ob_ref[...] = s.astype(BF)

    blk = (None, rb, cols)
    same = pl.BlockSpec(blk, lambda k, i, c: (k, i, 0))
    return _pc(
        body, name=name,
        grid_spec=pltpu.PrefetchScalarGridSpec(
            num_scalar_prefetch=1, grid=(n, nb),
            in_specs=[pl.BlockSpec(blk, lambda k, i, c: (k, c[0] * nb + i, 0)), same], out_specs=[same, same]),
        out_shape=[jax.ShapeDtypeStruct(recv4.shape, F32), jax.ShapeDtypeStruct(recv4.shape, BF)],
        compiler_params=_params("arbitrary", "arbitrary"),
    )(core, full4, recv4)


def _add_chips(s4, r3, place, rb, name):
    _, rh, cols = r3.shape
    nb = rh // rb

    def body(pl_ref, s_ref, r_ref, o_ref):
        o_ref[...] = ((s_ref[...] + r_ref[0].astype(F32)) + r_ref[1].astype(F32)) + r_ref[2].astype(F32)

    return _pc(
        body, name=name,
        grid_spec=pltpu.PrefetchScalarGridSpec(
            num_scalar_prefetch=1, grid=(nb,),
            in_specs=[pl.BlockSpec((None, rb, cols), lambda i, s: (s[0], i, 0)), pl.BlockSpec((3, rb, cols), lambda i, s: (0, i, 0))],
            out_specs=pl.BlockSpec((rb, cols), lambda i, s: (s[1] * nb + i, 0))),
        out_shape=jax.ShapeDtypeStruct((2 * rh, cols), F32),
        compiler_params=_params("arbitrary"),
    )(place, s4, r3)


def _add_small(a, b):
    def body(a_ref, b_ref, o_ref):
        o_ref[...] = a_ref[...] + b_ref[...]

    return _pc(body, name="add_small_cores", out_shape=jax.ShapeDtypeStruct(a.shape, F32))(a, b)


def _sum_small_chips(own, slots, place):
    def body(pl_ref, own_ref, s_ref, o_ref):
        j = pl_ref[0]

        def term(k):
            return jnp.where(j == k, own_ref[...], s_ref[jnp.maximum((j ^ k) - 1, 0)])

        o_ref[...] = ((term(0) + term(1)) + term(2)) + term(3)

    vmem = pl.BlockSpec(memory_space=pltpu.VMEM)
    return _pc(body, name="sum_small_chips", in_specs=[pl.BlockSpec(memory_space=pltpu.SMEM), vmem, vmem], out_specs=vmem,
               out_shape=jax.ShapeDtypeStruct(own.shape, F32))(place, own, slots)


def _adamw(w, g, m, v, rb, name):
    rows, cols = w.shape

    def body(w_ref, g_ref, m_ref, v_ref, d_ref, m2_ref, v2_ref):
        g_ = g_ref[...]
        m2 = ADAM_B1 * m_ref[...] + (1.0 - ADAM_B1) * g_
        v2 = ADAM_B2 * v_ref[...] + (1.0 - ADAM_B2) * (g_ * g_)
        m_hat = m2 / (1.0 - ADAM_B1 ** ADAM_STEP)
        v_hat = v2 / (1.0 - ADAM_B2 ** ADAM_STEP)
        d_ref[...] = -ADAM_LR * (m_hat / (jnp.sqrt(v_hat) + ADAM_EPS) + ADAM_WD * w_ref[...])
        m2_ref[...] = m2
        v2_ref[...] = v2

    blk = pl.BlockSpec((rb, cols), lambda i: (i, 0))
    return _pc(body, name=name, grid=(rows // rb,), in_specs=[blk] * 4, out_specs=[blk] * 3,
               out_shape=[jax.ShapeDtypeStruct((rows, cols), F32)] * 3, compiler_params=_params("arbitrary"))(w, g, m, v)


LANES = 128
SMALL_GRADS = (("b_gate", 2 * D), ("conv_w", 3 * W_A), ("v_norm_g", W_B), ("v_norm_b", W_B),
               ("w_s", N_HEAD * CHUNK * CHUNK), ("b_s", N_HEAD * CHUNK), ("ln1_g", D), ("ln1_b", D), ("ln2_g", D), ("ln2_b", D),
               ("loss", 1))


def _pack_rows(parts):
    rows = []
    for a in parts:
        a = a.reshape(-1)
        a = jnp.pad(a, (0, (-a.shape[0]) % LANES))
        rows.append(a.reshape(-1, LANES))
    out = jnp.concatenate(rows, axis=0)
    return jnp.pad(out, ((0, (-out.shape[0]) % 8), (0, 0)))


def _unpack_rows(buf, sizes):
    out, r = [], 0
    for n in sizes:
        nr = -(-n // LANES)
        out.append(buf[r:r + nr].reshape(-1)[:n])
        r += nr
    return out


TM_PROJ = 1024
TM_MIX = 256
TM_DX = 1024
DX_FIRST = 4
TK_DW = 2048
TK_DW_IN = 1024
RB_ADD = 64
RB_ADAM = 128
CONV_ROWS = 8


def _reduce_adds_1(grads, recvs, core, tag):
    out = [_add_own_half(g, r, core, RB_ADD if r.shape[1] % 256 else 256, f"add_cores_{tag}{a}")
           for a, (g, r) in enumerate(zip(grads, recvs))]
    return [o[0] for o in out], [o[1] for o in out]


def _reduce_adds_2(sums, recvs, place, tag):
    return [_add_chips(s, r, place, RB_ADD if r.shape[1] % 256 else 256, f"add_chips_{tag}{a}")
            for a, (s, r) in enumerate(zip(sums, recvs))]


def kernel(x, w_in, b_gate, conv_w, v_norm_g, v_norm_b, w_s, b_s, w_pa, w_pb, w_o, ln1_g, ln1_b, w_ff1, w_ff2, ln2_g, ln2_b, loss_target, m_w_in, m_b_gate, m_conv_w, m_v_norm_g, m_v_norm_b, m_w_s, m_b_s, m_w_pa, m_w_pb, m_w_o, m_ln1_g, m_ln1_b, m_w_ff1, m_w_ff2, m_ln2_g, m_ln2_b, v_w_in, v_b_gate, v_conv_w, v_v_norm_g, v_v_norm_b, v_w_s, v_b_s, v_w_pa, v_w_pb, v_w_o, v_ln1_g, v_ln1_b, v_w_ff1, v_w_ff2, v_ln2_g, v_ln2_b):
    t = x.shape[1]
    core = lax.axis_index("c").astype(jnp.int32).reshape(1)
    chip_idx = 2 * lax.axis_index("x") + lax.axis_index("y")
    chip = chip_idx.astype(jnp.int32).reshape(1)
    place = jnp.concatenate([chip, core])
    x2 = x.reshape(t, D)
    tgt = loss_target.reshape(t, D)

    win4, proj4, ff14, ff24 = _cast_shards(w_in[0], w_pa[0], w_pb[0], w_o[0], w_ff1[0], w_ff2[0], chip)
    conv4 = lax.dynamic_update_slice(jnp.zeros((N_CHIP, CONV_ROWS, W_A // N_CHIP), F32),
                                     jnp.pad(conv_w[0], ((0, CONV_ROWS - 3), (0, 0)))[None], (chip_idx, 0, 0))
    win4, conv4 = _comm_call(_gather_comm([win4], conv4), "gather_w_in")
    (p, xb, pedge), (proj4, ff14, ff24) = _proj_fwd(x2, win4, TM_PROJ, TM_MIX, _gather_comm([proj4, ff14, ff24]))

    def full(name, rows_total):
        off, rows = PROJ_OFF[name]
        return proj4[:, off:off + rows, :].reshape(rows_total, D)

    wpa, wpb, wo = full("w_pa", W_A), full("w_pb", W_B), full("w_o", D)
    wff2 = ff24.reshape(D_FF, D)
    cw = jnp.transpose(conv4[:, :3, :], (1, 0, 2)).reshape(3, W_A)
    wsb = w_s[0].astype(BF)
    wstb = jnp.swapaxes(w_s[0], 1, 2).astype(BF)
    bsf = jnp.repeat(jnp.transpose(b_s[0]), CHUNK, axis=1)

    r1, ya, yb = _mix_fwd(p, pedge, x2, wpa, wpb, wo, wsb, bsf, b_gate, cw, v_norm_g, v_norm_b, TM_MIX)
    dr1, dr1b, dedge, x1b, hidb, dh1b, dr2b, acc = _ffn_fwd_bwd(r1, tgt, ff14, wff2, ln1_g, ln1_b, ln2_g, ln2_b, TM_MIX)
    g_ff = [_dw(x1b, dh1b, N_CHIP, D, FF_SHARD, False, True, TK_DW, "dw_ff1"),
            _dw(hidb, dr2b, N_CHIP, FF_SHARD, D, True, False, TK_DW, "dw_ff2")]
    (dp, ab, bbb, zb, dyab, dybb, dbg, dcw, dvgb, dws, dbs_sum), r_ff = _mix_bwd(
        p, pedge, dr1, dedge, ya, yb, wpa, wpb, wo, wsb, wstb, bsf, b_gate, cw, v_norm_g, v_norm_b, TM_MIX, _pair_comm(g_ff))
    s_ff, sb_ff = _reduce_adds_1(g_ff, r_ff, core, "ff")
    dwin4, c_ff = _dw(xb, dp, N_CHIP, D, NP_SHARD, False, True, TK_DW_IN, "dw_in", _chips_comm(sb_ff))
    f_ff = _reduce_adds_2(s_ff, c_ff, place, "ff")
    dwpa, (g_ff1, g_ff2) = _dw(ab, dyab, 1, W_A, D, False, False, TK_DW, "dw_pa", _join_comm(f_ff))
    dwpb = _dw(bbb, dybb, 1, W_B, D, False, False, TK_DW, "dw_pb")
    dwo = _dw(zb, dr1b, 1, D, D, False, False, TK_DW, "dw_o")
    dproj4 = jnp.concatenate([dwpa.reshape(N_CHIP, -1, D), dwpb.reshape(N_CHIP, -1, D), dwo.reshape(N_CHIP, -1, D)], axis=1)
    dbs = jnp.transpose(jnp.sum(dbs_sum.reshape(CHUNK, N_HEAD, CHUNK), axis=-1))
    small = _pack_rows([dbg[0], dcw[0:3], dvgb[0], dvgb[1], dws, dbs, acc[0], acc[1], acc[2], acc[3],
                        0.5 * jnp.sum(acc[4]) / D])
    g_rest = [dwin4, dproj4]
    nblk = t // TM_DX
    first = min(DX_FIRST, nblk - 1)
    dx, r_rest = _dx(dp, win4, dr1, TM_DX, 0, first, None, "dx_a", _pair_comm(g_rest, small))
    s_rest, sb_rest = _reduce_adds_1(g_rest, r_rest[:2], core, "rest")
    csmall = _add_small(small, r_rest[2])
    dx, c_rest = _dx(dp, win4, dr1, TM_DX, first, nblk - first, dx, "dx_b", _chips_comm(sb_rest, csmall))
    f_rest = _reduce_adds_2(s_rest, c_rest[:2], place, "rest")
    gsmall = _sum_small_chips(csmall, c_rest[2], place)
    g_in, g_proj = _comm_call(_join_comm(f_rest), "join_rest")

    grads = {"w_in": g_in, "w_ff1": g_ff1, "w_ff2": g_ff2}
    for name, _ in PROJ_ROWS:
        off, rows = PROJ_OFF[name]
        grads[name] = g_proj[off:off + rows, :]
    for (name, n), flat in zip(SMALL_GRADS, _unpack_rows(gsmall, [n for _, n in SMALL_GRADS])):
        grads[name] = flat
    loss = grads.pop("loss").reshape(())
    grads["conv_w"] = lax.dynamic_slice(grads["conv_w"].reshape(3, W_A), (0, chip_idx * (W_A // N_CHIP)), (3, W_A // N_CHIP))

    weights = dict(w_in=w_in, b_gate=b_gate, conv_w=conv_w, v_norm_g=v_norm_g, v_norm_b=v_norm_b, w_s=w_s, b_s=b_s,
                   w_pa=w_pa, w_pb=w_pb, w_o=w_o, ln1_g=ln1_g, ln1_b=ln1_b, w_ff1=w_ff1, w_ff2=w_ff2, ln2_g=ln2_g, ln2_b=ln2_b)
    mom1 = dict(w_in=m_w_in, b_gate=m_b_gate, conv_w=m_conv_w, v_norm_g=m_v_norm_g, v_norm_b=m_v_norm_b, w_s=m_w_s,
                b_s=m_b_s, w_pa=m_w_pa, w_pb=m_w_pb, w_o=m_w_o, ln1_g=m_ln1_g, ln1_b=m_ln1_b, w_ff1=m_w_ff1,
                w_ff2=m_w_ff2, ln2_g=m_ln2_g, ln2_b=m_ln2_b)
    mom2 = dict(w_in=v_w_in, b_gate=v_b_gate, conv_w=v_conv_w, v_norm_g=v_v_norm_g, v_norm_b=v_v_norm_b, w_s=v_w_s,
                b_s=v_b_s, w_pa=v_w_pa, w_pb=v_w_pb, w_o=v_w_o, ln1_g=v_ln1_g, ln1_b=v_ln1_b, w_ff1=v_w_ff1,
                w_ff2=v_w_ff2, ln2_g=v_ln2_g, ln2_b=v_ln2_b)
    order = list(weights)
    big = ("w_in", "w_pa", "w_pb", "w_o", "w_ff1", "w_ff2")
    delta, new_m, new_v = {}, {}, {}
    for name in big:
        w2 = weights[name][0]
        delta[name], new_m[name], new_v[name] = _adamw(w2, grads[name], mom1[name][0], mom2[name][0], RB_ADAM, "adamw_" + name)
    little = [n for n in order if n not in big]
    sizes = [weights[n].size for n in little]
    wsmall = _pack_rows([weights[n] for n in little])
    ds, ms, vs = _adamw(wsmall, _pack_rows([grads[n] for n in little]), _pack_rows([mom1[n] for n in little]),
                        _pack_rows([mom2[n] for n in little]), wsmall.shape[0], "adamw_small")
    for name, d_, m_, v_ in zip(little, _unpack_rows(ds, sizes), _unpack_rows(ms, sizes), _unpack_rows(vs, sizes)):
        delta[name], new_m[name], new_v[name] = d_, m_, v_

    shaped = lambda d: [d[n].reshape(weights[n].shape) for n in order]
    return (loss, dx.reshape(x.shape), *shaped(grads), *shaped(delta), *shaped(new_m), *shaped(new_v))
```

```python
import functools
from typing import NamedTuple

import jax
import jax.numpy as jnp
from jax import lax
from jax.experimental import pallas as pl
from jax.experimental.pallas import tpu as pltpu

D = 1024
W_A = 1536
W_B = 1024
CHUNK = 128
N_HEAD = 8
D_FF = 4096
N_PROJ = 3 * W_A + 2 * W_B + 2 * D
OFF_CA, OFF_HA, OFF_UB, OFF_VB, OFF_GA, OFF_GB = 1536, 3072, 4608, 5632, 6656, 7680
LN_EPS = 1e-5
ALPHA = 2.0 ** 0.25
N_CHIP = 4
NP_SHARD = N_PROJ // N_CHIP
FF_SHARD = D_FF // N_CHIP
ADAM_LR, ADAM_B1, ADAM_B2, ADAM_EPS, ADAM_WD, ADAM_STEP = 0.001, 0.9, 0.999, 1e-08, 0.01, 10

PROJ_ROWS = (("w_pa", W_A // N_CHIP), ("w_pb", W_B // N_CHIP), ("w_o", D // N_CHIP))
PROJ_OFF = {}
_o = 0
for _n, _r in PROJ_ROWS:
    PROJ_OFF[_n] = (_o, _r)
    _o += _r
PROJ_TOTAL = _o

V7X_VMEM_BYTES = 64 * 1024 * 1024
VMEM_LIMIT = 56 * 1024 * 1024
HALO = 8

BF = jnp.bfloat16
F32 = jnp.float32
MESH = pl.DeviceIdType.MESH
HBM_SPEC = pl.BlockSpec(memory_space=pltpu.HBM)


def _pc(body, **kw):
    return pl.pallas_call(body, **kw)


def _params(*sem):
    return pltpu.CompilerParams(dimension_semantics=sem, vmem_limit_bytes=VMEM_LIMIT)


def _resident(shape):
    n = len(shape)
    return pl.BlockSpec(shape, lambda *_: (0,) * n, pipeline_mode=pl.Buffered(1))


def _dot(a, b):
    return jnp.dot(a, b, preferred_element_type=F32)


def _dot_nt(a, b):
    return lax.dot_general(a, b, (((1,), (1,)), ((), ())), preferred_element_type=F32)


def _dot_tn(a, b):
    return lax.dot_general(a, b, (((0,), (0,)), ((), ())), preferred_element_type=F32)


def _gelu(x):
    t = jnp.tanh(0.7978845608028654 * (x + 0.044715 * (x * x * x)))
    return 0.5 * x * (1.0 + t), t


def _gelu_grad(x, t):
    return 0.5 * (1.0 + t) + 0.5 * x * (1.0 - t * t) * (0.7978845608028654 * (1.0 + 0.134145 * (x * x)))


def _ln_stats(r):
    mu = jnp.mean(r, axis=-1, keepdims=True)
    xc = r - mu
    var = jnp.mean(xc * xc, axis=-1, keepdims=True)
    rstd = lax.rsqrt(var + LN_EPS)
    return xc * rstd, rstd


def _ln_bwd(dy, g, xh, rstd):
    dxh = dy * g
    m1 = jnp.mean(dxh, axis=-1, keepdims=True)
    m2 = jnp.mean(dxh * xh, axis=-1, keepdims=True)
    return rstd * (dxh - m1 - xh * m2)


def _colsum(v):
    return jnp.sum(v, axis=0, keepdims=True)


class _Comm(NamedTuple):
    args: tuple
    out_shape: tuple
    aliases: dict
    n_sems: int
    stages: tuple


def _place():
    x, y, c = lax.axis_index("x"), lax.axis_index("y"), lax.axis_index("c")
    return x, y, c, 2 * x + y


def _flip(x, y, c, r):
    return (x ^ (r >> 1), y ^ (r & 1), c)


def _remote(src, dst, send_sems, recv_sems, k, peer):
    return pltpu.make_async_remote_copy(src_ref=src, dst_ref=dst, send_sem=send_sems.at[k], recv_sem=recv_sems.at[k],
                                        device_id=peer, device_id_type=MESH)


def _host_call(body, comm, *, name, grid, in_specs, out_specs, out_shape, args, scratch_shapes=(), aliases=None, prefetch=None,
               body_reads_comm=False):
    sem = ("arbitrary",) * len(grid)
    aliases = dict(aliases or {})
    n_pre = 0 if prefetch is None else 1
    n_in, n_out, n_scr = len(in_specs), len(out_specs), len(scratch_shapes)
    c_in, c_out = (0, 0) if comm is None else (len(comm.args), len(comm.out_shape))
    steps = {"first": (0,) * len(grid), "late": (grid[0] - 1,) + (0,) * (len(grid) - 1), "last": tuple(g - 1 for g in grid)}

    def wrapped(*refs):
        refs = refs[n_pre:]
        own_in, cin = refs[:n_in], refs[n_in:n_in + c_in]
        o0 = n_in + c_in
        own_out, cout = refs[o0:o0 + n_out], refs[o0 + n_out:o0 + n_out + c_out]
        s0 = o0 + n_out + c_out
        scr, sems = refs[s0:s0 + n_scr], refs[s0 + n_scr:]

        def run(before):
            for phase, fn in () if comm is None else comm.stages:
                at_step = isinstance(phase, tuple)
                if before != (at_step or phase == "first"):
                    continue
                step = phase if at_step else steps[phase]
                cond = pl.program_id(0) == step[0]
                for d in range(1, len(grid)):
                    cond = jnp.logical_and(cond, pl.program_id(d) == step[d])
                pl.when(cond)(functools.partial(fn, cin, cout, *sems))

        run(True)
        if body_reads_comm:
            body(*own_in, *own_out, *scr, comm_refs=cout)
        else:
            body(*own_in, *own_out, *scr)
        run(False)

    in_specs = list(in_specs) + [HBM_SPEC] * c_in
    out_specs = list(out_specs) + [HBM_SPEC] * c_out
    out_shape = list(out_shape) + ([] if comm is None else list(comm.out_shape))
    scratch_shapes = list(scratch_shapes) + ([] if comm is None else [pltpu.SemaphoreType.DMA((comm.n_sems,))] * 2)
    args = tuple(args) + (() if comm is None else tuple(comm.args))
    if comm is not None:
        aliases.update({n_in + i: n_out + o for i, o in comm.aliases.items()})
    aliases = {i + n_pre: o for i, o in aliases.items()}
    if prefetch is None:
        kw = dict(grid=grid, in_specs=in_specs, out_specs=out_specs, scratch_shapes=scratch_shapes)
    else:
        kw = dict(grid_spec=pltpu.PrefetchScalarGridSpec(num_scalar_prefetch=1, grid=grid, in_specs=in_specs,
                                                         out_specs=out_specs, scratch_shapes=scratch_shapes))
        args = (prefetch,) + args
    outs = _pc(wrapped, name=name, out_shape=out_shape, input_output_aliases=aliases, compiler_params=_params(*sem), **kw)(*args)
    return outs[:n_out], outs[n_out:]


def _comm_call(comm, name):
    def body(*refs):
        c_in, c_out = len(comm.args), len(comm.out_shape)
        cin, cout, (send_sems, recv_sems) = refs[:c_in], refs[c_in:c_in + c_out], refs[c_in + c_out:]
        for phase in ("first", "late", "last"):
            for ph, fn in comm.stages:
                if ph == phase:
                    fn(cin, cout, send_sems, recv_sems)

    return _pc(body, name=name, in_specs=[HBM_SPEC] * len(comm.args), out_specs=[HBM_SPEC] * len(comm.out_shape),
               out_shape=list(comm.out_shape), scratch_shapes=[pltpu.SemaphoreType.DMA((comm.n_sems,))] * 2,
               input_output_aliases=dict(comm.aliases))(*comm.args)


def _gather_comm(bufs, whole=None, eager=0):
    n = len(bufs)
    halves = [b.shape[1] // 2 for b in bufs]
    k_ici = lambda a, r: 3 * a + r - 1
    k_d2d = lambda a, r: 3 * n + 3 * a + r - 1
    k_whole = lambda r: 6 * n + r - 1

    def half(ref, slot, c, a):
        return ref.at[slot, pl.ds(c * halves[a], halves[a])]

    def send(cin, cout, ss, rs):
        x, y, c, j = _place()
        for a in range(n):
            mine = half(cout[a], j, c, a)
            for r in (1, 2, 3):
                _remote(mine, mine, ss, rs, k_ici(a, r), _flip(x, y, c, r)).start()
        if whole is not None:
            for r in (1, 2, 3):
                _remote(cout[n].at[j], cout[n].at[j], ss, rs, k_whole(r), _flip(x, y, c, r)).start()

    def pass_on(cout, ss, rs, a, r):
        x, y, c, j = _place()
        landed = half(cout[a], j ^ r, c, a)
        _remote(landed, landed, ss, rs, k_ici(a, r), (x, y, 1 - c)).wait_recv()
        _remote(landed, landed, ss, rs, k_d2d(a, r), (x, y, 1 - c)).start()

    def passed_on(cout, ss, rs, a, r):
        x, y, c, j = _place()
        theirs = half(cout[a], j ^ r, 1 - c, a)
        _remote(theirs, theirs, ss, rs, k_d2d(a, r), (x, y, 1 - c)).wait_recv()

    def arrive(r, cin, cout, ss, rs):
        for a in range(eager):
            pass_on(cout, ss, rs, a, r)
        for a in range(eager):
            passed_on(cout, ss, rs, a, r)

    def forward(cin, cout, ss, rs):
        for a in range(eager, n):
            for r in (1, 2, 3):
                pass_on(cout, ss, rs, a, r)

    def finish(cin, cout, ss, rs):
        x, y, c, j = _place()
        sibling = (x, y, 1 - c)
        for a in range(eager, n):
            for r in (1, 2, 3):
                passed_on(cout, ss, rs, a, r)
        for a in range(n):
            mine = half(cout[a], j, c, a)
            for r in (1, 2, 3):
                _remote(mine, mine, ss, rs, k_ici(a, r), sibling).wait_send()
                landed = half(cout[a], j ^ r, c, a)
                _remote(landed, landed, ss, rs, k_d2d(a, r), sibling).wait_send()
        if whole is not None:
            for r in (1, 2, 3):
                cp = _remote(cout[n].at[j ^ r], cout[n].at[j ^ r], ss, rs, k_whole(r), sibling)
                cp.wait_recv()
                cp.wait_send()

    args = tuple(bufs) + ((whole,) if whole is not None else ())
    out_shape = tuple(jax.ShapeDtypeStruct(b.shape, b.dtype) for b in args)
    aliases = {a: a for a in range(len(args))}
    arrivals = tuple(((r, 0), functools.partial(arrive, r)) for r in (1, 2, 3)) if eager else ()
    return _Comm(args, out_shape, aliases, 6 * n + 3, (("first", send),) + arrivals + (("late", forward), ("last", finish)))


def _pair_comm(grads, small=None):
    n = len(grads)
    halves = [g.shape[1] // 2 for g in grads]

    def copies(cin, cout, ss, rs):
        x, y, c, _ = _place()
        sibling = (x, y, 1 - c)
        cps = [_remote(cin[a].at[:, pl.ds((1 - c) * halves[a], halves[a]), :], cout[a], ss, rs, a, sibling) for a in range(n)]
        if small is not None:
            cps.append(_remote(cin[n], cout[n], ss, rs, n, sibling))
        return cps

    def start(cin, cout, ss, rs):
        for cp in copies(cin, cout, ss, rs):
            cp.start()

    def finish(cin, cout, ss, rs):
        for cp in copies(cin, cout, ss, rs):
            cp.wait()

    args = tuple(grads) + ((small,) if small is not None else ())
    out_shape = tuple(jax.ShapeDtypeStruct((N_CHIP, h, g.shape[2]), F32) for g, h in zip(grads, halves))
    out_shape += (jax.ShapeDtypeStruct(small.shape, F32),) if small is not None else ()
    return _Comm(args, out_shape, {}, n + 1, (("first", start), ("last", finish)))


def _chips_comm(sums_bf, small=None):
    n = len(sums_bf)

    def copies(cin, cout, ss, rs):
        x, y, c, j = _place()
        cps = []
        for r in (1, 2, 3):
            peer = _flip(x, y, c, r)
            for a in range(n):
                cps.append(_remote(cin[a].at[j ^ r], cout[a].at[r - 1], ss, rs, (n + 1) * (r - 1) + a, peer))
            if small is not None:
                cps.append(_remote(cin[n], cout[n].at[r - 1], ss, rs, (n + 1) * (r - 1) + n, peer))
        return cps

    def start(cin, cout, ss, rs):
        for cp in copies(cin, cout, ss, rs):
            cp.start()

    def finish(cin, cout, ss, rs):
        for cp in copies(cin, cout, ss, rs):
            cp.wait()

    args = tuple(sums_bf) + ((small,) if small is not None else ())
    out_shape = tuple(jax.ShapeDtypeStruct((3,) + s.shape[1:], BF) for s in sums_bf)
    out_shape += (jax.ShapeDtypeStruct((3,) + small.shape, F32),) if small is not None else ()
    return _Comm(args, out_shape, {}, 3 * (n + 1), (("first", start), ("last", finish)))


def _join_comm(shards):
    n = len(shards)
    halves = [s.shape[0] // 2 for s in shards]

    def start(cin, cout, ss, rs):
        x, y, c, _ = _place()
        for a in range(n):
            mine = cout[a].at[pl.ds(c * halves[a], halves[a]), :]
            _remote(mine, mine, ss, rs, a, (x, y, 1 - c)).start()

    def finish(cin, cout, ss, rs):
        x, y, c, _ = _place()
        for a in range(n):
            theirs = cout[a].at[pl.ds((1 - c) * halves[a], halves[a]), :]
            cp = _remote(theirs, theirs, ss, rs, a, (x, y, 1 - c))
            cp.wait_recv()
            cp.wait_send()

    out_shape = tuple(jax.ShapeDtypeStruct(s.shape, F32) for s in shards)
    return _Comm(tuple(shards), out_shape, {a: a for a in range(n)}, n, (("first", start), ("last", finish)))


def _cast_shards(w_in, w_pa, w_pb, w_o, w_ff1, w_ff2, chip):
    def body(j_ref, win_ref, wpa_ref, wpb_ref, wo_ref, wff1_ref, wff2_ref, win4_ref, proj4_ref, ff14_ref, ff24_ref):
        win4_ref[...] = win_ref[...].astype(BF)
        for name, ref in (("w_pa", wpa_ref), ("w_pb", wpb_ref), ("w_o", wo_ref)):
            off, rows = PROJ_OFF[name]
            proj4_ref[off:off + rows, :] = ref[...].astype(BF)
        ff14_ref[...] = wff1_ref[...].astype(BF)
        ff24_ref[...] = wff2_ref[...].astype(BF)

    whole = lambda a: pl.BlockSpec(a.shape, lambda i, j: (0, 0), pipeline_mode=pl.Buffered(1))
    slot = lambda rows, cols: pl.BlockSpec((None, rows, cols), lambda i, j: (j[0], 0, 0))
    ws = (w_in, w_pa, w_pb, w_o, w_ff1, w_ff2)
    shapes = ((D, NP_SHARD), (PROJ_TOTAL, D), (D, FF_SHARD), (FF_SHARD, D))
    return _pc(
        body, name="cast_shards",
        grid_spec=pltpu.PrefetchScalarGridSpec(num_scalar_prefetch=1, grid=(1,), in_specs=[whole(w) for w in ws],
                                               out_specs=[slot(*s) for s in shapes]),
        out_shape=[jax.ShapeDtypeStruct((N_CHIP,) + s, BF) for s in shapes],
        compiler_params=_params("arbitrary"))(chip, *ws)


def _proj_fwd(x, chip, tm, tmix, comm):
    t = x.shape[0]
    sub = tm // tmix

    def body(x_ref, p_ref, edge_ref, w_ref, w_sem, comm_refs):
        @pl.when(pl.program_id(1) == 0)
        def _():
            _, _, _, j = _place()
            block = pltpu.make_async_copy(comm_refs[0].at[j ^ pl.program_id(0)], w_ref, w_sem)
            block.start()
            block.wait()

        p_ref[...] = _dot(x_ref[...].astype(BF), w_ref[...])
        for s in range(sub):
            edge_ref[s, 0:HALO, :] = p_ref[s * tmix:s * tmix + HALO, :]
            edge_ref[s, HALO:2 * HALO, :] = p_ref[(s + 1) * tmix - HALO:(s + 1) * tmix, :]

    return _host_call(
        body, comm, name="proj_fwd", grid=(N_CHIP, t // tm), prefetch=chip, body_reads_comm=True,
        in_specs=[pl.BlockSpec((tm, D), lambda r, i, j: (i, 0))],
        out_specs=[pl.BlockSpec((tm, NP_SHARD), lambda r, i, j: (i, j[0] ^ r)),
                   pl.BlockSpec((sub, 2 * HALO, NP_SHARD), lambda r, i, j: (i, 0, j[0] ^ r))],
        out_shape=[jax.ShapeDtypeStruct((t, N_PROJ), F32), jax.ShapeDtypeStruct((t // tmix, 2 * HALO, N_PROJ), F32)],
        scratch_shapes=[pltpu.VMEM((D, NP_SHARD), BF), pltpu.SemaphoreType.DMA],
        args=(x,))


def _edge_specs(nt, w):
    return [pl.BlockSpec((None, HALO, w), lambda i: (jnp.maximum(i - 1, 0), 1, 0)),
            pl.BlockSpec((None, HALO, w), lambda i: (jnp.minimum(i + 1, nt - 1), 0, 0))]


def _end_masks(nt):
    i = pl.program_id(0)
    return (i > 0).astype(F32), (i < nt - 1).astype(F32)


def _conv_fwd(p_ref, prev_ref, next_ref, cw_ref, tm, has_prev, has_next):
    ca = p_ref[:, OFF_CA:OFF_HA]
    ha = p_ref[:, OFF_HA:OFF_UB]
    ch = ca * ha
    ch_prev = prev_ref[HALO - 1:HALO, OFF_CA:OFF_HA] * prev_ref[HALO - 1:HALO, OFF_HA:OFF_UB] * has_prev
    ch_next = next_ref[0:1, OFF_CA:OFF_HA] * next_ref[0:1, OFF_HA:OFF_UB] * has_next
    row = lax.broadcasted_iota(jnp.int32, (tm, W_A), 0)
    ch_m1 = jnp.where(row == 0, ch_prev, pltpu.roll(ch, 1, 0))
    ch_p1 = jnp.where(row == tm - 1, ch_next, pltpu.roll(ch, tm - 1, 0))
    cv = cw_ref[0:1, :] * ch_m1 + cw_ref[1:2, :] * ch + cw_ref[2:3, :] * ch_p1
    return ca, ha, ch, ch_m1, ch_p1, cv


def _spatial_fwd(p_ref, vg_ref, vb_ref, ws_ref, bsf_ref, vnb_ref, mixed_ref, tm):
    vb_pre = p_ref[:, OFF_VB:OFF_GA]
    gv, tv = _gelu(vb_pre)
    xhv, rstdv = _ln_stats(gv)
    vnb_ref[...] = (xhv * vg_ref[...] + vb_ref[...]).astype(BF)
    for c in range(tm // CHUNK):
        rows = slice(c * CHUNK, (c + 1) * CHUNK)
        for h in range(N_HEAD):
            cols = slice(h * CHUNK, (h + 1) * CHUNK)
            mixed_ref[rows, cols] = _dot(ws_ref[h], vnb_ref[rows, cols]) + bsf_ref[:, cols]
    return vb_pre, tv, xhv, rstdv


def _mix_fwd(p, pedge, x, wpa, wpb, wo, wsb, bsf, bg, cw, vg, vb, tm):
    t = x.shape[0]
    nt = t // tm

    def body(p_ref, prev_ref, next_ref, x_ref, wpa_ref, wpb_ref, wo_ref, ws_ref, bsf_ref, bg_ref, cw_ref, vg_ref, vb_ref,
             r1_ref, ya_ref, yb_ref, vnb_ref, mixed_ref):
        has_prev, has_next = _end_masks(nt)
        _, _, _, _, _, cv = _conv_fwd(p_ref, prev_ref, next_ref, cw_ref, tm, has_prev, has_next)
        a = p_ref[:, 0:OFF_CA] * cv
        ya = _dot(a.astype(BF), wpa_ref[...])
        ya_ref[...] = ya
        _spatial_fwd(p_ref, vg_ref, vb_ref, ws_ref, bsf_ref, vnb_ref, mixed_ref, tm)
        gu, _ = _gelu(p_ref[:, OFF_UB:OFF_VB])
        bb = gu * mixed_ref[...]
        yb = _dot(bb.astype(BF), wpb_ref[...])
        yb_ref[...] = yb
        ga = jax.nn.sigmoid(p_ref[:, OFF_GA:OFF_GB] + bg_ref[:, 0:D])
        gb = jax.nn.sigmoid(p_ref[:, OFF_GB:N_PROJ] + bg_ref[:, D:2 * D])
        z = ga * ya + gb * yb
        r1_ref[...] = ALPHA * x_ref[...] + _dot(z.astype(BF), wo_ref[...])

    tile = lambda w: pl.BlockSpec((tm, w), lambda i: (i, 0))
    return _pc(
        body, name="mix_fwd", grid=(nt,),
        in_specs=[tile(N_PROJ), *_edge_specs(nt, N_PROJ), tile(D),
                  _resident((W_A, D)), _resident((W_B, D)), _resident((D, D)), _resident((N_HEAD, CHUNK, CHUNK)),
                  _resident((CHUNK, W_B)), _resident((1, 2 * D)), _resident((3, W_A)), _resident((1, W_B)),
                  _resident((1, W_B))],
        out_specs=[tile(D), tile(D), tile(D)],
        out_shape=[jax.ShapeDtypeStruct((t, D), F32)] * 3,
        scratch_shapes=[pltpu.VMEM((tm, W_B), BF), pltpu.VMEM((tm, W_B), F32)],
        compiler_params=_params("arbitrary"),
    )(p, pedge, pedge, x, wpa, wpb, wo, wsb, bsf, bg, cw, vg, vb)


def _ffn_fwd_bwd(r1, tgt, wff1, wff2, ln1g, ln1b, ln2g, ln2b, tm):
    t = r1.shape[0]

    def body(r1_ref, tgt_ref, w1_ref, w2_ref, g1_ref, b1_ref, g2_ref, b2_ref,
             dr1_ref, dr1b_ref, dedge_ref, x1b_ref, hidb_ref, dh1b_ref, dr2b_ref, acc_ref, relu_ref):
        @pl.when(pl.program_id(0) == 0)
        def _():
            acc_ref[...] = jnp.zeros_like(acc_ref)

        xh1, rstd1 = _ln_stats(r1_ref[...])
        x1 = xh1 * g1_ref[...] + b1_ref[...]
        x1b_ref[...] = x1.astype(BF)
        ffn = jnp.zeros((tm, D), F32)
        for j in range(N_CHIP):
            cols = slice(j * FF_SHARD, (j + 1) * FF_SHARD)
            r = jnp.maximum(_dot(x1b_ref[...], w1_ref[j]), 0.0)
            relu_ref[:, cols] = r
            hidb_ref[:, cols] = (r * r).astype(BF)
            ffn = ffn + _dot(hidb_ref[:, cols], w2_ref[cols, :])
        xh2, rstd2 = _ln_stats(ALPHA * x1 + ffn)
        diff = xh2 * g2_ref[...] + b2_ref[...] - tgt_ref[...]
        acc_ref[4:5, :] += _colsum(diff * diff)
        dx2 = diff * (1.0 / D)
        acc_ref[2:3, :] += _colsum(dx2 * xh2)
        acc_ref[3:4, :] += _colsum(dx2)
        dr2 = _ln_bwd(dx2, g2_ref[...], xh2, rstd2)
        dr2b_ref[...] = dr2.astype(BF)
        dx1 = ALPHA * dr2
        for j in range(N_CHIP):
            cols = slice(j * FF_SHARD, (j + 1) * FF_SHARD)
            dhid = _dot_nt(dr2b_ref[...], w2_ref[cols, :])
            dh1b_ref[:, cols] = (dhid * (2.0 * relu_ref[:, cols])).astype(BF)
            dx1 = dx1 + _dot_nt(dh1b_ref[:, cols], w1_ref[j])
        acc_ref[0:1, :] += _colsum(dx1 * xh1)
        acc_ref[1:2, :] += _colsum(dx1)
        dr1 = _ln_bwd(dx1, g1_ref[...], xh1, rstd1)
        dr1_ref[...] = dr1
        dr1b_ref[...] = dr1.astype(BF)
        dedge_ref[0:HALO, :] = dr1_ref[0:HALO, :]
        dedge_ref[HALO:2 * HALO, :] = dr1_ref[tm - HALO:tm, :]

    tile = lambda w: pl.BlockSpec((tm, w), lambda i: (i, 0))
    vec = _resident((1, D))
    return _pc(
        body, name="ffn_fwd_bwd", grid=(t // tm,),
        in_specs=[tile(D), tile(D), _resident((N_CHIP, D, FF_SHARD)), _resident((D_FF, D)), vec, vec, vec, vec],
        out_specs=[tile(D), tile(D), pl.BlockSpec((None, 2 * HALO, D), lambda i: (i, 0, 0)), tile(D), tile(D_FF),
                   tile(D_FF), tile(D), pl.BlockSpec((8, D), lambda i: (0, 0))],
        out_shape=[jax.ShapeDtypeStruct((t, D), F32), jax.ShapeDtypeStruct((t, D), BF),
                   jax.ShapeDtypeStruct((t // tm, 2 * HALO, D), F32), jax.ShapeDtypeStruct((t, D), BF),
                   jax.ShapeDtypeStruct((t, D_FF), BF), jax.ShapeDtypeStruct((t, D_FF), BF),
                   jax.ShapeDtypeStruct((t, D), BF), jax.ShapeDtypeStruct((8, D), F32)],
        scratch_shapes=[pltpu.VMEM((tm, D_FF), F32)],
        compiler_params=_params("arbitrary"),
    )(r1, tgt, wff1, wff2, ln1g, ln1b, ln2g, ln2b)


def _dw(a, b, nblk, am, bn, a_blocked, b_blocked, tk, name, comm=None):
    t = a.shape[0]

    def body(a_ref, b_ref, o_ref):
        @pl.when(pl.program_id(1) == 0)
        def _():
            o_ref[...] = jnp.zeros_like(o_ref)

        o_ref[...] += _dot_tn(a_ref[...].astype(BF), b_ref[...])

    outs, got = _host_call(
        body, comm, name=name, grid=(nblk, t // tk),
        in_specs=[pl.BlockSpec((tk, am), (lambda j, k: (k, j)) if a_blocked else (lambda j, k: (k, 0))),
                  pl.BlockSpec((tk, bn), (lambda j, k: (k, j)) if b_blocked else (lambda j, k: (k, 0)))],
        out_specs=[pl.BlockSpec((None, am, bn), lambda j, k: (j, 0, 0))],
        out_shape=[jax.ShapeDtypeStruct((nblk, am, bn), F32)], args=(a, b))
    return outs[0] if comm is None else (outs[0], got)


def _dx(dp, win4, dr1, tm, blk0, nblk, filled, name, comm):
    t = dp.shape[0]

    def body(dp_ref, w_ref, dr1_ref, *rest):
        dx_ref = rest[-1]

        @pl.when(pl.program_id(1) == 0)
        def _():
            dx_ref[...] = ALPHA * dr1_ref[...]

        dx_ref[...] += _dot_nt(dp_ref[...], w_ref[...])

    in_specs = [pl.BlockSpec((tm, NP_SHARD), lambda i, j: (i + blk0, j)),
                pl.BlockSpec((None, D, NP_SHARD), lambda i, j: (j, 0, 0)),
                pl.BlockSpec((tm, D), lambda i, j: (i + blk0, 0))]
    args = (dp, win4, dr1)
    aliases = None
    if filled is not None:
        in_specs.append(pl.BlockSpec(memory_space=pl.ANY))
        args += (filled,)
        aliases = {3: 0}
    outs, got = _host_call(
        body, comm, name=name, grid=(nblk, N_CHIP), in_specs=in_specs,
        out_specs=[pl.BlockSpec((tm, D), lambda i, j: (i + blk0, 0))],
        out_shape=[jax.ShapeDtypeStruct((t, D), F32)], args=args, aliases=aliases)
    return outs[0], got


def _mix_bwd(p, pedge, dr1, dedge, ya, yb, wpa, wpb, wo, wsb, wstb, bsf, bg, cw, vg, vb, tm, comm):
    t = p.shape[0]
    nt = t // tm
    te = tm + 2 * HALO
    mid = slice(HALO, HALO + tm)

    def body(p_ref, prev_ref, next_ref, dr1_ref, dprev_ref, dnext_ref, ya_ref, yb_ref, wpa_ref, wpb_ref, wo_ref,
             ws_ref, wst_ref, bsf_ref, bg_ref, cw_ref, vg_ref, vb_ref,
             dp_ref, ab_ref, bbb_ref, zb_ref, dyab_ref, dybb_ref, dbg_ref, dcw_ref, dvgb_ref, dws_ref, dbs_ref,
             vnb_ref, mixed_ref, dmixb_ref, dvn_ref):
        @pl.when(pl.program_id(0) == 0)
        def _():
            for r in (dbg_ref, dcw_ref, dvgb_ref, dws_ref, dbs_ref):
                r[...] = jnp.zeros_like(r)

        has_prev, has_next = _end_masks(nt)
        ca, ha, ch, ch_m1, ch_p1, cv = _conv_fwd(p_ref, prev_ref, next_ref, cw_ref, tm, has_prev, has_next)
        ba = p_ref[:, 0:OFF_CA]
        ab_ref[...] = (ba * cv).astype(BF)
        vb_pre, tv, xhv, rstdv = _spatial_fwd(p_ref, vg_ref, vb_ref, ws_ref, bsf_ref, vnb_ref, mixed_ref, tm)
        ub = p_ref[:, OFF_UB:OFF_VB]
        gu, tu = _gelu(ub)
        bbb_ref[...] = (gu * mixed_ref[...]).astype(BF)
        bga = bg_ref[:, 0:D]
        ga = jax.nn.sigmoid(p_ref[:, OFF_GA:OFF_GB] + bga)
        gb = jax.nn.sigmoid(p_ref[:, OFF_GB:N_PROJ] + bg_ref[:, D:2 * D])
        ya = ya_ref[...]
        yb = yb_ref[...]
        zb_ref[...] = (ga * ya + gb * yb).astype(BF)

        dr1_ext = jnp.concatenate([dprev_ref[...] * has_prev, dr1_ref[...], dnext_ref[...] * has_next], axis=0)
        dz_ext = _dot_nt(dr1_ext.astype(BF), wo_ref[...])
        ga_ext = jnp.concatenate([jax.nn.sigmoid(prev_ref[:, OFF_GA:OFF_GB] + bga), ga,
                                  jax.nn.sigmoid(next_ref[:, OFF_GA:OFF_GB] + bga)], axis=0)
        dya_ext = dz_ext * ga_ext
        dyab_ref[...] = dya_ext[mid].astype(BF)
        da_ext = _dot_nt(dya_ext.astype(BF), wpa_ref[...])
        ba_ext = jnp.concatenate([prev_ref[:, 0:OFF_CA], ba, next_ref[:, 0:OFF_CA]], axis=0)
        dcv_ext = da_ext * ba_ext
        dcv = dcv_ext[mid]
        dch = (cw_ref[0:1, :] * pltpu.roll(dcv_ext, te - 1, 0)[mid] + cw_ref[1:2, :] * dcv
               + cw_ref[2:3, :] * pltpu.roll(dcv_ext, 1, 0)[mid])
        dp_ref[:, 0:OFF_CA] = (da_ext[mid] * cv).astype(BF)
        dp_ref[:, OFF_CA:OFF_HA] = (dch * ha).astype(BF)
        dp_ref[:, OFF_HA:OFF_UB] = (dch * ca).astype(BF)
        dcw_ref[0:1, :] += _colsum(dcv * ch_m1)
        dcw_ref[1:2, :] += _colsum(dcv * ch)
        dcw_ref[2:3, :] += _colsum(dcv * ch_p1)

        dz = dz_ext[mid]
        dga = dz * ya * ga * (1.0 - ga)
        dgb = dz * yb * gb * (1.0 - gb)
        dp_ref[:, OFF_GA:OFF_GB] = dga.astype(BF)
        dp_ref[:, OFF_GB:N_PROJ] = dgb.astype(BF)
        dbg_ref[0:1, 0:D] += _colsum(dga)
        dbg_ref[0:1, D:2 * D] += _colsum(dgb)

        dybb_ref[...] = (dz * gb).astype(BF)
        dbb = _dot_nt(dybb_ref[...], wpb_ref[...])
        dp_ref[:, OFF_UB:OFF_VB] = (dbb * mixed_ref[...] * _gelu_grad(ub, tu)).astype(BF)
        dmixed = dbb * gu
        dmixb_ref[...] = dmixed.astype(BF)
        for c in range(tm // CHUNK):
            rows = slice(c * CHUNK, (c + 1) * CHUNK)
            dbs_ref[...] += dmixed[rows]
            for h in range(N_HEAD):
                cols = slice(h * CHUNK, (h + 1) * CHUNK)
                dws_ref[h] += _dot_nt(dmixb_ref[rows, cols], vnb_ref[rows, cols])
                dvn_ref[rows, cols] = _dot(wst_ref[h], dmixb_ref[rows, cols])
        dvn = dvn_ref[...]
        dvgb_ref[0:1, :] += _colsum(dvn * xhv)
        dvgb_ref[1:2, :] += _colsum(dvn)
        dgv = _ln_bwd(dvn, vg_ref[...], xhv, rstdv)
        dp_ref[:, OFF_VB:OFF_GA] = (dgv * _gelu_grad(vb_pre, tv)).astype(BF)

    tile = lambda w: pl.BlockSpec((tm, w), lambda i: (i, 0))
    acc = lambda *s: pl.BlockSpec(s, lambda i: (0,) * len(s))
    return _host_call(
        body, comm, name="mix_bwd", grid=(nt,),
        in_specs=[tile(N_PROJ), *_edge_specs(nt, N_PROJ), tile(D), *_edge_specs(nt, D), tile(D), tile(D),
                  _resident((W_A, D)), _resident((W_B, D)), _resident((D, D)), _resident((N_HEAD, CHUNK, CHUNK)),
                  _resident((N_HEAD, CHUNK, CHUNK)), _resident((CHUNK, W_B)), _resident((1, 2 * D)),
                  _resident((3, W_A)), _resident((1, W_B)), _resident((1, W_B))],
        out_specs=[tile(N_PROJ), tile(W_A), tile(W_B), tile(D), tile(D), tile(D),
                   acc(8, 2 * D), acc(8, W_A), acc(8, W_B), acc(N_HEAD, CHUNK, CHUNK), acc(CHUNK, W_B)],
        out_shape=[jax.ShapeDtypeStruct((t, N_PROJ), BF), jax.ShapeDtypeStruct((t, W_A), BF),
                   jax.ShapeDtypeStruct((t, W_B), BF), jax.ShapeDtypeStruct((t, D), BF), jax.ShapeDtypeStruct((t, D), BF),
                   jax.ShapeDtypeStruct((t, D), BF), jax.ShapeDtypeStruct((8, 2 * D), F32),
                   jax.ShapeDtypeStruct((8, W_A), F32), jax.ShapeDtypeStruct((8, W_B), F32),
                   jax.ShapeDtypeStruct((N_HEAD, CHUNK, CHUNK), F32), jax.ShapeDtypeStruct((CHUNK, W_B), F32)],
        scratch_shapes=[pltpu.VMEM((tm, W_B), BF), pltpu.VMEM((tm, W_B), F32), pltpu.VMEM((tm, W_B), BF),
                        pltpu.VMEM((tm, W_B), F32)],
        args=(p, pedge, pedge, dr1, dedge, dedge, ya, yb, wpa, wpb, wo, wsb, wstb, bsf, bg, cw, vg, vb))


def _add_own_half(full4, recv4, core, rb, name):
    n, rh, cols = recv4.shape
    nb = rh // rb

    def body(c_ref, a_ref, b_ref, o_ref, ob_ref):
        s = a_ref[...] + b_ref[...]
        o_ref[...] = s
        ob_ref[...] = s.astype(BF)

    blk = (None, rb, cols)
    same = pl.BlockSpec(blk, lambda k, i, c: (k, i, 0))
    return _pc(
        body, name=name,
        grid_spec=pltpu.PrefetchScalarGridSpec(
            num_scalar_prefetch=1, grid=(n, nb),
            in_specs=[pl.BlockSpec(blk, lambda k, i, c: (k, c[0] * nb + i, 0)), same], out_specs=[same, same]),
        out_shape=[jax.ShapeDtypeStruct(recv4.shape, F32), jax.ShapeDtypeStruct(recv4.shape, BF)],
        compiler_params=_params("arbitrary", "arbitrary"),
    )(core, full4, recv4)


def _add_chips(s4, r3, place, rb, name):
    _, rh, cols = r3.shape
    nb = rh // rb

    def body(pl_ref, s_ref, r_ref, o_ref):
        o_ref[...] = ((s_ref[...] + r_ref[0].astype(F32)) + r_ref[1].astype(F32)) + r_ref[2].astype(F32)

    return _pc(
        body, name=name,
        grid_spec=pltpu.PrefetchScalarGridSpec(
            num_scalar_prefetch=1, grid=(nb,),
            in_specs=[pl.BlockSpec((None, rb, cols), lambda i, s: (s[0], i, 0)), pl.BlockSpec((3, rb, cols), lambda i, s: (0, i, 0))],
            out_specs=pl.BlockSpec((rb, cols), lambda i, s: (s[1] * nb + i, 0))),
        out_shape=jax.ShapeDtypeStruct((2 * rh, cols), F32),
        compiler_params=_params("arbitrary"),
    )(place, s4, r3)


def _add_small(a, b):
    def body(a_ref, b_ref, o_ref):
        o_ref[...] = a_ref[...] + b_ref[...]

    return _pc(body, name="add_small_cores", out_shape=jax.ShapeDtypeStruct(a.shape, F32))(a, b)


def _sum_small_chips(own, slots, place):
    def body(pl_ref, own_ref, s_ref, o_ref):
        j = pl_ref[0]

        def term(k):
            return jnp.where(j == k, own_ref[...], s_ref[jnp.maximum((j ^ k) - 1, 0)])

        o_ref[...] = ((term(0) + term(1)) + term(2)) + term(3)

    vmem = pl.BlockSpec(memory_space=pltpu.VMEM)
    return _pc(body, name="sum_small_chips", in_specs=[pl.BlockSpec(memory_space=pltpu.SMEM), vmem, vmem], out_specs=vmem,
               out_shape=jax.ShapeDtypeStruct(own.shape, F32))(place, own, slots)


def _adamw(w, g, m, v, rb, name):
    rows, cols = w.shape

    def body(w_ref, g_ref, m_ref, v_ref, d_ref, m2_ref, v2_ref):
        g_ = g_ref[...]
        m2 = ADAM_B1 * m_ref[...] + (1.0 - ADAM_B1) * g_
        v2 = ADAM_B2 * v_ref[...] + (1.0 - ADAM_B2) * (g_ * g_)
        m_hat = m2 / (1.0 - ADAM_B1 ** ADAM_STEP)
        v_hat = v2 / (1.0 - ADAM_B2 ** ADAM_STEP)
        d_ref[...] = -ADAM_LR * (m_hat / (jnp.sqrt(v_hat) + ADAM_EPS) + ADAM_WD * w_ref[...])
        m2_ref[...] = m2
        v2_ref[...] = v2

    blk = pl.BlockSpec((rb, cols), lambda i: (i, 0))
    return _pc(body, name=name, grid=(rows // rb,), in_specs=[blk] * 4, out_specs=[blk] * 3,
               out_shape=[jax.ShapeDtypeStruct((rows, cols), F32)] * 3, compiler_params=_params("arbitrary"))(w, g, m, v)


LANES = 128
SMALL_GRADS = (("b_gate", 2 * D), ("conv_w", 3 * W_A), ("v_norm_g", W_B), ("v_norm_b", W_B),
               ("w_s", N_HEAD * CHUNK * CHUNK), ("b_s", N_HEAD * CHUNK), ("ln1_g", D), ("ln1_b", D), ("ln2_g", D), ("ln2_b", D),
               ("loss", 1))


def _pack_rows(parts):
    rows = []
    for a in parts:
        a = a.reshape(-1)
        a = jnp.pad(a, (0, (-a.shape[0]) % LANES))
        rows.append(a.reshape(-1, LANES))
    out = jnp.concatenate(rows, axis=0)
    return jnp.pad(out, ((0, (-out.shape[0]) % 8), (0, 0)))


def _unpack_rows(buf, sizes):
    out, r = [], 0
    for n in sizes:
        nr = -(-n // LANES)
        out.append(buf[r:r + nr].reshape(-1)[:n])
        r += nr
    return out


TM_PROJ = 1024
TM_MIX = 256
TM_DX = 1024
DX_PAIR = 3
DX_ALONE = 2
TK_DW = 2048
TK_DW_IN = 1024
RB_ADD = 64
RB_ADAM = 128
CONV_ROWS = 8


def _reduce_adds_1(grads, recvs, core, tag):
    out = [_add_own_half(g, r, core, RB_ADD if r.shape[1] % 256 else 256, f"add_cores_{tag}{a}")
           for a, (g, r) in enumerate(zip(grads, recvs))]
    return [o[0] for o in out], [o[1] for o in out]


def _reduce_adds_2(sums, recvs, place, tag):
    return [_add_chips(s, r, place, RB_ADD if r.shape[1] % 256 else 256, f"add_chips_{tag}{a}")
            for a, (s, r) in enumerate(zip(sums, recvs))]


def kernel(x, w_in, b_gate, conv_w, v_norm_g, v_norm_b, w_s, b_s, w_pa, w_pb, w_o, ln1_g, ln1_b, w_ff1, w_ff2, ln2_g, ln2_b, loss_target, m_w_in, m_b_gate, m_conv_w, m_v_norm_g, m_v_norm_b, m_w_s, m_b_s, m_w_pa, m_w_pb, m_w_o, m_ln1_g, m_ln1_b, m_w_ff1, m_w_ff2, m_ln2_g, m_ln2_b, v_w_in, v_b_gate, v_conv_w, v_v_norm_g, v_v_norm_b, v_w_s, v_b_s, v_w_pa, v_w_pb, v_w_o, v_ln1_g, v_ln1_b, v_w_ff1, v_w_ff2, v_ln2_g, v_ln2_b):
    t = x.shape[1]
    core = lax.axis_index("c").astype(jnp.int32).reshape(1)
    chip_idx = 2 * lax.axis_index("x") + lax.axis_index("y")
    chip = chip_idx.astype(jnp.int32).reshape(1)
    place = jnp.concatenate([chip, core])
    x2 = x.reshape(t, D)
    tgt = loss_target.reshape(t, D)

    win4, proj4, ff14, ff24 = _cast_shards(w_in[0], w_pa[0], w_pb[0], w_o[0], w_ff1[0], w_ff2[0], chip)
    conv4 = lax.dynamic_update_slice(jnp.zeros((N_CHIP, CONV_ROWS, W_A // N_CHIP), F32),
                                     jnp.pad(conv_w[0], ((0, CONV_ROWS - 3), (0, 0)))[None], (chip_idx, 0, 0))
    (p, pedge), (win4, proj4, ff14, ff24, conv4) = _proj_fwd(
        x2, chip, TM_PROJ, TM_MIX, _gather_comm([win4, proj4, ff14, ff24], conv4, eager=1))

    def full(name, rows_total):
        off, rows = PROJ_OFF[name]
        return proj4[:, off:off + rows, :].reshape(rows_total, D)

    wpa, wpb, wo = full("w_pa", W_A), full("w_pb", W_B), full("w_o", D)
    wff2 = ff24.reshape(D_FF, D)
    cw = jnp.transpose(conv4[:, :3, :], (1, 0, 2)).reshape(3, W_A)
    wsb = w_s[0].astype(BF)
    wstb = jnp.swapaxes(w_s[0], 1, 2).astype(BF)
    bsf = jnp.repeat(jnp.transpose(b_s[0]), CHUNK, axis=1)

    r1, ya, yb = _mix_fwd(p, pedge, x2, wpa, wpb, wo, wsb, bsf, b_gate, cw, v_norm_g, v_norm_b, TM_MIX)
    dr1, dr1b, dedge, x1b, hidb, dh1b, dr2b, acc = _ffn_fwd_bwd(r1, tgt, ff14, wff2, ln1_g, ln1_b, ln2_g, ln2_b, TM_MIX)
    g_ff = [_dw(x1b, dh1b, N_CHIP, D, FF_SHARD, False, True, TK_DW, "dw_ff1"),
            _dw(hidb, dr2b, N_CHIP, FF_SHARD, D, True, False, TK_DW, "dw_ff2")]
    (dp, ab, bbb, zb, dyab, dybb, dbg, dcw, dvgb, dws, dbs_sum), r_ff = _mix_bwd(
        p, pedge, dr1, dedge, ya, yb, wpa, wpb, wo, wsb, wstb, bsf, b_gate, cw, v_norm_g, v_norm_b, TM_MIX, _pair_comm(g_ff))
    s_ff, sb_ff = _reduce_adds_1(g_ff, r_ff, core, "ff")
    dwin4, c_ff = _dw(x2, dp, N_CHIP, D, NP_SHARD, False, True, TK_DW_IN, "dw_in", _chips_comm(sb_ff))
    f_ff = _reduce_adds_2(s_ff, c_ff, place, "ff")
    dwpa, (g_ff1, g_ff2) = _dw(ab, dyab, 1, W_A, D, False, False, TK_DW, "dw_pa", _join_comm(f_ff))
    dwpb = _dw(bbb, dybb, 1, W_B, D, False, False, TK_DW, "dw_pb")
    dwo = _dw(zb, dr1b, 1, D, D, False, False, TK_DW, "dw_o")
    dproj4 = jnp.concatenate([dwpa.reshape(N_CHIP, -1, D), dwpb.reshape(N_CHIP, -1, D), dwo.reshape(N_CHIP, -1, D)], axis=1)
    dbs = jnp.transpose(jnp.sum(dbs_sum.reshape(CHUNK, N_HEAD, CHUNK), axis=-1))
    small = _pack_rows([dbg[0], dcw[0:3], dvgb[0], dvgb[1], dws, dbs, acc[0], acc[1], acc[2], acc[3],
                        0.5 * jnp.sum(acc[4]) / D])
    g_rest = [dwin4, dproj4]
    nblk = t // TM_DX
    n_a, n_c = max(1, min(DX_PAIR, nblk // 4)), max(1, min(DX_ALONE, nblk // 4))
    n_b = nblk - n_a - n_c
    dx, r_rest = _dx(dp, win4, dr1, TM_DX, 0, n_a, None, "dx_a", _pair_comm(g_rest, small))
    s_rest, sb_rest = _reduce_adds_1(g_rest, r_rest[:2], core, "rest")
    csmall = _add_small(small, r_rest[2])
    dx, c_rest = _dx(dp, win4, dr1, TM_DX, n_a, n_b, dx, "dx_b", _chips_comm(sb_rest, csmall))
    dx, _ = _dx(dp, win4, dr1, TM_DX, n_a + n_b, n_c, dx, "dx_c", None)
    f_rest = _reduce_adds_2(s_rest, c_rest[:2], place, "rest")
    gsmall = _sum_small_chips(csmall, c_rest[2], place)
    g_in, g_proj = _comm_call(_join_comm(f_rest), "join_rest")

    grads = {"w_in": g_in, "w_ff1": g_ff1, "w_ff2": g_ff2}
    for name, _ in PROJ_ROWS:
        off, rows = PROJ_OFF[name]
        grads[name] = g_proj[off:off + rows, :]
    for (name, n), flat in zip(SMALL_GRADS, _unpack_rows(gsmall, [n for _, n in SMALL_GRADS])):
        grads[name] = flat
    loss = grads.pop("loss").reshape(())
    grads["conv_w"] = lax.dynamic_slice(grads["conv_w"].reshape(3, W_A), (0, chip_idx * (W_A // N_CHIP)), (3, W_A // N_CHIP))

    weights = dict(w_in=w_in, b_gate=b_gate, conv_w=conv_w, v_norm_g=v_norm_g, v_norm_b=v_norm_b, w_s=w_s, b_s=b_s,
                   w_pa=w_pa, w_pb=w_pb, w_o=w_o, ln1_g=ln1_g, ln1_b=ln1_b, w_ff1=w_ff1, w_ff2=w_ff2, ln2_g=ln2_g, ln2_b=ln2_b)
    mom1 = dict(w_in=m_w_in, b_gate=m_b_gate, conv_w=m_conv_w, v_norm_g=m_v_norm_g, v_norm_b=m_v_norm_b, w_s=m_w_s,
                b_s=m_b_s, w_pa=m_w_pa, w_pb=m_w_pb, w_o=m_w_o, ln1_g=m_ln1_g, ln1_b=m_ln1_b, w_ff1=m_w_ff1,
                w_ff2=m_w_ff2, ln2_g=m_ln2_g, ln2_b=m_ln2_b)
    mom2 = dict(w_in=v_w_in, b_gate=v_b_gate, conv_w=v_conv_w, v_norm_g=v_v_norm_g, v_norm_b=v_v_norm_b, w_s=v_w_s,
                b_s=v_b_s, w_pa=v_w_pa, w_pb=v_w_pb, w_o=v_w_o, ln1_g=v_ln1_g, ln1_b=v_ln1_b, w_ff1=v_w_ff1,
                w_ff2=v_w_ff2, ln2_g=v_ln2_g, ln2_b=v_ln2_b)
    order = list(weights)
    big = ("w_in", "w_pa", "w_pb", "w_o", "w_ff1", "w_ff2")
    delta, new_m, new_v = {}, {}, {}
    for name in big:
        w2 = weights[name][0]
        delta[name], new_m[name], new_v[name] = _adamw(w2, grads[name], mom1[name][0], mom2[name][0], RB_ADAM, "adamw_" + name)
    little = [n for n in order if n not in big]
    sizes = [weights[n].size for n in little]
    wsmall = _pack_rows([weights[n] for n in little])
    ds, ms, vs = _adamw(wsmall, _pack_rows([grads[n] for n in little]), _pack_rows([mom1[n] for n in little]),
                        _pack_rows([mom2[n] for n in little]), wsmall.shape[0], "adamw_small")
    for name, d_, m_, v_ in zip(little, _unpack_rows(ds, sizes), _unpack_rows(ms, sizes), _unpack_rows(vs, sizes)):
        delta[name], new_m[name], new_v[name] = d_, m_, v_

    shaped = lambda d: [d[n].reshape(weights[n].shape) for n in order]
    return (loss, dx.reshape(x.shape), *shaped(grads), *shaped(delta), *shaped(new_m), *shaped(new_v))
```

```python
import functools
from typing import NamedTuple

import jax
import jax.numpy as jnp
from jax import lax
from jax.experimental import pallas as pl
from jax.experimental.pallas import tpu as pltpu

D = 1024
W_A = 1536
W_B = 1024
CHUNK = 128
N_HEAD = 8
D_FF = 4096
N_PROJ = 3 * W_A + 2 * W_B + 2 * D
OFF_CA, OFF_HA, OFF_UB, OFF_VB, OFF_GA, OFF_GB = 1536, 3072, 4608, 5632, 6656, 7680
LN_EPS = 1e-5
ALPHA = 2.0 ** 0.25
N_CHIP = 4
NP_SHARD = N_PROJ // N_CHIP
FF_SHARD = D_FF // N_CHIP
ADAM_LR, ADAM_B1, ADAM_B2, ADAM_EPS, ADAM_WD, ADAM_STEP = 0.001, 0.9, 0.999, 1e-08, 0.01, 10

PROJ_ROWS = (("w_pa", W_A // N_CHIP), ("w_pb", W_B // N_CHIP), ("w_o", D // N_CHIP))
PROJ_OFF = {}
_o = 0
for _n, _r in PROJ_ROWS:
    PROJ_OFF[_n] = (_o, _r)
    _o += _r
PROJ_TOTAL = _o

V7X_VMEM_BYTES = 64 * 1024 * 1024
VMEM_LIMIT = 56 * 1024 * 1024
HALO = 8
EDGE_TILE = 128

BF = jnp.bfloat16
F32 = jnp.float32
MESH = pl.DeviceIdType.MESH
HBM_SPEC = pl.BlockSpec(memory_space=pltpu.HBM)


def _pc(body, **kw):
    return pl.pallas_call(body, **kw)


def _params(*sem):
    return pltpu.CompilerParams(dimension_semantics=sem, vmem_limit_bytes=VMEM_LIMIT)


def _resident(shape):
    n = len(shape)
    return pl.BlockSpec(shape, lambda *_: (0,) * n, pipeline_mode=pl.Buffered(1))


def _dot(a, b):
    return jnp.dot(a, b, preferred_element_type=F32)


def _dot_nt(a, b):
    return lax.dot_general(a, b, (((1,), (1,)), ((), ())), preferred_element_type=F32)


def _dot_tn(a, b):
    return lax.dot_general(a, b, (((0,), (0,)), ((), ())), preferred_element_type=F32)


def _gelu(x):
    t = jnp.tanh(0.7978845608028654 * (x + 0.044715 * (x * x * x)))
    return 0.5 * x * (1.0 + t), t


def _gelu_grad(x, t):
    return 0.5 * (1.0 + t) + 0.5 * x * (1.0 - t * t) * (0.7978845608028654 * (1.0 + 0.134145 * (x * x)))


def _ln_stats(r):
    mu = jnp.mean(r, axis=-1, keepdims=True)
    xc = r - mu
    var = jnp.mean(xc * xc, axis=-1, keepdims=True)
    rstd = lax.rsqrt(var + LN_EPS)
    return xc * rstd, rstd


def _ln_bwd(dy, g, xh, rstd):
    dxh = dy * g
    m1 = jnp.mean(dxh, axis=-1, keepdims=True)
    m2 = jnp.mean(dxh * xh, axis=-1, keepdims=True)
    return rstd * (dxh - m1 - xh * m2)


def _colsum(v):
    return jnp.sum(v, axis=0, keepdims=True)


class _Comm(NamedTuple):
    args: tuple
    out_shape: tuple
    aliases: dict
    n_sems: int
    stages: tuple


def _place():
    x, y, c = lax.axis_index("x"), lax.axis_index("y"), lax.axis_index("c")
    return x, y, c, 2 * x + y


def _flip(x, y, c, r):
    return (x ^ (r >> 1), y ^ (r & 1), c)


def _remote(src, dst, send_sems, recv_sems, k, peer):
    return pltpu.make_async_remote_copy(src_ref=src, dst_ref=dst, send_sem=send_sems.at[k], recv_sem=recv_sems.at[k],
                                        device_id=peer, device_id_type=MESH)


def _host_call(body, comm, *, name, grid, in_specs, out_specs, out_shape, args, scratch_shapes=(), aliases=None, prefetch=None,
               body_reads_comm=False):
    sem = ("arbitrary",) * len(grid)
    aliases = dict(aliases or {})
    n_pre = 0 if prefetch is None else 1
    n_in, n_out, n_scr = len(in_specs), len(out_specs), len(scratch_shapes)
    c_in, c_out = (0, 0) if comm is None else (len(comm.args), len(comm.out_shape))
    steps = {"first": (0,) * len(grid), "late": (grid[0] - 1,) + (0,) * (len(grid) - 1), "last": tuple(g - 1 for g in grid)}

    def wrapped(*refs):
        refs = refs[n_pre:]
        own_in, cin = refs[:n_in], refs[n_in:n_in + c_in]
        o0 = n_in + c_in
        own_out, cout = refs[o0:o0 + n_out], refs[o0 + n_out:o0 + n_out + c_out]
        s0 = o0 + n_out + c_out
        scr, sems = refs[s0:s0 + n_scr], refs[s0 + n_scr:]

        def run(before):
            for phase, fn in () if comm is None else comm.stages:
                at_step = isinstance(phase, tuple)
                if before != (at_step or phase == "first"):
                    continue
                step = phase if at_step else steps[phase]
                cond = pl.program_id(0) == step[0]
                for d in range(1, len(grid)):
                    cond = jnp.logical_and(cond, pl.program_id(d) == step[d])
                pl.when(cond)(functools.partial(fn, cin, cout, *sems))

        run(True)
        if body_reads_comm:
            body(*own_in, *own_out, *scr, comm_refs=cout)
        else:
            body(*own_in, *own_out, *scr)
        run(False)

    in_specs = list(in_specs) + [HBM_SPEC] * c_in
    out_specs = list(out_specs) + [HBM_SPEC] * c_out
    out_shape = list(out_shape) + ([] if comm is None else list(comm.out_shape))
    scratch_shapes = list(scratch_shapes) + ([] if comm is None else [pltpu.SemaphoreType.DMA((comm.n_sems,))] * 2)
    args = tuple(args) + (() if comm is None else tuple(comm.args))
    if comm is not None:
        aliases.update({n_in + i: n_out + o for i, o in comm.aliases.items()})
    aliases = {i + n_pre: o for i, o in aliases.items()}
    if prefetch is None:
        kw = dict(grid=grid, in_specs=in_specs, out_specs=out_specs, scratch_shapes=scratch_shapes)
    else:
        kw = dict(grid_spec=pltpu.PrefetchScalarGridSpec(num_scalar_prefetch=1, grid=grid, in_specs=in_specs,
                                                         out_specs=out_specs, scratch_shapes=scratch_shapes))
        args = (prefetch,) + args
    outs = _pc(wrapped, name=name, out_shape=out_shape, input_output_aliases=aliases, compiler_params=_params(*sem), **kw)(*args)
    return outs[:n_out], outs[n_out:]


def _comm_call(comm, name):
    def body(*refs):
        c_in, c_out = len(comm.args), len(comm.out_shape)
        cin, cout, (send_sems, recv_sems) = refs[:c_in], refs[c_in:c_in + c_out], refs[c_in + c_out:]
        for phase in ("first", "late", "last"):
            for ph, fn in comm.stages:
                if ph == phase:
                    fn(cin, cout, send_sems, recv_sems)

    return _pc(body, name=name, in_specs=[HBM_SPEC] * len(comm.args), out_specs=[HBM_SPEC] * len(comm.out_shape),
               out_shape=list(comm.out_shape), scratch_shapes=[pltpu.SemaphoreType.DMA((comm.n_sems,))] * 2,
               input_output_aliases=dict(comm.aliases))(*comm.args)


def _gather_comm(bufs, whole=None, eager=0):
    n = len(bufs)
    halves = [b.shape[1] // 2 for b in bufs]
    k_ici = lambda a, r: 3 * a + r - 1
    k_d2d = lambda a, r: 3 * n + 3 * a + r - 1
    k_whole = lambda r: 6 * n + r - 1

    def half(ref, slot, c, a):
        return ref.at[slot, pl.ds(c * halves[a], halves[a])]

    def send(cin, cout, ss, rs):
        x, y, c, j = _place()
        for a in range(n):
            mine = half(cout[a], j, c, a)
            for r in (1, 2, 3):
                _remote(mine, mine, ss, rs, k_ici(a, r), _flip(x, y, c, r)).start()
        if whole is not None:
            for r in (1, 2, 3):
                _remote(cout[n].at[j], cout[n].at[j], ss, rs, k_whole(r), _flip(x, y, c, r)).start()

    def pass_on(cout, ss, rs, a, r):
        x, y, c, j = _place()
        landed = half(cout[a], j ^ r, c, a)
        _remote(landed, landed, ss, rs, k_ici(a, r), (x, y, 1 - c)).wait_recv()
        _remote(landed, landed, ss, rs, k_d2d(a, r), (x, y, 1 - c)).start()

    def passed_on(cout, ss, rs, a, r):
        x, y, c, j = _place()
        theirs = half(cout[a], j ^ r, 1 - c, a)
        _remote(theirs, theirs, ss, rs, k_d2d(a, r), (x, y, 1 - c)).wait_recv()

    def arrive(r, cin, cout, ss, rs):
        for a in range(eager):
            pass_on(cout, ss, rs, a, r)
        for a in range(eager):
            passed_on(cout, ss, rs, a, r)

    def forward(cin, cout, ss, rs):
        for a in range(eager, n):
            for r in (1, 2, 3):
                pass_on(cout, ss, rs, a, r)

    def finish(cin, cout, ss, rs):
        x, y, c, j = _place()
        sibling = (x, y, 1 - c)
        for a in range(eager, n):
            for r in (1, 2, 3):
                passed_on(cout, ss, rs, a, r)
        for a in range(n):
            mine = half(cout[a], j, c, a)
            for r in (1, 2, 3):
                _remote(mine, mine, ss, rs, k_ici(a, r), sibling).wait_send()
                landed = half(cout[a], j ^ r, c, a)
                _remote(landed, landed, ss, rs, k_d2d(a, r), sibling).wait_send()
        if whole is not None:
            for r in (1, 2, 3):
                cp = _remote(cout[n].at[j ^ r], cout[n].at[j ^ r], ss, rs, k_whole(r), sibling)
                cp.wait_recv()
                cp.wait_send()

    args = tuple(bufs) + ((whole,) if whole is not None else ())
    out_shape = tuple(jax.ShapeDtypeStruct(b.shape, b.dtype) for b in args)
    aliases = {a: a for a in range(len(args))}
    arrivals = tuple(((r, 0), functools.partial(arrive, r)) for r in (1, 2, 3)) if eager else ()
    return _Comm(args, out_shape, aliases, 6 * n + 3, (("first", send),) + arrivals + (("late", forward), ("last", finish)))


def _pair_comm(grads, small=None):
    n = len(grads)
    halves = [g.shape[1] // 2 for g in grads]

    def copies(cin, cout, ss, rs):
        x, y, c, _ = _place()
        sibling = (x, y, 1 - c)
        cps = [_remote(cin[a].at[:, pl.ds((1 - c) * halves[a], halves[a]), :], cout[a], ss, rs, a, sibling) for a in range(n)]
        if small is not None:
            cps.append(_remote(cin[n], cout[n], ss, rs, n, sibling))
        return cps

    def start(cin, cout, ss, rs):
        for cp in copies(cin, cout, ss, rs):
            cp.start()

    def finish(cin, cout, ss, rs):
        for cp in copies(cin, cout, ss, rs):
            cp.wait()

    args = tuple(grads) + ((small,) if small is not None else ())
    out_shape = tuple(jax.ShapeDtypeStruct((N_CHIP, h, g.shape[2]), F32) for g, h in zip(grads, halves))
    out_shape += (jax.ShapeDtypeStruct(small.shape, F32),) if small is not None else ()
    return _Comm(args, out_shape, {}, n + 1, (("first", start), ("last", finish)))


def _chips_comm(sums_bf, small=None):
    n = len(sums_bf)

    def copies(cin, cout, ss, rs):
        x, y, c, j = _place()
        cps = []
        for r in (1, 2, 3):
            peer = _flip(x, y, c, r)
            for a in range(n):
                cps.append(_remote(cin[a].at[j ^ r], cout[a].at[r - 1], ss, rs, (n + 1) * (r - 1) + a, peer))
            if small is not None:
                cps.append(_remote(cin[n], cout[n].at[r - 1], ss, rs, (n + 1) * (r - 1) + n, peer))
        return cps

    def start(cin, cout, ss, rs):
        for cp in copies(cin, cout, ss, rs):
            cp.start()

    def finish(cin, cout, ss, rs):
        for cp in copies(cin, cout, ss, rs):
            cp.wait()

    args = tuple(sums_bf) + ((small,) if small is not None else ())
    out_shape = tuple(jax.ShapeDtypeStruct((3,) + s.shape[1:], BF) for s in sums_bf)
    out_shape += (jax.ShapeDtypeStruct((3,) + small.shape, F32),) if small is not None else ()
    return _Comm(args, out_shape, {}, 3 * (n + 1), (("first", start), ("last", finish)))


def _join_comm(shards):
    n = len(shards)
    halves = [s.shape[0] // 2 for s in shards]

    def start(cin, cout, ss, rs):
        x, y, c, _ = _place()
        for a in range(n):
            mine = cout[a].at[pl.ds(c * halves[a], halves[a]), :]
            _remote(mine, mine, ss, rs, a, (x, y, 1 - c)).start()

    def finish(cin, cout, ss, rs):
        x, y, c, _ = _place()
        for a in range(n):
            theirs = cout[a].at[pl.ds((1 - c) * halves[a], halves[a]), :]
            cp = _remote(theirs, theirs, ss, rs, a, (x, y, 1 - c))
            cp.wait_recv()
            cp.wait_send()

    out_shape = tuple(jax.ShapeDtypeStruct(s.shape, F32) for s in shards)
    return _Comm(tuple(shards), out_shape, {a: a for a in range(n)}, n, (("first", start), ("last", finish)))


def _cast_shards(w_in, w_pa, w_pb, w_o, w_ff1, w_ff2, chip):
    def body(j_ref, win_ref, wpa_ref, wpb_ref, wo_ref, wff1_ref, wff2_ref, win4_ref, proj4_ref, ff14_ref, ff24_ref):
        win4_ref[...] = win_ref[...].astype(BF)
        for name, ref in (("w_pa", wpa_ref), ("w_pb", wpb_ref), ("w_o", wo_ref)):
            off, rows = PROJ_OFF[name]
            proj4_ref[off:off + rows, :] = ref[...].astype(BF)
        ff14_ref[...] = wff1_ref[...].astype(BF)
        ff24_ref[...] = wff2_ref[...].astype(BF)

    whole = lambda a: pl.BlockSpec(a.shape, lambda i, j: (0, 0), pipeline_mode=pl.Buffered(1))
    slot = lambda rows, cols: pl.BlockSpec((None, rows, cols), lambda i, j: (j[0], 0, 0))
    ws = (w_in, w_pa, w_pb, w_o, w_ff1, w_ff2)
    shapes = ((D, NP_SHARD), (PROJ_TOTAL, D), (D, FF_SHARD), (FF_SHARD, D))
    return _pc(
        body, name="cast_shards",
        grid_spec=pltpu.PrefetchScalarGridSpec(num_scalar_prefetch=1, grid=(1,), in_specs=[whole(w) for w in ws],
                                               out_specs=[slot(*s) for s in shapes]),
        out_shape=[jax.ShapeDtypeStruct((N_CHIP,) + s, BF) for s in shapes],
        compiler_params=_params("arbitrary"))(chip, *ws)


def _proj_fwd(x, chip, tm, comm):
    t = x.shape[0]
    sub = tm // EDGE_TILE

    def body(x_ref, p_ref, edge_ref, w_ref, w_sem, comm_refs):
        @pl.when(pl.program_id(1) == 0)
        def _():
            _, _, _, j = _place()
            block = pltpu.make_async_copy(comm_refs[0].at[j ^ pl.program_id(0)], w_ref, w_sem)
            block.start()
            block.wait()

        p_ref[...] = _dot(x_ref[...].astype(BF), w_ref[...])
        _write_edges(edge_ref, p_ref, tm)

    return _host_call(
        body, comm, name="proj_fwd", grid=(N_CHIP, t // tm), prefetch=chip, body_reads_comm=True,
        in_specs=[pl.BlockSpec((tm, D), lambda r, i, j: (i, 0))],
        out_specs=[pl.BlockSpec((tm, NP_SHARD), lambda r, i, j: (i, j[0] ^ r)),
                   pl.BlockSpec((sub, 2 * HALO, NP_SHARD), lambda r, i, j: (i, 0, j[0] ^ r))],
        out_shape=[jax.ShapeDtypeStruct((t, N_PROJ), F32), jax.ShapeDtypeStruct((t // EDGE_TILE, 2 * HALO, N_PROJ), F32)],
        scratch_shapes=[pltpu.VMEM((D, NP_SHARD), BF), pltpu.SemaphoreType.DMA],
        args=(x,))


def _edge_specs(t, tm, w):
    k, last = tm // EDGE_TILE, t // EDGE_TILE - 1
    return [pl.BlockSpec((None, HALO, w), lambda i: (jnp.maximum(i * k - 1, 0), 1, 0)),
            pl.BlockSpec((None, HALO, w), lambda i: (jnp.minimum((i + 1) * k, last), 0, 0))]


def _write_edges(edge_ref, rows_ref, tm):
    for s in range(tm // EDGE_TILE):
        edge_ref[s, 0:HALO, :] = rows_ref[s * EDGE_TILE:s * EDGE_TILE + HALO, :]
        edge_ref[s, HALO:2 * HALO, :] = rows_ref[(s + 1) * EDGE_TILE - HALO:(s + 1) * EDGE_TILE, :]


def _end_masks(nt):
    i = pl.program_id(0)
    return (i > 0).astype(F32), (i < nt - 1).astype(F32)


def _conv_fwd(p_ref, prev_ref, next_ref, cw_ref, tm, has_prev, has_next):
    ca = p_ref[:, OFF_CA:OFF_HA]
    ha = p_ref[:, OFF_HA:OFF_UB]
    ch = ca * ha
    ch_prev = prev_ref[HALO - 1:HALO, OFF_CA:OFF_HA] * prev_ref[HALO - 1:HALO, OFF_HA:OFF_UB] * has_prev
    ch_next = next_ref[0:1, OFF_CA:OFF_HA] * next_ref[0:1, OFF_HA:OFF_UB] * has_next
    row = lax.broadcasted_iota(jnp.int32, (tm, W_A), 0)
    ch_m1 = jnp.where(row == 0, ch_prev, pltpu.roll(ch, 1, 0))
    ch_p1 = jnp.where(row == tm - 1, ch_next, pltpu.roll(ch, tm - 1, 0))
    cv = cw_ref[0:1, :] * ch_m1 + cw_ref[1:2, :] * ch + cw_ref[2:3, :] * ch_p1
    return ca, ha, ch, ch_m1, ch_p1, cv


def _spatial_fwd(p_ref, vg_ref, vb_ref, ws_ref, bsf_ref, vnb_ref, mixed_ref, tm):
    vb_pre = p_ref[:, OFF_VB:OFF_GA]
    gv, tv = _gelu(vb_pre)
    xhv, rstdv = _ln_stats(gv)
    vnb_ref[...] = (xhv * vg_ref[...] + vb_ref[...]).astype(BF)
    for c in range(tm // CHUNK):
        rows = slice(c * CHUNK, (c + 1) * CHUNK)
        for h in range(N_HEAD):
            cols = slice(h * CHUNK, (h + 1) * CHUNK)
            mixed_ref[rows, cols] = _dot(ws_ref[h], vnb_ref[rows, cols]) + bsf_ref[:, cols]
    return vb_pre, tv, xhv, rstdv


def _mix_fwd(p, pedge, x, wpa, wpb, wo, wsb, bsf, bg, cw, vg, vb, tm):
    t = x.shape[0]
    nt = t // tm

    def body(p_ref, prev_ref, next_ref, x_ref, wpa_ref, wpb_ref, wo_ref, ws_ref, bsf_ref, bg_ref, cw_ref, vg_ref, vb_ref,
             r1_ref, ya_ref, yb_ref, vnb_ref, mixed_ref):
        has_prev, has_next = _end_masks(nt)
        _, _, _, _, _, cv = _conv_fwd(p_ref, prev_ref, next_ref, cw_ref, tm, has_prev, has_next)
        a = p_ref[:, 0:OFF_CA] * cv
        ya = _dot(a.astype(BF), wpa_ref[...])
        ya_ref[...] = ya
        _spatial_fwd(p_ref, vg_ref, vb_ref, ws_ref, bsf_ref, vnb_ref, mixed_ref, tm)
        gu, _ = _gelu(p_ref[:, OFF_UB:OFF_VB])
        bb = gu * mixed_ref[...]
        yb = _dot(bb.astype(BF), wpb_ref[...])
        yb_ref[...] = yb
        ga = jax.nn.sigmoid(p_ref[:, OFF_GA:OFF_GB] + bg_ref[:, 0:D])
        gb = jax.nn.sigmoid(p_ref[:, OFF_GB:N_PROJ] + bg_ref[:, D:2 * D])
        z = ga * ya + gb * yb
        r1_ref[...] = ALPHA * x_ref[...] + _dot(z.astype(BF), wo_ref[...])

    tile = lambda w: pl.BlockSpec((tm, w), lambda i: (i, 0))
    return _pc(
        body, name="mix_fwd", grid=(nt,),
        in_specs=[tile(N_PROJ), *_edge_specs(t, tm, N_PROJ), tile(D),
                  _resident((W_A, D)), _resident((W_B, D)), _resident((D, D)), _resident((N_HEAD, CHUNK, CHUNK)),
                  _resident((CHUNK, W_B)), _resident((1, 2 * D)), _resident((3, W_A)), _resident((1, W_B)),
                  _resident((1, W_B))],
        out_specs=[tile(D), tile(D), tile(D)],
        out_shape=[jax.ShapeDtypeStruct((t, D), F32)] * 3,
        scratch_shapes=[pltpu.VMEM((tm, W_B), BF), pltpu.VMEM((tm, W_B), F32)],
        compiler_params=_params("arbitrary"),
    )(p, pedge, pedge, x, wpa, wpb, wo, wsb, bsf, bg, cw, vg, vb)


def _ffn_fwd_bwd(r1, tgt, wff1, wff2, ln1g, ln1b, ln2g, ln2b, tm):
    t = r1.shape[0]

    def body(r1_ref, tgt_ref, w1_ref, w2_ref, g1_ref, b1_ref, g2_ref, b2_ref,
             dr1_ref, dr1b_ref, dedge_ref, x1b_ref, hidb_ref, dh1b_ref, dr2b_ref, acc_ref, relu_ref):
        @pl.when(pl.program_id(0) == 0)
        def _():
            acc_ref[...] = jnp.zeros_like(acc_ref)

        xh1, rstd1 = _ln_stats(r1_ref[...])
        x1 = xh1 * g1_ref[...] + b1_ref[...]
        x1b_ref[...] = x1.astype(BF)
        ffn = jnp.zeros((tm, D), F32)
        for j in range(N_CHIP):
            cols = slice(j * FF_SHARD, (j + 1) * FF_SHARD)
            r = jnp.maximum(_dot(x1b_ref[...], w1_ref[j]), 0.0)
            relu_ref[:, cols] = r
            hidb_ref[:, cols] = (r * r).astype(BF)
            ffn = ffn + _dot(hidb_ref[:, cols], w2_ref[cols, :])
        xh2, rstd2 = _ln_stats(ALPHA * x1 + ffn)
        diff = xh2 * g2_ref[...] + b2_ref[...] - tgt_ref[...]
        acc_ref[4:5, :] += _colsum(diff * diff)
        dx2 = diff * (1.0 / D)
        acc_ref[2:3, :] += _colsum(dx2 * xh2)
        acc_ref[3:4, :] += _colsum(dx2)
        dr2 = _ln_bwd(dx2, g2_ref[...], xh2, rstd2)
        dr2b_ref[...] = dr2.astype(BF)
        dx1 = ALPHA * dr2
        for j in range(N_CHIP):
            cols = slice(j * FF_SHARD, (j + 1) * FF_SHARD)
            dhid = _dot_nt(dr2b_ref[...], w2_ref[cols, :])
            dh1b_ref[:, cols] = (dhid * (2.0 * relu_ref[:, cols])).astype(BF)
            dx1 = dx1 + _dot_nt(dh1b_ref[:, cols], w1_ref[j])
        acc_ref[0:1, :] += _colsum(dx1 * xh1)
        acc_ref[1:2, :] += _colsum(dx1)
        dr1 = _ln_bwd(dx1, g1_ref[...], xh1, rstd1)
        dr1_ref[...] = dr1
        dr1b_ref[...] = dr1.astype(BF)
        _write_edges(dedge_ref, dr1_ref, tm)

    tile = lambda w: pl.BlockSpec((tm, w), lambda i: (i, 0))
    vec = _resident((1, D))
    return _pc(
        body, name="ffn_fwd_bwd", grid=(t // tm,),
        in_specs=[tile(D), tile(D), _resident((N_CHIP, D, FF_SHARD)), _resident((D_FF, D)), vec, vec, vec, vec],
        out_specs=[tile(D), tile(D), pl.BlockSpec((tm // EDGE_TILE, 2 * HALO, D), lambda i: (i, 0, 0)), tile(D), tile(D_FF),
                   tile(D_FF), tile(D), pl.BlockSpec((8, D), lambda i: (0, 0))],
        out_shape=[jax.ShapeDtypeStruct((t, D), F32), jax.ShapeDtypeStruct((t, D), BF),
                   jax.ShapeDtypeStruct((t // EDGE_TILE, 2 * HALO, D), F32), jax.ShapeDtypeStruct((t, D), BF),
                   jax.ShapeDtypeStruct((t, D_FF), BF), jax.ShapeDtypeStruct((t, D_FF), BF),
                   jax.ShapeDtypeStruct((t, D), BF), jax.ShapeDtypeStruct((8, D), F32)],
        scratch_shapes=[pltpu.VMEM((tm, D_FF), F32)],
        compiler_params=_params("arbitrary"),
    )(r1, tgt, wff1, wff2, ln1g, ln1b, ln2g, ln2b)


def _dw(a, b, nblk, am, bn, a_blocked, b_blocked, tk, name, comm=None):
    t = a.shape[0]

    def body(a_ref, b_ref, o_ref):
        @pl.when(pl.program_id(1) == 0)
        def _():
            o_ref[...] = jnp.zeros_like(o_ref)

        o_ref[...] += _dot_tn(a_ref[...].astype(BF), b_ref[...])

    outs, got = _host_call(
        body, comm, name=name, grid=(nblk, t // tk),
        in_specs=[pl.BlockSpec((tk, am), (lambda j, k: (k, j)) if a_blocked else (lambda j, k: (k, 0))),
                  pl.BlockSpec((tk, bn), (lambda j, k: (k, j)) if b_blocked else (lambda j, k: (k, 0)))],
        out_specs=[pl.BlockSpec((None, am, bn), lambda j, k: (j, 0, 0))],
        out_shape=[jax.ShapeDtypeStruct((nblk, am, bn), F32)], args=(a, b))
    return outs[0] if comm is None else (outs[0], got)


def _dx(dp, win4, dr1, tm, blk0, nblk, filled, name, comm):
    t = dp.shape[0]

    def body(dp_ref, w_ref, dr1_ref, *rest):
        dx = ALPHA * dr1_ref[...]
        for j in range(N_CHIP):
            dx = dx + _dot_nt(dp_ref[:, j * NP_SHARD:(j + 1) * NP_SHARD], w_ref[j])
        rest[-1][...] = dx

    in_specs = [pl.BlockSpec((tm, N_PROJ), lambda i: (i + blk0, 0)), _resident((N_CHIP, D, NP_SHARD)),
                pl.BlockSpec((tm, D), lambda i: (i + blk0, 0))]
    args = (dp, win4, dr1)
    aliases = None
    if filled is not None:
        in_specs.append(pl.BlockSpec(memory_space=pl.ANY))
        args += (filled,)
        aliases = {3: 0}
    outs, got = _host_call(
        body, comm, name=name, grid=(nblk,), in_specs=in_specs, out_specs=[pl.BlockSpec((tm, D), lambda i: (i + blk0, 0))],
        out_shape=[jax.ShapeDtypeStruct((t, D), F32)], args=args, aliases=aliases)
    return outs[0], got


def _mix_bwd(p, pedge, dr1, dedge, ya, yb, wpa, wpb, wo, wsb, wstb, bsf, bg, cw, vg, vb, tm, comm):
    t = p.shape[0]
    nt = t // tm
    te = tm + 2 * HALO
    mid = slice(HALO, HALO + tm)

    def body(p_ref, prev_ref, next_ref, dr1_ref, dprev_ref, dnext_ref, ya_ref, yb_ref, wpa_ref, wpb_ref, wo_ref,
             ws_ref, wst_ref, bsf_ref, bg_ref, cw_ref, vg_ref, vb_ref,
             dp_ref, ab_ref, bbb_ref, zb_ref, dyab_ref, dybb_ref, dbg_ref, dcw_ref, dvgb_ref, dws_ref, dbs_ref,
             vnb_ref, mixed_ref, dmixb_ref, dvn_ref):
        @pl.when(pl.program_id(0) == 0)
        def _():
            for r in (dbg_ref, dcw_ref, dvgb_ref, dws_ref, dbs_ref):
                r[...] = jnp.zeros_like(r)

        has_prev, has_next = _end_masks(nt)
        ca, ha, ch, ch_m1, ch_p1, cv = _conv_fwd(p_ref, prev_ref, next_ref, cw_ref, tm, has_prev, has_next)
        ba = p_ref[:, 0:OFF_CA]
        ab_ref[...] = (ba * cv).astype(BF)
        vb_pre, tv, xhv, rstdv = _spatial_fwd(p_ref, vg_ref, vb_ref, ws_ref, bsf_ref, vnb_ref, mixed_ref, tm)
        ub = p_ref[:, OFF_UB:OFF_VB]
        gu, tu = _gelu(ub)
        bbb_ref[...] = (gu * mixed_ref[...]).astype(BF)
        bga = bg_ref[:, 0:D]
        ga = jax.nn.sigmoid(p_ref[:, OFF_GA:OFF_GB] + bga)
        gb = jax.nn.sigmoid(p_ref[:, OFF_GB:N_PROJ] + bg_ref[:, D:2 * D])
        ya = ya_ref[...]
        yb = yb_ref[...]
        zb_ref[...] = (ga * ya + gb * yb).astype(BF)

        dr1_ext = jnp.concatenate([dprev_ref[...] * has_prev, dr1_ref[...], dnext_ref[...] * has_next], axis=0)
        dz_ext = _dot_nt(dr1_ext.astype(BF), wo_ref[...])
        ga_ext = jnp.concatenate([jax.nn.sigmoid(prev_ref[:, OFF_GA:OFF_GB] + bga), ga,
                                  jax.nn.sigmoid(next_ref[:, OFF_GA:OFF_GB] + bga)], axis=0)
        dya_ext = dz_ext * ga_ext
        dyab_ref[...] = dya_ext[mid].astype(BF)
        da_ext = _dot_nt(dya_ext.astype(BF), wpa_ref[...])
        ba_ext = jnp.concatenate([prev_ref[:, 0:OFF_CA], ba, next_ref[:, 0:OFF_CA]], axis=0)
        dcv_ext = da_ext * ba_ext
        dcv = dcv_ext[mid]
        dch = (cw_ref[0:1, :] * pltpu.roll(dcv_ext, te - 1, 0)[mid] + cw_ref[1:2, :] * dcv
               + cw_ref[2:3, :] * pltpu.roll(dcv_ext, 1, 0)[mid])
        dp_ref[:, 0:OFF_CA] = (da_ext[mid] * cv).astype(BF)
        dp_ref[:, OFF_CA:OFF_HA] = (dch * ha).astype(BF)
        dp_ref[:, OFF_HA:OFF_UB] = (dch * ca).astype(BF)
        dcw_ref[0:1, :] += _colsum(dcv * ch_m1)
        dcw_ref[1:2, :] += _colsum(dcv * ch)
        dcw_ref[2:3, :] += _colsum(dcv * ch_p1)

        dz = dz_ext[mid]
        dga = dz * ya * ga * (1.0 - ga)
        dgb = dz * yb * gb * (1.0 - gb)
        dp_ref[:, OFF_GA:OFF_GB] = dga.astype(BF)
        dp_ref[:, OFF_GB:N_PROJ] = dgb.astype(BF)
        dbg_ref[0:1, 0:D] += _colsum(dga)
        dbg_ref[0:1, D:2 * D] += _colsum(dgb)

        dybb_ref[...] = (dz * gb).astype(BF)
        dbb = _dot_nt(dybb_ref[...], wpb_ref[...])
        dp_ref[:, OFF_UB:OFF_VB] = (dbb * mixed_ref[...] * _gelu_grad(ub, tu)).astype(BF)
        dmixed = dbb * gu
        dmixb_ref[...] = dmixed.astype(BF)
        for c in range(tm // CHUNK):
            rows = slice(c * CHUNK, (c + 1) * CHUNK)
            dbs_ref[...] += dmixed[rows]
            for h in range(N_HEAD):
                cols = slice(h * CHUNK, (h + 1) * CHUNK)
                dws_ref[h] += _dot_nt(dmixb_ref[rows, cols], vnb_ref[rows, cols])
                dvn_ref[rows, cols] = _dot(wst_ref[h], dmixb_ref[rows, cols])
        dvn = dvn_ref[...]
        dvgb_ref[0:1, :] += _colsum(dvn * xhv)
        dvgb_ref[1:2, :] += _colsum(dvn)
        dgv = _ln_bwd(dvn, vg_ref[...], xhv, rstdv)
        dp_ref[:, OFF_VB:OFF_GA] = (dgv * _gelu_grad(vb_pre, tv)).astype(BF)

    tile = lambda w: pl.BlockSpec((tm, w), lambda i: (i, 0))
    acc = lambda *s: pl.BlockSpec(s, lambda i: (0,) * len(s))
    return _host_call(
        body, comm, name="mix_bwd", grid=(nt,),
        in_specs=[tile(N_PROJ), *_edge_specs(t, tm, N_PROJ), tile(D), *_edge_specs(t, tm, D), tile(D), tile(D),
                  _resident((W_A, D)), _resident((W_B, D)), _resident((D, D)), _resident((N_HEAD, CHUNK, CHUNK)),
                  _resident((N_HEAD, CHUNK, CHUNK)), _resident((CHUNK, W_B)), _resident((1, 2 * D)),
                  _resident((3, W_A)), _resident((1, W_B)), _resident((1, W_B))],
        out_specs=[tile(N_PROJ), tile(W_A), tile(W_B), tile(D), tile(D), tile(D),
                   acc(8, 2 * D), acc(8, W_A), acc(8, W_B), acc(N_HEAD, CHUNK, CHUNK), acc(CHUNK, W_B)],
        out_shape=[jax.ShapeDtypeStruct((t, N_PROJ), BF), jax.ShapeDtypeStruct((t, W_A), BF),
                   jax.ShapeDtypeStruct((t, W_B), BF), jax.ShapeDtypeStruct((t, D), BF), jax.ShapeDtypeStruct((t, D), BF),
                   jax.ShapeDtypeStruct((t, D), BF), jax.ShapeDtypeStruct((8, 2 * D), F32),
                   jax.ShapeDtypeStruct((8, W_A), F32), jax.ShapeDtypeStruct((8, W_B), F32),
                   jax.ShapeDtypeStruct((N_HEAD, CHUNK, CHUNK), F32), jax.ShapeDtypeStruct((CHUNK, W_B), F32)],
        scratch_shapes=[pltpu.VMEM((tm, W_B), BF), pltpu.VMEM((tm, W_B), F32), pltpu.VMEM((tm, W_B), BF),
                        pltpu.VMEM((tm, W_B), F32)],
        args=(p, pedge, pedge, dr1, dedge, dedge, ya, yb, wpa, wpb, wo, wsb, wstb, bsf, bg, cw, vg, vb))


def _add_own_half(full4, recv4, core, rb, name):
    n, rh, cols = recv4.shape
    nb = rh // rb

    def body(c_ref, a_ref, b_ref, o_ref, ob_ref):
        s = a_ref[...] + b_ref[...]
        o_ref[...] = s
        ob_ref[...] = s.astype(BF)

    blk = (None, rb, cols)
    same = pl.BlockSpec(blk, lambda k, i, c: (k, i, 0))
    return _pc(
        body, name=name,
        grid_spec=pltpu.PrefetchScalarGridSpec(
            num_scalar_prefetch=1, grid=(n, nb),
            in_specs=[pl.BlockSpec(blk, lambda k, i, c: (k, c[0] * nb + i, 0)), same], out_specs=[same, same]),
        out_shape=[jax.ShapeDtypeStruct(recv4.shape, F32), jax.ShapeDtypeStruct(recv4.shape, BF)],
        compiler_params=_params("arbitrary", "arbitrary"),
    )(core, full4, recv4)


def _add_chips(s4, r3, place, rb, name):
    _, rh, cols = r3.shape
    nb = rh // rb

    def body(pl_ref, s_ref, r_ref, o_ref):
        o_ref[...] = ((s_ref[...] + r_ref[0].astype(F32)) + r_ref[1].astype(F32)) + r_ref[2].astype(F32)

    return _pc(
        body, name=name,
        grid_spec=pltpu.PrefetchScalarGridSpec(
            num_scalar_prefetch=1, grid=(nb,),
            in_specs=[pl.BlockSpec((None, rb, cols), lambda i, s: (s[0], i, 0)), pl.BlockSpec((3, rb, cols), lambda i, s: (0, i, 0))],
            out_specs=pl.BlockSpec((rb, cols), lambda i, s: (s[1] * nb + i, 0))),
        out_shape=jax.ShapeDtypeStruct((2 * rh, cols), F32),
        compiler_params=_params("arbitrary"),
    )(place, s4, r3)


def _add_small(a, b):
    def body(a_ref, b_ref, o_ref):
        o_ref[...] = a_ref[...] + b_ref[...]

    return _pc(body, name="add_small_cores", out_shape=jax.ShapeDtypeStruct(a.shape, F32))(a, b)


def _sum_small_chips(own, slots, place):
    def body(pl_ref, own_ref, s_ref, o_ref):
        j = pl_ref[0]

        def term(k):
            return jnp.where(j == k, own_ref[...], s_ref[jnp.maximum((j ^ k) - 1, 0)])

        o_ref[...] = ((term(0) + term(1)) + term(2)) + term(3)

    vmem = pl.BlockSpec(memory_space=pltpu.VMEM)
    return _pc(body, name="sum_small_chips", in_specs=[pl.BlockSpec(memory_space=pltpu.SMEM), vmem, vmem], out_specs=vmem,
               out_shape=jax.ShapeDtypeStruct(own.shape, F32))(place, own, slots)


def _adamw(w, g, m, v, rb, name):
    rows, cols = w.shape

    def body(w_ref, g_ref, m_ref, v_ref, d_ref, m2_ref, v2_ref):
        g_ = g_ref[...]
        m2 = ADAM_B1 * m_ref[...] + (1.0 - ADAM_B1) * g_
        v2 = ADAM_B2 * v_ref[...] + (1.0 - ADAM_B2) * (g_ * g_)
        m_hat = m2 / (1.0 - ADAM_B1 ** ADAM_STEP)
        v_hat = v2 / (1.0 - ADAM_B2 ** ADAM_STEP)
        d_ref[...] = -ADAM_LR * (m_hat / (jnp.sqrt(v_hat) + ADAM_EPS) + ADAM_WD * w_ref[...])
        m2_ref[...] = m2
        v2_ref[...] = v2

    blk = pl.BlockSpec((rb, cols), lambda i: (i, 0))
    return _pc(body, name=name, grid=(rows // rb,), in_specs=[blk] * 4, out_specs=[blk] * 3,
               out_shape=[jax.ShapeDtypeStruct((rows, cols), F32)] * 3, compiler_params=_params("arbitrary"))(w, g, m, v)


LANES = 128
SMALL_GRADS = (("b_gate", 2 * D), ("conv_w", 3 * W_A), ("v_norm_g", W_B), ("v_norm_b", W_B),
               ("w_s", N_HEAD * CHUNK * CHUNK), ("b_s", N_HEAD * CHUNK), ("ln1_g", D), ("ln1_b", D), ("ln2_g", D), ("ln2_b", D),
               ("loss", 1))


def _pack_rows(parts):
    rows = []
    for a in parts:
        a = a.reshape(-1)
        a = jnp.pad(a, (0, (-a.shape[0]) % LANES))
        rows.append(a.reshape(-1, LANES))
    out = jnp.concatenate(rows, axis=0)
    return jnp.pad(out, ((0, (-out.shape[0]) % 8), (0, 0)))


def _unpack_rows(buf, sizes):
    out, r = [], 0
    for n in sizes:
        nr = -(-n // LANES)
        out.append(buf[r:r + nr].reshape(-1)[:n])
        r += nr
    return out


TM_PROJ = 1024
TM_MIX = 256
TM_DX = 512
DX_PAIR = 6
DX_ALONE = 4
TK_DW = 2048
TK_DW_IN = 1024
RB_ADD = 64
RB_ADAM = 128
CONV_ROWS = 8


def _reduce_adds_1(grads, recvs, core, tag):
    out = [_add_own_half(g, r, core, RB_ADD if r.shape[1] % 256 else 256, f"add_cores_{tag}{a}")
           for a, (g, r) in enumerate(zip(grads, recvs))]
    return [o[0] for o in out], [o[1] for o in out]


def _reduce_adds_2(sums, recvs, place, tag):
    return [_add_chips(s, r, place, RB_ADD if r.shape[1] % 256 else 256, f"add_chips_{tag}{a}")
            for a, (s, r) in enumerate(zip(sums, recvs))]


def kernel(x, w_in, b_gate, conv_w, v_norm_g, v_norm_b, w_s, b_s, w_pa, w_pb, w_o, ln1_g, ln1_b, w_ff1, w_ff2, ln2_g, ln2_b, loss_target, m_w_in, m_b_gate, m_conv_w, m_v_norm_g, m_v_norm_b, m_w_s, m_b_s, m_w_pa, m_w_pb, m_w_o, m_ln1_g, m_ln1_b, m_w_ff1, m_w_ff2, m_ln2_g, m_ln2_b, v_w_in, v_b_gate, v_conv_w, v_v_norm_g, v_v_norm_b, v_w_s, v_b_s, v_w_pa, v_w_pb, v_w_o, v_ln1_g, v_ln1_b, v_w_ff1, v_w_ff2, v_ln2_g, v_ln2_b):
    t = x.shape[1]
    core = lax.axis_index("c").astype(jnp.int32).reshape(1)
    chip_idx = 2 * lax.axis_index("x") + lax.axis_index("y")
    chip = chip_idx.astype(jnp.int32).reshape(1)
    place = jnp.concatenate([chip, core])
    x2 = x.reshape(t, D)
    tgt = loss_target.reshape(t, D)

    win4, proj4, ff14, ff24 = _cast_shards(w_in[0], w_pa[0], w_pb[0], w_o[0], w_ff1[0], w_ff2[0], chip)
    conv4 = lax.dynamic_update_slice(jnp.zeros((N_CHIP, CONV_ROWS, W_A // N_CHIP), F32),
                                     jnp.pad(conv_w[0], ((0, CONV_ROWS - 3), (0, 0)))[None], (chip_idx, 0, 0))
    (p, pedge), (win4, proj4, ff14, ff24, conv4) = _proj_fwd(
        x2, chip, TM_PROJ, _gather_comm([win4, proj4, ff14, ff24], conv4, eager=1))

    def full(name, rows_total):
        off, rows = PROJ_OFF[name]
        return proj4[:, off:off + rows, :].reshape(rows_total, D)

    wpa, wpb, wo = full("w_pa", W_A), full("w_pb", W_B), full("w_o", D)
    wff2 = ff24.reshape(D_FF, D)
    cw = jnp.transpose(conv4[:, :3, :], (1, 0, 2)).reshape(3, W_A)
    wsb = w_s[0].astype(BF)
    wstb = jnp.swapaxes(w_s[0], 1, 2).astype(BF)
    bsf = jnp.repeat(jnp.transpose(b_s[0]), CHUNK, axis=1)

    r1, ya, yb = _mix_fwd(p, pedge, x2, wpa, wpb, wo, wsb, bsf, b_gate, cw, v_norm_g, v_norm_b, TM_MIX)
    dr1, dr1b, dedge, x1b, hidb, dh1b, dr2b, acc = _ffn_fwd_bwd(r1, tgt, ff14, wff2, ln1_g, ln1_b, ln2_g, ln2_b, TM_MIX)
    g_ff = [_dw(x1b, dh1b, N_CHIP, D, FF_SHARD, False, True, TK_DW, "dw_ff1"),
            _dw(hidb, dr2b, N_CHIP, FF_SHARD, D, True, False, TK_DW, "dw_ff2")]
    (dp, ab, bbb, zb, dyab, dybb, dbg, dcw, dvgb, dws, dbs_sum), r_ff = _mix_bwd(
        p, pedge, dr1, dedge, ya, yb, wpa, wpb, wo, wsb, wstb, bsf, b_gate, cw, v_norm_g, v_norm_b, TM_MIX, _pair_comm(g_ff))
    s_ff, sb_ff = _reduce_adds_1(g_ff, r_ff, core, "ff")
    dwin4, c_ff = _dw(x2, dp, N_CHIP, D, NP_SHARD, False, True, TK_DW_IN, "dw_in", _chips_comm(sb_ff))
    f_ff = _reduce_adds_2(s_ff, c_ff, place, "ff")
    dwpa, (g_ff1, g_ff2) = _dw(ab, dyab, 1, W_A, D, False, False, TK_DW, "dw_pa", _join_comm(f_ff))
    dwpb = _dw(bbb, dybb, 1, W_B, D, False, False, TK_DW, "dw_pb")
    dwo = _dw(zb, dr1b, 1, D, D, False, False, TK_DW, "dw_o")
    dproj4 = jnp.concatenate([dwpa.reshape(N_CHIP, -1, D), dwpb.reshape(N_CHIP, -1, D), dwo.reshape(N_CHIP, -1, D)], axis=1)
    dbs = jnp.transpose(jnp.sum(dbs_sum.reshape(CHUNK, N_HEAD, CHUNK), axis=-1))
    small = _pack_rows([dbg[0], dcw[0:3], dvgb[0], dvgb[1], dws, dbs, acc[0], acc[1], acc[2], acc[3],
                        0.5 * jnp.sum(acc[4]) / D])
    g_rest = [dwin4, dproj4]
    nblk = t // TM_DX
    n_a, n_c = max(1, min(DX_PAIR, nblk // 4)), max(1, min(DX_ALONE, nblk // 4))
    n_b = nblk - n_a - n_c
    dx, r_rest = _dx(dp, win4, dr1, TM_DX, 0, n_a, None, "dx_a", _pair_comm(g_rest, small))
    s_rest, sb_rest = _reduce_adds_1(g_rest, r_rest[:2], core, "rest")
    csmall = _add_small(small, r_rest[2])
    dx, c_rest = _dx(dp, win4, dr1, TM_DX, n_a, n_b, dx, "dx_b", _chips_comm(sb_rest, csmall))
    dx, _ = _dx(dp, win4, dr1, TM_DX, n_a + n_b, n_c, dx, "dx_c", None)
    f_rest = _reduce_adds_2(s_rest, c_rest[:2], place, "rest")
    gsmall = _sum_small_chips(csmall, c_rest[2], place)
    g_in, g_proj = _comm_call(_join_comm(f_rest), "join_rest")

    grads = {"w_in": g_in, "w_ff1": g_ff1, "w_ff2": g_ff2}
    for name, _ in PROJ_ROWS:
        off, rows = PROJ_OFF[name]
        grads[name] = g_proj[off:off + rows, :]
    for (name, n), flat in zip(SMALL_GRADS, _unpack_rows(gsmall, [n for _, n in SMALL_GRADS])):
        grads[name] = flat
    loss = grads.pop("loss").reshape(())
    grads["conv_w"] = lax.dynamic_slice(grads["conv_w"].reshape(3, W_A), (0, chip_idx * (W_A // N_CHIP)), (3, W_A // N_CHIP))

    weights = dict(w_in=w_in, b_gate=b_gate, conv_w=conv_w, v_norm_g=v_norm_g, v_norm_b=v_norm_b, w_s=w_s, b_s=b_s,
                   w_pa=w_pa, w_pb=w_pb, w_o=w_o, ln1_g=ln1_g, ln1_b=ln1_b, w_ff1=w_ff1, w_ff2=w_ff2, ln2_g=ln2_g, ln2_b=ln2_b)
    mom1 = dict(w_in=m_w_in, b_gate=m_b_gate, conv_w=m_conv_w, v_norm_g=m_v_norm_g, v_norm_b=m_v_norm_b, w_s=m_w_s,
                b_s=m_b_s, w_pa=m_w_pa, w_pb=m_w_pb, w_o=m_w_o, ln1_g=m_ln1_g, ln1_b=m_ln1_b, w_ff1=m_w_ff1,
                w_ff2=m_w_ff2, ln2_g=m_ln2_g, ln2_b=m_ln2_b)
    mom2 = dict(w_in=v_w_in, b_gate=v_b_gate, conv_w=v_conv_w, v_norm_g=v_v_norm_g, v_norm_b=v_v_norm_b, w_s=v_w_s,
                b_s=v_b_s, w_pa=v_w_pa, w_pb=v_w_pb, w_o=v_w_o, ln1_g=v_ln1_g, ln1_b=v_ln1_b, w_ff1=v_w_ff1,
                w_ff2=v_w_ff2, ln2_g=v_ln2_g, ln2_b=v_ln2_b)
    order = list(weights)
    big = ("w_in", "w_pa", "w_pb", "w_o", "w_ff1", "w_ff2")
    delta, new_m, new_v = {}, {}, {}
    for name in big:
        w2 = weights[name][0]
        delta[name], new_m[name], new_v[name] = _adamw(w2, grads[name], mom1[name][0], mom2[name][0], RB_ADAM, "adamw_" + name)
    little = [n for n in order if n not in big]
    sizes = [weights[n].size for n in little]
    wsmall = _pack_rows([weights[n] for n in little])
    ds, ms, vs = _adamw(wsmall, _pack_rows([grads[n] for n in little]), _pack_rows([mom1[n] for n in little]),
                        _pack_rows([mom2[n] for n in little]), wsmall.shape[0], "adamw_small")
    for name, d_, m_, v_ in zip(little, _unpack_rows(ds, sizes), _unpack_rows(ms, sizes), _unpack_rows(vs, sizes)):
        delta[name], new_m[name], new_v[name] = d_, m_, v_

    shaped = lambda d: [d[n].reshape(weights[n].shape) for n in order]
    return (loss, dx.reshape(x.shape), *shaped(grads), *shaped(delta), *shaped(new_m), *shaped(new_v))
```

```python
import functools
from typing import NamedTuple

import jax
import jax.numpy as jnp
from jax import lax
from jax.experimental import pallas as pl
from jax.experimental.pallas import tpu as pltpu

D = 1024
W_A = 1536
W_B = 1024
CHUNK = 128
N_HEAD = 8
D_FF = 4096
N_PROJ = 3 * W_A + 2 * W_B + 2 * D
OFF_CA, OFF_HA, OFF_UB, OFF_VB, OFF_GA, OFF_GB = 1536, 3072, 4608, 5632, 6656, 7680
LN_EPS = 1e-5
ALPHA = 2.0 ** 0.25
N_CHIP = 4
NP_SHARD = N_PROJ // N_CHIP
FF_SHARD = D_FF // N_CHIP
ADAM_LR, ADAM_B1, ADAM_B2, ADAM_EPS, ADAM_WD, ADAM_STEP = 0.001, 0.9, 0.999, 1e-08, 0.01, 10

PROJ_ROWS = (("w_pa", W_A // N_CHIP), ("w_pb", W_B // N_CHIP), ("w_o", D // N_CHIP))
PROJ_OFF = {}
_o = 0
for _n, _r in PROJ_ROWS:
    PROJ_OFF[_n] = (_o, _r)
    _o += _r
PROJ_TOTAL = _o

V7X_VMEM_BYTES = 64 * 1024 * 1024
VMEM_LIMIT = 56 * 1024 * 1024
HALO = 8
EDGE_TILE = 128

BF = jnp.bfloat16
F32 = jnp.float32
MESH = pl.DeviceIdType.MESH
HBM_SPEC = pl.BlockSpec(memory_space=pltpu.HBM)


def _pc(body, **kw):
    return pl.pallas_call(body, **kw)


def _params(*sem):
    return pltpu.CompilerParams(dimension_semantics=sem, vmem_limit_bytes=VMEM_LIMIT)


def _resident(shape):
    n = len(shape)
    return pl.BlockSpec(shape, lambda *_: (0,) * n, pipeline_mode=pl.Buffered(1))


def _dot(a, b):
    return jnp.dot(a, b, preferred_element_type=F32)


def _dot_nt(a, b):
    return lax.dot_general(a, b, (((1,), (1,)), ((), ())), preferred_element_type=F32)


def _dot_tn(a, b):
    return lax.dot_general(a, b, (((0,), (0,)), ((), ())), preferred_element_type=F32)


def _gelu(x):
    t = jnp.tanh(0.7978845608028654 * (x + 0.044715 * (x * x * x)))
    return 0.5 * x * (1.0 + t), t


def _gelu_grad(x, t):
    return 0.5 * (1.0 + t) + 0.5 * x * (1.0 - t * t) * (0.7978845608028654 * (1.0 + 0.134145 * (x * x)))


def _ln_stats(r):
    mu = jnp.mean(r, axis=-1, keepdims=True)
    xc = r - mu
    var = jnp.mean(xc * xc, axis=-1, keepdims=True)
    rstd = lax.rsqrt(var + LN_EPS)
    return xc * rstd, rstd


def _ln_bwd(dy, g, xh, rstd):
    dxh = dy * g
    m1 = jnp.mean(dxh, axis=-1, keepdims=True)
    m2 = jnp.mean(dxh * xh, axis=-1, keepdims=True)
    return rstd * (dxh - m1 - xh * m2)


def _colsum(v):
    return jnp.sum(v, axis=0, keepdims=True)


class _Comm(NamedTuple):
    args: tuple
    out_shape: tuple
    aliases: dict
    n_sems: int
    stages: tuple


def _place():
    x, y, c = lax.axis_index("x"), lax.axis_index("y"), lax.axis_index("c")
    return x, y, c, 2 * x + y


def _flip(x, y, c, r):
    return (x ^ (r >> 1), y ^ (r & 1), c)


def _remote(src, dst, send_sems, recv_sems, k, peer):
    return pltpu.make_async_remote_copy(src_ref=src, dst_ref=dst, send_sem=send_sems.at[k], recv_sem=recv_sems.at[k],
                                        device_id=peer, device_id_type=MESH)


def _host_call(body, comm, *, name, grid, in_specs, out_specs, out_shape, args, scratch_shapes=(), aliases=None, prefetch=None,
               body_reads_comm=False):
    sem = ("arbitrary",) * len(grid)
    aliases = dict(aliases or {})
    n_pre = 0 if prefetch is None else 1
    n_in, n_out, n_scr = len(in_specs), len(out_specs), len(scratch_shapes)
    c_in, c_out = (0, 0) if comm is None else (len(comm.args), len(comm.out_shape))
    steps = {"first": (0,) * len(grid), "late": (grid[0] - 1,) + (0,) * (len(grid) - 1), "last": tuple(g - 1 for g in grid)}

    def wrapped(*refs):
        refs = refs[n_pre:]
        own_in, cin = refs[:n_in], refs[n_in:n_in + c_in]
        o0 = n_in + c_in
        own_out, cout = refs[o0:o0 + n_out], refs[o0 + n_out:o0 + n_out + c_out]
        s0 = o0 + n_out + c_out
        scr, sems = refs[s0:s0 + n_scr], refs[s0 + n_scr:]

        def run(before):
            for phase, fn in () if comm is None else comm.stages:
                at_step = isinstance(phase, tuple)
                if before != (at_step or phase == "first"):
                    continue
                step = phase if at_step else steps[phase]
                cond = pl.program_id(0) == step[0]
                for d in range(1, len(grid)):
                    cond = jnp.logical_and(cond, pl.program_id(d) == step[d])
                pl.when(cond)(functools.partial(fn, cin, cout, *sems))

        run(True)
        if body_reads_comm:
            body(*own_in, *own_out, *scr, comm_refs=cout)
        else:
            body(*own_in, *own_out, *scr)
        run(False)

    in_specs = list(in_specs) + [HBM_SPEC] * c_in
    out_specs = list(out_specs) + [HBM_SPEC] * c_out
    out_shape = list(out_shape) + ([] if comm is None else list(comm.out_shape))
    scratch_shapes = list(scratch_shapes) + ([] if comm is None else [pltpu.SemaphoreType.DMA((comm.n_sems,))] * 2)
    args = tuple(args) + (() if comm is None else tuple(comm.args))
    if comm is not None:
        aliases.update({n_in + i: n_out + o for i, o in comm.aliases.items()})
    aliases = {i + n_pre: o for i, o in aliases.items()}
    if prefetch is None:
        kw = dict(grid=grid, in_specs=in_specs, out_specs=out_specs, scratch_shapes=scratch_shapes)
    else:
        kw = dict(grid_spec=pltpu.PrefetchScalarGridSpec(num_scalar_prefetch=1, grid=grid, in_specs=in_specs,
                                                         out_specs=out_specs, scratch_shapes=scratch_shapes))
        args = (prefetch,) + args
    outs = _pc(wrapped, name=name, out_shape=out_shape, input_output_aliases=aliases, compiler_params=_params(*sem), **kw)(*args)
    return outs[:n_out], outs[n_out:]


def _comm_call(comm, name):
    def body(*refs):
        c_in, c_out = len(comm.args), len(comm.out_shape)
        cin, cout, (send_sems, recv_sems) = refs[:c_in], refs[c_in:c_in + c_out], refs[c_in + c_out:]
        for phase in ("first", "late", "last"):
            for ph, fn in comm.stages:
                if ph == phase:
                    fn(cin, cout, send_sems, recv_sems)

    return _pc(body, name=name, in_specs=[HBM_SPEC] * len(comm.args), out_specs=[HBM_SPEC] * len(comm.out_shape),
               out_shape=list(comm.out_shape), scratch_shapes=[pltpu.SemaphoreType.DMA((comm.n_sems,))] * 2,
               input_output_aliases=dict(comm.aliases))(*comm.args)


def _gather_comm(bufs, whole=None, eager=0):
    n = len(bufs)
    halves = [b.shape[1] // 2 for b in bufs]
    k_ici = lambda a, r: 3 * a + r - 1
    k_d2d = lambda a, r: 3 * n + 3 * a + r - 1
    k_whole = lambda r: 6 * n + r - 1

    def half(ref, slot, c, a):
        return ref.at[slot, pl.ds(c * halves[a], halves[a])]

    def send(cin, cout, ss, rs):
        x, y, c, j = _place()
        for a in range(n):
            mine = half(cout[a], j, c, a)
            for r in (1, 2, 3):
                _remote(mine, mine, ss, rs, k_ici(a, r), _flip(x, y, c, r)).start()
        if whole is not None:
            for r in (1, 2, 3):
                _remote(cout[n].at[j], cout[n].at[j], ss, rs, k_whole(r), _flip(x, y, c, r)).start()

    def pass_on(cout, ss, rs, a, r):
        x, y, c, j = _place()
        landed = half(cout[a], j ^ r, c, a)
        _remote(landed, landed, ss, rs, k_ici(a, r), (x, y, 1 - c)).wait_recv()
        _remote(landed, landed, ss, rs, k_d2d(a, r), (x, y, 1 - c)).start()

    def passed_on(cout, ss, rs, a, r):
        x, y, c, j = _place()
        theirs = half(cout[a], j ^ r, 1 - c, a)
        _remote(theirs, theirs, ss, rs, k_d2d(a, r), (x, y, 1 - c)).wait_recv()

    def arrive(r, cin, cout, ss, rs):
        for a in range(eager):
            pass_on(cout, ss, rs, a, r)
        for a in range(eager):
            passed_on(cout, ss, rs, a, r)

    def forward(cin, cout, ss, rs):
        for a in range(eager, n):
            for r in (1, 2, 3):
                pass_on(cout, ss, rs, a, r)

    def finish(cin, cout, ss, rs):
        x, y, c, j = _place()
        sibling = (x, y, 1 - c)
        for a in range(eager, n):
            for r in (1, 2, 3):
                passed_on(cout, ss, rs, a, r)
        for a in range(n):
            mine = half(cout[a], j, c, a)
            for r in (1, 2, 3):
                _remote(mine, mine, ss, rs, k_ici(a, r), sibling).wait_send()
                landed = half(cout[a], j ^ r, c, a)
                _remote(landed, landed, ss, rs, k_d2d(a, r), sibling).wait_send()
        if whole is not None:
            for r in (1, 2, 3):
                cp = _remote(cout[n].at[j ^ r], cout[n].at[j ^ r], ss, rs, k_whole(r), sibling)
                cp.wait_recv()
                cp.wait_send()

    args = tuple(bufs) + ((whole,) if whole is not None else ())
    out_shape = tuple(jax.ShapeDtypeStruct(b.shape, b.dtype) for b in args)
    aliases = {a: a for a in range(len(args))}
    arrivals = tuple(((r, 0), functools.partial(arrive, r)) for r in (1, 2, 3)) if eager else ()
    return _Comm(args, out_shape, aliases, 6 * n + 3, (("first", send),) + arrivals + (("late", forward), ("last", finish)))


def _pair_comm(grads, small=None):
    n = len(grads)
    halves = [g.shape[1] // 2 for g in grads]

    def copies(cin, cout, ss, rs):
        x, y, c, _ = _place()
        sibling = (x, y, 1 - c)
        cps = [_remote(cin[a].at[:, pl.ds((1 - c) * halves[a], halves[a]), :], cout[a], ss, rs, a, sibling) for a in range(n)]
        if small is not None:
            cps.append(_remote(cin[n], cout[n], ss, rs, n, sibling))
        return cps

    def start(cin, cout, ss, rs):
        for cp in copies(cin, cout, ss, rs):
            cp.start()

    def finish(cin, cout, ss, rs):
        for cp in copies(cin, cout, ss, rs):
            cp.wait()

    args = tuple(grads) + ((small,) if small is not None else ())
    out_shape = tuple(jax.ShapeDtypeStruct((N_CHIP, h, g.shape[2]), F32) for g, h in zip(grads, halves))
    out_shape += (jax.ShapeDtypeStruct(small.shape, F32),) if small is not None else ()
    return _Comm(args, out_shape, {}, n + 1, (("first", start), ("last", finish)))


def _chips_comm(sums_bf, small=None):
    n = len(sums_bf)

    def copies(cin, cout, ss, rs):
        x, y, c, j = _place()
        cps = []
        for r in (1, 2, 3):
            peer = _flip(x, y, c, r)
            for a in range(n):
                cps.append(_remote(cin[a].at[j ^ r], cout[a].at[r - 1], ss, rs, (n + 1) * (r - 1) + a, peer))
            if small is not None:
                cps.append(_remote(cin[n], cout[n].at[r - 1], ss, rs, (n + 1) * (r - 1) + n, peer))
        return cps

    def start(cin, cout, ss, rs):
        for cp in copies(cin, cout, ss, rs):
            cp.start()

    def finish(cin, cout, ss, rs):
        for cp in copies(cin, cout, ss, rs):
            cp.wait()

    args = tuple(sums_bf) + ((small,) if small is not None else ())
    out_shape = tuple(jax.ShapeDtypeStruct((3,) + s.shape[1:], BF) for s in sums_bf)
    out_shape += (jax.ShapeDtypeStruct((3,) + small.shape, F32),) if small is not None else ()
    return _Comm(args, out_shape, {}, 3 * (n + 1), (("first", start), ("last", finish)))


def _join_comm(shards):
    n = len(shards)
    halves = [s.shape[0] // 2 for s in shards]

    def start(cin, cout, ss, rs):
        x, y, c, _ = _place()
        for a in range(n):
            mine = cout[a].at[pl.ds(c * halves[a], halves[a]), :]
            _remote(mine, mine, ss, rs, a, (x, y, 1 - c)).start()

    def finish(cin, cout, ss, rs):
        x, y, c, _ = _place()
        for a in range(n):
            theirs = cout[a].at[pl.ds((1 - c) * halves[a], halves[a]), :]
            cp = _remote(theirs, theirs, ss, rs, a, (x, y, 1 - c))
            cp.wait_recv()
            cp.wait_send()

    out_shape = tuple(jax.ShapeDtypeStruct(s.shape, F32) for s in shards)
    return _Comm(tuple(shards), out_shape, {a: a for a in range(n)}, n, (("first", start), ("last", finish)))


def _cast_shards(w_in, w_pa, w_pb, w_o, w_ff1, w_ff2, chip):
    def body(j_ref, win_ref, wpa_ref, wpb_ref, wo_ref, wff1_ref, wff2_ref, win4_ref, proj4_ref, ff14_ref, ff24_ref):
        win4_ref[...] = win_ref[...].astype(BF)
        for name, ref in (("w_pa", wpa_ref), ("w_pb", wpb_ref), ("w_o", wo_ref)):
            off, rows = PROJ_OFF[name]
            proj4_ref[off:off + rows, :] = ref[...].astype(BF)
        ff14_ref[...] = wff1_ref[...].astype(BF)
        ff24_ref[...] = wff2_ref[...].astype(BF)

    whole = lambda a: pl.BlockSpec(a.shape, lambda i, j: (0, 0), pipeline_mode=pl.Buffered(1))
    slot = lambda rows, cols: pl.BlockSpec((None, rows, cols), lambda i, j: (j[0], 0, 0))
    ws = (w_in, w_pa, w_pb, w_o, w_ff1, w_ff2)
    shapes = ((D, NP_SHARD), (PROJ_TOTAL, D), (D, FF_SHARD), (FF_SHARD, D))
    return _pc(
        body, name="cast_shards",
        grid_spec=pltpu.PrefetchScalarGridSpec(num_scalar_prefetch=1, grid=(1,), in_specs=[whole(w) for w in ws],
                                               out_specs=[slot(*s) for s in shapes]),
        out_shape=[jax.ShapeDtypeStruct((N_CHIP,) + s, BF) for s in shapes],
        compiler_params=_params("arbitrary"))(chip, *ws)


def _proj_fwd(x, chip, tm, comm):
    t = x.shape[0]
    sub = tm // EDGE_TILE

    def body(x_ref, p_ref, edge_ref, w_ref, w_sem, comm_refs):
        @pl.when(pl.program_id(1) == 0)
        def _():
            _, _, _, j = _place()
            block = pltpu.make_async_copy(comm_refs[0].at[j ^ pl.program_id(0)], w_ref, w_sem)
            block.start()
            block.wait()

        p_ref[...] = _dot(x_ref[...].astype(BF), w_ref[...])
        _write_edges(edge_ref, p_ref, tm)

    return _host_call(
        body, comm, name="proj_fwd", grid=(N_CHIP, t // tm), prefetch=chip, body_reads_comm=True,
        in_specs=[pl.BlockSpec((tm, D), lambda r, i, j: (i, 0))],
        out_specs=[pl.BlockSpec((tm, NP_SHARD), lambda r, i, j: (i, j[0] ^ r)),
                   pl.BlockSpec((sub, 2 * HALO, NP_SHARD), lambda r, i, j: (i, 0, j[0] ^ r))],
        out_shape=[jax.ShapeDtypeStruct((t, N_PROJ), F32), jax.ShapeDtypeStruct((t // EDGE_TILE, 2 * HALO, N_PROJ), F32)],
        scratch_shapes=[pltpu.VMEM((D, NP_SHARD), BF), pltpu.SemaphoreType.DMA],
        args=(x,))


def _edge_specs(t, tm, w):
    k, last = tm // EDGE_TILE, t // EDGE_TILE - 1
    return [pl.BlockSpec((None, HALO, w), lambda i: (jnp.maximum(i * k - 1, 0), 1, 0)),
            pl.BlockSpec((None, HALO, w), lambda i: (jnp.minimum((i + 1) * k, last), 0, 0))]


def _write_edges(edge_ref, rows_ref, tm):
    for s in range(tm // EDGE_TILE):
        edge_ref[s, 0:HALO, :] = rows_ref[s * EDGE_TILE:s * EDGE_TILE + HALO, :]
        edge_ref[s, HALO:2 * HALO, :] = rows_ref[(s + 1) * EDGE_TILE - HALO:(s + 1) * EDGE_TILE, :]


def _end_masks(nt):
    i = pl.program_id(0)
    return (i > 0).astype(F32), (i < nt - 1).astype(F32)


def _conv_fwd(p_ref, prev_ref, next_ref, cw_ref, tm, has_prev, has_next):
    ca = p_ref[:, OFF_CA:OFF_HA]
    ha = p_ref[:, OFF_HA:OFF_UB]
    ch = ca * ha
    ch_prev = prev_ref[HALO - 1:HALO, OFF_CA:OFF_HA] * prev_ref[HALO - 1:HALO, OFF_HA:OFF_UB] * has_prev
    ch_next = next_ref[0:1, OFF_CA:OFF_HA] * next_ref[0:1, OFF_HA:OFF_UB] * has_next
    row = lax.broadcasted_iota(jnp.int32, (tm, W_A), 0)
    ch_m1 = jnp.where(row == 0, ch_prev, pltpu.roll(ch, 1, 0))
    ch_p1 = jnp.where(row == tm - 1, ch_next, pltpu.roll(ch, tm - 1, 0))
    cv = cw_ref[0:1, :] * ch_m1 + cw_ref[1:2, :] * ch + cw_ref[2:3, :] * ch_p1
    return ca, ha, ch, ch_m1, ch_p1, cv


def _spatial_fwd(p_ref, vg_ref, vb_ref, ws_ref, bsf_ref, vnb_ref, mixed_ref, tm):
    vb_pre = p_ref[:, OFF_VB:OFF_GA]
    gv, tv = _gelu(vb_pre)
    xhv, rstdv = _ln_stats(gv)
    vnb_ref[...] = (xhv * vg_ref[...] + vb_ref[...]).astype(BF)
    for c in range(tm // CHUNK):
        rows = slice(c * CHUNK, (c + 1) * CHUNK)
        for h in range(N_HEAD):
            cols = slice(h * CHUNK, (h + 1) * CHUNK)
            mixed_ref[rows, cols] = _dot(ws_ref[h], vnb_ref[rows, cols]) + bsf_ref[:, cols]
    return vb_pre, tv, xhv, rstdv


def _mix_fwd(p, pedge, x, wpa, wpb, wo, wsb, bsf, bg, cw, vg, vb, tm):
    t = x.shape[0]
    nt = t // tm

    def body(p_ref, prev_ref, next_ref, x_ref, wpa_ref, wpb_ref, wo_ref, ws_ref, bsf_ref, bg_ref, cw_ref, vg_ref, vb_ref,
             r1_ref, ya_ref, yb_ref, vnb_ref, mixed_ref):
        has_prev, has_next = _end_masks(nt)
        _, _, _, _, _, cv = _conv_fwd(p_ref, prev_ref, next_ref, cw_ref, tm, has_prev, has_next)
        a = p_ref[:, 0:OFF_CA] * cv
        ya = _dot(a.astype(BF), wpa_ref[...])
        ya_ref[...] = ya
        _spatial_fwd(p_ref, vg_ref, vb_ref, ws_ref, bsf_ref, vnb_ref, mixed_ref, tm)
        gu, _ = _gelu(p_ref[:, OFF_UB:OFF_VB])
        bb = gu * mixed_ref[...]
        yb = _dot(bb.astype(BF), wpb_ref[...])
        yb_ref[...] = yb
        ga = jax.nn.sigmoid(p_ref[:, OFF_GA:OFF_GB] + bg_ref[:, 0:D])
        gb = jax.nn.sigmoid(p_ref[:, OFF_GB:N_PROJ] + bg_ref[:, D:2 * D])
        z = ga * ya + gb * yb
        r1_ref[...] = ALPHA * x_ref[...] + _dot(z.astype(BF), wo_ref[...])

    tile = lambda w: pl.BlockSpec((tm, w), lambda i: (i, 0))
    return _pc(
        body, name="mix_fwd", grid=(nt,),
        in_specs=[tile(N_PROJ), *_edge_specs(t, tm, N_PROJ), tile(D),
                  _resident((W_A, D)), _resident((W_B, D)), _resident((D, D)), _resident((N_HEAD, CHUNK, CHUNK)),
                  _resident((CHUNK, W_B)), _resident((1, 2 * D)), _resident((3, W_A)), _resident((1, W_B)),
                  _resident((1, W_B))],
        out_specs=[tile(D), tile(D), tile(D)],
        out_shape=[jax.ShapeDtypeStruct((t, D), F32)] * 3,
        scratch_shapes=[pltpu.VMEM((tm, W_B), BF), pltpu.VMEM((tm, W_B), F32)],
        compiler_params=_params("arbitrary"),
    )(p, pedge, pedge, x, wpa, wpb, wo, wsb, bsf, bg, cw, vg, vb)


def _ffn_fwd_bwd(r1, tgt, wff1, wff2, ln1g, ln1b, ln2g, ln2b, tm):
    t = r1.shape[0]

    def body(r1_ref, tgt_ref, w1_ref, w2_ref, g1_ref, b1_ref, g2_ref, b2_ref,
             dr1_ref, dr1b_ref, dedge_ref, x1b_ref, hidb_ref, dh1b_ref, dr2b_ref, acc_ref, relu_ref):
        @pl.when(pl.program_id(0) == 0)
        def _():
            acc_ref[...] = jnp.zeros_like(acc_ref)

        xh1, rstd1 = _ln_stats(r1_ref[...])
        x1 = xh1 * g1_ref[...] + b1_ref[...]
        x1b_ref[...] = x1.astype(BF)
        ffn = jnp.zeros((tm, D), F32)
        for j in range(N_CHIP):
            cols = slice(j * FF_SHARD, (j + 1) * FF_SHARD)
            r = jnp.maximum(_dot(x1b_ref[...], w1_ref[j]), 0.0)
            relu_ref[:, cols] = r
            hidb_ref[:, cols] = (r * r).astype(BF)
            ffn = ffn + _dot(hidb_ref[:, cols], w2_ref[cols, :])
        xh2, rstd2 = _ln_stats(ALPHA * x1 + ffn)
        diff = xh2 * g2_ref[...] + b2_ref[...] - tgt_ref[...]
        acc_ref[4:5, :] += _colsum(diff * diff)
        dx2 = diff * (1.0 / D)
        acc_ref[2:3, :] += _colsum(dx2 * xh2)
        acc_ref[3:4, :] += _colsum(dx2)
        dr2 = _ln_bwd(dx2, g2_ref[...], xh2, rstd2)
        dr2b_ref[...] = dr2.astype(BF)
        dx1 = ALPHA * dr2
        for j in range(N_CHIP):
            cols = slice(j * FF_SHARD, (j + 1) * FF_SHARD)
            dhid = _dot_nt(dr2b_ref[...], w2_ref[cols, :])
            dh1b_ref[:, cols] = (dhid * (2.0 * relu_ref[:, cols])).astype(BF)
            dx1 = dx1 + _dot_nt(dh1b_ref[:, cols], w1_ref[j])
        acc_ref[0:1, :] += _colsum(dx1 * xh1)
        acc_ref[1:2, :] += _colsum(dx1)
        dr1 = _ln_bwd(dx1, g1_ref[...], xh1, rstd1)
        dr1_ref[...] = dr1
        dr1b_ref[...] = dr1.astype(BF)
        _write_edges(dedge_ref, dr1_ref, tm)

    tile = lambda w: pl.BlockSpec((tm, w), lambda i: (i, 0))
    vec = _resident((1, D))
    return _pc(
        body, name="ffn_fwd_bwd", grid=(t // tm,),
        in_specs=[tile(D), tile(D), _resident((N_CHIP, D, FF_SHARD)), _resident((D_FF, D)), vec, vec, vec, vec],
        out_specs=[tile(D), tile(D), pl.BlockSpec((tm // EDGE_TILE, 2 * HALO, D), lambda i: (i, 0, 0)), tile(D), tile(D_FF),
                   tile(D_FF), tile(D), pl.BlockSpec((8, D), lambda i: (0, 0))],
        out_shape=[jax.ShapeDtypeStruct((t, D), F32), jax.ShapeDtypeStruct((t, D), BF),
                   jax.ShapeDtypeStruct((t // EDGE_TILE, 2 * HALO, D), F32), jax.ShapeDtypeStruct((t, D), BF),
                   jax.ShapeDtypeStruct((t, D_FF), BF), jax.ShapeDtypeStruct((t, D_FF), BF),
                   jax.ShapeDtypeStruct((t, D), BF), jax.ShapeDtypeStruct((8, D), F32)],
        scratch_shapes=[pltpu.VMEM((tm, D_FF), F32)],
        compiler_params=_params("arbitrary"),
    )(r1, tgt, wff1, wff2, ln1g, ln1b, ln2g, ln2b)


def _dw(a, b, nblk, am, bn, a_blocked, b_blocked, tk, name, comm=None):
    t = a.shape[0]

    def body(a_ref, b_ref, o_ref):
        @pl.when(pl.program_id(1) == 0)
        def _():
            o_ref[...] = jnp.zeros_like(o_ref)

        o_ref[...] += _dot_tn(a_ref[...].astype(BF), b_ref[...])

    outs, got = _host_call(
        body, comm, name=name, grid=(nblk, t // tk),
        in_specs=[pl.BlockSpec((tk, am), (lambda j, k: (k, j)) if a_blocked else (lambda j, k: (k, 0))),
                  pl.BlockSpec((tk, bn), (lambda j, k: (k, j)) if b_blocked else (lambda j, k: (k, 0)))],
        out_specs=[pl.BlockSpec((None, am, bn), lambda j, k: (j, 0, 0))],
        out_shape=[jax.ShapeDtypeStruct((nblk, am, bn), F32)], args=(a, b))
    return outs[0] if comm is None else (outs[0], got)


def _dw_proj(ab, dyab, bbb, dybb, zb, dr1b, tk, comm):
    t = ab.shape[0]
    pairs = (("w_pa", 0, 1), ("w_pb", 2, 3), ("w_o", 4, 5))

    def body(*refs):
        o_ref = refs[6]

        @pl.when(pl.program_id(0) == 0)
        def _():
            o_ref[...] = jnp.zeros_like(o_ref)

        for name, ia, ib in pairs:
            off, rows = PROJ_OFF[name]
            for k in range(N_CHIP):
                o_ref[k, off:off + rows, :] += _dot_tn(refs[ia][:, k * rows:(k + 1) * rows], refs[ib][...])

    tile = lambda w: pl.BlockSpec((tk, w), lambda i: (i, 0))
    outs, got = _host_call(
        body, comm, name="dw_proj", grid=(t // tk,), in_specs=[tile(W_A), tile(D), tile(W_B), tile(D), tile(D), tile(D)],
        out_specs=[pl.BlockSpec((N_CHIP, PROJ_TOTAL, D), lambda i: (0, 0, 0))],
        out_shape=[jax.ShapeDtypeStruct((N_CHIP, PROJ_TOTAL, D), F32)], args=(ab, dyab, bbb, dybb, zb, dr1b))
    return outs[0], got


def _dx(dp, win4, dr1, tm, blk0, nblk, filled, name, comm):
    t = dp.shape[0]

    def body(dp_ref, w_ref, dr1_ref, *rest):
        dx = ALPHA * dr1_ref[...]
        for j in range(N_CHIP):
            dx = dx + _dot_nt(dp_ref[:, j * NP_SHARD:(j + 1) * NP_SHARD], w_ref[j])
        rest[-1][...] = dx

    in_specs = [pl.BlockSpec((tm, N_PROJ), lambda i: (i + blk0, 0)), _resident((N_CHIP, D, NP_SHARD)),
                pl.BlockSpec((tm, D), lambda i: (i + blk0, 0))]
    args = (dp, win4, dr1)
    aliases = None
    if filled is not None:
        in_specs.append(pl.BlockSpec(memory_space=pl.ANY))
        args += (filled,)
        aliases = {3: 0}
    outs, got = _host_call(
        body, comm, name=name, grid=(nblk,), in_specs=in_specs, out_specs=[pl.BlockSpec((tm, D), lambda i: (i + blk0, 0))],
        out_shape=[jax.ShapeDtypeStruct((t, D), F32)], args=args, aliases=aliases)
    return outs[0], got


def _mix_bwd(p, pedge, dr1, dedge, ya, yb, wpa, wpb, wo, wsb, wstb, bsf, bg, cw, vg, vb, tm, comm):
    t = p.shape[0]
    nt = t // tm
    te = tm + 2 * HALO
    mid = slice(HALO, HALO + tm)

    def body(p_ref, prev_ref, next_ref, dr1_ref, dprev_ref, dnext_ref, ya_ref, yb_ref, wpa_ref, wpb_ref, wo_ref,
             ws_ref, wst_ref, bsf_ref, bg_ref, cw_ref, vg_ref, vb_ref,
             dp_ref, ab_ref, bbb_ref, zb_ref, dyab_ref, dybb_ref, dbg_ref, dcw_ref, dvgb_ref, dws_ref, dbs_ref,
             vnb_ref, mixed_ref, dmixb_ref, dvn_ref):
        @pl.when(pl.program_id(0) == 0)
        def _():
            for r in (dbg_ref, dcw_ref, dvgb_ref, dws_ref, dbs_ref):
                r[...] = jnp.zeros_like(r)

        has_prev, has_next = _end_masks(nt)
        ca, ha, ch, ch_m1, ch_p1, cv = _conv_fwd(p_ref, prev_ref, next_ref, cw_ref, tm, has_prev, has_next)
        ba = p_ref[:, 0:OFF_CA]
        ab_ref[...] = (ba * cv).astype(BF)
        vb_pre, tv, xhv, rstdv = _spatial_fwd(p_ref, vg_ref, vb_ref, ws_ref, bsf_ref, vnb_ref, mixed_ref, tm)
        ub = p_ref[:, OFF_UB:OFF_VB]
        gu, tu = _gelu(ub)
        bbb_ref[...] = (gu * mixed_ref[...]).astype(BF)
        bga = bg_ref[:, 0:D]
        ga = jax.nn.sigmoid(p_ref[:, OFF_GA:OFF_GB] + bga)
        gb = jax.nn.sigmoid(p_ref[:, OFF_GB:N_PROJ] + bg_ref[:, D:2 * D])
        ya = ya_ref[...]
        yb = yb_ref[...]
        zb_ref[...] = (ga * ya + gb * yb).astype(BF)

        dr1_ext = jnp.concatenate([dprev_ref[...] * has_prev, dr1_ref[...], dnext_ref[...] * has_next], axis=0)
        dz_ext = _dot_nt(dr1_ext.astype(BF), wo_ref[...])
        ga_ext = jnp.concatenate([jax.nn.sigmoid(prev_ref[:, OFF_GA:OFF_GB] + bga), ga,
                                  jax.nn.sigmoid(next_ref[:, OFF_GA:OFF_GB] + bga)], axis=0)
        dya_ext = dz_ext * ga_ext
        dyab_ref[...] = dya_ext[mid].astype(BF)
        da_ext = _dot_nt(dya_ext.astype(BF), wpa_ref[...])
        ba_ext = jnp.concatenate([prev_ref[:, 0:OFF_CA], ba, next_ref[:, 0:OFF_CA]], axis=0)
        dcv_ext = da_ext * ba_ext
        dcv = dcv_ext[mid]
        dch = (cw_ref[0:1, :] * pltpu.roll(dcv_ext, te - 1, 0)[mid] + cw_ref[1:2, :] * dcv
               + cw_ref[2:3, :] * pltpu.roll(dcv_ext, 1, 0)[mid])
        dp_ref[:, 0:OFF_CA] = (da_ext[mid] * cv).astype(BF)
        dp_ref[:, OFF_CA:OFF_HA] = (dch * ha).astype(BF)
        dp_ref[:, OFF_HA:OFF_UB] = (dch * ca).astype(BF)
        dcw_ref[0:1, :] += _colsum(dcv * ch_m1)
        dcw_ref[1:2, :] += _colsum(dcv * ch)
        dcw_ref[2:3, :] += _colsum(dcv * ch_p1)

        dz = dz_ext[mid]
        dga = dz * ya * ga * (1.0 - ga)
        dgb = dz * yb * gb * (1.0 - gb)
        dp_ref[:, OFF_GA:OFF_GB] = dga.astype(BF)
        dp_ref[:, OFF_GB:N_PROJ] = dgb.astype(BF)
        dbg_ref[0:1, 0:D] += _colsum(dga)
        dbg_ref[0:1, D:2 * D] += _colsum(dgb)

        dybb_ref[...] = (dz * gb).astype(BF)
        dbb = _dot_nt(dybb_ref[...], wpb_ref[...])
        dp_ref[:, OFF_UB:OFF_VB] = (dbb * mixed_ref[...] * _gelu_grad(ub, tu)).astype(BF)
        dmixed = dbb * gu
        dmixb_ref[...] = dmixed.astype(BF)
        for c in range(tm // CHUNK):
            rows = slice(c * CHUNK, (c + 1) * CHUNK)
            dbs_ref[...] += dmixed[rows]
            for h in range(N_HEAD):
                cols = slice(h * CHUNK, (h + 1) * CHUNK)
                dws_ref[h] += _dot_nt(dmixb_ref[rows, cols], vnb_ref[rows, cols])
                dvn_ref[rows, cols] = _dot(wst_ref[h], dmixb_ref[rows, cols])
        dvn = dvn_ref[...]
        dvgb_ref[0:1, :] += _colsum(dvn * xhv)
        dvgb_ref[1:2, :] += _colsum(dvn)
        dgv = _ln_bwd(dvn, vg_ref[...], xhv, rstdv)
        dp_ref[:, OFF_VB:OFF_GA] = (dgv * _gelu_grad(vb_pre, tv)).astype(BF)

    tile = lambda w: pl.BlockSpec((tm, w), lambda i: (i, 0))
    acc = lambda *s: pl.BlockSpec(s, lambda i: (0,) * len(s))
    return _host_call(
        body, comm, name="mix_bwd", grid=(nt,),
        in_specs=[tile(N_PROJ), *_edge_specs(t, tm, N_PROJ), tile(D), *_edge_specs(t, tm, D), tile(D), tile(D),
                  _resident((W_A, D)), _resident((W_B, D)), _resident((D, D)), _resident((N_HEAD, CHUNK, CHUNK)),
                  _resident((N_HEAD, CHUNK, CHUNK)), _resident((CHUNK, W_B)), _resident((1, 2 * D)),
                  _resident((3, W_A)), _resident((1, W_B)), _resident((1, W_B))],
        out_specs=[tile(N_PROJ), tile(W_A), tile(W_B), tile(D), tile(D), tile(D),
                   acc(8, 2 * D), acc(8, W_A), acc(8, W_B), acc(N_HEAD, CHUNK, CHUNK), acc(CHUNK, W_B)],
        out_shape=[jax.ShapeDtypeStruct((t, N_PROJ), BF), jax.ShapeDtypeStruct((t, W_A), BF),
                   jax.ShapeDtypeStruct((t, W_B), BF), jax.ShapeDtypeStruct((t, D), BF), jax.ShapeDtypeStruct((t, D), BF),
                   jax.ShapeDtypeStruct((t, D), BF), jax.ShapeDtypeStruct((8, 2 * D), F32),
                   jax.ShapeDtypeStruct((8, W_A), F32), jax.ShapeDtypeStruct((8, W_B), F32),
                   jax.ShapeDtypeStruct((N_HEAD, CHUNK, CHUNK), F32), jax.ShapeDtypeStruct((CHUNK, W_B), F32)],
        scratch_shapes=[pltpu.VMEM((tm, W_B), BF), pltpu.VMEM((tm, W_B), F32), pltpu.VMEM((tm, W_B), BF),
                        pltpu.VMEM((tm, W_B), F32)],
        args=(p, pedge, pedge, dr1, dedge, dedge, ya, yb, wpa, wpb, wo, wsb, wstb, bsf, bg, cw, vg, vb))


def _add_own_half(full4, recv4, core, rb, name):
    n, rh, cols = recv4.shape
    nb = rh // rb

    def body(c_ref, a_ref, b_ref, o_ref, ob_ref):
        s = a_ref[...] + b_ref[...]
        o_ref[...] = s
        ob_ref[...] = s.astype(BF)

    blk = (None, rb, cols)
    same = pl.BlockSpec(blk, lambda k, i, c: (k, i, 0))
    return _pc(
        body, name=name,
        grid_spec=pltpu.PrefetchScalarGridSpec(
            num_scalar_prefetch=1, grid=(n, nb),
            in_specs=[pl.BlockSpec(blk, lambda k, i, c: (k, c[0] * nb + i, 0)), same], out_specs=[same, same]),
        out_shape=[jax.ShapeDtypeStruct(recv4.shape, F32), jax.ShapeDtypeStruct(recv4.shape, BF)],
        compiler_params=_params("arbitrary", "arbitrary"),
    )(core, full4, recv4)


def _add_chips(s4, r3, place, rb, name):
    _, rh, cols = r3.shape
    nb = rh // rb

    def body(pl_ref, s_ref, r_ref, o_ref):
        o_ref[...] = ((s_ref[...] + r_ref[0].astype(F32)) + r_ref[1].astype(F32)) + r_ref[2].astype(F32)

    return _pc(
        body, name=name,
        grid_spec=pltpu.PrefetchScalarGridSpec(
            num_scalar_prefetch=1, grid=(nb,),
            in_specs=[pl.BlockSpec((None, rb, cols), lambda i, s: (s[0], i, 0)), pl.BlockSpec((3, rb, cols), lambda i, s: (0, i, 0))],
            out_specs=pl.BlockSpec((rb, cols), lambda i, s: (s[1] * nb + i, 0))),
        out_shape=jax.ShapeDtypeStruct((2 * rh, cols), F32),
        compiler_params=_params("arbitrary"),
    )(place, s4, r3)


def _add_small(a, b):
    def body(a_ref, b_ref, o_ref):
        o_ref[...] = a_ref[...] + b_ref[...]

    return _pc(body, name="add_small_cores", out_shape=jax.ShapeDtypeStruct(a.shape, F32))(a, b)


def _sum_small_chips(own, slots, place):
    def body(pl_ref, own_ref, s_ref, o_ref):
        j = pl_ref[0]

        def term(k):
            return jnp.where(j == k, own_ref[...], s_ref[jnp.maximum((j ^ k) - 1, 0)])

        o_ref[...] = ((term(0) + term(1)) + term(2)) + term(3)

    vmem = pl.BlockSpec(memory_space=pltpu.VMEM)
    return _pc(body, name="sum_small_chips", in_specs=[pl.BlockSpec(memory_space=pltpu.SMEM), vmem, vmem], out_specs=vmem,
               out_shape=jax.ShapeDtypeStruct(own.shape, F32))(place, own, slots)


def _adamw(w, g, m, v, rb, name):
    rows, cols = w.shape

    def body(w_ref, g_ref, m_ref, v_ref, d_ref, m2_ref, v2_ref):
        g_ = g_ref[...]
        m2 = ADAM_B1 * m_ref[...] + (1.0 - ADAM_B1) * g_
        v2 = ADAM_B2 * v_ref[...] + (1.0 - ADAM_B2) * (g_ * g_)
        m_hat = m2 / (1.0 - ADAM_B1 ** ADAM_STEP)
        v_hat = v2 / (1.0 - ADAM_B2 ** ADAM_STEP)
        d_ref[...] = -ADAM_LR * (m_hat / (jnp.sqrt(v_hat) + ADAM_EPS) + ADAM_WD * w_ref[...])
        m2_ref[...] = m2
        v2_ref[...] = v2

    blk = pl.BlockSpec((rb, cols), lambda i: (i, 0))
    return _pc(body, name=name, grid=(rows // rb,), in_specs=[blk] * 4, out_specs=[blk] * 3,
               out_shape=[jax.ShapeDtypeStruct((rows, cols), F32)] * 3, compiler_params=_params("arbitrary"))(w, g, m, v)


LANES = 128
SMALL_GRADS = (("b_gate", 2 * D), ("conv_w", 3 * W_A), ("v_norm_g", W_B), ("v_norm_b", W_B),
               ("w_s", N_HEAD * CHUNK * CHUNK), ("b_s", N_HEAD * CHUNK), ("ln1_g", D), ("ln1_b", D), ("ln2_g", D), ("ln2_b", D),
               ("loss", 1))


def _pack_rows(parts):
    rows = []
    for a in parts:
        a = a.reshape(-1)
        a = jnp.pad(a, (0, (-a.shape[0]) % LANES))
        rows.append(a.reshape(-1, LANES))
    out = jnp.concatenate(rows, axis=0)
    return jnp.pad(out, ((0, (-out.shape[0]) % 8), (0, 0)))


def _unpack_rows(buf, sizes):
    out, r = [], 0
    for n in sizes:
        nr = -(-n // LANES)
        out.append(buf[r:r + nr].reshape(-1)[:n])
        r += nr
    return out


TM_PROJ = 1024
TM_MIX = 256
TM_DX = 512
DX_PAIR = 6
TK_DW = 2048
TK_DW_IN = 1024
TK_DW_PROJ = 1024
ADD_BLOCK_BYTES = 3 * 1024 * 1024
RB_ADAM = 128
CONV_ROWS = 8


def _add_rows(rows, cols):
    while rows * cols * 4 > ADD_BLOCK_BYTES and rows % 32 == 0:
        rows //= 2
    return rows


def _reduce_adds_1(grads, recvs, core, tag):
    out = [_add_own_half(g, r, core, _add_rows(*r.shape[1:]), f"add_cores_{tag}{a}") for a, (g, r) in enumerate(zip(grads, recvs))]
    return [o[0] for o in out], [o[1] for o in out]


def _reduce_adds_2(sums, recvs, place, tag):
    return [_add_chips(s, r, place, _add_rows(*r.shape[1:]), f"add_chips_{tag}{a}") for a, (s, r) in enumerate(zip(sums, recvs))]


def kernel(x, w_in, b_gate, conv_w, v_norm_g, v_norm_b, w_s, b_s, w_pa, w_pb, w_o, ln1_g, ln1_b, w_ff1, w_ff2, ln2_g, ln2_b, loss_target, m_w_in, m_b_gate, m_conv_w, m_v_norm_g, m_v_norm_b, m_w_s, m_b_s, m_w_pa, m_w_pb, m_w_o, m_ln1_g, m_ln1_b, m_w_ff1, m_w_ff2, m_ln2_g, m_ln2_b, v_w_in, v_b_gate, v_conv_w, v_v_norm_g, v_v_norm_b, v_w_s, v_b_s, v_w_pa, v_w_pb, v_w_o, v_ln1_g, v_ln1_b, v_w_ff1, v_w_ff2, v_ln2_g, v_ln2_b):
    t = x.shape[1]
    core = lax.axis_index("c").astype(jnp.int32).reshape(1)
    chip_idx = 2 * lax.axis_index("x") + lax.axis_index("y")
    chip = chip_idx.astype(jnp.int32).reshape(1)
    place = jnp.concatenate([chip, core])
    x2 = x.reshape(t, D)
    tgt = loss_target.reshape(t, D)

    win4, proj4, ff14, ff24 = _cast_shards(w_in[0], w_pa[0], w_pb[0], w_o[0], w_ff1[0], w_ff2[0], chip)
    conv4 = lax.dynamic_update_slice(jnp.zeros((N_CHIP, CONV_ROWS, W_A // N_CHIP), F32),
                                     jnp.pad(conv_w[0], ((0, CONV_ROWS - 3), (0, 0)))[None], (chip_idx, 0, 0))
    (p, pedge), (win4, proj4, ff14, ff24, conv4) = _proj_fwd(
        x2, chip, TM_PROJ, _gather_comm([win4, proj4, ff14, ff24], conv4, eager=1))

    def full(name, rows_total):
        off, rows = PROJ_OFF[name]
        return proj4[:, off:off + rows, :].reshape(rows_total, D)

    wpa, wpb, wo = full("w_pa", W_A), full("w_pb", W_B), full("w_o", D)
    wff2 = ff24.reshape(D_FF, D)
    cw = jnp.transpose(conv4[:, :3, :], (1, 0, 2)).reshape(3, W_A)
    wsb = w_s[0].astype(BF)
    wstb = jnp.swapaxes(w_s[0], 1, 2).astype(BF)
    bsf = jnp.repeat(jnp.transpose(b_s[0]), CHUNK, axis=1)

    r1, ya, yb = _mix_fwd(p, pedge, x2, wpa, wpb, wo, wsb, bsf, b_gate, cw, v_norm_g, v_norm_b, TM_MIX)
    dr1, dr1b, dedge, x1b, hidb, dh1b, dr2b, acc = _ffn_fwd_bwd(r1, tgt, ff14, wff2, ln1_g, ln1_b, ln2_g, ln2_b, TM_MIX)
    g_ff = [_dw(x1b, dh1b, N_CHIP, D, FF_SHARD, False, True, TK_DW, "dw_ff1"),
            _dw(hidb, dr2b, N_CHIP, FF_SHARD, D, True, False, TK_DW, "dw_ff2")]
    (dp, ab, bbb, zb, dyab, dybb, dbg, dcw, dvgb, dws, dbs_sum), r_ff = _mix_bwd(
        p, pedge, dr1, dedge, ya, yb, wpa, wpb, wo, wsb, wstb, bsf, b_gate, cw, v_norm_g, v_norm_b, TM_MIX, _pair_comm(g_ff))
    s_ff, sb_ff = _reduce_adds_1(g_ff, r_ff, core, "ff")
    dwin4, c_ff = _dw(x2, dp, N_CHIP, D, NP_SHARD, False, True, TK_DW_IN, "dw_in", _chips_comm(sb_ff))
    f_ff = _reduce_adds_2(s_ff, c_ff, place, "ff")
    dproj4, (g_ff1, g_ff2) = _dw_proj(ab, dyab, bbb, dybb, zb, dr1b, TK_DW_PROJ, _join_comm(f_ff))
    dbs = jnp.transpose(jnp.sum(dbs_sum.reshape(CHUNK, N_HEAD, CHUNK), axis=-1))
    small = _pack_rows([dbg[0], dcw[0:3], dvgb[0], dvgb[1], dws, dbs, acc[0], acc[1], acc[2], acc[3],
                        0.5 * jnp.sum(acc[4]) / D])
    g_rest = [dwin4, dproj4]
    nblk = t // TM_DX
    n_a = max(1, min(DX_PAIR, nblk // 4))
    dx, r_rest = _dx(dp, win4, dr1, TM_DX, 0, n_a, None, "dx_a", _pair_comm(g_rest, small))
    s_rest, sb_rest = _reduce_adds_1(g_rest, r_rest[:2], core, "rest")
    csmall = _add_small(small, r_rest[2])
    dx, c_rest = _dx(dp, win4, dr1, TM_DX, n_a, nblk - n_a, dx, "dx_b", _chips_comm(sb_rest, csmall))
    f_rest = _reduce_adds_2(s_rest, c_rest[:2], place, "rest")
    gsmall = _sum_small_chips(csmall, c_rest[2], place)
    g_in, g_proj = _comm_call(_join_comm(f_rest), "join_rest")

    grads = {"w_in": g_in, "w_ff1": g_ff1, "w_ff2": g_ff2}
    for name, _ in PROJ_ROWS:
        off, rows = PROJ_OFF[name]
        grads[name] = g_proj[off:off + rows, :]
    for (name, n), flat in zip(SMALL_GRADS, _unpack_rows(gsmall, [n for _, n in SMALL_GRADS])):
        grads[name] = flat
    loss = grads.pop("loss").reshape(())
    grads["conv_w"] = lax.dynamic_slice(grads["conv_w"].reshape(3, W_A), (0, chip_idx * (W_A // N_CHIP)), (3, W_A // N_CHIP))

    weights = dict(w_in=w_in, b_gate=b_gate, conv_w=conv_w, v_norm_g=v_norm_g, v_norm_b=v_norm_b, w_s=w_s, b_s=b_s,
                   w_pa=w_pa, w_pb=w_pb, w_o=w_o, ln1_g=ln1_g, ln1_b=ln1_b, w_ff1=w_ff1, w_ff2=w_ff2, ln2_g=ln2_g, ln2_b=ln2_b)
    mom1 = dict(w_in=m_w_in, b_gate=m_b_gate, conv_w=m_conv_w, v_norm_g=m_v_norm_g, v_norm_b=m_v_norm_b, w_s=m_w_s,
                b_s=m_b_s, w_pa=m_w_pa, w_pb=m_w_pb, w_o=m_w_o, ln1_g=m_ln1_g, ln1_b=m_ln1_b, w_ff1=m_w_ff1,
                w_ff2=m_w_ff2, ln2_g=m_ln2_g, ln2_b=m_ln2_b)
    mom2 = dict(w_in=v_w_in, b_gate=v_b_gate, conv_w=v_conv_w, v_norm_g=v_v_norm_g, v_norm_b=v_v_norm_b, w_s=v_w_s,
                b_s=v_b_s, w_pa=v_w_pa, w_pb=v_w_pb, w_o=v_w_o, ln1_g=v_ln1_g, ln1_b=v_ln1_b, w_ff1=v_w_ff1,
                w_ff2=v_w_ff2, ln2_g=v_ln2_g, ln2_b=v_ln2_b)
    order = list(weights)
    big = ("w_in", "w_pa", "w_pb", "w_o", "w_ff1", "w_ff2")
    delta, new_m, new_v = {}, {}, {}
    for name in big:
        w2 = weights[name][0]
        delta[name], new_m[name], new_v[name] = _adamw(w2, grads[name], mom1[name][0], mom2[name][0], RB_ADAM, "adamw_" + name)
    little = [n for n in order if n not in big]
    sizes = [weights[n].size for n in little]
    wsmall = _pack_rows([weights[n] for n in little])
    ds, ms, vs = _adamw(wsmall, _pack_rows([grads[n] for n in little]), _pack_rows([mom1[n] for n in little]),
                        _pack_rows([mom2[n] for n in little]), wsmall.shape[0], "adamw_small")
    for name, d_, m_, v_ in zip(little, _unpack_rows(ds, sizes), _unpack_rows(ms, sizes), _unpack_rows(vs, sizes)):
        delta[name], new_m[name], new_v[name] = d_, m_, v_

    shaped = lambda d: [d[n].reshape(weights[n].shape) for n in order]
    return (loss, dx.reshape(x.shape), *shaped(grads), *shaped(delta), *shaped(new_m), *shaped(new_v))
```

```python
import functools
from typing import NamedTuple

import jax
import jax.numpy as jnp
from jax import lax
from jax.experimental import pallas as pl
from jax.experimental.pallas import tpu as pltpu

D = 1024
W_A = 1536
W_B = 1024
CHUNK = 128
N_HEAD = 8
D_FF = 4096
N_PROJ = 3 * W_A + 2 * W_B + 2 * D
OFF_CA, OFF_HA, OFF_UB, OFF_VB, OFF_GA, OFF_GB = 1536, 3072, 4608, 5632, 6656, 7680
LN_EPS = 1e-5
ALPHA = 2.0 ** 0.25
N_CHIP = 4
NP_SHARD = N_PROJ // N_CHIP
FF_SHARD = D_FF // N_CHIP
ADAM_LR, ADAM_B1, ADAM_B2, ADAM_EPS, ADAM_WD, ADAM_STEP = 0.001, 0.9, 0.999, 1e-08, 0.01, 10

PROJ_ROWS = (("w_pa", W_A // N_CHIP), ("w_pb", W_B // N_CHIP), ("w_o", D // N_CHIP))
PROJ_OFF = {}
_o = 0
for _n, _r in PROJ_ROWS:
    PROJ_OFF[_n] = (_o, _r)
    _o += _r
PROJ_TOTAL = _o

V7X_VMEM_BYTES = 64 * 1024 * 1024
VMEM_LIMIT = 56 * 1024 * 1024
HALO = 8
EDGE_TILE = 256

BF = jnp.bfloat16
F32 = jnp.float32
MESH = pl.DeviceIdType.MESH
HBM_SPEC = pl.BlockSpec(memory_space=pltpu.HBM)


def _pc(body, **kw):
    return pl.pallas_call(body, **kw)


def _params(*sem):
    return pltpu.CompilerParams(dimension_semantics=sem, vmem_limit_bytes=VMEM_LIMIT)


def _resident(shape):
    n = len(shape)
    return pl.BlockSpec(shape, lambda *_: (0,) * n, pipeline_mode=pl.Buffered(1))


def _dot(a, b):
    return jnp.dot(a, b, preferred_element_type=F32)


def _dot_nt(a, b):
    return lax.dot_general(a, b, (((1,), (1,)), ((), ())), preferred_element_type=F32)


def _dot_tn(a, b):
    return lax.dot_general(a, b, (((0,), (0,)), ((), ())), preferred_element_type=F32)


def _gelu(x):
    t = jnp.tanh(0.7978845608028654 * (x + 0.044715 * (x * x * x)))
    return 0.5 * x * (1.0 + t), t


def _gelu_grad(x, t):
    return 0.5 * (1.0 + t) + 0.5 * x * (1.0 - t * t) * (0.7978845608028654 * (1.0 + 0.134145 * (x * x)))


def _ln_stats(r):
    mu = jnp.mean(r, axis=-1, keepdims=True)
    xc = r - mu
    var = jnp.mean(xc * xc, axis=-1, keepdims=True)
    rstd = lax.rsqrt(var + LN_EPS)
    return xc * rstd, rstd


def _ln_bwd(dy, g, xh, rstd):
    dxh = dy * g
    m1 = jnp.mean(dxh, axis=-1, keepdims=True)
    m2 = jnp.mean(dxh * xh, axis=-1, keepdims=True)
    return rstd * (dxh - m1 - xh * m2)


def _colsum(v):
    return jnp.sum(v, axis=0, keepdims=True)


class _Comm(NamedTuple):
    args: tuple
    out_shape: tuple
    aliases: dict
    n_sems: int
    stages: tuple


def _place():
    x, y, c = lax.axis_index("x"), lax.axis_index("y"), lax.axis_index("c")
    return x, y, c, 2 * x + y


def _flip(x, y, c, r):
    return (x ^ (r >> 1), y ^ (r & 1), c)


def _remote(src, dst, send_sems, recv_sems, k, peer):
    return pltpu.make_async_remote_copy(src_ref=src, dst_ref=dst, send_sem=send_sems.at[k], recv_sem=recv_sems.at[k],
                                        device_id=peer, device_id_type=MESH)


def _host_call(body, comm, *, name, grid, in_specs, out_specs, out_shape, args, scratch_shapes=(), aliases=None, prefetch=None,
               body_reads_comm=False):
    sem = ("arbitrary",) * len(grid)
    aliases = dict(aliases or {})
    n_pre = 0 if prefetch is None else 1
    n_in, n_out, n_scr = len(in_specs), len(out_specs), len(scratch_shapes)
    c_in, c_out = (0, 0) if comm is None else (len(comm.args), len(comm.out_shape))
    steps = {"first": (0,) * len(grid), "late": (grid[0] - 1,) + (0,) * (len(grid) - 1), "last": tuple(g - 1 for g in grid)}

    def wrapped(*refs):
        refs = refs[n_pre:]
        own_in, cin = refs[:n_in], refs[n_in:n_in + c_in]
        o0 = n_in + c_in
        own_out, cout = refs[o0:o0 + n_out], refs[o0 + n_out:o0 + n_out + c_out]
        s0 = o0 + n_out + c_out
        scr, sems = refs[s0:s0 + n_scr], refs[s0 + n_scr:]

        def run(before):
            for phase, fn in () if comm is None else comm.stages:
                at_step = isinstance(phase, tuple)
                if before != (at_step or phase == "first"):
                    continue
                step = phase if at_step else steps[phase]
                cond = pl.program_id(0) == step[0]
                for d in range(1, len(grid)):
                    cond = jnp.logical_and(cond, pl.program_id(d) == step[d])
                pl.when(cond)(functools.partial(fn, cin, cout, *sems))

        run(True)
        if body_reads_comm:
            body(*own_in, *own_out, *scr, comm_refs=cout)
        else:
            body(*own_in, *own_out, *scr)
        run(False)

    in_specs = list(in_specs) + [HBM_SPEC] * c_in
    out_specs = list(out_specs) + [HBM_SPEC] * c_out
    out_shape = list(out_shape) + ([] if comm is None else list(comm.out_shape))
    scratch_shapes = list(scratch_shapes) + ([] if comm is None else [pltpu.SemaphoreType.DMA((comm.n_sems,))] * 2)
    args = tuple(args) + (() if comm is None else tuple(comm.args))
    if comm is not None:
        aliases.update({n_in + i: n_out + o for i, o in comm.aliases.items()})
    aliases = {i + n_pre: o for i, o in aliases.items()}
    if prefetch is None:
        kw = dict(grid=grid, in_specs=in_specs, out_specs=out_specs, scratch_shapes=scratch_shapes)
    else:
        kw = dict(grid_spec=pltpu.PrefetchScalarGridSpec(num_scalar_prefetch=1, grid=grid, in_specs=in_specs,
                                                         out_specs=out_specs, scratch_shapes=scratch_shapes))
        args = (prefetch,) + args
    outs = _pc(wrapped, name=name, out_shape=out_shape, input_output_aliases=aliases, compiler_params=_params(*sem), **kw)(*args)
    return outs[:n_out], outs[n_out:]


def _comm_call(comm, name):
    def body(*refs):
        c_in, c_out = len(comm.args), len(comm.out_shape)
        cin, cout, (send_sems, recv_sems) = refs[:c_in], refs[c_in:c_in + c_out], refs[c_in + c_out:]
        for phase in ("first", "late", "last"):
            for ph, fn in comm.stages:
                if ph == phase:
                    fn(cin, cout, send_sems, recv_sems)

    return _pc(body, name=name, in_specs=[HBM_SPEC] * len(comm.args), out_specs=[HBM_SPEC] * len(comm.out_shape),
               out_shape=list(comm.out_shape), scratch_shapes=[pltpu.SemaphoreType.DMA((comm.n_sems,))] * 2,
               input_output_aliases=dict(comm.aliases))(*comm.args)


def _gather_comm(bufs, whole=None, eager=0):
    n = len(bufs)
    halves = [b.shape[1] // 2 for b in bufs]
    k_ici = lambda a, r: 3 * a + r - 1
    k_d2d = lambda a, r: 3 * n + 3 * a + r - 1
    k_whole = lambda r: 6 * n + r - 1

    def half(ref, slot, c, a):
        return ref.at[slot, pl.ds(c * halves[a], halves[a])]

    def send(cin, cout, ss, rs):
        x, y, c, j = _place()
        for a in range(n):
            mine = half(cout[a], j, c, a)
            for r in (1, 2, 3):
                _remote(mine, mine, ss, rs, k_ici(a, r), _flip(x, y, c, r)).start()
        if whole is not None:
            for r in (1, 2, 3):
                _remote(cout[n].at[j], cout[n].at[j], ss, rs, k_whole(r), _flip(x, y, c, r)).start()

    def pass_on(cout, ss, rs, a, r):
        x, y, c, j = _place()
        landed = half(cout[a], j ^ r, c, a)
        _remote(landed, landed, ss, rs, k_ici(a, r), (x, y, 1 - c)).wait_recv()
        _remote(landed, landed, ss, rs, k_d2d(a, r), (x, y, 1 - c)).start()

    def passed_on(cout, ss, rs, a, r):
        x, y, c, j = _place()
        theirs = half(cout[a], j ^ r, 1 - c, a)
        _remote(theirs, theirs, ss, rs, k_d2d(a, r), (x, y, 1 - c)).wait_recv()

    def arrive(r, cin, cout, ss, rs):
        for a in range(eager):
            pass_on(cout, ss, rs, a, r)
        for a in range(eager):
            passed_on(cout, ss, rs, a, r)

    def forward(cin, cout, ss, rs):
        for a in range(eager, n):
            for r in (1, 2, 3):
                pass_on(cout, ss, rs, a, r)

    def finish(cin, cout, ss, rs):
        x, y, c, j = _place()
        sibling = (x, y, 1 - c)
        for a in range(eager, n):
            for r in (1, 2, 3):
                passed_on(cout, ss, rs, a, r)
        for a in range(n):
            mine = half(cout[a], j, c, a)
            for r in (1, 2, 3):
                _remote(mine, mine, ss, rs, k_ici(a, r), sibling).wait_send()
                landed = half(cout[a], j ^ r, c, a)
                _remote(landed, landed, ss, rs, k_d2d(a, r), sibling).wait_send()
        if whole is not None:
            for r in (1, 2, 3):
                cp = _remote(cout[n].at[j ^ r], cout[n].at[j ^ r], ss, rs, k_whole(r), sibling)
                cp.wait_recv()
                cp.wait_send()

    args = tuple(bufs) + ((whole,) if whole is not None else ())
    out_shape = tuple(jax.ShapeDtypeStruct(b.shape, b.dtype) for b in args)
    aliases = {a: a for a in range(len(args))}
    arrivals = tuple(((r, 0), functools.partial(arrive, r)) for r in (1, 2, 3)) if eager else ()
    return _Comm(args, out_shape, aliases, 6 * n + 3, (("first", send),) + arrivals + (("late", forward), ("last", finish)))


def _pair_comm(grads, small=None):
    n = len(grads)
    halves = [g.shape[1] // 2 for g in grads]

    def copies(cin, cout, ss, rs):
        x, y, c, _ = _place()
        sibling = (x, y, 1 - c)
        cps = [_remote(cin[a].at[:, pl.ds((1 - c) * halves[a], halves[a]), :], cout[a], ss, rs, a, sibling) for a in range(n)]
        if small is not None:
            cps.append(_remote(cin[n], cout[n], ss, rs, n, sibling))
        return cps

    def start(cin, cout, ss, rs):
        for cp in copies(cin, cout, ss, rs):
            cp.start()

    def finish(cin, cout, ss, rs):
        for cp in copies(cin, cout, ss, rs):
            cp.wait()

    args = tuple(grads) + ((small,) if small is not None else ())
    out_shape = tuple(jax.ShapeDtypeStruct((N_CHIP, h, g.shape[2]), F32) for g, h in zip(grads, halves))
    out_shape += (jax.ShapeDtypeStruct(small.shape, F32),) if small is not None else ()
    return _Comm(args, out_shape, {}, n + 1, (("first", start), ("last", finish)))


def _chips_comm(sums_bf, small=None):
    n = len(sums_bf)

    def copies(cin, cout, ss, rs):
        x, y, c, j = _place()
        cps = []
        for r in (1, 2, 3):
            peer = _flip(x, y, c, r)
            for a in range(n):
                cps.append(_remote(cin[a].at[j ^ r], cout[a].at[r - 1], ss, rs, (n + 1) * (r - 1) + a, peer))
            if small is not None:
                cps.append(_remote(cin[n], cout[n].at[r - 1], ss, rs, (n + 1) * (r - 1) + n, peer))
        return cps

    def start(cin, cout, ss, rs):
        for cp in copies(cin, cout, ss, rs):
            cp.start()

    def finish(cin, cout, ss, rs):
        for cp in copies(cin, cout, ss, rs):
            cp.wait()

    args = tuple(sums_bf) + ((small,) if small is not None else ())
    out_shape = tuple(jax.ShapeDtypeStruct((3,) + s.shape[1:], BF) for s in sums_bf)
    out_shape += (jax.ShapeDtypeStruct((3,) + small.shape, F32),) if small is not None else ()
    return _Comm(args, out_shape, {}, 3 * (n + 1), (("first", start), ("last", finish)))


def _join_comm(shards):
    n = len(shards)
    halves = [s.shape[0] // 2 for s in shards]

    def start(cin, cout, ss, rs):
        x, y, c, _ = _place()
        for a in range(n):
            mine = cout[a].at[pl.ds(c * halves[a], halves[a]), :]
            _remote(mine, mine, ss, rs, a, (x, y, 1 - c)).start()

    def finish(cin, cout, ss, rs):
        x, y, c, _ = _place()
        for a in range(n):
            theirs = cout[a].at[pl.ds((1 - c) * halves[a], halves[a]), :]
            cp = _remote(theirs, theirs, ss, rs, a, (x, y, 1 - c))
            cp.wait_recv()
            cp.wait_send()

    out_shape = tuple(jax.ShapeDtypeStruct(s.shape, F32) for s in shards)
    return _Comm(tuple(shards), out_shape, {a: a for a in range(n)}, n, (("first", start), ("last", finish)))


def _cast_shards(w_in, w_pa, w_pb, w_o, w_ff1, w_ff2, chip):
    def body(j_ref, win_ref, wpa_ref, wpb_ref, wo_ref, wff1_ref, wff2_ref, win4_ref, proj4_ref, ff14_ref, ff24_ref):
        win4_ref[...] = win_ref[...].astype(BF)
        for name, ref in (("w_pa", wpa_ref), ("w_pb", wpb_ref), ("w_o", wo_ref)):
            off, rows = PROJ_OFF[name]
            proj4_ref[off:off + rows, :] = ref[...].astype(BF)
        ff14_ref[...] = wff1_ref[...].astype(BF)
        ff24_ref[...] = wff2_ref[...].astype(BF)

    whole = lambda a: pl.BlockSpec(a.shape, lambda i, j: (0, 0), pipeline_mode=pl.Buffered(1))
    slot = lambda rows, cols: pl.BlockSpec((None, rows, cols), lambda i, j: (j[0], 0, 0))
    ws = (w_in, w_pa, w_pb, w_o, w_ff1, w_ff2)
    shapes = ((D, NP_SHARD), (PROJ_TOTAL, D), (D, FF_SHARD), (FF_SHARD, D))
    return _pc(
        body, name="cast_shards",
        grid_spec=pltpu.PrefetchScalarGridSpec(num_scalar_prefetch=1, grid=(1,), in_specs=[whole(w) for w in ws],
                                               out_specs=[slot(*s) for s in shapes]),
        out_shape=[jax.ShapeDtypeStruct((N_CHIP,) + s, BF) for s in shapes],
        compiler_params=_params("arbitrary"))(chip, *ws)


def _proj_fwd(x, chip, tm, comm):
    t = x.shape[0]
    sub = tm // EDGE_TILE

    def body(x_ref, p_ref, edge_ref, w_ref, w_sem, comm_refs):
        @pl.when(pl.program_id(1) == 0)
        def _():
            _, _, _, j = _place()
            block = pltpu.make_async_copy(comm_refs[0].at[j ^ pl.program_id(0)], w_ref, w_sem)
            block.start()
            block.wait()

        p_ref[...] = _dot(x_ref[...].astype(BF), w_ref[...])
        _write_edges(edge_ref, p_ref, tm)

    return _host_call(
        body, comm, name="proj_fwd", grid=(N_CHIP, t // tm), prefetch=chip, body_reads_comm=True,
        in_specs=[pl.BlockSpec((tm, D), lambda r, i, j: (i, 0))],
        out_specs=[pl.BlockSpec((tm, NP_SHARD), lambda r, i, j: (i, j[0] ^ r)),
                   pl.BlockSpec((sub, 2 * HALO, NP_SHARD), lambda r, i, j: (i, 0, j[0] ^ r))],
        out_shape=[jax.ShapeDtypeStruct((t, N_PROJ), F32), jax.ShapeDtypeStruct((t // EDGE_TILE, 2 * HALO, N_PROJ), F32)],
        scratch_shapes=[pltpu.VMEM((D, NP_SHARD), BF), pltpu.SemaphoreType.DMA],
        args=(x,))


def _edge_specs(t, tm, w):
    k, last = tm // EDGE_TILE, t // EDGE_TILE - 1
    return [pl.BlockSpec((None, HALO, w), lambda i: (jnp.maximum(i * k - 1, 0), 1, 0)),
            pl.BlockSpec((None, HALO, w), lambda i: (jnp.minimum((i + 1) * k, last), 0, 0))]


def _write_edges(edge_ref, rows_ref, tm):
    for s in range(tm // EDGE_TILE):
        edge_ref[s, 0:HALO, :] = rows_ref[s * EDGE_TILE:s * EDGE_TILE + HALO, :]
        edge_ref[s, HALO:2 * HALO, :] = rows_ref[(s + 1) * EDGE_TILE - HALO:(s + 1) * EDGE_TILE, :]


def _end_masks(nt):
    i = pl.program_id(0)
    return (i > 0).astype(F32), (i < nt - 1).astype(F32)


def _conv_fwd(p_ref, prev_ref, next_ref, cw_ref, tm, has_prev, has_next):
    ca = p_ref[:, OFF_CA:OFF_HA]
    ha = p_ref[:, OFF_HA:OFF_UB]
    ch = ca * ha
    ch_prev = prev_ref[HALO - 1:HALO, OFF_CA:OFF_HA] * prev_ref[HALO - 1:HALO, OFF_HA:OFF_UB] * has_prev
    ch_next = next_ref[0:1, OFF_CA:OFF_HA] * next_ref[0:1, OFF_HA:OFF_UB] * has_next
    row = lax.broadcasted_iota(jnp.int32, (tm, W_A), 0)
    ch_m1 = jnp.where(row == 0, ch_prev, pltpu.roll(ch, 1, 0))
    ch_p1 = jnp.where(row == tm - 1, ch_next, pltpu.roll(ch, tm - 1, 0))
    cv = cw_ref[0:1, :] * ch_m1 + cw_ref[1:2, :] * ch + cw_ref[2:3, :] * ch_p1
    return ca, ha, ch, ch_m1, ch_p1, cv


def _spatial_fwd(p_ref, vg_ref, vb_ref, ws_ref, bsf_ref, vnb_ref, mixed_ref, tm):
    vb_pre = p_ref[:, OFF_VB:OFF_GA]
    gv, tv = _gelu(vb_pre)
    xhv, rstdv = _ln_stats(gv)
    vnb_ref[...] = (xhv * vg_ref[...] + vb_ref[...]).astype(BF)
    for c in range(tm // CHUNK):
        rows = slice(c * CHUNK, (c + 1) * CHUNK)
        for h in range(N_HEAD):
            cols = slice(h * CHUNK, (h + 1) * CHUNK)
            mixed_ref[rows, cols] = _dot(ws_ref[h], vnb_ref[rows, cols]) + bsf_ref[:, cols]
    return vb_pre, tv, xhv, rstdv


def _mix_fwd(p, pedge, x, wpa, wpb, wo, wsb, bsf, bg, cw, vg, vb, tm):
    t = x.shape[0]
    nt = t // tm

    def body(p_ref, prev_ref, next_ref, x_ref, wpa_ref, wpb_ref, wo_ref, ws_ref, bsf_ref, bg_ref, cw_ref, vg_ref, vb_ref,
             r1_ref, ya_ref, yb_ref, vnb_ref, mixed_ref):
        has_prev, has_next = _end_masks(nt)
        _, _, _, _, _, cv = _conv_fwd(p_ref, prev_ref, next_ref, cw_ref, tm, has_prev, has_next)
        a = p_ref[:, 0:OFF_CA] * cv
        ya = _dot(a.astype(BF), wpa_ref[...])
        ya_ref[...] = ya.astype(BF)
        _spatial_fwd(p_ref, vg_ref, vb_ref, ws_ref, bsf_ref, vnb_ref, mixed_ref, tm)
        gu, _ = _gelu(p_ref[:, OFF_UB:OFF_VB])
        bb = gu * mixed_ref[...]
        yb = _dot(bb.astype(BF), wpb_ref[...])
        yb_ref[...] = yb.astype(BF)
        ga = jax.nn.sigmoid(p_ref[:, OFF_GA:OFF_GB] + bg_ref[:, 0:D])
        gb = jax.nn.sigmoid(p_ref[:, OFF_GB:N_PROJ] + bg_ref[:, D:2 * D])
        z = ga * ya + gb * yb
        r1_ref[...] = ALPHA * x_ref[...] + _dot(z.astype(BF), wo_ref[...])

    tile = lambda w: pl.BlockSpec((tm, w), lambda i: (i, 0))
    return _pc(
        body, name="mix_fwd", grid=(nt,),
        in_specs=[tile(N_PROJ), *_edge_specs(t, tm, N_PROJ), tile(D),
                  _resident((W_A, D)), _resident((W_B, D)), _resident((D, D)), _resident((N_HEAD, CHUNK, CHUNK)),
                  _resident((CHUNK, W_B)), _resident((1, 2 * D)), _resident((3, W_A)), _resident((1, W_B)),
                  _resident((1, W_B))],
        out_specs=[tile(D), tile(D), tile(D)],
        out_shape=[jax.ShapeDtypeStruct((t, D), F32), jax.ShapeDtypeStruct((t, D), BF), jax.ShapeDtypeStruct((t, D), BF)],
        scratch_shapes=[pltpu.VMEM((tm, W_B), BF), pltpu.VMEM((tm, W_B), F32)],
        compiler_params=_params("arbitrary"),
    )(p, pedge, pedge, x, wpa, wpb, wo, wsb, bsf, bg, cw, vg, vb)


def _ffn_fwd_bwd(r1, tgt, wff1, wff2, ln1g, ln1b, ln2g, ln2b, tm):
    t = r1.shape[0]

    def body(r1_ref, tgt_ref, w1_ref, w2_ref, g1_ref, b1_ref, g2_ref, b2_ref,
             dr1_ref, dr1b_ref, dedge_ref, x1b_ref, hidb_ref, dh1b_ref, dr2b_ref, acc_ref, relu_ref):
        @pl.when(pl.program_id(0) == 0)
        def _():
            acc_ref[...] = jnp.zeros_like(acc_ref)

        xh1, rstd1 = _ln_stats(r1_ref[...])
        x1 = xh1 * g1_ref[...] + b1_ref[...]
        x1b_ref[...] = x1.astype(BF)
        ffn = jnp.zeros((tm, D), F32)
        for j in range(N_CHIP):
            cols = slice(j * FF_SHARD, (j + 1) * FF_SHARD)
            r = jnp.maximum(_dot(x1b_ref[...], w1_ref[j]), 0.0)
            relu_ref[:, cols] = r
            hidb_ref[:, cols] = (r * r).astype(BF)
            ffn = ffn + _dot(hidb_ref[:, cols], w2_ref[cols, :])
        xh2, rstd2 = _ln_stats(ALPHA * x1 + ffn)
        diff = xh2 * g2_ref[...] + b2_ref[...] - tgt_ref[...]
        acc_ref[4:5, :] += _colsum(diff * diff)
        dx2 = diff * (1.0 / D)
        acc_ref[2:3, :] += _colsum(dx2 * xh2)
        acc_ref[3:4, :] += _colsum(dx2)
        dr2 = _ln_bwd(dx2, g2_ref[...], xh2, rstd2)
        dr2b_ref[...] = dr2.astype(BF)
        dx1 = ALPHA * dr2
        for j in range(N_CHIP):
            cols = slice(j * FF_SHARD, (j + 1) * FF_SHARD)
            dhid = _dot_nt(dr2b_ref[...], w2_ref[cols, :])
            dh1b_ref[:, cols] = (dhid * (2.0 * relu_ref[:, cols])).astype(BF)
            dx1 = dx1 + _dot_nt(dh1b_ref[:, cols], w1_ref[j])
        acc_ref[0:1, :] += _colsum(dx1 * xh1)
        acc_ref[1:2, :] += _colsum(dx1)
        dr1 = _ln_bwd(dx1, g1_ref[...], xh1, rstd1)
        dr1_ref[...] = dr1
        dr1b_ref[...] = dr1.astype(BF)
        _write_edges(dedge_ref, dr1_ref, tm)

    tile = lambda w: pl.BlockSpec((tm, w), lambda i: (i, 0))
    vec = _resident((1, D))
    return _pc(
        body, name="ffn_fwd_bwd", grid=(t // tm,),
        in_specs=[tile(D), tile(D), _resident((N_CHIP, D, FF_SHARD)), _resident((D_FF, D)), vec, vec, vec, vec],
        out_specs=[tile(D), tile(D), pl.BlockSpec((tm // EDGE_TILE, 2 * HALO, D), lambda i: (i, 0, 0)), tile(D), tile(D_FF),
                   tile(D_FF), tile(D), pl.BlockSpec((8, D), lambda i: (0, 0))],
        out_shape=[jax.ShapeDtypeStruct((t, D), F32), jax.ShapeDtypeStruct((t, D), BF),
                   jax.ShapeDtypeStruct((t // EDGE_TILE, 2 * HALO, D), F32), jax.ShapeDtypeStruct((t, D), BF),
                   jax.ShapeDtypeStruct((t, D_FF), BF), jax.ShapeDtypeStruct((t, D_FF), BF),
                   jax.ShapeDtypeStruct((t, D), BF), jax.ShapeDtypeStruct((8, D), F32)],
        scratch_shapes=[pltpu.VMEM((tm, D_FF), F32)],
        compiler_params=_params("arbitrary"),
    )(r1, tgt, wff1, wff2, ln1g, ln1b, ln2g, ln2b)


def _dw(a, b, nblk, am, bn, a_blocked, b_blocked, tk, name, comm=None):
    t = a.shape[0]

    def body(a_ref, b_ref, o_ref):
        @pl.when(pl.program_id(1) == 0)
        def _():
            o_ref[...] = jnp.zeros_like(o_ref)

        o_ref[...] += _dot_tn(a_ref[...].astype(BF), b_ref[...])

    outs, got = _host_call(
        body, comm, name=name, grid=(nblk, t // tk),
        in_specs=[pl.BlockSpec((tk, am), (lambda j, k: (k, j)) if a_blocked else (lambda j, k: (k, 0))),
                  pl.BlockSpec((tk, bn), (lambda j, k: (k, j)) if b_blocked else (lambda j, k: (k, 0)))],
        out_specs=[pl.BlockSpec((None, am, bn), lambda j, k: (j, 0, 0))],
        out_shape=[jax.ShapeDtypeStruct((nblk, am, bn), F32)], args=(a, b))
    return outs[0] if comm is None else (outs[0], got)


def _dw_proj(ab, dyab, bbb, dybb, zb, dr1b, tk, comm):
    t = ab.shape[0]
    pairs = (("w_pa", 0, 1), ("w_pb", 2, 3), ("w_o", 4, 5))

    def body(*refs):
        o_ref = refs[6]

        @pl.when(pl.program_id(0) == 0)
        def _():
            o_ref[...] = jnp.zeros_like(o_ref)

        for name, ia, ib in pairs:
            off, rows = PROJ_OFF[name]
            for k in range(N_CHIP):
                o_ref[k, off:off + rows, :] += _dot_tn(refs[ia][:, k * rows:(k + 1) * rows], refs[ib][...])

    tile = lambda w: pl.BlockSpec((tk, w), lambda i: (i, 0))
    outs, got = _host_call(
        body, comm, name="dw_proj", grid=(t // tk,), in_specs=[tile(W_A), tile(D), tile(W_B), tile(D), tile(D), tile(D)],
        out_specs=[pl.BlockSpec((N_CHIP, PROJ_TOTAL, D), lambda i: (0, 0, 0))],
        out_shape=[jax.ShapeDtypeStruct((N_CHIP, PROJ_TOTAL, D), F32)], args=(ab, dyab, bbb, dybb, zb, dr1b))
    return outs[0], got


def _dx(dp, win4, dr1, tm, blk0, nblk, filled, name, comm):
    t = dp.shape[0]

    def body(dp_ref, w_ref, dr1_ref, *rest):
        dx = ALPHA * dr1_ref[...]
        for j in range(N_CHIP):
            dx = dx + _dot_nt(dp_ref[:, j * NP_SHARD:(j + 1) * NP_SHARD], w_ref[j])
        rest[-1][...] = dx

    in_specs = [pl.BlockSpec((tm, N_PROJ), lambda i: (i + blk0, 0)), _resident((N_CHIP, D, NP_SHARD)),
                pl.BlockSpec((tm, D), lambda i: (i + blk0, 0))]
    args = (dp, win4, dr1)
    aliases = None
    if filled is not None:
        in_specs.append(pl.BlockSpec(memory_space=pl.ANY))
        args += (filled,)
        aliases = {3: 0}
    outs, got = _host_call(
        body, comm, name=name, grid=(nblk,), in_specs=in_specs, out_specs=[pl.BlockSpec((tm, D), lambda i: (i + blk0, 0))],
        out_shape=[jax.ShapeDtypeStruct((t, D), F32)], args=args, aliases=aliases)
    return outs[0], got


def _mix_bwd(p, pedge, dr1, dedge, ya, yb, wpa, wpb, wo, wsb, wstb, bsf, bg, cw, vg, vb, tm, comm):
    t = p.shape[0]
    nt = t // tm
    te = tm + 2 * HALO
    mid = slice(HALO, HALO + tm)

    def body(p_ref, prev_ref, next_ref, dr1_ref, dprev_ref, dnext_ref, ya_ref, yb_ref, wpa_ref, wpb_ref, wo_ref,
             ws_ref, wst_ref, bsf_ref, bg_ref, cw_ref, vg_ref, vb_ref,
             dp_ref, ab_ref, bbb_ref, zb_ref, dyab_ref, dybb_ref, dbg_ref, dcw_ref, dvgb_ref, dws_ref, dbs_ref,
             vnb_ref, mixed_ref, dmixb_ref, dvn_ref):
        @pl.when(pl.program_id(0) == 0)
        def _():
            for r in (dbg_ref, dcw_ref, dvgb_ref, dws_ref, dbs_ref):
                r[...] = jnp.zeros_like(r)

        has_prev, has_next = _end_masks(nt)
        ca, ha, ch, ch_m1, ch_p1, cv = _conv_fwd(p_ref, prev_ref, next_ref, cw_ref, tm, has_prev, has_next)
        ba = p_ref[:, 0:OFF_CA]
        ab_ref[...] = (ba * cv).astype(BF)
        vb_pre, tv, xhv, rstdv = _spatial_fwd(p_ref, vg_ref, vb_ref, ws_ref, bsf_ref, vnb_ref, mixed_ref, tm)
        ub = p_ref[:, OFF_UB:OFF_VB]
        gu, tu = _gelu(ub)
        bbb_ref[...] = (gu * mixed_ref[...]).astype(BF)
        bga = bg_ref[:, 0:D]
        ga = jax.nn.sigmoid(p_ref[:, OFF_GA:OFF_GB] + bga)
        gb = jax.nn.sigmoid(p_ref[:, OFF_GB:N_PROJ] + bg_ref[:, D:2 * D])
        ya = ya_ref[...].astype(F32)
        yb = yb_ref[...].astype(F32)
        zb_ref[...] = (ga * ya + gb * yb).astype(BF)

        dr1_ext = jnp.concatenate([dprev_ref[...] * has_prev, dr1_ref[...], dnext_ref[...] * has_next], axis=0)
        dz_ext = _dot_nt(dr1_ext.astype(BF), wo_ref[...])
        ga_ext = jnp.concatenate([jax.nn.sigmoid(prev_ref[:, OFF_GA:OFF_GB] + bga), ga,
                                  jax.nn.sigmoid(next_ref[:, OFF_GA:OFF_GB] + bga)], axis=0)
        dya_ext = dz_ext * ga_ext
        dyab_ref[...] = dya_ext[mid].astype(BF)
        da_ext = _dot_nt(dya_ext.astype(BF), wpa_ref[...])
        ba_ext = jnp.concatenate([prev_ref[:, 0:OFF_CA], ba, next_ref[:, 0:OFF_CA]], axis=0)
        dcv_ext = da_ext * ba_ext
        dcv = dcv_ext[mid]
        dch = (cw_ref[0:1, :] * pltpu.roll(dcv_ext, te - 1, 0)[mid] + cw_ref[1:2, :] * dcv
               + cw_ref[2:3, :] * pltpu.roll(dcv_ext, 1, 0)[mid])
        dp_ref[:, 0:OFF_CA] = (da_ext[mid] * cv).astype(BF)
        dp_ref[:, OFF_CA:OFF_HA] = (dch * ha).astype(BF)
        dp_ref[:, OFF_HA:OFF_UB] = (dch * ca).astype(BF)
        dcw_ref[0:1, :] += _colsum(dcv * ch_m1)
        dcw_ref[1:2, :] += _colsum(dcv * ch)
        dcw_ref[2:3, :] += _colsum(dcv * ch_p1)

        dz = dz_ext[mid]
        dga = dz * ya * ga * (1.0 - ga)
        dgb = dz * yb * gb * (1.0 - gb)
        dp_ref[:, OFF_GA:OFF_GB] = dga.astype(BF)
        dp_ref[:, OFF_GB:N_PROJ] = dgb.astype(BF)
        dbg_ref[0:1, 0:D] += _colsum(dga)
        dbg_ref[0:1, D:2 * D] += _colsum(dgb)

        dybb_ref[...] = (dz * gb).astype(BF)
        dbb = _dot_nt(dybb_ref[...], wpb_ref[...])
        dp_ref[:, OFF_UB:OFF_VB] = (dbb * mixed_ref[...] * _gelu_grad(ub, tu)).astype(BF)
        dmixed = dbb * gu
        dmixb_ref[...] = dmixed.astype(BF)
        for c in range(tm // CHUNK):
            rows = slice(c * CHUNK, (c + 1) * CHUNK)
            dbs_ref[...] += dmixed[rows]
            for h in range(N_HEAD):
                cols = slice(h * CHUNK, (h + 1) * CHUNK)
                dws_ref[h] += _dot_nt(dmixb_ref[rows, cols], vnb_ref[rows, cols])
                dvn_ref[rows, cols] = _dot(wst_ref[h], dmixb_ref[rows, cols])
        dvn = dvn_ref[...]
        dvgb_ref[0:1, :] += _colsum(dvn * xhv)
        dvgb_ref[1:2, :] += _colsum(dvn)
        dgv = _ln_bwd(dvn, vg_ref[...], xhv, rstdv)
        dp_ref[:, OFF_VB:OFF_GA] = (dgv * _gelu_grad(vb_pre, tv)).astype(BF)

    tile = lambda w: pl.BlockSpec((tm, w), lambda i: (i, 0))
    acc = lambda *s: pl.BlockSpec(s, lambda i: (0,) * len(s))
    return _host_call(
        body, comm, name="mix_bwd", grid=(nt,),
        in_specs=[tile(N_PROJ), *_edge_specs(t, tm, N_PROJ), tile(D), *_edge_specs(t, tm, D), tile(D), tile(D),
                  _resident((W_A, D)), _resident((W_B, D)), _resident((D, D)), _resident((N_HEAD, CHUNK, CHUNK)),
                  _resident((N_HEAD, CHUNK, CHUNK)), _resident((CHUNK, W_B)), _resident((1, 2 * D)),
                  _resident((3, W_A)), _resident((1, W_B)), _resident((1, W_B))],
        out_specs=[tile(N_PROJ), tile(W_A), tile(W_B), tile(D), tile(D), tile(D),
                   acc(8, 2 * D), acc(8, W_A), acc(8, W_B), acc(N_HEAD, CHUNK, CHUNK), acc(CHUNK, W_B)],
        out_shape=[jax.ShapeDtypeStruct((t, N_PROJ), BF), jax.ShapeDtypeStruct((t, W_A), BF),
                   jax.ShapeDtypeStruct((t, W_B), BF), jax.ShapeDtypeStruct((t, D), BF), jax.ShapeDtypeStruct((t, D), BF),
                   jax.ShapeDtypeStruct((t, D), BF), jax.ShapeDtypeStruct((8, 2 * D), F32),
                   jax.ShapeDtypeStruct((8, W_A), F32), jax.ShapeDtypeStruct((8, W_B), F32),
                   jax.ShapeDtypeStruct((N_HEAD, CHUNK, CHUNK), F32), jax.ShapeDtypeStruct((CHUNK, W_B), F32)],
        scratch_shapes=[pltpu.VMEM((tm, W_B), BF), pltpu.VMEM((tm, W_B), F32), pltpu.VMEM((tm, W_B), BF),
                        pltpu.VMEM((tm, W_B), F32)],
        args=(p, pedge, pedge, dr1, dedge, dedge, ya, yb, wpa, wpb, wo, wsb, wstb, bsf, bg, cw, vg, vb))


def _add_own_half(full4, recv4, place, rb, name):
    n, rh, cols = recv4.shape
    nb = rh // rb

    def body(pl_ref, a_ref, b_ref, own_ref, ob_ref):
        s = a_ref[...] + b_ref[...]
        ob_ref[...] = s.astype(BF)

        @pl.when(pl.program_id(1) == pl_ref[0])
        def _():
            own_ref[...] = s

    blk = (None, rb, cols)
    return _pc(
        body, name=name,
        grid_spec=pltpu.PrefetchScalarGridSpec(
            num_scalar_prefetch=1, grid=(nb, n),
            in_specs=[pl.BlockSpec(blk, lambda i, k, s: (k, s[1] * nb + i, 0)), pl.BlockSpec(blk, lambda i, k, s: (k, i, 0))],
            out_specs=[pl.BlockSpec((rb, cols), lambda i, k, s: (i, 0)), pl.BlockSpec(blk, lambda i, k, s: (k, i, 0))]),
        out_shape=[jax.ShapeDtypeStruct((rh, cols), F32), jax.ShapeDtypeStruct(recv4.shape, BF)],
        compiler_params=_params("arbitrary", "arbitrary"),
    )(place, full4, recv4)


def _add_chips(own, r3, place, rb, name):
    _, rh, cols = r3.shape
    nb = rh // rb

    def body(pl_ref, s_ref, r_ref, o_ref):
        o_ref[...] = ((s_ref[...] + r_ref[0].astype(F32)) + r_ref[1].astype(F32)) + r_ref[2].astype(F32)

    return _pc(
        body, name=name,
        grid_spec=pltpu.PrefetchScalarGridSpec(
            num_scalar_prefetch=1, grid=(nb,),
            in_specs=[pl.BlockSpec((rb, cols), lambda i, s: (i, 0)), pl.BlockSpec((3, rb, cols), lambda i, s: (0, i, 0))],
            out_specs=pl.BlockSpec((rb, cols), lambda i, s: (s[1] * nb + i, 0))),
        out_shape=jax.ShapeDtypeStruct((2 * rh, cols), F32),
        compiler_params=_params("arbitrary"),
    )(place, own, r3)


def _add_small(a, b):
    def body(a_ref, b_ref, o_ref):
        o_ref[...] = a_ref[...] + b_ref[...]

    return _pc(body, name="add_small_cores", out_shape=jax.ShapeDtypeStruct(a.shape, F32))(a, b)


def _sum_small_chips(own, slots, place):
    def body(pl_ref, own_ref, s_ref, o_ref):
        j = pl_ref[0]

        def term(k):
            return jnp.where(j == k, own_ref[...], s_ref[jnp.maximum((j ^ k) - 1, 0)])

        o_ref[...] = ((term(0) + term(1)) + term(2)) + term(3)

    vmem = pl.BlockSpec(memory_space=pltpu.VMEM)
    return _pc(body, name="sum_small_chips", in_specs=[pl.BlockSpec(memory_space=pltpu.SMEM), vmem, vmem], out_specs=vmem,
               out_shape=jax.ShapeDtypeStruct(own.shape, F32))(place, own, slots)


def _adamw(w, g, m, v, rb, name):
    rows, cols = w.shape

    def body(w_ref, g_ref, m_ref, v_ref, d_ref, m2_ref, v2_ref):
        g_ = g_ref[...]
        m2 = ADAM_B1 * m_ref[...] + (1.0 - ADAM_B1) * g_
        v2 = ADAM_B2 * v_ref[...] + (1.0 - ADAM_B2) * (g_ * g_)
        m_hat = m2 / (1.0 - ADAM_B1 ** ADAM_STEP)
        v_hat = v2 / (1.0 - ADAM_B2 ** ADAM_STEP)
        d_ref[...] = -ADAM_LR * (m_hat / (jnp.sqrt(v_hat) + ADAM_EPS) + ADAM_WD * w_ref[...])
        m2_ref[...] = m2
        v2_ref[...] = v2

    blk = pl.BlockSpec((rb, cols), lambda i: (i, 0))
    return _pc(body, name=name, grid=(rows // rb,), in_specs=[blk] * 4, out_specs=[blk] * 3,
               out_shape=[jax.ShapeDtypeStruct((rows, cols), F32)] * 3, compiler_params=_params("arbitrary"))(w, g, m, v)


LANES = 128
SMALL_GRADS = (("b_gate", 2 * D), ("conv_w", 3 * W_A), ("v_norm_g", W_B), ("v_norm_b", W_B),
               ("w_s", N_HEAD * CHUNK * CHUNK), ("b_s", N_HEAD * CHUNK), ("ln1_g", D), ("ln1_b", D), ("ln2_g", D), ("ln2_b", D),
               ("loss", 1))


def _pack_rows(parts):
    rows = []
    for a in parts:
        a = a.reshape(-1)
        a = jnp.pad(a, (0, (-a.shape[0]) % LANES))
        rows.append(a.reshape(-1, LANES))
    out = jnp.concatenate(rows, axis=0)
    return jnp.pad(out, ((0, (-out.shape[0]) % 8), (0, 0)))


def _unpack_rows(buf, sizes):
    out, r = [], 0
    for n in sizes:
        nr = -(-n // LANES)
        out.append(buf[r:r + nr].reshape(-1)[:n])
        r += nr
    return out


TM_PROJ = 1024
TM_MIX = 256
TM_DX = 512
DX_PAIR = 6
TK_DW = 2048
TK_DW_IN = 1024
TK_DW_PROJ = 1024
ADD_BLOCK_BYTES = 3 * 1024 * 1024
RB_ADAM = 128
CONV_ROWS = 8


def _add_rows(rows, cols):
    while rows * cols * 4 > ADD_BLOCK_BYTES and rows % 32 == 0:
        rows //= 2
    return rows


def _reduce_adds_1(grads, recvs, place, tag):
    out = [_add_own_half(g, r, place, _add_rows(*r.shape[1:]), f"add_cores_{tag}{a}") for a, (g, r) in enumerate(zip(grads, recvs))]
    return [o[0] for o in out], [o[1] for o in out]


def _reduce_adds_2(sums, recvs, place, tag):
    return [_add_chips(s, r, place, _add_rows(*r.shape[1:]), f"add_chips_{tag}{a}") for a, (s, r) in enumerate(zip(sums, recvs))]


def kernel(x, w_in, b_gate, conv_w, v_norm_g, v_norm_b, w_s, b_s, w_pa, w_pb, w_o, ln1_g, ln1_b, w_ff1, w_ff2, ln2_g, ln2_b, loss_target, m_w_in, m_b_gate, m_conv_w, m_v_norm_g, m_v_norm_b, m_w_s, m_b_s, m_w_pa, m_w_pb, m_w_o, m_ln1_g, m_ln1_b, m_w_ff1, m_w_ff2, m_ln2_g, m_ln2_b, v_w_in, v_b_gate, v_conv_w, v_v_norm_g, v_v_norm_b, v_w_s, v_b_s, v_w_pa, v_w_pb, v_w_o, v_ln1_g, v_ln1_b, v_w_ff1, v_w_ff2, v_ln2_g, v_ln2_b):
    t = x.shape[1]
    core = lax.axis_index("c").astype(jnp.int32).reshape(1)
    chip_idx = 2 * lax.axis_index("x") + lax.axis_index("y")
    chip = chip_idx.astype(jnp.int32).reshape(1)
    place = jnp.concatenate([chip, core])
    x2 = x.reshape(t, D)
    tgt = loss_target.reshape(t, D)

    win4, proj4, ff14, ff24 = _cast_shards(w_in[0], w_pa[0], w_pb[0], w_o[0], w_ff1[0], w_ff2[0], chip)
    conv4 = lax.dynamic_update_slice(jnp.zeros((N_CHIP, CONV_ROWS, W_A // N_CHIP), F32),
                                     jnp.pad(conv_w[0], ((0, CONV_ROWS - 3), (0, 0)))[None], (chip_idx, 0, 0))
    (p, pedge), (win4, proj4, ff14, ff24, conv4) = _proj_fwd(
        x2, chip, TM_PROJ, _gather_comm([win4, proj4, ff14, ff24], conv4, eager=1))

    def full(name, rows_total):
        off, rows = PROJ_OFF[name]
        return proj4[:, off:off + rows, :].reshape(rows_total, D)

    wpa, wpb, wo = full("w_pa", W_A), full("w_pb", W_B), full("w_o", D)
    wff2 = ff24.reshape(D_FF, D)
    cw = jnp.transpose(conv4[:, :3, :], (1, 0, 2)).reshape(3, W_A)
    wsb = w_s[0].astype(BF)
    wstb = jnp.swapaxes(w_s[0], 1, 2).astype(BF)
    bsf = jnp.repeat(jnp.transpose(b_s[0]), CHUNK, axis=1)

    r1, ya, yb = _mix_fwd(p, pedge, x2, wpa, wpb, wo, wsb, bsf, b_gate, cw, v_norm_g, v_norm_b, TM_MIX)
    dr1, dr1b, dedge, x1b, hidb, dh1b, dr2b, acc = _ffn_fwd_bwd(r1, tgt, ff14, wff2, ln1_g, ln1_b, ln2_g, ln2_b, TM_MIX)
    g_ff = [_dw(x1b, dh1b, N_CHIP, D, FF_SHARD, False, True, TK_DW, "dw_ff1"),
            _dw(hidb, dr2b, N_CHIP, FF_SHARD, D, True, False, TK_DW, "dw_ff2")]
    (dp, ab, bbb, zb, dyab, dybb, dbg, dcw, dvgb, dws, dbs_sum), r_ff = _mix_bwd(
        p, pedge, dr1, dedge, ya, yb, wpa, wpb, wo, wsb, wstb, bsf, b_gate, cw, v_norm_g, v_norm_b, TM_MIX, _pair_comm(g_ff))
    s_ff, sb_ff = _reduce_adds_1(g_ff, r_ff, place, "ff")
    dwin4, c_ff = _dw(x2, dp, N_CHIP, D, NP_SHARD, False, True, TK_DW_IN, "dw_in", _chips_comm(sb_ff))
    f_ff = _reduce_adds_2(s_ff, c_ff, place, "ff")
    dproj4, (g_ff1, g_ff2) = _dw_proj(ab, dyab, bbb, dybb, zb, dr1b, TK_DW_PROJ, _join_comm(f_ff))
    dbs = jnp.transpose(jnp.sum(dbs_sum.reshape(CHUNK, N_HEAD, CHUNK), axis=-1))
    small = _pack_rows([dbg[0], dcw[0:3], dvgb[0], dvgb[1], dws, dbs, acc[0], acc[1], acc[2], acc[3],
                        0.5 * jnp.sum(acc[4]) / D])
    g_rest = [dwin4, dproj4]
    nblk = t // TM_DX
    n_a = max(1, min(DX_PAIR, nblk // 4))
    dx, r_rest = _dx(dp, win4, dr1, TM_DX, 0, n_a, None, "dx_a", _pair_comm(g_rest, small))
    s_rest, sb_rest = _reduce_adds_1(g_rest, r_rest[:2], place, "rest")
    csmall = _add_small(small, r_rest[2])
    dx, c_rest = _dx(dp, win4, dr1, TM_DX, n_a, nblk - n_a, dx, "dx_b", _chips_comm(sb_rest, csmall))
    f_rest = _reduce_adds_2(s_rest, c_rest[:2], place, "rest")
    gsmall = _sum_small_chips(csmall, c_rest[2], place)
    g_in, g_proj = _comm_call(_join_comm(f_rest), "join_rest")

    grads = {"w_in": g_in, "w_ff1": g_ff1, "w_ff2": g_ff2}
    for name, _ in PROJ_ROWS:
        off, rows = PROJ_OFF[name]
        grads[name] = g_proj[off:off + rows, :]
    for (name, n), flat in zip(SMALL_GRADS, _unpack_rows(gsmall, [n for _, n in SMALL_GRADS])):
        grads[name] = flat
    loss = grads.pop("loss").reshape(())
    grads["conv_w"] = lax.dynamic_slice(grads["conv_w"].reshape(3, W_A), (0, chip_idx * (W_A // N_CHIP)), (3, W_A // N_CHIP))

    weights = dict(w_in=w_in, b_gate=b_gate, conv_w=conv_w, v_norm_g=v_norm_g, v_norm_b=v_norm_b, w_s=w_s, b_s=b_s,
                   w_pa=w_pa, w_pb=w_pb, w_o=w_o, ln1_g=ln1_g, ln1_b=ln1_b, w_ff1=w_ff1, w_ff2=w_ff2, ln2_g=ln2_g, ln2_b=ln2_b)
    mom1 = dict(w_in=m_w_in, b_gate=m_b_gate, conv_w=m_conv_w, v_norm_g=m_v_norm_g, v_norm_b=m_v_norm_b, w_s=m_w_s,
                b_s=m_b_s, w_pa=m_w_pa, w_pb=m_w_pb, w_o=m_w_o, ln1_g=m_ln1_g, ln1_b=m_ln1_b, w_ff1=m_w_ff1,
                w_ff2=m_w_ff2, ln2_g=m_ln2_g, ln2_b=m_ln2_b)
    mom2 = dict(w_in=v_w_in, b_gate=v_b_gate, conv_w=v_conv_w, v_norm_g=v_v_norm_g, v_norm_b=v_v_norm_b, w_s=v_w_s,
                b_s=v_b_s, w_pa=v_w_pa, w_pb=v_w_pb, w_o=v_w_o, ln1_g=v_ln1_g, ln1_b=v_ln1_b, w_ff1=v_w_ff1,
                w_ff2=v_w_ff2, ln2_g=v_ln2_g, ln2_b=v_ln2_b)
    order = list(weights)
    big = ("w_in", "w_pa", "w_pb", "w_o", "w_ff1", "w_ff2")
    delta, new_m, new_v = {}, {}, {}
    for name in big:
        w2 = weights[name][0]
        delta[name], new_m[name], new_v[name] = _adamw(w2, grads[name], mom1[name][0], mom2[name][0], RB_ADAM, "adamw_" + name)
    little = [n for n in order if n not in big]
    sizes = [weights[n].size for n in little]
    wsmall = _pack_rows([weights[n] for n in little])
    ds, ms, vs = _adamw(wsmall, _pack_rows([grads[n] for n in little]), _pack_rows([mom1[n] for n in little]),
                        _pack_rows([mom2[n] for n in little]), wsmall.shape[0], "adamw_small")
    for name, d_, m_, v_ in zip(little, _unpack_rows(ds, sizes), _unpack_rows(ms, sizes), _unpack_rows(vs, sizes)):
        delta[name], new_m[name], new_v[name] = d_, m_, v_

    shaped = lambda d: [d[n].reshape(weights[n].shape) for n in order]
    return (loss, dx.reshape(x.shape), *shaped(grads), *shaped(delta), *shaped(new_m), *shaped(new_v))
```

```python
import functools
from typing import NamedTuple

import jax
import jax.numpy as jnp
from jax import lax
from jax.experimental import pallas as pl
from jax.experimental.pallas import tpu as pltpu

D = 1024
W_A = 1536
W_B = 1024
CHUNK = 128
N_HEAD = 8
D_FF = 4096
N_PROJ = 3 * W_A + 2 * W_B + 2 * D
OFF_CA, OFF_HA, OFF_UB, OFF_VB, OFF_GA, OFF_GB = 1536, 3072, 4608, 5632, 6656, 7680
LN_EPS = 1e-5
ALPHA = 2.0 ** 0.25
N_CHIP = 4
NP_SHARD = N_PROJ // N_CHIP
FF_SHARD = D_FF // N_CHIP
ADAM_LR, ADAM_B1, ADAM_B2, ADAM_EPS, ADAM_WD, ADAM_STEP = 0.001, 0.9, 0.999, 1e-08, 0.01, 10

PROJ_ROWS = (("w_pa", W_A // N_CHIP), ("w_pb", W_B // N_CHIP), ("w_o", D // N_CHIP))
PROJ_OFF = {}
_o = 0
for _n, _r in PROJ_ROWS:
    PROJ_OFF[_n] = (_o, _r)
    _o += _r
PROJ_TOTAL = _o

V7X_VMEM_BYTES = 64 * 1024 * 1024
VMEM_LIMIT = 56 * 1024 * 1024
HALO = 8
EDGE_TILE = 256

BF = jnp.bfloat16
F32 = jnp.float32
MESH = pl.DeviceIdType.MESH
HBM_SPEC = pl.BlockSpec(memory_space=pltpu.HBM)


def _pc(body, **kw):
    return pl.pallas_call(body, **kw)


def _params(*sem):
    return pltpu.CompilerParams(dimension_semantics=sem, vmem_limit_bytes=VMEM_LIMIT)


def _resident(shape):
    n = len(shape)
    return pl.BlockSpec(shape, lambda *_: (0,) * n, pipeline_mode=pl.Buffered(1))


def _dot(a, b):
    return jnp.dot(a, b, preferred_element_type=F32)


def _dot_nt(a, b):
    return lax.dot_general(a, b, (((1,), (1,)), ((), ())), preferred_element_type=F32)


def _dot_tn(a, b):
    return lax.dot_general(a, b, (((0,), (0,)), ((), ())), preferred_element_type=F32)


def _gelu(x):
    t = jnp.tanh(0.7978845608028654 * (x + 0.044715 * (x * x * x)))
    return 0.5 * x * (1.0 + t), t


def _gelu_grad(x, t):
    return 0.5 * (1.0 + t) + 0.5 * x * (1.0 - t * t) * (0.7978845608028654 * (1.0 + 0.134145 * (x * x)))


def _ln_stats(r):
    mu = jnp.mean(r, axis=-1, keepdims=True)
    xc = r - mu
    var = jnp.mean(xc * xc, axis=-1, keepdims=True)
    rstd = lax.rsqrt(var + LN_EPS)
    return xc * rstd, rstd


def _ln_bwd(dy, g, xh, rstd):
    dxh = dy * g
    m1 = jnp.mean(dxh, axis=-1, keepdims=True)
    m2 = jnp.mean(dxh * xh, axis=-1, keepdims=True)
    return rstd * (dxh - m1 - xh * m2)


def _colsum(v):
    return jnp.sum(v, axis=0, keepdims=True)


class _Comm(NamedTuple):
    args: tuple
    out_shape: tuple
    aliases: dict
    n_sems: int
    stages: tuple


def _place():
    x, y, c = lax.axis_index("x"), lax.axis_index("y"), lax.axis_index("c")
    return x, y, c, 2 * x + y


def _flip(x, y, c, r):
    return (x ^ (r >> 1), y ^ (r & 1), c)


def _remote(src, dst, send_sems, recv_sems, k, peer):
    return pltpu.make_async_remote_copy(src_ref=src, dst_ref=dst, send_sem=send_sems.at[k], recv_sem=recv_sems.at[k],
                                        device_id=peer, device_id_type=MESH)


def _host_call(body, comm, *, name, grid, in_specs, out_specs, out_shape, args, scratch_shapes=(), aliases=None, prefetch=None,
               body_reads_comm=False):
    sem = ("arbitrary",) * len(grid)
    aliases = dict(aliases or {})
    n_pre = 0 if prefetch is None else 1
    n_in, n_out, n_scr = len(in_specs), len(out_specs), len(scratch_shapes)
    c_in, c_out = (0, 0) if comm is None else (len(comm.args), len(comm.out_shape))
    steps = {"first": (0,) * len(grid), "late": (grid[0] - 1,) + (0,) * (len(grid) - 1), "last": tuple(g - 1 for g in grid)}

    def wrapped(*refs):
        refs = refs[n_pre:]
        own_in, cin = refs[:n_in], refs[n_in:n_in + c_in]
        o0 = n_in + c_in
        own_out, cout = refs[o0:o0 + n_out], refs[o0 + n_out:o0 + n_out + c_out]
        s0 = o0 + n_out + c_out
        scr, sems = refs[s0:s0 + n_scr], refs[s0 + n_scr:]

        def run(before):
            for phase, fn in () if comm is None else comm.stages:
                at_step = isinstance(phase, tuple)
                if before != (at_step or phase == "first"):
                    continue
                step = phase if at_step else steps[phase]
                cond = pl.program_id(0) == step[0]
                for d in range(1, len(grid)):
                    cond = jnp.logical_and(cond, pl.program_id(d) == step[d])
                pl.when(cond)(functools.partial(fn, cin, cout, *sems))

        run(True)
        if body_reads_comm:
            body(*own_in, *own_out, *scr, comm_refs=cout)
        else:
            body(*own_in, *own_out, *scr)
        run(False)

    in_specs = list(in_specs) + [HBM_SPEC] * c_in
    out_specs = list(out_specs) + [HBM_SPEC] * c_out
    out_shape = list(out_shape) + ([] if comm is None else list(comm.out_shape))
    scratch_shapes = list(scratch_shapes) + ([] if comm is None else [pltpu.SemaphoreType.DMA((comm.n_sems,))] * 2)
    args = tuple(args) + (() if comm is None else tuple(comm.args))
    if comm is not None:
        aliases.update({n_in + i: n_out + o for i, o in comm.aliases.items()})
    aliases = {i + n_pre: o for i, o in aliases.items()}
    if prefetch is None:
        kw = dict(grid=grid, in_specs=in_specs, out_specs=out_specs, scratch_shapes=scratch_shapes)
    else:
        kw = dict(grid_spec=pltpu.PrefetchScalarGridSpec(num_scalar_prefetch=1, grid=grid, in_specs=in_specs,
                                                         out_specs=out_specs, scratch_shapes=scratch_shapes))
        args = (prefetch,) + args
    outs = _pc(wrapped, name=name, out_shape=out_shape, input_output_aliases=aliases, compiler_params=_params(*sem), **kw)(*args)
    return outs[:n_out], outs[n_out:]


def _comm_call(comm, name):
    def body(*refs):
        c_in, c_out = len(comm.args), len(comm.out_shape)
        cin, cout, (send_sems, recv_sems) = refs[:c_in], refs[c_in:c_in + c_out], refs[c_in + c_out:]
        for phase in ("first", "late", "last"):
            for ph, fn in comm.stages:
                if ph == phase:
                    fn(cin, cout, send_sems, recv_sems)

    return _pc(body, name=name, in_specs=[HBM_SPEC] * len(comm.args), out_specs=[HBM_SPEC] * len(comm.out_shape),
               out_shape=list(comm.out_shape), scratch_shapes=[pltpu.SemaphoreType.DMA((comm.n_sems,))] * 2,
               input_output_aliases=dict(comm.aliases))(*comm.args)


def _gather_comm(bufs, whole=None, eager=0):
    n = len(bufs)
    halves = [b.shape[1] // 2 for b in bufs]
    k_ici = lambda a, r: 3 * a + r - 1
    k_d2d = lambda a, r: 3 * n + 3 * a + r - 1
    k_whole = lambda r: 6 * n + r - 1

    def half(ref, slot, c, a):
        return ref.at[slot, pl.ds(c * halves[a], halves[a])]

    def send(cin, cout, ss, rs):
        x, y, c, j = _place()
        for a in range(n):
            mine = half(cout[a], j, c, a)
            for r in (1, 2, 3):
                _remote(mine, mine, ss, rs, k_ici(a, r), _flip(x, y, c, r)).start()
        if whole is not None:
            for r in (1, 2, 3):
                _remote(cout[n].at[j], cout[n].at[j], ss, rs, k_whole(r), _flip(x, y, c, r)).start()

    def pass_on(cout, ss, rs, a, r):
        x, y, c, j = _place()
        landed = half(cout[a], j ^ r, c, a)
        _remote(landed, landed, ss, rs, k_ici(a, r), (x, y, 1 - c)).wait_recv()
        _remote(landed, landed, ss, rs, k_d2d(a, r), (x, y, 1 - c)).start()

    def passed_on(cout, ss, rs, a, r):
        x, y, c, j = _place()
        theirs = half(cout[a], j ^ r, 1 - c, a)
        _remote(theirs, theirs, ss, rs, k_d2d(a, r), (x, y, 1 - c)).wait_recv()

    def arrive(r, cin, cout, ss, rs):
        for a in range(eager):
            pass_on(cout, ss, rs, a, r)
        for a in range(eager):
            passed_on(cout, ss, rs, a, r)

    def forward(cin, cout, ss, rs):
        for a in range(eager, n):
            for r in (1, 2, 3):
                pass_on(cout, ss, rs, a, r)

    def finish(cin, cout, ss, rs):
        x, y, c, j = _place()
        sibling = (x, y, 1 - c)
        for a in range(eager, n):
            for r in (1, 2, 3):
                passed_on(cout, ss, rs, a, r)
        for a in range(n):
            mine = half(cout[a], j, c, a)
            for r in (1, 2, 3):
                _remote(mine, mine, ss, rs, k_ici(a, r), sibling).wait_send()
                landed = half(cout[a], j ^ r, c, a)
                _remote(landed, landed, ss, rs, k_d2d(a, r), sibling).wait_send()
        if whole is not None:
            for r in (1, 2, 3):
                cp = _remote(cout[n].at[j ^ r], cout[n].at[j ^ r], ss, rs, k_whole(r), sibling)
                cp.wait_recv()
                cp.wait_send()

    args = tuple(bufs) + ((whole,) if whole is not None else ())
    out_shape = tuple(jax.ShapeDtypeStruct(b.shape, b.dtype) for b in args)
    aliases = {a: a for a in range(len(args))}
    arrivals = tuple(((r, 0), functools.partial(arrive, r)) for r in (1, 2, 3)) if eager else ()
    return _Comm(args, out_shape, aliases, 6 * n + 3, (("first", send),) + arrivals + (("late", forward), ("last", finish)))


def _pair_comm(grads, small=None):
    n = len(grads)
    halves = [g.shape[1] // 2 for g in grads]

    def copies(cin, cout, ss, rs):
        x, y, c, _ = _place()
        sibling = (x, y, 1 - c)
        cps = [_remote(cin[a].at[:, pl.ds((1 - c) * halves[a], halves[a]), :], cout[a], ss, rs, a, sibling) for a in range(n)]
        if small is not None:
            cps.append(_remote(cin[n], cout[n], ss, rs, n, sibling))
        return cps

    def start(cin, cout, ss, rs):
        for cp in copies(cin, cout, ss, rs):
            cp.start()

    def finish(cin, cout, ss, rs):
        for cp in copies(cin, cout, ss, rs):
            cp.wait()

    args = tuple(grads) + ((small,) if small is not None else ())
    out_shape = tuple(jax.ShapeDtypeStruct((N_CHIP, h, g.shape[2]), F32) for g, h in zip(grads, halves))
    out_shape += (jax.ShapeDtypeStruct(small.shape, F32),) if small is not None else ()
    return _Comm(args, out_shape, {}, n + 1, (("first", start), ("last", finish)))


def _chips_comm(sums_bf, small=None):
    n = len(sums_bf)

    def copies(cin, cout, ss, rs):
        x, y, c, j = _place()
        cps = []
        for r in (1, 2, 3):
            peer = _flip(x, y, c, r)
            for a in range(n):
                cps.append(_remote(cin[a].at[j ^ r], cout[a].at[r - 1], ss, rs, (n + 1) * (r - 1) + a, peer))
            if small is not None:
                cps.append(_remote(cin[n], cout[n].at[r - 1], ss, rs, (n + 1) * (r - 1) + n, peer))
        return cps

    def start(cin, cout, ss, rs):
        for cp in copies(cin, cout, ss, rs):
            cp.start()

    def finish(cin, cout, ss, rs):
        for cp in copies(cin, cout, ss, rs):
            cp.wait()

    args = tuple(sums_bf) + ((small,) if small is not None else ())
    out_shape = tuple(jax.ShapeDtypeStruct((3,) + s.shape[1:], BF) for s in sums_bf)
    out_shape += (jax.ShapeDtypeStruct((3,) + small.shape, F32),) if small is not None else ()
    return _Comm(args, out_shape, {}, 3 * (n + 1), (("first", start), ("last", finish)))


def _join_comm(shards):
    n = len(shards)
    halves = [s.shape[0] // 2 for s in shards]

    def start(cin, cout, ss, rs):
        x, y, c, _ = _place()
        for a in range(n):
            mine = cout[a].at[pl.ds(c * halves[a], halves[a]), :]
            _remote(mine, mine, ss, rs, a, (x, y, 1 - c)).start()

    def finish(cin, cout, ss, rs):
        x, y, c, _ = _place()
        for a in range(n):
            theirs = cout[a].at[pl.ds((1 - c) * halves[a], halves[a]), :]
            cp = _remote(theirs, theirs, ss, rs, a, (x, y, 1 - c))
            cp.wait_recv()
            cp.wait_send()

    out_shape = tuple(jax.ShapeDtypeStruct(s.shape, F32) for s in shards)
    return _Comm(tuple(shards), out_shape, {a: a for a in range(n)}, n, (("first", start), ("last", finish)))


def _cast_shards(w_in, w_pa, w_pb, w_o, w_ff1, w_ff2, chip):
    def body(j_ref, win_ref, wpa_ref, wpb_ref, wo_ref, wff1_ref, wff2_ref, win4_ref, proj4_ref, ff14_ref, ff24_ref):
        win4_ref[...] = win_ref[...].astype(BF)
        for name, ref in (("w_pa", wpa_ref), ("w_pb", wpb_ref), ("w_o", wo_ref)):
            off, rows = PROJ_OFF[name]
            proj4_ref[off:off + rows, :] = ref[...].astype(BF)
        ff14_ref[...] = wff1_ref[...].astype(BF)
        ff24_ref[...] = wff2_ref[...].astype(BF)

    whole = lambda a: pl.BlockSpec(a.shape, lambda i, j: (0, 0), pipeline_mode=pl.Buffered(1))
    slot = lambda rows, cols: pl.BlockSpec((None, rows, cols), lambda i, j: (j[0], 0, 0))
    ws = (w_in, w_pa, w_pb, w_o, w_ff1, w_ff2)
    shapes = ((D, NP_SHARD), (PROJ_TOTAL, D), (D, FF_SHARD), (FF_SHARD, D))
    return _pc(
        body, name="cast_shards",
        grid_spec=pltpu.PrefetchScalarGridSpec(num_scalar_prefetch=1, grid=(1,), in_specs=[whole(w) for w in ws],
                                               out_specs=[slot(*s) for s in shapes]),
        out_shape=[jax.ShapeDtypeStruct((N_CHIP,) + s, BF) for s in shapes],
        compiler_params=_params("arbitrary"))(chip, *ws)


def _proj_fwd(x, chip, tm, comm):
    t = x.shape[0]
    sub = tm // EDGE_TILE

    def body(x_ref, p_ref, edge_ref, w_ref, w_sem, comm_refs):
        @pl.when(pl.program_id(1) == 0)
        def _():
            _, _, _, j = _place()
            block = pltpu.make_async_copy(comm_refs[0].at[j ^ pl.program_id(0)], w_ref, w_sem)
            block.start()
            block.wait()

        p_ref[...] = _dot(x_ref[...].astype(BF), w_ref[...])
        _write_edges(edge_ref, p_ref, tm)

    return _host_call(
        body, comm, name="proj_fwd", grid=(N_CHIP, t // tm), prefetch=chip, body_reads_comm=True,
        in_specs=[pl.BlockSpec((tm, D), lambda r, i, j: (i, 0))],
        out_specs=[pl.BlockSpec((tm, NP_SHARD), lambda r, i, j: (i, j[0] ^ r)),
                   pl.BlockSpec((sub, 2 * HALO, NP_SHARD), lambda r, i, j: (i, 0, j[0] ^ r))],
        out_shape=[jax.ShapeDtypeStruct((t, N_PROJ), F32), jax.ShapeDtypeStruct((t // EDGE_TILE, 2 * HALO, N_PROJ), F32)],
        scratch_shapes=[pltpu.VMEM((D, NP_SHARD), BF), pltpu.SemaphoreType.DMA],
        args=(x,))


def _edge_specs(t, tm, w):
    k, last = tm // EDGE_TILE, t // EDGE_TILE - 1
    return [pl.BlockSpec((None, HALO, w), lambda i: (jnp.maximum(i * k - 1, 0), 1, 0)),
            pl.BlockSpec((None, HALO, w), lambda i: (jnp.minimum((i + 1) * k, last), 0, 0))]


def _write_edges(edge_ref, rows, tm):
    for s in range(tm // EDGE_TILE):
        edge_ref[s, 0:HALO, :] = rows[s * EDGE_TILE:s * EDGE_TILE + HALO, :]
        edge_ref[s, HALO:2 * HALO, :] = rows[(s + 1) * EDGE_TILE - HALO:(s + 1) * EDGE_TILE, :]


def _pcols(p_ref, lo, hi):
    return p_ref[:, lo:hi]


def _end_masks(nt):
    i = pl.program_id(0)
    return (i > 0).astype(F32), (i < nt - 1).astype(F32)


def _conv_fwd(p_ref, prev_ref, next_ref, cw_ref, tm, has_prev, has_next):
    ca = _pcols(p_ref, OFF_CA, OFF_HA)
    ha = _pcols(p_ref, OFF_HA, OFF_UB)
    ch = ca * ha
    ch_prev = prev_ref[HALO - 1:HALO, OFF_CA:OFF_HA] * prev_ref[HALO - 1:HALO, OFF_HA:OFF_UB] * has_prev
    ch_next = next_ref[0:1, OFF_CA:OFF_HA] * next_ref[0:1, OFF_HA:OFF_UB] * has_next
    row = lax.broadcasted_iota(jnp.int32, (tm, W_A), 0)
    ch_m1 = jnp.where(row == 0, ch_prev, pltpu.roll(ch, 1, 0))
    ch_p1 = jnp.where(row == tm - 1, ch_next, pltpu.roll(ch, tm - 1, 0))
    cv = cw_ref[0:1, :] * ch_m1 + cw_ref[1:2, :] * ch + cw_ref[2:3, :] * ch_p1
    return ca, ha, ch, ch_m1, ch_p1, cv


def _spatial_fwd(p_ref, vg_ref, vb_ref, ws_ref, bsf_ref, vnb_ref, mixed_ref, tm):
    vb_pre = _pcols(p_ref, OFF_VB, OFF_GA)
    gv, tv = _gelu(vb_pre)
    xhv, rstdv = _ln_stats(gv)
    vnb_ref[...] = (xhv * vg_ref[...] + vb_ref[...]).astype(BF)
    for c in range(tm // CHUNK):
        rows = slice(c * CHUNK, (c + 1) * CHUNK)
        for h in range(N_HEAD):
            cols = slice(h * CHUNK, (h + 1) * CHUNK)
            mixed_ref[rows, cols] = _dot(ws_ref[h], vnb_ref[rows, cols]) + bsf_ref[:, cols]
    return vb_pre, tv, xhv, rstdv


def _mix_fwd(p, pedge, x, wpa, wpb, wo, wsb, bsf, bg, cw, vg, vb, tm):
    t = x.shape[0]
    nt = t // tm

    def body(p_ref, prev_ref, next_ref, x_ref, wpa_ref, wpb_ref, wo_ref, ws_ref, bsf_ref, bg_ref, cw_ref, vg_ref, vb_ref,
             r1_ref, ya_ref, yb_ref, vnb_ref, mixed_ref):
        has_prev, has_next = _end_masks(nt)
        _, _, _, _, _, cv = _conv_fwd(p_ref, prev_ref, next_ref, cw_ref, tm, has_prev, has_next)
        a = _pcols(p_ref, 0, OFF_CA) * cv
        ya = _dot(a.astype(BF), wpa_ref[...])
        ya_ref[...] = ya
        _spatial_fwd(p_ref, vg_ref, vb_ref, ws_ref, bsf_ref, vnb_ref, mixed_ref, tm)
        gu, _ = _gelu(_pcols(p_ref, OFF_UB, OFF_VB))
        bb = gu * mixed_ref[...]
        yb = _dot(bb.astype(BF), wpb_ref[...])
        yb_ref[...] = yb
        ga = jax.nn.sigmoid(_pcols(p_ref, OFF_GA, OFF_GB) + bg_ref[:, 0:D])
        gb = jax.nn.sigmoid(_pcols(p_ref, OFF_GB, N_PROJ) + bg_ref[:, D:2 * D])
        z = ga * ya + gb * yb
        r1_ref[...] = ALPHA * x_ref[...] + _dot(z.astype(BF), wo_ref[...])

    tile = lambda w: pl.BlockSpec((tm, w), lambda i: (i, 0))
    return _pc(
        body, name="mix_fwd", grid=(nt,),
        in_specs=[tile(N_PROJ), *_edge_specs(t, tm, N_PROJ), tile(D),
                  _resident((W_A, D)), _resident((W_B, D)), _resident((D, D)), _resident((N_HEAD, CHUNK, CHUNK)),
                  _resident((CHUNK, W_B)), _resident((1, 2 * D)), _resident((3, W_A)), _resident((1, W_B)),
                  _resident((1, W_B))],
        out_specs=[tile(D), tile(D), tile(D)],
        out_shape=[jax.ShapeDtypeStruct((t, D), F32)] * 3,
        scratch_shapes=[pltpu.VMEM((tm, W_B), BF), pltpu.VMEM((tm, W_B), F32)],
        compiler_params=_params("arbitrary"),
    )(p, pedge, pedge, x, wpa, wpb, wo, wsb, bsf, bg, cw, vg, vb)


def _ffn_fwd_bwd(r1, tgt, wff1, wff2, ln1g, ln1b, ln2g, ln2b, tm):
    t = r1.shape[0]

    def body(r1_ref, tgt_ref, w1_ref, w2_ref, g1_ref, b1_ref, g2_ref, b2_ref,
             dr1_ref, dr1b_ref, dedge_ref, x1b_ref, hidb_ref, dh1b_ref, dr2b_ref, acc_ref, relu_ref):
        @pl.when(pl.program_id(0) == 0)
        def _():
            acc_ref[...] = jnp.zeros_like(acc_ref)

        xh1, rstd1 = _ln_stats(r1_ref[...])
        x1 = xh1 * g1_ref[...] + b1_ref[...]
        x1b_ref[...] = x1.astype(BF)
        ffn = jnp.zeros((tm, D), F32)
        for j in range(N_CHIP):
            cols = slice(j * FF_SHARD, (j + 1) * FF_SHARD)
            r = jnp.maximum(_dot(x1b_ref[...], w1_ref[j]), 0.0)
            relu_ref[:, cols] = r
            hidb_ref[:, cols] = (r * r).astype(BF)
            ffn = ffn + _dot(hidb_ref[:, cols], w2_ref[cols, :])
        xh2, rstd2 = _ln_stats(ALPHA * x1 + ffn)
        diff = xh2 * g2_ref[...] + b2_ref[...] - tgt_ref[...]
        acc_ref[4:5, :] += _colsum(diff * diff)
        dx2 = diff * (1.0 / D)
        acc_ref[2:3, :] += _colsum(dx2 * xh2)
        acc_ref[3:4, :] += _colsum(dx2)
        dr2 = _ln_bwd(dx2, g2_ref[...], xh2, rstd2)
        dr2b_ref[...] = dr2.astype(BF)
        dx1 = ALPHA * dr2
        for j in range(N_CHIP):
            cols = slice(j * FF_SHARD, (j + 1) * FF_SHARD)
            dhid = _dot_nt(dr2b_ref[...], w2_ref[cols, :])
            dh1b_ref[:, cols] = (dhid * (2.0 * relu_ref[:, cols])).astype(BF)
            dx1 = dx1 + _dot_nt(dh1b_ref[:, cols], w1_ref[j])
        acc_ref[0:1, :] += _colsum(dx1 * xh1)
        acc_ref[1:2, :] += _colsum(dx1)
        dr1 = _ln_bwd(dx1, g1_ref[...], xh1, rstd1)
        dr1_ref[...] = dr1
        dr1b_ref[...] = dr1.astype(BF)
        _write_edges(dedge_ref, dr1_ref, tm)

    tile = lambda w: pl.BlockSpec((tm, w), lambda i: (i, 0))
    vec = _resident((1, D))
    return _pc(
        body, name="ffn_fwd_bwd", grid=(t // tm,),
        in_specs=[tile(D), tile(D), _resident((N_CHIP, D, FF_SHARD)), _resident((D_FF, D)), vec, vec, vec, vec],
        out_specs=[tile(D), tile(D), pl.BlockSpec((tm // EDGE_TILE, 2 * HALO, D), lambda i: (i, 0, 0)), tile(D), tile(D_FF),
                   tile(D_FF), tile(D), pl.BlockSpec((8, D), lambda i: (0, 0))],
        out_shape=[jax.ShapeDtypeStruct((t, D), F32), jax.ShapeDtypeStruct((t, D), BF),
                   jax.ShapeDtypeStruct((t // EDGE_TILE, 2 * HALO, D), F32), jax.ShapeDtypeStruct((t, D), BF),
                   jax.ShapeDtypeStruct((t, D_FF), BF), jax.ShapeDtypeStruct((t, D_FF), BF),
                   jax.ShapeDtypeStruct((t, D), BF), jax.ShapeDtypeStruct((8, D), F32)],
        scratch_shapes=[pltpu.VMEM((tm, D_FF), F32)],
        compiler_params=_params("arbitrary"),
    )(r1, tgt, wff1, wff2, ln1g, ln1b, ln2g, ln2b)


def _dw(a, b, nblk, am, bn, a_blocked, b_blocked, tk, name, comm=None):
    t = a.shape[0]

    def body(a_ref, b_ref, o_ref):
        @pl.when(pl.program_id(1) == 0)
        def _():
            o_ref[...] = jnp.zeros_like(o_ref)

        o_ref[...] += _dot_tn(a_ref[...].astype(BF), b_ref[...])

    outs, got = _host_call(
        body, comm, name=name, grid=(nblk, t // tk),
        in_specs=[pl.BlockSpec((tk, am), (lambda j, k: (k, j)) if a_blocked else (lambda j, k: (k, 0))),
                  pl.BlockSpec((tk, bn), (lambda j, k: (k, j)) if b_blocked else (lambda j, k: (k, 0)))],
        out_specs=[pl.BlockSpec((None, am, bn), lambda j, k: (j, 0, 0))],
        out_shape=[jax.ShapeDtypeStruct((nblk, am, bn), F32)], args=(a, b))
    return outs[0] if comm is None else (outs[0], got)


def _dw_proj(ab, dyab, bbb, dybb, zb, dr1b, tk, comm):
    t = ab.shape[0]
    pairs = (("w_pa", 0, 1), ("w_pb", 2, 3), ("w_o", 4, 5))

    def body(*refs):
        o_ref = refs[6]

        @pl.when(pl.program_id(0) == 0)
        def _():
            o_ref[...] = jnp.zeros_like(o_ref)

        for name, ia, ib in pairs:
            off, rows = PROJ_OFF[name]
            for k in range(N_CHIP):
                o_ref[k, off:off + rows, :] += _dot_tn(refs[ia][:, k * rows:(k + 1) * rows], refs[ib][...])

    tile = lambda w: pl.BlockSpec((tk, w), lambda i: (i, 0))
    outs, got = _host_call(
        body, comm, name="dw_proj", grid=(t // tk,), in_specs=[tile(W_A), tile(D), tile(W_B), tile(D), tile(D), tile(D)],
        out_specs=[pl.BlockSpec((N_CHIP, PROJ_TOTAL, D), lambda i: (0, 0, 0))],
        out_shape=[jax.ShapeDtypeStruct((N_CHIP, PROJ_TOTAL, D), F32)], args=(ab, dyab, bbb, dybb, zb, dr1b))
    return outs[0], got


def _dx(dp, win4, dr1, tm, blk0, nblk, filled, name, comm):
    t = dp.shape[0]

    def body(dp_ref, w_ref, dr1_ref, *rest):
        dx = ALPHA * dr1_ref[...]
        for j in range(N_CHIP):
            dx = dx + _dot_nt(dp_ref[:, j * NP_SHARD:(j + 1) * NP_SHARD], w_ref[j])
        rest[-1][...] = dx

    in_specs = [pl.BlockSpec((tm, N_PROJ), lambda i: (i + blk0, 0)), _resident((N_CHIP, D, NP_SHARD)),
                pl.BlockSpec((tm, D), lambda i: (i + blk0, 0))]
    args = (dp, win4, dr1)
    aliases = None
    if filled is not None:
        in_specs.append(pl.BlockSpec(memory_space=pl.ANY))
        args += (filled,)
        aliases = {3: 0}
    outs, got = _host_call(
        body, comm, name=name, grid=(nblk,), in_specs=in_specs, out_specs=[pl.BlockSpec((tm, D), lambda i: (i + blk0, 0))],
        out_shape=[jax.ShapeDtypeStruct((t, D), F32)], args=args, aliases=aliases)
    return outs[0], got


def _mix_bwd(p, pedge, dr1, dedge, ya, yb, wpa, wpb, wo, wsb, wstb, bsf, bg, cw, vg, vb, tm, comm):
    t = p.shape[0]
    nt = t // tm
    te = tm + 2 * HALO
    mid = slice(HALO, HALO + tm)

    def body(p_ref, prev_ref, next_ref, dr1_ref, dprev_ref, dnext_ref, ya_ref, yb_ref, wpa_ref, wpb_ref, wo_ref,
             ws_ref, wst_ref, bsf_ref, bg_ref, cw_ref, vg_ref, vb_ref,
             dp_ref, ab_ref, bbb_ref, zb_ref, dyab_ref, dybb_ref, dbg_ref, dcw_ref, dvgb_ref, dws_ref, dbs_ref,
             vnb_ref, mixed_ref, dmixb_ref, dvn_ref):
        @pl.when(pl.program_id(0) == 0)
        def _():
            for r in (dbg_ref, dcw_ref, dvgb_ref, dws_ref, dbs_ref):
                r[...] = jnp.zeros_like(r)

        has_prev, has_next = _end_masks(nt)
        ca, ha, ch, ch_m1, ch_p1, cv = _conv_fwd(p_ref, prev_ref, next_ref, cw_ref, tm, has_prev, has_next)
        ba = _pcols(p_ref, 0, OFF_CA)
        ab_ref[...] = (ba * cv).astype(BF)
        vb_pre, tv, xhv, rstdv = _spatial_fwd(p_ref, vg_ref, vb_ref, ws_ref, bsf_ref, vnb_ref, mixed_ref, tm)
        ub = _pcols(p_ref, OFF_UB, OFF_VB)
        gu, tu = _gelu(ub)
        bbb_ref[...] = (gu * mixed_ref[...]).astype(BF)
        bga = bg_ref[:, 0:D]
        ga = jax.nn.sigmoid(_pcols(p_ref, OFF_GA, OFF_GB) + bga)
        gb = jax.nn.sigmoid(_pcols(p_ref, OFF_GB, N_PROJ) + bg_ref[:, D:2 * D])
        ya = ya_ref[...]
        yb = yb_ref[...]
        zb_ref[...] = (ga * ya + gb * yb).astype(BF)

        dr1_ext = jnp.concatenate([dprev_ref[...] * has_prev, dr1_ref[...], dnext_ref[...] * has_next], axis=0)
        dz_ext = _dot_nt(dr1_ext.astype(BF), wo_ref[...])
        ga_ext = jnp.concatenate([jax.nn.sigmoid(prev_ref[:, OFF_GA:OFF_GB] + bga), ga,
                                  jax.nn.sigmoid(next_ref[:, OFF_GA:OFF_GB] + bga)], axis=0)
        dya_ext = dz_ext * ga_ext
        dyab_ref[...] = dya_ext[mid].astype(BF)
        da_ext = _dot_nt(dya_ext.astype(BF), wpa_ref[...])
        ba_ext = jnp.concatenate([prev_ref[:, 0:OFF_CA], ba, next_ref[:, 0:OFF_CA]], axis=0)
        dcv_ext = da_ext * ba_ext
        dcv = dcv_ext[mid]
        dch = (cw_ref[0:1, :] * pltpu.roll(dcv_ext, te - 1, 0)[mid] + cw_ref[1:2, :] * dcv
               + cw_ref[2:3, :] * pltpu.roll(dcv_ext, 1, 0)[mid])
        dp_ref[:, 0:OFF_CA] = (da_ext[mid] * cv).astype(BF)
        dp_ref[:, OFF_CA:OFF_HA] = (dch * ha).astype(BF)
        dp_ref[:, OFF_HA:OFF_UB] = (dch * ca).astype(BF)
        dcw_ref[0:1, :] += _colsum(dcv * ch_m1)
        dcw_ref[1:2, :] += _colsum(dcv * ch)
        dcw_ref[2:3, :] += _colsum(dcv * ch_p1)

        dz = dz_ext[mid]
        dga = dz * ya * ga * (1.0 - ga)
        dgb = dz * yb * gb * (1.0 - gb)
        dp_ref[:, OFF_GA:OFF_GB] = dga.astype(BF)
        dp_ref[:, OFF_GB:N_PROJ] = dgb.astype(BF)
        dbg_ref[0:1, 0:D] += _colsum(dga)
        dbg_ref[0:1, D:2 * D] += _colsum(dgb)

        dybb_ref[...] = (dz * gb).astype(BF)
        dbb = _dot_nt(dybb_ref[...], wpb_ref[...])
        dp_ref[:, OFF_UB:OFF_VB] = (dbb * mixed_ref[...] * _gelu_grad(ub, tu)).astype(BF)
        dmixed = dbb * gu
        dmixb_ref[...] = dmixed.astype(BF)
        for c in range(tm // CHUNK):
            rows = slice(c * CHUNK, (c + 1) * CHUNK)
            dbs_ref[...] += dmixed[rows]
            for h in range(N_HEAD):
                cols = slice(h * CHUNK, (h + 1) * CHUNK)
                dws_ref[h] += _dot_nt(dmixb_ref[rows, cols], vnb_ref[rows, cols])
                dvn_ref[rows, cols] = _dot(wst_ref[h], dmixb_ref[rows, cols])
        dvn = dvn_ref[...]
        dvgb_ref[0:1, :] += _colsum(dvn * xhv)
        dvgb_ref[1:2, :] += _colsum(dvn)
        dgv = _ln_bwd(dvn, vg_ref[...], xhv, rstdv)
        dp_ref[:, OFF_VB:OFF_GA] = (dgv * _gelu_grad(vb_pre, tv)).astype(BF)

    tile = lambda w: pl.BlockSpec((tm, w), lambda i: (i, 0))
    acc = lambda *s: pl.BlockSpec(s, lambda i: (0,) * len(s))
    return _host_call(
        body, comm, name="mix_bwd", grid=(nt,),
        in_specs=[tile(N_PROJ), *_edge_specs(t, tm, N_PROJ), tile(D), *_edge_specs(t, tm, D), tile(D), tile(D),
                  _resident((W_A, D)), _resident((W_B, D)), _resident((D, D)), _resident((N_HEAD, CHUNK, CHUNK)),
                  _resident((N_HEAD, CHUNK, CHUNK)), _resident((CHUNK, W_B)), _resident((1, 2 * D)),
                  _resident((3, W_A)), _resident((1, W_B)), _resident((1, W_B))],
        out_specs=[tile(N_PROJ), tile(W_A), tile(W_B), tile(D), tile(D), tile(D),
                   acc(8, 2 * D), acc(8, W_A), acc(8, W_B), acc(N_HEAD, CHUNK, CHUNK), acc(CHUNK, W_B)],
        out_shape=[jax.ShapeDtypeStruct((t, N_PROJ), BF), jax.ShapeDtypeStruct((t, W_A), BF),
                   jax.ShapeDtypeStruct((t, W_B), BF), jax.ShapeDtypeStruct((t, D), BF), jax.ShapeDtypeStruct((t, D), BF),
                   jax.ShapeDtypeStruct((t, D), BF), jax.ShapeDtypeStruct((8, 2 * D), F32),
                   jax.ShapeDtypeStruct((8, W_A), F32), jax.ShapeDtypeStruct((8, W_B), F32),
                   jax.ShapeDtypeStruct((N_HEAD, CHUNK, CHUNK), F32), jax.ShapeDtypeStruct((CHUNK, W_B), F32)],
        scratch_shapes=[pltpu.VMEM((tm, W_B), BF), pltpu.VMEM((tm, W_B), F32), pltpu.VMEM((tm, W_B), BF),
                        pltpu.VMEM((tm, W_B), F32)],
        args=(p, pedge, pedge, dr1, dedge, dedge, ya, yb, wpa, wpb, wo, wsb, wstb, bsf, bg, cw, vg, vb))


def _add_own_half(full4, recv4, place, rb, name):
    n, rh, cols = recv4.shape
    nb = rh // rb

    def body(pl_ref, a_ref, b_ref, own_ref, ob_ref):
        s = a_ref[...] + b_ref[...]
        ob_ref[...] = s.astype(BF)

        @pl.when(pl.program_id(1) == pl_ref[0])
        def _():
            own_ref[...] = s

    blk = (None, rb, cols)
    return _pc(
        body, name=name,
        grid_spec=pltpu.PrefetchScalarGridSpec(
            num_scalar_prefetch=1, grid=(nb, n),
            in_specs=[pl.BlockSpec(blk, lambda i, k, s: (k, s[1] * nb + i, 0)), pl.BlockSpec(blk, lambda i, k, s: (k, i, 0))],
            out_specs=[pl.BlockSpec((rb, cols), lambda i, k, s: (i, 0)), pl.BlockSpec(blk, lambda i, k, s: (k, i, 0))]),
        out_shape=[jax.ShapeDtypeStruct((rh, cols), F32), jax.ShapeDtypeStruct(recv4.shape, BF)],
        compiler_params=_params("arbitrary", "arbitrary"),
    )(place, full4, recv4)


def _add_chips(own, r3, place, rb, name):
    _, rh, cols = r3.shape
    nb = rh // rb

    def body(pl_ref, s_ref, r_ref, o_ref):
        o_ref[...] = ((s_ref[...] + r_ref[0].astype(F32)) + r_ref[1].astype(F32)) + r_ref[2].astype(F32)

    return _pc(
        body, name=name,
        grid_spec=pltpu.PrefetchScalarGridSpec(
            num_scalar_prefetch=1, grid=(nb,),
            in_specs=[pl.BlockSpec((rb, cols), lambda i, s: (i, 0)), pl.BlockSpec((3, rb, cols), lambda i, s: (0, i, 0))],
            out_specs=pl.BlockSpec((rb, cols), lambda i, s: (s[1] * nb + i, 0))),
        out_shape=jax.ShapeDtypeStruct((2 * rh, cols), F32),
        compiler_params=_params("arbitrary"),
    )(place, own, r3)


def _add_small(a, b):
    def body(a_ref, b_ref, o_ref):
        o_ref[...] = a_ref[...] + b_ref[...]

    return _pc(body, name="add_small_cores", out_shape=jax.ShapeDtypeStruct(a.shape, F32))(a, b)


def _sum_small_chips(own, slots, place):
    def body(pl_ref, own_ref, s_ref, o_ref):
        j = pl_ref[0]

        def term(k):
            return jnp.where(j == k, own_ref[...], s_ref[jnp.maximum((j ^ k) - 1, 0)])

        o_ref[...] = ((term(0) + term(1)) + term(2)) + term(3)

    vmem = pl.BlockSpec(memory_space=pltpu.VMEM)
    return _pc(body, name="sum_small_chips", in_specs=[pl.BlockSpec(memory_space=pltpu.SMEM), vmem, vmem], out_specs=vmem,
               out_shape=jax.ShapeDtypeStruct(own.shape, F32))(place, own, slots)


def _adamw(w, g, m, v, rb, name):
    rows, cols = w.shape

    def body(w_ref, g_ref, m_ref, v_ref, d_ref, m2_ref, v2_ref):
        g_ = g_ref[...]
        m2 = ADAM_B1 * m_ref[...] + (1.0 - ADAM_B1) * g_
        v2 = ADAM_B2 * v_ref[...] + (1.0 - ADAM_B2) * (g_ * g_)
        m_hat = m2 / (1.0 - ADAM_B1 ** ADAM_STEP)
        v_hat = v2 / (1.0 - ADAM_B2 ** ADAM_STEP)
        d_ref[...] = -ADAM_LR * (m_hat / (jnp.sqrt(v_hat) + ADAM_EPS) + ADAM_WD * w_ref[...])
        m2_ref[...] = m2
        v2_ref[...] = v2

    blk = pl.BlockSpec((rb, cols), lambda i: (i, 0))
    return _pc(body, name=name, grid=(rows // rb,), in_specs=[blk] * 4, out_specs=[blk] * 3,
               out_shape=[jax.ShapeDtypeStruct((rows, cols), F32)] * 3, compiler_params=_params("arbitrary"))(w, g, m, v)


LANES = 128
SMALL_GRADS = (("b_gate", 2 * D), ("conv_w", 3 * W_A), ("v_norm_g", W_B), ("v_norm_b", W_B),
               ("w_s", N_HEAD * CHUNK * CHUNK), ("b_s", N_HEAD * CHUNK), ("ln1_g", D), ("ln1_b", D), ("ln2_g", D), ("ln2_b", D),
               ("loss", 1))


def _pack_rows(parts):
    rows = []
    for a in parts:
        a = a.reshape(-1)
        a = jnp.pad(a, (0, (-a.shape[0]) % LANES))
        rows.append(a.reshape(-1, LANES))
    out = jnp.concatenate(rows, axis=0)
    return jnp.pad(out, ((0, (-out.shape[0]) % 8), (0, 0)))


def _unpack_rows(buf, sizes):
    out, r = [], 0
    for n in sizes:
        nr = -(-n // LANES)
        out.append(buf[r:r + nr].reshape(-1)[:n])
        r += nr
    return out


TM_PROJ = 1024
TM_MIX = 256
TM_DX = 512
DX_PAIR = 6
TK_DW = 4096
TK_DW_IN = 2048
TK_DW_PROJ = 1024
ADD_BLOCK_BYTES = 3 * 1024 * 1024
RB_ADAM = 128
CONV_ROWS = 8


def _add_rows(rows, cols):
    while rows * cols * 4 > ADD_BLOCK_BYTES and rows % 32 == 0:
        rows //= 2
    return rows


def _reduce_adds_1(grads, recvs, place, tag):
    out = [_add_own_half(g, r, place, _add_rows(*r.shape[1:]), f"add_cores_{tag}{a}") for a, (g, r) in enumerate(zip(grads, recvs))]
    return [o[0] for o in out], [o[1] for o in out]


def _reduce_adds_2(sums, recvs, place, tag):
    return [_add_chips(s, r, place, _add_rows(*r.shape[1:]), f"add_chips_{tag}{a}") for a, (s, r) in enumerate(zip(sums, recvs))]


def kernel(x, w_in, b_gate, conv_w, v_norm_g, v_norm_b, w_s, b_s, w_pa, w_pb, w_o, ln1_g, ln1_b, w_ff1, w_ff2, ln2_g, ln2_b, loss_target, m_w_in, m_b_gate, m_conv_w, m_v_norm_g, m_v_norm_b, m_w_s, m_b_s, m_w_pa, m_w_pb, m_w_o, m_ln1_g, m_ln1_b, m_w_ff1, m_w_ff2, m_ln2_g, m_ln2_b, v_w_in, v_b_gate, v_conv_w, v_v_norm_g, v_v_norm_b, v_w_s, v_b_s, v_w_pa, v_w_pb, v_w_o, v_ln1_g, v_ln1_b, v_w_ff1, v_w_ff2, v_ln2_g, v_ln2_b):
    t = x.shape[1]
    core = lax.axis_index("c").astype(jnp.int32).reshape(1)
    chip_idx = 2 * lax.axis_index("x") + lax.axis_index("y")
    chip = chip_idx.astype(jnp.int32).reshape(1)
    place = jnp.concatenate([chip, core])
    x2 = x.reshape(t, D)
    tgt = loss_target.reshape(t, D)

    win4, proj4, ff14, ff24 = _cast_shards(w_in[0], w_pa[0], w_pb[0], w_o[0], w_ff1[0], w_ff2[0], chip)
    conv4 = lax.dynamic_update_slice(jnp.zeros((N_CHIP, CONV_ROWS, W_A // N_CHIP), F32),
                                     jnp.pad(conv_w[0], ((0, CONV_ROWS - 3), (0, 0)))[None], (chip_idx, 0, 0))
    (p, pedge), (win4, proj4, ff14, ff24, conv4) = _proj_fwd(
        x2, chip, TM_PROJ, _gather_comm([win4, proj4, ff14, ff24], conv4, eager=1))

    def full(name, rows_total):
        off, rows = PROJ_OFF[name]
        return proj4[:, off:off + rows, :].reshape(rows_total, D)

    wpa, wpb, wo = full("w_pa", W_A), full("w_pb", W_B), full("w_o", D)
    wff2 = ff24.reshape(D_FF, D)
    cw = jnp.transpose(conv4[:, :3, :], (1, 0, 2)).reshape(3, W_A)
    wsb = w_s[0].astype(BF)
    wstb = jnp.swapaxes(w_s[0], 1, 2).astype(BF)
    bsf = jnp.repeat(jnp.transpose(b_s[0]), CHUNK, axis=1)

    r1, ya, yb = _mix_fwd(p, pedge, x2, wpa, wpb, wo, wsb, bsf, b_gate, cw, v_norm_g, v_norm_b, TM_MIX)
    dr1, dr1b, dedge, x1b, hidb, dh1b, dr2b, acc = _ffn_fwd_bwd(r1, tgt, ff14, wff2, ln1_g, ln1_b, ln2_g, ln2_b, TM_MIX)
    g_ff = [_dw(x1b, dh1b, N_CHIP, D, FF_SHARD, False, True, TK_DW, "dw_ff1"),
            _dw(hidb, dr2b, N_CHIP, FF_SHARD, D, True, False, TK_DW, "dw_ff2")]
    (dp, ab, bbb, zb, dyab, dybb, dbg, dcw, dvgb, dws, dbs_sum), r_ff = _mix_bwd(
        p, pedge, dr1, dedge, ya, yb, wpa, wpb, wo, wsb, wstb, bsf, b_gate, cw, v_norm_g, v_norm_b, TM_MIX, _pair_comm(g_ff))
    s_ff, sb_ff = _reduce_adds_1(g_ff, r_ff, place, "ff")
    dwin4, c_ff = _dw(x2, dp, N_CHIP, D, NP_SHARD, False, True, TK_DW_IN, "dw_in", _chips_comm(sb_ff))
    f_ff = _reduce_adds_2(s_ff, c_ff, place, "ff")
    dproj4, (g_ff1, g_ff2) = _dw_proj(ab, dyab, bbb, dybb, zb, dr1b, TK_DW_PROJ, _join_comm(f_ff))
    dbs = jnp.transpose(jnp.sum(dbs_sum.reshape(CHUNK, N_HEAD, CHUNK), axis=-1))
    small = _pack_rows([dbg[0], dcw[0:3], dvgb[0], dvgb[1], dws, dbs, acc[0], acc[1], acc[2], acc[3],
                        0.5 * jnp.sum(acc[4]) / D])
    g_rest = [dwin4, dproj4]
    nblk = t // TM_DX
    n_a = max(1, min(DX_PAIR, nblk // 4))
    dx, r_rest = _dx(dp, win4, dr1, TM_DX, 0, n_a, None, "dx_a", _pair_comm(g_rest, small))
    s_rest, sb_rest = _reduce_adds_1(g_rest, r_rest[:2], place, "rest")
    csmall = _add_small(small, r_rest[2])
    dx, c_rest = _dx(dp, win4, dr1, TM_DX, n_a, nblk - n_a, dx, "dx_b", _chips_comm(sb_rest, csmall))
    f_rest = _reduce_adds_2(s_rest, c_rest[:2], place, "rest")
    gsmall = _sum_small_chips(csmall, c_rest[2], place)
    g_in, g_proj = _comm_call(_join_comm(f_rest), "join_rest")

    grads = {"w_in": g_in, "w_ff1": g_ff1, "w_ff2": g_ff2}
    for name, _ in PROJ_ROWS:
        off, rows = PROJ_OFF[name]
        grads[name] = g_proj[off:off + rows, :]
    for (name, n), flat in zip(SMALL_GRADS, _unpack_rows(gsmall, [n for _, n in SMALL_GRADS])):
        grads[name] = flat
    loss = grads.pop("loss").reshape(())
    grads["conv_w"] = lax.dynamic_slice(grads["conv_w"].reshape(3, W_A), (0, chip_idx * (W_A // N_CHIP)), (3, W_A // N_CHIP))

    weights = dict(w_in=w_in, b_gate=b_gate, conv_w=conv_w, v_norm_g=v_norm_g, v_norm_b=v_norm_b, w_s=w_s, b_s=b_s,
                   w_pa=w_pa, w_pb=w_pb, w_o=w_o, ln1_g=ln1_g, ln1_b=ln1_b, w_ff1=w_ff1, w_ff2=w_ff2, ln2_g=ln2_g, ln2_b=ln2_b)
    mom1 = dict(w_in=m_w_in, b_gate=m_b_gate, conv_w=m_conv_w, v_norm_g=m_v_norm_g, v_norm_b=m_v_norm_b, w_s=m_w_s,
                b_s=m_b_s, w_pa=m_w_pa, w_pb=m_w_pb, w_o=m_w_o, ln1_g=m_ln1_g, ln1_b=m_ln1_b, w_ff1=m_w_ff1,
                w_ff2=m_w_ff2, ln2_g=m_ln2_g, ln2_b=m_ln2_b)
    mom2 = dict(w_in=v_w_in, b_gate=v_b_gate, conv_w=v_conv_w, v_norm_g=v_v_norm_g, v_norm_b=v_v_norm_b, w_s=v_w_s,
                b_s=v_b_s, w_pa=v_w_pa, w_pb=v_w_pb, w_o=v_w_o, ln1_g=v_ln1_g, ln1_b=v_ln1_b, w_ff1=v_w_ff1,
                w_ff2=v_w_ff2, ln2_g=v_ln2_g, ln2_b=v_ln2_b)
    order = list(weights)
    big = ("w_in", "w_pa", "w_pb", "w_o", "w_ff1", "w_ff2")
    delta, new_m, new_v = {}, {}, {}
    for name in big:
        w2 = weights[name][0]
        delta[name], new_m[name], new_v[name] = _adamw(w2, grads[name], mom1[name][0], mom2[name][0], RB_ADAM, "adamw_" + name)
    little = [n for n in order if n not in big]
    sizes = [weights[n].size for n in little]
    wsmall = _pack_rows([weights[n] for n in little])
    ds, ms, vs = _adamw(wsmall, _pack_rows([grads[n] for n in little]), _pack_rows([mom1[n] for n in little]),
                        _pack_rows([mom2[n] for n in little]), wsmall.shape[0], "adamw_small")
    for name, d_, m_, v_ in zip(little, _unpack_rows(ds, sizes), _unpack_rows(ms, sizes), _unpack_rows(vs, sizes)):
        delta[name], new_m[name], new_v[name] = d_, m_, v_

    shaped = lambda d: [d[n].reshape(weights[n].shape) for n in order]
    return (loss, dx.reshape(x.shape), *shaped(grads), *shaped(delta), *shaped(new_m), *shaped(new_v))
```

```python
import functools
from typing import NamedTuple

import jax
import jax.numpy as jnp
from jax import lax
from jax.experimental import pallas as pl
from jax.experimental.pallas import tpu as pltpu

D = 1024
W_A = 1536
W_B = 1024
CHUNK = 128
N_HEAD = 8
D_FF = 4096
N_PROJ = 3 * W_A + 2 * W_B + 2 * D
OFF_CA, OFF_HA, OFF_UB, OFF_VB, OFF_GA, OFF_GB = 1536, 3072, 4608, 5632, 6656, 7680
LN_EPS = 1e-5
ALPHA = 2.0 ** 0.25
N_CHIP = 4
NP_SHARD = N_PROJ // N_CHIP
FF_SHARD = D_FF // N_CHIP
ADAM_LR, ADAM_B1, ADAM_B2, ADAM_EPS, ADAM_WD, ADAM_STEP = 0.001, 0.9, 0.999, 1e-08, 0.01, 10

PROJ_ROWS = (("w_pa", W_A // N_CHIP), ("w_pb", W_B // N_CHIP), ("w_o", D // N_CHIP))
PROJ_OFF = {}
_o = 0
for _n, _r in PROJ_ROWS:
    PROJ_OFF[_n] = (_o, _r)
    _o += _r
PROJ_TOTAL = _o

V7X_VMEM_BYTES = 64 * 1024 * 1024
VMEM_LIMIT = V7X_VMEM_BYTES - 8 * 1024 * 1024
HALO = 8
EDGE_TILE = 256

BF = jnp.bfloat16
F32 = jnp.float32
MESH = pl.DeviceIdType.MESH
HBM_SPEC = pl.BlockSpec(memory_space=pltpu.HBM)


def _pc(body, **kw):
    return pl.pallas_call(body, **kw)


def _params(*sem):
    return pltpu.CompilerParams(dimension_semantics=sem, vmem_limit_bytes=VMEM_LIMIT)


def _resident(shape):
    n = len(shape)
    return pl.BlockSpec(shape, lambda *_: (0,) * n, pipeline_mode=pl.Buffered(1))


def _dot(a, b):
    return jnp.dot(a, b, preferred_element_type=F32)


def _dot_nt(a, b):
    return lax.dot_general(a, b, (((1,), (1,)), ((), ())), preferred_element_type=F32)


def _dot_tn(a, b):
    return lax.dot_general(a, b, (((0,), (0,)), ((), ())), preferred_element_type=F32)


def _gelu(x):
    t = jnp.tanh(0.7978845608028654 * (x + 0.044715 * (x * x * x)))
    return 0.5 * x * (1.0 + t), t


def _gelu_grad(x, t):
    return 0.5 * (1.0 + t) + 0.5 * x * (1.0 - t * t) * (0.7978845608028654 * (1.0 + 0.134145 * (x * x)))


def _ln_stats(r):
    mu = jnp.mean(r, axis=-1, keepdims=True)
    xc = r - mu
    var = jnp.mean(xc * xc, axis=-1, keepdims=True)
    rstd = lax.rsqrt(var + LN_EPS)
    return xc * rstd, rstd


def _ln_bwd(dy, g, xh, rstd):
    dxh = dy * g
    m1 = jnp.mean(dxh, axis=-1, keepdims=True)
    m2 = jnp.mean(dxh * xh, axis=-1, keepdims=True)
    return rstd * (dxh - m1 - xh * m2)


def _colsum(v):
    return jnp.sum(v, axis=0, keepdims=True)


class _Comm(NamedTuple):
    args: tuple
    out_shape: tuple
    aliases: dict
    n_sems: int
    stages: tuple


def _place():
    x, y, c = lax.axis_index("x"), lax.axis_index("y"), lax.axis_index("c")
    return x, y, c, 2 * x + y


def _flip(x, y, c, r):
    return (x ^ (r >> 1), y ^ (r & 1), c)


def _remote(src, dst, send_sems, recv_sems, k, peer):
    return pltpu.make_async_remote_copy(src_ref=src, dst_ref=dst, send_sem=send_sems.at[k], recv_sem=recv_sems.at[k],
                                        device_id=peer, device_id_type=MESH)


def _host_call(body, comm, *, name, grid, in_specs, out_specs, out_shape, args, scratch_shapes=(), aliases=None, prefetch=None,
               body_reads_comm=False):
    sem = ("arbitrary",) * len(grid)
    aliases = dict(aliases or {})
    n_pre = 0 if prefetch is None else 1
    n_in, n_out, n_scr = len(in_specs), len(out_specs), len(scratch_shapes)
    c_in, c_out = (0, 0) if comm is None else (len(comm.args), len(comm.out_shape))
    steps = {"first": (0,) * len(grid), "late": (grid[0] - 1,) + (0,) * (len(grid) - 1), "last": tuple(g - 1 for g in grid)}

    def wrapped(*refs):
        refs = refs[n_pre:]
        own_in, cin = refs[:n_in], refs[n_in:n_in + c_in]
        o0 = n_in + c_in
        own_out, cout = refs[o0:o0 + n_out], refs[o0 + n_out:o0 + n_out + c_out]
        s0 = o0 + n_out + c_out
        scr, sems = refs[s0:s0 + n_scr], refs[s0 + n_scr:]

        def run(before):
            for phase, fn in () if comm is None else comm.stages:
                at_step = isinstance(phase, tuple)
                if before != (at_step or phase == "first"):
                    continue
                step = phase if at_step else steps[phase]
                cond = pl.program_id(0) == step[0]
                for d in range(1, len(grid)):
                    cond = jnp.logical_and(cond, pl.program_id(d) == step[d])
                pl.when(cond)(functools.partial(fn, cin, cout, *sems))

        run(True)
        if body_reads_comm:
            body(*own_in, *own_out, *scr, comm_refs=cout)
        else:
            body(*own_in, *own_out, *scr)
        run(False)

    in_specs = list(in_specs) + [HBM_SPEC] * c_in
    out_specs = list(out_specs) + [HBM_SPEC] * c_out
    out_shape = list(out_shape) + ([] if comm is None else list(comm.out_shape))
    scratch_shapes = list(scratch_shapes) + ([] if comm is None else [pltpu.SemaphoreType.DMA((comm.n_sems,))] * 2)
    args = tuple(args) + (() if comm is None else tuple(comm.args))
    if comm is not None:
        aliases.update({n_in + i: n_out + o for i, o in comm.aliases.items()})
    aliases = {i + n_pre: o for i, o in aliases.items()}
    if prefetch is None:
        kw = dict(grid=grid, in_specs=in_specs, out_specs=out_specs, scratch_shapes=scratch_shapes)
    else:
        kw = dict(grid_spec=pltpu.PrefetchScalarGridSpec(num_scalar_prefetch=1, grid=grid, in_specs=in_specs,
                                                         out_specs=out_specs, scratch_shapes=scratch_shapes))
        args = (prefetch,) + args
    outs = _pc(wrapped, name=name, out_shape=out_shape, input_output_aliases=aliases, compiler_params=_params(*sem), **kw)(*args)
    return outs[:n_out], outs[n_out:]


def _comm_call(comm, name):
    def body(*refs):
        c_in, c_out = len(comm.args), len(comm.out_shape)
        cin, cout, (send_sems, recv_sems) = refs[:c_in], refs[c_in:c_in + c_out], refs[c_in + c_out:]
        for phase in ("first", "late", "last"):
            for ph, fn in comm.stages:
                if ph == phase:
                    fn(cin, cout, send_sems, recv_sems)

    return _pc(body, name=name, in_specs=[HBM_SPEC] * len(comm.args), out_specs=[HBM_SPEC] * len(comm.out_shape),
               out_shape=list(comm.out_shape), scratch_shapes=[pltpu.SemaphoreType.DMA((comm.n_sems,))] * 2,
               input_output_aliases=dict(comm.aliases))(*comm.args)


def _gather_comm(bufs, whole=None, eager=0):
    n = len(bufs)
    halves = [b.shape[1] // 2 for b in bufs]
    k_ici = lambda a, r: 3 * a + r - 1
    k_d2d = lambda a, r: 3 * n + 3 * a + r - 1
    k_whole = lambda r: 6 * n + r - 1

    def half(ref, slot, c, a):
        return ref.at[slot, pl.ds(c * halves[a], halves[a])]

    def send(cin, cout, ss, rs):
        x, y, c, j = _place()
        for a in range(n):
            mine = half(cout[a], j, c, a)
            for r in (1, 2, 3):
                _remote(mine, mine, ss, rs, k_ici(a, r), _flip(x, y, c, r)).start()
        if whole is not None:
            for r in (1, 2, 3):
                _remote(cout[n].at[j], cout[n].at[j], ss, rs, k_whole(r), _flip(x, y, c, r)).start()

    def pass_on(cout, ss, rs, a, r):
        x, y, c, j = _place()
        landed = half(cout[a], j ^ r, c, a)
        _remote(landed, landed, ss, rs, k_ici(a, r), (x, y, 1 - c)).wait_recv()
        _remote(landed, landed, ss, rs, k_d2d(a, r), (x, y, 1 - c)).start()

    def passed_on(cout, ss, rs, a, r):
        x, y, c, j = _place()
        theirs = half(cout[a], j ^ r, 1 - c, a)
        _remote(theirs, theirs, ss, rs, k_d2d(a, r), (x, y, 1 - c)).wait_recv()

    def arrive(r, cin, cout, ss, rs):
        for a in range(eager):
            pass_on(cout, ss, rs, a, r)
        for a in range(eager):
            passed_on(cout, ss, rs, a, r)

    def forward(cin, cout, ss, rs):
        for a in range(eager, n):
            for r in (1, 2, 3):
                pass_on(cout, ss, rs, a, r)

    def finish(cin, cout, ss, rs):
        x, y, c, j = _place()
        sibling = (x, y, 1 - c)
        for a in range(eager, n):
            for r in (1, 2, 3):
                passed_on(cout, ss, rs, a, r)
        for a in range(n):
            mine = half(cout[a], j, c, a)
            for r in (1, 2, 3):
                _remote(mine, mine, ss, rs, k_ici(a, r), sibling).wait_send()
                landed = half(cout[a], j ^ r, c, a)
                _remote(landed, landed, ss, rs, k_d2d(a, r), sibling).wait_send()
        if whole is not None:
            for r in (1, 2, 3):
                cp = _remote(cout[n].at[j ^ r], cout[n].at[j ^ r], ss, rs, k_whole(r), sibling)
                cp.wait_recv()
                cp.wait_send()

    args = tuple(bufs) + ((whole,) if whole is not None else ())
    out_shape = tuple(jax.ShapeDtypeStruct(b.shape, b.dtype) for b in args)
    aliases = {a: a for a in range(len(args))}
    arrivals = tuple(((r, 0), functools.partial(arrive, r)) for r in (1, 2, 3)) if eager else ()
    return _Comm(args, out_shape, aliases, 6 * n + 3, (("first", send),) + arrivals + (("late", forward), ("last", finish)))


def _pair_comm(grads, small=None):
    n = len(grads)
    halves = [g.shape[1] // 2 for g in grads]

    def copies(cin, cout, ss, rs):
        x, y, c, _ = _place()
        sibling = (x, y, 1 - c)
        cps = [_remote(cin[a].at[:, pl.ds((1 - c) * halves[a], halves[a]), :], cout[a], ss, rs, a, sibling) for a in range(n)]
        if small is not None:
            cps.append(_remote(cin[n], cout[n], ss, rs, n, sibling))
        return cps

    def start(cin, cout, ss, rs):
        for cp in copies(cin, cout, ss, rs):
            cp.start()

    def finish(cin, cout, ss, rs):
        for cp in copies(cin, cout, ss, rs):
            cp.wait()

    args = tuple(grads) + ((small,) if small is not None else ())
    out_shape = tuple(jax.ShapeDtypeStruct((N_CHIP, h, g.shape[2]), F32) for g, h in zip(grads, halves))
    out_shape += (jax.ShapeDtypeStruct(small.shape, F32),) if small is not None else ()
    return _Comm(args, out_shape, {}, n + 1, (("first", start), ("last", finish)))


def _chips_comm(sums_bf, small=None):
    n = len(sums_bf)

    def copies(cin, cout, ss, rs):
        x, y, c, j = _place()
        cps = []
        for r in (1, 2, 3):
            peer = _flip(x, y, c, r)
            for a in range(n):
                cps.append(_remote(cin[a].at[j ^ r], cout[a].at[r - 1], ss, rs, (n + 1) * (r - 1) + a, peer))
            if small is not None:
                cps.append(_remote(cin[n], cout[n].at[r - 1], ss, rs, (n + 1) * (r - 1) + n, peer))
        return cps

    def start(cin, cout, ss, rs):
        for cp in copies(cin, cout, ss, rs):
            cp.start()

    def finish(cin, cout, ss, rs):
        for cp in copies(cin, cout, ss, rs):
            cp.wait()

    args = tuple(sums_bf) + ((small,) if small is not None else ())
    out_shape = tuple(jax.ShapeDtypeStruct((3,) + s.shape[1:], BF) for s in sums_bf)
    out_shape += (jax.ShapeDtypeStruct((3,) + small.shape, F32),) if small is not None else ()
    return _Comm(args, out_shape, {}, 3 * (n + 1), (("first", start), ("last", finish)))


def _join_comm(shards):
    n = len(shards)
    halves = [s.shape[0] // 2 for s in shards]

    def start(cin, cout, ss, rs):
        x, y, c, _ = _place()
        for a in range(n):
            mine = cout[a].at[pl.ds(c * halves[a], halves[a]), :]
            _remote(mine, mine, ss, rs, a, (x, y, 1 - c)).start()

    def finish(cin, cout, ss, rs):
        x, y, c, _ = _place()
        for a in range(n):
            theirs = cout[a].at[pl.ds((1 - c) * halves[a], halves[a]), :]
            cp = _remote(theirs, theirs, ss, rs, a, (x, y, 1 - c))
            cp.wait_recv()
            cp.wait_send()

    out_shape = tuple(jax.ShapeDtypeStruct(s.shape, F32) for s in shards)
    return _Comm(tuple(shards), out_shape, {a: a for a in range(n)}, n, (("first", start), ("last", finish)))


def _cast_shards(w_in, w_pa, w_pb, w_o, w_ff1, w_ff2, chip):
    def body(j_ref, win_ref, wpa_ref, wpb_ref, wo_ref, wff1_ref, wff2_ref, win4_ref, proj4_ref, ff14_ref, ff24_ref):
        win4_ref[...] = win_ref[...].astype(BF)
        for name, ref in (("w_pa", wpa_ref), ("w_pb", wpb_ref), ("w_o", wo_ref)):
            off, rows = PROJ_OFF[name]
            proj4_ref[off:off + rows, :] = ref[...].astype(BF)
        ff14_ref[...] = wff1_ref[...].astype(BF)
        ff24_ref[...] = wff2_ref[...].astype(BF)

    whole = lambda a: pl.BlockSpec(a.shape, lambda i, j: (0, 0), pipeline_mode=pl.Buffered(1))
    slot = lambda rows, cols: pl.BlockSpec((None, rows, cols), lambda i, j: (j[0], 0, 0))
    ws = (w_in, w_pa, w_pb, w_o, w_ff1, w_ff2)
    shapes = ((D, NP_SHARD), (PROJ_TOTAL, D), (D, FF_SHARD), (FF_SHARD, D))
    return _pc(
        body, name="cast_shards",
        grid_spec=pltpu.PrefetchScalarGridSpec(num_scalar_prefetch=1, grid=(1,), in_specs=[whole(w) for w in ws],
                                               out_specs=[slot(*s) for s in shapes]),
        out_shape=[jax.ShapeDtypeStruct((N_CHIP,) + s, BF) for s in shapes],
        compiler_params=_params("arbitrary"))(chip, *ws)


def _proj_fwd(x, chip, tm, comm):
    t = x.shape[0]
    sub = tm // EDGE_TILE

    def body(x_ref, p_ref, edge_ref, w_ref, w_sem, comm_refs):
        @pl.when(pl.program_id(1) == 0)
        def _():
            _, _, _, j = _place()
            block = pltpu.make_async_copy(comm_refs[0].at[j ^ pl.program_id(0)], w_ref, w_sem)
            block.start()
            block.wait()

        p_ref[...] = _dot(x_ref[...].astype(BF), w_ref[...])
        _write_edges(edge_ref, p_ref, tm)

    return _host_call(
        body, comm, name="proj_fwd", grid=(N_CHIP, t // tm), prefetch=chip, body_reads_comm=True,
        in_specs=[pl.BlockSpec((tm, D), lambda r, i, j: (i, 0))],
        out_specs=[pl.BlockSpec((tm, NP_SHARD), lambda r, i, j: (i, j[0] ^ r)),
                   pl.BlockSpec((sub, 2 * HALO, NP_SHARD), lambda r, i, j: (i, 0, j[0] ^ r))],
        out_shape=[jax.ShapeDtypeStruct((t, N_PROJ), F32), jax.ShapeDtypeStruct((t // EDGE_TILE, 2 * HALO, N_PROJ), F32)],
        scratch_shapes=[pltpu.VMEM((D, NP_SHARD), BF), pltpu.SemaphoreType.DMA],
        args=(x,))


def _edge_specs(t, tm, w):
    k, last = tm // EDGE_TILE, t // EDGE_TILE - 1
    return [pl.BlockSpec((None, HALO, w), lambda i: (jnp.maximum(i * k - 1, 0), 1, 0)),
            pl.BlockSpec((None, HALO, w), lambda i: (jnp.minimum((i + 1) * k, last), 0, 0))]


def _write_edges(edge_ref, rows, tm):
    for s in range(tm // EDGE_TILE):
        edge_ref[s, 0:HALO, :] = rows[s * EDGE_TILE:s * EDGE_TILE + HALO, :]
        edge_ref[s, HALO:2 * HALO, :] = rows[(s + 1) * EDGE_TILE - HALO:(s + 1) * EDGE_TILE, :]


def _pcols(p_ref, lo, hi):
    return p_ref[:, lo:hi]


def _end_masks(nt):
    i = pl.program_id(0)
    return (i > 0).astype(F32), (i < nt - 1).astype(F32)


def _conv_fwd(p_ref, prev_ref, next_ref, cw_ref, tm, has_prev, has_next):
    ca = _pcols(p_ref, OFF_CA, OFF_HA)
    ha = _pcols(p_ref, OFF_HA, OFF_UB)
    ch = ca * ha
    ch_prev = prev_ref[HALO - 1:HALO, OFF_CA:OFF_HA] * prev_ref[HALO - 1:HALO, OFF_HA:OFF_UB] * has_prev
    ch_next = next_ref[0:1, OFF_CA:OFF_HA] * next_ref[0:1, OFF_HA:OFF_UB] * has_next
    row = lax.broadcasted_iota(jnp.int32, (tm, W_A), 0)
    ch_m1 = jnp.where(row == 0, ch_prev, pltpu.roll(ch, 1, 0))
    ch_p1 = jnp.where(row == tm - 1, ch_next, pltpu.roll(ch, tm - 1, 0))
    cv = cw_ref[0:1, :] * ch_m1 + cw_ref[1:2, :] * ch + cw_ref[2:3, :] * ch_p1
    return ca, ha, ch, ch_m1, ch_p1, cv


def _spatial_fwd(p_ref, vg_ref, vb_ref, ws_ref, bsf_ref, vnb_ref, mixed_ref, tm):
    vb_pre = _pcols(p_ref, OFF_VB, OFF_GA)
    gv, tv = _gelu(vb_pre)
    xhv, rstdv = _ln_stats(gv)
    vnb_ref[...] = (xhv * vg_ref[...] + vb_ref[...]).astype(BF)
    for c in range(tm // CHUNK):
        rows = slice(c * CHUNK, (c + 1) * CHUNK)
        for h in range(N_HEAD):
            cols = slice(h * CHUNK, (h + 1) * CHUNK)
            mixed_ref[rows, cols] = _dot(ws_ref[h], vnb_ref[rows, cols]) + bsf_ref[:, cols]
    return vb_pre, tv, xhv, rstdv


def _mix_fwd(p, pedge, x, wpa, wpb, wo, wsb, bsf, bg, cw, vg, vb, tm):
    t = x.shape[0]
    nt = t // tm

    def body(p_ref, prev_ref, next_ref, x_ref, wpa_ref, wpb_ref, wo_ref, ws_ref, bsf_ref, bg_ref, cw_ref, vg_ref, vb_ref,
             r1_ref, ya_ref, yb_ref, vnb_ref, mixed_ref):
        has_prev, has_next = _end_masks(nt)
        _, _, _, _, _, cv = _conv_fwd(p_ref, prev_ref, next_ref, cw_ref, tm, has_prev, has_next)
        a = _pcols(p_ref, 0, OFF_CA) * cv
        ya = _dot(a.astype(BF), wpa_ref[...])
        ya_ref[...] = ya
        _spatial_fwd(p_ref, vg_ref, vb_ref, ws_ref, bsf_ref, vnb_ref, mixed_ref, tm)
        gu, _ = _gelu(_pcols(p_ref, OFF_UB, OFF_VB))
        bb = gu * mixed_ref[...]
        yb = _dot(bb.astype(BF), wpb_ref[...])
        yb_ref[...] = yb
        ga = jax.nn.sigmoid(_pcols(p_ref, OFF_GA, OFF_GB) + bg_ref[:, 0:D])
        gb = jax.nn.sigmoid(_pcols(p_ref, OFF_GB, N_PROJ) + bg_ref[:, D:2 * D])
        z = ga * ya + gb * yb
        r1_ref[...] = ALPHA * x_ref[...] + _dot(z.astype(BF), wo_ref[...])

    tile = lambda w: pl.BlockSpec((tm, w), lambda i: (i, 0))
    return _pc(
        body, name="mix_fwd", grid=(nt,),
        in_specs=[tile(N_PROJ), *_edge_specs(t, tm, N_PROJ), tile(D),
                  _resident((W_A, D)), _resident((W_B, D)), _resident((D, D)), _resident((N_HEAD, CHUNK, CHUNK)),
                  _resident((CHUNK, W_B)), _resident((1, 2 * D)), _resident((3, W_A)), _resident((1, W_B)),
                  _resident((1, W_B))],
        out_specs=[tile(D), tile(D), tile(D)],
        out_shape=[jax.ShapeDtypeStruct((t, D), F32)] * 3,
        scratch_shapes=[pltpu.VMEM((tm, W_B), BF), pltpu.VMEM((tm, W_B), F32)],
        compiler_params=_params("arbitrary"),
    )(p, pedge, pedge, x, wpa, wpb, wo, wsb, bsf, bg, cw, vg, vb)


def _ffn_fwd_bwd(r1, tgt, wff1, wff2, ln1g, ln1b, ln2g, ln2b, tm):
    t = r1.shape[0]

    def body(r1_ref, tgt_ref, w1_ref, w2_ref, g1_ref, b1_ref, g2_ref, b2_ref,
             dr1_ref, dr1b_ref, dedge_ref, x1b_ref, hidb_ref, dh1b_ref, dr2b_ref, acc_ref, relu_ref):
        @pl.when(pl.program_id(0) == 0)
        def _():
            acc_ref[...] = jnp.zeros_like(acc_ref)

        xh1, rstd1 = _ln_stats(r1_ref[...])
        x1 = xh1 * g1_ref[...] + b1_ref[...]
        x1b_ref[...] = x1.astype(BF)
        ffn = jnp.zeros((tm, D), F32)
        for j in range(N_CHIP):
            cols = slice(j * FF_SHARD, (j + 1) * FF_SHARD)
            r = jnp.maximum(_dot(x1b_ref[...], w1_ref[j]), 0.0)
            relu_ref[:, cols] = r
            hidb_ref[:, cols] = (r * r).astype(BF)
            ffn = ffn + _dot(hidb_ref[:, cols], w2_ref[cols, :])
        xh2, rstd2 = _ln_stats(ALPHA * x1 + ffn)
        diff = xh2 * g2_ref[...] + b2_ref[...] - tgt_ref[...]
        acc_ref[4:5, :] += _colsum(diff * diff)
        dx2 = diff * (1.0 / D)
        acc_ref[2:3, :] += _colsum(dx2 * xh2)
        acc_ref[3:4, :] += _colsum(dx2)
        dr2 = _ln_bwd(dx2, g2_ref[...], xh2, rstd2)
        dr2b_ref[...] = dr2.astype(BF)
        dx1 = ALPHA * dr2
        for j in range(N_CHIP):
            cols = slice(j * FF_SHARD, (j + 1) * FF_SHARD)
            dhid = _dot_nt(dr2b_ref[...], w2_ref[cols, :])
            dh1b_ref[:, cols] = (dhid * (2.0 * relu_ref[:, cols])).astype(BF)
            dx1 = dx1 + _dot_nt(dh1b_ref[:, cols], w1_ref[j])
        acc_ref[0:1, :] += _colsum(dx1 * xh1)
        acc_ref[1:2, :] += _colsum(dx1)
        dr1 = _ln_bwd(dx1, g1_ref[...], xh1, rstd1)
        dr1_ref[...] = dr1
        dr1b_ref[...] = dr1.astype(BF)
        _write_edges(dedge_ref, dr1_ref, tm)

    tile = lambda w: pl.BlockSpec((tm, w), lambda i: (i, 0))
    vec = _resident((1, D))
    return _pc(
        body, name="ffn_fwd_bwd", grid=(t // tm,),
        in_specs=[tile(D), tile(D), _resident((N_CHIP, D, FF_SHARD)), _resident((D_FF, D)), vec, vec, vec, vec],
        out_specs=[tile(D), tile(D), pl.BlockSpec((tm // EDGE_TILE, 2 * HALO, D), lambda i: (i, 0, 0)), tile(D), tile(D_FF),
                   tile(D_FF), tile(D), pl.BlockSpec((8, D), lambda i: (0, 0))],
        out_shape=[jax.ShapeDtypeStruct((t, D), F32), jax.ShapeDtypeStruct((t, D), BF),
                   jax.ShapeDtypeStruct((t // EDGE_TILE, 2 * HALO, D), F32), jax.ShapeDtypeStruct((t, D), BF),
                   jax.ShapeDtypeStruct((t, D_FF), BF), jax.ShapeDtypeStruct((t, D_FF), BF),
                   jax.ShapeDtypeStruct((t, D), BF), jax.ShapeDtypeStruct((8, D), F32)],
        scratch_shapes=[pltpu.VMEM((tm, D_FF), F32)],
        compiler_params=_params("arbitrary"),
    )(r1, tgt, wff1, wff2, ln1g, ln1b, ln2g, ln2b)


def _dw(a, b, nblk, am, bn, a_blocked, b_blocked, tk, name, comm=None):
    t = a.shape[0]

    def body(a_ref, b_ref, o_ref):
        @pl.when(pl.program_id(1) == 0)
        def _():
            o_ref[...] = jnp.zeros_like(o_ref)

        o_ref[...] += _dot_tn(a_ref[...].astype(BF), b_ref[...])

    outs, got = _host_call(
        body, comm, name=name, grid=(nblk, t // tk),
        in_specs=[pl.BlockSpec((tk, am), (lambda j, k: (k, j)) if a_blocked else (lambda j, k: (k, 0))),
                  pl.BlockSpec((tk, bn), (lambda j, k: (k, j)) if b_blocked else (lambda j, k: (k, 0)))],
        out_specs=[pl.BlockSpec((None, am, bn), lambda j, k: (j, 0, 0))],
        out_shape=[jax.ShapeDtypeStruct((nblk, am, bn), F32)], args=(a, b))
    return outs[0] if comm is None else (outs[0], got)


def _dw_proj(ab, dyab, bbb, dybb, zb, dr1b, tk, comm):
    t = ab.shape[0]
    pairs = (("w_pa", 0, 1), ("w_pb", 2, 3), ("w_o", 4, 5))

    def body(*refs):
        o_ref = refs[6]

        @pl.when(pl.program_id(0) == 0)
        def _():
            o_ref[...] = jnp.zeros_like(o_ref)

        for name, ia, ib in pairs:
            off, rows = PROJ_OFF[name]
            for k in range(N_CHIP):
                o_ref[k, off:off + rows, :] += _dot_tn(refs[ia][:, k * rows:(k + 1) * rows], refs[ib][...])

    tile = lambda w: pl.BlockSpec((tk, w), lambda i: (i, 0))
    outs, got = _host_call(
        body, comm, name="dw_proj", grid=(t // tk,), in_specs=[tile(W_A), tile(D), tile(W_B), tile(D), tile(D), tile(D)],
        out_specs=[pl.BlockSpec((N_CHIP, PROJ_TOTAL, D), lambda i: (0, 0, 0))],
        out_shape=[jax.ShapeDtypeStruct((N_CHIP, PROJ_TOTAL, D), F32)], args=(ab, dyab, bbb, dybb, zb, dr1b))
    return outs[0], got


def _dx(dp, win4, dr1, tm, blk0, nblk, filled, name, comm):
    t = dp.shape[0]

    def body(dp_ref, w_ref, dr1_ref, *rest):
        dx = ALPHA * dr1_ref[...]
        for j in range(N_CHIP):
            dx = dx + _dot_nt(dp_ref[:, j * NP_SHARD:(j + 1) * NP_SHARD], w_ref[j])
        rest[-1][...] = dx

    in_specs = [pl.BlockSpec((tm, N_PROJ), lambda i: (i + blk0, 0)), _resident((N_CHIP, D, NP_SHARD)),
                pl.BlockSpec((tm, D), lambda i: (i + blk0, 0))]
    args = (dp, win4, dr1)
    aliases = None
    if filled is not None:
        in_specs.append(pl.BlockSpec(memory_space=pl.ANY))
        args += (filled,)
        aliases = {3: 0}
    outs, got = _host_call(
        body, comm, name=name, grid=(nblk,), in_specs=in_specs, out_specs=[pl.BlockSpec((tm, D), lambda i: (i + blk0, 0))],
        out_shape=[jax.ShapeDtypeStruct((t, D), F32)], args=args, aliases=aliases)
    return outs[0], got


def _mix_bwd(p, pedge, dr1, dedge, ya, yb, wpa, wpb, wo, wsb, wstb, bsf, bg, cw, vg, vb, tm, comm):
    t = p.shape[0]
    nt = t // tm
    te = tm + 2 * HALO
    mid = slice(HALO, HALO + tm)

    def body(p_ref, prev_ref, next_ref, dr1_ref, dprev_ref, dnext_ref, ya_ref, yb_ref, wpa_ref, wpb_ref, wo_ref,
             ws_ref, wst_ref, bsf_ref, bg_ref, cw_ref, vg_ref, vb_ref,
             dp_ref, ab_ref, bbb_ref, zb_ref, dyab_ref, dybb_ref, dbg_ref, dcw_ref, dvgb_ref, dws_ref, dbs_ref,
             vnb_ref, mixed_ref, dmixb_ref, dvn_ref):
        @pl.when(pl.program_id(0) == 0)
        def _():
            for r in (dbg_ref, dcw_ref, dvgb_ref, dws_ref, dbs_ref):
                r[...] = jnp.zeros_like(r)

        has_prev, has_next = _end_masks(nt)
        ca, ha, ch, ch_m1, ch_p1, cv = _conv_fwd(p_ref, prev_ref, next_ref, cw_ref, tm, has_prev, has_next)
        ba = _pcols(p_ref, 0, OFF_CA)
        ab_ref[...] = (ba * cv).astype(BF)
        vb_pre, tv, xhv, rstdv = _spatial_fwd(p_ref, vg_ref, vb_ref, ws_ref, bsf_ref, vnb_ref, mixed_ref, tm)
        ub = _pcols(p_ref, OFF_UB, OFF_VB)
        gu, tu = _gelu(ub)
        bbb_ref[...] = (gu * mixed_ref[...]).astype(BF)
        bga = bg_ref[:, 0:D]
        ga = jax.nn.sigmoid(_pcols(p_ref, OFF_GA, OFF_GB) + bga)
        gb = jax.nn.sigmoid(_pcols(p_ref, OFF_GB, N_PROJ) + bg_ref[:, D:2 * D])
        ya = ya_ref[...]
        yb = yb_ref[...]
        zb_ref[...] = (ga * ya + gb * yb).astype(BF)

        dr1_ext = jnp.concatenate([dprev_ref[...] * has_prev, dr1_ref[...], dnext_ref[...] * has_next], axis=0)
        dz_ext = _dot_nt(dr1_ext.astype(BF), wo_ref[...])
        ga_ext = jnp.concatenate([jax.nn.sigmoid(prev_ref[:, OFF_GA:OFF_GB] + bga), ga,
                                  jax.nn.sigmoid(next_ref[:, OFF_GA:OFF_GB] + bga)], axis=0)
        dya_ext = dz_ext * ga_ext
        dyab_ref[...] = dya_ext[mid].astype(BF)
        da_ext = _dot_nt(dya_ext.astype(BF), wpa_ref[...])
        ba_ext = jnp.concatenate([prev_ref[:, 0:OFF_CA], ba, next_ref[:, 0:OFF_CA]], axis=0)
        dcv_ext = da_ext * ba_ext
        dcv = dcv_ext[mid]
        dch = (cw_ref[0:1, :] * pltpu.roll(dcv_ext, te - 1, 0)[mid] + cw_ref[1:2, :] * dcv
               + cw_ref[2:3, :] * pltpu.roll(dcv_ext, 1, 0)[mid])
        dp_ref[:, 0:OFF_CA] = (da_ext[mid] * cv).astype(BF)
        dp_ref[:, OFF_CA:OFF_HA] = (dch * ha).astype(BF)
        dp_ref[:, OFF_HA:OFF_UB] = (dch * ca).astype(BF)
        dcw_ref[0:1, :] += _colsum(dcv * ch_m1)
        dcw_ref[1:2, :] += _colsum(dcv * ch)
        dcw_ref[2:3, :] += _colsum(dcv * ch_p1)

        dz = dz_ext[mid]
        dga = dz * ya * ga * (1.0 - ga)
        dgb = dz * yb * gb * (1.0 - gb)
        dp_ref[:, OFF_GA:OFF_GB] = dga.astype(BF)
        dp_ref[:, OFF_GB:N_PROJ] = dgb.astype(BF)
        dbg_ref[0:1, 0:D] += _colsum(dga)
        dbg_ref[0:1, D:2 * D] += _colsum(dgb)

        dybb_ref[...] = (dz * gb).astype(BF)
        dbb = _dot_nt(dybb_ref[...], wpb_ref[...])
        dp_ref[:, OFF_UB:OFF_VB] = (dbb * mixed_ref[...] * _gelu_grad(ub, tu)).astype(BF)
        dmixed = dbb * gu
        dmixb_ref[...] = dmixed.astype(BF)
        for c in range(tm // CHUNK):
            rows = slice(c * CHUNK, (c + 1) * CHUNK)
            dbs_ref[...] += dmixed[rows]
            for h in range(N_HEAD):
                cols = slice(h * CHUNK, (h + 1) * CHUNK)
                dws_ref[h] += _dot_nt(dmixb_ref[rows, cols], vnb_ref[rows, cols])
                dvn_ref[rows, cols] = _dot(wst_ref[h], dmixb_ref[rows, cols])
        dvn = dvn_ref[...]
        dvgb_ref[0:1, :] += _colsum(dvn * xhv)
        dvgb_ref[1:2, :] += _colsum(dvn)
        dgv = _ln_bwd(dvn, vg_ref[...], xhv, rstdv)
        dp_ref[:, OFF_VB:OFF_GA] = (dgv * _gelu_grad(vb_pre, tv)).astype(BF)

    tile = lambda w: pl.BlockSpec((tm, w), lambda i: (i, 0))
    acc = lambda *s: pl.BlockSpec(s, lambda i: (0,) * len(s))
    return _host_call(
        body, comm, name="mix_bwd", grid=(nt,),
        in_specs=[tile(N_PROJ), *_edge_specs(t, tm, N_PROJ), tile(D), *_edge_specs(t, tm, D), tile(D), tile(D),
                  _resident((W_A, D)), _resident((W_B, D)), _resident((D, D)), _resident((N_HEAD, CHUNK, CHUNK)),
                  _resident((N_HEAD, CHUNK, CHUNK)), _resident((CHUNK, W_B)), _resident((1, 2 * D)),
                  _resident((3, W_A)), _resident((1, W_B)), _resident((1, W_B))],
        out_specs=[tile(N_PROJ), tile(W_A), tile(W_B), tile(D), tile(D), tile(D),
                   acc(8, 2 * D), acc(8, W_A), acc(8, W_B), acc(N_HEAD, CHUNK, CHUNK), acc(CHUNK, W_B)],
        out_shape=[jax.ShapeDtypeStruct((t, N_PROJ), BF), jax.ShapeDtypeStruct((t, W_A), BF),
                   jax.ShapeDtypeStruct((t, W_B), BF), jax.ShapeDtypeStruct((t, D), BF), jax.ShapeDtypeStruct((t, D), BF),
                   jax.ShapeDtypeStruct((t, D), BF), jax.ShapeDtypeStruct((8, 2 * D), F32),
                   jax.ShapeDtypeStruct((8, W_A), F32), jax.ShapeDtypeStruct((8, W_B), F32),
                   jax.ShapeDtypeStruct((N_HEAD, CHUNK, CHUNK), F32), jax.ShapeDtypeStruct((CHUNK, W_B), F32)],
        scratch_shapes=[pltpu.VMEM((tm, W_B), BF), pltpu.VMEM((tm, W_B), F32), pltpu.VMEM((tm, W_B), BF),
                        pltpu.VMEM((tm, W_B), F32)],
        args=(p, pedge, pedge, dr1, dedge, dedge, ya, yb, wpa, wpb, wo, wsb, wstb, bsf, bg, cw, vg, vb))


def _add_own_half(full4, recv4, place, rb, name):
    n, rh, cols = recv4.shape
    nb = rh // rb

    def body(pl_ref, a_ref, b_ref, own_ref, ob_ref):
        s = a_ref[...] + b_ref[...]
        ob_ref[...] = s.astype(BF)

        @pl.when(pl.program_id(1) == pl_ref[0])
        def _():
            own_ref[...] = s

    blk = (None, rb, cols)
    return _pc(
        body, name=name,
        grid_spec=pltpu.PrefetchScalarGridSpec(
            num_scalar_prefetch=1, grid=(nb, n),
            in_specs=[pl.BlockSpec(blk, lambda i, k, s: (k, s[1] * nb + i, 0)), pl.BlockSpec(blk, lambda i, k, s: (k, i, 0))],
            out_specs=[pl.BlockSpec((rb, cols), lambda i, k, s: (i, 0)), pl.BlockSpec(blk, lambda i, k, s: (k, i, 0))]),
        out_shape=[jax.ShapeDtypeStruct((rh, cols), F32), jax.ShapeDtypeStruct(recv4.shape, BF)],
        compiler_params=_params("arbitrary", "arbitrary"),
    )(place, full4, recv4)


def _add_chips(own, r3, place, rb, name):
    _, rh, cols = r3.shape
    nb = rh // rb

    def body(pl_ref, s_ref, r_ref, o_ref):
        o_ref[...] = ((s_ref[...] + r_ref[0].astype(F32)) + r_ref[1].astype(F32)) + r_ref[2].astype(F32)

    return _pc(
        body, name=name,
        grid_spec=pltpu.PrefetchScalarGridSpec(
            num_scalar_prefetch=1, grid=(nb,),
            in_specs=[pl.BlockSpec((rb, cols), lambda i, s: (i, 0)), pl.BlockSpec((3, rb, cols), lambda i, s: (0, i, 0))],
            out_specs=pl.BlockSpec((rb, cols), lambda i, s: (s[1] * nb + i, 0))),
        out_shape=jax.ShapeDtypeStruct((2 * rh, cols), F32),
        compiler_params=_params("arbitrary"),
    )(place, own, r3)


def _add_small(a, b):
    def body(a_ref, b_ref, o_ref):
        o_ref[...] = a_ref[...] + b_ref[...]

    return _pc(body, name="add_small_cores", out_shape=jax.ShapeDtypeStruct(a.shape, F32))(a, b)


def _sum_small_chips(own, slots, place):
    def body(pl_ref, own_ref, s_ref, o_ref):
        j = pl_ref[0]

        def term(k):
            return jnp.where(j == k, own_ref[...], s_ref[jnp.maximum((j ^ k) - 1, 0)])

        o_ref[...] = ((term(0) + term(1)) + term(2)) + term(3)

    vmem = pl.BlockSpec(memory_space=pltpu.VMEM)
    return _pc(body, name="sum_small_chips", in_specs=[pl.BlockSpec(memory_space=pltpu.SMEM), vmem, vmem], out_specs=vmem,
               out_shape=jax.ShapeDtypeStruct(own.shape, F32))(place, own, slots)


def _adamw_step(w, g, m, v):
    m2 = ADAM_B1 * m + (1.0 - ADAM_B1) * g
    v2 = ADAM_B2 * v + (1.0 - ADAM_B2) * (g * g)
    m_hat = m2 / (1.0 - ADAM_B1 ** ADAM_STEP)
    v_hat = v2 / (1.0 - ADAM_B2 ** ADAM_STEP)
    return -ADAM_LR * (m_hat / (jnp.sqrt(v_hat) + ADAM_EPS) + ADAM_WD * w), m2, v2


def _adamw(w, g, m, v, rb, name):
    rows, cols = w.shape

    def body(w_ref, g_ref, m_ref, v_ref, d_ref, m2_ref, v2_ref):
        d_ref[...], m2_ref[...], v2_ref[...] = _adamw_step(w_ref[...], g_ref[...], m_ref[...], v_ref[...])

    blk = pl.BlockSpec((rb, cols), lambda i: (i, 0))
    return _pc(body, name=name, grid=(rows // rb,), in_specs=[blk] * 4, out_specs=[blk] * 3,
               out_shape=[jax.ShapeDtypeStruct((rows, cols), F32)] * 3, compiler_params=_params("arbitrary"))(w, g, m, v)


def _adamw_small(ws, gs, ms, vs):
    n = len(ws)

    def body(*refs):
        ins, outs = refs[:4 * n], refs[4 * n:]
        for i in range(n):
            outs[i][...], outs[n + i][...], outs[2 * n + i][...] = _adamw_step(*(ins[k * n + i][...] for k in range(4)))

    outs = _pc(body, name="adamw_small", out_shape=[jax.ShapeDtypeStruct(w.shape, F32) for w in ws] * 3)(*ws, *gs, *ms, *vs)
    return outs[:n], outs[n:2 * n], outs[2 * n:]


LANES = 128
SMALL_GRADS = (("b_gate", 2 * D), ("conv_w", 3 * W_A), ("v_norm_g", W_B), ("v_norm_b", W_B),
               ("w_s", N_HEAD * CHUNK * CHUNK), ("b_s", N_HEAD * CHUNK), ("ln1_g", D), ("ln1_b", D), ("ln2_g", D), ("ln2_b", D),
               ("loss", 1))


def _pack_rows(parts):
    rows = []
    for a in parts:
        a = a.reshape(-1)
        a = jnp.pad(a, (0, (-a.shape[0]) % LANES))
        rows.append(a.reshape(-1, LANES))
    out = jnp.concatenate(rows, axis=0)
    return jnp.pad(out, ((0, (-out.shape[0]) % 8), (0, 0)))


def _unpack_rows(buf, sizes):
    out, r = [], 0
    for n in sizes:
        nr = -(-n // LANES)
        out.append(buf[r:r + nr].reshape(-1)[:n])
        r += nr
    return out


TM_PROJ = 1024
TM_MIX = 256
TM_DX = 512
DX_PAIR = 6
TK_DW = 4096
TK_DW_IN = 2048
TK_DW_PROJ = 1024
ADD_BLOCK_BYTES = 3 * 1024 * 1024
RB_ADAM = 128
CONV_ROWS = 8


def _add_rows(rows, cols):
    while rows * cols * 4 > ADD_BLOCK_BYTES and rows % 32 == 0:
        rows //= 2
    return rows


def _reduce_adds_1(grads, recvs, place, tag):
    out = [_add_own_half(g, r, place, _add_rows(*r.shape[1:]), f"add_cores_{tag}{a}") for a, (g, r) in enumerate(zip(grads, recvs))]
    return [o[0] for o in out], [o[1] for o in out]


def _reduce_adds_2(sums, recvs, place, tag):
    return [_add_chips(s, r, place, _add_rows(*r.shape[1:]), f"add_chips_{tag}{a}") for a, (s, r) in enumerate(zip(sums, recvs))]


def kernel(x, w_in, b_gate, conv_w, v_norm_g, v_norm_b, w_s, b_s, w_pa, w_pb, w_o, ln1_g, ln1_b, w_ff1, w_ff2, ln2_g, ln2_b, loss_target, m_w_in, m_b_gate, m_conv_w, m_v_norm_g, m_v_norm_b, m_w_s, m_b_s, m_w_pa, m_w_pb, m_w_o, m_ln1_g, m_ln1_b, m_w_ff1, m_w_ff2, m_ln2_g, m_ln2_b, v_w_in, v_b_gate, v_conv_w, v_v_norm_g, v_v_norm_b, v_w_s, v_b_s, v_w_pa, v_w_pb, v_w_o, v_ln1_g, v_ln1_b, v_w_ff1, v_w_ff2, v_ln2_g, v_ln2_b):
    t = x.shape[1]
    core = lax.axis_index("c").astype(jnp.int32).reshape(1)
    chip_idx = 2 * lax.axis_index("x") + lax.axis_index("y")
    chip = chip_idx.astype(jnp.int32).reshape(1)
    place = jnp.concatenate([chip, core])
    x2 = x.reshape(t, D)
    tgt = loss_target.reshape(t, D)

    win4, proj4, ff14, ff24 = _cast_shards(w_in[0], w_pa[0], w_pb[0], w_o[0], w_ff1[0], w_ff2[0], chip)
    conv4 = lax.dynamic_update_slice(jnp.zeros((N_CHIP, CONV_ROWS, W_A // N_CHIP), F32),
                                     jnp.pad(conv_w[0], ((0, CONV_ROWS - 3), (0, 0)))[None], (chip_idx, 0, 0))
    (p, pedge), (win4, proj4, ff14, ff24, conv4) = _proj_fwd(
        x2, chip, TM_PROJ, _gather_comm([win4, proj4, ff14, ff24], conv4, eager=1))

    def full(name, rows_total):
        off, rows = PROJ_OFF[name]
        return proj4[:, off:off + rows, :].reshape(rows_total, D)

    wpa, wpb, wo = full("w_pa", W_A), full("w_pb", W_B), full("w_o", D)
    wff2 = ff24.reshape(D_FF, D)
    cw = jnp.transpose(conv4[:, :3, :], (1, 0, 2)).reshape(3, W_A)
    wsb = w_s[0].astype(BF)
    wstb = jnp.swapaxes(w_s[0], 1, 2).astype(BF)
    bsf = jnp.repeat(jnp.transpose(b_s[0]), CHUNK, axis=1)

    r1, ya, yb = _mix_fwd(p, pedge, x2, wpa, wpb, wo, wsb, bsf, b_gate, cw, v_norm_g, v_norm_b, TM_MIX)
    dr1, dr1b, dedge, x1b, hidb, dh1b, dr2b, acc = _ffn_fwd_bwd(r1, tgt, ff14, wff2, ln1_g, ln1_b, ln2_g, ln2_b, TM_MIX)
    g_ff = [_dw(x1b, dh1b, N_CHIP, D, FF_SHARD, False, True, TK_DW, "dw_ff1"),
            _dw(hidb, dr2b, N_CHIP, FF_SHARD, D, True, False, TK_DW, "dw_ff2")]
    (dp, ab, bbb, zb, dyab, dybb, dbg, dcw, dvgb, dws, dbs_sum), r_ff = _mix_bwd(
        p, pedge, dr1, dedge, ya, yb, wpa, wpb, wo, wsb, wstb, bsf, b_gate, cw, v_norm_g, v_norm_b, TM_MIX, _pair_comm(g_ff))
    s_ff, sb_ff = _reduce_adds_1(g_ff, r_ff, place, "ff")
    dwin4, c_ff = _dw(x2, dp, N_CHIP, D, NP_SHARD, False, True, TK_DW_IN, "dw_in", _chips_comm(sb_ff))
    f_ff = _reduce_adds_2(s_ff, c_ff, place, "ff")
    dproj4, (g_ff1, g_ff2) = _dw_proj(ab, dyab, bbb, dybb, zb, dr1b, TK_DW_PROJ, _join_comm(f_ff))
    dbs = jnp.transpose(jnp.sum(dbs_sum.reshape(CHUNK, N_HEAD, CHUNK), axis=-1))
    small = _pack_rows([dbg[0], dcw[0:3], dvgb[0], dvgb[1], dws, dbs, acc[0], acc[1], acc[2], acc[3],
                        0.5 * jnp.sum(acc[4]) / D])
    g_rest = [dwin4, dproj4]
    nblk = t // TM_DX
    n_a = max(1, min(DX_PAIR, nblk // 4))
    dx, r_rest = _dx(dp, win4, dr1, TM_DX, 0, n_a, None, "dx_a", _pair_comm(g_rest, small))
    s_rest, sb_rest = _reduce_adds_1(g_rest, r_rest[:2], place, "rest")
    csmall = _add_small(small, r_rest[2])
    dx, c_rest = _dx(dp, win4, dr1, TM_DX, n_a, nblk - n_a, dx, "dx_b", _chips_comm(sb_rest, csmall))
    f_rest = _reduce_adds_2(s_rest, c_rest[:2], place, "rest")
    gsmall = _sum_small_chips(csmall, c_rest[2], place)
    g_in, g_proj = _comm_call(_join_comm(f_rest), "join_rest")

    grads = {"w_in": g_in, "w_ff1": g_ff1, "w_ff2": g_ff2}
    for name, _ in PROJ_ROWS:
        off, rows = PROJ_OFF[name]
        grads[name] = g_proj[off:off + rows, :]
    for (name, n), flat in zip(SMALL_GRADS, _unpack_rows(gsmall, [n for _, n in SMALL_GRADS])):
        grads[name] = flat
    loss = grads.pop("loss").reshape(())
    grads["conv_w"] = lax.dynamic_slice(grads["conv_w"].reshape(3, W_A), (0, chip_idx * (W_A // N_CHIP)), (3, W_A // N_CHIP))

    weights = dict(w_in=w_in, b_gate=b_gate, conv_w=conv_w, v_norm_g=v_norm_g, v_norm_b=v_norm_b, w_s=w_s, b_s=b_s,
                   w_pa=w_pa, w_pb=w_pb, w_o=w_o, ln1_g=ln1_g, ln1_b=ln1_b, w_ff1=w_ff1, w_ff2=w_ff2, ln2_g=ln2_g, ln2_b=ln2_b)
    mom1 = dict(w_in=m_w_in, b_gate=m_b_gate, conv_w=m_conv_w, v_norm_g=m_v_norm_g, v_norm_b=m_v_norm_b, w_s=m_w_s,
                b_s=m_b_s, w_pa=m_w_pa, w_pb=m_w_pb, w_o=m_w_o, ln1_g=m_ln1_g, ln1_b=m_ln1_b, w_ff1=m_w_ff1,
                w_ff2=m_w_ff2, ln2_g=m_ln2_g, ln2_b=m_ln2_b)
    mom2 = dict(w_in=v_w_in, b_gate=v_b_gate, conv_w=v_conv_w, v_norm_g=v_v_norm_g, v_norm_b=v_v_norm_b, w_s=v_w_s,
                b_s=v_b_s, w_pa=v_w_pa, w_pb=v_w_pb, w_o=v_w_o, ln1_g=v_ln1_g, ln1_b=v_ln1_b, w_ff1=v_w_ff1,
                w_ff2=v_w_ff2, ln2_g=v_ln2_g, ln2_b=v_ln2_b)
    order = list(weights)
    big = ("w_in", "w_pa", "w_pb", "w_o", "w_ff1", "w_ff2")
    delta, new_m, new_v = {}, {}, {}
    for name in big:
        w2 = weights[name][0]
        delta[name], new_m[name], new_v[name] = _adamw(w2, grads[name], mom1[name][0], mom2[name][0], RB_ADAM, "adamw_" + name)
    little = [n for n in order if n not in big]
    flat2d = lambda a: a.reshape(-1, a.shape[-1])
    ds, ms, vs = _adamw_small(*([flat2d(d[n].reshape(weights[n].shape)) for n in little] for d in (weights, grads, mom1, mom2)))
    for name, d_, m_, v_ in zip(little, ds, ms, vs):
        delta[name], new_m[name], new_v[name] = d_, m_, v_

    shaped = lambda d: [d[n].reshape(weights[n].shape) for n in order]
    return (loss, dx.reshape(x.shape), *shaped(grads), *shaped(delta), *shaped(new_m), *shaped(new_v))
```

```python
import functools
from typing import NamedTuple

import jax
import jax.numpy as jnp
from jax import lax
from jax.experimental import pallas as pl
from jax.experimental.pallas import tpu as pltpu

D = 1024
W_A = 1536
W_B = 1024
CHUNK = 128
N_HEAD = 8
D_FF = 4096
N_PROJ = 3 * W_A + 2 * W_B + 2 * D
OFF_CA, OFF_HA, OFF_UB, OFF_VB, OFF_GA, OFF_GB = 1536, 3072, 4608, 5632, 6656, 7680
LN_EPS = 1e-5
ALPHA = 2.0 ** 0.25
N_CHIP = 4
NP_SHARD = N_PROJ // N_CHIP
FF_SHARD = D_FF // N_CHIP
ADAM_LR, ADAM_B1, ADAM_B2, ADAM_EPS, ADAM_WD, ADAM_STEP = 0.001, 0.9, 0.999, 1e-08, 0.01, 10

PROJ_ROWS = (("w_pa", W_A // N_CHIP), ("w_pb", W_B // N_CHIP), ("w_o", D // N_CHIP))
PROJ_OFF = {}
_o = 0
for _n, _r in PROJ_ROWS:
    PROJ_OFF[_n] = (_o, _r)
    _o += _r
PROJ_TOTAL = _o

V7X_VMEM_BYTES = 64 * 1024 * 1024
VMEM_LIMIT = V7X_VMEM_BYTES - 8 * 1024 * 1024
HALO = 8
P_BUFFERS = 3
EDGE_TILE = 256

BF = jnp.bfloat16
F32 = jnp.float32
MESH = pl.DeviceIdType.MESH
HBM_SPEC = pl.BlockSpec(memory_space=pltpu.HBM)


def _pc(body, **kw):
    return pl.pallas_call(body, **kw)


def _params(*sem):
    return pltpu.CompilerParams(dimension_semantics=sem, vmem_limit_bytes=VMEM_LIMIT)


def _resident(shape):
    n = len(shape)
    return pl.BlockSpec(shape, lambda *_: (0,) * n, pipeline_mode=pl.Buffered(1))


def _dot(a, b):
    return jnp.dot(a, b, preferred_element_type=F32)


def _dot_nt(a, b):
    return lax.dot_general(a, b, (((1,), (1,)), ((), ())), preferred_element_type=F32)


def _dot_tn(a, b):
    return lax.dot_general(a, b, (((0,), (0,)), ((), ())), preferred_element_type=F32)


def _gelu(x):
    t = jnp.tanh(0.7978845608028654 * (x + 0.044715 * (x * x * x)))
    return 0.5 * x * (1.0 + t), t


def _gelu_grad(x, t):
    return 0.5 * (1.0 + t) + 0.5 * x * (1.0 - t * t) * (0.7978845608028654 * (1.0 + 0.134145 * (x * x)))


def _ln_stats(r):
    mu = jnp.mean(r, axis=-1, keepdims=True)
    xc = r - mu
    var = jnp.mean(xc * xc, axis=-1, keepdims=True)
    rstd = lax.rsqrt(var + LN_EPS)
    return xc * rstd, rstd


def _ln_bwd(dy, g, xh, rstd):
    dxh = dy * g
    m1 = jnp.mean(dxh, axis=-1, keepdims=True)
    m2 = jnp.mean(dxh * xh, axis=-1, keepdims=True)
    return rstd * (dxh - m1 - xh * m2)


def _colsum(v):
    return jnp.sum(v, axis=0, keepdims=True)


class _Comm(NamedTuple):
    args: tuple
    out_shape: tuple
    aliases: dict
    n_sems: int
    stages: tuple


def _place():
    x, y, c = lax.axis_index("x"), lax.axis_index("y"), lax.axis_index("c")
    return x, y, c, 2 * x + y


def _flip(x, y, c, r):
    return (x ^ (r >> 1), y ^ (r & 1), c)


def _remote(src, dst, send_sems, recv_sems, k, peer):
    return pltpu.make_async_remote_copy(src_ref=src, dst_ref=dst, send_sem=send_sems.at[k], recv_sem=recv_sems.at[k],
                                        device_id=peer, device_id_type=MESH)


def _host_call(body, comm, *, name, grid, in_specs, out_specs, out_shape, args, scratch_shapes=(), aliases=None, prefetch=None,
               body_reads_comm=False):
    sem = ("arbitrary",) * len(grid)
    aliases = dict(aliases or {})
    n_pre = 0 if prefetch is None else 1
    n_in, n_out, n_scr = len(in_specs), len(out_specs), len(scratch_shapes)
    c_in, c_out = (0, 0) if comm is None else (len(comm.args), len(comm.out_shape))
    steps = {"first": (0,) * len(grid), "late": (grid[0] - 1,) + (0,) * (len(grid) - 1), "last": tuple(g - 1 for g in grid)}

    def wrapped(*refs):
        refs = refs[n_pre:]
        own_in, cin = refs[:n_in], refs[n_in:n_in + c_in]
        o0 = n_in + c_in
        own_out, cout = refs[o0:o0 + n_out], refs[o0 + n_out:o0 + n_out + c_out]
        s0 = o0 + n_out + c_out
        scr, sems = refs[s0:s0 + n_scr], refs[s0 + n_scr:]

        def run(before):
            for phase, fn in () if comm is None else comm.stages:
                at_step = isinstance(phase, tuple)
                if before != (at_step or phase == "first"):
                    continue
                step = phase if at_step else steps[phase]
                cond = pl.program_id(0) == step[0]
                for d in range(1, len(grid)):
                    cond = jnp.logical_and(cond, pl.program_id(d) == step[d])
                pl.when(cond)(functools.partial(fn, cin, cout, *sems))

        run(True)
        if body_reads_comm:
            body(*own_in, *own_out, *scr, comm_refs=cout)
        else:
            body(*own_in, *own_out, *scr)
        run(False)

    in_specs = list(in_specs) + [HBM_SPEC] * c_in
    out_specs = list(out_specs) + [HBM_SPEC] * c_out
    out_shape = list(out_shape) + ([] if comm is None else list(comm.out_shape))
    scratch_shapes = list(scratch_shapes) + ([] if comm is None else [pltpu.SemaphoreType.DMA((comm.n_sems,))] * 2)
    args = tuple(args) + (() if comm is None else tuple(comm.args))
    if comm is not None:
        aliases.update({n_in + i: n_out + o for i, o in comm.aliases.items()})
    aliases = {i + n_pre: o for i, o in aliases.items()}
    if prefetch is None:
        kw = dict(grid=grid, in_specs=in_specs, out_specs=out_specs, scratch_shapes=scratch_shapes)
    else:
        kw = dict(grid_spec=pltpu.PrefetchScalarGridSpec(num_scalar_prefetch=1, grid=grid, in_specs=in_specs,
                                                         out_specs=out_specs, scratch_shapes=scratch_shapes))
        args = (prefetch,) + args
    outs = _pc(wrapped, name=name, out_shape=out_shape, input_output_aliases=aliases, compiler_params=_params(*sem), **kw)(*args)
    return outs[:n_out], outs[n_out:]


def _comm_call(comm, name):
    def body(*refs):
        c_in, c_out = len(comm.args), len(comm.out_shape)
        cin, cout, (send_sems, recv_sems) = refs[:c_in], refs[c_in:c_in + c_out], refs[c_in + c_out:]
        for phase in ("first", "late", "last"):
            for ph, fn in comm.stages:
                if ph == phase:
                    fn(cin, cout, send_sems, recv_sems)

    return _pc(body, name=name, in_specs=[HBM_SPEC] * len(comm.args), out_specs=[HBM_SPEC] * len(comm.out_shape),
               out_shape=list(comm.out_shape), scratch_shapes=[pltpu.SemaphoreType.DMA((comm.n_sems,))] * 2,
               input_output_aliases=dict(comm.aliases))(*comm.args)


def _gather_comm(bufs, whole=None, eager=0):
    n = len(bufs)
    halves = [b.shape[1] // 2 for b in bufs]
    k_ici = lambda a, r: 3 * a + r - 1
    k_d2d = lambda a, r: 3 * n + 3 * a + r - 1
    k_whole = lambda r: 6 * n + r - 1

    def half(ref, slot, c, a):
        return ref.at[slot, pl.ds(c * halves[a], halves[a])]

    def send(cin, cout, ss, rs):
        x, y, c, j = _place()
        for a in range(n):
            mine = half(cout[a], j, c, a)
            for r in (1, 2, 3):
                _remote(mine, mine, ss, rs, k_ici(a, r), _flip(x, y, c, r)).start()
        if whole is not None:
            for r in (1, 2, 3):
                _remote(cout[n].at[j], cout[n].at[j], ss, rs, k_whole(r), _flip(x, y, c, r)).start()

    def pass_on(cout, ss, rs, a, r):
        x, y, c, j = _place()
        landed = half(cout[a], j ^ r, c, a)
        _remote(landed, landed, ss, rs, k_ici(a, r), (x, y, 1 - c)).wait_recv()
        _remote(landed, landed, ss, rs, k_d2d(a, r), (x, y, 1 - c)).start()

    def passed_on(cout, ss, rs, a, r):
        x, y, c, j = _place()
        theirs = half(cout[a], j ^ r, 1 - c, a)
        _remote(theirs, theirs, ss, rs, k_d2d(a, r), (x, y, 1 - c)).wait_recv()

    def arrive(r, cin, cout, ss, rs):
        for a in range(eager):
            pass_on(cout, ss, rs, a, r)
        for a in range(eager):
            passed_on(cout, ss, rs, a, r)

    def forward(cin, cout, ss, rs):
        for a in range(eager, n):
            for r in (1, 2, 3):
                pass_on(cout, ss, rs, a, r)

    def finish(cin, cout, ss, rs):
        x, y, c, j = _place()
        sibling = (x, y, 1 - c)
        for a in range(eager, n):
            for r in (1, 2, 3):
                passed_on(cout, ss, rs, a, r)
        for a in range(n):
            mine = half(cout[a], j, c, a)
            for r in (1, 2, 3):
                _remote(mine, mine, ss, rs, k_ici(a, r), sibling).wait_send()
                landed = half(cout[a], j ^ r, c, a)
                _remote(landed, landed, ss, rs, k_d2d(a, r), sibling).wait_send()
        if whole is not None:
            for r in (1, 2, 3):
                cp = _remote(cout[n].at[j ^ r], cout[n].at[j ^ r], ss, rs, k_whole(r), sibling)
                cp.wait_recv()
                cp.wait_send()

    args = tuple(bufs) + ((whole,) if whole is not None else ())
    out_shape = tuple(jax.ShapeDtypeStruct(b.shape, b.dtype) for b in args)
    aliases = {a: a for a in range(len(args))}
    arrivals = tuple(((r, 0), functools.partial(arrive, r)) for r in (1, 2, 3)) if eager else ()
    return _Comm(args, out_shape, aliases, 6 * n + 3, (("first", send),) + arrivals + (("late", forward), ("last", finish)))


def _pair_comm(grads, small=None):
    n = len(grads)
    halves = [g.shape[1] // 2 for g in grads]

    def copies(cin, cout, ss, rs):
        x, y, c, _ = _place()
        sibling = (x, y, 1 - c)
        cps = [_remote(cin[a].at[:, pl.ds((1 - c) * halves[a], halves[a]), :], cout[a], ss, rs, a, sibling) for a in range(n)]
        if small is not None:
            cps.append(_remote(cin[n], cout[n], ss, rs, n, sibling))
        return cps

    def start(cin, cout, ss, rs):
        for cp in copies(cin, cout, ss, rs):
            cp.start()

    def finish(cin, cout, ss, rs):
        for cp in copies(cin, cout, ss, rs):
            cp.wait()

    args = tuple(grads) + ((small,) if small is not None else ())
    out_shape = tuple(jax.ShapeDtypeStruct((N_CHIP, h, g.shape[2]), F32) for g, h in zip(grads, halves))
    out_shape += (jax.ShapeDtypeStruct(small.shape, F32),) if small is not None else ()
    return _Comm(args, out_shape, {}, n + 1, (("first", start), ("last", finish)))


def _chips_comm(sums_bf, small=None):
    n = len(sums_bf)

    def copies(cin, cout, ss, rs):
        x, y, c, j = _place()
        cps = []
        for r in (1, 2, 3):
            peer = _flip(x, y, c, r)
            for a in range(n):
                cps.append(_remote(cin[a].at[j ^ r], cout[a].at[r - 1], ss, rs, (n + 1) * (r - 1) + a, peer))
            if small is not None:
                cps.append(_remote(cin[n], cout[n].at[r - 1], ss, rs, (n + 1) * (r - 1) + n, peer))
        return cps

    def start(cin, cout, ss, rs):
        for cp in copies(cin, cout, ss, rs):
            cp.start()

    def finish(cin, cout, ss, rs):
        for cp in copies(cin, cout, ss, rs):
            cp.wait()

    args = tuple(sums_bf) + ((small,) if small is not None else ())
    out_shape = tuple(jax.ShapeDtypeStruct((3,) + s.shape[1:], BF) for s in sums_bf)
    out_shape += (jax.ShapeDtypeStruct((3,) + small.shape, F32),) if small is not None else ()
    return _Comm(args, out_shape, {}, 3 * (n + 1), (("first", start), ("last", finish)))


def _join_comm(shards):
    n = len(shards)
    halves = [s.shape[0] // 2 for s in shards]

    def start(cin, cout, ss, rs):
        x, y, c, _ = _place()
        for a in range(n):
            mine = cout[a].at[pl.ds(c * halves[a], halves[a]), :]
            _remote(mine, mine, ss, rs, a, (x, y, 1 - c)).start()

    def finish(cin, cout, ss, rs):
        x, y, c, _ = _place()
        for a in range(n):
            theirs = cout[a].at[pl.ds((1 - c) * halves[a], halves[a]), :]
            cp = _remote(theirs, theirs, ss, rs, a, (x, y, 1 - c))
            cp.wait_recv()
            cp.wait_send()

    out_shape = tuple(jax.ShapeDtypeStruct(s.shape, F32) for s in shards)
    return _Comm(tuple(shards), out_shape, {a: a for a in range(n)}, n, (("first", start), ("last", finish)))


def _cast_shards(w_in, w_pa, w_pb, w_o, w_ff1, w_ff2, chip):
    def body(j_ref, win_ref, wpa_ref, wpb_ref, wo_ref, wff1_ref, wff2_ref, win4_ref, proj4_ref, ff14_ref, ff24_ref):
        win4_ref[...] = win_ref[...].astype(BF)
        for name, ref in (("w_pa", wpa_ref), ("w_pb", wpb_ref), ("w_o", wo_ref)):
            off, rows = PROJ_OFF[name]
            proj4_ref[off:off + rows, :] = ref[...].astype(BF)
        ff14_ref[...] = wff1_ref[...].astype(BF)
        ff24_ref[...] = wff2_ref[...].astype(BF)

    whole = lambda a: pl.BlockSpec(a.shape, lambda i, j: (0, 0), pipeline_mode=pl.Buffered(1))
    slot = lambda rows, cols: pl.BlockSpec((None, rows, cols), lambda i, j: (j[0], 0, 0))
    ws = (w_in, w_pa, w_pb, w_o, w_ff1, w_ff2)
    shapes = ((D, NP_SHARD), (PROJ_TOTAL, D), (D, FF_SHARD), (FF_SHARD, D))
    return _pc(
        body, name="cast_shards",
        grid_spec=pltpu.PrefetchScalarGridSpec(num_scalar_prefetch=1, grid=(1,), in_specs=[whole(w) for w in ws],
                                               out_specs=[slot(*s) for s in shapes]),
        out_shape=[jax.ShapeDtypeStruct((N_CHIP,) + s, BF) for s in shapes],
        compiler_params=_params("arbitrary"))(chip, *ws)


def _proj_fwd(x, chip, tm, comm):
    t = x.shape[0]
    sub = tm // EDGE_TILE

    def body(x_ref, p_ref, edge_ref, w_ref, w_sem, comm_refs):
        @pl.when(pl.program_id(1) == 0)
        def _():
            _, _, _, j = _place()
            block = pltpu.make_async_copy(comm_refs[0].at[j ^ pl.program_id(0)], w_ref, w_sem)
            block.start()
            block.wait()

        p_ref[...] = _dot(x_ref[...].astype(BF), w_ref[...])
        _write_edges(edge_ref, p_ref, tm)

    return _host_call(
        body, comm, name="proj_fwd", grid=(N_CHIP, t // tm), prefetch=chip, body_reads_comm=True,
        in_specs=[pl.BlockSpec((tm, D), lambda r, i, j: (i, 0))],
        out_specs=[pl.BlockSpec((tm, NP_SHARD), lambda r, i, j: (i, j[0] ^ r)),
                   pl.BlockSpec((sub, 2 * HALO, NP_SHARD), lambda r, i, j: (i, 0, j[0] ^ r))],
        out_shape=[jax.ShapeDtypeStruct((t, N_PROJ), F32), jax.ShapeDtypeStruct((t // EDGE_TILE, 2 * HALO, N_PROJ), F32)],
        scratch_shapes=[pltpu.VMEM((D, NP_SHARD), BF), pltpu.SemaphoreType.DMA],
        args=(x,))


def _edge_specs(t, tm, w):
    k, last = tm // EDGE_TILE, t // EDGE_TILE - 1
    return [pl.BlockSpec((None, HALO, w), lambda i: (jnp.maximum(i * k - 1, 0), 1, 0)),
            pl.BlockSpec((None, HALO, w), lambda i: (jnp.minimum((i + 1) * k, last), 0, 0))]


def _write_edges(edge_ref, rows, tm):
    for s in range(tm // EDGE_TILE):
        edge_ref[s, 0:HALO, :] = rows[s * EDGE_TILE:s * EDGE_TILE + HALO, :]
        edge_ref[s, HALO:2 * HALO, :] = rows[(s + 1) * EDGE_TILE - HALO:(s + 1) * EDGE_TILE, :]


def _pcols(p_ref, lo, hi):
    return p_ref[:, lo:hi]


def _end_masks(nt):
    i = pl.program_id(0)
    return (i > 0).astype(F32), (i < nt - 1).astype(F32)


def _conv_fwd(p_ref, prev_ref, next_ref, cw_ref, tm, has_prev, has_next):
    ca = _pcols(p_ref, OFF_CA, OFF_HA)
    ha = _pcols(p_ref, OFF_HA, OFF_UB)
    ch = ca * ha
    ch_prev = prev_ref[HALO - 1:HALO, OFF_CA:OFF_HA] * prev_ref[HALO - 1:HALO, OFF_HA:OFF_UB] * has_prev
    ch_next = next_ref[0:1, OFF_CA:OFF_HA] * next_ref[0:1, OFF_HA:OFF_UB] * has_next
    row = lax.broadcasted_iota(jnp.int32, (tm, W_A), 0)
    ch_m1 = jnp.where(row == 0, ch_prev, pltpu.roll(ch, 1, 0))
    ch_p1 = jnp.where(row == tm - 1, ch_next, pltpu.roll(ch, tm - 1, 0))
    cv = cw_ref[0:1, :] * ch_m1 + cw_ref[1:2, :] * ch + cw_ref[2:3, :] * ch_p1
    return ca, ha, ch, ch_m1, ch_p1, cv


def _spatial_fwd(p_ref, vg_ref, vb_ref, ws_ref, bsf_ref, vnb_ref, mixed_ref, tm):
    vb_pre = _pcols(p_ref, OFF_VB, OFF_GA)
    gv, tv = _gelu(vb_pre)
    xhv, rstdv = _ln_stats(gv)
    vnb_ref[...] = (xhv * vg_ref[...] + vb_ref[...]).astype(BF)
    for c in range(tm // CHUNK):
        rows = slice(c * CHUNK, (c + 1) * CHUNK)
        for h in range(N_HEAD):
            cols = slice(h * CHUNK, (h + 1) * CHUNK)
            mixed_ref[rows, cols] = _dot(ws_ref[h], vnb_ref[rows, cols]) + bsf_ref[:, cols]
    return vb_pre, tv, xhv, rstdv


def _mix_fwd(p, pedge, x, wpa, wpb, wo, wsb, bsf, bg, cw, vg, vb, tm):
    t = x.shape[0]
    nt = t // tm

    def outer(p_hbm, pedge_hbm, x_hbm, wpa_ref, wpb_ref, wo_ref, ws_ref, bsf_ref, bg_ref, cw_ref, vg_ref, vb_ref,
              r1_hbm, ya_hbm, yb_hbm, vnb_ref, mixed_ref):
        def body(step, p_ref, prev_ref, next_ref, x_ref, r1_ref, ya_ref, yb_ref):
            i = step.index[0]
            has_prev, has_next = (i > 0).astype(F32), (i < nt - 1).astype(F32)
            _, _, _, _, _, cv = _conv_fwd(p_ref, prev_ref, next_ref, cw_ref, tm, has_prev, has_next)
            a = _pcols(p_ref, 0, OFF_CA) * cv
            ya = _dot(a.astype(BF), wpa_ref[...])
            ya_ref[...] = ya
            _spatial_fwd(p_ref, vg_ref, vb_ref, ws_ref, bsf_ref, vnb_ref, mixed_ref, tm)
            gu, _ = _gelu(_pcols(p_ref, OFF_UB, OFF_VB))
            bb = gu * mixed_ref[...]
            yb = _dot(bb.astype(BF), wpb_ref[...])
            yb_ref[...] = yb
            ga = jax.nn.sigmoid(_pcols(p_ref, OFF_GA, OFF_GB) + bg_ref[:, 0:D])
            gb = jax.nn.sigmoid(_pcols(p_ref, OFF_GB, N_PROJ) + bg_ref[:, D:2 * D])
            z = ga * ya + gb * yb
            r1_ref[...] = ALPHA * x_ref[...] + _dot(z.astype(BF), wo_ref[...])

        tile = lambda w, **kw: pl.BlockSpec((tm, w), lambda i: (i, 0), **kw)
        pltpu.emit_pipeline(
            body, grid=(nt,), _explicit_indices=True,
            in_specs=[tile(N_PROJ, pipeline_mode=pl.Buffered(P_BUFFERS)), *_edge_specs(t, tm, N_PROJ), tile(D)],
            out_specs=[tile(D), tile(D), tile(D)],
        )(p_hbm, pedge_hbm, pedge_hbm, x_hbm, r1_hbm, ya_hbm, yb_hbm)

    hbm, vmem = pl.BlockSpec(memory_space=pl.ANY), pl.BlockSpec(memory_space=pltpu.VMEM)
    return _pc(
        outer, name="mix_fwd", in_specs=[hbm] * 3 + [vmem] * 9, out_specs=[hbm] * 3,
        out_shape=[jax.ShapeDtypeStruct((t, D), F32)] * 3,
        scratch_shapes=[pltpu.VMEM((tm, W_B), BF), pltpu.VMEM((tm, W_B), F32)],
        compiler_params=pltpu.CompilerParams(vmem_limit_bytes=VMEM_LIMIT),
    )(p, pedge, x, wpa, wpb, wo, wsb, bsf, bg, cw, vg, vb)


def _ffn_fwd_bwd(r1, tgt, wff1, wff2, ln1g, ln1b, ln2g, ln2b, tm):
    t = r1.shape[0]

    def body(r1_ref, tgt_ref, w1_ref, w2_ref, g1_ref, b1_ref, g2_ref, b2_ref,
             dr1_ref, dr1b_ref, dedge_ref, x1b_ref, hidb_ref, dh1b_ref, dr2b_ref, acc_ref, relu_ref):
        @pl.when(pl.program_id(0) == 0)
        def _():
            acc_ref[...] = jnp.zeros_like(acc_ref)

        xh1, rstd1 = _ln_stats(r1_ref[...])
        x1 = xh1 * g1_ref[...] + b1_ref[...]
        x1b_ref[...] = x1.astype(BF)
        ffn = jnp.zeros((tm, D), F32)
        for j in range(N_CHIP):
            cols = slice(j * FF_SHARD, (j + 1) * FF_SHARD)
            r = jnp.maximum(_dot(x1b_ref[...], w1_ref[j]), 0.0)
            relu_ref[:, cols] = r
            hidb_ref[:, cols] = (r * r).astype(BF)
            ffn = ffn + _dot(hidb_ref[:, cols], w2_ref[cols, :])
        xh2, rstd2 = _ln_stats(ALPHA * x1 + ffn)
        diff = xh2 * g2_ref[...] + b2_ref[...] - tgt_ref[...]
        acc_ref[4:5, :] += _colsum(diff * diff)
        dx2 = diff * (1.0 / D)
        acc_ref[2:3, :] += _colsum(dx2 * xh2)
        acc_ref[3:4, :] += _colsum(dx2)
        dr2 = _ln_bwd(dx2, g2_ref[...], xh2, rstd2)
        dr2b_ref[...] = dr2.astype(BF)
        dx1 = ALPHA * dr2
        for j in range(N_CHIP):
            cols = slice(j * FF_SHARD, (j + 1) * FF_SHARD)
            dhid = _dot_nt(dr2b_ref[...], w2_ref[cols, :])
            dh1b_ref[:, cols] = (dhid * (2.0 * relu_ref[:, cols])).astype(BF)
            dx1 = dx1 + _dot_nt(dh1b_ref[:, cols], w1_ref[j])
        acc_ref[0:1, :] += _colsum(dx1 * xh1)
        acc_ref[1:2, :] += _colsum(dx1)
        dr1 = _ln_bwd(dx1, g1_ref[...], xh1, rstd1)
        dr1_ref[...] = dr1
        dr1b_ref[...] = dr1.astype(BF)
        _write_edges(dedge_ref, dr1_ref, tm)

    tile = lambda w: pl.BlockSpec((tm, w), lambda i: (i, 0))
    vec = _resident((1, D))
    return _pc(
        body, name="ffn_fwd_bwd", grid=(t // tm,),
        in_specs=[tile(D), tile(D), _resident((N_CHIP, D, FF_SHARD)), _resident((D_FF, D)), vec, vec, vec, vec],
        out_specs=[tile(D), tile(D), pl.BlockSpec((tm // EDGE_TILE, 2 * HALO, D), lambda i: (i, 0, 0)), tile(D), tile(D_FF),
                   tile(D_FF), tile(D), pl.BlockSpec((8, D), lambda i: (0, 0))],
        out_shape=[jax.ShapeDtypeStruct((t, D), F32), jax.ShapeDtypeStruct((t, D), BF),
                   jax.ShapeDtypeStruct((t // EDGE_TILE, 2 * HALO, D), F32), jax.ShapeDtypeStruct((t, D), BF),
                   jax.ShapeDtypeStruct((t, D_FF), BF), jax.ShapeDtypeStruct((t, D_FF), BF),
                   jax.ShapeDtypeStruct((t, D), BF), jax.ShapeDtypeStruct((8, D), F32)],
        scratch_shapes=[pltpu.VMEM((tm, D_FF), F32)],
        compiler_params=_params("arbitrary"),
    )(r1, tgt, wff1, wff2, ln1g, ln1b, ln2g, ln2b)


def _dw(a, b, nblk, am, bn, a_blocked, b_blocked, tk, name, comm=None):
    t = a.shape[0]

    def body(a_ref, b_ref, o_ref):
        @pl.when(pl.program_id(1) == 0)
        def _():
            o_ref[...] = jnp.zeros_like(o_ref)

        o_ref[...] += _dot_tn(a_ref[...].astype(BF), b_ref[...])

    outs, got = _host_call(
        body, comm, name=name, grid=(nblk, t // tk),
        in_specs=[pl.BlockSpec((tk, am), (lambda j, k: (k, j)) if a_blocked else (lambda j, k: (k, 0))),
                  pl.BlockSpec((tk, bn), (lambda j, k: (k, j)) if b_blocked else (lambda j, k: (k, 0)))],
        out_specs=[pl.BlockSpec((None, am, bn), lambda j, k: (j, 0, 0))],
        out_shape=[jax.ShapeDtypeStruct((nblk, am, bn), F32)], args=(a, b))
    return outs[0] if comm is None else (outs[0], got)


def _dw_proj(ab, dyab, bbb, dybb, zb, dr1b, tk, comm):
    t = ab.shape[0]
    pairs = (("w_pa", 0, 1), ("w_pb", 2, 3), ("w_o", 4, 5))

    def body(*refs):
        o_ref = refs[6]

        @pl.when(pl.program_id(0) == 0)
        def _():
            o_ref[...] = jnp.zeros_like(o_ref)

        for name, ia, ib in pairs:
            off, rows = PROJ_OFF[name]
            for k in range(N_CHIP):
                o_ref[k, off:off + rows, :] += _dot_tn(refs[ia][:, k * rows:(k + 1) * rows], refs[ib][...])

    tile = lambda w: pl.BlockSpec((tk, w), lambda i: (i, 0))
    outs, got = _host_call(
        body, comm, name="dw_proj", grid=(t // tk,), in_specs=[tile(W_A), tile(D), tile(W_B), tile(D), tile(D), tile(D)],
        out_specs=[pl.BlockSpec((N_CHIP, PROJ_TOTAL, D), lambda i: (0, 0, 0))],
        out_shape=[jax.ShapeDtypeStruct((N_CHIP, PROJ_TOTAL, D), F32)], args=(ab, dyab, bbb, dybb, zb, dr1b))
    return outs[0], got


def _dx(dp, win4, dr1, tm, blk0, nblk, filled, name, comm):
    t = dp.shape[0]

    def body(dp_ref, w_ref, dr1_ref, *rest):
        dx = ALPHA * dr1_ref[...]
        for j in range(N_CHIP):
            dx = dx + _dot_nt(dp_ref[:, j * NP_SHARD:(j + 1) * NP_SHARD], w_ref[j])
        rest[-1][...] = dx

    in_specs = [pl.BlockSpec((tm, N_PROJ), lambda i: (i + blk0, 0)), _resident((N_CHIP, D, NP_SHARD)),
                pl.BlockSpec((tm, D), lambda i: (i + blk0, 0))]
    args = (dp, win4, dr1)
    aliases = None
    if filled is not None:
        in_specs.append(pl.BlockSpec(memory_space=pl.ANY))
        args += (filled,)
        aliases = {3: 0}
    outs, got = _host_call(
        body, comm, name=name, grid=(nblk,), in_specs=in_specs, out_specs=[pl.BlockSpec((tm, D), lambda i: (i + blk0, 0))],
        out_shape=[jax.ShapeDtypeStruct((t, D), F32)], args=args, aliases=aliases)
    return outs[0], got


def _mix_bwd(p, pedge, dr1, dedge, ya, yb, wpa, wpb, wo, wsb, wstb, bsf, bg, cw, vg, vb, tm, comm):
    t = p.shape[0]
    nt = t // tm
    te = tm + 2 * HALO
    mid = slice(HALO, HALO + tm)

    def body(p_ref, prev_ref, next_ref, dr1_ref, dprev_ref, dnext_ref, ya_ref, yb_ref, wpa_ref, wpb_ref, wo_ref,
             ws_ref, wst_ref, bsf_ref, bg_ref, cw_ref, vg_ref, vb_ref,
             dp_ref, ab_ref, bbb_ref, zb_ref, dyab_ref, dybb_ref, dbg_ref, dcw_ref, dvgb_ref, dws_ref, dbs_ref,
             vnb_ref, mixed_ref, dmixb_ref, dvn_ref):
        @pl.when(pl.program_id(0) == 0)
        def _():
            for r in (dbg_ref, dcw_ref, dvgb_ref, dws_ref, dbs_ref):
                r[...] = jnp.zeros_like(r)

        has_prev, has_next = _end_masks(nt)
        ca, ha, ch, ch_m1, ch_p1, cv = _conv_fwd(p_ref, prev_ref, next_ref, cw_ref, tm, has_prev, has_next)
        ba = _pcols(p_ref, 0, OFF_CA)
        ab_ref[...] = (ba * cv).astype(BF)
        vb_pre, tv, xhv, rstdv = _spatial_fwd(p_ref, vg_ref, vb_ref, ws_ref, bsf_ref, vnb_ref, mixed_ref, tm)
        ub = _pcols(p_ref, OFF_UB, OFF_VB)
        gu, tu = _gelu(ub)
        bbb_ref[...] = (gu * mixed_ref[...]).astype(BF)
        bga = bg_ref[:, 0:D]
        ga = jax.nn.sigmoid(_pcols(p_ref, OFF_GA, OFF_GB) + bga)
        gb = jax.nn.sigmoid(_pcols(p_ref, OFF_GB, N_PROJ) + bg_ref[:, D:2 * D])
        ya = ya_ref[...]
        yb = yb_ref[...]
        zb_ref[...] = (ga * ya + gb * yb).astype(BF)

        dr1_ext = jnp.concatenate([dprev_ref[...] * has_prev, dr1_ref[...], dnext_ref[...] * has_next], axis=0)
        dz_ext = _dot_nt(dr1_ext.astype(BF), wo_ref[...])
        ga_ext = jnp.concatenate([jax.nn.sigmoid(prev_ref[:, OFF_GA:OFF_GB] + bga), ga,
                                  jax.nn.sigmoid(next_ref[:, OFF_GA:OFF_GB] + bga)], axis=0)
        dya_ext = dz_ext * ga_ext
        dyab_ref[...] = dya_ext[mid].astype(BF)
        da_ext = _dot_nt(dya_ext.astype(BF), wpa_ref[...])
        ba_ext = jnp.concatenate([prev_ref[:, 0:OFF_CA], ba, next_ref[:, 0:OFF_CA]], axis=0)
        dcv_ext = da_ext * ba_ext
        dcv = dcv_ext[mid]
        dch = (cw_ref[0:1, :] * pltpu.roll(dcv_ext, te - 1, 0)[mid] + cw_ref[1:2, :] * dcv
               + cw_ref[2:3, :] * pltpu.roll(dcv_ext, 1, 0)[mid])
        dp_ref[:, 0:OFF_CA] = (da_ext[mid] * cv).astype(BF)
        dp_ref[:, OFF_CA:OFF_HA] = (dch * ha).astype(BF)
        dp_ref[:, OFF_HA:OFF_UB] = (dch * ca).astype(BF)
        dcw_ref[0:1, :] += _colsum(dcv * ch_m1)
        dcw_ref[1:2, :] += _colsum(dcv * ch)
        dcw_ref[2:3, :] += _colsum(dcv * ch_p1)

        dz = dz_ext[mid]
        dga = dz * ya * ga * (1.0 - ga)
        dgb = dz * yb * gb * (1.0 - gb)
        dp_ref[:, OFF_GA:OFF_GB] = dga.astype(BF)
        dp_ref[:, OFF_GB:N_PROJ] = dgb.astype(BF)
        dbg_ref[0:1, 0:D] += _colsum(dga)
        dbg_ref[0:1, D:2 * D] += _colsum(dgb)

        dybb_ref[...] = (dz * gb).astype(BF)
        dbb = _dot_nt(dybb_ref[...], wpb_ref[...])
        dp_ref[:, OFF_UB:OFF_VB] = (dbb * mixed_ref[...] * _gelu_grad(ub, tu)).astype(BF)
        dmixed = dbb * gu
        dmixb_ref[...] = dmixed.astype(BF)
        for c in range(tm // CHUNK):
            rows = slice(c * CHUNK, (c + 1) * CHUNK)
            dbs_ref[...] += dmixed[rows]
            for h in range(N_HEAD):
                cols = slice(h * CHUNK, (h + 1) * CHUNK)
                dws_ref[h] += _dot_nt(dmixb_ref[rows, cols], vnb_ref[rows, cols])
                dvn_ref[rows, cols] = _dot(wst_ref[h], dmixb_ref[rows, cols])
        dvn = dvn_ref[...]
        dvgb_ref[0:1, :] += _colsum(dvn * xhv)
        dvgb_ref[1:2, :] += _colsum(dvn)
        dgv = _ln_bwd(dvn, vg_ref[...], xhv, rstdv)
        dp_ref[:, OFF_VB:OFF_GA] = (dgv * _gelu_grad(vb_pre, tv)).astype(BF)

    tile = lambda w: pl.BlockSpec((tm, w), lambda i: (i, 0))
    acc = lambda *s: pl.BlockSpec(s, lambda i: (0,) * len(s))
    return _host_call(
        body, comm, name="mix_bwd", grid=(nt,),
        in_specs=[tile(N_PROJ), *_edge_specs(t, tm, N_PROJ), tile(D), *_edge_specs(t, tm, D), tile(D), tile(D),
                  _resident((W_A, D)), _resident((W_B, D)), _resident((D, D)), _resident((N_HEAD, CHUNK, CHUNK)),
                  _resident((N_HEAD, CHUNK, CHUNK)), _resident((CHUNK, W_B)), _resident((1, 2 * D)),
                  _resident((3, W_A)), _resident((1, W_B)), _resident((1, W_B))],
        out_specs=[tile(N_PROJ), tile(W_A), tile(W_B), tile(D), tile(D), tile(D),
                   acc(8, 2 * D), acc(8, W_A), acc(8, W_B), acc(N_HEAD, CHUNK, CHUNK), acc(CHUNK, W_B)],
        out_shape=[jax.ShapeDtypeStruct((t, N_PROJ), BF), jax.ShapeDtypeStruct((t, W_A), BF),
                   jax.ShapeDtypeStruct((t, W_B), BF), jax.ShapeDtypeStruct((t, D), BF), jax.ShapeDtypeStruct((t, D), BF),
                   jax.ShapeDtypeStruct((t, D), BF), jax.ShapeDtypeStruct((8, 2 * D), F32),
                   jax.ShapeDtypeStruct((8, W_A), F32), jax.ShapeDtypeStruct((8, W_B), F32),
                   jax.ShapeDtypeStruct((N_HEAD, CHUNK, CHUNK), F32), jax.ShapeDtypeStruct((CHUNK, W_B), F32)],
        scratch_shapes=[pltpu.VMEM((tm, W_B), BF), pltpu.VMEM((tm, W_B), F32), pltpu.VMEM((tm, W_B), BF),
                        pltpu.VMEM((tm, W_B), F32)],
        args=(p, pedge, pedge, dr1, dedge, dedge, ya, yb, wpa, wpb, wo, wsb, wstb, bsf, bg, cw, vg, vb))


def _add_own_half(full4, recv4, place, rb, name):
    n, rh, cols = recv4.shape
    nb = rh // rb

    def body(pl_ref, a_ref, b_ref, own_ref, ob_ref):
        s = a_ref[...] + b_ref[...]
        ob_ref[...] = s.astype(BF)

        @pl.when(pl.program_id(1) == pl_ref[0])
        def _():
            own_ref[...] = s

    blk = (None, rb, cols)
    return _pc(
        body, name=name,
        grid_spec=pltpu.PrefetchScalarGridSpec(
            num_scalar_prefetch=1, grid=(nb, n),
            in_specs=[pl.BlockSpec(blk, lambda i, k, s: (k, s[1] * nb + i, 0)), pl.BlockSpec(blk, lambda i, k, s: (k, i, 0))],
            out_specs=[pl.BlockSpec((rb, cols), lambda i, k, s: (i, 0)), pl.BlockSpec(blk, lambda i, k, s: (k, i, 0))]),
        out_shape=[jax.ShapeDtypeStruct((rh, cols), F32), jax.ShapeDtypeStruct(recv4.shape, BF)],
        compiler_params=_params("arbitrary", "arbitrary"),
    )(place, full4, recv4)


def _add_chips(own, r3, place, rb, name):
    _, rh, cols = r3.shape
    nb = rh // rb

    def body(pl_ref, s_ref, r_ref, o_ref):
        o_ref[...] = ((s_ref[...] + r_ref[0].astype(F32)) + r_ref[1].astype(F32)) + r_ref[2].astype(F32)

    return _pc(
        body, name=name,
        grid_spec=pltpu.PrefetchScalarGridSpec(
            num_scalar_prefetch=1, grid=(nb,),
            in_specs=[pl.BlockSpec((rb, cols), lambda i, s: (i, 0)), pl.BlockSpec((3, rb, cols), lambda i, s: (0, i, 0))],
            out_specs=pl.BlockSpec((rb, cols), lambda i, s: (s[1] * nb + i, 0))),
        out_shape=jax.ShapeDtypeStruct((2 * rh, cols), F32),
        compiler_params=_params("arbitrary"),
    )(place, own, r3)


def _add_small(a, b):
    def body(a_ref, b_ref, o_ref):
        o_ref[...] = a_ref[...] + b_ref[...]

    return _pc(body, name="add_small_cores", out_shape=jax.ShapeDtypeStruct(a.shape, F32))(a, b)


def _sum_small_chips(own, slots, place):
    def body(pl_ref, own_ref, s_ref, o_ref):
        j = pl_ref[0]

        def term(k):
            return jnp.where(j == k, own_ref[...], s_ref[jnp.maximum((j ^ k) - 1, 0)])

        o_ref[...] = ((term(0) + term(1)) + term(2)) + term(3)

    vmem = pl.BlockSpec(memory_space=pltpu.VMEM)
    return _pc(body, name="sum_small_chips", in_specs=[pl.BlockSpec(memory_space=pltpu.SMEM), vmem, vmem], out_specs=vmem,
               out_shape=jax.ShapeDtypeStruct(own.shape, F32))(place, own, slots)


def _adamw_step(w, g, m, v):
    m2 = ADAM_B1 * m + (1.0 - ADAM_B1) * g
    v2 = ADAM_B2 * v + (1.0 - ADAM_B2) * (g * g)
    m_hat = m2 / (1.0 - ADAM_B1 ** ADAM_STEP)
    v_hat = v2 / (1.0 - ADAM_B2 ** ADAM_STEP)
    return -ADAM_LR * (m_hat / (jnp.sqrt(v_hat) + ADAM_EPS) + ADAM_WD * w), m2, v2


def _adamw(w, g, m, v, rb, name):
    rows, cols = w.shape

    def body(w_ref, g_ref, m_ref, v_ref, d_ref, m2_ref, v2_ref):
        d_ref[...], m2_ref[...], v2_ref[...] = _adamw_step(w_ref[...], g_ref[...], m_ref[...], v_ref[...])

    blk = pl.BlockSpec((rb, cols), lambda i: (i, 0))
    return _pc(body, name=name, grid=(rows // rb,), in_specs=[blk] * 4, out_specs=[blk] * 3,
               out_shape=[jax.ShapeDtypeStruct((rows, cols), F32)] * 3, compiler_params=_params("arbitrary"))(w, g, m, v)


def _adamw_small(ws, gs, ms, vs):
    n = len(ws)

    def body(*refs):
        ins, outs = refs[:4 * n], refs[4 * n:]
        for i in range(n):
            outs[i][...], outs[n + i][...], outs[2 * n + i][...] = _adamw_step(*(ins[k * n + i][...] for k in range(4)))

    outs = _pc(body, name="adamw_small", out_shape=[jax.ShapeDtypeStruct(w.shape, F32) for w in ws] * 3)(*ws, *gs, *ms, *vs)
    return outs[:n], outs[n:2 * n], outs[2 * n:]


LANES = 128
SMALL_GRADS = (("b_gate", 2 * D), ("conv_w", 3 * W_A), ("v_norm_g", W_B), ("v_norm_b", W_B),
               ("w_s", N_HEAD * CHUNK * CHUNK), ("b_s", N_HEAD * CHUNK), ("ln1_g", D), ("ln1_b", D), ("ln2_g", D), ("ln2_b", D),
               ("loss", 1))


def _pack_rows(parts):
    rows = []
    for a in parts:
        a = a.reshape(-1)
        a = jnp.pad(a, (0, (-a.shape[0]) % LANES))
        rows.append(a.reshape(-1, LANES))
    out = jnp.concatenate(rows, axis=0)
    return jnp.pad(out, ((0, (-out.shape[0]) % 8), (0, 0)))


def _unpack_rows(buf, sizes):
    out, r = [], 0
    for n in sizes:
        nr = -(-n // LANES)
        out.append(buf[r:r + nr].reshape(-1)[:n])
        r += nr
    return out


TM_PROJ = 1024
TM_MIX = 256
TM_DX = 512
DX_PAIR = 6
TK_DW = 4096
TK_DW_IN = 2048
TK_DW_PROJ = 1024
ADD_BLOCK_BYTES = 3 * 1024 * 1024
RB_ADAM = 128
CONV_ROWS = 8


def _add_rows(rows, cols):
    while rows * cols * 4 > ADD_BLOCK_BYTES and rows % 32 == 0:
        rows //= 2
    return rows


def _reduce_adds_1(grads, recvs, place, tag):
    out = [_add_own_half(g, r, place, _add_rows(*r.shape[1:]), f"add_cores_{tag}{a}") for a, (g, r) in enumerate(zip(grads, recvs))]
    return [o[0] for o in out], [o[1] for o in out]


def _reduce_adds_2(sums, recvs, place, tag):
    return [_add_chips(s, r, place, _add_rows(*r.shape[1:]), f"add_chips_{tag}{a}") for a, (s, r) in enumerate(zip(sums, recvs))]


def kernel(x, w_in, b_gate, conv_w, v_norm_g, v_norm_b, w_s, b_s, w_pa, w_pb, w_o, ln1_g, ln1_b, w_ff1, w_ff2, ln2_g, ln2_b, loss_target, m_w_in, m_b_gate, m_conv_w, m_v_norm_g, m_v_norm_b, m_w_s, m_b_s, m_w_pa, m_w_pb, m_w_o, m_ln1_g, m_ln1_b, m_w_ff1, m_w_ff2, m_ln2_g, m_ln2_b, v_w_in, v_b_gate, v_conv_w, v_v_norm_g, v_v_norm_b, v_w_s, v_b_s, v_w_pa, v_w_pb, v_w_o, v_ln1_g, v_ln1_b, v_w_ff1, v_w_ff2, v_ln2_g, v_ln2_b):
    t = x.shape[1]
    core = lax.axis_index("c").astype(jnp.int32).reshape(1)
    chip_idx = 2 * lax.axis_index("x") + lax.axis_index("y")
    chip = chip_idx.astype(jnp.int32).reshape(1)
    place = jnp.concatenate([chip, core])
    x2 = x.reshape(t, D)
    tgt = loss_target.reshape(t, D)

    win4, proj4, ff14, ff24 = _cast_shards(w_in[0], w_pa[0], w_pb[0], w_o[0], w_ff1[0], w_ff2[0], chip)
    conv4 = lax.dynamic_update_slice(jnp.zeros((N_CHIP, CONV_ROWS, W_A // N_CHIP), F32),
                                     jnp.pad(conv_w[0], ((0, CONV_ROWS - 3), (0, 0)))[None], (chip_idx, 0, 0))
    (p, pedge), (win4, proj4, ff14, ff24, conv4) = _proj_fwd(
        x2, chip, TM_PROJ, _gather_comm([win4, proj4, ff14, ff24], conv4, eager=1))

    def full(name, rows_total):
        off, rows = PROJ_OFF[name]
        return proj4[:, off:off + rows, :].reshape(rows_total, D)

    wpa, wpb, wo = full("w_pa", W_A), full("w_pb", W_B), full("w_o", D)
    wff2 = ff24.reshape(D_FF, D)
    cw = jnp.transpose(conv4[:, :3, :], (1, 0, 2)).reshape(3, W_A)
    wsb = w_s[0].astype(BF)
    wstb = jnp.swapaxes(w_s[0], 1, 2).astype(BF)
    bsf = jnp.repeat(jnp.transpose(b_s[0]), CHUNK, axis=1)

    r1, ya, yb = _mix_fwd(p, pedge, x2, wpa, wpb, wo, wsb, bsf, b_gate, cw, v_norm_g, v_norm_b, TM_MIX)
    dr1, dr1b, dedge, x1b, hidb, dh1b, dr2b, acc = _ffn_fwd_bwd(r1, tgt, ff14, wff2, ln1_g, ln1_b, ln2_g, ln2_b, TM_MIX)
    g_ff = [_dw(x1b, dh1b, N_CHIP, D, FF_SHARD, False, True, TK_DW, "dw_ff1"),
            _dw(hidb, dr2b, N_CHIP, FF_SHARD, D, True, False, TK_DW, "dw_ff2")]
    (dp, ab, bbb, zb, dyab, dybb, dbg, dcw, dvgb, dws, dbs_sum), r_ff = _mix_bwd(
        p, pedge, dr1, dedge, ya, yb, wpa, wpb, wo, wsb, wstb, bsf, b_gate, cw, v_norm_g, v_norm_b, TM_MIX, _pair_comm(g_ff))
    s_ff, sb_ff = _reduce_adds_1(g_ff, r_ff, place, "ff")
    dwin4, c_ff = _dw(x2, dp, N_CHIP, D, NP_SHARD, False, True, TK_DW_IN, "dw_in", _chips_comm(sb_ff))
    f_ff = _reduce_adds_2(s_ff, c_ff, place, "ff")
    dproj4, (g_ff1, g_ff2) = _dw_proj(ab, dyab, bbb, dybb, zb, dr1b, TK_DW_PROJ, _join_comm(f_ff))
    dbs = jnp.transpose(jnp.sum(dbs_sum.reshape(CHUNK, N_HEAD, CHUNK), axis=-1))
    small = _pack_rows([dbg[0], dcw[0:3], dvgb[0], dvgb[1], dws, dbs, acc[0], acc[1], acc[2], acc[3],
                        0.5 * jnp.sum(acc[4]) / D])
    g_rest = [dwin4, dproj4]
    nblk = t // TM_DX
    n_a = max(1, min(DX_PAIR, nblk // 4))
    dx, r_rest = _dx(dp, win4, dr1, TM_DX, 0, n_a, None, "dx_a", _pair_comm(g_rest, small))
    s_rest, sb_rest = _reduce_adds_1(g_rest, r_rest[:2], place, "rest")
    csmall = _add_small(small, r_rest[2])
    dx, c_rest = _dx(dp, win4, dr1, TM_DX, n_a, nblk - n_a, dx, "dx_b", _chips_comm(sb_rest, csmall))
    f_rest = _reduce_adds_2(s_rest, c_rest[:2], place, "rest")
    gsmall = _sum_small_chips(csmall, c_rest[2], place)
    g_in, g_proj = _comm_call(_join_comm(f_rest), "join_rest")

    grads = {"w_in": g_in, "w_ff1": g_ff1, "w_ff2": g_ff2}
    for name, _ in PROJ_ROWS:
        off, rows = PROJ_OFF[name]
        grads[name] = g_proj[off:off + rows, :]
    for (name, n), flat in zip(SMALL_GRADS, _unpack_rows(gsmall, [n for _, n in SMALL_GRADS])):
        grads[name] = flat
    loss = grads.pop("loss").reshape(())
    grads["conv_w"] = lax.dynamic_slice(grads["conv_w"].reshape(3, W_A), (0, chip_idx * (W_A // N_CHIP)), (3, W_A // N_CHIP))

    weights = dict(w_in=w_in, b_gate=b_gate, conv_w=conv_w, v_norm_g=v_norm_g, v_norm_b=v_norm_b, w_s=w_s, b_s=b_s,
                   w_pa=w_pa, w_pb=w_pb, w_o=w_o, ln1_g=ln1_g, ln1_b=ln1_b, w_ff1=w_ff1, w_ff2=w_ff2, ln2_g=ln2_g, ln2_b=ln2_b)
    mom1 = dict(w_in=m_w_in, b_gate=m_b_gate, conv_w=m_conv_w, v_norm_g=m_v_norm_g, v_norm_b=m_v_norm_b, w_s=m_w_s,
                b_s=m_b_s, w_pa=m_w_pa, w_pb=m_w_pb, w_o=m_w_o, ln1_g=m_ln1_g, ln1_b=m_ln1_b, w_ff1=m_w_ff1,
                w_ff2=m_w_ff2, ln2_g=m_ln2_g, ln2_b=m_ln2_b)
    mom2 = dict(w_in=v_w_in, b_gate=v_b_gate, conv_w=v_conv_w, v_norm_g=v_v_norm_g, v_norm_b=v_v_norm_b, w_s=v_w_s,
                b_s=v_b_s, w_pa=v_w_pa, w_pb=v_w_pb, w_o=v_w_o, ln1_g=v_ln1_g, ln1_b=v_ln1_b, w_ff1=v_w_ff1,
                w_ff2=v_w_ff2, ln2_g=v_ln2_g, ln2_b=v_ln2_b)
    order = list(weights)
    big = ("w_in", "w_pa", "w_pb", "w_o", "w_ff1", "w_ff2")
    delta, new_m, new_v = {}, {}, {}
    for name in big:
        w2 = weights[name][0]
        delta[name], new_m[name], new_v[name] = _adamw(w2, grads[name], mom1[name][0], mom2[name][0], RB_ADAM, "adamw_" + name)
    little = [n for n in order if n not in big]
    flat2d = lambda a: a.reshape(-1, a.shape[-1])
    ds, ms, vs = _adamw_small(*([flat2d(d[n].reshape(weights[n].shape)) for n in little] for d in (weights, grads, mom1, mom2)))
    for name, d_, m_, v_ in zip(little, ds, ms, vs):
        delta[name], new_m[name], new_v[name] = d_, m_, v_

    shaped = lambda d: [d[n].reshape(weights[n].shape) for n in order]
    return (loss, dx.reshape(x.shape), *shaped(grads), *shaped(delta), *shaped(new_m), *shaped(new_v))
```

```python
import functools
from typing import NamedTuple

import jax
import jax.numpy as jnp
from jax import lax
from jax.experimental import pallas as pl
from jax.experimental.pallas import tpu as pltpu

D = 1024
W_A = 1536
W_B = 1024
CHUNK = 128
N_HEAD = 8
D_FF = 4096
N_PROJ = 3 * W_A + 2 * W_B + 2 * D
OFF_CA, OFF_HA, OFF_UB, OFF_VB, OFF_GA, OFF_GB = 1536, 3072, 4608, 5632, 6656, 7680
LN_EPS = 1e-5
ALPHA = 2.0 ** 0.25
N_CHIP = 4
NP_SHARD = N_PROJ // N_CHIP
FF_SHARD = D_FF // N_CHIP
ADAM_LR, ADAM_B1, ADAM_B2, ADAM_EPS, ADAM_WD, ADAM_STEP = 0.001, 0.9, 0.999, 1e-08, 0.01, 10

PROJ_ROWS = (("w_pa", W_A // N_CHIP), ("w_pb", W_B // N_CHIP), ("w_o", D // N_CHIP))
PROJ_OFF = {}
_o = 0
for _n, _r in PROJ_ROWS:
    PROJ_OFF[_n] = (_o, _r)
    _o += _r
PROJ_TOTAL = _o

V7X_VMEM_BYTES = 64 * 1024 * 1024
VMEM_LIMIT = V7X_VMEM_BYTES - 8 * 1024 * 1024
HALO = 8
EDGE_TILE = 256

BF = jnp.bfloat16
F32 = jnp.float32
MESH = pl.DeviceIdType.MESH
HBM_SPEC = pl.BlockSpec(memory_space=pltpu.HBM)


def _pc(body, **kw):
    return pl.pallas_call(body, **kw)


def _params(*sem):
    return pltpu.CompilerParams(dimension_semantics=sem, vmem_limit_bytes=VMEM_LIMIT)


def _resident(shape):
    n = len(shape)
    return pl.BlockSpec(shape, lambda *_: (0,) * n, pipeline_mode=pl.Buffered(1))


def _dot(a, b):
    return jnp.dot(a, b, preferred_element_type=F32)


def _dot_nt(a, b):
    return lax.dot_general(a, b, (((1,), (1,)), ((), ())), preferred_element_type=F32)


def _dot_tn(a, b):
    return lax.dot_general(a, b, (((0,), (0,)), ((), ())), preferred_element_type=F32)


def _gelu(x):
    t = jnp.tanh(0.7978845608028654 * (x + 0.044715 * (x * x * x)))
    return 0.5 * x * (1.0 + t), t


def _gelu_grad(x, t):
    return 0.5 * (1.0 + t) + 0.5 * x * (1.0 - t * t) * (0.7978845608028654 * (1.0 + 0.134145 * (x * x)))


def _sigmoid(x):
    return 0.5 * jnp.tanh(0.5 * x) + 0.5


def _ln_stats(r):
    mu = jnp.mean(r, axis=-1, keepdims=True)
    xc = r - mu
    var = jnp.mean(xc * xc, axis=-1, keepdims=True)
    rstd = lax.rsqrt(var + LN_EPS)
    return xc * rstd, rstd


def _ln_bwd(dy, g, xh, rstd):
    dxh = dy * g
    m1 = jnp.mean(dxh, axis=-1, keepdims=True)
    m2 = jnp.mean(dxh * xh, axis=-1, keepdims=True)
    return rstd * (dxh - m1 - xh * m2)


def _colsum(v):
    return jnp.sum(v, axis=0, keepdims=True)


class _Comm(NamedTuple):
    args: tuple
    out_shape: tuple
    aliases: dict
    n_sems: int
    stages: tuple


def _place():
    x, y, c = lax.axis_index("x"), lax.axis_index("y"), lax.axis_index("c")
    return x, y, c, 2 * x + y


def _flip(x, y, c, r):
    return (x ^ (r >> 1), y ^ (r & 1), c)


def _remote(src, dst, send_sems, recv_sems, k, peer):
    return pltpu.make_async_remote_copy(src_ref=src, dst_ref=dst, send_sem=send_sems.at[k], recv_sem=recv_sems.at[k],
                                        device_id=peer, device_id_type=MESH)


def _host_call(body, comm, *, name, grid, in_specs, out_specs, out_shape, args, scratch_shapes=(), aliases=None, prefetch=None,
               body_reads_comm=False):
    sem = ("arbitrary",) * len(grid)
    aliases = dict(aliases or {})
    n_pre = 0 if prefetch is None else 1
    n_in, n_out, n_scr = len(in_specs), len(out_specs), len(scratch_shapes)
    c_in, c_out = (0, 0) if comm is None else (len(comm.args), len(comm.out_shape))
    steps = {"first": (0,) * len(grid), "late": (grid[0] - 1,) + (0,) * (len(grid) - 1), "last": tuple(g - 1 for g in grid)}

    def wrapped(*refs):
        refs = refs[n_pre:]
        own_in, cin = refs[:n_in], refs[n_in:n_in + c_in]
        o0 = n_in + c_in
        own_out, cout = refs[o0:o0 + n_out], refs[o0 + n_out:o0 + n_out + c_out]
        s0 = o0 + n_out + c_out
        scr, sems = refs[s0:s0 + n_scr], refs[s0 + n_scr:]

        def run(before):
            for phase, fn in () if comm is None else comm.stages:
                at_step = isinstance(phase, tuple)
                if before != (at_step or phase == "first"):
                    continue
                step = phase if at_step else steps[phase]
                cond = pl.program_id(0) == step[0]
                for d in range(1, len(grid)):
                    cond = jnp.logical_and(cond, pl.program_id(d) == step[d])
                pl.when(cond)(functools.partial(fn, cin, cout, *sems))

        run(True)
        if body_reads_comm:
            body(*own_in, *own_out, *scr, comm_refs=cout)
        else:
            body(*own_in, *own_out, *scr)
        run(False)

    in_specs = list(in_specs) + [HBM_SPEC] * c_in
    out_specs = list(out_specs) + [HBM_SPEC] * c_out
    out_shape = list(out_shape) + ([] if comm is None else list(comm.out_shape))
    scratch_shapes = list(scratch_shapes) + ([] if comm is None else [pltpu.SemaphoreType.DMA((comm.n_sems,))] * 2)
    args = tuple(args) + (() if comm is None else tuple(comm.args))
    if comm is not None:
        aliases.update({n_in + i: n_out + o for i, o in comm.aliases.items()})
    aliases = {i + n_pre: o for i, o in aliases.items()}
    if prefetch is None:
        kw = dict(grid=grid, in_specs=in_specs, out_specs=out_specs, scratch_shapes=scratch_shapes)
    else:
        kw = dict(grid_spec=pltpu.PrefetchScalarGridSpec(num_scalar_prefetch=1, grid=grid, in_specs=in_specs,
                                                         out_specs=out_specs, scratch_shapes=scratch_shapes))
        args = (prefetch,) + args
    outs = _pc(wrapped, name=name, out_shape=out_shape, input_output_aliases=aliases, compiler_params=_params(*sem), **kw)(*args)
    return outs[:n_out], outs[n_out:]


def _comm_call(comm, name):
    def body(*refs):
        c_in, c_out = len(comm.args), len(comm.out_shape)
        cin, cout, (send_sems, recv_sems) = refs[:c_in], refs[c_in:c_in + c_out], refs[c_in + c_out:]
        for phase in ("first", "late", "last"):
            for ph, fn in comm.stages:
                if ph == phase:
                    fn(cin, cout, send_sems, recv_sems)

    return _pc(body, name=name, in_specs=[HBM_SPEC] * len(comm.args), out_specs=[HBM_SPEC] * len(comm.out_shape),
               out_shape=list(comm.out_shape), scratch_shapes=[pltpu.SemaphoreType.DMA((comm.n_sems,))] * 2,
               input_output_aliases=dict(comm.aliases))(*comm.args)


def _gather_comm(bufs, whole=None, eager=0):
    n = len(bufs)
    halves = [b.shape[1] // 2 for b in bufs]
    k_ici = lambda a, r: 3 * a + r - 1
    k_d2d = lambda a, r: 3 * n + 3 * a + r - 1
    k_whole = lambda r: 6 * n + r - 1

    def half(ref, slot, c, a):
        return ref.at[slot, pl.ds(c * halves[a], halves[a])]

    def send(cin, cout, ss, rs):
        x, y, c, j = _place()
        for a in range(n):
            mine = half(cout[a], j, c, a)
            for r in (1, 2, 3):
                _remote(mine, mine, ss, rs, k_ici(a, r), _flip(x, y, c, r)).start()
        if whole is not None:
            for r in (1, 2, 3):
                _remote(cout[n].at[j], cout[n].at[j], ss, rs, k_whole(r), _flip(x, y, c, r)).start()

    def pass_on(cout, ss, rs, a, r):
        x, y, c, j = _place()
        landed = half(cout[a], j ^ r, c, a)
        _remote(landed, landed, ss, rs, k_ici(a, r), (x, y, 1 - c)).wait_recv()
        _remote(landed, landed, ss, rs, k_d2d(a, r), (x, y, 1 - c)).start()

    def passed_on(cout, ss, rs, a, r):
        x, y, c, j = _place()
        theirs = half(cout[a], j ^ r, 1 - c, a)
        _remote(theirs, theirs, ss, rs, k_d2d(a, r), (x, y, 1 - c)).wait_recv()

    def arrive(r, cin, cout, ss, rs):
        for a in range(eager):
            pass_on(cout, ss, rs, a, r)
        for a in range(eager):
            passed_on(cout, ss, rs, a, r)

    def forward(cin, cout, ss, rs):
        for a in range(eager, n):
            for r in (1, 2, 3):
                pass_on(cout, ss, rs, a, r)

    def finish(cin, cout, ss, rs):
        x, y, c, j = _place()
        sibling = (x, y, 1 - c)
        for a in range(eager, n):
            for r in (1, 2, 3):
                passed_on(cout, ss, rs, a, r)
        for a in range(n):
            mine = half(cout[a], j, c, a)
            for r in (1, 2, 3):
                _remote(mine, mine, ss, rs, k_ici(a, r), sibling).wait_send()
                landed = half(cout[a], j ^ r, c, a)
                _remote(landed, landed, ss, rs, k_d2d(a, r), sibling).wait_send()
        if whole is not None:
            for r in (1, 2, 3):
                cp = _remote(cout[n].at[j ^ r], cout[n].at[j ^ r], ss, rs, k_whole(r), sibling)
                cp.wait_recv()
                cp.wait_send()

    args = tuple(bufs) + ((whole,) if whole is not None else ())
    out_shape = tuple(jax.ShapeDtypeStruct(b.shape, b.dtype) for b in args)
    aliases = {a: a for a in range(len(args))}
    arrivals = tuple(((r, 0), functools.partial(arrive, r)) for r in (1, 2, 3)) if eager else ()
    return _Comm(args, out_shape, aliases, 6 * n + 3, (("first", send),) + arrivals + (("late", forward), ("last", finish)))


def _pair_comm(grads, small=None):
    n = len(grads)
    halves = [g.shape[1] // 2 for g in grads]

    def copies(cin, cout, ss, rs):
        x, y, c, _ = _place()
        sibling = (x, y, 1 - c)
        cps = [_remote(cin[a].at[:, pl.ds((1 - c) * halves[a], halves[a]), :], cout[a], ss, rs, a, sibling) for a in range(n)]
        if small is not None:
            cps.append(_remote(cin[n], cout[n], ss, rs, n, sibling))
        return cps

    def start(cin, cout, ss, rs):
        for cp in copies(cin, cout, ss, rs):
            cp.start()

    def finish(cin, cout, ss, rs):
        for cp in copies(cin, cout, ss, rs):
            cp.wait()

    args = tuple(grads) + ((small,) if small is not None else ())
    out_shape = tuple(jax.ShapeDtypeStruct((N_CHIP, h, g.shape[2]), F32) for g, h in zip(grads, halves))
    out_shape += (jax.ShapeDtypeStruct(small.shape, F32),) if small is not None else ()
    return _Comm(args, out_shape, {}, n + 1, (("first", start), ("last", finish)))


def _chips_comm(sums_bf, small=None):
    n = len(sums_bf)

    def copies(cin, cout, ss, rs):
        x, y, c, j = _place()
        cps = []
        for r in (1, 2, 3):
            peer = _flip(x, y, c, r)
            for a in range(n):
                cps.append(_remote(cin[a].at[j ^ r], cout[a].at[r - 1], ss, rs, (n + 1) * (r - 1) + a, peer))
            if small is not None:
                cps.append(_remote(cin[n], cout[n].at[r - 1], ss, rs, (n + 1) * (r - 1) + n, peer))
        return cps

    def start(cin, cout, ss, rs):
        for cp in copies(cin, cout, ss, rs):
            cp.start()

    def finish(cin, cout, ss, rs):
        for cp in copies(cin, cout, ss, rs):
            cp.wait()

    args = tuple(sums_bf) + ((small,) if small is not None else ())
    out_shape = tuple(jax.ShapeDtypeStruct((3,) + s.shape[1:], BF) for s in sums_bf)
    out_shape += (jax.ShapeDtypeStruct((3,) + small.shape, F32),) if small is not None else ()
    return _Comm(args, out_shape, {}, 3 * (n + 1), (("first", start), ("last", finish)))


def _join_comm(shards):
    n = len(shards)
    halves = [s.shape[0] // 2 for s in shards]

    def start(cin, cout, ss, rs):
        x, y, c, _ = _place()
        for a in range(n):
            mine = cout[a].at[pl.ds(c * halves[a], halves[a]), :]
            _remote(mine, mine, ss, rs, a, (x, y, 1 - c)).start()

    def finish(cin, cout, ss, rs):
        x, y, c, _ = _place()
        for a in range(n):
            theirs = cout[a].at[pl.ds((1 - c) * halves[a], halves[a]), :]
            cp = _remote(theirs, theirs, ss, rs, a, (x, y, 1 - c))
            cp.wait_recv()
            cp.wait_send()

    out_shape = tuple(jax.ShapeDtypeStruct(s.shape, F32) for s in shards)
    return _Comm(tuple(shards), out_shape, {a: a for a in range(n)}, n, (("first", start), ("last", finish)))


def _cast_shards(w_in, w_pa, w_pb, w_o, w_ff1, w_ff2, chip):
    def body(j_ref, win_ref, wpa_ref, wpb_ref, wo_ref, wff1_ref, wff2_ref, win4_ref, proj4_ref, ff14_ref, ff24_ref):
        win4_ref[...] = win_ref[...].astype(BF)
        for name, ref in (("w_pa", wpa_ref), ("w_pb", wpb_ref), ("w_o", wo_ref)):
            off, rows = PROJ_OFF[name]
            proj4_ref[off:off + rows, :] = ref[...].astype(BF)
        ff14_ref[...] = wff1_ref[...].astype(BF)
        ff24_ref[...] = wff2_ref[...].astype(BF)

    whole = lambda a: pl.BlockSpec(a.shape, lambda i, j: (0, 0), pipeline_mode=pl.Buffered(1))
    slot = lambda rows, cols: pl.BlockSpec((None, rows, cols), lambda i, j: (j[0], 0, 0))
    ws = (w_in, w_pa, w_pb, w_o, w_ff1, w_ff2)
    shapes = ((D, NP_SHARD), (PROJ_TOTAL, D), (D, FF_SHARD), (FF_SHARD, D))
    return _pc(
        body, name="cast_shards",
        grid_spec=pltpu.PrefetchScalarGridSpec(num_scalar_prefetch=1, grid=(1,), in_specs=[whole(w) for w in ws],
                                               out_specs=[slot(*s) for s in shapes]),
        out_shape=[jax.ShapeDtypeStruct((N_CHIP,) + s, BF) for s in shapes],
        compiler_params=_params("arbitrary"))(chip, *ws)


def _proj_fwd(x, chip, tm, comm):
    t = x.shape[0]
    sub = tm // EDGE_TILE

    def body(x_ref, p_ref, edge_ref, w_ref, w_sem, comm_refs):
        @pl.when(pl.program_id(1) == 0)
        def _():
            _, _, _, j = _place()
            block = pltpu.make_async_copy(comm_refs[0].at[j ^ pl.program_id(0)], w_ref, w_sem)
            block.start()
            block.wait()

        p_ref[...] = _dot(x_ref[...].astype(BF), w_ref[...])
        _write_edges(edge_ref, p_ref, tm)

    return _host_call(
        body, comm, name="proj_fwd", grid=(N_CHIP, t // tm), prefetch=chip, body_reads_comm=True,
        in_specs=[pl.BlockSpec((tm, D), lambda r, i, j: (i, 0))],
        out_specs=[pl.BlockSpec((tm, NP_SHARD), lambda r, i, j: (i, j[0] ^ r)),
                   pl.BlockSpec((sub, 2 * HALO, NP_SHARD), lambda r, i, j: (i, 0, j[0] ^ r))],
        out_shape=[jax.ShapeDtypeStruct((t, N_PROJ), F32), jax.ShapeDtypeStruct((t // EDGE_TILE, 2 * HALO, N_PROJ), F32)],
        scratch_shapes=[pltpu.VMEM((D, NP_SHARD), BF), pltpu.SemaphoreType.DMA],
        args=(x,))


def _edge_specs(t, tm, w):
    k, last = tm // EDGE_TILE, t // EDGE_TILE - 1
    return [pl.BlockSpec((None, HALO, w), lambda i: (jnp.maximum(i * k - 1, 0), 1, 0)),
            pl.BlockSpec((None, HALO, w), lambda i: (jnp.minimum((i + 1) * k, last), 0, 0))]


def _write_edges(edge_ref, rows, tm):
    for s in range(tm // EDGE_TILE):
        edge_ref[s, 0:HALO, :] = rows[s * EDGE_TILE:s * EDGE_TILE + HALO, :]
        edge_ref[s, HALO:2 * HALO, :] = rows[(s + 1) * EDGE_TILE - HALO:(s + 1) * EDGE_TILE, :]


def _pcols(p_ref, lo, hi):
    return p_ref[:, lo:hi]


def _end_masks(nt):
    i = pl.program_id(0)
    return (i > 0).astype(F32), (i < nt - 1).astype(F32)


def _conv_fwd(p_ref, prev_ref, next_ref, cw_ref, tm, has_prev, has_next):
    ca = _pcols(p_ref, OFF_CA, OFF_HA)
    ha = _pcols(p_ref, OFF_HA, OFF_UB)
    ch = ca * ha
    ch_prev = prev_ref[HALO - 1:HALO, OFF_CA:OFF_HA] * prev_ref[HALO - 1:HALO, OFF_HA:OFF_UB] * has_prev
    ch_next = next_ref[0:1, OFF_CA:OFF_HA] * next_ref[0:1, OFF_HA:OFF_UB] * has_next
    row = lax.broadcasted_iota(jnp.int32, (tm, W_A), 0)
    ch_m1 = jnp.where(row == 0, ch_prev, pltpu.roll(ch, 1, 0))
    ch_p1 = jnp.where(row == tm - 1, ch_next, pltpu.roll(ch, tm - 1, 0))
    cv = cw_ref[0:1, :] * ch_m1 + cw_ref[1:2, :] * ch + cw_ref[2:3, :] * ch_p1
    return ca, ha, ch, ch_m1, ch_p1, cv


def _spatial_fwd(p_ref, vg_ref, vb_ref, ws_ref, bsf_ref, vnb_ref, mixed_ref, tm):
    vb_pre = _pcols(p_ref, OFF_VB, OFF_GA)
    gv, tv = _gelu(vb_pre)
    xhv, rstdv = _ln_stats(gv)
    vnb_ref[...] = (xhv * vg_ref[...] + vb_ref[...]).astype(BF)
    for c in range(tm // CHUNK):
        rows = slice(c * CHUNK, (c + 1) * CHUNK)
        for h in range(N_HEAD):
            cols = slice(h * CHUNK, (h + 1) * CHUNK)
            mixed_ref[rows, cols] = _dot(ws_ref[h], vnb_ref[rows, cols]) + bsf_ref[:, cols]
    return vb_pre, tv, xhv, rstdv


def _mix_fwd(p, pedge, x, wpa, wpb, wo, wsb, bsf, bg, cw, vg, vb, tm):
    t = x.shape[0]
    nt = t // tm

    def body(p_ref, prev_ref, next_ref, x_ref, wpa_ref, wpb_ref, wo_ref, ws_ref, bsf_ref, bg_ref, cw_ref, vg_ref, vb_ref,
             r1_ref, ya_ref, yb_ref, vnb_ref, mixed_ref):
        has_prev, has_next = _end_masks(nt)
        _, _, _, _, _, cv = _conv_fwd(p_ref, prev_ref, next_ref, cw_ref, tm, has_prev, has_next)
        a = _pcols(p_ref, 0, OFF_CA) * cv
        ya = _dot(a.astype(BF), wpa_ref[...])
        ya_ref[...] = ya
        _spatial_fwd(p_ref, vg_ref, vb_ref, ws_ref, bsf_ref, vnb_ref, mixed_ref, tm)
        gu, _ = _gelu(_pcols(p_ref, OFF_UB, OFF_VB))
        bb = gu * mixed_ref[...]
        yb = _dot(bb.astype(BF), wpb_ref[...])
        yb_ref[...] = yb
        ga = _sigmoid(_pcols(p_ref, OFF_GA, OFF_GB) + bg_ref[:, 0:D])
        gb = _sigmoid(_pcols(p_ref, OFF_GB, N_PROJ) + bg_ref[:, D:2 * D])
        z = ga * ya + gb * yb
        r1_ref[...] = ALPHA * x_ref[...] + _dot(z.astype(BF), wo_ref[...])

    tile = lambda w: pl.BlockSpec((tm, w), lambda i: (i, 0))
    return _pc(
        body, name="mix_fwd", grid=(nt,),
        in_specs=[tile(N_PROJ), *_edge_specs(t, tm, N_PROJ), tile(D),
                  _resident((W_A, D)), _resident((W_B, D)), _resident((D, D)), _resident((N_HEAD, CHUNK, CHUNK)),
                  _resident((CHUNK, W_B)), _resident((1, 2 * D)), _resident((3, W_A)), _resident((1, W_B)),
                  _resident((1, W_B))],
        out_specs=[tile(D), tile(D), tile(D)],
        out_shape=[jax.ShapeDtypeStruct((t, D), F32)] * 3,
        scratch_shapes=[pltpu.VMEM((tm, W_B), BF), pltpu.VMEM((tm, W_B), F32)],
        compiler_params=_params("arbitrary"),
    )(p, pedge, pedge, x, wpa, wpb, wo, wsb, bsf, bg, cw, vg, vb)


def _ffn_fwd_bwd(r1, tgt, wff1, wff2, ln1g, ln1b, ln2g, ln2b, tm):
    t = r1.shape[0]

    def body(r1_ref, tgt_ref, w1_ref, w2_ref, g1_ref, b1_ref, g2_ref, b2_ref,
             dr1_ref, dr1b_ref, dedge_ref, x1b_ref, hidb_ref, dh1b_ref, dr2b_ref, acc_ref, relu_ref):
        @pl.when(pl.program_id(0) == 0)
        def _():
            acc_ref[...] = jnp.zeros_like(acc_ref)

        xh1, rstd1 = _ln_stats(r1_ref[...])
        x1 = xh1 * g1_ref[...] + b1_ref[...]
        x1b_ref[...] = x1.astype(BF)
        ffn = jnp.zeros((tm, D), F32)
        for j in range(N_CHIP):
            cols = slice(j * FF_SHARD, (j + 1) * FF_SHARD)
            r = jnp.maximum(_dot(x1b_ref[...], w1_ref[j]), 0.0)
            relu_ref[:, cols] = r
            hidb_ref[:, cols] = (r * r).astype(BF)
            ffn = ffn + _dot(hidb_ref[:, cols], w2_ref[cols, :])
        xh2, rstd2 = _ln_stats(ALPHA * x1 + ffn)
        diff = xh2 * g2_ref[...] + b2_ref[...] - tgt_ref[...]
        acc_ref[4:5, :] += _colsum(diff * diff)
        dx2 = diff * (1.0 / D)
        acc_ref[2:3, :] += _colsum(dx2 * xh2)
        acc_ref[3:4, :] += _colsum(dx2)
        dr2 = _ln_bwd(dx2, g2_ref[...], xh2, rstd2)
        dr2b_ref[...] = dr2.astype(BF)
        dx1 = ALPHA * dr2
        for j in range(N_CHIP):
            cols = slice(j * FF_SHARD, (j + 1) * FF_SHARD)
            dhid = _dot_nt(dr2b_ref[...], w2_ref[cols, :])
            dh1b_ref[:, cols] = (dhid * (2.0 * relu_ref[:, cols])).astype(BF)
            dx1 = dx1 + _dot_nt(dh1b_ref[:, cols], w1_ref[j])
        acc_ref[0:1, :] += _colsum(dx1 * xh1)
        acc_ref[1:2, :] += _colsum(dx1)
        dr1 = _ln_bwd(dx1, g1_ref[...], xh1, rstd1)
        dr1_ref[...] = dr1
        dr1b_ref[...] = dr1.astype(BF)
        _write_edges(dedge_ref, dr1_ref, tm)

    tile = lambda w: pl.BlockSpec((tm, w), lambda i: (i, 0))
    vec = _resident((1, D))
    return _pc(
        body, name="ffn_fwd_bwd", grid=(t // tm,),
        in_specs=[tile(D), tile(D), _resident((N_CHIP, D, FF_SHARD)), _resident((D_FF, D)), vec, vec, vec, vec],
        out_specs=[tile(D), tile(D), pl.BlockSpec((tm // EDGE_TILE, 2 * HALO, D), lambda i: (i, 0, 0)), tile(D), tile(D_FF),
                   tile(D_FF), tile(D), pl.BlockSpec((8, D), lambda i: (0, 0))],
        out_shape=[jax.ShapeDtypeStruct((t, D), F32), jax.ShapeDtypeStruct((t, D), BF),
                   jax.ShapeDtypeStruct((t // EDGE_TILE, 2 * HALO, D), F32), jax.ShapeDtypeStruct((t, D), BF),
                   jax.ShapeDtypeStruct((t, D_FF), BF), jax.ShapeDtypeStruct((t, D_FF), BF),
                   jax.ShapeDtypeStruct((t, D), BF), jax.ShapeDtypeStruct((8, D), F32)],
        scratch_shapes=[pltpu.VMEM((tm, D_FF), F32)],
        compiler_params=_params("arbitrary"),
    )(r1, tgt, wff1, wff2, ln1g, ln1b, ln2g, ln2b)


def _dw(a, b, nblk, am, bn, a_blocked, b_blocked, tk, name, comm=None):
    t = a.shape[0]

    def body(a_ref, b_ref, o_ref):
        @pl.when(pl.program_id(1) == 0)
        def _():
            o_ref[...] = jnp.zeros_like(o_ref)

        o_ref[...] += _dot_tn(a_ref[...].astype(BF), b_ref[...])

    outs, got = _host_call(
        body, comm, name=name, grid=(nblk, t // tk),
        in_specs=[pl.BlockSpec((tk, am), (lambda j, k: (k, j)) if a_blocked else (lambda j, k: (k, 0))),
                  pl.BlockSpec((tk, bn), (lambda j, k: (k, j)) if b_blocked else (lambda j, k: (k, 0)))],
        out_specs=[pl.BlockSpec((None, am, bn), lambda j, k: (j, 0, 0))],
        out_shape=[jax.ShapeDtypeStruct((nblk, am, bn), F32)], args=(a, b))
    return outs[0] if comm is None else (outs[0], got)


def _dw_proj(ab, dyab, bbb, dybb, zb, dr1b, tk, comm):
    t = ab.shape[0]
    pairs = (("w_pa", 0, 1), ("w_pb", 2, 3), ("w_o", 4, 5))

    def body(*refs):
        o_ref = refs[6]

        @pl.when(pl.program_id(0) == 0)
        def _():
            o_ref[...] = jnp.zeros_like(o_ref)

        for name, ia, ib in pairs:
            off, rows = PROJ_OFF[name]
            for k in range(N_CHIP):
                o_ref[k, off:off + rows, :] += _dot_tn(refs[ia][:, k * rows:(k + 1) * rows], refs[ib][...])

    tile = lambda w: pl.BlockSpec((tk, w), lambda i: (i, 0))
    outs, got = _host_call(
        body, comm, name="dw_proj", grid=(t // tk,), in_specs=[tile(W_A), tile(D), tile(W_B), tile(D), tile(D), tile(D)],
        out_specs=[pl.BlockSpec((N_CHIP, PROJ_TOTAL, D), lambda i: (0, 0, 0))],
        out_shape=[jax.ShapeDtypeStruct((N_CHIP, PROJ_TOTAL, D), F32)], args=(ab, dyab, bbb, dybb, zb, dr1b))
    return outs[0], got


def _dx(dp, win4, dr1, tm, blk0, nblk, filled, name, comm):
    t = dp.shape[0]

    def body(dp_ref, w_ref, dr1_ref, *rest):
        dx = ALPHA * dr1_ref[...]
        for j in range(N_CHIP):
            dx = dx + _dot_nt(dp_ref[:, j * NP_SHARD:(j + 1) * NP_SHARD], w_ref[j])
        rest[-1][...] = dx

    in_specs = [pl.BlockSpec((tm, N_PROJ), lambda i: (i + blk0, 0)), _resident((N_CHIP, D, NP_SHARD)),
                pl.BlockSpec((tm, D), lambda i: (i + blk0, 0))]
    args = (dp, win4, dr1)
    aliases = None
    if filled is not None:
        in_specs.append(pl.BlockSpec(memory_space=pl.ANY))
        args += (filled,)
        aliases = {3: 0}
    outs, got = _host_call(
        body, comm, name=name, grid=(nblk,), in_specs=in_specs, out_specs=[pl.BlockSpec((tm, D), lambda i: (i + blk0, 0))],
        out_shape=[jax.ShapeDtypeStruct((t, D), F32)], args=args, aliases=aliases)
    return outs[0], got


def _mix_bwd(p, pedge, dr1, dedge, ya, yb, wpa, wpb, wo, wsb, wstb, bsf, bg, cw, vg, vb, tm, comm):
    t = p.shape[0]
    nt = t // tm
    te = tm + 2 * HALO
    mid = slice(HALO, HALO + tm)

    def body(p_ref, prev_ref, next_ref, dr1_ref, dprev_ref, dnext_ref, ya_ref, yb_ref, wpa_ref, wpb_ref, wo_ref,
             ws_ref, wst_ref, bsf_ref, bg_ref, cw_ref, vg_ref, vb_ref,
             dp_ref, ab_ref, bbb_ref, zb_ref, dyab_ref, dybb_ref, dbg_ref, dcw_ref, dvgb_ref, dws_ref, dbs_ref,
             vnb_ref, mixed_ref, dmixb_ref, dvn_ref):
        @pl.when(pl.program_id(0) == 0)
        def _():
            for r in (dbg_ref, dcw_ref, dvgb_ref, dws_ref, dbs_ref):
                r[...] = jnp.zeros_like(r)

        has_prev, has_next = _end_masks(nt)
        ca, ha, ch, ch_m1, ch_p1, cv = _conv_fwd(p_ref, prev_ref, next_ref, cw_ref, tm, has_prev, has_next)
        ba = _pcols(p_ref, 0, OFF_CA)
        ab_ref[...] = (ba * cv).astype(BF)
        vb_pre, tv, xhv, rstdv = _spatial_fwd(p_ref, vg_ref, vb_ref, ws_ref, bsf_ref, vnb_ref, mixed_ref, tm)
        ub = _pcols(p_ref, OFF_UB, OFF_VB)
        gu, tu = _gelu(ub)
        bbb_ref[...] = (gu * mixed_ref[...]).astype(BF)
        bga = bg_ref[:, 0:D]
        ga = _sigmoid(_pcols(p_ref, OFF_GA, OFF_GB) + bga)
        gb = _sigmoid(_pcols(p_ref, OFF_GB, N_PROJ) + bg_ref[:, D:2 * D])
        ya = ya_ref[...]
        yb = yb_ref[...]
        zb_ref[...] = (ga * ya + gb * yb).astype(BF)

        dr1_ext = jnp.concatenate([dprev_ref[...] * has_prev, dr1_ref[...], dnext_ref[...] * has_next], axis=0)
        dz_ext = _dot_nt(dr1_ext.astype(BF), wo_ref[...])
        ga_ext = jnp.concatenate([_sigmoid(prev_ref[:, OFF_GA:OFF_GB] + bga), ga,
                                  _sigmoid(next_ref[:, OFF_GA:OFF_GB] + bga)], axis=0)
        dya_ext = dz_ext * ga_ext
        dyab_ref[...] = dya_ext[mid].astype(BF)
        da_ext = _dot_nt(dya_ext.astype(BF), wpa_ref[...])
        ba_ext = jnp.concatenate([prev_ref[:, 0:OFF_CA], ba, next_ref[:, 0:OFF_CA]], axis=0)
        dcv_ext = da_ext * ba_ext
        dcv = dcv_ext[mid]
        dch = (cw_ref[0:1, :] * pltpu.roll(dcv_ext, te - 1, 0)[mid] + cw_ref[1:2, :] * dcv
               + cw_ref[2:3, :] * pltpu.roll(dcv_ext, 1, 0)[mid])
        dp_ref[:, 0:OFF_CA] = (da_ext[mid] * cv).astype(BF)
        dp_ref[:, OFF_CA:OFF_HA] = (dch * ha).astype(BF)
        dp_ref[:, OFF_HA:OFF_UB] = (dch * ca).astype(BF)
        dcw_ref[0:1, :] += _colsum(dcv * ch_m1)
        dcw_ref[1:2, :] += _colsum(dcv * ch)
        dcw_ref[2:3, :] += _colsum(dcv * ch_p1)

        dz = dz_ext[mid]
        dga = dz * ya * ga * (1.0 - ga)
        dgb = dz * yb * gb * (1.0 - gb)
        dp_ref[:, OFF_GA:OFF_GB] = dga.astype(BF)
        dp_ref[:, OFF_GB:N_PROJ] = dgb.astype(BF)
        dbg_ref[0:1, 0:D] += _colsum(dga)
        dbg_ref[0:1, D:2 * D] += _colsum(dgb)

        dybb_ref[...] = (dz * gb).astype(BF)
        dbb = _dot_nt(dybb_ref[...], wpb_ref[...])
        dp_ref[:, OFF_UB:OFF_VB] = (dbb * mixed_ref[...] * _gelu_grad(ub, tu)).astype(BF)
        dmixed = dbb * gu
        dmixb_ref[...] = dmixed.astype(BF)
        for c in range(tm // CHUNK):
            rows = slice(c * CHUNK, (c + 1) * CHUNK)
            dbs_ref[...] += dmixed[rows]
            for h in range(N_HEAD):
                cols = slice(h * CHUNK, (h + 1) * CHUNK)
                dws_ref[h] += _dot_nt(dmixb_ref[rows, cols], vnb_ref[rows, cols])
                dvn_ref[rows, cols] = _dot(wst_ref[h], dmixb_ref[rows, cols])
        dvn = dvn_ref[...]
        dvgb_ref[0:1, :] += _colsum(dvn * xhv)
        dvgb_ref[1:2, :] += _colsum(dvn)
        dgv = _ln_bwd(dvn, vg_ref[...], xhv, rstdv)
        dp_ref[:, OFF_VB:OFF_GA] = (dgv * _gelu_grad(vb_pre, tv)).astype(BF)

    tile = lambda w: pl.BlockSpec((tm, w), lambda i: (i, 0))
    acc = lambda *s: pl.BlockSpec(s, lambda i: (0,) * len(s))
    return _host_call(
        body, comm, name="mix_bwd", grid=(nt,),
        in_specs=[tile(N_PROJ), *_edge_specs(t, tm, N_PROJ), tile(D), *_edge_specs(t, tm, D), tile(D), tile(D),
                  _resident((W_A, D)), _resident((W_B, D)), _resident((D, D)), _resident((N_HEAD, CHUNK, CHUNK)),
                  _resident((N_HEAD, CHUNK, CHUNK)), _resident((CHUNK, W_B)), _resident((1, 2 * D)),
                  _resident((3, W_A)), _resident((1, W_B)), _resident((1, W_B))],
        out_specs=[tile(N_PROJ), tile(W_A), tile(W_B), tile(D), tile(D), tile(D),
                   acc(8, 2 * D), acc(8, W_A), acc(8, W_B), acc(N_HEAD, CHUNK, CHUNK), acc(CHUNK, W_B)],
        out_shape=[jax.ShapeDtypeStruct((t, N_PROJ), BF), jax.ShapeDtypeStruct((t, W_A), BF),
                   jax.ShapeDtypeStruct((t, W_B), BF), jax.ShapeDtypeStruct((t, D), BF), jax.ShapeDtypeStruct((t, D), BF),
                   jax.ShapeDtypeStruct((t, D), BF), jax.ShapeDtypeStruct((8, 2 * D), F32),
                   jax.ShapeDtypeStruct((8, W_A), F32), jax.ShapeDtypeStruct((8, W_B), F32),
                   jax.ShapeDtypeStruct((N_HEAD, CHUNK, CHUNK), F32), jax.ShapeDtypeStruct((CHUNK, W_B), F32)],
        scratch_shapes=[pltpu.VMEM((tm, W_B), BF), pltpu.VMEM((tm, W_B), F32), pltpu.VMEM((tm, W_B), BF),
                        pltpu.VMEM((tm, W_B), F32)],
        args=(p, pedge, pedge, dr1, dedge, dedge, ya, yb, wpa, wpb, wo, wsb, wstb, bsf, bg, cw, vg, vb))


def _add_own_half(full4, recv4, place, rb, name):
    n, rh, cols = recv4.shape
    nb = rh // rb

    def body(pl_ref, a_ref, b_ref, own_ref, ob_ref):
        s = a_ref[...] + b_ref[...]
        ob_ref[...] = s.astype(BF)

        @pl.when(pl.program_id(1) == pl_ref[0])
        def _():
            own_ref[...] = s

    blk = (None, rb, cols)
    return _pc(
        body, name=name,
        grid_spec=pltpu.PrefetchScalarGridSpec(
            num_scalar_prefetch=1, grid=(nb, n),
            in_specs=[pl.BlockSpec(blk, lambda i, k, s: (k, s[1] * nb + i, 0)), pl.BlockSpec(blk, lambda i, k, s: (k, i, 0))],
            out_specs=[pl.BlockSpec((rb, cols), lambda i, k, s: (i, 0)), pl.BlockSpec(blk, lambda i, k, s: (k, i, 0))]),
        out_shape=[jax.ShapeDtypeStruct((rh, cols), F32), jax.ShapeDtypeStruct(recv4.shape, BF)],
        compiler_params=_params("arbitrary", "arbitrary"),
    )(place, full4, recv4)


def _add_chips(own, r3, place, rb, name):
    _, rh, cols = r3.shape
    nb = rh // rb

    def body(pl_ref, s_ref, r_ref, o_ref):
        o_ref[...] = ((s_ref[...] + r_ref[0].astype(F32)) + r_ref[1].astype(F32)) + r_ref[2].astype(F32)

    return _pc(
        body, name=name,
        grid_spec=pltpu.PrefetchScalarGridSpec(
            num_scalar_prefetch=1, grid=(nb,),
            in_specs=[pl.BlockSpec((rb, cols), lambda i, s: (i, 0)), pl.BlockSpec((3, rb, cols), lambda i, s: (0, i, 0))],
            out_specs=pl.BlockSpec((rb, cols), lambda i, s: (s[1] * nb + i, 0))),
        out_shape=jax.ShapeDtypeStruct((2 * rh, cols), F32),
        compiler_params=_params("arbitrary"),
    )(place, own, r3)


def _add_small(a, b):
    def body(a_ref, b_ref, o_ref):
        o_ref[...] = a_ref[...] + b_ref[...]

    return _pc(body, name="add_small_cores", out_shape=jax.ShapeDtypeStruct(a.shape, F32))(a, b)


def _sum_small_chips(own, slots, place):
    def body(pl_ref, own_ref, s_ref, o_ref):
        j = pl_ref[0]

        def term(k):
            return jnp.where(j == k, own_ref[...], s_ref[jnp.maximum((j ^ k) - 1, 0)])

        o_ref[...] = ((term(0) + term(1)) + term(2)) + term(3)

    vmem = pl.BlockSpec(memory_space=pltpu.VMEM)
    return _pc(body, name="sum_small_chips", in_specs=[pl.BlockSpec(memory_space=pltpu.SMEM), vmem, vmem], out_specs=vmem,
               out_shape=jax.ShapeDtypeStruct(own.shape, F32))(place, own, slots)


def _adamw_step(w, g, m, v):
    m2 = ADAM_B1 * m + (1.0 - ADAM_B1) * g
    v2 = ADAM_B2 * v + (1.0 - ADAM_B2) * (g * g)
    m_hat = m2 / (1.0 - ADAM_B1 ** ADAM_STEP)
    v_hat = v2 / (1.0 - ADAM_B2 ** ADAM_STEP)
    return -ADAM_LR * (m_hat / (jnp.sqrt(v_hat) + ADAM_EPS) + ADAM_WD * w), m2, v2


def _adamw(w, g, m, v, rb, name):
    rows, cols = w.shape

    def body(w_ref, g_ref, m_ref, v_ref, d_ref, m2_ref, v2_ref):
        d_ref[...], m2_ref[...], v2_ref[...] = _adamw_step(w_ref[...], g_ref[...], m_ref[...], v_ref[...])

    blk = pl.BlockSpec((rb, cols), lambda i: (i, 0))
    return _pc(body, name=name, grid=(rows // rb,), in_specs=[blk] * 4, out_specs=[blk] * 3,
               out_shape=[jax.ShapeDtypeStruct((rows, cols), F32)] * 3, compiler_params=_params("arbitrary"))(w, g, m, v)


def _adamw_small(ws, gs, ms, vs):
    n = len(ws)

    def body(*refs):
        ins, outs = refs[:4 * n], refs[4 * n:]
        for i in range(n):
            outs[i][...], outs[n + i][...], outs[2 * n + i][...] = _adamw_step(*(ins[k * n + i][...] for k in range(4)))

    outs = _pc(body, name="adamw_small", out_shape=[jax.ShapeDtypeStruct(w.shape, F32) for w in ws] * 3)(*ws, *gs, *ms, *vs)
    return outs[:n], outs[n:2 * n], outs[2 * n:]


LANES = 128
SMALL_GRADS = (("b_gate", 2 * D), ("conv_w", 3 * W_A), ("v_norm_g", W_B), ("v_norm_b", W_B),
               ("w_s", N_HEAD * CHUNK * CHUNK), ("b_s", N_HEAD * CHUNK), ("ln1_g", D), ("ln1_b", D), ("ln2_g", D), ("ln2_b", D),
               ("loss", 1))


def _pack_rows(parts):
    rows = []
    for a in parts:
        a = a.reshape(-1)
        a = jnp.pad(a, (0, (-a.shape[0]) % LANES))
        rows.append(a.reshape(-1, LANES))
    out = jnp.concatenate(rows, axis=0)
    return jnp.pad(out, ((0, (-out.shape[0]) % 8), (0, 0)))


def _unpack_rows(buf, sizes):
    out, r = [], 0
    for n in sizes:
        nr = -(-n // LANES)
        out.append(buf[r:r + nr].reshape(-1)[:n])
        r += nr
    return out


TM_PROJ = 1024
TM_MIX = 256
TM_DX = 512
DX_PAIR = 6
TK_DW = 4096
TK_DW_IN = 2048
TK_DW_PROJ = 1024
ADD_BLOCK_BYTES = 3 * 1024 * 1024
RB_ADAM = 128
CONV_ROWS = 8


def _add_rows(rows, cols):
    while rows * cols * 4 > ADD_BLOCK_BYTES and rows % 32 == 0:
        rows //= 2
    return rows


def _reduce_adds_1(grads, recvs, place, tag):
    out = [_add_own_half(g, r, place, _add_rows(*r.shape[1:]), f"add_cores_{tag}{a}") for a, (g, r) in enumerate(zip(grads, recvs))]
    return [o[0] for o in out], [o[1] for o in out]


def _reduce_adds_2(sums, recvs, place, tag):
    return [_add_chips(s, r, place, _add_rows(*r.shape[1:]), f"add_chips_{tag}{a}") for a, (s, r) in enumerate(zip(sums, recvs))]


def kernel(x, w_in, b_gate, conv_w, v_norm_g, v_norm_b, w_s, b_s, w_pa, w_pb, w_o, ln1_g, ln1_b, w_ff1, w_ff2, ln2_g, ln2_b, loss_target, m_w_in, m_b_gate, m_conv_w, m_v_norm_g, m_v_norm_b, m_w_s, m_b_s, m_w_pa, m_w_pb, m_w_o, m_ln1_g, m_ln1_b, m_w_ff1, m_w_ff2, m_ln2_g, m_ln2_b, v_w_in, v_b_gate, v_conv_w, v_v_norm_g, v_v_norm_b, v_w_s, v_b_s, v_w_pa, v_w_pb, v_w_o, v_ln1_g, v_ln1_b, v_w_ff1, v_w_ff2, v_ln2_g, v_ln2_b):
    t = x.shape[1]
    core = lax.axis_index("c").astype(jnp.int32).reshape(1)
    chip_idx = 2 * lax.axis_index("x") + lax.axis_index("y")
    chip = chip_idx.astype(jnp.int32).reshape(1)
    place = jnp.concatenate([chip, core])
    x2 = x.reshape(t, D)
    tgt = loss_target.reshape(t, D)

    win4, proj4, ff14, ff24 = _cast_shards(w_in[0], w_pa[0], w_pb[0], w_o[0], w_ff1[0], w_ff2[0], chip)
    conv4 = lax.dynamic_update_slice(jnp.zeros((N_CHIP, CONV_ROWS, W_A // N_CHIP), F32),
                                     jnp.pad(conv_w[0], ((0, CONV_ROWS - 3), (0, 0)))[None], (chip_idx, 0, 0))
    (p, pedge), (win4, proj4, ff14, ff24, conv4) = _proj_fwd(
        x2, chip, TM_PROJ, _gather_comm([win4, proj4, ff14, ff24], conv4, eager=1))

    def full(name, rows_total):
        off, rows = PROJ_OFF[name]
        return proj4[:, off:off + rows, :].reshape(rows_total, D)

    wpa, wpb, wo = full("w_pa", W_A), full("w_pb", W_B), full("w_o", D)
    wff2 = ff24.reshape(D_FF, D)
    cw = jnp.transpose(conv4[:, :3, :], (1, 0, 2)).reshape(3, W_A)
    wsb = w_s[0].astype(BF)
    wstb = jnp.swapaxes(w_s[0], 1, 2).astype(BF)
    bsf = jnp.repeat(jnp.transpose(b_s[0]), CHUNK, axis=1)

    r1, ya, yb = _mix_fwd(p, pedge, x2, wpa, wpb, wo, wsb, bsf, b_gate, cw, v_norm_g, v_norm_b, TM_MIX)
    dr1, dr1b, dedge, x1b, hidb, dh1b, dr2b, acc = _ffn_fwd_bwd(r1, tgt, ff14, wff2, ln1_g, ln1_b, ln2_g, ln2_b, TM_MIX)
    g_ff = [_dw(x1b, dh1b, N_CHIP, D, FF_SHARD, False, True, TK_DW, "dw_ff1"),
            _dw(hidb, dr2b, N_CHIP, FF_SHARD, D, True, False, TK_DW, "dw_ff2")]
    (dp, ab, bbb, zb, dyab, dybb, dbg, dcw, dvgb, dws, dbs_sum), r_ff = _mix_bwd(
        p, pedge, dr1, dedge, ya, yb, wpa, wpb, wo, wsb, wstb, bsf, b_gate, cw, v_norm_g, v_norm_b, TM_MIX, _pair_comm(g_ff))
    s_ff, sb_ff = _reduce_adds_1(g_ff, r_ff, place, "ff")
    dwin4, c_ff = _dw(x2, dp, N_CHIP, D, NP_SHARD, False, True, TK_DW_IN, "dw_in", _chips_comm(sb_ff))
    f_ff = _reduce_adds_2(s_ff, c_ff, place, "ff")
    dproj4, (g_ff1, g_ff2) = _dw_proj(ab, dyab, bbb, dybb, zb, dr1b, TK_DW_PROJ, _join_comm(f_ff))
    dbs = jnp.transpose(jnp.sum(dbs_sum.reshape(CHUNK, N_HEAD, CHUNK), axis=-1))
    small = _pack_rows([dbg[0], dcw[0:3], dvgb[0], dvgb[1], dws, dbs, acc[0], acc[1], acc[2], acc[3],
                        0.5 * jnp.sum(acc[4]) / D])
    g_rest = [dwin4, dproj4]
    nblk = t // TM_DX
    n_a = max(1, min(DX_PAIR, nblk // 4))
    dx, r_rest = _dx(dp, win4, dr1, TM_DX, 0, n_a, None, "dx_a", _pair_comm(g_rest, small))
    s_rest, sb_rest = _reduce_adds_1(g_rest, r_rest[:2], place, "rest")
    csmall = _add_small(small, r_rest[2])
    dx, c_rest = _dx(dp, win4, dr1, TM_DX, n_a, nblk - n_a, dx, "dx_b", _chips_comm(sb_rest, csmall))
    f_rest = _reduce_adds_2(s_rest, c_rest[:2], place, "rest")
    gsmall = _sum_small_chips(csmall, c_rest[2], place)
    g_in, g_proj = _comm_call(_join_comm(f_rest), "join_rest")

    grads = {"w_in": g_in, "w_ff1": g_ff1, "w_ff2": g_ff2}
    for name, _ in PROJ_ROWS:
        off, rows = PROJ_OFF[name]
        grads[name] = g_proj[off:off + rows, :]
    for (name, n), flat in zip(SMALL_GRADS, _unpack_rows(gsmall, [n for _, n in SMALL_GRADS])):
        grads[name] = flat
    loss = grads.pop("loss").reshape(())
    grads["conv_w"] = lax.dynamic_slice(grads["conv_w"].reshape(3, W_A), (0, chip_idx * (W_A // N_CHIP)), (3, W_A // N_CHIP))

    weights = dict(w_in=w_in, b_gate=b_gate, conv_w=conv_w, v_norm_g=v_norm_g, v_norm_b=v_norm_b, w_s=w_s, b_s=b_s,
                   w_pa=w_pa, w_pb=w_pb, w_o=w_o, ln1_g=ln1_g, ln1_b=ln1_b, w_ff1=w_ff1, w_ff2=w_ff2, ln2_g=ln2_g, ln2_b=ln2_b)
    mom1 = dict(w_in=m_w_in, b_gate=m_b_gate, conv_w=m_conv_w, v_norm_g=m_v_norm_g, v_norm_b=m_v_norm_b, w_s=m_w_s,
                b_s=m_b_s, w_pa=m_w_pa, w_pb=m_w_pb, w_o=m_w_o, ln1_g=m_ln1_g, ln1_b=m_ln1_b, w_ff1=m_w_ff1,
                w_ff2=m_w_ff2, ln2_g=m_ln2_g, ln2_b=m_ln2_b)
    mom2 = dict(w_in=v_w_in, b_gate=v_b_gate, conv_w=v_conv_w, v_norm_g=v_v_norm_g, v_norm_b=v_v_norm_b, w_s=v_w_s,
                b_s=v_b_s, w_pa=v_w_pa, w_pb=v_w_pb, w_o=v_w_o, ln1_g=v_ln1_g, ln1_b=v_ln1_b, w_ff1=v_w_ff1,
                w_ff2=v_w_ff2, ln2_g=v_ln2_g, ln2_b=v_ln2_b)
    order = list(weights)
    big = ("w_in", "w_pa", "w_pb", "w_o", "w_ff1", "w_ff2")
    delta, new_m, new_v = {}, {}, {}
    for name in big:
        w2 = weights[name][0]
        delta[name], new_m[name], new_v[name] = _adamw(w2, grads[name], mom1[name][0], mom2[name][0], RB_ADAM, "adamw_" + name)
    little = [n for n in order if n not in big]
    flat2d = lambda a: a.reshape(-1, a.shape[-1])
    ds, ms, vs = _adamw_small(*([flat2d(d[n].reshape(weights[n].shape)) for n in little] for d in (weights, grads, mom1, mom2)))
    for name, d_, m_, v_ in zip(little, ds, ms, vs):
        delta[name], new_m[name], new_v[name] = d_, m_, v_

    shaped = lambda d: [d[n].reshape(weights[n].shape) for n in order]
    return (loss, dx.reshape(x.shape), *shaped(grads), *shaped(delta), *shaped(new_m), *shaped(new_v))
```

```python
import functools
from typing import NamedTuple

import jax
import jax.numpy as jnp
from jax import lax
from jax.experimental import pallas as pl
from jax.experimental.pallas import tpu as pltpu
from jax.experimental.pallas import tpu_sc as plsc

D = 1024
W_A = 1536
W_B = 1024
CHUNK = 128
N_HEAD = 8
D_FF = 4096
N_PROJ = 3 * W_A + 2 * W_B + 2 * D
OFF_CA, OFF_HA, OFF_UB, OFF_VB, OFF_GA, OFF_GB = 1536, 3072, 4608, 5632, 6656, 7680
LN_EPS = 1e-5
ALPHA = 2.0 ** 0.25
N_CHIP = 4
NP_SHARD = N_PROJ // N_CHIP
FF_SHARD = D_FF // N_CHIP
ADAM_LR, ADAM_B1, ADAM_B2, ADAM_EPS, ADAM_WD, ADAM_STEP = 0.001, 0.9, 0.999, 1e-08, 0.01, 10

PROJ_ROWS = (("w_pa", W_A // N_CHIP), ("w_pb", W_B // N_CHIP), ("w_o", D // N_CHIP))
PROJ_OFF = {}
_o = 0
for _n, _r in PROJ_ROWS:
    PROJ_OFF[_n] = (_o, _r)
    _o += _r
PROJ_TOTAL = _o

V7X_VMEM_BYTES = 64 * 1024 * 1024
VMEM_LIMIT = V7X_VMEM_BYTES - 8 * 1024 * 1024
HALO = 8
EDGE_TILE = 256

BF = jnp.bfloat16
F32 = jnp.float32
MESH = pl.DeviceIdType.MESH
HBM_SPEC = pl.BlockSpec(memory_space=pltpu.HBM)


def _pc(body, **kw):
    return pl.pallas_call(body, **kw)


def _params(*sem):
    return pltpu.CompilerParams(dimension_semantics=sem, vmem_limit_bytes=VMEM_LIMIT)


def _resident(shape):
    n = len(shape)
    return pl.BlockSpec(shape, lambda *_: (0,) * n, pipeline_mode=pl.Buffered(1))


def _dot(a, b):
    return jnp.dot(a, b, preferred_element_type=F32)


def _dot_nt(a, b):
    return lax.dot_general(a, b, (((1,), (1,)), ((), ())), preferred_element_type=F32)


def _dot_tn(a, b):
    return lax.dot_general(a, b, (((0,), (0,)), ((), ())), preferred_element_type=F32)


def _gelu(x):
    t = jnp.tanh(0.7978845608028654 * (x + 0.044715 * (x * x * x)))
    return 0.5 * x * (1.0 + t), t


def _gelu_grad(x, t):
    return 0.5 * (1.0 + t) + 0.5 * x * (1.0 - t * t) * (0.7978845608028654 * (1.0 + 0.134145 * (x * x)))


def _ln_stats(r):
    mu = jnp.mean(r, axis=-1, keepdims=True)
    xc = r - mu
    var = jnp.mean(xc * xc, axis=-1, keepdims=True)
    rstd = lax.rsqrt(var + LN_EPS)
    return xc * rstd, rstd


def _ln_bwd(dy, g, xh, rstd):
    dxh = dy * g
    m1 = jnp.mean(dxh, axis=-1, keepdims=True)
    m2 = jnp.mean(dxh * xh, axis=-1, keepdims=True)
    return rstd * (dxh - m1 - xh * m2)


def _colsum(v):
    return jnp.sum(v, axis=0, keepdims=True)


class _Comm(NamedTuple):
    args: tuple
    out_shape: tuple
    aliases: dict
    n_sems: int
    stages: tuple


def _place():
    x, y, c = lax.axis_index("x"), lax.axis_index("y"), lax.axis_index("c")
    return x, y, c, 2 * x + y


def _flip(x, y, c, r):
    return (x ^ (r >> 1), y ^ (r & 1), c)


def _remote(src, dst, send_sems, recv_sems, k, peer):
    return pltpu.make_async_remote_copy(src_ref=src, dst_ref=dst, send_sem=send_sems.at[k], recv_sem=recv_sems.at[k],
                                        device_id=peer, device_id_type=MESH)


def _host_call(body, comm, *, name, grid, in_specs, out_specs, out_shape, args, scratch_shapes=(), aliases=None, prefetch=None,
               body_reads_comm=False):
    sem = ("arbitrary",) * len(grid)
    aliases = dict(aliases or {})
    n_pre = 0 if prefetch is None else 1
    n_in, n_out, n_scr = len(in_specs), len(out_specs), len(scratch_shapes)
    c_in, c_out = (0, 0) if comm is None else (len(comm.args), len(comm.out_shape))
    steps = {"first": (0,) * len(grid), "late": (grid[0] - 1,) + (0,) * (len(grid) - 1), "last": tuple(g - 1 for g in grid)}

    def wrapped(*refs):
        refs = refs[n_pre:]
        own_in, cin = refs[:n_in], refs[n_in:n_in + c_in]
        o0 = n_in + c_in
        own_out, cout = refs[o0:o0 + n_out], refs[o0 + n_out:o0 + n_out + c_out]
        s0 = o0 + n_out + c_out
        scr, sems = refs[s0:s0 + n_scr], refs[s0 + n_scr:]

        def run(before):
            for phase, fn in () if comm is None else comm.stages:
                at_step = isinstance(phase, tuple)
                if before != (at_step or phase == "first"):
                    continue
                step = phase if at_step else steps[phase]
                cond = pl.program_id(0) == step[0]
                for d in range(1, len(grid)):
                    cond = jnp.logical_and(cond, pl.program_id(d) == step[d])
                pl.when(cond)(functools.partial(fn, cin, cout, *sems))

        run(True)
        if body_reads_comm:
            body(*own_in, *own_out, *scr, comm_refs=cout)
        else:
            body(*own_in, *own_out, *scr)
        run(False)

    in_specs = list(in_specs) + [HBM_SPEC] * c_in
    out_specs = list(out_specs) + [HBM_SPEC] * c_out
    out_shape = list(out_shape) + ([] if comm is None else list(comm.out_shape))
    scratch_shapes = list(scratch_shapes) + ([] if comm is None else [pltpu.SemaphoreType.DMA((comm.n_sems,))] * 2)
    args = tuple(args) + (() if comm is None else tuple(comm.args))
    if comm is not None:
        aliases.update({n_in + i: n_out + o for i, o in comm.aliases.items()})
    aliases = {i + n_pre: o for i, o in aliases.items()}
    if prefetch is None:
        kw = dict(grid=grid, in_specs=in_specs, out_specs=out_specs, scratch_shapes=scratch_shapes)
    else:
        kw = dict(grid_spec=pltpu.PrefetchScalarGridSpec(num_scalar_prefetch=1, grid=grid, in_specs=in_specs,
                                                         out_specs=out_specs, scratch_shapes=scratch_shapes))
        args = (prefetch,) + args
    outs = _pc(wrapped, name=name, out_shape=out_shape, input_output_aliases=aliases, compiler_params=_params(*sem), **kw)(*args)
    return outs[:n_out], outs[n_out:]


def _comm_call(comm, name):
    def body(*refs):
        c_in, c_out = len(comm.args), len(comm.out_shape)
        cin, cout, (send_sems, recv_sems) = refs[:c_in], refs[c_in:c_in + c_out], refs[c_in + c_out:]
        for phase in ("first", "late", "last"):
            for ph, fn in comm.stages:
                if ph == phase:
                    fn(cin, cout, send_sems, recv_sems)

    return _pc(body, name=name, in_specs=[HBM_SPEC] * len(comm.args), out_specs=[HBM_SPEC] * len(comm.out_shape),
               out_shape=list(comm.out_shape), scratch_shapes=[pltpu.SemaphoreType.DMA((comm.n_sems,))] * 2,
               input_output_aliases=dict(comm.aliases))(*comm.args)


def _gather_comm(bufs, whole=None, eager=0):
    n = len(bufs)
    halves = [b.shape[1] // 2 for b in bufs]
    k_ici = lambda a, r: 3 * a + r - 1
    k_d2d = lambda a, r: 3 * n + 3 * a + r - 1
    k_whole = lambda r: 6 * n + r - 1

    def half(ref, slot, c, a):
        return ref.at[slot, pl.ds(c * halves[a], halves[a])]

    def send(cin, cout, ss, rs):
        x, y, c, j = _place()
        for a in range(n):
            mine = half(cout[a], j, c, a)
            for r in (1, 2, 3):
                _remote(mine, mine, ss, rs, k_ici(a, r), _flip(x, y, c, r)).start()
        if whole is not None:
            for r in (1, 2, 3):
                _remote(cout[n].at[j], cout[n].at[j], ss, rs, k_whole(r), _flip(x, y, c, r)).start()

    def pass_on(cout, ss, rs, a, r):
        x, y, c, j = _place()
        landed = half(cout[a], j ^ r, c, a)
        _remote(landed, landed, ss, rs, k_ici(a, r), (x, y, 1 - c)).wait_recv()
        _remote(landed, landed, ss, rs, k_d2d(a, r), (x, y, 1 - c)).start()

    def passed_on(cout, ss, rs, a, r):
        x, y, c, j = _place()
        theirs = half(cout[a], j ^ r, 1 - c, a)
        _remote(theirs, theirs, ss, rs, k_d2d(a, r), (x, y, 1 - c)).wait_recv()

    def arrive(r, cin, cout, ss, rs):
        for a in range(eager):
            pass_on(cout, ss, rs, a, r)
        for a in range(eager):
            passed_on(cout, ss, rs, a, r)

    def forward(cin, cout, ss, rs):
        for a in range(eager, n):
            for r in (1, 2, 3):
                pass_on(cout, ss, rs, a, r)

    def finish(cin, cout, ss, rs):
        x, y, c, j = _place()
        sibling = (x, y, 1 - c)
        for a in range(eager, n):
            for r in (1, 2, 3):
                passed_on(cout, ss, rs, a, r)
        for a in range(n):
            mine = half(cout[a], j, c, a)
            for r in (1, 2, 3):
                _remote(mine, mine, ss, rs, k_ici(a, r), sibling).wait_send()
                landed = half(cout[a], j ^ r, c, a)
                _remote(landed, landed, ss, rs, k_d2d(a, r), sibling).wait_send()
        if whole is not None:
            for r in (1, 2, 3):
                cp = _remote(cout[n].at[j ^ r], cout[n].at[j ^ r], ss, rs, k_whole(r), sibling)
                cp.wait_recv()
                cp.wait_send()

    args = tuple(bufs) + ((whole,) if whole is not None else ())
    out_shape = tuple(jax.ShapeDtypeStruct(b.shape, b.dtype) for b in args)
    aliases = {a: a for a in range(len(args))}
    arrivals = tuple(((r, 0), functools.partial(arrive, r)) for r in (1, 2, 3)) if eager else ()
    return _Comm(args, out_shape, aliases, 6 * n + 3, (("first", send),) + arrivals + (("late", forward), ("last", finish)))


def _pair_comm(grads, small=None):
    n = len(grads)
    halves = [g.shape[1] // 2 for g in grads]

    def copies(cin, cout, ss, rs):
        x, y, c, _ = _place()
        sibling = (x, y, 1 - c)
        cps = [_remote(cin[a].at[:, pl.ds((1 - c) * halves[a], halves[a]), :], cout[a], ss, rs, a, sibling) for a in range(n)]
        if small is not None:
            cps.append(_remote(cin[n], cout[n], ss, rs, n, sibling))
        return cps

    def start(cin, cout, ss, rs):
        for cp in copies(cin, cout, ss, rs):
            cp.start()

    def finish(cin, cout, ss, rs):
        for cp in copies(cin, cout, ss, rs):
            cp.wait()

    args = tuple(grads) + ((small,) if small is not None else ())
    out_shape = tuple(jax.ShapeDtypeStruct((N_CHIP, h, g.shape[2]), F32) for g, h in zip(grads, halves))
    out_shape += (jax.ShapeDtypeStruct(small.shape, F32),) if small is not None else ()
    return _Comm(args, out_shape, {}, n + 1, (("first", start), ("last", finish)))


def _chips_comm(sums_bf, small=None):
    n = len(sums_bf)

    def copies(cin, cout, ss, rs):
        x, y, c, j = _place()
        cps = []
        for r in (1, 2, 3):
            peer = _flip(x, y, c, r)
            for a in range(n):
                cps.append(_remote(cin[a].at[j ^ r], cout[a].at[r - 1], ss, rs, (n + 1) * (r - 1) + a, peer))
            if small is not None:
                cps.append(_remote(cin[n], cout[n].at[r - 1], ss, rs, (n + 1) * (r - 1) + n, peer))
        return cps

    def start(cin, cout, ss, rs):
        for cp in copies(cin, cout, ss, rs):
            cp.start()

    def finish(cin, cout, ss, rs):
        for cp in copies(cin, cout, ss, rs):
            cp.wait()

    args = tuple(sums_bf) + ((small,) if small is not None else ())
    out_shape = tuple(jax.ShapeDtypeStruct((3,) + s.shape[1:], BF) for s in sums_bf)
    out_shape += (jax.ShapeDtypeStruct((3,) + small.shape, F32),) if small is not None else ()
    return _Comm(args, out_shape, {}, 3 * (n + 1), (("first", start), ("last", finish)))


def _join_comm(shards):
    n = len(shards)
    halves = [s.shape[0] // 2 for s in shards]

    def start(cin, cout, ss, rs):
        x, y, c, _ = _place()
        for a in range(n):
            mine = cout[a].at[pl.ds(c * halves[a], halves[a]), :]
            _remote(mine, mine, ss, rs, a, (x, y, 1 - c)).start()

    def finish(cin, cout, ss, rs):
        x, y, c, _ = _place()
        for a in range(n):
            theirs = cout[a].at[pl.ds((1 - c) * halves[a], halves[a]), :]
            cp = _remote(theirs, theirs, ss, rs, a, (x, y, 1 - c))
            cp.wait_recv()
            cp.wait_send()

    out_shape = tuple(jax.ShapeDtypeStruct(s.shape, F32) for s in shards)
    return _Comm(tuple(shards), out_shape, {a: a for a in range(n)}, n, (("first", start), ("last", finish)))


def _cast_shards(w_in, w_pa, w_pb, w_o, w_ff1, w_ff2, chip):
    def body(j_ref, win_ref, wpa_ref, wpb_ref, wo_ref, wff1_ref, wff2_ref, win4_ref, proj4_ref, ff14_ref, ff24_ref):
        win4_ref[...] = win_ref[...].astype(BF)
        for name, ref in (("w_pa", wpa_ref), ("w_pb", wpb_ref), ("w_o", wo_ref)):
            off, rows = PROJ_OFF[name]
            proj4_ref[off:off + rows, :] = ref[...].astype(BF)
        ff14_ref[...] = wff1_ref[...].astype(BF)
        ff24_ref[...] = wff2_ref[...].astype(BF)

    whole = lambda a: pl.BlockSpec(a.shape, lambda i, j: (0, 0), pipeline_mode=pl.Buffered(1))
    slot = lambda rows, cols: pl.BlockSpec((None, rows, cols), lambda i, j: (j[0], 0, 0))
    ws = (w_in, w_pa, w_pb, w_o, w_ff1, w_ff2)
    shapes = ((D, NP_SHARD), (PROJ_TOTAL, D), (D, FF_SHARD), (FF_SHARD, D))
    return _pc(
        body, name="cast_shards",
        grid_spec=pltpu.PrefetchScalarGridSpec(num_scalar_prefetch=1, grid=(1,), in_specs=[whole(w) for w in ws],
                                               out_specs=[slot(*s) for s in shapes]),
        out_shape=[jax.ShapeDtypeStruct((N_CHIP,) + s, BF) for s in shapes],
        compiler_params=_params("arbitrary"))(chip, *ws)


def _proj_fwd(x, chip, tm, comm):
    t = x.shape[0]
    sub = tm // EDGE_TILE

    def body(x_ref, p_ref, edge_ref, w_ref, w_sem, comm_refs):
        @pl.when(pl.program_id(1) == 0)
        def _():
            _, _, _, j = _place()
            block = pltpu.make_async_copy(comm_refs[0].at[j ^ pl.program_id(0)], w_ref, w_sem)
            block.start()
            block.wait()

        p_ref[...] = _dot(x_ref[...].astype(BF), w_ref[...])
        _write_edges(edge_ref, p_ref, tm)

    return _host_call(
        body, comm, name="proj_fwd", grid=(N_CHIP, t // tm), prefetch=chip, body_reads_comm=True,
        in_specs=[pl.BlockSpec((tm, D), lambda r, i, j: (i, 0))],
        out_specs=[pl.BlockSpec((tm, NP_SHARD), lambda r, i, j: (i, j[0] ^ r)),
                   pl.BlockSpec((sub, 2 * HALO, NP_SHARD), lambda r, i, j: (i, 0, j[0] ^ r))],
        out_shape=[jax.ShapeDtypeStruct((t, N_PROJ), F32), jax.ShapeDtypeStruct((t // EDGE_TILE, 2 * HALO, N_PROJ), F32)],
        scratch_shapes=[pltpu.VMEM((D, NP_SHARD), BF), pltpu.SemaphoreType.DMA],
        args=(x,))


def _edge_specs(t, tm, w):
    k, last = tm // EDGE_TILE, t // EDGE_TILE - 1
    return [pl.BlockSpec((None, HALO, w), lambda i: (jnp.maximum(i * k - 1, 0), 1, 0)),
            pl.BlockSpec((None, HALO, w), lambda i: (jnp.minimum((i + 1) * k, last), 0, 0))]


def _write_edges(edge_ref, rows, tm):
    for s in range(tm // EDGE_TILE):
        edge_ref[s, 0:HALO, :] = rows[s * EDGE_TILE:s * EDGE_TILE + HALO, :]
        edge_ref[s, HALO:2 * HALO, :] = rows[(s + 1) * EDGE_TILE - HALO:(s + 1) * EDGE_TILE, :]


def _pcols(p_ref, lo, hi):
    return p_ref[:, lo:hi]


def _end_masks(nt):
    i = pl.program_id(0)
    return (i > 0).astype(F32), (i < nt - 1).astype(F32)


def _conv_fwd(p_ref, prev_ref, next_ref, cw_ref, tm, has_prev, has_next):
    ca = _pcols(p_ref, OFF_CA, OFF_HA)
    ha = _pcols(p_ref, OFF_HA, OFF_UB)
    ch = ca * ha
    ch_prev = prev_ref[HALO - 1:HALO, OFF_CA:OFF_HA] * prev_ref[HALO - 1:HALO, OFF_HA:OFF_UB] * has_prev
    ch_next = next_ref[0:1, OFF_CA:OFF_HA] * next_ref[0:1, OFF_HA:OFF_UB] * has_next
    row = lax.broadcasted_iota(jnp.int32, (tm, W_A), 0)
    ch_m1 = jnp.where(row == 0, ch_prev, pltpu.roll(ch, 1, 0))
    ch_p1 = jnp.where(row == tm - 1, ch_next, pltpu.roll(ch, tm - 1, 0))
    cv = cw_ref[0:1, :] * ch_m1 + cw_ref[1:2, :] * ch + cw_ref[2:3, :] * ch_p1
    return ca, ha, ch, ch_m1, ch_p1, cv


def _spatial_fwd(p_ref, vg_ref, vb_ref, ws_ref, bsf_ref, vnb_ref, mixed_ref, tm):
    vb_pre = _pcols(p_ref, OFF_VB, OFF_GA)
    gv, tv = _gelu(vb_pre)
    xhv, rstdv = _ln_stats(gv)
    vnb_ref[...] = (xhv * vg_ref[...] + vb_ref[...]).astype(BF)
    for c in range(tm // CHUNK):
        rows = slice(c * CHUNK, (c + 1) * CHUNK)
        for h in range(N_HEAD):
            cols = slice(h * CHUNK, (h + 1) * CHUNK)
            mixed_ref[rows, cols] = _dot(ws_ref[h], vnb_ref[rows, cols]) + bsf_ref[:, cols]
    return vb_pre, tv, xhv, rstdv


def _mix_fwd(p, pedge, x, wpa, wpb, wo, wsb, bsf, bg, cw, vg, vb, tm):
    t = x.shape[0]
    nt = t // tm

    def body(p_ref, prev_ref, next_ref, x_ref, wpa_ref, wpb_ref, wo_ref, ws_ref, bsf_ref, bg_ref, cw_ref, vg_ref, vb_ref,
             r1_ref, ya_ref, yb_ref, vnb_ref, mixed_ref):
        has_prev, has_next = _end_masks(nt)
        _, _, _, _, _, cv = _conv_fwd(p_ref, prev_ref, next_ref, cw_ref, tm, has_prev, has_next)
        a = _pcols(p_ref, 0, OFF_CA) * cv
        ya = _dot(a.astype(BF), wpa_ref[...])
        ya_ref[...] = ya
        _spatial_fwd(p_ref, vg_ref, vb_ref, ws_ref, bsf_ref, vnb_ref, mixed_ref, tm)
        gu, _ = _gelu(_pcols(p_ref, OFF_UB, OFF_VB))
        bb = gu * mixed_ref[...]
        yb = _dot(bb.astype(BF), wpb_ref[...])
        yb_ref[...] = yb
        ga = jax.nn.sigmoid(_pcols(p_ref, OFF_GA, OFF_GB) + bg_ref[:, 0:D])
        gb = jax.nn.sigmoid(_pcols(p_ref, OFF_GB, N_PROJ) + bg_ref[:, D:2 * D])
        z = ga * ya + gb * yb
        r1_ref[...] = ALPHA * x_ref[...] + _dot(z.astype(BF), wo_ref[...])

    tile = lambda w: pl.BlockSpec((tm, w), lambda i: (i, 0))
    return _pc(
        body, name="mix_fwd", grid=(nt,),
        in_specs=[tile(N_PROJ), *_edge_specs(t, tm, N_PROJ), tile(D),
                  _resident((W_A, D)), _resident((W_B, D)), _resident((D, D)), _resident((N_HEAD, CHUNK, CHUNK)),
                  _resident((CHUNK, W_B)), _resident((1, 2 * D)), _resident((3, W_A)), _resident((1, W_B)),
                  _resident((1, W_B))],
        out_specs=[tile(D), tile(D), tile(D)],
        out_shape=[jax.ShapeDtypeStruct((t, D), F32)] * 3,
        scratch_shapes=[pltpu.VMEM((tm, W_B), BF), pltpu.VMEM((tm, W_B), F32)],
        compiler_params=_params("arbitrary"),
    )(p, pedge, pedge, x, wpa, wpb, wo, wsb, bsf, bg, cw, vg, vb)


def _ffn_fwd_bwd(r1, tgt, wff1, wff2, ln1g, ln1b, ln2g, ln2b, tm):
    t = r1.shape[0]

    def body(r1_ref, tgt_ref, w1_ref, w2_ref, g1_ref, b1_ref, g2_ref, b2_ref,
             dr1_ref, dr1b_ref, dedge_ref, x1b_ref, hidb_ref, dh1b_ref, dr2b_ref, acc_ref, relu_ref):
        @pl.when(pl.program_id(0) == 0)
        def _():
            acc_ref[...] = jnp.zeros_like(acc_ref)

        xh1, rstd1 = _ln_stats(r1_ref[...])
        x1 = xh1 * g1_ref[...] + b1_ref[...]
        x1b_ref[...] = x1.astype(BF)
        ffn = jnp.zeros((tm, D), F32)
        for j in range(N_CHIP):
            cols = slice(j * FF_SHARD, (j + 1) * FF_SHARD)
            r = jnp.maximum(_dot(x1b_ref[...], w1_ref[j]), 0.0)
            relu_ref[:, cols] = r
            hidb_ref[:, cols] = (r * r).astype(BF)
            ffn = ffn + _dot(hidb_ref[:, cols], w2_ref[cols, :])
        xh2, rstd2 = _ln_stats(ALPHA * x1 + ffn)
        diff = xh2 * g2_ref[...] + b2_ref[...] - tgt_ref[...]
        acc_ref[4:5, :] += _colsum(diff * diff)
        dx2 = diff * (1.0 / D)
        acc_ref[2:3, :] += _colsum(dx2 * xh2)
        acc_ref[3:4, :] += _colsum(dx2)
        dr2 = _ln_bwd(dx2, g2_ref[...], xh2, rstd2)
        dr2b_ref[...] = dr2.astype(BF)
        dx1 = ALPHA * dr2
        for j in range(N_CHIP):
            cols = slice(j * FF_SHARD, (j + 1) * FF_SHARD)
            dhid = _dot_nt(dr2b_ref[...], w2_ref[cols, :])
            dh1b_ref[:, cols] = (dhid * (2.0 * relu_ref[:, cols])).astype(BF)
            dx1 = dx1 + _dot_nt(dh1b_ref[:, cols], w1_ref[j])
        acc_ref[0:1, :] += _colsum(dx1 * xh1)
        acc_ref[1:2, :] += _colsum(dx1)
        dr1 = _ln_bwd(dx1, g1_ref[...], xh1, rstd1)
        dr1_ref[...] = dr1
        dr1b_ref[...] = dr1.astype(BF)
        _write_edges(dedge_ref, dr1_ref, tm)

    tile = lambda w: pl.BlockSpec((tm, w), lambda i: (i, 0))
    vec = _resident((1, D))
    return _pc(
        body, name="ffn_fwd_bwd", grid=(t // tm,),
        in_specs=[tile(D), tile(D), _resident((N_CHIP, D, FF_SHARD)), _resident((D_FF, D)), vec, vec, vec, vec],
        out_specs=[tile(D), tile(D), pl.BlockSpec((tm // EDGE_TILE, 2 * HALO, D), lambda i: (i, 0, 0)), tile(D), tile(D_FF),
                   tile(D_FF), tile(D), pl.BlockSpec((8, D), lambda i: (0, 0))],
        out_shape=[jax.ShapeDtypeStruct((t, D), F32), jax.ShapeDtypeStruct((t, D), BF),
                   jax.ShapeDtypeStruct((t // EDGE_TILE, 2 * HALO, D), F32), jax.ShapeDtypeStruct((t, D), BF),
                   jax.ShapeDtypeStruct((t, D_FF), BF), jax.ShapeDtypeStruct((t, D_FF), BF),
                   jax.ShapeDtypeStruct((t, D), BF), jax.ShapeDtypeStruct((8, D), F32)],
        scratch_shapes=[pltpu.VMEM((tm, D_FF), F32)],
        compiler_params=_params("arbitrary"),
    )(r1, tgt, wff1, wff2, ln1g, ln1b, ln2g, ln2b)


def _dw(a, b, nblk, am, bn, a_blocked, b_blocked, tk, name, comm=None):
    t = a.shape[0]

    def body(a_ref, b_ref, o_ref):
        @pl.when(pl.program_id(1) == 0)
        def _():
            o_ref[...] = jnp.zeros_like(o_ref)

        o_ref[...] += _dot_tn(a_ref[...].astype(BF), b_ref[...])

    outs, got = _host_call(
        body, comm, name=name, grid=(nblk, t // tk),
        in_specs=[pl.BlockSpec((tk, am), (lambda j, k: (k, j)) if a_blocked else (lambda j, k: (k, 0))),
                  pl.BlockSpec((tk, bn), (lambda j, k: (k, j)) if b_blocked else (lambda j, k: (k, 0)))],
        out_specs=[pl.BlockSpec((None, am, bn), lambda j, k: (j, 0, 0))],
        out_shape=[jax.ShapeDtypeStruct((nblk, am, bn), F32)], args=(a, b))
    return outs[0] if comm is None else (outs[0], got)


def _dw_proj(ab, dyab, bbb, dybb, zb, dr1b, tk, comm):
    t = ab.shape[0]
    pairs = (("w_pa", 0, 1), ("w_pb", 2, 3), ("w_o", 4, 5))

    def body(*refs):
        o_ref = refs[6]

        @pl.when(pl.program_id(0) == 0)
        def _():
            o_ref[...] = jnp.zeros_like(o_ref)

        for name, ia, ib in pairs:
            off, rows = PROJ_OFF[name]
            for k in range(N_CHIP):
                o_ref[k, off:off + rows, :] += _dot_tn(refs[ia][:, k * rows:(k + 1) * rows], refs[ib][...])

    tile = lambda w: pl.BlockSpec((tk, w), lambda i: (i, 0))
    outs, got = _host_call(
        body, comm, name="dw_proj", grid=(t // tk,), in_specs=[tile(W_A), tile(D), tile(W_B), tile(D), tile(D), tile(D)],
        out_specs=[pl.BlockSpec((N_CHIP, PROJ_TOTAL, D), lambda i: (0, 0, 0))],
        out_shape=[jax.ShapeDtypeStruct((N_CHIP, PROJ_TOTAL, D), F32)], args=(ab, dyab, bbb, dybb, zb, dr1b))
    return outs[0], got


def _dx(dp, win4, dr1, tm, blk0, nblk, filled, name, comm):
    t = dp.shape[0]

    def body(dp_ref, w_ref, dr1_ref, *rest):
        dx = ALPHA * dr1_ref[...]
        for j in range(N_CHIP):
            dx = dx + _dot_nt(dp_ref[:, j * NP_SHARD:(j + 1) * NP_SHARD], w_ref[j])
        rest[-1][...] = dx

    in_specs = [pl.BlockSpec((tm, N_PROJ), lambda i: (i + blk0, 0)), _resident((N_CHIP, D, NP_SHARD)),
                pl.BlockSpec((tm, D), lambda i: (i + blk0, 0))]
    args = (dp, win4, dr1)
    aliases = None
    if filled is not None:
        in_specs.append(pl.BlockSpec(memory_space=pl.ANY))
        args += (filled,)
        aliases = {3: 0}
    outs, got = _host_call(
        body, comm, name=name, grid=(nblk,), in_specs=in_specs, out_specs=[pl.BlockSpec((tm, D), lambda i: (i + blk0, 0))],
        out_shape=[jax.ShapeDtypeStruct((t, D), F32)], args=args, aliases=aliases)
    return outs[0], got


def _mix_bwd(p, pedge, dr1, dedge, ya, yb, wpa, wpb, wo, wsb, wstb, bsf, bg, cw, vg, vb, tm, comm):
    t = p.shape[0]
    nt = t // tm
    te = tm + 2 * HALO
    mid = slice(HALO, HALO + tm)

    def body(p_ref, prev_ref, next_ref, dr1_ref, dprev_ref, dnext_ref, ya_ref, yb_ref, wpa_ref, wpb_ref, wo_ref,
             ws_ref, wst_ref, bsf_ref, bg_ref, cw_ref, vg_ref, vb_ref,
             dp_ref, ab_ref, bbb_ref, zb_ref, dyab_ref, dybb_ref, dbg_ref, dcw_ref, dvgb_ref, dws_ref, dbs_ref,
             vnb_ref, mixed_ref, dmixb_ref, dvn_ref):
        @pl.when(pl.program_id(0) == 0)
        def _():
            for r in (dbg_ref, dcw_ref, dvgb_ref, dws_ref, dbs_ref):
                r[...] = jnp.zeros_like(r)

        has_prev, has_next = _end_masks(nt)
        ca, ha, ch, ch_m1, ch_p1, cv = _conv_fwd(p_ref, prev_ref, next_ref, cw_ref, tm, has_prev, has_next)
        ba = _pcols(p_ref, 0, OFF_CA)
        ab_ref[...] = (ba * cv).astype(BF)
        vb_pre, tv, xhv, rstdv = _spatial_fwd(p_ref, vg_ref, vb_ref, ws_ref, bsf_ref, vnb_ref, mixed_ref, tm)
        ub = _pcols(p_ref, OFF_UB, OFF_VB)
        gu, tu = _gelu(ub)
        bbb_ref[...] = (gu * mixed_ref[...]).astype(BF)
        bga = bg_ref[:, 0:D]
        ga = jax.nn.sigmoid(_pcols(p_ref, OFF_GA, OFF_GB) + bga)
        gb = jax.nn.sigmoid(_pcols(p_ref, OFF_GB, N_PROJ) + bg_ref[:, D:2 * D])
        ya = ya_ref[...]
        yb = yb_ref[...]
        zb_ref[...] = (ga * ya + gb * yb).astype(BF)

        dr1_ext = jnp.concatenate([dprev_ref[...] * has_prev, dr1_ref[...], dnext_ref[...] * has_next], axis=0)
        dz_ext = _dot_nt(dr1_ext.astype(BF), wo_ref[...])
        ga_ext = jnp.concatenate([jax.nn.sigmoid(prev_ref[:, OFF_GA:OFF_GB] + bga), ga,
                                  jax.nn.sigmoid(next_ref[:, OFF_GA:OFF_GB] + bga)], axis=0)
        dya_ext = dz_ext * ga_ext
        dyab_ref[...] = dya_ext[mid].astype(BF)
        da_ext = _dot_nt(dya_ext.astype(BF), wpa_ref[...])
        ba_ext = jnp.concatenate([prev_ref[:, 0:OFF_CA], ba, next_ref[:, 0:OFF_CA]], axis=0)
        dcv_ext = da_ext * ba_ext
        dcv = dcv_ext[mid]
        dch = (cw_ref[0:1, :] * pltpu.roll(dcv_ext, te - 1, 0)[mid] + cw_ref[1:2, :] * dcv
               + cw_ref[2:3, :] * pltpu.roll(dcv_ext, 1, 0)[mid])
        dp_ref[:, 0:OFF_CA] = (da_ext[mid] * cv).astype(BF)
        dp_ref[:, OFF_CA:OFF_HA] = (dch * ha).astype(BF)
        dp_ref[:, OFF_HA:OFF_UB] = (dch * ca).astype(BF)
        dcw_ref[0:1, :] += _colsum(dcv * ch_m1)
        dcw_ref[1:2, :] += _colsum(dcv * ch)
        dcw_ref[2:3, :] += _colsum(dcv * ch_p1)

        dz = dz_ext[mid]
        dga = dz * ya * ga * (1.0 - ga)
        dgb = dz * yb * gb * (1.0 - gb)
        dp_ref[:, OFF_GA:OFF_GB] = dga.astype(BF)
        dp_ref[:, OFF_GB:N_PROJ] = dgb.astype(BF)
        dbg_ref[0:1, 0:D] += _colsum(dga)
        dbg_ref[0:1, D:2 * D] += _colsum(dgb)

        dybb_ref[...] = (dz * gb).astype(BF)
        dbb = _dot_nt(dybb_ref[...], wpb_ref[...])
        dp_ref[:, OFF_UB:OFF_VB] = (dbb * mixed_ref[...] * _gelu_grad(ub, tu)).astype(BF)
        dmixed = dbb * gu
        dmixb_ref[...] = dmixed.astype(BF)
        for c in range(tm // CHUNK):
            rows = slice(c * CHUNK, (c + 1) * CHUNK)
            dbs_ref[...] += dmixed[rows]
            for h in range(N_HEAD):
                cols = slice(h * CHUNK, (h + 1) * CHUNK)
                dws_ref[h] += _dot_nt(dmixb_ref[rows, cols], vnb_ref[rows, cols])
                dvn_ref[rows, cols] = _dot(wst_ref[h], dmixb_ref[rows, cols])
        dvn = dvn_ref[...]
        dvgb_ref[0:1, :] += _colsum(dvn * xhv)
        dvgb_ref[1:2, :] += _colsum(dvn)
        dgv = _ln_bwd(dvn, vg_ref[...], xhv, rstdv)
        dp_ref[:, OFF_VB:OFF_GA] = (dgv * _gelu_grad(vb_pre, tv)).astype(BF)

    tile = lambda w: pl.BlockSpec((tm, w), lambda i: (i, 0))
    acc = lambda *s: pl.BlockSpec(s, lambda i: (0,) * len(s))
    return _host_call(
        body, comm, name="mix_bwd", grid=(nt,),
        in_specs=[tile(N_PROJ), *_edge_specs(t, tm, N_PROJ), tile(D), *_edge_specs(t, tm, D), tile(D), tile(D),
                  _resident((W_A, D)), _resident((W_B, D)), _resident((D, D)), _resident((N_HEAD, CHUNK, CHUNK)),
                  _resident((N_HEAD, CHUNK, CHUNK)), _resident((CHUNK, W_B)), _resident((1, 2 * D)),
                  _resident((3, W_A)), _resident((1, W_B)), _resident((1, W_B))],
        out_specs=[tile(N_PROJ), tile(W_A), tile(W_B), tile(D), tile(D), tile(D),
                   acc(8, 2 * D), acc(8, W_A), acc(8, W_B), acc(N_HEAD, CHUNK, CHUNK), acc(CHUNK, W_B)],
        out_shape=[jax.ShapeDtypeStruct((t, N_PROJ), BF), jax.ShapeDtypeStruct((t, W_A), BF),
                   jax.ShapeDtypeStruct((t, W_B), BF), jax.ShapeDtypeStruct((t, D), BF), jax.ShapeDtypeStruct((t, D), BF),
                   jax.ShapeDtypeStruct((t, D), BF), jax.ShapeDtypeStruct((8, 2 * D), F32),
                   jax.ShapeDtypeStruct((8, W_A), F32), jax.ShapeDtypeStruct((8, W_B), F32),
                   jax.ShapeDtypeStruct((N_HEAD, CHUNK, CHUNK), F32), jax.ShapeDtypeStruct((CHUNK, W_B), F32)],
        scratch_shapes=[pltpu.VMEM((tm, W_B), BF), pltpu.VMEM((tm, W_B), F32), pltpu.VMEM((tm, W_B), BF),
                        pltpu.VMEM((tm, W_B), F32)],
        args=(p, pedge, pedge, dr1, dedge, dedge, ya, yb, wpa, wpb, wo, wsb, wstb, bsf, bg, cw, vg, vb))


def _add_own_half(full4, recv4, place, rb, name):
    n, rh, cols = recv4.shape
    nb = rh // rb

    def body(pl_ref, a_ref, b_ref, own_ref, ob_ref):
        s = a_ref[...] + b_ref[...]
        ob_ref[...] = s.astype(BF)

        @pl.when(pl.program_id(1) == pl_ref[0])
        def _():
            own_ref[...] = s

    blk = (None, rb, cols)
    return _pc(
        body, name=name,
        grid_spec=pltpu.PrefetchScalarGridSpec(
            num_scalar_prefetch=1, grid=(nb, n),
            in_specs=[pl.BlockSpec(blk, lambda i, k, s: (k, s[1] * nb + i, 0)), pl.BlockSpec(blk, lambda i, k, s: (k, i, 0))],
            out_specs=[pl.BlockSpec((rb, cols), lambda i, k, s: (i, 0)), pl.BlockSpec(blk, lambda i, k, s: (k, i, 0))]),
        out_shape=[jax.ShapeDtypeStruct((rh, cols), F32), jax.ShapeDtypeStruct(recv4.shape, BF)],
        compiler_params=_params("arbitrary", "arbitrary"),
    )(place, full4, recv4)


def _add_chips(own, r3, place, rb, name):
    _, rh, cols = r3.shape
    nb = rh // rb

    def body(pl_ref, s_ref, r_ref, o_ref):
        o_ref[...] = ((s_ref[...] + r_ref[0].astype(F32)) + r_ref[1].astype(F32)) + r_ref[2].astype(F32)

    return _pc(
        body, name=name,
        grid_spec=pltpu.PrefetchScalarGridSpec(
            num_scalar_prefetch=1, grid=(nb,),
            in_specs=[pl.BlockSpec((rb, cols), lambda i, s: (i, 0)), pl.BlockSpec((3, rb, cols), lambda i, s: (0, i, 0))],
            out_specs=pl.BlockSpec((rb, cols), lambda i, s: (s[1] * nb + i, 0))),
        out_shape=jax.ShapeDtypeStruct((2 * rh, cols), F32),
        compiler_params=_params("arbitrary"),
    )(place, own, r3)


def _add_small(a, b):
    def body(a_ref, b_ref, o_ref):
        o_ref[...] = a_ref[...] + b_ref[...]

    return _pc(body, name="add_small_cores", out_shape=jax.ShapeDtypeStruct(a.shape, F32))(a, b)


def _sum_small_chips(own, slots, place):
    def body(pl_ref, own_ref, s_ref, o_ref):
        j = pl_ref[0]

        def term(k):
            return jnp.where(j == k, own_ref[...], s_ref[jnp.maximum((j ^ k) - 1, 0)])

        o_ref[...] = ((term(0) + term(1)) + term(2)) + term(3)

    vmem = pl.BlockSpec(memory_space=pltpu.VMEM)
    return _pc(body, name="sum_small_chips", in_specs=[pl.BlockSpec(memory_space=pltpu.SMEM), vmem, vmem], out_specs=vmem,
               out_shape=jax.ShapeDtypeStruct(own.shape, F32))(place, own, slots)


def _adamw_step(w, g, m, v):
    m2 = ADAM_B1 * m + (1.0 - ADAM_B1) * g
    v2 = ADAM_B2 * v + (1.0 - ADAM_B2) * (g * g)
    m_hat = m2 / (1.0 - ADAM_B1 ** ADAM_STEP)
    v_hat = v2 / (1.0 - ADAM_B2 ** ADAM_STEP)
    return -ADAM_LR * (m_hat / (jnp.sqrt(v_hat) + ADAM_EPS) + ADAM_WD * w), m2, v2


def _adamw(w, g, m, v, rb, name):
    rows, cols = w.shape

    def body(w_ref, g_ref, m_ref, v_ref, d_ref, m2_ref, v2_ref):
        d_ref[...], m2_ref[...], v2_ref[...] = _adamw_step(w_ref[...], g_ref[...], m_ref[...], v_ref[...])

    blk = pl.BlockSpec((rb, cols), lambda i: (i, 0))
    return _pc(body, name=name, grid=(rows // rb,), in_specs=[blk] * 4, out_specs=[blk] * 3,
               out_shape=[jax.ShapeDtypeStruct((rows, cols), F32)] * 3, compiler_params=_params("arbitrary"))(w, g, m, v)


SC_TILES = 32
SC_LANES = 16
SC_ROWS = 16


def _adamw_sc(ws, gs, ms, vs):
    n = len(ws)
    rows, cols = ws[0].shape
    per_tile = rows // SC_TILES

    def body(*refs):
        ins, outs, (wb, gb, mb, vb, db) = refs[:4 * n], refs[4 * n:7 * n], refs[7 * n:]
        tile = lax.axis_index("sc_subcore") * 2 + lax.axis_index("sc_core")
        for i in range(n):
            for ps in range(per_tile // SC_ROWS):
                slab = pl.ds(tile * per_tile + ps * SC_ROWS, SC_ROWS)
                for k, buf in enumerate((wb, gb, mb, vb)):
                    pltpu.sync_copy(ins[k * n + i].at[slab, :], buf)

                @pl.loop(0, SC_ROWS)
                def _(r):
                    @pl.loop(0, cols, step=SC_LANES)
                    def _(c):
                        at = (r, pl.ds(c, SC_LANES))
                        db[at], mb[at], vb[at] = _adamw_step(wb[at], gb[at], mb[at], vb[at])

                for k, buf in enumerate((db, mb, vb)):
                    pltpu.sync_copy(buf, outs[k * n + i].at[slab, :])

    outs = pl.kernel(
        body, name="adamw_sc", out_type=[jax.ShapeDtypeStruct((rows, cols), F32)] * (3 * n),
        mesh=plsc.VectorSubcoreMesh(core_axis_name="sc_core", subcore_axis_name="sc_subcore"),
        scratch_types=[pltpu.VMEM((SC_ROWS, cols), F32)] * 5,
    )(*ws, *gs, *ms, *vs)
    return outs[:n], outs[n:2 * n], outs[2 * n:]


def _adamw_small(ws, gs, ms, vs):
    n = len(ws)

    def body(*refs):
        ins, outs = refs[:4 * n], refs[4 * n:]
        for i in range(n):
            outs[i][...], outs[n + i][...], outs[2 * n + i][...] = _adamw_step(*(ins[k * n + i][...] for k in range(4)))

    outs = _pc(body, name="adamw_small", out_shape=[jax.ShapeDtypeStruct(w.shape, F32) for w in ws] * 3)(*ws, *gs, *ms, *vs)
    return outs[:n], outs[n:2 * n], outs[2 * n:]


LANES = 128
SMALL_GRADS = (("b_gate", 2 * D), ("conv_w", 3 * W_A), ("v_norm_g", W_B), ("v_norm_b", W_B),
               ("w_s", N_HEAD * CHUNK * CHUNK), ("b_s", N_HEAD * CHUNK), ("ln1_g", D), ("ln1_b", D), ("ln2_g", D), ("ln2_b", D),
               ("loss", 1))


def _pack_rows(parts):
    rows = []
    for a in parts:
        a = a.reshape(-1)
        a = jnp.pad(a, (0, (-a.shape[0]) % LANES))
        rows.append(a.reshape(-1, LANES))
    out = jnp.concatenate(rows, axis=0)
    return jnp.pad(out, ((0, (-out.shape[0]) % 8), (0, 0)))


def _unpack_rows(buf, sizes):
    out, r = [], 0
    for n in sizes:
        nr = -(-n // LANES)
        out.append(buf[r:r + nr].reshape(-1)[:n])
        r += nr
    return out


TM_PROJ = 1024
TM_MIX = 256
TM_DX = 512
DX_PAIR = 6
TK_DW = 4096
TK_DW_IN = 2048
TK_DW_PROJ = 1024
ADD_BLOCK_BYTES = 3 * 1024 * 1024
RB_ADAM = 128
CONV_ROWS = 8


def _add_rows(rows, cols):
    while rows * cols * 4 > ADD_BLOCK_BYTES and rows % 32 == 0:
        rows //= 2
    return rows


def _reduce_adds_1(grads, recvs, place, tag):
    out = [_add_own_half(g, r, place, _add_rows(*r.shape[1:]), f"add_cores_{tag}{a}") for a, (g, r) in enumerate(zip(grads, recvs))]
    return [o[0] for o in out], [o[1] for o in out]


def _reduce_adds_2(sums, recvs, place, tag):
    return [_add_chips(s, r, place, _add_rows(*r.shape[1:]), f"add_chips_{tag}{a}") for a, (s, r) in enumerate(zip(sums, recvs))]


def kernel(x, w_in, b_gate, conv_w, v_norm_g, v_norm_b, w_s, b_s, w_pa, w_pb, w_o, ln1_g, ln1_b, w_ff1, w_ff2, ln2_g, ln2_b, loss_target, m_w_in, m_b_gate, m_conv_w, m_v_norm_g, m_v_norm_b, m_w_s, m_b_s, m_w_pa, m_w_pb, m_w_o, m_ln1_g, m_ln1_b, m_w_ff1, m_w_ff2, m_ln2_g, m_ln2_b, v_w_in, v_b_gate, v_conv_w, v_v_norm_g, v_v_norm_b, v_w_s, v_b_s, v_w_pa, v_w_pb, v_w_o, v_ln1_g, v_ln1_b, v_w_ff1, v_w_ff2, v_ln2_g, v_ln2_b):
    t = x.shape[1]
    core = lax.axis_index("c").astype(jnp.int32).reshape(1)
    chip_idx = 2 * lax.axis_index("x") + lax.axis_index("y")
    chip = chip_idx.astype(jnp.int32).reshape(1)
    place = jnp.concatenate([chip, core])
    x2 = x.reshape(t, D)
    tgt = loss_target.reshape(t, D)

    win4, proj4, ff14, ff24 = _cast_shards(w_in[0], w_pa[0], w_pb[0], w_o[0], w_ff1[0], w_ff2[0], chip)
    conv4 = lax.dynamic_update_slice(jnp.zeros((N_CHIP, CONV_ROWS, W_A // N_CHIP), F32),
                                     jnp.pad(conv_w[0], ((0, CONV_ROWS - 3), (0, 0)))[None], (chip_idx, 0, 0))
    (p, pedge), (win4, proj4, ff14, ff24, conv4) = _proj_fwd(
        x2, chip, TM_PROJ, _gather_comm([win4, proj4, ff14, ff24], conv4, eager=1))

    def full(name, rows_total):
        off, rows = PROJ_OFF[name]
        return proj4[:, off:off + rows, :].reshape(rows_total, D)

    wpa, wpb, wo = full("w_pa", W_A), full("w_pb", W_B), full("w_o", D)
    wff2 = ff24.reshape(D_FF, D)
    cw = jnp.transpose(conv4[:, :3, :], (1, 0, 2)).reshape(3, W_A)
    wsb = w_s[0].astype(BF)
    wstb = jnp.swapaxes(w_s[0], 1, 2).astype(BF)
    bsf = jnp.repeat(jnp.transpose(b_s[0]), CHUNK, axis=1)

    r1, ya, yb = _mix_fwd(p, pedge, x2, wpa, wpb, wo, wsb, bsf, b_gate, cw, v_norm_g, v_norm_b, TM_MIX)
    dr1, dr1b, dedge, x1b, hidb, dh1b, dr2b, acc = _ffn_fwd_bwd(r1, tgt, ff14, wff2, ln1_g, ln1_b, ln2_g, ln2_b, TM_MIX)
    g_ff = [_dw(x1b, dh1b, N_CHIP, D, FF_SHARD, False, True, TK_DW, "dw_ff1"),
            _dw(hidb, dr2b, N_CHIP, FF_SHARD, D, True, False, TK_DW, "dw_ff2")]
    (dp, ab, bbb, zb, dyab, dybb, dbg, dcw, dvgb, dws, dbs_sum), r_ff = _mix_bwd(
        p, pedge, dr1, dedge, ya, yb, wpa, wpb, wo, wsb, wstb, bsf, b_gate, cw, v_norm_g, v_norm_b, TM_MIX, _pair_comm(g_ff))
    s_ff, sb_ff = _reduce_adds_1(g_ff, r_ff, place, "ff")
    dwin4, c_ff = _dw(x2, dp, N_CHIP, D, NP_SHARD, False, True, TK_DW_IN, "dw_in", _chips_comm(sb_ff))
    f_ff = _reduce_adds_2(s_ff, c_ff, place, "ff")
    dproj4, (g_ff1, g_ff2) = _dw_proj(ab, dyab, bbb, dybb, zb, dr1b, TK_DW_PROJ, _join_comm(f_ff))
    dbs = jnp.transpose(jnp.sum(dbs_sum.reshape(CHUNK, N_HEAD, CHUNK), axis=-1))
    small = _pack_rows([dbg[0], dcw[0:3], dvgb[0], dvgb[1], dws, dbs, acc[0], acc[1], acc[2], acc[3],
                        0.5 * jnp.sum(acc[4]) / D])
    g_rest = [dwin4, dproj4]
    nblk = t // TM_DX
    n_a = max(1, min(DX_PAIR, nblk // 4))
    dx, r_rest = _dx(dp, win4, dr1, TM_DX, 0, n_a, None, "dx_a", _pair_comm(g_rest, small))
    s_rest, sb_rest = _reduce_adds_1(g_rest, r_rest[:2], place, "rest")
    csmall = _add_small(small, r_rest[2])
    dx, c_rest = _dx(dp, win4, dr1, TM_DX, n_a, nblk - n_a, dx, "dx_b", _chips_comm(sb_rest, csmall))
    f_rest = _reduce_adds_2(s_rest, c_rest[:2], place, "rest")
    gsmall = _sum_small_chips(csmall, c_rest[2], place)
    g_in, g_proj = _comm_call(_join_comm(f_rest), "join_rest")

    grads = {"w_in": g_in, "w_ff1": g_ff1, "w_ff2": g_ff2}
    for name, _ in PROJ_ROWS:
        off, rows = PROJ_OFF[name]
        grads[name] = g_proj[off:off + rows, :]
    for (name, n), flat in zip(SMALL_GRADS, _unpack_rows(gsmall, [n for _, n in SMALL_GRADS])):
        grads[name] = flat
    loss = grads.pop("loss").reshape(())
    grads["conv_w"] = lax.dynamic_slice(grads["conv_w"].reshape(3, W_A), (0, chip_idx * (W_A // N_CHIP)), (3, W_A // N_CHIP))

    weights = dict(w_in=w_in, b_gate=b_gate, conv_w=conv_w, v_norm_g=v_norm_g, v_norm_b=v_norm_b, w_s=w_s, b_s=b_s,
                   w_pa=w_pa, w_pb=w_pb, w_o=w_o, ln1_g=ln1_g, ln1_b=ln1_b, w_ff1=w_ff1, w_ff2=w_ff2, ln2_g=ln2_g, ln2_b=ln2_b)
    mom1 = dict(w_in=m_w_in, b_gate=m_b_gate, conv_w=m_conv_w, v_norm_g=m_v_norm_g, v_norm_b=m_v_norm_b, w_s=m_w_s,
                b_s=m_b_s, w_pa=m_w_pa, w_pb=m_w_pb, w_o=m_w_o, ln1_g=m_ln1_g, ln1_b=m_ln1_b, w_ff1=m_w_ff1,
                w_ff2=m_w_ff2, ln2_g=m_ln2_g, ln2_b=m_ln2_b)
    mom2 = dict(w_in=v_w_in, b_gate=v_b_gate, conv_w=v_conv_w, v_norm_g=v_v_norm_g, v_norm_b=v_v_norm_b, w_s=v_w_s,
                b_s=v_b_s, w_pa=v_w_pa, w_pb=v_w_pb, w_o=v_w_o, ln1_g=v_ln1_g, ln1_b=v_ln1_b, w_ff1=v_w_ff1,
                w_ff2=v_w_ff2, ln2_g=v_ln2_g, ln2_b=v_ln2_b)
    order = list(weights)
    big = ("w_in", "w_pa", "w_pb", "w_o", "w_ff1", "w_ff2")
    delta, new_m, new_v = {}, {}, {}
    early = ("w_ff1", "w_ff2")
    ds, ms, vs = _adamw_sc(*([d[n][0] if d is not grads else d[n] for n in early] for d in (weights, grads, mom1, mom2)))
    for name, d_, m_, v_ in zip(early, ds, ms, vs):
        delta[name], new_m[name], new_v[name] = d_, m_, v_
    for name in big:
        if name in early:
            continue
        w2 = weights[name][0]
        delta[name], new_m[name], new_v[name] = _adamw(w2, grads[name], mom1[name][0], mom2[name][0], RB_ADAM, "adamw_" + name)
    little = [n for n in order if n not in big]
    flat2d = lambda a: a.reshape(-1, a.shape[-1])
    ds, ms, vs = _adamw_small(*([flat2d(d[n].reshape(weights[n].shape)) for n in little] for d in (weights, grads, mom1, mom2)))
    for name, d_, m_, v_ in zip(little, ds, ms, vs):
        delta[name], new_m[name], new_v[name] = d_, m_, v_

    shaped = lambda d: [d[n].reshape(weights[n].shape) for n in order]
    return (loss, dx.reshape(x.shape), *shaped(grads), *shaped(delta), *shaped(new_m), *shaped(new_v))
```

```python
import functools
from typing import NamedTuple

import jax
import jax.numpy as jnp
from jax import lax
from jax.experimental import pallas as pl
from jax.experimental.pallas import tpu as pltpu
from jax.experimental.pallas import tpu_sc as plsc

D = 1024
W_A = 1536
W_B = 1024
CHUNK = 128
N_HEAD = 8
D_FF = 4096
N_PROJ = 3 * W_A + 2 * W_B + 2 * D
OFF_CA, OFF_HA, OFF_UB, OFF_VB, OFF_GA, OFF_GB = 1536, 3072, 4608, 5632, 6656, 7680
LN_EPS = 1e-5
ALPHA = 2.0 ** 0.25
N_CHIP = 4
NP_SHARD = N_PROJ // N_CHIP
FF_SHARD = D_FF // N_CHIP
ADAM_LR, ADAM_B1, ADAM_B2, ADAM_EPS, ADAM_WD, ADAM_STEP = 0.001, 0.9, 0.999, 1e-08, 0.01, 10

PROJ_ROWS = (("w_pa", W_A // N_CHIP), ("w_pb", W_B // N_CHIP), ("w_o", D // N_CHIP))
PROJ_OFF = {}
_o = 0
for _n, _r in PROJ_ROWS:
    PROJ_OFF[_n] = (_o, _r)
    _o += _r
PROJ_TOTAL = _o

V7X_VMEM_BYTES = 64 * 1024 * 1024
VMEM_LIMIT = V7X_VMEM_BYTES - 8 * 1024 * 1024
HALO = 8
EDGE_TILE = 256

BF = jnp.bfloat16
F32 = jnp.float32
MESH = pl.DeviceIdType.MESH
HBM_SPEC = pl.BlockSpec(memory_space=pltpu.HBM)


def _pc(body, **kw):
    return pl.pallas_call(body, **kw)


def _params(*sem):
    return pltpu.CompilerParams(dimension_semantics=sem, vmem_limit_bytes=VMEM_LIMIT)


def _resident(shape):
    n = len(shape)
    return pl.BlockSpec(shape, lambda *_: (0,) * n, pipeline_mode=pl.Buffered(1))


def _dot(a, b):
    return jnp.dot(a, b, preferred_element_type=F32)


def _dot_nt(a, b):
    return lax.dot_general(a, b, (((1,), (1,)), ((), ())), preferred_element_type=F32)


def _dot_tn(a, b):
    return lax.dot_general(a, b, (((0,), (0,)), ((), ())), preferred_element_type=F32)


def _gelu(x):
    t = jnp.tanh(0.7978845608028654 * (x + 0.044715 * (x * x * x)))
    return 0.5 * x * (1.0 + t), t


def _gelu_grad(x, t):
    return 0.5 * (1.0 + t) + 0.5 * x * (1.0 - t * t) * (0.7978845608028654 * (1.0 + 0.134145 * (x * x)))


def _ln_stats(r):
    mu = jnp.mean(r, axis=-1, keepdims=True)
    xc = r - mu
    var = jnp.mean(xc * xc, axis=-1, keepdims=True)
    rstd = lax.rsqrt(var + LN_EPS)
    return xc * rstd, rstd


def _ln_bwd(dy, g, xh, rstd):
    dxh = dy * g
    m1 = jnp.mean(dxh, axis=-1, keepdims=True)
    m2 = jnp.mean(dxh * xh, axis=-1, keepdims=True)
    return rstd * (dxh - m1 - xh * m2)


def _colsum(v):
    return jnp.sum(v, axis=0, keepdims=True)


class _Comm(NamedTuple):
    args: tuple
    out_shape: tuple
    aliases: dict
    n_sems: int
    stages: tuple


def _place():
    x, y, c = lax.axis_index("x"), lax.axis_index("y"), lax.axis_index("c")
    return x, y, c, 2 * x + y


def _flip(x, y, c, r):
    return (x ^ (r >> 1), y ^ (r & 1), c)


def _remote(src, dst, send_sems, recv_sems, k, peer):
    return pltpu.make_async_remote_copy(src_ref=src, dst_ref=dst, send_sem=send_sems.at[k], recv_sem=recv_sems.at[k],
                                        device_id=peer, device_id_type=MESH)


def _host_call(body, comm, *, name, grid, in_specs, out_specs, out_shape, args, scratch_shapes=(), aliases=None, prefetch=None,
               body_reads_comm=False):
    sem = ("arbitrary",) * len(grid)
    aliases = dict(aliases or {})
    n_pre = 0 if prefetch is None else 1
    n_in, n_out, n_scr = len(in_specs), len(out_specs), len(scratch_shapes)
    c_in, c_out = (0, 0) if comm is None else (len(comm.args), len(comm.out_shape))
    steps = {"first": (0,) * len(grid), "late": (grid[0] - 1,) + (0,) * (len(grid) - 1), "last": tuple(g - 1 for g in grid)}

    def wrapped(*refs):
        refs = refs[n_pre:]
        own_in, cin = refs[:n_in], refs[n_in:n_in + c_in]
        o0 = n_in + c_in
        own_out, cout = refs[o0:o0 + n_out], refs[o0 + n_out:o0 + n_out + c_out]
        s0 = o0 + n_out + c_out
        scr, sems = refs[s0:s0 + n_scr], refs[s0 + n_scr:]

        def run(before):
            for phase, fn in () if comm is None else comm.stages:
                at_step = isinstance(phase, tuple)
                if before != (at_step or phase == "first"):
                    continue
                step = phase if at_step else steps[phase]
                cond = pl.program_id(0) == step[0]
                for d in range(1, len(grid)):
                    cond = jnp.logical_and(cond, pl.program_id(d) == step[d])
                pl.when(cond)(functools.partial(fn, cin, cout, *sems))

        run(True)
        if body_reads_comm:
            body(*own_in, *own_out, *scr, comm_refs=cout)
        else:
            body(*own_in, *own_out, *scr)
        run(False)

    in_specs = list(in_specs) + [HBM_SPEC] * c_in
    out_specs = list(out_specs) + [HBM_SPEC] * c_out
    out_shape = list(out_shape) + ([] if comm is None else list(comm.out_shape))
    scratch_shapes = list(scratch_shapes) + ([] if comm is None else [pltpu.SemaphoreType.DMA((comm.n_sems,))] * 2)
    args = tuple(args) + (() if comm is None else tuple(comm.args))
    if comm is not None:
        aliases.update({n_in + i: n_out + o for i, o in comm.aliases.items()})
    aliases = {i + n_pre: o for i, o in aliases.items()}
    if prefetch is None:
        kw = dict(grid=grid, in_specs=in_specs, out_specs=out_specs, scratch_shapes=scratch_shapes)
    else:
        kw = dict(grid_spec=pltpu.PrefetchScalarGridSpec(num_scalar_prefetch=1, grid=grid, in_specs=in_specs,
                                                         out_specs=out_specs, scratch_shapes=scratch_shapes))
        args = (prefetch,) + args
    outs = _pc(wrapped, name=name, out_shape=out_shape, input_output_aliases=aliases, compiler_params=_params(*sem), **kw)(*args)
    return outs[:n_out], outs[n_out:]


def _comm_call(comm, name):
    def body(*refs):
        c_in, c_out = len(comm.args), len(comm.out_shape)
        cin, cout, (send_sems, recv_sems) = refs[:c_in], refs[c_in:c_in + c_out], refs[c_in + c_out:]
        for phase in ("first", "late", "last"):
            for ph, fn in comm.stages:
                if ph == phase:
                    fn(cin, cout, send_sems, recv_sems)

    return _pc(body, name=name, in_specs=[HBM_SPEC] * len(comm.args), out_specs=[HBM_SPEC] * len(comm.out_shape),
               out_shape=list(comm.out_shape), scratch_shapes=[pltpu.SemaphoreType.DMA((comm.n_sems,))] * 2,
               input_output_aliases=dict(comm.aliases))(*comm.args)


def _sequencer_call(comm, name, collective_id, peers_of):
    hbm = pltpu.MemorySpace.HBM
    cin = [jax.new_ref(a, memory_space=hbm) for a in comm.args]
    cout = [jax.empty_ref(jax.ShapeDtypeStruct(o.shape, o.dtype), memory_space=hbm) for o in comm.out_shape]

    @pl.kernel(mesh=plsc.ScalarSubcoreMesh(axis_name="sequencer", num_cores=1), name=name,
               scratch_types=(pltpu.SemaphoreType.DMA((comm.n_sems,)), pltpu.SemaphoreType.DMA((comm.n_sems,))),
               compiler_params=pltpu.CompilerParams(collective_id=collective_id))
    def launch(send_sems, recv_sems):
        barrier = pltpu.get_barrier_semaphore()
        peers = peers_of()
        for peer in peers:
            pl.semaphore_signal(barrier, inc=1, device_id=peer, device_id_type=MESH)
        pl.semaphore_wait(barrier, len(peers))
        for phase in ("first", "late", "last"):
            for ph, fn in comm.stages:
                if ph == phase:
                    fn(cin, cout, send_sems, recv_sems)

    launch()
    return [r[...] for r in cout]


def _sibling_peer():
    x, y, c, _ = _place()
    return [(x, y, 1 - c)]


def _chip_peers():
    x, y, c, _ = _place()
    return [_flip(x, y, c, r) for r in (1, 2, 3)]


def _gather_comm(bufs, whole=None, eager=0):
    n = len(bufs)
    halves = [b.shape[1] // 2 for b in bufs]
    k_ici = lambda a, r: 3 * a + r - 1
    k_d2d = lambda a, r: 3 * n + 3 * a + r - 1
    k_whole = lambda r: 6 * n + r - 1

    def half(ref, slot, c, a):
        return ref.at[slot, pl.ds(c * halves[a], halves[a])]

    def send(cin, cout, ss, rs):
        x, y, c, j = _place()
        for a in range(n):
            mine = half(cout[a], j, c, a)
            for r in (1, 2, 3):
                _remote(mine, mine, ss, rs, k_ici(a, r), _flip(x, y, c, r)).start()
        if whole is not None:
            for r in (1, 2, 3):
                _remote(cout[n].at[j], cout[n].at[j], ss, rs, k_whole(r), _flip(x, y, c, r)).start()

    def pass_on(cout, ss, rs, a, r):
        x, y, c, j = _place()
        landed = half(cout[a], j ^ r, c, a)
        _remote(landed, landed, ss, rs, k_ici(a, r), (x, y, 1 - c)).wait_recv()
        _remote(landed, landed, ss, rs, k_d2d(a, r), (x, y, 1 - c)).start()

    def passed_on(cout, ss, rs, a, r):
        x, y, c, j = _place()
        theirs = half(cout[a], j ^ r, 1 - c, a)
        _remote(theirs, theirs, ss, rs, k_d2d(a, r), (x, y, 1 - c)).wait_recv()

    def arrive(r, cin, cout, ss, rs):
        for a in range(eager):
            pass_on(cout, ss, rs, a, r)
        for a in range(eager):
            passed_on(cout, ss, rs, a, r)

    def forward(cin, cout, ss, rs):
        for a in range(eager, n):
            for r in (1, 2, 3):
                pass_on(cout, ss, rs, a, r)

    def finish(cin, cout, ss, rs):
        x, y, c, j = _place()
        sibling = (x, y, 1 - c)
        for a in range(eager, n):
            for r in (1, 2, 3):
                passed_on(cout, ss, rs, a, r)
        for a in range(n):
            mine = half(cout[a], j, c, a)
            for r in (1, 2, 3):
                _remote(mine, mine, ss, rs, k_ici(a, r), sibling).wait_send()
                landed = half(cout[a], j ^ r, c, a)
                _remote(landed, landed, ss, rs, k_d2d(a, r), sibling).wait_send()
        if whole is not None:
            for r in (1, 2, 3):
                cp = _remote(cout[n].at[j ^ r], cout[n].at[j ^ r], ss, rs, k_whole(r), sibling)
                cp.wait_recv()
                cp.wait_send()

    args = tuple(bufs) + ((whole,) if whole is not None else ())
    out_shape = tuple(jax.ShapeDtypeStruct(b.shape, b.dtype) for b in args)
    aliases = {a: a for a in range(len(args))}
    arrivals = tuple(((r, 0), functools.partial(arrive, r)) for r in (1, 2, 3)) if eager else ()
    return _Comm(args, out_shape, aliases, 6 * n + 3, (("first", send),) + arrivals + (("late", forward), ("last", finish)))


def _pair_comm(grads, small=None):
    n = len(grads)
    halves = [g.shape[1] // 2 for g in grads]

    def copies(cin, cout, ss, rs):
        x, y, c, _ = _place()
        sibling = (x, y, 1 - c)
        cps = [_remote(cin[a].at[:, pl.ds((1 - c) * halves[a], halves[a]), :], cout[a], ss, rs, a, sibling) for a in range(n)]
        if small is not None:
            cps.append(_remote(cin[n], cout[n], ss, rs, n, sibling))
        return cps

    def start(cin, cout, ss, rs):
        for cp in copies(cin, cout, ss, rs):
            cp.start()

    def finish(cin, cout, ss, rs):
        for cp in copies(cin, cout, ss, rs):
            cp.wait()

    args = tuple(grads) + ((small,) if small is not None else ())
    out_shape = tuple(jax.ShapeDtypeStruct((N_CHIP, h, g.shape[2]), F32) for g, h in zip(grads, halves))
    out_shape += (jax.ShapeDtypeStruct(small.shape, F32),) if small is not None else ()
    return _Comm(args, out_shape, {}, n + 1, (("first", start), ("last", finish)))


def _chips_comm(sums_bf, small=None):
    n = len(sums_bf)

    def copies(cin, cout, ss, rs):
        x, y, c, j = _place()
        cps = []
        for r in (1, 2, 3):
            peer = _flip(x, y, c, r)
            for a in range(n):
                cps.append(_remote(cin[a].at[j ^ r], cout[a].at[r - 1], ss, rs, (n + 1) * (r - 1) + a, peer))
            if small is not None:
                cps.append(_remote(cin[n], cout[n].at[r - 1], ss, rs, (n + 1) * (r - 1) + n, peer))
        return cps

    def start(cin, cout, ss, rs):
        for cp in copies(cin, cout, ss, rs):
            cp.start()

    def finish(cin, cout, ss, rs):
        for cp in copies(cin, cout, ss, rs):
            cp.wait()

    args = tuple(sums_bf) + ((small,) if small is not None else ())
    out_shape = tuple(jax.ShapeDtypeStruct((3,) + s.shape[1:], BF) for s in sums_bf)
    out_shape += (jax.ShapeDtypeStruct((3,) + small.shape, F32),) if small is not None else ()
    return _Comm(args, out_shape, {}, 3 * (n + 1), (("first", start), ("last", finish)))


def _join_comm(shards):
    n = len(shards)
    halves = [s.shape[0] // 2 for s in shards]

    def start(cin, cout, ss, rs):
        x, y, c, _ = _place()
        for a in range(n):
            mine = cout[a].at[pl.ds(c * halves[a], halves[a]), :]
            _remote(mine, mine, ss, rs, a, (x, y, 1 - c)).start()

    def finish(cin, cout, ss, rs):
        x, y, c, _ = _place()
        for a in range(n):
            theirs = cout[a].at[pl.ds((1 - c) * halves[a], halves[a]), :]
            cp = _remote(theirs, theirs, ss, rs, a, (x, y, 1 - c))
            cp.wait_recv()
            cp.wait_send()

    out_shape = tuple(jax.ShapeDtypeStruct(s.shape, F32) for s in shards)
    return _Comm(tuple(shards), out_shape, {a: a for a in range(n)}, n, (("first", start), ("last", finish)))


def _cast_shards(w_in, w_pa, w_pb, w_o, w_ff1, w_ff2, chip):
    def body(j_ref, win_ref, wpa_ref, wpb_ref, wo_ref, wff1_ref, wff2_ref, win4_ref, proj4_ref, ff14_ref, ff24_ref):
        win4_ref[...] = win_ref[...].astype(BF)
        for name, ref in (("w_pa", wpa_ref), ("w_pb", wpb_ref), ("w_o", wo_ref)):
            off, rows = PROJ_OFF[name]
            proj4_ref[off:off + rows, :] = ref[...].astype(BF)
        ff14_ref[...] = wff1_ref[...].astype(BF)
        ff24_ref[...] = wff2_ref[...].astype(BF)

    whole = lambda a: pl.BlockSpec(a.shape, lambda i, j: (0, 0), pipeline_mode=pl.Buffered(1))
    slot = lambda rows, cols: pl.BlockSpec((None, rows, cols), lambda i, j: (j[0], 0, 0))
    ws = (w_in, w_pa, w_pb, w_o, w_ff1, w_ff2)
    shapes = ((D, NP_SHARD), (PROJ_TOTAL, D), (D, FF_SHARD), (FF_SHARD, D))
    return _pc(
        body, name="cast_shards",
        grid_spec=pltpu.PrefetchScalarGridSpec(num_scalar_prefetch=1, grid=(1,), in_specs=[whole(w) for w in ws],
                                               out_specs=[slot(*s) for s in shapes]),
        out_shape=[jax.ShapeDtypeStruct((N_CHIP,) + s, BF) for s in shapes],
        compiler_params=_params("arbitrary"))(chip, *ws)


def _proj_fwd(x, chip, tm, comm):
    t = x.shape[0]
    sub = tm // EDGE_TILE

    def body(x_ref, p_ref, edge_ref, w_ref, w_sem, comm_refs):
        @pl.when(pl.program_id(1) == 0)
        def _():
            _, _, _, j = _place()
            block = pltpu.make_async_copy(comm_refs[0].at[j ^ pl.program_id(0)], w_ref, w_sem)
            block.start()
            block.wait()

        p_ref[...] = _dot(x_ref[...].astype(BF), w_ref[...])
        _write_edges(edge_ref, p_ref, tm)

    return _host_call(
        body, comm, name="proj_fwd", grid=(N_CHIP, t // tm), prefetch=chip, body_reads_comm=True,
        in_specs=[pl.BlockSpec((tm, D), lambda r, i, j: (i, 0))],
        out_specs=[pl.BlockSpec((tm, NP_SHARD), lambda r, i, j: (i, j[0] ^ r)),
                   pl.BlockSpec((sub, 2 * HALO, NP_SHARD), lambda r, i, j: (i, 0, j[0] ^ r))],
        out_shape=[jax.ShapeDtypeStruct((t, N_PROJ), F32), jax.ShapeDtypeStruct((t // EDGE_TILE, 2 * HALO, N_PROJ), F32)],
        scratch_shapes=[pltpu.VMEM((D, NP_SHARD), BF), pltpu.SemaphoreType.DMA],
        args=(x,))


def _edge_specs(t, tm, w):
    k, last = tm // EDGE_TILE, t // EDGE_TILE - 1
    return [pl.BlockSpec((None, HALO, w), lambda i: (jnp.maximum(i * k - 1, 0), 1, 0)),
            pl.BlockSpec((None, HALO, w), lambda i: (jnp.minimum((i + 1) * k, last), 0, 0))]


def _write_edges(edge_ref, rows, tm):
    for s in range(tm // EDGE_TILE):
        edge_ref[s, 0:HALO, :] = rows[s * EDGE_TILE:s * EDGE_TILE + HALO, :]
        edge_ref[s, HALO:2 * HALO, :] = rows[(s + 1) * EDGE_TILE - HALO:(s + 1) * EDGE_TILE, :]


def _pcols(p_ref, lo, hi):
    return p_ref[:, lo:hi]


def _end_masks(nt):
    i = pl.program_id(0)
    return (i > 0).astype(F32), (i < nt - 1).astype(F32)


def _conv_fwd(p_ref, prev_ref, next_ref, cw_ref, tm, has_prev, has_next):
    ca = _pcols(p_ref, OFF_CA, OFF_HA)
    ha = _pcols(p_ref, OFF_HA, OFF_UB)
    ch = ca * ha
    ch_prev = prev_ref[HALO - 1:HALO, OFF_CA:OFF_HA] * prev_ref[HALO - 1:HALO, OFF_HA:OFF_UB] * has_prev
    ch_next = next_ref[0:1, OFF_CA:OFF_HA] * next_ref[0:1, OFF_HA:OFF_UB] * has_next
    row = lax.broadcasted_iota(jnp.int32, (tm, W_A), 0)
    ch_m1 = jnp.where(row == 0, ch_prev, pltpu.roll(ch, 1, 0))
    ch_p1 = jnp.where(row == tm - 1, ch_next, pltpu.roll(ch, tm - 1, 0))
    cv = cw_ref[0:1, :] * ch_m1 + cw_ref[1:2, :] * ch + cw_ref[2:3, :] * ch_p1
    return ca, ha, ch, ch_m1, ch_p1, cv


def _spatial_fwd(p_ref, vg_ref, vb_ref, ws_ref, bsf_ref, vnb_ref, mixed_ref, tm):
    vb_pre = _pcols(p_ref, OFF_VB, OFF_GA)
    gv, tv = _gelu(vb_pre)
    xhv, rstdv = _ln_stats(gv)
    vnb_ref[...] = (xhv * vg_ref[...] + vb_ref[...]).astype(BF)
    for c in range(tm // CHUNK):
        rows = slice(c * CHUNK, (c + 1) * CHUNK)
        for h in range(N_HEAD):
            cols = slice(h * CHUNK, (h + 1) * CHUNK)
            mixed_ref[rows, cols] = _dot(ws_ref[h], vnb_ref[rows, cols]) + bsf_ref[:, cols]
    return vb_pre, tv, xhv, rstdv


def _mix_fwd(p, pedge, x, wpa, wpb, wo, wsb, bsf, bg, cw, vg, vb, tm):
    t = x.shape[0]
    nt = t // tm

    def body(p_ref, prev_ref, next_ref, x_ref, wpa_ref, wpb_ref, wo_ref, ws_ref, bsf_ref, bg_ref, cw_ref, vg_ref, vb_ref,
             r1_ref, ya_ref, yb_ref, vnb_ref, mixed_ref):
        has_prev, has_next = _end_masks(nt)
        _, _, _, _, _, cv = _conv_fwd(p_ref, prev_ref, next_ref, cw_ref, tm, has_prev, has_next)
        a = _pcols(p_ref, 0, OFF_CA) * cv
        ya = _dot(a.astype(BF), wpa_ref[...])
        ya_ref[...] = ya
        _spatial_fwd(p_ref, vg_ref, vb_ref, ws_ref, bsf_ref, vnb_ref, mixed_ref, tm)
        gu, _ = _gelu(_pcols(p_ref, OFF_UB, OFF_VB))
        bb = gu * mixed_ref[...]
        yb = _dot(bb.astype(BF), wpb_ref[...])
        yb_ref[...] = yb
        ga = jax.nn.sigmoid(_pcols(p_ref, OFF_GA, OFF_GB) + bg_ref[:, 0:D])
        gb = jax.nn.sigmoid(_pcols(p_ref, OFF_GB, N_PROJ) + bg_ref[:, D:2 * D])
        z = ga * ya + gb * yb
        r1_ref[...] = ALPHA * x_ref[...] + _dot(z.astype(BF), wo_ref[...])

    tile = lambda w: pl.BlockSpec((tm, w), lambda i: (i, 0))
    return _pc(
        body, name="mix_fwd", grid=(nt,),
        in_specs=[tile(N_PROJ), *_edge_specs(t, tm, N_PROJ), tile(D),
                  _resident((W_A, D)), _resident((W_B, D)), _resident((D, D)), _resident((N_HEAD, CHUNK, CHUNK)),
                  _resident((CHUNK, W_B)), _resident((1, 2 * D)), _resident((3, W_A)), _resident((1, W_B)),
                  _resident((1, W_B))],
        out_specs=[tile(D), tile(D), tile(D)],
        out_shape=[jax.ShapeDtypeStruct((t, D), F32)] * 3,
        scratch_shapes=[pltpu.VMEM((tm, W_B), BF), pltpu.VMEM((tm, W_B), F32)],
        compiler_params=_params("arbitrary"),
    )(p, pedge, pedge, x, wpa, wpb, wo, wsb, bsf, bg, cw, vg, vb)


def _ffn_fwd_bwd(r1, tgt, wff1, wff2, ln1g, ln1b, ln2g, ln2b, tm):
    t = r1.shape[0]

    def body(r1_ref, tgt_ref, w1_ref, w2_ref, g1_ref, b1_ref, g2_ref, b2_ref,
             dr1_ref, dr1b_ref, dedge_ref, x1b_ref, hidb_ref, dh1b_ref, dr2b_ref, acc_ref, relu_ref):
        @pl.when(pl.program_id(0) == 0)
        def _():
            acc_ref[...] = jnp.zeros_like(acc_ref)

        xh1, rstd1 = _ln_stats(r1_ref[...])
        x1 = xh1 * g1_ref[...] + b1_ref[...]
        x1b_ref[...] = x1.astype(BF)
        ffn = jnp.zeros((tm, D), F32)
        for j in range(N_CHIP):
            cols = slice(j * FF_SHARD, (j + 1) * FF_SHARD)
            r = jnp.maximum(_dot(x1b_ref[...], w1_ref[j]), 0.0)
            relu_ref[:, cols] = r
            hidb_ref[:, cols] = (r * r).astype(BF)
            ffn = ffn + _dot(hidb_ref[:, cols], w2_ref[cols, :])
        xh2, rstd2 = _ln_stats(ALPHA * x1 + ffn)
        diff = xh2 * g2_ref[...] + b2_ref[...] - tgt_ref[...]
        acc_ref[4:5, :] += _colsum(diff * diff)
        dx2 = diff * (1.0 / D)
        acc_ref[2:3, :] += _colsum(dx2 * xh2)
        acc_ref[3:4, :] += _colsum(dx2)
        dr2 = _ln_bwd(dx2, g2_ref[...], xh2, rstd2)
        dr2b_ref[...] = dr2.astype(BF)
        dx1 = ALPHA * dr2
        for j in range(N_CHIP):
            cols = slice(j * FF_SHARD, (j + 1) * FF_SHARD)
            dhid = _dot_nt(dr2b_ref[...], w2_ref[cols, :])
            dh1b_ref[:, cols] = (dhid * (2.0 * relu_ref[:, cols])).astype(BF)
            dx1 = dx1 + _dot_nt(dh1b_ref[:, cols], w1_ref[j])
        acc_ref[0:1, :] += _colsum(dx1 * xh1)
        acc_ref[1:2, :] += _colsum(dx1)
        dr1 = _ln_bwd(dx1, g1_ref[...], xh1, rstd1)
        dr1_ref[...] = dr1
        dr1b_ref[...] = dr1.astype(BF)
        _write_edges(dedge_ref, dr1_ref, tm)

    tile = lambda w: pl.BlockSpec((tm, w), lambda i: (i, 0))
    vec = _resident((1, D))
    return _pc(
        body, name="ffn_fwd_bwd", grid=(t // tm,),
        in_specs=[tile(D), tile(D), _resident((N_CHIP, D, FF_SHARD)), _resident((D_FF, D)), vec, vec, vec, vec],
        out_specs=[tile(D), tile(D), pl.BlockSpec((tm // EDGE_TILE, 2 * HALO, D), lambda i: (i, 0, 0)), tile(D), tile(D_FF),
                   tile(D_FF), tile(D), pl.BlockSpec((8, D), lambda i: (0, 0))],
        out_shape=[jax.ShapeDtypeStruct((t, D), F32), jax.ShapeDtypeStruct((t, D), BF),
                   jax.ShapeDtypeStruct((t // EDGE_TILE, 2 * HALO, D), F32), jax.ShapeDtypeStruct((t, D), BF),
                   jax.ShapeDtypeStruct((t, D_FF), BF), jax.ShapeDtypeStruct((t, D_FF), BF),
                   jax.ShapeDtypeStruct((t, D), BF), jax.ShapeDtypeStruct((8, D), F32)],
        scratch_shapes=[pltpu.VMEM((tm, D_FF), F32)],
        compiler_params=_params("arbitrary"),
    )(r1, tgt, wff1, wff2, ln1g, ln1b, ln2g, ln2b)


def _dw(a, b, nblk, am, bn, a_blocked, b_blocked, tk, name, comm=None):
    t = a.shape[0]

    def body(a_ref, b_ref, o_ref):
        @pl.when(pl.program_id(1) == 0)
        def _():
            o_ref[...] = jnp.zeros_like(o_ref)

        o_ref[...] += _dot_tn(a_ref[...].astype(BF), b_ref[...])

    outs, got = _host_call(
        body, comm, name=name, grid=(nblk, t // tk),
        in_specs=[pl.BlockSpec((tk, am), (lambda j, k: (k, j)) if a_blocked else (lambda j, k: (k, 0))),
                  pl.BlockSpec((tk, bn), (lambda j, k: (k, j)) if b_blocked else (lambda j, k: (k, 0)))],
        out_specs=[pl.BlockSpec((None, am, bn), lambda j, k: (j, 0, 0))],
        out_shape=[jax.ShapeDtypeStruct((nblk, am, bn), F32)], args=(a, b))
    return outs[0] if comm is None else (outs[0], got)


def _dw_proj(ab, dyab, bbb, dybb, zb, dr1b, tk, comm):
    t = ab.shape[0]
    pairs = (("w_pa", 0, 1), ("w_pb", 2, 3), ("w_o", 4, 5))

    def body(*refs):
        o_ref = refs[6]

        @pl.when(pl.program_id(0) == 0)
        def _():
            o_ref[...] = jnp.zeros_like(o_ref)

        for name, ia, ib in pairs:
            off, rows = PROJ_OFF[name]
            for k in range(N_CHIP):
                o_ref[k, off:off + rows, :] += _dot_tn(refs[ia][:, k * rows:(k + 1) * rows], refs[ib][...])

    tile = lambda w: pl.BlockSpec((tk, w), lambda i: (i, 0))
    outs, got = _host_call(
        body, comm, name="dw_proj", grid=(t // tk,), in_specs=[tile(W_A), tile(D), tile(W_B), tile(D), tile(D), tile(D)],
        out_specs=[pl.BlockSpec((N_CHIP, PROJ_TOTAL, D), lambda i: (0, 0, 0))],
        out_shape=[jax.ShapeDtypeStruct((N_CHIP, PROJ_TOTAL, D), F32)], args=(ab, dyab, bbb, dybb, zb, dr1b))
    return outs[0], got


def _dx(dp, win4, dr1, tm, blk0, nblk, filled, name, comm, after=None):
    t = dp.shape[0]

    def body(dp_ref, w_ref, dr1_ref, *rest):
        dx = ALPHA * dr1_ref[...]
        for j in range(N_CHIP):
            dx = dx + _dot_nt(dp_ref[:, j * NP_SHARD:(j + 1) * NP_SHARD], w_ref[j])
        rest[-1][...] = dx

    in_specs = [pl.BlockSpec((tm, N_PROJ), lambda i: (i + blk0, 0)), _resident((N_CHIP, D, NP_SHARD)),
                pl.BlockSpec((tm, D), lambda i: (i + blk0, 0))]
    args = (dp, win4, dr1)
    aliases = None
    if filled is not None:
        in_specs.append(pl.BlockSpec(memory_space=pl.ANY))
        args += (filled,)
        aliases = {3: 0}
    if after is not None:
        in_specs.append(pl.BlockSpec(memory_space=pl.ANY))
        args += (after,)
    outs, got = _host_call(
        body, comm, name=name, grid=(nblk,), in_specs=in_specs, out_specs=[pl.BlockSpec((tm, D), lambda i: (i + blk0, 0))],
        out_shape=[jax.ShapeDtypeStruct((t, D), F32)], args=args, aliases=aliases)
    return outs[0], got


def _mix_bwd(p, pedge, dr1, dedge, ya, yb, wpa, wpb, wo, wsb, wstb, bsf, bg, cw, vg, vb, tm, comm):
    t = p.shape[0]
    nt = t // tm
    te = tm + 2 * HALO
    mid = slice(HALO, HALO + tm)

    def body(p_ref, prev_ref, next_ref, dr1_ref, dprev_ref, dnext_ref, ya_ref, yb_ref, wpa_ref, wpb_ref, wo_ref,
             ws_ref, wst_ref, bsf_ref, bg_ref, cw_ref, vg_ref, vb_ref,
             dp_ref, ab_ref, bbb_ref, zb_ref, dyab_ref, dybb_ref, dbg_ref, dcw_ref, dvgb_ref, dws_ref, dbs_ref,
             vnb_ref, mixed_ref, dmixb_ref, dvn_ref):
        @pl.when(pl.program_id(0) == 0)
        def _():
            for r in (dbg_ref, dcw_ref, dvgb_ref, dws_ref, dbs_ref):
                r[...] = jnp.zeros_like(r)

        has_prev, has_next = _end_masks(nt)
        ca, ha, ch, ch_m1, ch_p1, cv = _conv_fwd(p_ref, prev_ref, next_ref, cw_ref, tm, has_prev, has_next)
        ba = _pcols(p_ref, 0, OFF_CA)
        ab_ref[...] = (ba * cv).astype(BF)
        vb_pre, tv, xhv, rstdv = _spatial_fwd(p_ref, vg_ref, vb_ref, ws_ref, bsf_ref, vnb_ref, mixed_ref, tm)
        ub = _pcols(p_ref, OFF_UB, OFF_VB)
        gu, tu = _gelu(ub)
        bbb_ref[...] = (gu * mixed_ref[...]).astype(BF)
        bga = bg_ref[:, 0:D]
        ga = jax.nn.sigmoid(_pcols(p_ref, OFF_GA, OFF_GB) + bga)
        gb = jax.nn.sigmoid(_pcols(p_ref, OFF_GB, N_PROJ) + bg_ref[:, D:2 * D])
        ya = ya_ref[...]
        yb = yb_ref[...]
        zb_ref[...] = (ga * ya + gb * yb).astype(BF)

        dr1_ext = jnp.concatenate([dprev_ref[...] * has_prev, dr1_ref[...], dnext_ref[...] * has_next], axis=0)
        dz_ext = _dot_nt(dr1_ext.astype(BF), wo_ref[...])
        ga_ext = jnp.concatenate([jax.nn.sigmoid(prev_ref[:, OFF_GA:OFF_GB] + bga), ga,
                                  jax.nn.sigmoid(next_ref[:, OFF_GA:OFF_GB] + bga)], axis=0)
        dya_ext = dz_ext * ga_ext
        dyab_ref[...] = dya_ext[mid].astype(BF)
        da_ext = _dot_nt(dya_ext.astype(BF), wpa_ref[...])
        ba_ext = jnp.concatenate([prev_ref[:, 0:OFF_CA], ba, next_ref[:, 0:OFF_CA]], axis=0)
        dcv_ext = da_ext * ba_ext
        dcv = dcv_ext[mid]
        dch = (cw_ref[0:1, :] * pltpu.roll(dcv_ext, te - 1, 0)[mid] + cw_ref[1:2, :] * dcv
               + cw_ref[2:3, :] * pltpu.roll(dcv_ext, 1, 0)[mid])
        dp_ref[:, 0:OFF_CA] = (da_ext[mid] * cv).astype(BF)
        dp_ref[:, OFF_CA:OFF_HA] = (dch * ha).astype(BF)
        dp_ref[:, OFF_HA:OFF_UB] = (dch * ca).astype(BF)
        dcw_ref[0:1, :] += _colsum(dcv * ch_m1)
        dcw_ref[1:2, :] += _colsum(dcv * ch)
        dcw_ref[2:3, :] += _colsum(dcv * ch_p1)

        dz = dz_ext[mid]
        dga = dz * ya * ga * (1.0 - ga)
        dgb = dz * yb * gb * (1.0 - gb)
        dp_ref[:, OFF_GA:OFF_GB] = dga.astype(BF)
        dp_ref[:, OFF_GB:N_PROJ] = dgb.astype(BF)
        dbg_ref[0:1, 0:D] += _colsum(dga)
        dbg_ref[0:1, D:2 * D] += _colsum(dgb)

        dybb_ref[...] = (dz * gb).astype(BF)
        dbb = _dot_nt(dybb_ref[...], wpb_ref[...])
        dp_ref[:, OFF_UB:OFF_VB] = (dbb * mixed_ref[...] * _gelu_grad(ub, tu)).astype(BF)
        dmixed = dbb * gu
        dmixb_ref[...] = dmixed.astype(BF)
        for c in range(tm // CHUNK):
            rows = slice(c * CHUNK, (c + 1) * CHUNK)
            dbs_ref[...] += dmixed[rows]
            for h in range(N_HEAD):
                cols = slice(h * CHUNK, (h + 1) * CHUNK)
                dws_ref[h] += _dot_nt(dmixb_ref[rows, cols], vnb_ref[rows, cols])
                dvn_ref[rows, cols] = _dot(wst_ref[h], dmixb_ref[rows, cols])
        dvn = dvn_ref[...]
        dvgb_ref[0:1, :] += _colsum(dvn * xhv)
        dvgb_ref[1:2, :] += _colsum(dvn)
        dgv = _ln_bwd(dvn, vg_ref[...], xhv, rstdv)
        dp_ref[:, OFF_VB:OFF_GA] = (dgv * _gelu_grad(vb_pre, tv)).astype(BF)

    tile = lambda w: pl.BlockSpec((tm, w), lambda i: (i, 0))
    acc = lambda *s: pl.BlockSpec(s, lambda i: (0,) * len(s))
    return _host_call(
        body, comm, name="mix_bwd", grid=(nt,),
        in_specs=[tile(N_PROJ), *_edge_specs(t, tm, N_PROJ), tile(D), *_edge_specs(t, tm, D), tile(D), tile(D),
                  _resident((W_A, D)), _resident((W_B, D)), _resident((D, D)), _resident((N_HEAD, CHUNK, CHUNK)),
                  _resident((N_HEAD, CHUNK, CHUNK)), _resident((CHUNK, W_B)), _resident((1, 2 * D)),
                  _resident((3, W_A)), _resident((1, W_B)), _resident((1, W_B))],
        out_specs=[tile(N_PROJ), tile(W_A), tile(W_B), tile(D), tile(D), tile(D),
                   acc(8, 2 * D), acc(8, W_A), acc(8, W_B), acc(N_HEAD, CHUNK, CHUNK), acc(CHUNK, W_B)],
        out_shape=[jax.ShapeDtypeStruct((t, N_PROJ), BF), jax.ShapeDtypeStruct((t, W_A), BF),
                   jax.ShapeDtypeStruct((t, W_B), BF), jax.ShapeDtypeStruct((t, D), BF), jax.ShapeDtypeStruct((t, D), BF),
                   jax.ShapeDtypeStruct((t, D), BF), jax.ShapeDtypeStruct((8, 2 * D), F32),
                   jax.ShapeDtypeStruct((8, W_A), F32), jax.ShapeDtypeStruct((8, W_B), F32),
                   jax.ShapeDtypeStruct((N_HEAD, CHUNK, CHUNK), F32), jax.ShapeDtypeStruct((CHUNK, W_B), F32)],
        scratch_shapes=[pltpu.VMEM((tm, W_B), BF), pltpu.VMEM((tm, W_B), F32), pltpu.VMEM((tm, W_B), BF),
                        pltpu.VMEM((tm, W_B), F32)],
        args=(p, pedge, pedge, dr1, dedge, dedge, ya, yb, wpa, wpb, wo, wsb, wstb, bsf, bg, cw, vg, vb))


def _add_own_half(full4, recv4, place, rb, name, after=None):
    n, rh, cols = recv4.shape
    nb = rh // rb

    def body(pl_ref, a_ref, b_ref, *rest):
        own_ref, ob_ref = rest[-2:]
        s = a_ref[...] + b_ref[...]
        ob_ref[...] = s.astype(BF)

        @pl.when(pl.program_id(1) == pl_ref[0])
        def _():
            own_ref[...] = s

    blk = (None, rb, cols)
    return _pc(
        body, name=name,
        grid_spec=pltpu.PrefetchScalarGridSpec(
            num_scalar_prefetch=1, grid=(nb, n),
            in_specs=[pl.BlockSpec(blk, lambda i, k, s: (k, s[1] * nb + i, 0)), pl.BlockSpec(blk, lambda i, k, s: (k, i, 0))]
            + ([] if after is None else [pl.BlockSpec(memory_space=pl.ANY)]),
            out_specs=[pl.BlockSpec((rb, cols), lambda i, k, s: (i, 0)), pl.BlockSpec(blk, lambda i, k, s: (k, i, 0))]),
        out_shape=[jax.ShapeDtypeStruct((rh, cols), F32), jax.ShapeDtypeStruct(recv4.shape, BF)],
        compiler_params=_params("arbitrary", "arbitrary"),
    )(place, full4, recv4, *(() if after is None else (after,)))


def _add_chips(own, r3, place, rb, name):
    _, rh, cols = r3.shape
    nb = rh // rb

    def body(pl_ref, s_ref, r_ref, o_ref):
        o_ref[...] = ((s_ref[...] + r_ref[0].astype(F32)) + r_ref[1].astype(F32)) + r_ref[2].astype(F32)

    return _pc(
        body, name=name,
        grid_spec=pltpu.PrefetchScalarGridSpec(
            num_scalar_prefetch=1, grid=(nb,),
            in_specs=[pl.BlockSpec((rb, cols), lambda i, s: (i, 0)), pl.BlockSpec((3, rb, cols), lambda i, s: (0, i, 0))],
            out_specs=pl.BlockSpec((rb, cols), lambda i, s: (s[1] * nb + i, 0))),
        out_shape=jax.ShapeDtypeStruct((2 * rh, cols), F32),
        compiler_params=_params("arbitrary"),
    )(place, own, r3)


def _add_small(a, b):
    def body(a_ref, b_ref, o_ref):
        o_ref[...] = a_ref[...] + b_ref[...]

    return _pc(body, name="add_small_cores", out_shape=jax.ShapeDtypeStruct(a.shape, F32))(a, b)


def _sum_small_chips(own, slots, place):
    def body(pl_ref, own_ref, s_ref, o_ref):
        j = pl_ref[0]

        def term(k):
            return jnp.where(j == k, own_ref[...], s_ref[jnp.maximum((j ^ k) - 1, 0)])

        o_ref[...] = ((term(0) + term(1)) + term(2)) + term(3)

    vmem = pl.BlockSpec(memory_space=pltpu.VMEM)
    return _pc(body, name="sum_small_chips", in_specs=[pl.BlockSpec(memory_space=pltpu.SMEM), vmem, vmem], out_specs=vmem,
               out_shape=jax.ShapeDtypeStruct(own.shape, F32))(place, own, slots)


def _adamw_step(w, g, m, v):
    m2 = ADAM_B1 * m + (1.0 - ADAM_B1) * g
    v2 = ADAM_B2 * v + (1.0 - ADAM_B2) * (g * g)
    m_hat = m2 / (1.0 - ADAM_B1 ** ADAM_STEP)
    v_hat = v2 / (1.0 - ADAM_B2 ** ADAM_STEP)
    return -ADAM_LR * (m_hat / (jnp.sqrt(v_hat) + ADAM_EPS) + ADAM_WD * w), m2, v2


def _adamw(w, g, m, v, rb, name):
    rows, cols = w.shape

    def body(w_ref, g_ref, m_ref, v_ref, d_ref, m2_ref, v2_ref):
        d_ref[...], m2_ref[...], v2_ref[...] = _adamw_step(w_ref[...], g_ref[...], m_ref[...], v_ref[...])

    blk = pl.BlockSpec((rb, cols), lambda i: (i, 0))
    return _pc(body, name=name, grid=(rows // rb,), in_specs=[blk] * 4, out_specs=[blk] * 3,
               out_shape=[jax.ShapeDtypeStruct((rows, cols), F32)] * 3, compiler_params=_params("arbitrary"))(w, g, m, v)


SC_TILES = 32
SC_LANES = 16
SC_ROWS = 16


def _adamw_sc(ws, gs, ms, vs):
    n = len(ws)
    rows, cols = ws[0].shape
    per_tile = rows // SC_TILES

    def body(*refs):
        ins, outs, (wb, gb, mb, vb, db) = refs[:4 * n], refs[4 * n:7 * n], refs[7 * n:]
        tile = lax.axis_index("sc_subcore") * 2 + lax.axis_index("sc_core")
        for i in range(n):
            for ps in range(per_tile // SC_ROWS):
                slab = pl.ds(tile * per_tile + ps * SC_ROWS, SC_ROWS)
                for k, buf in enumerate((wb, gb, mb, vb)):
                    pltpu.sync_copy(ins[k * n + i].at[slab, :], buf)

                @pl.loop(0, SC_ROWS)
                def _(r):
                    @pl.loop(0, cols, step=SC_LANES)
                    def _(c):
                        at = (r, pl.ds(c, SC_LANES))
                        db[at], mb[at], vb[at] = _adamw_step(wb[at], gb[at], mb[at], vb[at])

                for k, buf in enumerate((db, mb, vb)):
                    pltpu.sync_copy(buf, outs[k * n + i].at[slab, :])

    outs = pl.kernel(
        body, name="adamw_sc", out_type=[jax.ShapeDtypeStruct((rows, cols), F32)] * (3 * n),
        mesh=plsc.VectorSubcoreMesh(core_axis_name="sc_core", subcore_axis_name="sc_subcore"),
        scratch_types=[pltpu.VMEM((SC_ROWS, cols), F32)] * 5,
    )(*ws, *gs, *ms, *vs)
    return outs[:n], outs[n:2 * n], outs[2 * n:]


def _adamw_small(ws, gs, ms, vs):
    n = len(ws)

    def body(*refs):
        ins, outs = refs[:4 * n], refs[4 * n:]
        for i in range(n):
            outs[i][...], outs[n + i][...], outs[2 * n + i][...] = _adamw_step(*(ins[k * n + i][...] for k in range(4)))

    outs = _pc(body, name="adamw_small", out_shape=[jax.ShapeDtypeStruct(w.shape, F32) for w in ws] * 3)(*ws, *gs, *ms, *vs)
    return outs[:n], outs[n:2 * n], outs[2 * n:]


LANES = 128
SMALL_GRADS = (("b_gate", 2 * D), ("conv_w", 3 * W_A), ("v_norm_g", W_B), ("v_norm_b", W_B),
               ("w_s", N_HEAD * CHUNK * CHUNK), ("b_s", N_HEAD * CHUNK), ("ln1_g", D), ("ln1_b", D), ("ln2_g", D), ("ln2_b", D),
               ("loss", 1))


def _pack_rows(parts):
    rows = []
    for a in parts:
        a = a.reshape(-1)
        a = jnp.pad(a, (0, (-a.shape[0]) % LANES))
        rows.append(a.reshape(-1, LANES))
    out = jnp.concatenate(rows, axis=0)
    return jnp.pad(out, ((0, (-out.shape[0]) % 8), (0, 0)))


def _unpack_rows(buf, sizes):
    out, r = [], 0
    for n in sizes:
        nr = -(-n // LANES)
        out.append(buf[r:r + nr].reshape(-1)[:n])
        r += nr
    return out


TM_PROJ = 1024
TM_MIX = 256
TM_DX = 512
DX_PAIR = 6
TK_DW = 4096
TK_DW_IN = 2048
TK_DW_PROJ = 1024
ADD_BLOCK_BYTES = 3 * 1024 * 1024
RB_ADAM = 128
CONV_ROWS = 8


def _add_rows(rows, cols):
    while rows * cols * 4 > ADD_BLOCK_BYTES and rows % 32 == 0:
        rows //= 2
    return rows


def _reduce_adds_1(grads, recvs, place, tag, after=None):
    out = [_add_own_half(g, r, place, _add_rows(*r.shape[1:]), f"add_cores_{tag}{a}", after) for a, (g, r) in enumerate(zip(grads, recvs))]
    return [o[0] for o in out], [o[1] for o in out]


def _reduce_adds_2(sums, recvs, place, tag):
    return [_add_chips(s, r, place, _add_rows(*r.shape[1:]), f"add_chips_{tag}{a}") for a, (s, r) in enumerate(zip(sums, recvs))]


def kernel(x, w_in, b_gate, conv_w, v_norm_g, v_norm_b, w_s, b_s, w_pa, w_pb, w_o, ln1_g, ln1_b, w_ff1, w_ff2, ln2_g, ln2_b, loss_target, m_w_in, m_b_gate, m_conv_w, m_v_norm_g, m_v_norm_b, m_w_s, m_b_s, m_w_pa, m_w_pb, m_w_o, m_ln1_g, m_ln1_b, m_w_ff1, m_w_ff2, m_ln2_g, m_ln2_b, v_w_in, v_b_gate, v_conv_w, v_v_norm_g, v_v_norm_b, v_w_s, v_b_s, v_w_pa, v_w_pb, v_w_o, v_ln1_g, v_ln1_b, v_w_ff1, v_w_ff2, v_ln2_g, v_ln2_b):
    t = x.shape[1]
    core = lax.axis_index("c").astype(jnp.int32).reshape(1)
    chip_idx = 2 * lax.axis_index("x") + lax.axis_index("y")
    chip = chip_idx.astype(jnp.int32).reshape(1)
    place = jnp.concatenate([chip, core])
    x2 = x.reshape(t, D)
    tgt = loss_target.reshape(t, D)

    win4, proj4, ff14, ff24 = _cast_shards(w_in[0], w_pa[0], w_pb[0], w_o[0], w_ff1[0], w_ff2[0], chip)
    conv4 = lax.dynamic_update_slice(jnp.zeros((N_CHIP, CONV_ROWS, W_A // N_CHIP), F32),
                                     jnp.pad(conv_w[0], ((0, CONV_ROWS - 3), (0, 0)))[None], (chip_idx, 0, 0))
    (p, pedge), (win4, proj4, ff14, ff24, conv4) = _proj_fwd(
        x2, chip, TM_PROJ, _gather_comm([win4, proj4, ff14, ff24], conv4, eager=1))

    def full(name, rows_total):
        off, rows = PROJ_OFF[name]
        return proj4[:, off:off + rows, :].reshape(rows_total, D)

    wpa, wpb, wo = full("w_pa", W_A), full("w_pb", W_B), full("w_o", D)
    wff2 = ff24.reshape(D_FF, D)
    cw = jnp.transpose(conv4[:, :3, :], (1, 0, 2)).reshape(3, W_A)
    wsb = w_s[0].astype(BF)
    wstb = jnp.swapaxes(w_s[0], 1, 2).astype(BF)
    bsf = jnp.repeat(jnp.transpose(b_s[0]), CHUNK, axis=1)

    r1, ya, yb = _mix_fwd(p, pedge, x2, wpa, wpb, wo, wsb, bsf, b_gate, cw, v_norm_g, v_norm_b, TM_MIX)
    dr1, dr1b, dedge, x1b, hidb, dh1b, dr2b, acc = _ffn_fwd_bwd(r1, tgt, ff14, wff2, ln1_g, ln1_b, ln2_g, ln2_b, TM_MIX)
    g_ff = [_dw(x1b, dh1b, N_CHIP, D, FF_SHARD, False, True, TK_DW, "dw_ff1"),
            _dw(hidb, dr2b, N_CHIP, FF_SHARD, D, True, False, TK_DW, "dw_ff2")]
    (dp, ab, bbb, zb, dyab, dybb, dbg, dcw, dvgb, dws, dbs_sum), r_ff = _mix_bwd(
        p, pedge, dr1, dedge, ya, yb, wpa, wpb, wo, wsb, wstb, bsf, b_gate, cw, v_norm_g, v_norm_b, TM_MIX, _pair_comm(g_ff))
    s_ff, sb_ff = _reduce_adds_1(g_ff, r_ff, place, "ff")
    dwin4, c_ff = _dw(x2, dp, N_CHIP, D, NP_SHARD, False, True, TK_DW_IN, "dw_in", _chips_comm(sb_ff))
    f_ff = _reduce_adds_2(s_ff, c_ff, place, "ff")
    dproj4, (g_ff1, g_ff2) = _dw_proj(ab, dyab, bbb, dybb, zb, dr1b, TK_DW_PROJ, _join_comm(f_ff))
    dbs = jnp.transpose(jnp.sum(dbs_sum.reshape(CHUNK, N_HEAD, CHUNK), axis=-1))
    small = _pack_rows([dbg[0], dcw[0:3], dvgb[0], dvgb[1], dws, dbs, acc[0], acc[1], acc[2], acc[3],
                        0.5 * jnp.sum(acc[4]) / D])
    g_rest = [dwin4, dproj4]
    nblk = t // TM_DX
    n_a = max(1, min(DX_PAIR, nblk // 4))
    r_rest = _sequencer_call(_pair_comm(g_rest, small), "pair_rest", 1, _sibling_peer)
    dx, _ = _dx(dp, win4, dr1, TM_DX, 0, n_a, None, "dx_a", None)
    s_rest, sb_rest = _reduce_adds_1(g_rest, r_rest[:2], place, "rest", after=dx)
    csmall = _add_small(small, r_rest[2])
    c_rest = _sequencer_call(_chips_comm(sb_rest, csmall), "chips_rest", 2, _chip_peers)
    dx, _ = _dx(dp, win4, dr1, TM_DX, n_a, nblk - n_a, dx, "dx_b", None, after=sb_rest[0])
    f_rest = _reduce_adds_2(s_rest, c_rest[:2], place, "rest")
    gsmall = _sum_small_chips(csmall, c_rest[2], place)
    g_in, g_proj = _comm_call(_join_comm(f_rest), "join_rest")

    grads = {"w_in": g_in, "w_ff1": g_ff1, "w_ff2": g_ff2}
    for name, _ in PROJ_ROWS:
        off, rows = PROJ_OFF[name]
        grads[name] = g_proj[off:off + rows, :]
    for (name, n), flat in zip(SMALL_GRADS, _unpack_rows(gsmall, [n for _, n in SMALL_GRADS])):
        grads[name] = flat
    loss = grads.pop("loss").reshape(())
    grads["conv_w"] = lax.dynamic_slice(grads["conv_w"].reshape(3, W_A), (0, chip_idx * (W_A // N_CHIP)), (3, W_A // N_CHIP))

    weights = dict(w_in=w_in, b_gate=b_gate, conv_w=conv_w, v_norm_g=v_norm_g, v_norm_b=v_norm_b, w_s=w_s, b_s=b_s,
                   w_pa=w_pa, w_pb=w_pb, w_o=w_o, ln1_g=ln1_g, ln1_b=ln1_b, w_ff1=w_ff1, w_ff2=w_ff2, ln2_g=ln2_g, ln2_b=ln2_b)
    mom1 = dict(w_in=m_w_in, b_gate=m_b_gate, conv_w=m_conv_w, v_norm_g=m_v_norm_g, v_norm_b=m_v_norm_b, w_s=m_w_s,
                b_s=m_b_s, w_pa=m_w_pa, w_pb=m_w_pb, w_o=m_w_o, ln1_g=m_ln1_g, ln1_b=m_ln1_b, w_ff1=m_w_ff1,
                w_ff2=m_w_ff2, ln2_g=m_ln2_g, ln2_b=m_ln2_b)
    mom2 = dict(w_in=v_w_in, b_gate=v_b_gate, conv_w=v_conv_w, v_norm_g=v_v_norm_g, v_norm_b=v_v_norm_b, w_s=v_w_s,
                b_s=v_b_s, w_pa=v_w_pa, w_pb=v_w_pb, w_o=v_w_o, ln1_g=v_ln1_g, ln1_b=v_ln1_b, w_ff1=v_w_ff1,
                w_ff2=v_w_ff2, ln2_g=v_ln2_g, ln2_b=v_ln2_b)
    order = list(weights)
    big = ("w_in", "w_pa", "w_pb", "w_o", "w_ff1", "w_ff2")
    delta, new_m, new_v = {}, {}, {}
    early = ("w_ff1", "w_ff2")
    ds, ms, vs = _adamw_sc(*([d[n][0] if d is not grads else d[n] for n in early] for d in (weights, grads, mom1, mom2)))
    for name, d_, m_, v_ in zip(early, ds, ms, vs):
        delta[name], new_m[name], new_v[name] = d_, m_, v_
    for name in big:
        if name in early:
            continue
        w2 = weights[name][0]
        delta[name], new_m[name], new_v[name] = _adamw(w2, grads[name], mom1[name][0], mom2[name][0], RB_ADAM, "adamw_" + name)
    little = [n for n in order if n not in big]
    flat2d = lambda a: a.reshape(-1, a.shape[-1])
    ds, ms, vs = _adamw_small(*([flat2d(d[n].reshape(weights[n].shape)) for n in little] for d in (weights, grads, mom1, mom2)))
    for name, d_, m_, v_ in zip(little, ds, ms, vs):
        delta[name], new_m[name], new_v[name] = d_, m_, v_

    shaped = lambda d: [d[n].reshape(weights[n].shape) for n in order]
    return (loss, dx.reshape(x.shape), *shaped(grads), *shaped(delta), *shaped(new_m), *shaped(new_v))
```

```python
import functools
from typing import NamedTuple

import jax
import jax.numpy as jnp
from jax import lax
from jax.experimental import pallas as pl
from jax.experimental.pallas import tpu as pltpu
from jax.experimental.pallas import tpu_sc as plsc

D = 1024
W_A = 1536
W_B = 1024
CHUNK = 128
N_HEAD = 8
D_FF = 4096
N_PROJ = 3 * W_A + 2 * W_B + 2 * D
OFF_CA, OFF_HA, OFF_UB, OFF_VB, OFF_GA, OFF_GB = 1536, 3072, 4608, 5632, 6656, 7680
LN_EPS = 1e-5
ALPHA = 2.0 ** 0.25
N_CHIP = 4
NP_SHARD = N_PROJ // N_CHIP
FF_SHARD = D_FF // N_CHIP
ADAM_LR, ADAM_B1, ADAM_B2, ADAM_EPS, ADAM_WD, ADAM_STEP = 0.001, 0.9, 0.999, 1e-08, 0.01, 10

PROJ_ROWS = (("w_pa", W_A // N_CHIP), ("w_pb", W_B // N_CHIP), ("w_o", D // N_CHIP))
PROJ_OFF = {}
_o = 0
for _n, _r in PROJ_ROWS:
    PROJ_OFF[_n] = (_o, _r)
    _o += _r
PROJ_TOTAL = _o

V7X_VMEM_BYTES = 64 * 1024 * 1024
VMEM_LIMIT = V7X_VMEM_BYTES - 8 * 1024 * 1024
HALO = 8
EDGE_TILE = 256

BF = jnp.bfloat16
F32 = jnp.float32
MESH = pl.DeviceIdType.MESH
HBM_SPEC = pl.BlockSpec(memory_space=pltpu.HBM)


def _pc(body, **kw):
    return pl.pallas_call(body, **kw)


def _params(*sem):
    return pltpu.CompilerParams(dimension_semantics=sem, vmem_limit_bytes=VMEM_LIMIT)


def _resident(shape):
    n = len(shape)
    return pl.BlockSpec(shape, lambda *_: (0,) * n, pipeline_mode=pl.Buffered(1))


def _dot(a, b):
    return jnp.dot(a, b, preferred_element_type=F32)


def _dot_nt(a, b):
    return lax.dot_general(a, b, (((1,), (1,)), ((), ())), preferred_element_type=F32)


def _dot_tn(a, b):
    return lax.dot_general(a, b, (((0,), (0,)), ((), ())), preferred_element_type=F32)


def _gelu(x):
    t = jnp.tanh(0.7978845608028654 * (x + 0.044715 * (x * x * x)))
    return 0.5 * x * (1.0 + t), t


def _gelu_grad(x, t):
    return 0.5 * (1.0 + t) + 0.5 * x * (1.0 - t * t) * (0.7978845608028654 * (1.0 + 0.134145 * (x * x)))


def _ln_stats(r):
    mu = jnp.mean(r, axis=-1, keepdims=True)
    xc = r - mu
    var = jnp.mean(xc * xc, axis=-1, keepdims=True)
    rstd = lax.rsqrt(var + LN_EPS)
    return xc * rstd, rstd


def _ln_bwd(dy, g, xh, rstd):
    dxh = dy * g
    m1 = jnp.mean(dxh, axis=-1, keepdims=True)
    m2 = jnp.mean(dxh * xh, axis=-1, keepdims=True)
    return rstd * (dxh - m1 - xh * m2)


def _colsum(v):
    return jnp.sum(v, axis=0, keepdims=True)


class _Comm(NamedTuple):
    args: tuple
    out_shape: tuple
    aliases: dict
    n_sems: int
    stages: tuple


def _place():
    x, y, c = lax.axis_index("x"), lax.axis_index("y"), lax.axis_index("c")
    return x, y, c, 2 * x + y


def _flip(x, y, c, r):
    return (x ^ (r >> 1), y ^ (r & 1), c)


def _remote(src, dst, send_sems, recv_sems, k, peer):
    return pltpu.make_async_remote_copy(src_ref=src, dst_ref=dst, send_sem=send_sems.at[k], recv_sem=recv_sems.at[k],
                                        device_id=peer, device_id_type=MESH)


def _host_call(body, comm, *, name, grid, in_specs, out_specs, out_shape, args, scratch_shapes=(), aliases=None, prefetch=None,
               body_reads_comm=False):
    sem = ("arbitrary",) * len(grid)
    aliases = dict(aliases or {})
    n_pre = 0 if prefetch is None else 1
    n_in, n_out, n_scr = len(in_specs), len(out_specs), len(scratch_shapes)
    c_in, c_out = (0, 0) if comm is None else (len(comm.args), len(comm.out_shape))
    steps = {"first": (0,) * len(grid), "late": (grid[0] - 1,) + (0,) * (len(grid) - 1), "last": tuple(g - 1 for g in grid)}

    def wrapped(*refs):
        refs = refs[n_pre:]
        own_in, cin = refs[:n_in], refs[n_in:n_in + c_in]
        o0 = n_in + c_in
        own_out, cout = refs[o0:o0 + n_out], refs[o0 + n_out:o0 + n_out + c_out]
        s0 = o0 + n_out + c_out
        scr, sems = refs[s0:s0 + n_scr], refs[s0 + n_scr:]

        def run(before):
            for phase, fn in () if comm is None else comm.stages:
                at_step = isinstance(phase, tuple)
                if before != (at_step or phase == "first"):
                    continue
                step = phase if at_step else steps[phase]
                cond = pl.program_id(0) == step[0]
                for d in range(1, len(grid)):
                    cond = jnp.logical_and(cond, pl.program_id(d) == step[d])
                pl.when(cond)(functools.partial(fn, cin, cout, *sems))

        run(True)
        if body_reads_comm:
            body(*own_in, *own_out, *scr, comm_refs=cout)
        else:
            body(*own_in, *own_out, *scr)
        run(False)

    in_specs = list(in_specs) + [HBM_SPEC] * c_in
    out_specs = list(out_specs) + [HBM_SPEC] * c_out
    out_shape = list(out_shape) + ([] if comm is None else list(comm.out_shape))
    scratch_shapes = list(scratch_shapes) + ([] if comm is None else [pltpu.SemaphoreType.DMA((comm.n_sems,))] * 2)
    args = tuple(args) + (() if comm is None else tuple(comm.args))
    if comm is not None:
        aliases.update({n_in + i: n_out + o for i, o in comm.aliases.items()})
    aliases = {i + n_pre: o for i, o in aliases.items()}
    if prefetch is None:
        kw = dict(grid=grid, in_specs=in_specs, out_specs=out_specs, scratch_shapes=scratch_shapes)
    else:
        kw = dict(grid_spec=pltpu.PrefetchScalarGridSpec(num_scalar_prefetch=1, grid=grid, in_specs=in_specs,
                                                         out_specs=out_specs, scratch_shapes=scratch_shapes))
        args = (prefetch,) + args
    outs = _pc(wrapped, name=name, out_shape=out_shape, input_output_aliases=aliases, compiler_params=_params(*sem), **kw)(*args)
    return outs[:n_out], outs[n_out:]


def _comm_call(comm, name):
    def body(*refs):
        c_in, c_out = len(comm.args), len(comm.out_shape)
        cin, cout, (send_sems, recv_sems) = refs[:c_in], refs[c_in:c_in + c_out], refs[c_in + c_out:]
        for phase in ("first", "late", "last"):
            for ph, fn in comm.stages:
                if ph == phase:
                    fn(cin, cout, send_sems, recv_sems)

    return _pc(body, name=name, in_specs=[HBM_SPEC] * len(comm.args), out_specs=[HBM_SPEC] * len(comm.out_shape),
               out_shape=list(comm.out_shape), scratch_shapes=[pltpu.SemaphoreType.DMA((comm.n_sems,))] * 2,
               input_output_aliases=dict(comm.aliases))(*comm.args)


def _sequencer_call(comm, name, collective_id, peers_of):
    hbm = pltpu.MemorySpace.HBM
    cin = [jax.new_ref(a, memory_space=hbm) for a in comm.args]
    cout = [jax.empty_ref(jax.ShapeDtypeStruct(o.shape, o.dtype), memory_space=hbm) for o in comm.out_shape]

    @pl.kernel(mesh=plsc.ScalarSubcoreMesh(axis_name="sequencer", num_cores=1), name=name,
               scratch_types=(pltpu.SemaphoreType.DMA((comm.n_sems,)), pltpu.SemaphoreType.DMA((comm.n_sems,))),
               compiler_params=pltpu.CompilerParams(collective_id=collective_id))
    def launch(send_sems, recv_sems):
        barrier = pltpu.get_barrier_semaphore()
        peers = peers_of()
        for peer in peers:
            pl.semaphore_signal(barrier, inc=1, device_id=peer, device_id_type=MESH)
        pl.semaphore_wait(barrier, len(peers))
        for phase in ("first", "late", "last"):
            for ph, fn in comm.stages:
                if ph == phase:
                    fn(cin, cout, send_sems, recv_sems)

    launch()
    return [r[...] for r in cout]


def _sibling_peer():
    x, y, c, _ = _place()
    return [(x, y, 1 - c)]


def _chip_peers():
    x, y, c, _ = _place()
    return [_flip(x, y, c, r) for r in (1, 2, 3)]


def _gather_comm(bufs, whole=None, eager=0):
    n = len(bufs)
    halves = [b.shape[1] // 2 for b in bufs]
    k_ici = lambda a, r: 3 * a + r - 1
    k_d2d = lambda a, r: 3 * n + 3 * a + r - 1
    k_whole = lambda r: 6 * n + r - 1

    def half(ref, slot, c, a):
        return ref.at[slot, pl.ds(c * halves[a], halves[a])]

    def send(cin, cout, ss, rs):
        x, y, c, j = _place()
        for a in range(n):
            mine = half(cout[a], j, c, a)
            for r in (1, 2, 3):
                _remote(mine, mine, ss, rs, k_ici(a, r), _flip(x, y, c, r)).start()
        if whole is not None:
            for r in (1, 2, 3):
                _remote(cout[n].at[j], cout[n].at[j], ss, rs, k_whole(r), _flip(x, y, c, r)).start()

    def pass_on(cout, ss, rs, a, r):
        x, y, c, j = _place()
        landed = half(cout[a], j ^ r, c, a)
        _remote(landed, landed, ss, rs, k_ici(a, r), (x, y, 1 - c)).wait_recv()
        _remote(landed, landed, ss, rs, k_d2d(a, r), (x, y, 1 - c)).start()

    def passed_on(cout, ss, rs, a, r):
        x, y, c, j = _place()
        theirs = half(cout[a], j ^ r, 1 - c, a)
        _remote(theirs, theirs, ss, rs, k_d2d(a, r), (x, y, 1 - c)).wait_recv()

    def arrive(r, cin, cout, ss, rs):
        for a in range(eager):
            pass_on(cout, ss, rs, a, r)
        for a in range(eager):
            passed_on(cout, ss, rs, a, r)

    def forward(cin, cout, ss, rs):
        for a in range(eager, n):
            for r in (1, 2, 3):
                pass_on(cout, ss, rs, a, r)

    def finish(cin, cout, ss, rs):
        x, y, c, j = _place()
        sibling = (x, y, 1 - c)
        for a in range(eager, n):
            for r in (1, 2, 3):
                passed_on(cout, ss, rs, a, r)
        for a in range(n):
            mine = half(cout[a], j, c, a)
            for r in (1, 2, 3):
                _remote(mine, mine, ss, rs, k_ici(a, r), sibling).wait_send()
                landed = half(cout[a], j ^ r, c, a)
                _remote(landed, landed, ss, rs, k_d2d(a, r), sibling).wait_send()
        if whole is not None:
            for r in (1, 2, 3):
                cp = _remote(cout[n].at[j ^ r], cout[n].at[j ^ r], ss, rs, k_whole(r), sibling)
                cp.wait_recv()
                cp.wait_send()

    args = tuple(bufs) + ((whole,) if whole is not None else ())
    out_shape = tuple(jax.ShapeDtypeStruct(b.shape, b.dtype) for b in args)
    aliases = {a: a for a in range(len(args))}
    arrivals = tuple(((r, 0), functools.partial(arrive, r)) for r in (1, 2, 3)) if eager else ()
    return _Comm(args, out_shape, aliases, 6 * n + 3, (("first", send),) + arrivals + (("late", forward), ("last", finish)))


def _pair_comm(grads, small=None):
    n = len(grads)
    halves = [g.shape[1] // 2 for g in grads]

    def copies(cin, cout, ss, rs):
        x, y, c, _ = _place()
        sibling = (x, y, 1 - c)
        cps = [_remote(cin[a].at[:, pl.ds((1 - c) * halves[a], halves[a]), :], cout[a], ss, rs, a, sibling) for a in range(n)]
        if small is not None:
            cps.append(_remote(cin[n], cout[n], ss, rs, n, sibling))
        return cps

    def start(cin, cout, ss, rs):
        for cp in copies(cin, cout, ss, rs):
            cp.start()

    def finish(cin, cout, ss, rs):
        for cp in copies(cin, cout, ss, rs):
            cp.wait()

    args = tuple(grads) + ((small,) if small is not None else ())
    out_shape = tuple(jax.ShapeDtypeStruct((N_CHIP, h, g.shape[2]), F32) for g, h in zip(grads, halves))
    out_shape += (jax.ShapeDtypeStruct(small.shape, F32),) if small is not None else ()
    return _Comm(args, out_shape, {}, n + 1, (("first", start), ("last", finish)))


def _chips_comm(sums_bf, small=None):
    n = len(sums_bf)

    def copies(cin, cout, ss, rs):
        x, y, c, j = _place()
        cps = []
        for r in (1, 2, 3):
            peer = _flip(x, y, c, r)
            for a in range(n):
                cps.append(_remote(cin[a].at[j ^ r], cout[a].at[r - 1], ss, rs, (n + 1) * (r - 1) + a, peer))
            if small is not None:
                cps.append(_remote(cin[n], cout[n].at[r - 1], ss, rs, (n + 1) * (r - 1) + n, peer))
        return cps

    def start(cin, cout, ss, rs):
        for cp in copies(cin, cout, ss, rs):
            cp.start()

    def finish(cin, cout, ss, rs):
        for cp in copies(cin, cout, ss, rs):
            cp.wait()

    args = tuple(sums_bf) + ((small,) if small is not None else ())
    out_shape = tuple(jax.ShapeDtypeStruct((3,) + s.shape[1:], BF) for s in sums_bf)
    out_shape += (jax.ShapeDtypeStruct((3,) + small.shape, F32),) if small is not None else ()
    return _Comm(args, out_shape, {}, 3 * (n + 1), (("first", start), ("last", finish)))


def _join_comm(shards):
    n = len(shards)
    halves = [s.shape[0] // 2 for s in shards]

    def start(cin, cout, ss, rs):
        x, y, c, _ = _place()
        for a in range(n):
            mine = cout[a].at[pl.ds(c * halves[a], halves[a]), :]
            _remote(mine, mine, ss, rs, a, (x, y, 1 - c)).start()

    def finish(cin, cout, ss, rs):
        x, y, c, _ = _place()
        for a in range(n):
            theirs = cout[a].at[pl.ds((1 - c) * halves[a], halves[a]), :]
            cp = _remote(theirs, theirs, ss, rs, a, (x, y, 1 - c))
            cp.wait_recv()
            cp.wait_send()

    out_shape = tuple(jax.ShapeDtypeStruct(s.shape, F32) for s in shards)
    return _Comm(tuple(shards), out_shape, {a: a for a in range(n)}, n, (("first", start), ("last", finish)))


def _cast_shards(w_in, w_pa, w_pb, w_o, w_ff1, w_ff2, chip):
    def body(j_ref, win_ref, wpa_ref, wpb_ref, wo_ref, wff1_ref, wff2_ref, win4_ref, proj4_ref, ff14_ref, ff24_ref):
        win4_ref[...] = win_ref[...].astype(BF)
        for name, ref in (("w_pa", wpa_ref), ("w_pb", wpb_ref), ("w_o", wo_ref)):
            off, rows = PROJ_OFF[name]
            proj4_ref[off:off + rows, :] = ref[...].astype(BF)
        ff14_ref[...] = wff1_ref[...].astype(BF)
        ff24_ref[...] = wff2_ref[...].astype(BF)

    whole = lambda a: pl.BlockSpec(a.shape, lambda i, j: (0, 0), pipeline_mode=pl.Buffered(1))
    slot = lambda rows, cols: pl.BlockSpec((None, rows, cols), lambda i, j: (j[0], 0, 0))
    ws = (w_in, w_pa, w_pb, w_o, w_ff1, w_ff2)
    shapes = ((D, NP_SHARD), (PROJ_TOTAL, D), (D, FF_SHARD), (FF_SHARD, D))
    return _pc(
        body, name="cast_shards",
        grid_spec=pltpu.PrefetchScalarGridSpec(num_scalar_prefetch=1, grid=(1,), in_specs=[whole(w) for w in ws],
                                               out_specs=[slot(*s) for s in shapes]),
        out_shape=[jax.ShapeDtypeStruct((N_CHIP,) + s, BF) for s in shapes],
        compiler_params=_params("arbitrary"))(chip, *ws)


def _proj_fwd(x, chip, tm, comm):
    t = x.shape[0]
    sub = tm // EDGE_TILE

    def body(x_ref, p_ref, edge_ref, w_ref, w_sem, comm_refs):
        @pl.when(pl.program_id(1) == 0)
        def _():
            _, _, _, j = _place()
            block = pltpu.make_async_copy(comm_refs[0].at[j ^ pl.program_id(0)], w_ref, w_sem)
            block.start()
            block.wait()

        p_ref[...] = _dot(x_ref[...].astype(BF), w_ref[...])
        _write_edges(edge_ref, p_ref, tm)

    return _host_call(
        body, comm, name="proj_fwd", grid=(N_CHIP, t // tm), prefetch=chip, body_reads_comm=True,
        in_specs=[pl.BlockSpec((tm, D), lambda r, i, j: (i, 0))],
        out_specs=[pl.BlockSpec((tm, NP_SHARD), lambda r, i, j: (i, j[0] ^ r)),
                   pl.BlockSpec((sub, 2 * HALO, NP_SHARD), lambda r, i, j: (i, 0, j[0] ^ r))],
        out_shape=[jax.ShapeDtypeStruct((t, N_PROJ), F32), jax.ShapeDtypeStruct((t // EDGE_TILE, 2 * HALO, N_PROJ), F32)],
        scratch_shapes=[pltpu.VMEM((D, NP_SHARD), BF), pltpu.SemaphoreType.DMA],
        args=(x,))


def _edge_specs(t, tm, w):
    k, last = tm // EDGE_TILE, t // EDGE_TILE - 1
    return [pl.BlockSpec((None, HALO, w), lambda i: (jnp.maximum(i * k - 1, 0), 1, 0)),
            pl.BlockSpec((None, HALO, w), lambda i: (jnp.minimum((i + 1) * k, last), 0, 0))]


def _write_edges(edge_ref, rows, tm):
    for s in range(tm // EDGE_TILE):
        edge_ref[s, 0:HALO, :] = rows[s * EDGE_TILE:s * EDGE_TILE + HALO, :]
        edge_ref[s, HALO:2 * HALO, :] = rows[(s + 1) * EDGE_TILE - HALO:(s + 1) * EDGE_TILE, :]


def _pcols(p_ref, lo, hi):
    return p_ref[:, lo:hi]


def _end_masks(nt):
    i = pl.program_id(0)
    return (i > 0).astype(F32), (i < nt - 1).astype(F32)


def _conv_fwd(p_ref, prev_ref, next_ref, cw_ref, tm, has_prev, has_next):
    ca = _pcols(p_ref, OFF_CA, OFF_HA)
    ha = _pcols(p_ref, OFF_HA, OFF_UB)
    ch = ca * ha
    ch_prev = prev_ref[HALO - 1:HALO, OFF_CA:OFF_HA] * prev_ref[HALO - 1:HALO, OFF_HA:OFF_UB] * has_prev
    ch_next = next_ref[0:1, OFF_CA:OFF_HA] * next_ref[0:1, OFF_HA:OFF_UB] * has_next
    row = lax.broadcasted_iota(jnp.int32, (tm, W_A), 0)
    ch_m1 = jnp.where(row == 0, ch_prev, pltpu.roll(ch, 1, 0))
    ch_p1 = jnp.where(row == tm - 1, ch_next, pltpu.roll(ch, tm - 1, 0))
    cv = cw_ref[0:1, :] * ch_m1 + cw_ref[1:2, :] * ch + cw_ref[2:3, :] * ch_p1
    return ca, ha, ch, ch_m1, ch_p1, cv


def _spatial_fwd(p_ref, vg_ref, vb_ref, ws_ref, bsf_ref, vnb_ref, mixed_ref, tm):
    vb_pre = _pcols(p_ref, OFF_VB, OFF_GA)
    gv, tv = _gelu(vb_pre)
    xhv, rstdv = _ln_stats(gv)
    vnb_ref[...] = (xhv * vg_ref[...] + vb_ref[...]).astype(BF)
    for c in range(tm // CHUNK):
        rows = slice(c * CHUNK, (c + 1) * CHUNK)
        for h in range(N_HEAD):
            cols = slice(h * CHUNK, (h + 1) * CHUNK)
            mixed_ref[rows, cols] = _dot(ws_ref[h], vnb_ref[rows, cols]) + bsf_ref[:, cols]
    return vb_pre, tv, xhv, rstdv


def _mix_fwd(p, pedge, x, wpa, wpb, wo, wsb, bsf, bg, cw, vg, vb, tm):
    t = x.shape[0]
    nt = t // tm

    def body(p_ref, prev_ref, next_ref, x_ref, wpa_ref, wpb_ref, wo_ref, ws_ref, bsf_ref, bg_ref, cw_ref, vg_ref, vb_ref,
             r1_ref, ya_ref, yb_ref, vnb_ref, mixed_ref):
        has_prev, has_next = _end_masks(nt)
        _, _, _, _, _, cv = _conv_fwd(p_ref, prev_ref, next_ref, cw_ref, tm, has_prev, has_next)
        a = _pcols(p_ref, 0, OFF_CA) * cv
        ya = _dot(a.astype(BF), wpa_ref[...])
        ya_ref[...] = ya
        _spatial_fwd(p_ref, vg_ref, vb_ref, ws_ref, bsf_ref, vnb_ref, mixed_ref, tm)
        gu, _ = _gelu(_pcols(p_ref, OFF_UB, OFF_VB))
        bb = gu * mixed_ref[...]
        yb = _dot(bb.astype(BF), wpb_ref[...])
        yb_ref[...] = yb
        ga = jax.nn.sigmoid(_pcols(p_ref, OFF_GA, OFF_GB) + bg_ref[:, 0:D])
        gb = jax.nn.sigmoid(_pcols(p_ref, OFF_GB, N_PROJ) + bg_ref[:, D:2 * D])
        z = ga * ya + gb * yb
        r1_ref[...] = ALPHA * x_ref[...] + _dot(z.astype(BF), wo_ref[...])

    tile = lambda w: pl.BlockSpec((tm, w), lambda i: (i, 0))
    return _pc(
        body, name="mix_fwd", grid=(nt,),
        in_specs=[tile(N_PROJ), *_edge_specs(t, tm, N_PROJ), tile(D),
                  _resident((W_A, D)), _resident((W_B, D)), _resident((D, D)), _resident((N_HEAD, CHUNK, CHUNK)),
                  _resident((CHUNK, W_B)), _resident((1, 2 * D)), _resident((3, W_A)), _resident((1, W_B)),
                  _resident((1, W_B))],
        out_specs=[tile(D), tile(D), tile(D)],
        out_shape=[jax.ShapeDtypeStruct((t, D), F32)] * 3,
        scratch_shapes=[pltpu.VMEM((tm, W_B), BF), pltpu.VMEM((tm, W_B), F32)],
        compiler_params=_params("arbitrary"),
    )(p, pedge, pedge, x, wpa, wpb, wo, wsb, bsf, bg, cw, vg, vb)


def _ffn_fwd_bwd(r1, tgt, wff1, wff2, ln1g, ln1b, ln2g, ln2b, tm):
    t = r1.shape[0]

    def body(r1_ref, tgt_ref, w1_ref, w2_ref, g1_ref, b1_ref, g2_ref, b2_ref,
             dr1_ref, dr1b_ref, dedge_ref, x1b_ref, hidb_ref, dh1b_ref, dr2b_ref, acc_ref, relu_ref):
        @pl.when(pl.program_id(0) == 0)
        def _():
            acc_ref[...] = jnp.zeros_like(acc_ref)

        xh1, rstd1 = _ln_stats(r1_ref[...])
        x1 = xh1 * g1_ref[...] + b1_ref[...]
        x1b_ref[...] = x1.astype(BF)
        ffn = jnp.zeros((tm, D), F32)
        for j in range(N_CHIP):
            cols = slice(j * FF_SHARD, (j + 1) * FF_SHARD)
            r = jnp.maximum(_dot(x1b_ref[...], w1_ref[j]), 0.0)
            relu_ref[:, cols] = r
            hidb_ref[:, cols] = (r * r).astype(BF)
            ffn = ffn + _dot(hidb_ref[:, cols], w2_ref[cols, :])
        xh2, rstd2 = _ln_stats(ALPHA * x1 + ffn)
        diff = xh2 * g2_ref[...] + b2_ref[...] - tgt_ref[...]
        acc_ref[4:5, :] += _colsum(diff * diff)
        dx2 = diff * (1.0 / D)
        acc_ref[2:3, :] += _colsum(dx2 * xh2)
        acc_ref[3:4, :] += _colsum(dx2)
        dr2 = _ln_bwd(dx2, g2_ref[...], xh2, rstd2)
        dr2b_ref[...] = dr2.astype(BF)
        dx1 = ALPHA * dr2
        for j in range(N_CHIP):
            cols = slice(j * FF_SHARD, (j + 1) * FF_SHARD)
            dhid = _dot_nt(dr2b_ref[...], w2_ref[cols, :])
            dh1b_ref[:, cols] = (dhid * (2.0 * relu_ref[:, cols])).astype(BF)
            dx1 = dx1 + _dot_nt(dh1b_ref[:, cols], w1_ref[j])
        acc_ref[0:1, :] += _colsum(dx1 * xh1)
        acc_ref[1:2, :] += _colsum(dx1)
        dr1 = _ln_bwd(dx1, g1_ref[...], xh1, rstd1)
        dr1_ref[...] = dr1
        dr1b_ref[...] = dr1.astype(BF)
        _write_edges(dedge_ref, dr1_ref, tm)

    tile = lambda w: pl.BlockSpec((tm, w), lambda i: (i, 0))
    vec = _resident((1, D))
    return _pc(
        body, name="ffn_fwd_bwd", grid=(t // tm,),
        in_specs=[tile(D), tile(D), _resident((N_CHIP, D, FF_SHARD)), _resident((D_FF, D)), vec, vec, vec, vec],
        out_specs=[tile(D), tile(D), pl.BlockSpec((tm // EDGE_TILE, 2 * HALO, D), lambda i: (i, 0, 0)), tile(D), tile(D_FF),
                   tile(D_FF), tile(D), pl.BlockSpec((8, D), lambda i: (0, 0))],
        out_shape=[jax.ShapeDtypeStruct((t, D), F32), jax.ShapeDtypeStruct((t, D), BF),
                   jax.ShapeDtypeStruct((t // EDGE_TILE, 2 * HALO, D), F32), jax.ShapeDtypeStruct((t, D), BF),
                   jax.ShapeDtypeStruct((t, D_FF), BF), jax.ShapeDtypeStruct((t, D_FF), BF),
                   jax.ShapeDtypeStruct((t, D), BF), jax.ShapeDtypeStruct((8, D), F32)],
        scratch_shapes=[pltpu.VMEM((tm, D_FF), F32)],
        compiler_params=_params("arbitrary"),
    )(r1, tgt, wff1, wff2, ln1g, ln1b, ln2g, ln2b)


def _dw(a, b, nblk, am, bn, a_blocked, b_blocked, tk, name, comm=None):
    t = a.shape[0]

    def body(a_ref, b_ref, o_ref):
        @pl.when(pl.program_id(1) == 0)
        def _():
            o_ref[...] = jnp.zeros_like(o_ref)

        o_ref[...] += _dot_tn(a_ref[...].astype(BF), b_ref[...])

    outs, got = _host_call(
        body, comm, name=name, grid=(nblk, t // tk),
        in_specs=[pl.BlockSpec((tk, am), (lambda j, k: (k, j)) if a_blocked else (lambda j, k: (k, 0))),
                  pl.BlockSpec((tk, bn), (lambda j, k: (k, j)) if b_blocked else (lambda j, k: (k, 0)))],
        out_specs=[pl.BlockSpec((None, am, bn), lambda j, k: (j, 0, 0))],
        out_shape=[jax.ShapeDtypeStruct((nblk, am, bn), F32)], args=(a, b))
    return outs[0] if comm is None else (outs[0], got)


def _dw_proj(ab, dyab, bbb, dybb, zb, dr1b, tk, comm):
    t = ab.shape[0]
    pairs = (("w_pa", 0, 1), ("w_pb", 2, 3), ("w_o", 4, 5))

    def body(*refs):
        o_ref = refs[6]

        @pl.when(pl.program_id(0) == 0)
        def _():
            o_ref[...] = jnp.zeros_like(o_ref)

        for name, ia, ib in pairs:
            off, rows = PROJ_OFF[name]
            for k in range(N_CHIP):
                o_ref[k, off:off + rows, :] += _dot_tn(refs[ia][:, k * rows:(k + 1) * rows], refs[ib][...])

    tile = lambda w: pl.BlockSpec((tk, w), lambda i: (i, 0))
    outs, got = _host_call(
        body, comm, name="dw_proj", grid=(t // tk,), in_specs=[tile(W_A), tile(D), tile(W_B), tile(D), tile(D), tile(D)],
        out_specs=[pl.BlockSpec((N_CHIP, PROJ_TOTAL, D), lambda i: (0, 0, 0))],
        out_shape=[jax.ShapeDtypeStruct((N_CHIP, PROJ_TOTAL, D), F32)], args=(ab, dyab, bbb, dybb, zb, dr1b))
    return outs[0], got


def _dx(dp, win4, dr1, tm, blk0, nblk, filled, name, comm, after=None):
    t = dp.shape[0]

    def body(dp_ref, w_ref, dr1_ref, *rest):
        dx = ALPHA * dr1_ref[...]
        for j in range(N_CHIP):
            dx = dx + _dot_nt(dp_ref[:, j * NP_SHARD:(j + 1) * NP_SHARD], w_ref[j])
        rest[-1][...] = dx

    in_specs = [pl.BlockSpec((tm, N_PROJ), lambda i: (i + blk0, 0)), _resident((N_CHIP, D, NP_SHARD)),
                pl.BlockSpec((tm, D), lambda i: (i + blk0, 0))]
    args = (dp, win4, dr1)
    aliases = None
    if filled is not None:
        in_specs.append(pl.BlockSpec(memory_space=pl.ANY))
        args += (filled,)
        aliases = {3: 0}
    if after is not None:
        in_specs.append(pl.BlockSpec(memory_space=pl.ANY))
        args += (after,)
    outs, got = _host_call(
        body, comm, name=name, grid=(nblk,), in_specs=in_specs, out_specs=[pl.BlockSpec((tm, D), lambda i: (i + blk0, 0))],
        out_shape=[jax.ShapeDtypeStruct((t, D), F32)], args=args, aliases=aliases)
    return outs[0], got


def _mix_bwd(p, pedge, dr1, dedge, ya, yb, wpa, wpb, wo, wsb, wstb, bsf, bg, cw, vg, vb, tm, comm):
    t = p.shape[0]
    nt = t // tm
    te = tm + 2 * HALO
    mid = slice(HALO, HALO + tm)

    def body(p_ref, prev_ref, next_ref, dr1_ref, dprev_ref, dnext_ref, ya_ref, yb_ref, wpa_ref, wpb_ref, wo_ref,
             ws_ref, wst_ref, bsf_ref, bg_ref, cw_ref, vg_ref, vb_ref,
             dp_ref, ab_ref, bbb_ref, zb_ref, dyab_ref, dybb_ref, dbg_ref, dcw_ref, dvgb_ref, dws_ref, dbs_ref,
             vnb_ref, mixed_ref, dmixb_ref, dvn_ref):
        @pl.when(pl.program_id(0) == 0)
        def _():
            for r in (dbg_ref, dcw_ref, dvgb_ref, dws_ref, dbs_ref):
                r[...] = jnp.zeros_like(r)

        has_prev, has_next = _end_masks(nt)
        ca, ha, ch, ch_m1, ch_p1, cv = _conv_fwd(p_ref, prev_ref, next_ref, cw_ref, tm, has_prev, has_next)
        ba = _pcols(p_ref, 0, OFF_CA)
        ab_ref[...] = (ba * cv).astype(BF)
        vb_pre, tv, xhv, rstdv = _spatial_fwd(p_ref, vg_ref, vb_ref, ws_ref, bsf_ref, vnb_ref, mixed_ref, tm)
        ub = _pcols(p_ref, OFF_UB, OFF_VB)
        gu, tu = _gelu(ub)
        bbb_ref[...] = (gu * mixed_ref[...]).astype(BF)
        bga = bg_ref[:, 0:D]
        ga = jax.nn.sigmoid(_pcols(p_ref, OFF_GA, OFF_GB) + bga)
        gb = jax.nn.sigmoid(_pcols(p_ref, OFF_GB, N_PROJ) + bg_ref[:, D:2 * D])
        ya = ya_ref[...]
        yb = yb_ref[...]
        zb_ref[...] = (ga * ya + gb * yb).astype(BF)

        dr1_ext = jnp.concatenate([dprev_ref[...] * has_prev, dr1_ref[...], dnext_ref[...] * has_next], axis=0)
        dz_ext = _dot_nt(dr1_ext.astype(BF), wo_ref[...])
        ga_ext = jnp.concatenate([jax.nn.sigmoid(prev_ref[:, OFF_GA:OFF_GB] + bga), ga,
                                  jax.nn.sigmoid(next_ref[:, OFF_GA:OFF_GB] + bga)], axis=0)
        dya_ext = dz_ext * ga_ext
        dyab_ref[...] = dya_ext[mid].astype(BF)
        da_ext = _dot_nt(dya_ext.astype(BF), wpa_ref[...])
        ba_ext = jnp.concatenate([prev_ref[:, 0:OFF_CA], ba, next_ref[:, 0:OFF_CA]], axis=0)
        dcv_ext = da_ext * ba_ext
        dcv = dcv_ext[mid]
        dch = (cw_ref[0:1, :] * pltpu.roll(dcv_ext, te - 1, 0)[mid] + cw_ref[1:2, :] * dcv
               + cw_ref[2:3, :] * pltpu.roll(dcv_ext, 1, 0)[mid])
        dp_ref[:, 0:OFF_CA] = (da_ext[mid] * cv).astype(BF)
        dp_ref[:, OFF_CA:OFF_HA] = (dch * ha).astype(BF)
        dp_ref[:, OFF_HA:OFF_UB] = (dch * ca).astype(BF)
        dcw_ref[0:1, :] += _colsum(dcv * ch_m1)
        dcw_ref[1:2, :] += _colsum(dcv * ch)
        dcw_ref[2:3, :] += _colsum(dcv * ch_p1)

        dz = dz_ext[mid]
        dga = dz * ya * ga * (1.0 - ga)
        dgb = dz * yb * gb * (1.0 - gb)
        dp_ref[:, OFF_GA:OFF_GB] = dga.astype(BF)
        dp_ref[:, OFF_GB:N_PROJ] = dgb.astype(BF)
        dbg_ref[0:1, 0:D] += _colsum(dga)
        dbg_ref[0:1, D:2 * D] += _colsum(dgb)

        dybb_ref[...] = (dz * gb).astype(BF)
        dbb = _dot_nt(dybb_ref[...], wpb_ref[...])
        dp_ref[:, OFF_UB:OFF_VB] = (dbb * mixed_ref[...] * _gelu_grad(ub, tu)).astype(BF)
        dmixed = dbb * gu
        dmixb_ref[...] = dmixed.astype(BF)
        for c in range(tm // CHUNK):
            rows = slice(c * CHUNK, (c + 1) * CHUNK)
            dbs_ref[...] += dmixed[rows]
            for h in range(N_HEAD):
                cols = slice(h * CHUNK, (h + 1) * CHUNK)
                dws_ref[h] += _dot_nt(dmixb_ref[rows, cols], vnb_ref[rows, cols])
                dvn_ref[rows, cols] = _dot(wst_ref[h], dmixb_ref[rows, cols])
        dvn = dvn_ref[...]
        dvgb_ref[0:1, :] += _colsum(dvn * xhv)
        dvgb_ref[1:2, :] += _colsum(dvn)
        dgv = _ln_bwd(dvn, vg_ref[...], xhv, rstdv)
        dp_ref[:, OFF_VB:OFF_GA] = (dgv * _gelu_grad(vb_pre, tv)).astype(BF)

    tile = lambda w: pl.BlockSpec((tm, w), lambda i: (i, 0))
    acc = lambda *s: pl.BlockSpec(s, lambda i: (0,) * len(s))
    return _host_call(
        body, comm, name="mix_bwd", grid=(nt,),
        in_specs=[tile(N_PROJ), *_edge_specs(t, tm, N_PROJ), tile(D), *_edge_specs(t, tm, D), tile(D), tile(D),
                  _resident((W_A, D)), _resident((W_B, D)), _resident((D, D)), _resident((N_HEAD, CHUNK, CHUNK)),
                  _resident((N_HEAD, CHUNK, CHUNK)), _resident((CHUNK, W_B)), _resident((1, 2 * D)),
                  _resident((3, W_A)), _resident((1, W_B)), _resident((1, W_B))],
        out_specs=[tile(N_PROJ), tile(W_A), tile(W_B), tile(D), tile(D), tile(D),
                   acc(8, 2 * D), acc(8, W_A), acc(8, W_B), acc(N_HEAD, CHUNK, CHUNK), acc(CHUNK, W_B)],
        out_shape=[jax.ShapeDtypeStruct((t, N_PROJ), BF), jax.ShapeDtypeStruct((t, W_A), BF),
                   jax.ShapeDtypeStruct((t, W_B), BF), jax.ShapeDtypeStruct((t, D), BF), jax.ShapeDtypeStruct((t, D), BF),
                   jax.ShapeDtypeStruct((t, D), BF), jax.ShapeDtypeStruct((8, 2 * D), F32),
                   jax.ShapeDtypeStruct((8, W_A), F32), jax.ShapeDtypeStruct((8, W_B), F32),
                   jax.ShapeDtypeStruct((N_HEAD, CHUNK, CHUNK), F32), jax.ShapeDtypeStruct((CHUNK, W_B), F32)],
        scratch_shapes=[pltpu.VMEM((tm, W_B), BF), pltpu.VMEM((tm, W_B), F32), pltpu.VMEM((tm, W_B), BF),
                        pltpu.VMEM((tm, W_B), F32)],
        args=(p, pedge, pedge, dr1, dedge, dedge, ya, yb, wpa, wpb, wo, wsb, wstb, bsf, bg, cw, vg, vb))


def _add_own_half(full4, recv4, place, rb, name, after=None):
    n, rh, cols = recv4.shape
    nb = rh // rb

    def body(pl_ref, a_ref, b_ref, *rest):
        own_ref, ob_ref = rest[-2:]
        s = a_ref[...] + b_ref[...]
        ob_ref[...] = s.astype(BF)

        @pl.when(pl.program_id(1) == pl_ref[0])
        def _():
            own_ref[...] = s

    blk = (None, rb, cols)
    return _pc(
        body, name=name,
        grid_spec=pltpu.PrefetchScalarGridSpec(
            num_scalar_prefetch=1, grid=(nb, n),
            in_specs=[pl.BlockSpec(blk, lambda i, k, s: (k, s[1] * nb + i, 0)), pl.BlockSpec(blk, lambda i, k, s: (k, i, 0))]
            + ([] if after is None else [pl.BlockSpec(memory_space=pl.ANY)]),
            out_specs=[pl.BlockSpec((rb, cols), lambda i, k, s: (i, 0)), pl.BlockSpec(blk, lambda i, k, s: (k, i, 0))]),
        out_shape=[jax.ShapeDtypeStruct((rh, cols), F32), jax.ShapeDtypeStruct(recv4.shape, BF)],
        compiler_params=_params("arbitrary", "arbitrary"),
    )(place, full4, recv4, *(() if after is None else (after,)))


def _add_chips(own, r3, place, rb, name):
    _, rh, cols = r3.shape
    nb = rh // rb

    def body(pl_ref, s_ref, r_ref, o_ref):
        o_ref[...] = ((s_ref[...] + r_ref[0].astype(F32)) + r_ref[1].astype(F32)) + r_ref[2].astype(F32)

    return _pc(
        body, name=name,
        grid_spec=pltpu.PrefetchScalarGridSpec(
            num_scalar_prefetch=1, grid=(nb,),
            in_specs=[pl.BlockSpec((rb, cols), lambda i, s: (i, 0)), pl.BlockSpec((3, rb, cols), lambda i, s: (0, i, 0))],
            out_specs=pl.BlockSpec((rb, cols), lambda i, s: (s[1] * nb + i, 0))),
        out_shape=jax.ShapeDtypeStruct((2 * rh, cols), F32),
        compiler_params=_params("arbitrary"),
    )(place, own, r3)


def _add_small(a, b):
    def body(a_ref, b_ref, o_ref):
        o_ref[...] = a_ref[...] + b_ref[...]

    return _pc(body, name="add_small_cores", out_shape=jax.ShapeDtypeStruct(a.shape, F32))(a, b)


def _sum_small_chips(own, slots, place):
    def body(pl_ref, own_ref, s_ref, o_ref):
        j = pl_ref[0]

        def term(k):
            return jnp.where(j == k, own_ref[...], s_ref[jnp.maximum((j ^ k) - 1, 0)])

        o_ref[...] = ((term(0) + term(1)) + term(2)) + term(3)

    vmem = pl.BlockSpec(memory_space=pltpu.VMEM)
    return _pc(body, name="sum_small_chips", in_specs=[pl.BlockSpec(memory_space=pltpu.SMEM), vmem, vmem], out_specs=vmem,
               out_shape=jax.ShapeDtypeStruct(own.shape, F32))(place, own, slots)


def _adamw_step(w, g, m, v):
    m2 = ADAM_B1 * m + (1.0 - ADAM_B1) * g
    v2 = ADAM_B2 * v + (1.0 - ADAM_B2) * (g * g)
    m_hat = m2 / (1.0 - ADAM_B1 ** ADAM_STEP)
    v_hat = v2 / (1.0 - ADAM_B2 ** ADAM_STEP)
    return -ADAM_LR * (m_hat / (jnp.sqrt(v_hat) + ADAM_EPS) + ADAM_WD * w), m2, v2


def _adamw(w, g, m, v, rb, name):
    rows, cols = w.shape

    def body(w_ref, g_ref, m_ref, v_ref, d_ref, m2_ref, v2_ref):
        d_ref[...], m2_ref[...], v2_ref[...] = _adamw_step(w_ref[...], g_ref[...], m_ref[...], v_ref[...])

    blk = pl.BlockSpec((rb, cols), lambda i: (i, 0))
    return _pc(body, name=name, grid=(rows // rb,), in_specs=[blk] * 4, out_specs=[blk] * 3,
               out_shape=[jax.ShapeDtypeStruct((rows, cols), F32)] * 3, compiler_params=_params("arbitrary"))(w, g, m, v)


SC_TILES = 32
SC_LANES = 16
SC_ROWS = 16


def _adamw_sc(ws, gs, ms, vs):
    n = len(ws)
    rows, cols = ws[0].shape
    per_tile = rows // SC_TILES

    def body(*refs):
        ins, outs, (wb, gb, mb, vb, db) = refs[:4 * n], refs[4 * n:7 * n], refs[7 * n:]
        tile = lax.axis_index("sc_subcore") * 2 + lax.axis_index("sc_core")
        for i in range(n):
            for ps in range(per_tile // SC_ROWS):
                slab = pl.ds(tile * per_tile + ps * SC_ROWS, SC_ROWS)
                for k, buf in enumerate((wb, gb, mb, vb)):
                    pltpu.sync_copy(ins[k * n + i].at[slab, :], buf)

                @pl.loop(0, SC_ROWS)
                def _(r):
                    @pl.loop(0, cols, step=SC_LANES)
                    def _(c):
                        at = (r, pl.ds(c, SC_LANES))
                        db[at], mb[at], vb[at] = _adamw_step(wb[at], gb[at], mb[at], vb[at])

                for k, buf in enumerate((db, mb, vb)):
                    pltpu.sync_copy(buf, outs[k * n + i].at[slab, :])

    outs = pl.kernel(
        body, name="adamw_sc", out_type=[jax.ShapeDtypeStruct((rows, cols), F32)] * (3 * n),
        mesh=plsc.VectorSubcoreMesh(core_axis_name="sc_core", subcore_axis_name="sc_subcore"),
        scratch_types=[pltpu.VMEM((SC_ROWS, cols), F32)] * 5,
    )(*ws, *gs, *ms, *vs)
    return outs[:n], outs[n:2 * n], outs[2 * n:]


def _adamw_small(ws, gs, ms, vs):
    n = len(ws)

    def body(*refs):
        ins, outs = refs[:4 * n], refs[4 * n:]
        for i in range(n):
            outs[i][...], outs[n + i][...], outs[2 * n + i][...] = _adamw_step(*(ins[k * n + i][...] for k in range(4)))

    outs = _pc(body, name="adamw_small", out_shape=[jax.ShapeDtypeStruct(w.shape, F32) for w in ws] * 3)(*ws, *gs, *ms, *vs)
    return outs[:n], outs[n:2 * n], outs[2 * n:]


LANES = 128
SMALL_GRADS = (("b_gate", 2 * D), ("conv_w", 3 * W_A), ("v_norm_g", W_B), ("v_norm_b", W_B),
               ("w_s", N_HEAD * CHUNK * CHUNK), ("b_s", N_HEAD * CHUNK), ("ln1_g", D), ("ln1_b", D), ("ln2_g", D), ("ln2_b", D),
               ("loss", 1))


def _pack_rows(parts):
    rows = []
    for a in parts:
        a = a.reshape(-1)
        a = jnp.pad(a, (0, (-a.shape[0]) % LANES))
        rows.append(a.reshape(-1, LANES))
    out = jnp.concatenate(rows, axis=0)
    return jnp.pad(out, ((0, (-out.shape[0]) % 8), (0, 0)))


def _unpack_rows(buf, sizes):
    out, r = [], 0
    for n in sizes:
        nr = -(-n // LANES)
        out.append(buf[r:r + nr].reshape(-1)[:n])
        r += nr
    return out


TM_PROJ = 1024
TM_MIX = 256
TM_DX = 512
DX_PAIR = 6
TK_DW = 4096
TK_DW_IN = 2048
TK_DW_PROJ = 1024
ADD_BLOCK_BYTES = 3 * 1024 * 1024
RB_ADAM = 128
CONV_ROWS = 8


def _add_rows(rows, cols):
    while rows * cols * 4 > ADD_BLOCK_BYTES and rows % 32 == 0:
        rows //= 2
    return rows


def _reduce_adds_1(grads, recvs, place, tag, after=None):
    out = [_add_own_half(g, r, place, _add_rows(*r.shape[1:]), f"add_cores_{tag}{a}", after) for a, (g, r) in enumerate(zip(grads, recvs))]
    return [o[0] for o in out], [o[1] for o in out]


def _reduce_adds_2(sums, recvs, place, tag):
    return [_add_chips(s, r, place, _add_rows(*r.shape[1:]), f"add_chips_{tag}{a}") for a, (s, r) in enumerate(zip(sums, recvs))]


def kernel(x, w_in, b_gate, conv_w, v_norm_g, v_norm_b, w_s, b_s, w_pa, w_pb, w_o, ln1_g, ln1_b, w_ff1, w_ff2, ln2_g, ln2_b, loss_target, m_w_in, m_b_gate, m_conv_w, m_v_norm_g, m_v_norm_b, m_w_s, m_b_s, m_w_pa, m_w_pb, m_w_o, m_ln1_g, m_ln1_b, m_w_ff1, m_w_ff2, m_ln2_g, m_ln2_b, v_w_in, v_b_gate, v_conv_w, v_v_norm_g, v_v_norm_b, v_w_s, v_b_s, v_w_pa, v_w_pb, v_w_o, v_ln1_g, v_ln1_b, v_w_ff1, v_w_ff2, v_ln2_g, v_ln2_b):
    t = x.shape[1]
    core = lax.axis_index("c").astype(jnp.int32).reshape(1)
    chip_idx = 2 * lax.axis_index("x") + lax.axis_index("y")
    chip = chip_idx.astype(jnp.int32).reshape(1)
    place = jnp.concatenate([chip, core])
    x2 = x.reshape(t, D)
    tgt = loss_target.reshape(t, D)

    win4, proj4, ff14, ff24 = _cast_shards(w_in[0], w_pa[0], w_pb[0], w_o[0], w_ff1[0], w_ff2[0], chip)
    conv4 = lax.dynamic_update_slice(jnp.zeros((N_CHIP, CONV_ROWS, W_A // N_CHIP), F32),
                                     jnp.pad(conv_w[0], ((0, CONV_ROWS - 3), (0, 0)))[None], (chip_idx, 0, 0))
    (p, pedge), (win4, proj4, ff14, ff24, conv4) = _proj_fwd(
        x2, chip, TM_PROJ, _gather_comm([win4, proj4, ff14, ff24], conv4, eager=1))

    def full(name, rows_total):
        off, rows = PROJ_OFF[name]
        return proj4[:, off:off + rows, :].reshape(rows_total, D)

    wpa, wpb, wo = full("w_pa", W_A), full("w_pb", W_B), full("w_o", D)
    wff2 = ff24.reshape(D_FF, D)
    cw = jnp.transpose(conv4[:, :3, :], (1, 0, 2)).reshape(3, W_A)
    wsb = w_s[0].astype(BF)
    wstb = jnp.swapaxes(w_s[0], 1, 2).astype(BF)
    bsf = jnp.repeat(jnp.transpose(b_s[0]), CHUNK, axis=1)

    r1, ya, yb = _mix_fwd(p, pedge, x2, wpa, wpb, wo, wsb, bsf, b_gate, cw, v_norm_g, v_norm_b, TM_MIX)
    dr1, dr1b, dedge, x1b, hidb, dh1b, dr2b, acc = _ffn_fwd_bwd(r1, tgt, ff14, wff2, ln1_g, ln1_b, ln2_g, ln2_b, TM_MIX)
    g_ff = [_dw(x1b, dh1b, N_CHIP, D, FF_SHARD, False, True, TK_DW, "dw_ff1"),
            _dw(hidb, dr2b, N_CHIP, FF_SHARD, D, True, False, TK_DW, "dw_ff2")]
    r_ff = _sequencer_call(_pair_comm(g_ff), "pair_ff", 3, _sibling_peer)
    (dp, ab, bbb, zb, dyab, dybb, dbg, dcw, dvgb, dws, dbs_sum), _ = _mix_bwd(
        p, pedge, dr1, dedge, ya, yb, wpa, wpb, wo, wsb, wstb, bsf, b_gate, cw, v_norm_g, v_norm_b, TM_MIX, None)
    s_ff, sb_ff = _reduce_adds_1(g_ff, r_ff, place, "ff", after=dp)
    dwin4, c_ff = _dw(x2, dp, N_CHIP, D, NP_SHARD, False, True, TK_DW_IN, "dw_in", _chips_comm(sb_ff))
    f_ff = _reduce_adds_2(s_ff, c_ff, place, "ff")
    dproj4, (g_ff1, g_ff2) = _dw_proj(ab, dyab, bbb, dybb, zb, dr1b, TK_DW_PROJ, _join_comm(f_ff))
    dbs = jnp.transpose(jnp.sum(dbs_sum.reshape(CHUNK, N_HEAD, CHUNK), axis=-1))
    small = _pack_rows([dbg[0], dcw[0:3], dvgb[0], dvgb[1], dws, dbs, acc[0], acc[1], acc[2], acc[3],
                        0.5 * jnp.sum(acc[4]) / D])
    g_rest = [dwin4, dproj4]
    nblk = t // TM_DX
    n_a = max(1, min(DX_PAIR, nblk // 4))
    r_rest = _sequencer_call(_pair_comm(g_rest, small), "pair_rest", 1, _sibling_peer)
    dx, _ = _dx(dp, win4, dr1, TM_DX, 0, n_a, None, "dx_a", None)
    s_rest, sb_rest = _reduce_adds_1(g_rest, r_rest[:2], place, "rest", after=dx)
    csmall = _add_small(small, r_rest[2])
    c_rest = _sequencer_call(_chips_comm(sb_rest, csmall), "chips_rest", 2, _chip_peers)
    dx, _ = _dx(dp, win4, dr1, TM_DX, n_a, nblk - n_a, dx, "dx_b", None, after=sb_rest[0])
    f_rest = _reduce_adds_2(s_rest, c_rest[:2], place, "rest")
    gsmall = _sum_small_chips(csmall, c_rest[2], place)
    g_in, g_proj = _comm_call(_join_comm(f_rest), "join_rest")

    grads = {"w_in": g_in, "w_ff1": g_ff1, "w_ff2": g_ff2}
    for name, _ in PROJ_ROWS:
        off, rows = PROJ_OFF[name]
        grads[name] = g_proj[off:off + rows, :]
    for (name, n), flat in zip(SMALL_GRADS, _unpack_rows(gsmall, [n for _, n in SMALL_GRADS])):
        grads[name] = flat
    loss = grads.pop("loss").reshape(())
    grads["conv_w"] = lax.dynamic_slice(grads["conv_w"].reshape(3, W_A), (0, chip_idx * (W_A // N_CHIP)), (3, W_A // N_CHIP))

    weights = dict(w_in=w_in, b_gate=b_gate, conv_w=conv_w, v_norm_g=v_norm_g, v_norm_b=v_norm_b, w_s=w_s, b_s=b_s,
                   w_pa=w_pa, w_pb=w_pb, w_o=w_o, ln1_g=ln1_g, ln1_b=ln1_b, w_ff1=w_ff1, w_ff2=w_ff2, ln2_g=ln2_g, ln2_b=ln2_b)
    mom1 = dict(w_in=m_w_in, b_gate=m_b_gate, conv_w=m_conv_w, v_norm_g=m_v_norm_g, v_norm_b=m_v_norm_b, w_s=m_w_s,
                b_s=m_b_s, w_pa=m_w_pa, w_pb=m_w_pb, w_o=m_w_o, ln1_g=m_ln1_g, ln1_b=m_ln1_b, w_ff1=m_w_ff1,
                w_ff2=m_w_ff2, ln2_g=m_ln2_g, ln2_b=m_ln2_b)
    mom2 = dict(w_in=v_w_in, b_gate=v_b_gate, conv_w=v_conv_w, v_norm_g=v_v_norm_g, v_norm_b=v_v_norm_b, w_s=v_w_s,
                b_s=v_b_s, w_pa=v_w_pa, w_pb=v_w_pb, w_o=v_w_o, ln1_g=v_ln1_g, ln1_b=v_ln1_b, w_ff1=v_w_ff1,
                w_ff2=v_w_ff2, ln2_g=v_ln2_g, ln2_b=v_ln2_b)
    order = list(weights)
    big = ("w_in", "w_pa", "w_pb", "w_o", "w_ff1", "w_ff2")
    delta, new_m, new_v = {}, {}, {}
    early = ("w_ff1", "w_ff2")
    ds, ms, vs = _adamw_sc(*([d[n][0] if d is not grads else d[n] for n in early] for d in (weights, grads, mom1, mom2)))
    for name, d_, m_, v_ in zip(early, ds, ms, vs):
        delta[name], new_m[name], new_v[name] = d_, m_, v_
    for name in big:
        if name in early:
            continue
        w2 = weights[name][0]
        delta[name], new_m[name], new_v[name] = _adamw(w2, grads[name], mom1[name][0], mom2[name][0], RB_ADAM, "adamw_" + name)
    little = [n for n in order if n not in big]
    flat2d = lambda a: a.reshape(-1, a.shape[-1])
    ds, ms, vs = _adamw_small(*([flat2d(d[n].reshape(weights[n].shape)) for n in little] for d in (weights, grads, mom1, mom2)))
    for name, d_, m_, v_ in zip(little, ds, ms, vs):
        delta[name], new_m[name], new_v[name] = d_, m_, v_

    shaped = lambda d: [d[n].reshape(weights[n].shape) for n in order]
    return (loss, dx.reshape(x.shape), *shaped(grads), *shaped(delta), *shaped(new_m), *shaped(new_v))
```

```python
import functools
from typing import NamedTuple

import jax
import jax.numpy as jnp
from jax import lax
from jax.experimental import pallas as pl
from jax.experimental.pallas import tpu as pltpu
from jax.experimental.pallas import tpu_sc as plsc

D = 1024
W_A = 1536
W_B = 1024
CHUNK = 128
N_HEAD = 8
D_FF = 4096
N_PROJ = 3 * W_A + 2 * W_B + 2 * D
OFF_CA, OFF_HA, OFF_UB, OFF_VB, OFF_GA, OFF_GB = 1536, 3072, 4608, 5632, 6656, 7680
LN_EPS = 1e-5
ALPHA = 2.0 ** 0.25
N_CHIP = 4
NP_SHARD = N_PROJ // N_CHIP
FF_SHARD = D_FF // N_CHIP
ADAM_LR, ADAM_B1, ADAM_B2, ADAM_EPS, ADAM_WD, ADAM_STEP = 0.001, 0.9, 0.999, 1e-08, 0.01, 10

PROJ_ROWS = (("w_pa", W_A // N_CHIP), ("w_pb", W_B // N_CHIP), ("w_o", D // N_CHIP))
PROJ_OFF = {}
_o = 0
for _n, _r in PROJ_ROWS:
    PROJ_OFF[_n] = (_o, _r)
    _o += _r
PROJ_TOTAL = _o

V7X_VMEM_BYTES = 64 * 1024 * 1024
VMEM_LIMIT = V7X_VMEM_BYTES - 8 * 1024 * 1024
HALO = 8
EDGE_TILE = 256

BF = jnp.bfloat16
F32 = jnp.float32
MESH = pl.DeviceIdType.MESH
HBM_SPEC = pl.BlockSpec(memory_space=pltpu.HBM)


def _pc(body, **kw):
    return pl.pallas_call(body, **kw)


def _params(*sem):
    return pltpu.CompilerParams(dimension_semantics=sem, vmem_limit_bytes=VMEM_LIMIT)


def _resident(shape):
    n = len(shape)
    return pl.BlockSpec(shape, lambda *_: (0,) * n, pipeline_mode=pl.Buffered(1))


def _dot(a, b):
    return jnp.dot(a, b, preferred_element_type=F32)


def _dot_nt(a, b):
    return lax.dot_general(a, b, (((1,), (1,)), ((), ())), preferred_element_type=F32)


def _dot_tn(a, b):
    return lax.dot_general(a, b, (((0,), (0,)), ((), ())), preferred_element_type=F32)


def _gelu(x):
    t = jnp.tanh(0.7978845608028654 * (x + 0.044715 * (x * x * x)))
    return 0.5 * x * (1.0 + t), t


def _gelu_grad(x, t):
    return 0.5 * (1.0 + t) + 0.5 * x * (1.0 - t * t) * (0.7978845608028654 * (1.0 + 0.134145 * (x * x)))


def _ln_stats(r):
    mu = jnp.mean(r, axis=-1, keepdims=True)
    xc = r - mu
    var = jnp.mean(xc * xc, axis=-1, keepdims=True)
    rstd = lax.rsqrt(var + LN_EPS)
    return xc * rstd, rstd


def _ln_bwd(dy, g, xh, rstd):
    dxh = dy * g
    m1 = jnp.mean(dxh, axis=-1, keepdims=True)
    m2 = jnp.mean(dxh * xh, axis=-1, keepdims=True)
    return rstd * (dxh - m1 - xh * m2)


def _colsum(v):
    return jnp.sum(v, axis=0, keepdims=True)


class _Comm(NamedTuple):
    args: tuple
    out_shape: tuple
    aliases: dict
    n_sems: int
    stages: tuple


def _place():
    x, y, c = lax.axis_index("x"), lax.axis_index("y"), lax.axis_index("c")
    return x, y, c, 2 * x + y


def _flip(x, y, c, r):
    return (x ^ (r >> 1), y ^ (r & 1), c)


def _remote(src, dst, send_sems, recv_sems, k, peer):
    return pltpu.make_async_remote_copy(src_ref=src, dst_ref=dst, send_sem=send_sems.at[k], recv_sem=recv_sems.at[k],
                                        device_id=peer, device_id_type=MESH)


def _host_call(body, comm, *, name, grid, in_specs, out_specs, out_shape, args, scratch_shapes=(), aliases=None, prefetch=None,
               body_reads_comm=False):
    sem = ("arbitrary",) * len(grid)
    aliases = dict(aliases or {})
    n_pre = 0 if prefetch is None else 1
    n_in, n_out, n_scr = len(in_specs), len(out_specs), len(scratch_shapes)
    c_in, c_out = (0, 0) if comm is None else (len(comm.args), len(comm.out_shape))
    steps = {"first": (0,) * len(grid), "late": (grid[0] - 1,) + (0,) * (len(grid) - 1), "last": tuple(g - 1 for g in grid)}

    def wrapped(*refs):
        refs = refs[n_pre:]
        own_in, cin = refs[:n_in], refs[n_in:n_in + c_in]
        o0 = n_in + c_in
        own_out, cout = refs[o0:o0 + n_out], refs[o0 + n_out:o0 + n_out + c_out]
        s0 = o0 + n_out + c_out
        scr, sems = refs[s0:s0 + n_scr], refs[s0 + n_scr:]

        def run(before):
            for phase, fn in () if comm is None else comm.stages:
                at_step = isinstance(phase, tuple)
                if before != (at_step or phase == "first"):
                    continue
                step = phase if at_step else steps[phase]
                cond = pl.program_id(0) == step[0]
                for d in range(1, len(grid)):
                    cond = jnp.logical_and(cond, pl.program_id(d) == step[d])
                pl.when(cond)(functools.partial(fn, cin, cout, *sems))

        run(True)
        if body_reads_comm:
            body(*own_in, *own_out, *scr, comm_refs=cout)
        else:
            body(*own_in, *own_out, *scr)
        run(False)

    in_specs = list(in_specs) + [HBM_SPEC] * c_in
    out_specs = list(out_specs) + [HBM_SPEC] * c_out
    out_shape = list(out_shape) + ([] if comm is None else list(comm.out_shape))
    scratch_shapes = list(scratch_shapes) + ([] if comm is None else [pltpu.SemaphoreType.DMA((comm.n_sems,))] * 2)
    args = tuple(args) + (() if comm is None else tuple(comm.args))
    if comm is not None:
        aliases.update({n_in + i: n_out + o for i, o in comm.aliases.items()})
    aliases = {i + n_pre: o for i, o in aliases.items()}
    if prefetch is None:
        kw = dict(grid=grid, in_specs=in_specs, out_specs=out_specs, scratch_shapes=scratch_shapes)
    else:
        kw = dict(grid_spec=pltpu.PrefetchScalarGridSpec(num_scalar_prefetch=1, grid=grid, in_specs=in_specs,
                                                         out_specs=out_specs, scratch_shapes=scratch_shapes))
        args = (prefetch,) + args
    outs = _pc(wrapped, name=name, out_shape=out_shape, input_output_aliases=aliases, compiler_params=_params(*sem), **kw)(*args)
    return outs[:n_out], outs[n_out:]


def _comm_call(comm, name):
    def body(*refs):
        c_in, c_out = len(comm.args), len(comm.out_shape)
        cin, cout, (send_sems, recv_sems) = refs[:c_in], refs[c_in:c_in + c_out], refs[c_in + c_out:]
        for phase in ("first", "late", "last"):
            for ph, fn in comm.stages:
                if ph == phase:
                    fn(cin, cout, send_sems, recv_sems)

    return _pc(body, name=name, in_specs=[HBM_SPEC] * len(comm.args), out_specs=[HBM_SPEC] * len(comm.out_shape),
               out_shape=list(comm.out_shape), scratch_shapes=[pltpu.SemaphoreType.DMA((comm.n_sems,))] * 2,
               input_output_aliases=dict(comm.aliases))(*comm.args)


def _sequencer_call(comm, name, collective_id, peers_of):
    hbm = pltpu.MemorySpace.HBM
    cin = [jax.new_ref(a, memory_space=hbm) for a in comm.args]
    cout = [jax.empty_ref(jax.ShapeDtypeStruct(o.shape, o.dtype), memory_space=hbm) for o in comm.out_shape]

    @pl.kernel(mesh=plsc.ScalarSubcoreMesh(axis_name="sequencer", num_cores=1), name=name,
               scratch_types=(pltpu.SemaphoreType.DMA((comm.n_sems,)), pltpu.SemaphoreType.DMA((comm.n_sems,))),
               compiler_params=pltpu.CompilerParams(collective_id=collective_id))
    def launch(send_sems, recv_sems):
        barrier = pltpu.get_barrier_semaphore()
        peers = peers_of()
        for peer in peers:
            pl.semaphore_signal(barrier, inc=1, device_id=peer, device_id_type=MESH)
        pl.semaphore_wait(barrier, len(peers))
        for phase in ("first", "late", "last"):
            for ph, fn in comm.stages:
                if ph == phase:
                    fn(cin, cout, send_sems, recv_sems)

    launch()
    return [r[...] for r in cout]


def _sibling_peer():
    x, y, c, _ = _place()
    return [(x, y, 1 - c)]


def _chip_peers():
    x, y, c, _ = _place()
    return [_flip(x, y, c, r) for r in (1, 2, 3)]


def _gather_comm(bufs, whole=None, eager=0):
    n = len(bufs)
    halves = [b.shape[1] // 2 for b in bufs]
    k_ici = lambda a, r: 3 * a + r - 1
    k_d2d = lambda a, r: 3 * n + 3 * a + r - 1
    k_whole = lambda r: 6 * n + r - 1

    def half(ref, slot, c, a):
        return ref.at[slot, pl.ds(c * halves[a], halves[a])]

    def send(cin, cout, ss, rs):
        x, y, c, j = _place()
        for a in range(n):
            mine = half(cout[a], j, c, a)
            for r in (1, 2, 3):
                _remote(mine, mine, ss, rs, k_ici(a, r), _flip(x, y, c, r)).start()
        if whole is not None:
            for r in (1, 2, 3):
                _remote(cout[n].at[j], cout[n].at[j], ss, rs, k_whole(r), _flip(x, y, c, r)).start()

    def pass_on(cout, ss, rs, a, r):
        x, y, c, j = _place()
        landed = half(cout[a], j ^ r, c, a)
        _remote(landed, landed, ss, rs, k_ici(a, r), (x, y, 1 - c)).wait_recv()
        _remote(landed, landed, ss, rs, k_d2d(a, r), (x, y, 1 - c)).start()

    def passed_on(cout, ss, rs, a, r):
        x, y, c, j = _place()
        theirs = half(cout[a], j ^ r, 1 - c, a)
        _remote(theirs, theirs, ss, rs, k_d2d(a, r), (x, y, 1 - c)).wait_recv()

    def arrive(r, cin, cout, ss, rs):
        for a in range(eager):
            pass_on(cout, ss, rs, a, r)
        for a in range(eager):
            passed_on(cout, ss, rs, a, r)

    def forward(cin, cout, ss, rs):
        for a in range(eager, n):
            for r in (1, 2, 3):
                pass_on(cout, ss, rs, a, r)

    def finish(cin, cout, ss, rs):
        x, y, c, j = _place()
        sibling = (x, y, 1 - c)
        for a in range(eager, n):
            for r in (1, 2, 3):
                passed_on(cout, ss, rs, a, r)
        for a in range(n):
            mine = half(cout[a], j, c, a)
            for r in (1, 2, 3):
                _remote(mine, mine, ss, rs, k_ici(a, r), sibling).wait_send()
                landed = half(cout[a], j ^ r, c, a)
                _remote(landed, landed, ss, rs, k_d2d(a, r), sibling).wait_send()
        if whole is not None:
            for r in (1, 2, 3):
                cp = _remote(cout[n].at[j ^ r], cout[n].at[j ^ r], ss, rs, k_whole(r), sibling)
                cp.wait_recv()
                cp.wait_send()

    args = tuple(bufs) + ((whole,) if whole is not None else ())
    out_shape = tuple(jax.ShapeDtypeStruct(b.shape, b.dtype) for b in args)
    aliases = {a: a for a in range(len(args))}
    arrivals = tuple(((r, 0), functools.partial(arrive, r)) for r in (1, 2, 3)) if eager else ()
    return _Comm(args, out_shape, aliases, 6 * n + 3, (("first", send),) + arrivals + (("late", forward), ("last", finish)))


def _pair_comm(grads, small=None):
    n = len(grads)
    halves = [g.shape[1] // 2 for g in grads]

    def copies(cin, cout, ss, rs):
        x, y, c, _ = _place()
        sibling = (x, y, 1 - c)
        cps = [_remote(cin[a].at[:, pl.ds((1 - c) * halves[a], halves[a]), :], cout[a], ss, rs, a, sibling) for a in range(n)]
        if small is not None:
            cps.append(_remote(cin[n], cout[n], ss, rs, n, sibling))
        return cps

    def start(cin, cout, ss, rs):
        for cp in copies(cin, cout, ss, rs):
            cp.start()

    def finish(cin, cout, ss, rs):
        for cp in copies(cin, cout, ss, rs):
            cp.wait()

    args = tuple(grads) + ((small,) if small is not None else ())
    out_shape = tuple(jax.ShapeDtypeStruct((N_CHIP, h, g.shape[2]), F32) for g, h in zip(grads, halves))
    out_shape += (jax.ShapeDtypeStruct(small.shape, F32),) if small is not None else ()
    return _Comm(args, out_shape, {}, n + 1, (("first", start), ("last", finish)))


def _chips_comm(sums_bf, small=None):
    n = len(sums_bf)

    def copies(cin, cout, ss, rs):
        x, y, c, j = _place()
        cps = []
        for r in (1, 2, 3):
            peer = _flip(x, y, c, r)
            for a in range(n):
                cps.append(_remote(cin[a].at[j ^ r], cout[a].at[r - 1], ss, rs, (n + 1) * (r - 1) + a, peer))
            if small is not None:
                cps.append(_remote(cin[n], cout[n].at[r - 1], ss, rs, (n + 1) * (r - 1) + n, peer))
        return cps

    def start(cin, cout, ss, rs):
        for cp in copies(cin, cout, ss, rs):
            cp.start()

    def finish(cin, cout, ss, rs):
        for cp in copies(cin, cout, ss, rs):
            cp.wait()

    args = tuple(sums_bf) + ((small,) if small is not None else ())
    out_shape = tuple(jax.ShapeDtypeStruct((3,) + s.shape[1:], BF) for s in sums_bf)
    out_shape += (jax.ShapeDtypeStruct((3,) + small.shape, F32),) if small is not None else ()
    return _Comm(args, out_shape, {}, 3 * (n + 1), (("first", start), ("last", finish)))


def _join_comm(shards):
    n = len(shards)
    halves = [s.shape[0] // 2 for s in shards]

    def start(cin, cout, ss, rs):
        x, y, c, _ = _place()
        for a in range(n):
            mine = cout[a].at[pl.ds(c * halves[a], halves[a]), :]
            _remote(mine, mine, ss, rs, a, (x, y, 1 - c)).start()

    def finish(cin, cout, ss, rs):
        x, y, c, _ = _place()
        for a in range(n):
            theirs = cout[a].at[pl.ds((1 - c) * halves[a], halves[a]), :]
            cp = _remote(theirs, theirs, ss, rs, a, (x, y, 1 - c))
            cp.wait_recv()
            cp.wait_send()

    out_shape = tuple(jax.ShapeDtypeStruct(s.shape, F32) for s in shards)
    return _Comm(tuple(shards), out_shape, {a: a for a in range(n)}, n, (("first", start), ("last", finish)))


def _cast_shards(w_in, w_pa, w_pb, w_o, w_ff1, w_ff2, chip):
    def body(j_ref, win_ref, wpa_ref, wpb_ref, wo_ref, wff1_ref, wff2_ref, win4_ref, proj4_ref, ff14_ref, ff24_ref):
        win4_ref[...] = win_ref[...].astype(BF)
        for name, ref in (("w_pa", wpa_ref), ("w_pb", wpb_ref), ("w_o", wo_ref)):
            off, rows = PROJ_OFF[name]
            proj4_ref[off:off + rows, :] = ref[...].astype(BF)
        ff14_ref[...] = wff1_ref[...].astype(BF)
        ff24_ref[...] = wff2_ref[...].astype(BF)

    whole = lambda a: pl.BlockSpec(a.shape, lambda i, j: (0, 0), pipeline_mode=pl.Buffered(1))
    slot = lambda rows, cols: pl.BlockSpec((None, rows, cols), lambda i, j: (j[0], 0, 0))
    ws = (w_in, w_pa, w_pb, w_o, w_ff1, w_ff2)
    shapes = ((D, NP_SHARD), (PROJ_TOTAL, D), (D, FF_SHARD), (FF_SHARD, D))
    return _pc(
        body, name="cast_shards",
        grid_spec=pltpu.PrefetchScalarGridSpec(num_scalar_prefetch=1, grid=(1,), in_specs=[whole(w) for w in ws],
                                               out_specs=[slot(*s) for s in shapes]),
        out_shape=[jax.ShapeDtypeStruct((N_CHIP,) + s, BF) for s in shapes],
        compiler_params=_params("arbitrary"))(chip, *ws)


def _proj_fwd(x, chip, tm, comm):
    t = x.shape[0]
    sub = tm // EDGE_TILE

    def body(x_ref, p_ref, edge_ref, w_ref, w_sem, comm_refs):
        @pl.when(pl.program_id(1) == 0)
        def _():
            _, _, _, j = _place()
            block = pltpu.make_async_copy(comm_refs[0].at[j ^ pl.program_id(0)], w_ref, w_sem)
            block.start()
            block.wait()

        p_ref[...] = _dot(x_ref[...].astype(BF), w_ref[...])
        _write_edges(edge_ref, p_ref, tm)

    return _host_call(
        body, comm, name="proj_fwd", grid=(N_CHIP, t // tm), prefetch=chip, body_reads_comm=True,
        in_specs=[pl.BlockSpec((tm, D), lambda r, i, j: (i, 0))],
        out_specs=[pl.BlockSpec((tm, NP_SHARD), lambda r, i, j: (i, j[0] ^ r)),
                   pl.BlockSpec((sub, 2 * HALO, NP_SHARD), lambda r, i, j: (i, 0, j[0] ^ r))],
        out_shape=[jax.ShapeDtypeStruct((t, N_PROJ), F32), jax.ShapeDtypeStruct((t // EDGE_TILE, 2 * HALO, N_PROJ), F32)],
        scratch_shapes=[pltpu.VMEM((D, NP_SHARD), BF), pltpu.SemaphoreType.DMA],
        args=(x,))


def _edge_specs(t, tm, w):
    k, last = tm // EDGE_TILE, t // EDGE_TILE - 1
    return [pl.BlockSpec((None, HALO, w), lambda i: (jnp.maximum(i * k - 1, 0), 1, 0)),
            pl.BlockSpec((None, HALO, w), lambda i: (jnp.minimum((i + 1) * k, last), 0, 0))]


def _write_edges(edge_ref, rows, tm):
    for s in range(tm // EDGE_TILE):
        edge_ref[s, 0:HALO, :] = rows[s * EDGE_TILE:s * EDGE_TILE + HALO, :]
        edge_ref[s, HALO:2 * HALO, :] = rows[(s + 1) * EDGE_TILE - HALO:(s + 1) * EDGE_TILE, :]


def _pcols(p_ref, lo, hi):
    return p_ref[:, lo:hi]


def _end_masks(nt):
    i = pl.program_id(0)
    return (i > 0).astype(F32), (i < nt - 1).astype(F32)


def _conv_fwd(p_ref, prev_ref, next_ref, cw_ref, tm, has_prev, has_next):
    ca = _pcols(p_ref, OFF_CA, OFF_HA)
    ha = _pcols(p_ref, OFF_HA, OFF_UB)
    ch = ca * ha
    ch_prev = prev_ref[HALO - 1:HALO, OFF_CA:OFF_HA] * prev_ref[HALO - 1:HALO, OFF_HA:OFF_UB] * has_prev
    ch_next = next_ref[0:1, OFF_CA:OFF_HA] * next_ref[0:1, OFF_HA:OFF_UB] * has_next
    row = lax.broadcasted_iota(jnp.int32, (tm, W_A), 0)
    ch_m1 = jnp.where(row == 0, ch_prev, pltpu.roll(ch, 1, 0))
    ch_p1 = jnp.where(row == tm - 1, ch_next, pltpu.roll(ch, tm - 1, 0))
    cv = cw_ref[0:1, :] * ch_m1 + cw_ref[1:2, :] * ch + cw_ref[2:3, :] * ch_p1
    return ca, ha, ch, ch_m1, ch_p1, cv


def _spatial_fwd(p_ref, vg_ref, vb_ref, ws_ref, bsf_ref, vnb_ref, mixed_ref, tm):
    vb_pre = _pcols(p_ref, OFF_VB, OFF_GA)
    gv, tv = _gelu(vb_pre)
    xhv, rstdv = _ln_stats(gv)
    vnb_ref[...] = (xhv * vg_ref[...] + vb_ref[...]).astype(BF)
    for c in range(tm // CHUNK):
        rows = slice(c * CHUNK, (c + 1) * CHUNK)
        for h in range(N_HEAD):
            cols = slice(h * CHUNK, (h + 1) * CHUNK)
            mixed_ref[rows, cols] = _dot(ws_ref[h], vnb_ref[rows, cols]) + bsf_ref[:, cols]
    return vb_pre, tv, xhv, rstdv


def _mix_fwd(p, pedge, x, wpa, wpb, wo, wsb, bsf, bg, cw, vg, vb, tm):
    t = x.shape[0]
    nt = t // tm

    def body(p_ref, prev_ref, next_ref, x_ref, wpa_ref, wpb_ref, wo_ref, ws_ref, bsf_ref, bg_ref, cw_ref, vg_ref, vb_ref,
             r1_ref, ya_ref, yb_ref, vnb_ref, mixed_ref):
        has_prev, has_next = _end_masks(nt)
        _, _, _, _, _, cv = _conv_fwd(p_ref, prev_ref, next_ref, cw_ref, tm, has_prev, has_next)
        a = _pcols(p_ref, 0, OFF_CA) * cv
        ya = _dot(a.astype(BF), wpa_ref[...])
        ya_ref[...] = ya
        _spatial_fwd(p_ref, vg_ref, vb_ref, ws_ref, bsf_ref, vnb_ref, mixed_ref, tm)
        gu, _ = _gelu(_pcols(p_ref, OFF_UB, OFF_VB))
        bb = gu * mixed_ref[...]
        yb = _dot(bb.astype(BF), wpb_ref[...])
        yb_ref[...] = yb
        ga = jax.nn.sigmoid(_pcols(p_ref, OFF_GA, OFF_GB) + bg_ref[:, 0:D])
        gb = jax.nn.sigmoid(_pcols(p_ref, OFF_GB, N_PROJ) + bg_ref[:, D:2 * D])
        z = ga * ya + gb * yb
        r1_ref[...] = ALPHA * x_ref[...] + _dot(z.astype(BF), wo_ref[...])

    tile = lambda w: pl.BlockSpec((tm, w), lambda i: (i, 0))
    return _pc(
        body, name="mix_fwd", grid=(nt,),
        in_specs=[tile(N_PROJ), *_edge_specs(t, tm, N_PROJ), tile(D),
                  _resident((W_A, D)), _resident((W_B, D)), _resident((D, D)), _resident((N_HEAD, CHUNK, CHUNK)),
                  _resident((CHUNK, W_B)), _resident((1, 2 * D)), _resident((3, W_A)), _resident((1, W_B)),
                  _resident((1, W_B))],
        out_specs=[tile(D), tile(D), tile(D)],
        out_shape=[jax.ShapeDtypeStruct((t, D), F32)] * 3,
        scratch_shapes=[pltpu.VMEM((tm, W_B), BF), pltpu.VMEM((tm, W_B), F32)],
        compiler_params=_params("arbitrary"),
    )(p, pedge, pedge, x, wpa, wpb, wo, wsb, bsf, bg, cw, vg, vb)


def _ffn_fwd_bwd(r1, tgt, wff1, wff2, ln1g, ln1b, ln2g, ln2b, tm):
    t = r1.shape[0]

    def body(r1_ref, tgt_ref, w1_ref, w2_ref, g1_ref, b1_ref, g2_ref, b2_ref,
             dr1_ref, dr1b_ref, dedge_ref, x1b_ref, hidb_ref, dh1b_ref, dr2b_ref, acc_ref, relu_ref):
        @pl.when(pl.program_id(0) == 0)
        def _():
            acc_ref[...] = jnp.zeros_like(acc_ref)

        xh1, rstd1 = _ln_stats(r1_ref[...])
        x1 = xh1 * g1_ref[...] + b1_ref[...]
        x1b_ref[...] = x1.astype(BF)
        ffn = jnp.zeros((tm, D), F32)
        for j in range(N_CHIP):
            cols = slice(j * FF_SHARD, (j + 1) * FF_SHARD)
            r = jnp.maximum(_dot(x1b_ref[...], w1_ref[j]), 0.0)
            relu_ref[:, cols] = r
            hidb_ref[:, cols] = (r * r).astype(BF)
            ffn = ffn + _dot(hidb_ref[:, cols], w2_ref[cols, :])
        xh2, rstd2 = _ln_stats(ALPHA * x1 + ffn)
        diff = xh2 * g2_ref[...] + b2_ref[...] - tgt_ref[...]
        acc_ref[4:5, :] += _colsum(diff * diff)
        dx2 = diff * (1.0 / D)
        acc_ref[2:3, :] += _colsum(dx2 * xh2)
        acc_ref[3:4, :] += _colsum(dx2)
        dr2 = _ln_bwd(dx2, g2_ref[...], xh2, rstd2)
        dr2b_ref[...] = dr2.astype(BF)
        dx1 = ALPHA * dr2
        for j in range(N_CHIP):
            cols = slice(j * FF_SHARD, (j + 1) * FF_SHARD)
            dhid = _dot_nt(dr2b_ref[...], w2_ref[cols, :])
            dh1b_ref[:, cols] = (dhid * (2.0 * relu_ref[:, cols])).astype(BF)
            dx1 = dx1 + _dot_nt(dh1b_ref[:, cols], w1_ref[j])
        acc_ref[0:1, :] += _colsum(dx1 * xh1)
        acc_ref[1:2, :] += _colsum(dx1)
        dr1 = _ln_bwd(dx1, g1_ref[...], xh1, rstd1)
        dr1_ref[...] = dr1
        dr1b_ref[...] = dr1.astype(BF)
        _write_edges(dedge_ref, dr1_ref, tm)

    tile = lambda w: pl.BlockSpec((tm, w), lambda i: (i, 0))
    vec = _resident((1, D))
    return _pc(
        body, name="ffn_fwd_bwd", grid=(t // tm,),
        in_specs=[tile(D), tile(D), _resident((N_CHIP, D, FF_SHARD)), _resident((D_FF, D)), vec, vec, vec, vec],
        out_specs=[tile(D), tile(D), pl.BlockSpec((tm // EDGE_TILE, 2 * HALO, D), lambda i: (i, 0, 0)), tile(D), tile(D_FF),
                   tile(D_FF), tile(D), pl.BlockSpec((8, D), lambda i: (0, 0))],
        out_shape=[jax.ShapeDtypeStruct((t, D), F32), jax.ShapeDtypeStruct((t, D), BF),
                   jax.ShapeDtypeStruct((t // EDGE_TILE, 2 * HALO, D), F32), jax.ShapeDtypeStruct((t, D), BF),
                   jax.ShapeDtypeStruct((t, D_FF), BF), jax.ShapeDtypeStruct((t, D_FF), BF),
                   jax.ShapeDtypeStruct((t, D), BF), jax.ShapeDtypeStruct((8, D), F32)],
        scratch_shapes=[pltpu.VMEM((tm, D_FF), F32)],
        compiler_params=_params("arbitrary"),
    )(r1, tgt, wff1, wff2, ln1g, ln1b, ln2g, ln2b)


def _dw(a, b, nblk, am, bn, a_blocked, b_blocked, tk, name, comm=None, after=None):
    t = a.shape[0]

    def body(a_ref, b_ref, *rest):
        o_ref = rest[-1]

        @pl.when(pl.program_id(1) == 0)
        def _():
            o_ref[...] = jnp.zeros_like(o_ref)

        o_ref[...] += _dot_tn(a_ref[...].astype(BF), b_ref[...])

    outs, got = _host_call(
        body, comm, name=name, grid=(nblk, t // tk),
        in_specs=[pl.BlockSpec((tk, am), (lambda j, k: (k, j)) if a_blocked else (lambda j, k: (k, 0))),
                  pl.BlockSpec((tk, bn), (lambda j, k: (k, j)) if b_blocked else (lambda j, k: (k, 0)))]
        + ([] if after is None else [pl.BlockSpec(memory_space=pl.ANY)]),
        out_specs=[pl.BlockSpec((None, am, bn), lambda j, k: (j, 0, 0))],
        out_shape=[jax.ShapeDtypeStruct((nblk, am, bn), F32)], args=(a, b) + (() if after is None else (after,)))
    return outs[0] if comm is None else (outs[0], got)


def _dw_proj(ab, dyab, bbb, dybb, zb, dr1b, tk, comm):
    t = ab.shape[0]
    pairs = (("w_pa", 0, 1), ("w_pb", 2, 3), ("w_o", 4, 5))

    def body(*refs):
        o_ref = refs[6]

        @pl.when(pl.program_id(0) == 0)
        def _():
            o_ref[...] = jnp.zeros_like(o_ref)

        for name, ia, ib in pairs:
            off, rows = PROJ_OFF[name]
            for k in range(N_CHIP):
                o_ref[k, off:off + rows, :] += _dot_tn(refs[ia][:, k * rows:(k + 1) * rows], refs[ib][...])

    tile = lambda w: pl.BlockSpec((tk, w), lambda i: (i, 0))
    outs, got = _host_call(
        body, comm, name="dw_proj", grid=(t // tk,), in_specs=[tile(W_A), tile(D), tile(W_B), tile(D), tile(D), tile(D)],
        out_specs=[pl.BlockSpec((N_CHIP, PROJ_TOTAL, D), lambda i: (0, 0, 0))],
        out_shape=[jax.ShapeDtypeStruct((N_CHIP, PROJ_TOTAL, D), F32)], args=(ab, dyab, bbb, dybb, zb, dr1b))
    return outs[0], got


def _dx(dp, win4, dr1, tm, blk0, nblk, filled, name, comm, after=None):
    t = dp.shape[0]

    def body(dp_ref, w_ref, dr1_ref, *rest):
        dx = ALPHA * dr1_ref[...]
        for j in range(N_CHIP):
            dx = dx + _dot_nt(dp_ref[:, j * NP_SHARD:(j + 1) * NP_SHARD], w_ref[j])
        rest[-1][...] = dx

    in_specs = [pl.BlockSpec((tm, N_PROJ), lambda i: (i + blk0, 0)), _resident((N_CHIP, D, NP_SHARD)),
                pl.BlockSpec((tm, D), lambda i: (i + blk0, 0))]
    args = (dp, win4, dr1)
    aliases = None
    if filled is not None:
        in_specs.append(pl.BlockSpec(memory_space=pl.ANY))
        args += (filled,)
        aliases = {3: 0}
    if after is not None:
        in_specs.append(pl.BlockSpec(memory_space=pl.ANY))
        args += (after,)
    outs, got = _host_call(
        body, comm, name=name, grid=(nblk,), in_specs=in_specs, out_specs=[pl.BlockSpec((tm, D), lambda i: (i + blk0, 0))],
        out_shape=[jax.ShapeDtypeStruct((t, D), F32)], args=args, aliases=aliases)
    return outs[0], got


def _mix_bwd(p, pedge, dr1, dedge, ya, yb, wpa, wpb, wo, wsb, wstb, bsf, bg, cw, vg, vb, tm, comm):
    t = p.shape[0]
    nt = t // tm
    te = tm + 2 * HALO
    mid = slice(HALO, HALO + tm)

    def body(p_ref, prev_ref, next_ref, dr1_ref, dprev_ref, dnext_ref, ya_ref, yb_ref, wpa_ref, wpb_ref, wo_ref,
             ws_ref, wst_ref, bsf_ref, bg_ref, cw_ref, vg_ref, vb_ref,
             dp_ref, ab_ref, bbb_ref, zb_ref, dyab_ref, dybb_ref, dbg_ref, dcw_ref, dvgb_ref, dws_ref, dbs_ref,
             vnb_ref, mixed_ref, dmixb_ref, dvn_ref):
        @pl.when(pl.program_id(0) == 0)
        def _():
            for r in (dbg_ref, dcw_ref, dvgb_ref, dws_ref, dbs_ref):
                r[...] = jnp.zeros_like(r)

        has_prev, has_next = _end_masks(nt)
        ca, ha, ch, ch_m1, ch_p1, cv = _conv_fwd(p_ref, prev_ref, next_ref, cw_ref, tm, has_prev, has_next)
        ba = _pcols(p_ref, 0, OFF_CA)
        ab_ref[...] = (ba * cv).astype(BF)
        vb_pre, tv, xhv, rstdv = _spatial_fwd(p_ref, vg_ref, vb_ref, ws_ref, bsf_ref, vnb_ref, mixed_ref, tm)
        ub = _pcols(p_ref, OFF_UB, OFF_VB)
        gu, tu = _gelu(ub)
        bbb_ref[...] = (gu * mixed_ref[...]).astype(BF)
        bga = bg_ref[:, 0:D]
        ga = jax.nn.sigmoid(_pcols(p_ref, OFF_GA, OFF_GB) + bga)
        gb = jax.nn.sigmoid(_pcols(p_ref, OFF_GB, N_PROJ) + bg_ref[:, D:2 * D])
        ya = ya_ref[...]
        yb = yb_ref[...]
        zb_ref[...] = (ga * ya + gb * yb).astype(BF)

        dr1_ext = jnp.concatenate([dprev_ref[...] * has_prev, dr1_ref[...], dnext_ref[...] * has_next], axis=0)
        dz_ext = _dot_nt(dr1_ext.astype(BF), wo_ref[...])
        ga_ext = jnp.concatenate([jax.nn.sigmoid(prev_ref[:, OFF_GA:OFF_GB] + bga), ga,
                                  jax.nn.sigmoid(next_ref[:, OFF_GA:OFF_GB] + bga)], axis=0)
        dya_ext = dz_ext * ga_ext
        dyab_ref[...] = dya_ext[mid].astype(BF)
        da_ext = _dot_nt(dya_ext.astype(BF), wpa_ref[...])
        ba_ext = jnp.concatenate([prev_ref[:, 0:OFF_CA], ba, next_ref[:, 0:OFF_CA]], axis=0)
        dcv_ext = da_ext * ba_ext
        dcv = dcv_ext[mid]
        dch = (cw_ref[0:1, :] * pltpu.roll(dcv_ext, te - 1, 0)[mid] + cw_ref[1:2, :] * dcv
               + cw_ref[2:3, :] * pltpu.roll(dcv_ext, 1, 0)[mid])
        dp_ref[:, 0:OFF_CA] = (da_ext[mid] * cv).astype(BF)
        dp_ref[:, OFF_CA:OFF_HA] = (dch * ha).astype(BF)
        dp_ref[:, OFF_HA:OFF_UB] = (dch * ca).astype(BF)
        dcw_ref[0:1, :] += _colsum(dcv * ch_m1)
        dcw_ref[1:2, :] += _colsum(dcv * ch)
        dcw_ref[2:3, :] += _colsum(dcv * ch_p1)

        dz = dz_ext[mid]
        dga = dz * ya * ga * (1.0 - ga)
        dgb = dz * yb * gb * (1.0 - gb)
        dp_ref[:, OFF_GA:OFF_GB] = dga.astype(BF)
        dp_ref[:, OFF_GB:N_PROJ] = dgb.astype(BF)
        dbg_ref[0:1, 0:D] += _colsum(dga)
        dbg_ref[0:1, D:2 * D] += _colsum(dgb)

        dybb_ref[...] = (dz * gb).astype(BF)
        dbb = _dot_nt(dybb_ref[...], wpb_ref[...])
        dp_ref[:, OFF_UB:OFF_VB] = (dbb * mixed_ref[...] * _gelu_grad(ub, tu)).astype(BF)
        dmixed = dbb * gu
        dmixb_ref[...] = dmixed.astype(BF)
        for c in range(tm // CHUNK):
            rows = slice(c * CHUNK, (c + 1) * CHUNK)
            dbs_ref[...] += dmixed[rows]
            for h in range(N_HEAD):
                cols = slice(h * CHUNK, (h + 1) * CHUNK)
                dws_ref[h] += _dot_nt(dmixb_ref[rows, cols], vnb_ref[rows, cols])
                dvn_ref[rows, cols] = _dot(wst_ref[h], dmixb_ref[rows, cols])
        dvn = dvn_ref[...]
        dvgb_ref[0:1, :] += _colsum(dvn * xhv)
        dvgb_ref[1:2, :] += _colsum(dvn)
        dgv = _ln_bwd(dvn, vg_ref[...], xhv, rstdv)
        dp_ref[:, OFF_VB:OFF_GA] = (dgv * _gelu_grad(vb_pre, tv)).astype(BF)

    tile = lambda w: pl.BlockSpec((tm, w), lambda i: (i, 0))
    acc = lambda *s: pl.BlockSpec(s, lambda i: (0,) * len(s))
    return _host_call(
        body, comm, name="mix_bwd", grid=(nt,),
        in_specs=[tile(N_PROJ), *_edge_specs(t, tm, N_PROJ), tile(D), *_edge_specs(t, tm, D), tile(D), tile(D),
                  _resident((W_A, D)), _resident((W_B, D)), _resident((D, D)), _resident((N_HEAD, CHUNK, CHUNK)),
                  _resident((N_HEAD, CHUNK, CHUNK)), _resident((CHUNK, W_B)), _resident((1, 2 * D)),
                  _resident((3, W_A)), _resident((1, W_B)), _resident((1, W_B))],
        out_specs=[tile(N_PROJ), tile(W_A), tile(W_B), tile(D), tile(D), tile(D),
                   acc(8, 2 * D), acc(8, W_A), acc(8, W_B), acc(N_HEAD, CHUNK, CHUNK), acc(CHUNK, W_B)],
        out_shape=[jax.ShapeDtypeStruct((t, N_PROJ), BF), jax.ShapeDtypeStruct((t, W_A), BF),
                   jax.ShapeDtypeStruct((t, W_B), BF), jax.ShapeDtypeStruct((t, D), BF), jax.ShapeDtypeStruct((t, D), BF),
                   jax.ShapeDtypeStruct((t, D), BF), jax.ShapeDtypeStruct((8, 2 * D), F32),
                   jax.ShapeDtypeStruct((8, W_A), F32), jax.ShapeDtypeStruct((8, W_B), F32),
                   jax.ShapeDtypeStruct((N_HEAD, CHUNK, CHUNK), F32), jax.ShapeDtypeStruct((CHUNK, W_B), F32)],
        scratch_shapes=[pltpu.VMEM((tm, W_B), BF), pltpu.VMEM((tm, W_B), F32), pltpu.VMEM((tm, W_B), BF),
                        pltpu.VMEM((tm, W_B), F32)],
        args=(p, pedge, pedge, dr1, dedge, dedge, ya, yb, wpa, wpb, wo, wsb, wstb, bsf, bg, cw, vg, vb))


def _add_own_half(full4, recv4, place, rb, name, after=None):
    n, rh, cols = recv4.shape
    nb = rh // rb

    def body(pl_ref, a_ref, b_ref, *rest):
        own_ref, ob_ref = rest[-2:]
        s = a_ref[...] + b_ref[...]
        ob_ref[...] = s.astype(BF)

        @pl.when(pl.program_id(1) == pl_ref[0])
        def _():
            own_ref[...] = s

    blk = (None, rb, cols)
    return _pc(
        body, name=name,
        grid_spec=pltpu.PrefetchScalarGridSpec(
            num_scalar_prefetch=1, grid=(nb, n),
            in_specs=[pl.BlockSpec(blk, lambda i, k, s: (k, s[1] * nb + i, 0)), pl.BlockSpec(blk, lambda i, k, s: (k, i, 0))]
            + ([] if after is None else [pl.BlockSpec(memory_space=pl.ANY)]),
            out_specs=[pl.BlockSpec((rb, cols), lambda i, k, s: (i, 0)), pl.BlockSpec(blk, lambda i, k, s: (k, i, 0))]),
        out_shape=[jax.ShapeDtypeStruct((rh, cols), F32), jax.ShapeDtypeStruct(recv4.shape, BF)],
        compiler_params=_params("arbitrary", "arbitrary"),
    )(place, full4, recv4, *(() if after is None else (after,)))


def _add_chips(own, r3, place, rb, name, after=None):
    _, rh, cols = r3.shape
    nb = rh // rb

    def body(pl_ref, s_ref, r_ref, *rest):
        rest[-1][...] = ((s_ref[...] + r_ref[0].astype(F32)) + r_ref[1].astype(F32)) + r_ref[2].astype(F32)

    return _pc(
        body, name=name,
        grid_spec=pltpu.PrefetchScalarGridSpec(
            num_scalar_prefetch=1, grid=(nb,),
            in_specs=[pl.BlockSpec((rb, cols), lambda i, s: (i, 0)), pl.BlockSpec((3, rb, cols), lambda i, s: (0, i, 0))]
            + ([] if after is None else [pl.BlockSpec(memory_space=pl.ANY)]),
            out_specs=pl.BlockSpec((rb, cols), lambda i, s: (s[1] * nb + i, 0))),
        out_shape=jax.ShapeDtypeStruct((2 * rh, cols), F32),
        compiler_params=_params("arbitrary"),
    )(place, own, r3, *(() if after is None else (after,)))


def _add_small(a, b):
    def body(a_ref, b_ref, o_ref):
        o_ref[...] = a_ref[...] + b_ref[...]

    return _pc(body, name="add_small_cores", out_shape=jax.ShapeDtypeStruct(a.shape, F32))(a, b)


def _sum_small_chips(own, slots, place):
    def body(pl_ref, own_ref, s_ref, o_ref):
        j = pl_ref[0]

        def term(k):
            return jnp.where(j == k, own_ref[...], s_ref[jnp.maximum((j ^ k) - 1, 0)])

        o_ref[...] = ((term(0) + term(1)) + term(2)) + term(3)

    vmem = pl.BlockSpec(memory_space=pltpu.VMEM)
    return _pc(body, name="sum_small_chips", in_specs=[pl.BlockSpec(memory_space=pltpu.SMEM), vmem, vmem], out_specs=vmem,
               out_shape=jax.ShapeDtypeStruct(own.shape, F32))(place, own, slots)


def _adamw_step(w, g, m, v):
    m2 = ADAM_B1 * m + (1.0 - ADAM_B1) * g
    v2 = ADAM_B2 * v + (1.0 - ADAM_B2) * (g * g)
    m_hat = m2 / (1.0 - ADAM_B1 ** ADAM_STEP)
    v_hat = v2 / (1.0 - ADAM_B2 ** ADAM_STEP)
    return -ADAM_LR * (m_hat / (jnp.sqrt(v_hat) + ADAM_EPS) + ADAM_WD * w), m2, v2


def _adamw(w, g, m, v, rb, name):
    rows, cols = w.shape

    def body(w_ref, g_ref, m_ref, v_ref, d_ref, m2_ref, v2_ref):
        d_ref[...], m2_ref[...], v2_ref[...] = _adamw_step(w_ref[...], g_ref[...], m_ref[...], v_ref[...])

    blk = pl.BlockSpec((rb, cols), lambda i: (i, 0))
    return _pc(body, name=name, grid=(rows // rb,), in_specs=[blk] * 4, out_specs=[blk] * 3,
               out_shape=[jax.ShapeDtypeStruct((rows, cols), F32)] * 3, compiler_params=_params("arbitrary"))(w, g, m, v)


SC_TILES = 32
SC_LANES = 16
SC_ROWS = 16


def _adamw_sc(ws, gs, ms, vs):
    n = len(ws)
    rows, cols = ws[0].shape
    per_tile = rows // SC_TILES

    def body(*refs):
        ins, outs, (wb, gb, mb, vb, db) = refs[:4 * n], refs[4 * n:7 * n], refs[7 * n:]
        tile = lax.axis_index("sc_subcore") * 2 + lax.axis_index("sc_core")
        for i in range(n):
            for ps in range(per_tile // SC_ROWS):
                slab = pl.ds(tile * per_tile + ps * SC_ROWS, SC_ROWS)
                for k, buf in enumerate((wb, gb, mb, vb)):
                    pltpu.sync_copy(ins[k * n + i].at[slab, :], buf)

                @pl.loop(0, SC_ROWS)
                def _(r):
                    @pl.loop(0, cols, step=SC_LANES)
                    def _(c):
                        at = (r, pl.ds(c, SC_LANES))
                        db[at], mb[at], vb[at] = _adamw_step(wb[at], gb[at], mb[at], vb[at])

                for k, buf in enumerate((db, mb, vb)):
                    pltpu.sync_copy(buf, outs[k * n + i].at[slab, :])

    outs = pl.kernel(
        body, name="adamw_sc", out_type=[jax.ShapeDtypeStruct((rows, cols), F32)] * (3 * n),
        mesh=plsc.VectorSubcoreMesh(core_axis_name="sc_core", subcore_axis_name="sc_subcore"),
        scratch_types=[pltpu.VMEM((SC_ROWS, cols), F32)] * 5,
    )(*ws, *gs, *ms, *vs)
    return outs[:n], outs[n:2 * n], outs[2 * n:]


def _adamw_small(ws, gs, ms, vs):
    n = len(ws)

    def body(*refs):
        ins, outs = refs[:4 * n], refs[4 * n:]
        for i in range(n):
            outs[i][...], outs[n + i][...], outs[2 * n + i][...] = _adamw_step(*(ins[k * n + i][...] for k in range(4)))

    outs = _pc(body, name="adamw_small", out_shape=[jax.ShapeDtypeStruct(w.shape, F32) for w in ws] * 3)(*ws, *gs, *ms, *vs)
    return outs[:n], outs[n:2 * n], outs[2 * n:]


LANES = 128
SMALL_GRADS = (("b_gate", 2 * D), ("conv_w", 3 * W_A), ("v_norm_g", W_B), ("v_norm_b", W_B),
               ("w_s", N_HEAD * CHUNK * CHUNK), ("b_s", N_HEAD * CHUNK), ("ln1_g", D), ("ln1_b", D), ("ln2_g", D), ("ln2_b", D),
               ("loss", 1))


def _pack_rows(parts):
    rows = []
    for a in parts:
        a = a.reshape(-1)
        a = jnp.pad(a, (0, (-a.shape[0]) % LANES))
        rows.append(a.reshape(-1, LANES))
    out = jnp.concatenate(rows, axis=0)
    return jnp.pad(out, ((0, (-out.shape[0]) % 8), (0, 0)))


def _unpack_rows(buf, sizes):
    out, r = [], 0
    for n in sizes:
        nr = -(-n // LANES)
        out.append(buf[r:r + nr].reshape(-1)[:n])
        r += nr
    return out


TM_PROJ = 1024
TM_MIX = 256
TM_DX = 512
DX_PAIR = 6
TK_DW = 4096
TK_DW_IN = 2048
TK_DW_PROJ = 1024
ADD_BLOCK_BYTES = 3 * 1024 * 1024
RB_ADAM = 128
CONV_ROWS = 8


def _add_rows(rows, cols):
    while rows * cols * 4 > ADD_BLOCK_BYTES and rows % 32 == 0:
        rows //= 2
    return rows


def _reduce_adds_1(grads, recvs, place, tag, after=None):
    out = [_add_own_half(g, r, place, _add_rows(*r.shape[1:]), f"add_cores_{tag}{a}", after) for a, (g, r) in enumerate(zip(grads, recvs))]
    return [o[0] for o in out], [o[1] for o in out]


def _reduce_adds_2(sums, recvs, place, tag, after=None):
    return [_add_chips(s, r, place, _add_rows(*r.shape[1:]), f"add_chips_{tag}{a}", after) for a, (s, r) in enumerate(zip(sums, recvs))]


def kernel(x, w_in, b_gate, conv_w, v_norm_g, v_norm_b, w_s, b_s, w_pa, w_pb, w_o, ln1_g, ln1_b, w_ff1, w_ff2, ln2_g, ln2_b, loss_target, m_w_in, m_b_gate, m_conv_w, m_v_norm_g, m_v_norm_b, m_w_s, m_b_s, m_w_pa, m_w_pb, m_w_o, m_ln1_g, m_ln1_b, m_w_ff1, m_w_ff2, m_ln2_g, m_ln2_b, v_w_in, v_b_gate, v_conv_w, v_v_norm_g, v_v_norm_b, v_w_s, v_b_s, v_w_pa, v_w_pb, v_w_o, v_ln1_g, v_ln1_b, v_w_ff1, v_w_ff2, v_ln2_g, v_ln2_b):
    t = x.shape[1]
    core = lax.axis_index("c").astype(jnp.int32).reshape(1)
    chip_idx = 2 * lax.axis_index("x") + lax.axis_index("y")
    chip = chip_idx.astype(jnp.int32).reshape(1)
    place = jnp.concatenate([chip, core])
    x2 = x.reshape(t, D)
    tgt = loss_target.reshape(t, D)

    win4, proj4, ff14, ff24 = _cast_shards(w_in[0], w_pa[0], w_pb[0], w_o[0], w_ff1[0], w_ff2[0], chip)
    conv4 = lax.dynamic_update_slice(jnp.zeros((N_CHIP, CONV_ROWS, W_A // N_CHIP), F32),
                                     jnp.pad(conv_w[0], ((0, CONV_ROWS - 3), (0, 0)))[None], (chip_idx, 0, 0))
    (p, pedge), (win4, proj4, ff14, ff24, conv4) = _proj_fwd(
        x2, chip, TM_PROJ, _gather_comm([win4, proj4, ff14, ff24], conv4, eager=1))

    def full(name, rows_total):
        off, rows = PROJ_OFF[name]
        return proj4[:, off:off + rows, :].reshape(rows_total, D)

    wpa, wpb, wo = full("w_pa", W_A), full("w_pb", W_B), full("w_o", D)
    wff2 = ff24.reshape(D_FF, D)
    cw = jnp.transpose(conv4[:, :3, :], (1, 0, 2)).reshape(3, W_A)
    wsb = w_s[0].astype(BF)
    wstb = jnp.swapaxes(w_s[0], 1, 2).astype(BF)
    bsf = jnp.repeat(jnp.transpose(b_s[0]), CHUNK, axis=1)

    r1, ya, yb = _mix_fwd(p, pedge, x2, wpa, wpb, wo, wsb, bsf, b_gate, cw, v_norm_g, v_norm_b, TM_MIX)
    dr1, dr1b, dedge, x1b, hidb, dh1b, dr2b, acc = _ffn_fwd_bwd(r1, tgt, ff14, wff2, ln1_g, ln1_b, ln2_g, ln2_b, TM_MIX)
    g_ff = [_dw(x1b, dh1b, N_CHIP, D, FF_SHARD, False, True, TK_DW, "dw_ff1"),
            _dw(hidb, dr2b, N_CHIP, FF_SHARD, D, True, False, TK_DW, "dw_ff2")]
    r_ff = _sequencer_call(_pair_comm(g_ff), "pair_ff", 3, _sibling_peer)
    (dp, ab, bbb, zb, dyab, dybb, dbg, dcw, dvgb, dws, dbs_sum), _ = _mix_bwd(
        p, pedge, dr1, dedge, ya, yb, wpa, wpb, wo, wsb, wstb, bsf, b_gate, cw, v_norm_g, v_norm_b, TM_MIX, None)
    s_ff, sb_ff = _reduce_adds_1(g_ff, r_ff, place, "ff", after=dp)
    c_ff = _sequencer_call(_chips_comm(sb_ff), "chips_ff", 4, _chip_peers)
    dwin4 = _dw(x2, dp, N_CHIP, D, NP_SHARD, False, True, TK_DW_IN, "dw_in", after=sb_ff[0])
    f_ff = _reduce_adds_2(s_ff, c_ff, place, "ff", after=dwin4)
    dproj4, (g_ff1, g_ff2) = _dw_proj(ab, dyab, bbb, dybb, zb, dr1b, TK_DW_PROJ, _join_comm(f_ff))
    dbs = jnp.transpose(jnp.sum(dbs_sum.reshape(CHUNK, N_HEAD, CHUNK), axis=-1))
    small = _pack_rows([dbg[0], dcw[0:3], dvgb[0], dvgb[1], dws, dbs, acc[0], acc[1], acc[2], acc[3],
                        0.5 * jnp.sum(acc[4]) / D])
    g_rest = [dwin4, dproj4]
    nblk = t // TM_DX
    n_a = max(1, min(DX_PAIR, nblk // 4))
    r_rest = _sequencer_call(_pair_comm(g_rest, small), "pair_rest", 1, _sibling_peer)
    dx, _ = _dx(dp, win4, dr1, TM_DX, 0, n_a, None, "dx_a", None)
    s_rest, sb_rest = _reduce_adds_1(g_rest, r_rest[:2], place, "rest", after=dx)
    csmall = _add_small(small, r_rest[2])
    c_rest = _sequencer_call(_chips_comm(sb_rest, csmall), "chips_rest", 2, _chip_peers)
    dx, _ = _dx(dp, win4, dr1, TM_DX, n_a, nblk - n_a, dx, "dx_b", None, after=sb_rest[0])
    f_rest = _reduce_adds_2(s_rest, c_rest[:2], place, "rest")
    gsmall = _sum_small_chips(csmall, c_rest[2], place)
    g_in, g_proj = _comm_call(_join_comm(f_rest), "join_rest")

    grads = {"w_in": g_in, "w_ff1": g_ff1, "w_ff2": g_ff2}
    for name, _ in PROJ_ROWS:
        off, rows = PROJ_OFF[name]
        grads[name] = g_proj[off:off + rows, :]
    for (name, n), flat in zip(SMALL_GRADS, _unpack_rows(gsmall, [n for _, n in SMALL_GRADS])):
        grads[name] = flat
    loss = grads.pop("loss").reshape(())
    grads["conv_w"] = lax.dynamic_slice(grads["conv_w"].reshape(3, W_A), (0, chip_idx * (W_A // N_CHIP)), (3, W_A // N_CHIP))

    weights = dict(w_in=w_in, b_gate=b_gate, conv_w=conv_w, v_norm_g=v_norm_g, v_norm_b=v_norm_b, w_s=w_s, b_s=b_s,
                   w_pa=w_pa, w_pb=w_pb, w_o=w_o, ln1_g=ln1_g, ln1_b=ln1_b, w_ff1=w_ff1, w_ff2=w_ff2, ln2_g=ln2_g, ln2_b=ln2_b)
    mom1 = dict(w_in=m_w_in, b_gate=m_b_gate, conv_w=m_conv_w, v_norm_g=m_v_norm_g, v_norm_b=m_v_norm_b, w_s=m_w_s,
                b_s=m_b_s, w_pa=m_w_pa, w_pb=m_w_pb, w_o=m_w_o, ln1_g=m_ln1_g, ln1_b=m_ln1_b, w_ff1=m_w_ff1,
                w_ff2=m_w_ff2, ln2_g=m_ln2_g, ln2_b=m_ln2_b)
    mom2 = dict(w_in=v_w_in, b_gate=v_b_gate, conv_w=v_conv_w, v_norm_g=v_v_norm_g, v_norm_b=v_v_norm_b, w_s=v_w_s,
                b_s=v_b_s, w_pa=v_w_pa, w_pb=v_w_pb, w_o=v_w_o, ln1_g=v_ln1_g, ln1_b=v_ln1_b, w_ff1=v_w_ff1,
                w_ff2=v_w_ff2, ln2_g=v_ln2_g, ln2_b=v_ln2_b)
    order = list(weights)
    big = ("w_in", "w_pa", "w_pb", "w_o", "w_ff1", "w_ff2")
    delta, new_m, new_v = {}, {}, {}
    early = ("w_ff1", "w_ff2")
    ds, ms, vs = _adamw_sc(*([d[n][0] if d is not grads else d[n] for n in early] for d in (weights, grads, mom1, mom2)))
    for name, d_, m_, v_ in zip(early, ds, ms, vs):
        delta[name], new_m[name], new_v[name] = d_, m_, v_
    for name in big:
        if name in early:
            continue
        w2 = weights[name][0]
        delta[name], new_m[name], new_v[name] = _adamw(w2, grads[name], mom1[name][0], mom2[name][0], RB_ADAM, "adamw_" + name)
    little = [n for n in order if n not in big]
    flat2d = lambda a: a.reshape(-1, a.shape[-1])
    ds, ms, vs = _adamw_small(*([flat2d(d[n].reshape(weights[n].shape)) for n in little] for d in (weights, grads, mom1, mom2)))
    for name, d_, m_, v_ in zip(little, ds, ms, vs):
        delta[name], new_m[name], new_v[name] = d_, m_, v_

    shaped = lambda d: [d[n].reshape(weights[n].shape) for n in order]
    return (loss, dx.reshape(x.shape), *shaped(grads), *shaped(delta), *shaped(new_m), *shaped(new_v))
```

```python
import functools
from typing import NamedTuple

import jax
import jax.numpy as jnp
from jax import lax
from jax.experimental import pallas as pl
from jax.experimental.pallas import tpu as pltpu
from jax.experimental.pallas import tpu_sc as plsc

D = 1024
W_A = 1536
W_B = 1024
CHUNK = 128
N_HEAD = 8
D_FF = 4096
N_PROJ = 3 * W_A + 2 * W_B + 2 * D
OFF_CA, OFF_HA, OFF_UB, OFF_VB, OFF_GA, OFF_GB = 1536, 3072, 4608, 5632, 6656, 7680
LN_EPS = 1e-5
ALPHA = 2.0 ** 0.25
N_CHIP = 4
NP_SHARD = N_PROJ // N_CHIP
FF_SHARD = D_FF // N_CHIP
ADAM_LR, ADAM_B1, ADAM_B2, ADAM_EPS, ADAM_WD, ADAM_STEP = 0.001, 0.9, 0.999, 1e-08, 0.01, 10

PROJ_ROWS = (("w_pa", W_A // N_CHIP), ("w_pb", W_B // N_CHIP), ("w_o", D // N_CHIP))
PROJ_OFF = {}
_o = 0
for _n, _r in PROJ_ROWS:
    PROJ_OFF[_n] = (_o, _r)
    _o += _r
PROJ_TOTAL = _o

V7X_VMEM_BYTES = 64 * 1024 * 1024
VMEM_LIMIT = V7X_VMEM_BYTES - 8 * 1024 * 1024
HALO = 8
EDGE_TILE = 256

BF = jnp.bfloat16
F32 = jnp.float32
MESH = pl.DeviceIdType.MESH
HBM_SPEC = pl.BlockSpec(memory_space=pltpu.HBM)


def _pc(body, **kw):
    return pl.pallas_call(body, **kw)


def _params(*sem):
    return pltpu.CompilerParams(dimension_semantics=sem, vmem_limit_bytes=VMEM_LIMIT)


def _resident(shape):
    n = len(shape)
    return pl.BlockSpec(shape, lambda *_: (0,) * n, pipeline_mode=pl.Buffered(1))


def _dot(a, b):
    return jnp.dot(a, b, preferred_element_type=F32)


def _dot_nt(a, b):
    return lax.dot_general(a, b, (((1,), (1,)), ((), ())), preferred_element_type=F32)


def _dot_tn(a, b):
    return lax.dot_general(a, b, (((0,), (0,)), ((), ())), preferred_element_type=F32)


def _gelu(x):
    t = jnp.tanh(0.7978845608028654 * (x + 0.044715 * (x * x * x)))
    return 0.5 * x * (1.0 + t), t


def _gelu_grad(x, t):
    return 0.5 * (1.0 + t) + 0.5 * x * (1.0 - t * t) * (0.7978845608028654 * (1.0 + 0.134145 * (x * x)))


def _ln_stats(r):
    mu = jnp.mean(r, axis=-1, keepdims=True)
    xc = r - mu
    var = jnp.mean(xc * xc, axis=-1, keepdims=True)
    rstd = lax.rsqrt(var + LN_EPS)
    return xc * rstd, rstd


def _ln_bwd(dy, g, xh, rstd):
    dxh = dy * g
    m1 = jnp.mean(dxh, axis=-1, keepdims=True)
    m2 = jnp.mean(dxh * xh, axis=-1, keepdims=True)
    return rstd * (dxh - m1 - xh * m2)


def _colsum(v):
    return jnp.sum(v, axis=0, keepdims=True)


class _Comm(NamedTuple):
    args: tuple
    out_shape: tuple
    aliases: dict
    n_sems: int
    stages: tuple


def _place():
    x, y, c = lax.axis_index("x"), lax.axis_index("y"), lax.axis_index("c")
    return x, y, c, 2 * x + y


def _flip(x, y, c, r):
    return (x ^ (r >> 1), y ^ (r & 1), c)


def _remote(src, dst, send_sems, recv_sems, k, peer):
    return pltpu.make_async_remote_copy(src_ref=src, dst_ref=dst, send_sem=send_sems.at[k], recv_sem=recv_sems.at[k],
                                        device_id=peer, device_id_type=MESH)


def _host_call(body, comm, *, name, grid, in_specs, out_specs, out_shape, args, scratch_shapes=(), aliases=None, prefetch=None,
               body_reads_comm=False):
    sem = ("arbitrary",) * len(grid)
    aliases = dict(aliases or {})
    n_pre = 0 if prefetch is None else 1
    n_in, n_out, n_scr = len(in_specs), len(out_specs), len(scratch_shapes)
    c_in, c_out = (0, 0) if comm is None else (len(comm.args), len(comm.out_shape))
    steps = {"first": (0,) * len(grid), "late": (grid[0] - 1,) + (0,) * (len(grid) - 1), "last": tuple(g - 1 for g in grid)}

    def wrapped(*refs):
        refs = refs[n_pre:]
        own_in, cin = refs[:n_in], refs[n_in:n_in + c_in]
        o0 = n_in + c_in
        own_out, cout = refs[o0:o0 + n_out], refs[o0 + n_out:o0 + n_out + c_out]
        s0 = o0 + n_out + c_out
        scr, sems = refs[s0:s0 + n_scr], refs[s0 + n_scr:]

        def run(before):
            for phase, fn in () if comm is None else comm.stages:
                at_step = isinstance(phase, tuple)
                if before != (at_step or phase == "first"):
                    continue
                step = phase if at_step else steps[phase]
                cond = pl.program_id(0) == step[0]
                for d in range(1, len(grid)):
                    cond = jnp.logical_and(cond, pl.program_id(d) == step[d])
                pl.when(cond)(functools.partial(fn, cin, cout, *sems))

        run(True)
        if body_reads_comm:
            body(*own_in, *own_out, *scr, comm_refs=cout)
        else:
            body(*own_in, *own_out, *scr)
        run(False)

    in_specs = list(in_specs) + [HBM_SPEC] * c_in
    out_specs = list(out_specs) + [HBM_SPEC] * c_out
    out_shape = list(out_shape) + ([] if comm is None else list(comm.out_shape))
    scratch_shapes = list(scratch_shapes) + ([] if comm is None else [pltpu.SemaphoreType.DMA((comm.n_sems,))] * 2)
    args = tuple(args) + (() if comm is None else tuple(comm.args))
    if comm is not None:
        aliases.update({n_in + i: n_out + o for i, o in comm.aliases.items()})
    aliases = {i + n_pre: o for i, o in aliases.items()}
    if prefetch is None:
        kw = dict(grid=grid, in_specs=in_specs, out_specs=out_specs, scratch_shapes=scratch_shapes)
    else:
        kw = dict(grid_spec=pltpu.PrefetchScalarGridSpec(num_scalar_prefetch=1, grid=grid, in_specs=in_specs,
                                                         out_specs=out_specs, scratch_shapes=scratch_shapes))
        args = (prefetch,) + args
    outs = _pc(wrapped, name=name, out_shape=out_shape, input_output_aliases=aliases, compiler_params=_params(*sem), **kw)(*args)
    return outs[:n_out], outs[n_out:]


def _comm_call(comm, name):
    def body(*refs):
        c_in, c_out = len(comm.args), len(comm.out_shape)
        cin, cout, (send_sems, recv_sems) = refs[:c_in], refs[c_in:c_in + c_out], refs[c_in + c_out:]
        for phase in ("first", "late", "last"):
            for ph, fn in comm.stages:
                if ph == phase:
                    fn(cin, cout, send_sems, recv_sems)

    return _pc(body, name=name, in_specs=[HBM_SPEC] * len(comm.args), out_specs=[HBM_SPEC] * len(comm.out_shape),
               out_shape=list(comm.out_shape), scratch_shapes=[pltpu.SemaphoreType.DMA((comm.n_sems,))] * 2,
               input_output_aliases=dict(comm.aliases))(*comm.args)


def _sequencer_call(comm, name, collective_id, peers_of):
    hbm = pltpu.MemorySpace.HBM
    cin = [jax.new_ref(a, memory_space=hbm) for a in comm.args]
    filled_by = {o: i for i, o in comm.aliases.items()}
    cout = [cin[filled_by[k]] if k in filled_by else jax.empty_ref(jax.ShapeDtypeStruct(o.shape, o.dtype), memory_space=hbm)
            for k, o in enumerate(comm.out_shape)]

    @pl.kernel(mesh=plsc.ScalarSubcoreMesh(axis_name="sequencer", num_cores=1), name=name,
               scratch_types=(pltpu.SemaphoreType.DMA((comm.n_sems,)), pltpu.SemaphoreType.DMA((comm.n_sems,))),
               compiler_params=pltpu.CompilerParams(collective_id=collective_id))
    def launch(send_sems, recv_sems):
        barrier = pltpu.get_barrier_semaphore()
        peers = peers_of()
        for peer in peers:
            pl.semaphore_signal(barrier, inc=1, device_id=peer, device_id_type=MESH)
        pl.semaphore_wait(barrier, len(peers))
        for phase in ("first", "late", "last"):
            for ph, fn in comm.stages:
                if ph == phase:
                    fn(cin, cout, send_sems, recv_sems)

    launch()
    return [r[...] for r in cout]


def _sibling_peer():
    x, y, c, _ = _place()
    return [(x, y, 1 - c)]


def _chip_peers():
    x, y, c, _ = _place()
    return [_flip(x, y, c, r) for r in (1, 2, 3)]


def _gather_comm(bufs, whole=None, eager=0):
    n = len(bufs)
    halves = [b.shape[1] // 2 for b in bufs]
    k_ici = lambda a, r: 3 * a + r - 1
    k_d2d = lambda a, r: 3 * n + 3 * a + r - 1
    k_whole = lambda r: 6 * n + r - 1

    def half(ref, slot, c, a):
        return ref.at[slot, pl.ds(c * halves[a], halves[a])]

    def send(cin, cout, ss, rs):
        x, y, c, j = _place()
        for a in range(n):
            mine = half(cout[a], j, c, a)
            for r in (1, 2, 3):
                _remote(mine, mine, ss, rs, k_ici(a, r), _flip(x, y, c, r)).start()
        if whole is not None:
            for r in (1, 2, 3):
                _remote(cout[n].at[j], cout[n].at[j], ss, rs, k_whole(r), _flip(x, y, c, r)).start()

    def pass_on(cout, ss, rs, a, r):
        x, y, c, j = _place()
        landed = half(cout[a], j ^ r, c, a)
        _remote(landed, landed, ss, rs, k_ici(a, r), (x, y, 1 - c)).wait_recv()
        _remote(landed, landed, ss, rs, k_d2d(a, r), (x, y, 1 - c)).start()

    def passed_on(cout, ss, rs, a, r):
        x, y, c, j = _place()
        theirs = half(cout[a], j ^ r, 1 - c, a)
        _remote(theirs, theirs, ss, rs, k_d2d(a, r), (x, y, 1 - c)).wait_recv()

    def arrive(r, cin, cout, ss, rs):
        for a in range(eager):
            pass_on(cout, ss, rs, a, r)
        for a in range(eager):
            passed_on(cout, ss, rs, a, r)

    def forward(cin, cout, ss, rs):
        for a in range(eager, n):
            for r in (1, 2, 3):
                pass_on(cout, ss, rs, a, r)

    def finish(cin, cout, ss, rs):
        x, y, c, j = _place()
        sibling = (x, y, 1 - c)
        for a in range(eager, n):
            for r in (1, 2, 3):
                passed_on(cout, ss, rs, a, r)
        for a in range(n):
            mine = half(cout[a], j, c, a)
            for r in (1, 2, 3):
                _remote(mine, mine, ss, rs, k_ici(a, r), sibling).wait_send()
                landed = half(cout[a], j ^ r, c, a)
                _remote(landed, landed, ss, rs, k_d2d(a, r), sibling).wait_send()
        if whole is not None:
            for r in (1, 2, 3):
                cp = _remote(cout[n].at[j ^ r], cout[n].at[j ^ r], ss, rs, k_whole(r), sibling)
                cp.wait_recv()
                cp.wait_send()

    args = tuple(bufs) + ((whole,) if whole is not None else ())
    out_shape = tuple(jax.ShapeDtypeStruct(b.shape, b.dtype) for b in args)
    aliases = {a: a for a in range(len(args))}
    arrivals = tuple(((r, 0), functools.partial(arrive, r)) for r in (1, 2, 3)) if eager else ()
    return _Comm(args, out_shape, aliases, 6 * n + 3, (("first", send),) + arrivals + (("late", forward), ("last", finish)))


def _pair_comm(grads, small=None):
    n = len(grads)
    halves = [g.shape[1] // 2 for g in grads]

    def copies(cin, cout, ss, rs):
        x, y, c, _ = _place()
        sibling = (x, y, 1 - c)
        cps = [_remote(cin[a].at[:, pl.ds((1 - c) * halves[a], halves[a]), :], cout[a], ss, rs, a, sibling) for a in range(n)]
        if small is not None:
            cps.append(_remote(cin[n], cout[n], ss, rs, n, sibling))
        return cps

    def start(cin, cout, ss, rs):
        for cp in copies(cin, cout, ss, rs):
            cp.start()

    def finish(cin, cout, ss, rs):
        for cp in copies(cin, cout, ss, rs):
            cp.wait()

    args = tuple(grads) + ((small,) if small is not None else ())
    out_shape = tuple(jax.ShapeDtypeStruct((N_CHIP, h, g.shape[2]), F32) for g, h in zip(grads, halves))
    out_shape += (jax.ShapeDtypeStruct(small.shape, F32),) if small is not None else ()
    return _Comm(args, out_shape, {}, n + 1, (("first", start), ("last", finish)))


def _chips_comm(sums_bf, small=None):
    n = len(sums_bf)

    def copies(cin, cout, ss, rs):
        x, y, c, j = _place()
        cps = []
        for r in (1, 2, 3):
            peer = _flip(x, y, c, r)
            for a in range(n):
                cps.append(_remote(cin[a].at[j ^ r], cout[a].at[r - 1], ss, rs, (n + 1) * (r - 1) + a, peer))
            if small is not None:
                cps.append(_remote(cin[n], cout[n].at[r - 1], ss, rs, (n + 1) * (r - 1) + n, peer))
        return cps

    def start(cin, cout, ss, rs):
        for cp in copies(cin, cout, ss, rs):
            cp.start()

    def finish(cin, cout, ss, rs):
        for cp in copies(cin, cout, ss, rs):
            cp.wait()

    args = tuple(sums_bf) + ((small,) if small is not None else ())
    out_shape = tuple(jax.ShapeDtypeStruct((3,) + s.shape[1:], BF) for s in sums_bf)
    out_shape += (jax.ShapeDtypeStruct((3,) + small.shape, F32),) if small is not None else ()
    return _Comm(args, out_shape, {}, 3 * (n + 1), (("first", start), ("last", finish)))


def _join_comm(shards):
    n = len(shards)
    halves = [s.shape[0] // 2 for s in shards]

    def start(cin, cout, ss, rs):
        x, y, c, _ = _place()
        for a in range(n):
            mine = cout[a].at[pl.ds(c * halves[a], halves[a]), :]
            _remote(mine, mine, ss, rs, a, (x, y, 1 - c)).start()

    def finish(cin, cout, ss, rs):
        x, y, c, _ = _place()
        for a in range(n):
            theirs = cout[a].at[pl.ds((1 - c) * halves[a], halves[a]), :]
            cp = _remote(theirs, theirs, ss, rs, a, (x, y, 1 - c))
            cp.wait_recv()
            cp.wait_send()

    out_shape = tuple(jax.ShapeDtypeStruct(s.shape, F32) for s in shards)
    return _Comm(tuple(shards), out_shape, {a: a for a in range(n)}, n, (("first", start), ("last", finish)))


def _cast_shards(w_in, w_pa, w_pb, w_o, w_ff1, w_ff2, chip):
    def body(j_ref, win_ref, wpa_ref, wpb_ref, wo_ref, wff1_ref, wff2_ref, win4_ref, proj4_ref, ff14_ref, ff24_ref):
        win4_ref[...] = win_ref[...].astype(BF)
        for name, ref in (("w_pa", wpa_ref), ("w_pb", wpb_ref), ("w_o", wo_ref)):
            off, rows = PROJ_OFF[name]
            proj4_ref[off:off + rows, :] = ref[...].astype(BF)
        ff14_ref[...] = wff1_ref[...].astype(BF)
        ff24_ref[...] = wff2_ref[...].astype(BF)

    whole = lambda a: pl.BlockSpec(a.shape, lambda i, j: (0, 0), pipeline_mode=pl.Buffered(1))
    slot = lambda rows, cols: pl.BlockSpec((None, rows, cols), lambda i, j: (j[0], 0, 0))
    ws = (w_in, w_pa, w_pb, w_o, w_ff1, w_ff2)
    shapes = ((D, NP_SHARD), (PROJ_TOTAL, D), (D, FF_SHARD), (FF_SHARD, D))
    return _pc(
        body, name="cast_shards",
        grid_spec=pltpu.PrefetchScalarGridSpec(num_scalar_prefetch=1, grid=(1,), in_specs=[whole(w) for w in ws],
                                               out_specs=[slot(*s) for s in shapes]),
        out_shape=[jax.ShapeDtypeStruct((N_CHIP,) + s, BF) for s in shapes],
        compiler_params=_params("arbitrary"))(chip, *ws)


def _proj_fwd(x, chip, tm, comm):
    t = x.shape[0]
    sub = tm // EDGE_TILE

    def body(x_ref, p_ref, edge_ref, w_ref, w_sem, comm_refs):
        @pl.when(pl.program_id(1) == 0)
        def _():
            _, _, _, j = _place()
            block = pltpu.make_async_copy(comm_refs[0].at[j ^ pl.program_id(0)], w_ref, w_sem)
            block.start()
            block.wait()

        p_ref[...] = _dot(x_ref[...].astype(BF), w_ref[...])
        _write_edges(edge_ref, p_ref, tm)

    return _host_call(
        body, comm, name="proj_fwd", grid=(N_CHIP, t // tm), prefetch=chip, body_reads_comm=True,
        in_specs=[pl.BlockSpec((tm, D), lambda r, i, j: (i, 0))],
        out_specs=[pl.BlockSpec((tm, NP_SHARD), lambda r, i, j: (i, j[0] ^ r)),
                   pl.BlockSpec((sub, 2 * HALO, NP_SHARD), lambda r, i, j: (i, 0, j[0] ^ r))],
        out_shape=[jax.ShapeDtypeStruct((t, N_PROJ), F32), jax.ShapeDtypeStruct((t // EDGE_TILE, 2 * HALO, N_PROJ), F32)],
        scratch_shapes=[pltpu.VMEM((D, NP_SHARD), BF), pltpu.SemaphoreType.DMA],
        args=(x,))


def _edge_specs(t, tm, w):
    k, last = tm // EDGE_TILE, t // EDGE_TILE - 1
    return [pl.BlockSpec((None, HALO, w), lambda i: (jnp.maximum(i * k - 1, 0), 1, 0)),
            pl.BlockSpec((None, HALO, w), lambda i: (jnp.minimum((i + 1) * k, last), 0, 0))]


def _write_edges(edge_ref, rows, tm):
    for s in range(tm // EDGE_TILE):
        edge_ref[s, 0:HALO, :] = rows[s * EDGE_TILE:s * EDGE_TILE + HALO, :]
        edge_ref[s, HALO:2 * HALO, :] = rows[(s + 1) * EDGE_TILE - HALO:(s + 1) * EDGE_TILE, :]


def _pcols(p_ref, lo, hi):
    return p_ref[:, lo:hi]


def _end_masks(nt):
    i = pl.program_id(0)
    return (i > 0).astype(F32), (i < nt - 1).astype(F32)


def _conv_fwd(p_ref, prev_ref, next_ref, cw_ref, tm, has_prev, has_next):
    ca = _pcols(p_ref, OFF_CA, OFF_HA)
    ha = _pcols(p_ref, OFF_HA, OFF_UB)
    ch = ca * ha
    ch_prev = prev_ref[HALO - 1:HALO, OFF_CA:OFF_HA] * prev_ref[HALO - 1:HALO, OFF_HA:OFF_UB] * has_prev
    ch_next = next_ref[0:1, OFF_CA:OFF_HA] * next_ref[0:1, OFF_HA:OFF_UB] * has_next
    row = lax.broadcasted_iota(jnp.int32, (tm, W_A), 0)
    ch_m1 = jnp.where(row == 0, ch_prev, pltpu.roll(ch, 1, 0))
    ch_p1 = jnp.where(row == tm - 1, ch_next, pltpu.roll(ch, tm - 1, 0))
    cv = cw_ref[0:1, :] * ch_m1 + cw_ref[1:2, :] * ch + cw_ref[2:3, :] * ch_p1
    return ca, ha, ch, ch_m1, ch_p1, cv


def _spatial_fwd(p_ref, vg_ref, vb_ref, ws_ref, bsf_ref, vnb_ref, mixed_ref, tm):
    vb_pre = _pcols(p_ref, OFF_VB, OFF_GA)
    gv, tv = _gelu(vb_pre)
    xhv, rstdv = _ln_stats(gv)
    vnb_ref[...] = (xhv * vg_ref[...] + vb_ref[...]).astype(BF)
    for c in range(tm // CHUNK):
        rows = slice(c * CHUNK, (c + 1) * CHUNK)
        for h in range(N_HEAD):
            cols = slice(h * CHUNK, (h + 1) * CHUNK)
            mixed_ref[rows, cols] = _dot(ws_ref[h], vnb_ref[rows, cols]) + bsf_ref[:, cols]
    return vb_pre, tv, xhv, rstdv


def _mix_fwd(p, pedge, x, wpa, wpb, wo, wsb, bsf, bg, cw, vg, vb, tm):
    t = x.shape[0]
    nt = t // tm

    def body(p_ref, prev_ref, next_ref, x_ref, wpa_ref, wpb_ref, wo_ref, ws_ref, bsf_ref, bg_ref, cw_ref, vg_ref, vb_ref,
             r1_ref, ya_ref, yb_ref, vnb_ref, mixed_ref):
        has_prev, has_next = _end_masks(nt)
        _, _, _, _, _, cv = _conv_fwd(p_ref, prev_ref, next_ref, cw_ref, tm, has_prev, has_next)
        a = _pcols(p_ref, 0, OFF_CA) * cv
        ya = _dot(a.astype(BF), wpa_ref[...])
        ya_ref[...] = ya
        _spatial_fwd(p_ref, vg_ref, vb_ref, ws_ref, bsf_ref, vnb_ref, mixed_ref, tm)
        gu, _ = _gelu(_pcols(p_ref, OFF_UB, OFF_VB))
        bb = gu * mixed_ref[...]
        yb = _dot(bb.astype(BF), wpb_ref[...])
        yb_ref[...] = yb
        ga = jax.nn.sigmoid(_pcols(p_ref, OFF_GA, OFF_GB) + bg_ref[:, 0:D])
        gb = jax.nn.sigmoid(_pcols(p_ref, OFF_GB, N_PROJ) + bg_ref[:, D:2 * D])
        z = ga * ya + gb * yb
        r1_ref[...] = ALPHA * x_ref[...] + _dot(z.astype(BF), wo_ref[...])

    tile = lambda w: pl.BlockSpec((tm, w), lambda i: (i, 0))
    return _pc(
        body, name="mix_fwd", grid=(nt,),
        in_specs=[tile(N_PROJ), *_edge_specs(t, tm, N_PROJ), tile(D),
                  _resident((W_A, D)), _resident((W_B, D)), _resident((D, D)), _resident((N_HEAD, CHUNK, CHUNK)),
                  _resident((CHUNK, W_B)), _resident((1, 2 * D)), _resident((3, W_A)), _resident((1, W_B)),
                  _resident((1, W_B))],
        out_specs=[tile(D), tile(D), tile(D)],
        out_shape=[jax.ShapeDtypeStruct((t, D), F32)] * 3,
        scratch_shapes=[pltpu.VMEM((tm, W_B), BF), pltpu.VMEM((tm, W_B), F32)],
        compiler_params=_params("arbitrary"),
    )(p, pedge, pedge, x, wpa, wpb, wo, wsb, bsf, bg, cw, vg, vb)


def _ffn_fwd_bwd(r1, tgt, wff1, wff2, ln1g, ln1b, ln2g, ln2b, tm):
    t = r1.shape[0]

    def body(r1_ref, tgt_ref, w1_ref, w2_ref, g1_ref, b1_ref, g2_ref, b2_ref,
             dr1_ref, dr1b_ref, dedge_ref, x1b_ref, hidb_ref, dh1b_ref, dr2b_ref, acc_ref, relu_ref):
        @pl.when(pl.program_id(0) == 0)
        def _():
            acc_ref[...] = jnp.zeros_like(acc_ref)

        xh1, rstd1 = _ln_stats(r1_ref[...])
        x1 = xh1 * g1_ref[...] + b1_ref[...]
        x1b_ref[...] = x1.astype(BF)
        ffn = jnp.zeros((tm, D), F32)
        for j in range(N_CHIP):
            cols = slice(j * FF_SHARD, (j + 1) * FF_SHARD)
            r = jnp.maximum(_dot(x1b_ref[...], w1_ref[j]), 0.0)
            relu_ref[:, cols] = r
            hidb_ref[:, cols] = (r * r).astype(BF)
            ffn = ffn + _dot(hidb_ref[:, cols], w2_ref[cols, :])
        xh2, rstd2 = _ln_stats(ALPHA * x1 + ffn)
        diff = xh2 * g2_ref[...] + b2_ref[...] - tgt_ref[...]
        acc_ref[4:5, :] += _colsum(diff * diff)
        dx2 = diff * (1.0 / D)
        acc_ref[2:3, :] += _colsum(dx2 * xh2)
        acc_ref[3:4, :] += _colsum(dx2)
        dr2 = _ln_bwd(dx2, g2_ref[...], xh2, rstd2)
        dr2b_ref[...] = dr2.astype(BF)
        dx1 = ALPHA * dr2
        for j in range(N_CHIP):
            cols = slice(j * FF_SHARD, (j + 1) * FF_SHARD)
            dhid = _dot_nt(dr2b_ref[...], w2_ref[cols, :])
            dh1b_ref[:, cols] = (dhid * (2.0 * relu_ref[:, cols])).astype(BF)
            dx1 = dx1 + _dot_nt(dh1b_ref[:, cols], w1_ref[j])
        acc_ref[0:1, :] += _colsum(dx1 * xh1)
        acc_ref[1:2, :] += _colsum(dx1)
        dr1 = _ln_bwd(dx1, g1_ref[...], xh1, rstd1)
        dr1_ref[...] = dr1
        dr1b_ref[...] = dr1.astype(BF)
        _write_edges(dedge_ref, dr1_ref, tm)

    tile = lambda w: pl.BlockSpec((tm, w), lambda i: (i, 0))
    vec = _resident((1, D))
    return _pc(
        body, name="ffn_fwd_bwd", grid=(t // tm,),
        in_specs=[tile(D), tile(D), _resident((N_CHIP, D, FF_SHARD)), _resident((D_FF, D)), vec, vec, vec, vec],
        out_specs=[tile(D), tile(D), pl.BlockSpec((tm // EDGE_TILE, 2 * HALO, D), lambda i: (i, 0, 0)), tile(D), tile(D_FF),
                   tile(D_FF), tile(D), pl.BlockSpec((8, D), lambda i: (0, 0))],
        out_shape=[jax.ShapeDtypeStruct((t, D), F32), jax.ShapeDtypeStruct((t, D), BF),
                   jax.ShapeDtypeStruct((t // EDGE_TILE, 2 * HALO, D), F32), jax.ShapeDtypeStruct((t, D), BF),
                   jax.ShapeDtypeStruct((t, D_FF), BF), jax.ShapeDtypeStruct((t, D_FF), BF),
                   jax.ShapeDtypeStruct((t, D), BF), jax.ShapeDtypeStruct((8, D), F32)],
        scratch_shapes=[pltpu.VMEM((tm, D_FF), F32)],
        compiler_params=_params("arbitrary"),
    )(r1, tgt, wff1, wff2, ln1g, ln1b, ln2g, ln2b)


def _dw(a, b, nblk, am, bn, a_blocked, b_blocked, tk, name, comm=None, after=None):
    t = a.shape[0]

    def body(a_ref, b_ref, *rest):
        o_ref = rest[-1]

        @pl.when(pl.program_id(1) == 0)
        def _():
            o_ref[...] = jnp.zeros_like(o_ref)

        o_ref[...] += _dot_tn(a_ref[...].astype(BF), b_ref[...])

    outs, got = _host_call(
        body, comm, name=name, grid=(nblk, t // tk),
        in_specs=[pl.BlockSpec((tk, am), (lambda j, k: (k, j)) if a_blocked else (lambda j, k: (k, 0))),
                  pl.BlockSpec((tk, bn), (lambda j, k: (k, j)) if b_blocked else (lambda j, k: (k, 0)))]
        + ([] if after is None else [pl.BlockSpec(memory_space=pl.ANY)]),
        out_specs=[pl.BlockSpec((None, am, bn), lambda j, k: (j, 0, 0))],
        out_shape=[jax.ShapeDtypeStruct((nblk, am, bn), F32)], args=(a, b) + (() if after is None else (after,)))
    return outs[0] if comm is None else (outs[0], got)


def _dw_proj(ab, dyab, bbb, dybb, zb, dr1b, tk, comm):
    t = ab.shape[0]
    pairs = (("w_pa", 0, 1), ("w_pb", 2, 3), ("w_o", 4, 5))

    def body(*refs):
        o_ref = refs[6]

        @pl.when(pl.program_id(0) == 0)
        def _():
            o_ref[...] = jnp.zeros_like(o_ref)

        for name, ia, ib in pairs:
            off, rows = PROJ_OFF[name]
            for k in range(N_CHIP):
                o_ref[k, off:off + rows, :] += _dot_tn(refs[ia][:, k * rows:(k + 1) * rows], refs[ib][...])

    tile = lambda w: pl.BlockSpec((tk, w), lambda i: (i, 0))
    outs, got = _host_call(
        body, comm, name="dw_proj", grid=(t // tk,), in_specs=[tile(W_A), tile(D), tile(W_B), tile(D), tile(D), tile(D)],
        out_specs=[pl.BlockSpec((N_CHIP, PROJ_TOTAL, D), lambda i: (0, 0, 0))],
        out_shape=[jax.ShapeDtypeStruct((N_CHIP, PROJ_TOTAL, D), F32)], args=(ab, dyab, bbb, dybb, zb, dr1b))
    return outs[0], got


def _dx(dp, win4, dr1, tm, blk0, nblk, filled, name, comm, after=None):
    t = dp.shape[0]

    def body(dp_ref, w_ref, dr1_ref, *rest):
        dx = ALPHA * dr1_ref[...]
        for j in range(N_CHIP):
            dx = dx + _dot_nt(dp_ref[:, j * NP_SHARD:(j + 1) * NP_SHARD], w_ref[j])
        rest[-1][...] = dx

    in_specs = [pl.BlockSpec((tm, N_PROJ), lambda i: (i + blk0, 0)), _resident((N_CHIP, D, NP_SHARD)),
                pl.BlockSpec((tm, D), lambda i: (i + blk0, 0))]
    args = (dp, win4, dr1)
    aliases = None
    if filled is not None:
        in_specs.append(pl.BlockSpec(memory_space=pl.ANY))
        args += (filled,)
        aliases = {3: 0}
    if after is not None:
        in_specs.append(pl.BlockSpec(memory_space=pl.ANY))
        args += (after,)
    outs, got = _host_call(
        body, comm, name=name, grid=(nblk,), in_specs=in_specs, out_specs=[pl.BlockSpec((tm, D), lambda i: (i + blk0, 0))],
        out_shape=[jax.ShapeDtypeStruct((t, D), F32)], args=args, aliases=aliases)
    return outs[0], got


def _mix_bwd(p, pedge, dr1, dedge, ya, yb, wpa, wpb, wo, wsb, wstb, bsf, bg, cw, vg, vb, tm, comm):
    t = p.shape[0]
    nt = t // tm
    te = tm + 2 * HALO
    mid = slice(HALO, HALO + tm)

    def body(p_ref, prev_ref, next_ref, dr1_ref, dprev_ref, dnext_ref, ya_ref, yb_ref, wpa_ref, wpb_ref, wo_ref,
             ws_ref, wst_ref, bsf_ref, bg_ref, cw_ref, vg_ref, vb_ref,
             dp_ref, ab_ref, bbb_ref, zb_ref, dyab_ref, dybb_ref, dbg_ref, dcw_ref, dvgb_ref, dws_ref, dbs_ref,
             vnb_ref, mixed_ref, dmixb_ref, dvn_ref):
        @pl.when(pl.program_id(0) == 0)
        def _():
            for r in (dbg_ref, dcw_ref, dvgb_ref, dws_ref, dbs_ref):
                r[...] = jnp.zeros_like(r)

        has_prev, has_next = _end_masks(nt)
        ca, ha, ch, ch_m1, ch_p1, cv = _conv_fwd(p_ref, prev_ref, next_ref, cw_ref, tm, has_prev, has_next)
        ba = _pcols(p_ref, 0, OFF_CA)
        ab_ref[...] = (ba * cv).astype(BF)
        vb_pre, tv, xhv, rstdv = _spatial_fwd(p_ref, vg_ref, vb_ref, ws_ref, bsf_ref, vnb_ref, mixed_ref, tm)
        ub = _pcols(p_ref, OFF_UB, OFF_VB)
        gu, tu = _gelu(ub)
        bbb_ref[...] = (gu * mixed_ref[...]).astype(BF)
        bga = bg_ref[:, 0:D]
        ga = jax.nn.sigmoid(_pcols(p_ref, OFF_GA, OFF_GB) + bga)
        gb = jax.nn.sigmoid(_pcols(p_ref, OFF_GB, N_PROJ) + bg_ref[:, D:2 * D])
        ya = ya_ref[...]
        yb = yb_ref[...]
        zb_ref[...] = (ga * ya + gb * yb).astype(BF)

        dr1_ext = jnp.concatenate([dprev_ref[...] * has_prev, dr1_ref[...], dnext_ref[...] * has_next], axis=0)
        dz_ext = _dot_nt(dr1_ext.astype(BF), wo_ref[...])
        ga_ext = jnp.concatenate([jax.nn.sigmoid(prev_ref[:, OFF_GA:OFF_GB] + bga), ga,
                                  jax.nn.sigmoid(next_ref[:, OFF_GA:OFF_GB] + bga)], axis=0)
        dya_ext = dz_ext * ga_ext
        dyab_ref[...] = dya_ext[mid].astype(BF)
        da_ext = _dot_nt(dya_ext.astype(BF), wpa_ref[...])
        ba_ext = jnp.concatenate([prev_ref[:, 0:OFF_CA], ba, next_ref[:, 0:OFF_CA]], axis=0)
        dcv_ext = da_ext * ba_ext
        dcv = dcv_ext[mid]
        dch = (cw_ref[0:1, :] * pltpu.roll(dcv_ext, te - 1, 0)[mid] + cw_ref[1:2, :] * dcv
               + cw_ref[2:3, :] * pltpu.roll(dcv_ext, 1, 0)[mid])
        dp_ref[:, 0:OFF_CA] = (da_ext[mid] * cv).astype(BF)
        dp_ref[:, OFF_CA:OFF_HA] = (dch * ha).astype(BF)
        dp_ref[:, OFF_HA:OFF_UB] = (dch * ca).astype(BF)
        dcw_ref[0:1, :] += _colsum(dcv * ch_m1)
        dcw_ref[1:2, :] += _colsum(dcv * ch)
        dcw_ref[2:3, :] += _colsum(dcv * ch_p1)

        dz = dz_ext[mid]
        dga = dz * ya * ga * (1.0 - ga)
        dgb = dz * yb * gb * (1.0 - gb)
        dp_ref[:, OFF_GA:OFF_GB] = dga.astype(BF)
        dp_ref[:, OFF_GB:N_PROJ] = dgb.astype(BF)
        dbg_ref[0:1, 0:D] += _colsum(dga)
        dbg_ref[0:1, D:2 * D] += _colsum(dgb)

        dybb_ref[...] = (dz * gb).astype(BF)
        dbb = _dot_nt(dybb_ref[...], wpb_ref[...])
        dp_ref[:, OFF_UB:OFF_VB] = (dbb * mixed_ref[...] * _gelu_grad(ub, tu)).astype(BF)
        dmixed = dbb * gu
        dmixb_ref[...] = dmixed.astype(BF)
        for c in range(tm // CHUNK):
            rows = slice(c * CHUNK, (c + 1) * CHUNK)
            dbs_ref[...] += dmixed[rows]
            for h in range(N_HEAD):
                cols = slice(h * CHUNK, (h + 1) * CHUNK)
                dws_ref[h] += _dot_nt(dmixb_ref[rows, cols], vnb_ref[rows, cols])
                dvn_ref[rows, cols] = _dot(wst_ref[h], dmixb_ref[rows, cols])
        dvn = dvn_ref[...]
        dvgb_ref[0:1, :] += _colsum(dvn * xhv)
        dvgb_ref[1:2, :] += _colsum(dvn)
        dgv = _ln_bwd(dvn, vg_ref[...], xhv, rstdv)
        dp_ref[:, OFF_VB:OFF_GA] = (dgv * _gelu_grad(vb_pre, tv)).astype(BF)

    tile = lambda w: pl.BlockSpec((tm, w), lambda i: (i, 0))
    acc = lambda *s: pl.BlockSpec(s, lambda i: (0,) * len(s))
    return _host_call(
        body, comm, name="mix_bwd", grid=(nt,),
        in_specs=[tile(N_PROJ), *_edge_specs(t, tm, N_PROJ), tile(D), *_edge_specs(t, tm, D), tile(D), tile(D),
                  _resident((W_A, D)), _resident((W_B, D)), _resident((D, D)), _resident((N_HEAD, CHUNK, CHUNK)),
                  _resident((N_HEAD, CHUNK, CHUNK)), _resident((CHUNK, W_B)), _resident((1, 2 * D)),
                  _resident((3, W_A)), _resident((1, W_B)), _resident((1, W_B))],
        out_specs=[tile(N_PROJ), tile(W_A), tile(W_B), tile(D), tile(D), tile(D),
                   acc(8, 2 * D), acc(8, W_A), acc(8, W_B), acc(N_HEAD, CHUNK, CHUNK), acc(CHUNK, W_B)],
        out_shape=[jax.ShapeDtypeStruct((t, N_PROJ), BF), jax.ShapeDtypeStruct((t, W_A), BF),
                   jax.ShapeDtypeStruct((t, W_B), BF), jax.ShapeDtypeStruct((t, D), BF), jax.ShapeDtypeStruct((t, D), BF),
                   jax.ShapeDtypeStruct((t, D), BF), jax.ShapeDtypeStruct((8, 2 * D), F32),
                   jax.ShapeDtypeStruct((8, W_A), F32), jax.ShapeDtypeStruct((8, W_B), F32),
                   jax.ShapeDtypeStruct((N_HEAD, CHUNK, CHUNK), F32), jax.ShapeDtypeStruct((CHUNK, W_B), F32)],
        scratch_shapes=[pltpu.VMEM((tm, W_B), BF), pltpu.VMEM((tm, W_B), F32), pltpu.VMEM((tm, W_B), BF),
                        pltpu.VMEM((tm, W_B), F32)],
        args=(p, pedge, pedge, dr1, dedge, dedge, ya, yb, wpa, wpb, wo, wsb, wstb, bsf, bg, cw, vg, vb))


def _add_own_half(full4, recv4, place, rb, name, after=None):
    n, rh, cols = recv4.shape
    nb = rh // rb

    def body(pl_ref, a_ref, b_ref, *rest):
        own_ref, ob_ref = rest[-2:]
        s = a_ref[...] + b_ref[...]
        ob_ref[...] = s.astype(BF)

        @pl.when(pl.program_id(1) == pl_ref[0])
        def _():
            own_ref[...] = s

    blk = (None, rb, cols)
    return _pc(
        body, name=name,
        grid_spec=pltpu.PrefetchScalarGridSpec(
            num_scalar_prefetch=1, grid=(nb, n),
            in_specs=[pl.BlockSpec(blk, lambda i, k, s: (k, s[1] * nb + i, 0)), pl.BlockSpec(blk, lambda i, k, s: (k, i, 0))]
            + ([] if after is None else [pl.BlockSpec(memory_space=pl.ANY)]),
            out_specs=[pl.BlockSpec((rb, cols), lambda i, k, s: (i, 0)), pl.BlockSpec(blk, lambda i, k, s: (k, i, 0))]),
        out_shape=[jax.ShapeDtypeStruct((rh, cols), F32), jax.ShapeDtypeStruct(recv4.shape, BF)],
        compiler_params=_params("arbitrary", "arbitrary"),
    )(place, full4, recv4, *(() if after is None else (after,)))


def _add_chips(own, r3, place, rb, name, after=None):
    _, rh, cols = r3.shape
    nb = rh // rb

    def body(pl_ref, s_ref, r_ref, *rest):
        rest[-1][...] = ((s_ref[...] + r_ref[0].astype(F32)) + r_ref[1].astype(F32)) + r_ref[2].astype(F32)

    return _pc(
        body, name=name,
        grid_spec=pltpu.PrefetchScalarGridSpec(
            num_scalar_prefetch=1, grid=(nb,),
            in_specs=[pl.BlockSpec((rb, cols), lambda i, s: (i, 0)), pl.BlockSpec((3, rb, cols), lambda i, s: (0, i, 0))]
            + ([] if after is None else [pl.BlockSpec(memory_space=pl.ANY)]),
            out_specs=pl.BlockSpec((rb, cols), lambda i, s: (s[1] * nb + i, 0))),
        out_shape=jax.ShapeDtypeStruct((2 * rh, cols), F32),
        compiler_params=_params("arbitrary"),
    )(place, own, r3, *(() if after is None else (after,)))


def _add_small(a, b):
    def body(a_ref, b_ref, o_ref):
        o_ref[...] = a_ref[...] + b_ref[...]

    return _pc(body, name="add_small_cores", out_shape=jax.ShapeDtypeStruct(a.shape, F32))(a, b)


def _sum_small_chips(own, slots, place):
    def body(pl_ref, own_ref, s_ref, o_ref):
        j = pl_ref[0]

        def term(k):
            return jnp.where(j == k, own_ref[...], s_ref[jnp.maximum((j ^ k) - 1, 0)])

        o_ref[...] = ((term(0) + term(1)) + term(2)) + term(3)

    vmem = pl.BlockSpec(memory_space=pltpu.VMEM)
    return _pc(body, name="sum_small_chips", in_specs=[pl.BlockSpec(memory_space=pltpu.SMEM), vmem, vmem], out_specs=vmem,
               out_shape=jax.ShapeDtypeStruct(own.shape, F32))(place, own, slots)


def _adamw_step(w, g, m, v):
    m2 = ADAM_B1 * m + (1.0 - ADAM_B1) * g
    v2 = ADAM_B2 * v + (1.0 - ADAM_B2) * (g * g)
    m_hat = m2 / (1.0 - ADAM_B1 ** ADAM_STEP)
    v_hat = v2 / (1.0 - ADAM_B2 ** ADAM_STEP)
    return -ADAM_LR * (m_hat / (jnp.sqrt(v_hat) + ADAM_EPS) + ADAM_WD * w), m2, v2


def _adamw(w, g, m, v, rb, name):
    rows, cols = w.shape

    def body(w_ref, g_ref, m_ref, v_ref, d_ref, m2_ref, v2_ref):
        d_ref[...], m2_ref[...], v2_ref[...] = _adamw_step(w_ref[...], g_ref[...], m_ref[...], v_ref[...])

    blk = pl.BlockSpec((rb, cols), lambda i: (i, 0))
    return _pc(body, name=name, grid=(rows // rb,), in_specs=[blk] * 4, out_specs=[blk] * 3,
               out_shape=[jax.ShapeDtypeStruct((rows, cols), F32)] * 3, compiler_params=_params("arbitrary"))(w, g, m, v)


SC_TILES = 32
SC_LANES = 16
SC_ROWS = 16


def _adamw_sc(ws, gs, ms, vs):
    n = len(ws)
    rows, cols = ws[0].shape
    per_tile = rows // SC_TILES

    def body(*refs):
        ins, outs, (wb, gb, mb, vb, db) = refs[:4 * n], refs[4 * n:7 * n], refs[7 * n:]
        tile = lax.axis_index("sc_subcore") * 2 + lax.axis_index("sc_core")
        for i in range(n):
            for ps in range(per_tile // SC_ROWS):
                slab = pl.ds(tile * per_tile + ps * SC_ROWS, SC_ROWS)
                for k, buf in enumerate((wb, gb, mb, vb)):
                    pltpu.sync_copy(ins[k * n + i].at[slab, :], buf)

                @pl.loop(0, SC_ROWS)
                def _(r):
                    @pl.loop(0, cols, step=SC_LANES)
                    def _(c):
                        at = (r, pl.ds(c, SC_LANES))
                        db[at], mb[at], vb[at] = _adamw_step(wb[at], gb[at], mb[at], vb[at])

                for k, buf in enumerate((db, mb, vb)):
                    pltpu.sync_copy(buf, outs[k * n + i].at[slab, :])

    outs = pl.kernel(
        body, name="adamw_sc", out_type=[jax.ShapeDtypeStruct((rows, cols), F32)] * (3 * n),
        mesh=plsc.VectorSubcoreMesh(core_axis_name="sc_core", subcore_axis_name="sc_subcore"),
        scratch_types=[pltpu.VMEM((SC_ROWS, cols), F32)] * 5,
    )(*ws, *gs, *ms, *vs)
    return outs[:n], outs[n:2 * n], outs[2 * n:]


def _adamw_small(ws, gs, ms, vs):
    n = len(ws)

    def body(*refs):
        ins, outs = refs[:4 * n], refs[4 * n:]
        for i in range(n):
            outs[i][...], outs[n + i][...], outs[2 * n + i][...] = _adamw_step(*(ins[k * n + i][...] for k in range(4)))

    outs = _pc(body, name="adamw_small", out_shape=[jax.ShapeDtypeStruct(w.shape, F32) for w in ws] * 3)(*ws, *gs, *ms, *vs)
    return outs[:n], outs[n:2 * n], outs[2 * n:]


LANES = 128
SMALL_GRADS = (("b_gate", 2 * D), ("conv_w", 3 * W_A), ("v_norm_g", W_B), ("v_norm_b", W_B),
               ("w_s", N_HEAD * CHUNK * CHUNK), ("b_s", N_HEAD * CHUNK), ("ln1_g", D), ("ln1_b", D), ("ln2_g", D), ("ln2_b", D),
               ("loss", 1))


def _pack_rows(parts):
    rows = []
    for a in parts:
        a = a.reshape(-1)
        a = jnp.pad(a, (0, (-a.shape[0]) % LANES))
        rows.append(a.reshape(-1, LANES))
    out = jnp.concatenate(rows, axis=0)
    return jnp.pad(out, ((0, (-out.shape[0]) % 8), (0, 0)))


def _unpack_rows(buf, sizes):
    out, r = [], 0
    for n in sizes:
        nr = -(-n // LANES)
        out.append(buf[r:r + nr].reshape(-1)[:n])
        r += nr
    return out


TM_PROJ = 1024
TM_MIX = 256
TM_DX = 512
DX_PAIR = 6
TK_DW = 4096
TK_DW_IN = 2048
TK_DW_PROJ = 1024
ADD_BLOCK_BYTES = 3 * 1024 * 1024
RB_ADAM = 128
CONV_ROWS = 8


def _add_rows(rows, cols):
    while rows * cols * 4 > ADD_BLOCK_BYTES and rows % 32 == 0:
        rows //= 2
    return rows


def _reduce_adds_1(grads, recvs, place, tag, after=None):
    out = [_add_own_half(g, r, place, _add_rows(*r.shape[1:]), f"add_cores_{tag}{a}", after) for a, (g, r) in enumerate(zip(grads, recvs))]
    return [o[0] for o in out], [o[1] for o in out]


def _reduce_adds_2(sums, recvs, place, tag, after=None):
    return [_add_chips(s, r, place, _add_rows(*r.shape[1:]), f"add_chips_{tag}{a}", after) for a, (s, r) in enumerate(zip(sums, recvs))]


def kernel(x, w_in, b_gate, conv_w, v_norm_g, v_norm_b, w_s, b_s, w_pa, w_pb, w_o, ln1_g, ln1_b, w_ff1, w_ff2, ln2_g, ln2_b, loss_target, m_w_in, m_b_gate, m_conv_w, m_v_norm_g, m_v_norm_b, m_w_s, m_b_s, m_w_pa, m_w_pb, m_w_o, m_ln1_g, m_ln1_b, m_w_ff1, m_w_ff2, m_ln2_g, m_ln2_b, v_w_in, v_b_gate, v_conv_w, v_v_norm_g, v_v_norm_b, v_w_s, v_b_s, v_w_pa, v_w_pb, v_w_o, v_ln1_g, v_ln1_b, v_w_ff1, v_w_ff2, v_ln2_g, v_ln2_b):
    t = x.shape[1]
    core = lax.axis_index("c").astype(jnp.int32).reshape(1)
    chip_idx = 2 * lax.axis_index("x") + lax.axis_index("y")
    chip = chip_idx.astype(jnp.int32).reshape(1)
    place = jnp.concatenate([chip, core])
    x2 = x.reshape(t, D)
    tgt = loss_target.reshape(t, D)

    win4, proj4, ff14, ff24 = _cast_shards(w_in[0], w_pa[0], w_pb[0], w_o[0], w_ff1[0], w_ff2[0], chip)
    conv4 = lax.dynamic_update_slice(jnp.zeros((N_CHIP, CONV_ROWS, W_A // N_CHIP), F32),
                                     jnp.pad(conv_w[0], ((0, CONV_ROWS - 3), (0, 0)))[None], (chip_idx, 0, 0))
    (p, pedge), (win4, proj4, ff14, ff24, conv4) = _proj_fwd(
        x2, chip, TM_PROJ, _gather_comm([win4, proj4, ff14, ff24], conv4, eager=1))

    def full(name, rows_total):
        off, rows = PROJ_OFF[name]
        return proj4[:, off:off + rows, :].reshape(rows_total, D)

    wpa, wpb, wo = full("w_pa", W_A), full("w_pb", W_B), full("w_o", D)
    wff2 = ff24.reshape(D_FF, D)
    cw = jnp.transpose(conv4[:, :3, :], (1, 0, 2)).reshape(3, W_A)
    wsb = w_s[0].astype(BF)
    wstb = jnp.swapaxes(w_s[0], 1, 2).astype(BF)
    bsf = jnp.repeat(jnp.transpose(b_s[0]), CHUNK, axis=1)

    r1, ya, yb = _mix_fwd(p, pedge, x2, wpa, wpb, wo, wsb, bsf, b_gate, cw, v_norm_g, v_norm_b, TM_MIX)
    dr1, dr1b, dedge, x1b, hidb, dh1b, dr2b, acc = _ffn_fwd_bwd(r1, tgt, ff14, wff2, ln1_g, ln1_b, ln2_g, ln2_b, TM_MIX)
    g_ff = [_dw(x1b, dh1b, N_CHIP, D, FF_SHARD, False, True, TK_DW, "dw_ff1"),
            _dw(hidb, dr2b, N_CHIP, FF_SHARD, D, True, False, TK_DW, "dw_ff2")]
    r_ff = _sequencer_call(_pair_comm(g_ff), "pair_ff", 3, _sibling_peer)
    (dp, ab, bbb, zb, dyab, dybb, dbg, dcw, dvgb, dws, dbs_sum), _ = _mix_bwd(
        p, pedge, dr1, dedge, ya, yb, wpa, wpb, wo, wsb, wstb, bsf, b_gate, cw, v_norm_g, v_norm_b, TM_MIX, None)
    s_ff, sb_ff = _reduce_adds_1(g_ff, r_ff, place, "ff", after=dp)
    c_ff = _sequencer_call(_chips_comm(sb_ff), "chips_ff", 4, _chip_peers)
    dwin4 = _dw(x2, dp, N_CHIP, D, NP_SHARD, False, True, TK_DW_IN, "dw_in", after=sb_ff[0])
    f_ff = _reduce_adds_2(s_ff, c_ff, place, "ff", after=dwin4)
    dproj4, (g_ff1, g_ff2) = _dw_proj(ab, dyab, bbb, dybb, zb, dr1b, TK_DW_PROJ, _join_comm(f_ff))
    dbs = jnp.transpose(jnp.sum(dbs_sum.reshape(CHUNK, N_HEAD, CHUNK), axis=-1))
    small = _pack_rows([dbg[0], dcw[0:3], dvgb[0], dvgb[1], dws, dbs, acc[0], acc[1], acc[2], acc[3],
                        0.5 * jnp.sum(acc[4]) / D])
    g_rest = [dwin4, dproj4]
    nblk = t // TM_DX
    n_a = max(1, min(DX_PAIR, nblk // 4))
    r_rest = _sequencer_call(_pair_comm(g_rest, small), "pair_rest", 1, _sibling_peer)
    dx, _ = _dx(dp, win4, dr1, TM_DX, 0, n_a, None, "dx_a", None)
    s_rest, sb_rest = _reduce_adds_1(g_rest, r_rest[:2], place, "rest", after=dx)
    csmall = _add_small(small, r_rest[2])
    c_rest = _sequencer_call(_chips_comm(sb_rest, csmall), "chips_rest", 2, _chip_peers)
    dx, _ = _dx(dp, win4, dr1, TM_DX, n_a, nblk - n_a, dx, "dx_b", None, after=sb_rest[0])
    f_rest = _reduce_adds_2(s_rest, c_rest[:2], place, "rest")
    gsmall = _sum_small_chips(csmall, c_rest[2], place)
    g_in, g_proj = _sequencer_call(_join_comm(f_rest), "join_rest", 5, _sibling_peer)

    grads = {"w_in": g_in, "w_ff1": g_ff1, "w_ff2": g_ff2}
    for name, _ in PROJ_ROWS:
        off, rows = PROJ_OFF[name]
        grads[name] = g_proj[off:off + rows, :]
    for (name, n), flat in zip(SMALL_GRADS, _unpack_rows(gsmall, [n for _, n in SMALL_GRADS])):
        grads[name] = flat
    loss = grads.pop("loss").reshape(())
    grads["conv_w"] = lax.dynamic_slice(grads["conv_w"].reshape(3, W_A), (0, chip_idx * (W_A // N_CHIP)), (3, W_A // N_CHIP))

    weights = dict(w_in=w_in, b_gate=b_gate, conv_w=conv_w, v_norm_g=v_norm_g, v_norm_b=v_norm_b, w_s=w_s, b_s=b_s,
                   w_pa=w_pa, w_pb=w_pb, w_o=w_o, ln1_g=ln1_g, ln1_b=ln1_b, w_ff1=w_ff1, w_ff2=w_ff2, ln2_g=ln2_g, ln2_b=ln2_b)
    mom1 = dict(w_in=m_w_in, b_gate=m_b_gate, conv_w=m_conv_w, v_norm_g=m_v_norm_g, v_norm_b=m_v_norm_b, w_s=m_w_s,
                b_s=m_b_s, w_pa=m_w_pa, w_pb=m_w_pb, w_o=m_w_o, ln1_g=m_ln1_g, ln1_b=m_ln1_b, w_ff1=m_w_ff1,
                w_ff2=m_w_ff2, ln2_g=m_ln2_g, ln2_b=m_ln2_b)
    mom2 = dict(w_in=v_w_in, b_gate=v_b_gate, conv_w=v_conv_w, v_norm_g=v_v_norm_g, v_norm_b=v_v_norm_b, w_s=v_w_s,
                b_s=v_b_s, w_pa=v_w_pa, w_pb=v_w_pb, w_o=v_w_o, ln1_g=v_ln1_g, ln1_b=v_ln1_b, w_ff1=v_w_ff1,
                w_ff2=v_w_ff2, ln2_g=v_ln2_g, ln2_b=v_ln2_b)
    order = list(weights)
    big = ("w_in", "w_pa", "w_pb", "w_o", "w_ff1", "w_ff2")
    delta, new_m, new_v = {}, {}, {}
    early = ("w_ff1", "w_ff2")
    ds, ms, vs = _adamw_sc(*([d[n][0] if d is not grads else d[n] for n in early] for d in (weights, grads, mom1, mom2)))
    for name, d_, m_, v_ in zip(early, ds, ms, vs):
        delta[name], new_m[name], new_v[name] = d_, m_, v_
    for name in big:
        if name in early:
            continue
        w2 = weights[name][0]
        delta[name], new_m[name], new_v[name] = _adamw(w2, grads[name], mom1[name][0], mom2[name][0], RB_ADAM, "adamw_" + name)
    little = [n for n in order if n not in big]
    flat2d = lambda a: a.reshape(-1, a.shape[-1])
    ds, ms, vs = _adamw_small(*([flat2d(d[n].reshape(weights[n].shape)) for n in little] for d in (weights, grads, mom1, mom2)))
    for name, d_, m_, v_ in zip(little, ds, ms, vs):
        delta[name], new_m[name], new_v[name] = d_, m_, v_

    shaped = lambda d: [d[n].reshape(weights[n].shape) for n in order]
    return (loss, dx.reshape(x.shape), *shaped(grads), *shaped(delta), *shaped(new_m), *shaped(new_v))
```

```python
import functools
from typing import NamedTuple

import jax
import jax.numpy as jnp
from jax import lax
from jax.experimental import pallas as pl
from jax.experimental.pallas import tpu as pltpu
from jax.experimental.pallas import tpu_sc as plsc

D = 1024
W_A = 1536
W_B = 1024
CHUNK = 128
N_HEAD = 8
D_FF = 4096
N_PROJ = 3 * W_A + 2 * W_B + 2 * D
OFF_CA, OFF_HA, OFF_UB, OFF_VB, OFF_GA, OFF_GB = 1536, 3072, 4608, 5632, 6656, 7680
LN_EPS = 1e-5
ALPHA = 2.0 ** 0.25
N_CHIP = 4
NP_SHARD = N_PROJ // N_CHIP
FF_SHARD = D_FF // N_CHIP
ADAM_LR, ADAM_B1, ADAM_B2, ADAM_EPS, ADAM_WD, ADAM_STEP = 0.001, 0.9, 0.999, 1e-08, 0.01, 10

PROJ_ROWS = (("w_pa", W_A // N_CHIP), ("w_pb", W_B // N_CHIP), ("w_o", D // N_CHIP))
PROJ_OFF = {}
_o = 0
for _n, _r in PROJ_ROWS:
    PROJ_OFF[_n] = (_o, _r)
    _o += _r
PROJ_TOTAL = _o

V7X_VMEM_BYTES = 64 * 1024 * 1024
VMEM_LIMIT = V7X_VMEM_BYTES - 8 * 1024 * 1024
HALO = 8
EDGE_TILE = 256

BF = jnp.bfloat16
F32 = jnp.float32
MESH = pl.DeviceIdType.MESH
HBM_SPEC = pl.BlockSpec(memory_space=pltpu.HBM)


def _pc(body, **kw):
    return pl.pallas_call(body, **kw)


def _params(*sem):
    return pltpu.CompilerParams(dimension_semantics=sem, vmem_limit_bytes=VMEM_LIMIT)


def _resident(shape):
    n = len(shape)
    return pl.BlockSpec(shape, lambda *_: (0,) * n, pipeline_mode=pl.Buffered(1))


def _dot(a, b):
    return jnp.dot(a, b, preferred_element_type=F32)


def _dot_nt(a, b):
    return lax.dot_general(a, b, (((1,), (1,)), ((), ())), preferred_element_type=F32)


def _dot_tn(a, b):
    return lax.dot_general(a, b, (((0,), (0,)), ((), ())), preferred_element_type=F32)


def _gelu(x):
    t = jnp.tanh(0.7978845608028654 * (x + 0.044715 * (x * x * x)))
    return 0.5 * x * (1.0 + t), t


def _gelu_grad(x, t):
    return 0.5 * (1.0 + t) + 0.5 * x * (1.0 - t * t) * (0.7978845608028654 * (1.0 + 0.134145 * (x * x)))


def _ln_stats(r):
    mu = jnp.mean(r, axis=-1, keepdims=True)
    xc = r - mu
    var = jnp.mean(xc * xc, axis=-1, keepdims=True)
    rstd = lax.rsqrt(var + LN_EPS)
    return xc * rstd, rstd


def _ln_bwd(dy, g, xh, rstd):
    dxh = dy * g
    m1 = jnp.mean(dxh, axis=-1, keepdims=True)
    m2 = jnp.mean(dxh * xh, axis=-1, keepdims=True)
    return rstd * (dxh - m1 - xh * m2)


def _colsum(v):
    return jnp.sum(v, axis=0, keepdims=True)


class _Comm(NamedTuple):
    args: tuple
    out_shape: tuple
    aliases: dict
    n_sems: int
    stages: tuple


def _place():
    x, y, c = lax.axis_index("x"), lax.axis_index("y"), lax.axis_index("c")
    return x, y, c, 2 * x + y


def _flip(x, y, c, r):
    return (x ^ (r >> 1), y ^ (r & 1), c)


def _remote(src, dst, send_sems, recv_sems, k, peer):
    return pltpu.make_async_remote_copy(src_ref=src, dst_ref=dst, send_sem=send_sems.at[k], recv_sem=recv_sems.at[k],
                                        device_id=peer, device_id_type=MESH)


def _host_call(body, comm, *, name, grid, in_specs, out_specs, out_shape, args, scratch_shapes=(), aliases=None, prefetch=None,
               body_reads_comm=False):
    sem = ("arbitrary",) * len(grid)
    aliases = dict(aliases or {})
    n_pre = 0 if prefetch is None else 1
    n_in, n_out, n_scr = len(in_specs), len(out_specs), len(scratch_shapes)
    c_in, c_out = (0, 0) if comm is None else (len(comm.args), len(comm.out_shape))
    steps = {"first": (0,) * len(grid), "late": (grid[0] - 1,) + (0,) * (len(grid) - 1), "last": tuple(g - 1 for g in grid)}

    def wrapped(*refs):
        refs = refs[n_pre:]
        own_in, cin = refs[:n_in], refs[n_in:n_in + c_in]
        o0 = n_in + c_in
        own_out, cout = refs[o0:o0 + n_out], refs[o0 + n_out:o0 + n_out + c_out]
        s0 = o0 + n_out + c_out
        scr, sems = refs[s0:s0 + n_scr], refs[s0 + n_scr:]

        def run(before):
            for phase, fn in () if comm is None else comm.stages:
                at_step = isinstance(phase, tuple)
                if before != (at_step or phase == "first"):
                    continue
                step = phase if at_step else steps[phase]
                cond = pl.program_id(0) == step[0]
                for d in range(1, len(grid)):
                    cond = jnp.logical_and(cond, pl.program_id(d) == step[d])
                pl.when(cond)(functools.partial(fn, cin, cout, *sems))

        run(True)
        if body_reads_comm:
            body(*own_in, *own_out, *scr, comm_refs=cout)
        else:
            body(*own_in, *own_out, *scr)
        run(False)

    in_specs = list(in_specs) + [HBM_SPEC] * c_in
    out_specs = list(out_specs) + [HBM_SPEC] * c_out
    out_shape = list(out_shape) + ([] if comm is None else list(comm.out_shape))
    scratch_shapes = list(scratch_shapes) + ([] if comm is None else [pltpu.SemaphoreType.DMA((comm.n_sems,))] * 2)
    args = tuple(args) + (() if comm is None else tuple(comm.args))
    if comm is not None:
        aliases.update({n_in + i: n_out + o for i, o in comm.aliases.items()})
    aliases = {i + n_pre: o for i, o in aliases.items()}
    if prefetch is None:
        kw = dict(grid=grid, in_specs=in_specs, out_specs=out_specs, scratch_shapes=scratch_shapes)
    else:
        kw = dict(grid_spec=pltpu.PrefetchScalarGridSpec(num_scalar_prefetch=1, grid=grid, in_specs=in_specs,
                                                         out_specs=out_specs, scratch_shapes=scratch_shapes))
        args = (prefetch,) + args
    outs = _pc(wrapped, name=name, out_shape=out_shape, input_output_aliases=aliases, compiler_params=_params(*sem), **kw)(*args)
    return outs[:n_out], outs[n_out:]


def _comm_call(comm, name):
    def body(*refs):
        c_in, c_out = len(comm.args), len(comm.out_shape)
        cin, cout, (send_sems, recv_sems) = refs[:c_in], refs[c_in:c_in + c_out], refs[c_in + c_out:]
        for phase in ("first", "late", "last"):
            for ph, fn in comm.stages:
                if ph == phase:
                    fn(cin, cout, send_sems, recv_sems)

    return _pc(body, name=name, in_specs=[HBM_SPEC] * len(comm.args), out_specs=[HBM_SPEC] * len(comm.out_shape),
               out_shape=list(comm.out_shape), scratch_shapes=[pltpu.SemaphoreType.DMA((comm.n_sems,))] * 2,
               input_output_aliases=dict(comm.aliases))(*comm.args)


def _sequencer_call(comm, name, collective_id, peers_of):
    hbm = pltpu.MemorySpace.HBM
    cin = [jax.new_ref(a, memory_space=hbm) for a in comm.args]
    cout = [jax.empty_ref(jax.ShapeDtypeStruct(o.shape, o.dtype), memory_space=hbm) for o in comm.out_shape]

    @pl.kernel(mesh=plsc.ScalarSubcoreMesh(axis_name="sequencer", num_cores=1), name=name,
               scratch_types=(pltpu.SemaphoreType.DMA((comm.n_sems,)), pltpu.SemaphoreType.DMA((comm.n_sems,))),
               compiler_params=pltpu.CompilerParams(collective_id=collective_id))
    def launch(send_sems, recv_sems):
        barrier = pltpu.get_barrier_semaphore()
        peers = peers_of()
        for peer in peers:
            pl.semaphore_signal(barrier, inc=1, device_id=peer, device_id_type=MESH)
        pl.semaphore_wait(barrier, len(peers))
        for phase in ("first", "late", "last"):
            for ph, fn in comm.stages:
                if ph == phase:
                    fn(cin, cout, send_sems, recv_sems)

    launch()
    return [r[...] for r in cout]


def _sibling_peer():
    x, y, c, _ = _place()
    return [(x, y, 1 - c)]


def _chip_peers():
    x, y, c, _ = _place()
    return [_flip(x, y, c, r) for r in (1, 2, 3)]


def _gather_comm(bufs, whole=None, eager=0):
    n = len(bufs)
    halves = [b.shape[1] // 2 for b in bufs]
    k_ici = lambda a, r: 3 * a + r - 1
    k_d2d = lambda a, r: 3 * n + 3 * a + r - 1
    k_whole = lambda r: 6 * n + r - 1

    def half(ref, slot, c, a):
        return ref.at[slot, pl.ds(c * halves[a], halves[a])]

    def send(cin, cout, ss, rs):
        x, y, c, j = _place()
        for a in range(n):
            mine = half(cout[a], j, c, a)
            for r in (1, 2, 3):
                _remote(mine, mine, ss, rs, k_ici(a, r), _flip(x, y, c, r)).start()
        if whole is not None:
            for r in (1, 2, 3):
                _remote(cout[n].at[j], cout[n].at[j], ss, rs, k_whole(r), _flip(x, y, c, r)).start()

    def pass_on(cout, ss, rs, a, r):
        x, y, c, j = _place()
        landed = half(cout[a], j ^ r, c, a)
        _remote(landed, landed, ss, rs, k_ici(a, r), (x, y, 1 - c)).wait_recv()
        _remote(landed, landed, ss, rs, k_d2d(a, r), (x, y, 1 - c)).start()

    def passed_on(cout, ss, rs, a, r):
        x, y, c, j = _place()
        theirs = half(cout[a], j ^ r, 1 - c, a)
        _remote(theirs, theirs, ss, rs, k_d2d(a, r), (x, y, 1 - c)).wait_recv()

    def arrive(r, cin, cout, ss, rs):
        for a in range(eager):
            pass_on(cout, ss, rs, a, r)
        for a in range(eager):
            passed_on(cout, ss, rs, a, r)

    def forward(cin, cout, ss, rs):
        for a in range(eager, n):
            for r in (1, 2, 3):
                pass_on(cout, ss, rs, a, r)

    def finish(cin, cout, ss, rs):
        x, y, c, j = _place()
        sibling = (x, y, 1 - c)
        for a in range(eager, n):
            for r in (1, 2, 3):
                passed_on(cout, ss, rs, a, r)
        for a in range(n):
            mine = half(cout[a], j, c, a)
            for r in (1, 2, 3):
                _remote(mine, mine, ss, rs, k_ici(a, r), sibling).wait_send()
                landed = half(cout[a], j ^ r, c, a)
                _remote(landed, landed, ss, rs, k_d2d(a, r), sibling).wait_send()
        if whole is not None:
            for r in (1, 2, 3):
                cp = _remote(cout[n].at[j ^ r], cout[n].at[j ^ r], ss, rs, k_whole(r), sibling)
                cp.wait_recv()
                cp.wait_send()

    args = tuple(bufs) + ((whole,) if whole is not None else ())
    out_shape = tuple(jax.ShapeDtypeStruct(b.shape, b.dtype) for b in args)
    aliases = {a: a for a in range(len(args))}
    arrivals = tuple(((r, 0), functools.partial(arrive, r)) for r in (1, 2, 3)) if eager else ()
    return _Comm(args, out_shape, aliases, 6 * n + 3, (("first", send),) + arrivals + (("late", forward), ("last", finish)))


def _pair_comm(grads, small=None):
    n = len(grads)
    halves = [g.shape[1] // 2 for g in grads]

    def copies(cin, cout, ss, rs):
        x, y, c, _ = _place()
        sibling = (x, y, 1 - c)
        cps = [_remote(cin[a].at[:, pl.ds((1 - c) * halves[a], halves[a]), :], cout[a], ss, rs, a, sibling) for a in range(n)]
        if small is not None:
            cps.append(_remote(cin[n], cout[n], ss, rs, n, sibling))
        return cps

    def start(cin, cout, ss, rs):
        for cp in copies(cin, cout, ss, rs):
            cp.start()

    def finish(cin, cout, ss, rs):
        for cp in copies(cin, cout, ss, rs):
            cp.wait()

    args = tuple(grads) + ((small,) if small is not None else ())
    out_shape = tuple(jax.ShapeDtypeStruct((N_CHIP, h, g.shape[2]), F32) for g, h in zip(grads, halves))
    out_shape += (jax.ShapeDtypeStruct(small.shape, F32),) if small is not None else ()
    return _Comm(args, out_shape, {}, n + 1, (("first", start), ("last", finish)))


def _chips_comm(sums_bf, small=None):
    n = len(sums_bf)

    def copies(cin, cout, ss, rs):
        x, y, c, j = _place()
        cps = []
        for r in (1, 2, 3):
            peer = _flip(x, y, c, r)
            for a in range(n):
                cps.append(_remote(cin[a].at[j ^ r], cout[a].at[r - 1], ss, rs, (n + 1) * (r - 1) + a, peer))
            if small is not None:
                cps.append(_remote(cin[n], cout[n].at[r - 1], ss, rs, (n + 1) * (r - 1) + n, peer))
        return cps

    def start(cin, cout, ss, rs):
        for cp in copies(cin, cout, ss, rs):
            cp.start()

    def finish(cin, cout, ss, rs):
        for cp in copies(cin, cout, ss, rs):
            cp.wait()

    args = tuple(sums_bf) + ((small,) if small is not None else ())
    out_shape = tuple(jax.ShapeDtypeStruct((3,) + s.shape[1:], BF) for s in sums_bf)
    out_shape += (jax.ShapeDtypeStruct((3,) + small.shape, F32),) if small is not None else ()
    return _Comm(args, out_shape, {}, 3 * (n + 1), (("first", start), ("last", finish)))


def _join_comm(shards):
    n = len(shards)
    halves = [s.shape[0] // 2 for s in shards]

    def start(cin, cout, ss, rs):
        x, y, c, _ = _place()
        for a in range(n):
            mine = cout[a].at[pl.ds(c * halves[a], halves[a]), :]
            _remote(mine, mine, ss, rs, a, (x, y, 1 - c)).start()

    def finish(cin, cout, ss, rs):
        x, y, c, _ = _place()
        for a in range(n):
            theirs = cout[a].at[pl.ds((1 - c) * halves[a], halves[a]), :]
            cp = _remote(theirs, theirs, ss, rs, a, (x, y, 1 - c))
            cp.wait_recv()
            cp.wait_send()

    out_shape = tuple(jax.ShapeDtypeStruct(s.shape, F32) for s in shards)
    return _Comm(tuple(shards), out_shape, {a: a for a in range(n)}, n, (("first", start), ("last", finish)))


def _cast_shards(w_in, w_pa, w_pb, w_o, w_ff1, w_ff2, chip):
    def body(j_ref, win_ref, wpa_ref, wpb_ref, wo_ref, wff1_ref, wff2_ref, win4_ref, proj4_ref, ff14_ref, ff24_ref):
        win4_ref[...] = win_ref[...].astype(BF)
        for name, ref in (("w_pa", wpa_ref), ("w_pb", wpb_ref), ("w_o", wo_ref)):
            off, rows = PROJ_OFF[name]
            proj4_ref[off:off + rows, :] = ref[...].astype(BF)
        ff14_ref[...] = wff1_ref[...].astype(BF)
        ff24_ref[...] = wff2_ref[...].astype(BF)

    whole = lambda a: pl.BlockSpec(a.shape, lambda i, j: (0, 0), pipeline_mode=pl.Buffered(1))
    slot = lambda rows, cols: pl.BlockSpec((None, rows, cols), lambda i, j: (j[0], 0, 0))
    ws = (w_in, w_pa, w_pb, w_o, w_ff1, w_ff2)
    shapes = ((D, NP_SHARD), (PROJ_TOTAL, D), (D, FF_SHARD), (FF_SHARD, D))
    return _pc(
        body, name="cast_shards",
        grid_spec=pltpu.PrefetchScalarGridSpec(num_scalar_prefetch=1, grid=(1,), in_specs=[whole(w) for w in ws],
                                               out_specs=[slot(*s) for s in shapes]),
        out_shape=[jax.ShapeDtypeStruct((N_CHIP,) + s, BF) for s in shapes],
        compiler_params=_params("arbitrary"))(chip, *ws)


def _proj_fwd(x, chip, tm, comm):
    t = x.shape[0]
    sub = tm // EDGE_TILE

    def body(x_ref, p_ref, edge_ref, w_ref, w_sem, comm_refs):
        @pl.when(pl.program_id(1) == 0)
        def _():
            _, _, _, j = _place()
            block = pltpu.make_async_copy(comm_refs[0].at[j ^ pl.program_id(0)], w_ref, w_sem)
            block.start()
            block.wait()

        p_ref[...] = _dot(x_ref[...].astype(BF), w_ref[...])
        _write_edges(edge_ref, p_ref, tm)

    return _host_call(
        body, comm, name="proj_fwd", grid=(N_CHIP, t // tm), prefetch=chip, body_reads_comm=True,
        in_specs=[pl.BlockSpec((tm, D), lambda r, i, j: (i, 0))],
        out_specs=[pl.BlockSpec((tm, NP_SHARD), lambda r, i, j: (i, j[0] ^ r)),
                   pl.BlockSpec((sub, 2 * HALO, NP_SHARD), lambda r, i, j: (i, 0, j[0] ^ r))],
        out_shape=[jax.ShapeDtypeStruct((t, N_PROJ), F32), jax.ShapeDtypeStruct((t // EDGE_TILE, 2 * HALO, N_PROJ), F32)],
        scratch_shapes=[pltpu.VMEM((D, NP_SHARD), BF), pltpu.SemaphoreType.DMA],
        args=(x,))


def _edge_specs(t, tm, w):
    k, last = tm // EDGE_TILE, t // EDGE_TILE - 1
    return [pl.BlockSpec((None, HALO, w), lambda i: (jnp.maximum(i * k - 1, 0), 1, 0)),
            pl.BlockSpec((None, HALO, w), lambda i: (jnp.minimum((i + 1) * k, last), 0, 0))]


def _write_edges(edge_ref, rows, tm):
    for s in range(tm // EDGE_TILE):
        edge_ref[s, 0:HALO, :] = rows[s * EDGE_TILE:s * EDGE_TILE + HALO, :]
        edge_ref[s, HALO:2 * HALO, :] = rows[(s + 1) * EDGE_TILE - HALO:(s + 1) * EDGE_TILE, :]


def _pcols(p_ref, lo, hi):
    return p_ref[:, lo:hi]


def _end_masks(nt):
    i = pl.program_id(0)
    return (i > 0).astype(F32), (i < nt - 1).astype(F32)


def _conv_fwd(p_ref, prev_ref, next_ref, cw_ref, tm, has_prev, has_next):
    ca = _pcols(p_ref, OFF_CA, OFF_HA)
    ha = _pcols(p_ref, OFF_HA, OFF_UB)
    ch = ca * ha
    ch_prev = prev_ref[HALO - 1:HALO, OFF_CA:OFF_HA] * prev_ref[HALO - 1:HALO, OFF_HA:OFF_UB] * has_prev
    ch_next = next_ref[0:1, OFF_CA:OFF_HA] * next_ref[0:1, OFF_HA:OFF_UB] * has_next
    row = lax.broadcasted_iota(jnp.int32, (tm, W_A), 0)
    ch_m1 = jnp.where(row == 0, ch_prev, pltpu.roll(ch, 1, 0))
    ch_p1 = jnp.where(row == tm - 1, ch_next, pltpu.roll(ch, tm - 1, 0))
    cv = cw_ref[0:1, :] * ch_m1 + cw_ref[1:2, :] * ch + cw_ref[2:3, :] * ch_p1
    return ca, ha, ch, ch_m1, ch_p1, cv


def _spatial_fwd(p_ref, vg_ref, vb_ref, ws_ref, bsf_ref, vnb_ref, mixed_ref, tm):
    vb_pre = _pcols(p_ref, OFF_VB, OFF_GA)
    gv, tv = _gelu(vb_pre)
    xhv, rstdv = _ln_stats(gv)
    vnb_ref[...] = (xhv * vg_ref[...] + vb_ref[...]).astype(BF)
    for c in range(tm // CHUNK):
        rows = slice(c * CHUNK, (c + 1) * CHUNK)
        for h in range(N_HEAD):
            cols = slice(h * CHUNK, (h + 1) * CHUNK)
            mixed_ref[rows, cols] = _dot(ws_ref[h], vnb_ref[rows, cols]) + bsf_ref[:, cols]
    return vb_pre, tv, xhv, rstdv


def _mix_fwd(p, pedge, x, wpa, wpb, wo, wsb, bsf, bg, cw, vg, vb, tm):
    t = x.shape[0]
    nt = t // tm

    def body(p_ref, prev_ref, next_ref, x_ref, wpa_ref, wpb_ref, wo_ref, ws_ref, bsf_ref, bg_ref, cw_ref, vg_ref, vb_ref,
             r1_ref, ya_ref, yb_ref, vnb_ref, mixed_ref):
        has_prev, has_next = _end_masks(nt)
        _, _, _, _, _, cv = _conv_fwd(p_ref, prev_ref, next_ref, cw_ref, tm, has_prev, has_next)
        a = _pcols(p_ref, 0, OFF_CA) * cv
        ya = _dot(a.astype(BF), wpa_ref[...])
        ya_ref[...] = ya
        _spatial_fwd(p_ref, vg_ref, vb_ref, ws_ref, bsf_ref, vnb_ref, mixed_ref, tm)
        gu, _ = _gelu(_pcols(p_ref, OFF_UB, OFF_VB))
        bb = gu * mixed_ref[...]
        yb = _dot(bb.astype(BF), wpb_ref[...])
        yb_ref[...] = yb
        ga = jax.nn.sigmoid(_pcols(p_ref, OFF_GA, OFF_GB) + bg_ref[:, 0:D])
        gb = jax.nn.sigmoid(_pcols(p_ref, OFF_GB, N_PROJ) + bg_ref[:, D:2 * D])
        z = ga * ya + gb * yb
        r1_ref[...] = ALPHA * x_ref[...] + _dot(z.astype(BF), wo_ref[...])

    tile = lambda w: pl.BlockSpec((tm, w), lambda i: (i, 0))
    return _pc(
        body, name="mix_fwd", grid=(nt,),
        in_specs=[tile(N_PROJ), *_edge_specs(t, tm, N_PROJ), tile(D),
                  _resident((W_A, D)), _resident((W_B, D)), _resident((D, D)), _resident((N_HEAD, CHUNK, CHUNK)),
                  _resident((CHUNK, W_B)), _resident((1, 2 * D)), _resident((3, W_A)), _resident((1, W_B)),
                  _resident((1, W_B))],
        out_specs=[tile(D), tile(D), tile(D)],
        out_shape=[jax.ShapeDtypeStruct((t, D), F32)] * 3,
        scratch_shapes=[pltpu.VMEM((tm, W_B), BF), pltpu.VMEM((tm, W_B), F32)],
        compiler_params=_params("arbitrary"),
    )(p, pedge, pedge, x, wpa, wpb, wo, wsb, bsf, bg, cw, vg, vb)


def _ffn_fwd_bwd(r1, tgt, wff1, wff2, ln1g, ln1b, ln2g, ln2b, tm):
    t = r1.shape[0]

    def body(r1_ref, tgt_ref, w1_ref, w2_ref, g1_ref, b1_ref, g2_ref, b2_ref,
             dr1_ref, dr1b_ref, dedge_ref, x1b_ref, hidb_ref, dh1b_ref, dr2b_ref, acc_ref, relu_ref):
        @pl.when(pl.program_id(0) == 0)
        def _():
            acc_ref[...] = jnp.zeros_like(acc_ref)

        xh1, rstd1 = _ln_stats(r1_ref[...])
        x1 = xh1 * g1_ref[...] + b1_ref[...]
        x1b_ref[...] = x1.astype(BF)
        ffn = jnp.zeros((tm, D), F32)
        for j in range(N_CHIP):
            cols = slice(j * FF_SHARD, (j + 1) * FF_SHARD)
            r = jnp.maximum(_dot(x1b_ref[...], w1_ref[j]), 0.0)
            relu_ref[:, cols] = r
            hidb_ref[:, cols] = (r * r).astype(BF)
            ffn = ffn + _dot(hidb_ref[:, cols], w2_ref[cols, :])
        xh2, rstd2 = _ln_stats(ALPHA * x1 + ffn)
        diff = xh2 * g2_ref[...] + b2_ref[...] - tgt_ref[...]
        acc_ref[4:5, :] += _colsum(diff * diff)
        dx2 = diff * (1.0 / D)
        acc_ref[2:3, :] += _colsum(dx2 * xh2)
        acc_ref[3:4, :] += _colsum(dx2)
        dr2 = _ln_bwd(dx2, g2_ref[...], xh2, rstd2)
        dr2b_ref[...] = dr2.astype(BF)
        dx1 = ALPHA * dr2
        for j in range(N_CHIP):
            cols = slice(j * FF_SHARD, (j + 1) * FF_SHARD)
            dhid = _dot_nt(dr2b_ref[...], w2_ref[cols, :])
            dh1b_ref[:, cols] = (dhid * (2.0 * relu_ref[:, cols])).astype(BF)
            dx1 = dx1 + _dot_nt(dh1b_ref[:, cols], w1_ref[j])
        acc_ref[0:1, :] += _colsum(dx1 * xh1)
        acc_ref[1:2, :] += _colsum(dx1)
        dr1 = _ln_bwd(dx1, g1_ref[...], xh1, rstd1)
        dr1_ref[...] = dr1
        dr1b_ref[...] = dr1.astype(BF)
        _write_edges(dedge_ref, dr1_ref, tm)

    tile = lambda w: pl.BlockSpec((tm, w), lambda i: (i, 0))
    vec = _resident((1, D))
    return _pc(
        body, name="ffn_fwd_bwd", grid=(t // tm,),
        in_specs=[tile(D), tile(D), _resident((N_CHIP, D, FF_SHARD)), _resident((D_FF, D)), vec, vec, vec, vec],
        out_specs=[tile(D), tile(D), pl.BlockSpec((tm // EDGE_TILE, 2 * HALO, D), lambda i: (i, 0, 0)), tile(D), tile(D_FF),
                   tile(D_FF), tile(D), pl.BlockSpec((8, D), lambda i: (0, 0))],
        out_shape=[jax.ShapeDtypeStruct((t, D), F32), jax.ShapeDtypeStruct((t, D), BF),
                   jax.ShapeDtypeStruct((t // EDGE_TILE, 2 * HALO, D), F32), jax.ShapeDtypeStruct((t, D), BF),
                   jax.ShapeDtypeStruct((t, D_FF), BF), jax.ShapeDtypeStruct((t, D_FF), BF),
                   jax.ShapeDtypeStruct((t, D), BF), jax.ShapeDtypeStruct((8, D), F32)],
        scratch_shapes=[pltpu.VMEM((tm, D_FF), F32)],
        compiler_params=_params("arbitrary"),
    )(r1, tgt, wff1, wff2, ln1g, ln1b, ln2g, ln2b)


def _dw(a, b, nblk, am, bn, a_blocked, b_blocked, tk, name, comm=None, after=None):
    t = a.shape[0]

    def body(a_ref, b_ref, *rest):
        o_ref = rest[-1]

        @pl.when(pl.program_id(1) == 0)
        def _():
            o_ref[...] = jnp.zeros_like(o_ref)

        o_ref[...] += _dot_tn(a_ref[...].astype(BF), b_ref[...])

    outs, got = _host_call(
        body, comm, name=name, grid=(nblk, t // tk),
        in_specs=[pl.BlockSpec((tk, am), (lambda j, k: (k, j)) if a_blocked else (lambda j, k: (k, 0))),
                  pl.BlockSpec((tk, bn), (lambda j, k: (k, j)) if b_blocked else (lambda j, k: (k, 0)))]
        + ([] if after is None else [pl.BlockSpec(memory_space=pl.ANY)]),
        out_specs=[pl.BlockSpec((None, am, bn), lambda j, k: (j, 0, 0))],
        out_shape=[jax.ShapeDtypeStruct((nblk, am, bn), F32)], args=(a, b) + (() if after is None else (after,)))
    return outs[0] if comm is None else (outs[0], got)


def _dw_proj(ab, dyab, bbb, dybb, zb, dr1b, tk, comm):
    t = ab.shape[0]
    pairs = (("w_pa", 0, 1), ("w_pb", 2, 3), ("w_o", 4, 5))

    def body(*refs):
        o_ref = refs[6]

        @pl.when(pl.program_id(0) == 0)
        def _():
            o_ref[...] = jnp.zeros_like(o_ref)

        for name, ia, ib in pairs:
            off, rows = PROJ_OFF[name]
            for k in range(N_CHIP):
                o_ref[k, off:off + rows, :] += _dot_tn(refs[ia][:, k * rows:(k + 1) * rows], refs[ib][...])

    tile = lambda w: pl.BlockSpec((tk, w), lambda i: (i, 0))
    outs, got = _host_call(
        body, comm, name="dw_proj", grid=(t // tk,), in_specs=[tile(W_A), tile(D), tile(W_B), tile(D), tile(D), tile(D)],
        out_specs=[pl.BlockSpec((N_CHIP, PROJ_TOTAL, D), lambda i: (0, 0, 0))],
        out_shape=[jax.ShapeDtypeStruct((N_CHIP, PROJ_TOTAL, D), F32)], args=(ab, dyab, bbb, dybb, zb, dr1b))
    return outs[0], got


def _dx(dp, win4, dr1, tm, blk0, nblk, filled, name, comm, after=None):
    t = dp.shape[0]

    def body(dp_ref, w_ref, dr1_ref, *rest):
        dx = ALPHA * dr1_ref[...]
        for j in range(N_CHIP):
            dx = dx + _dot_nt(dp_ref[:, j * NP_SHARD:(j + 1) * NP_SHARD], w_ref[j])
        rest[-1][...] = dx

    in_specs = [pl.BlockSpec((tm, N_PROJ), lambda i: (i + blk0, 0)), _resident((N_CHIP, D, NP_SHARD)),
                pl.BlockSpec((tm, D), lambda i: (i + blk0, 0))]
    args = (dp, win4, dr1)
    aliases = None
    if filled is not None:
        in_specs.append(pl.BlockSpec(memory_space=pl.ANY))
        args += (filled,)
        aliases = {3: 0}
    if after is not None:
        in_specs.append(pl.BlockSpec(memory_space=pl.ANY))
        args += (after,)
    outs, got = _host_call(
        body, comm, name=name, grid=(nblk,), in_specs=in_specs, out_specs=[pl.BlockSpec((tm, D), lambda i: (i + blk0, 0))],
        out_shape=[jax.ShapeDtypeStruct((t, D), F32)], args=args, aliases=aliases)
    return outs[0], got


def _mix_bwd(p, pedge, dr1, dedge, ya, yb, wpa, wpb, wo, wsb, wstb, bsf, bg, cw, vg, vb, tm, comm):
    t = p.shape[0]
    nt = t // tm
    te = tm + 2 * HALO
    mid = slice(HALO, HALO + tm)

    def body(p_ref, prev_ref, next_ref, dr1_ref, dprev_ref, dnext_ref, ya_ref, yb_ref, wpa_ref, wpb_ref, wo_ref,
             ws_ref, wst_ref, bsf_ref, bg_ref, cw_ref, vg_ref, vb_ref,
             dp_ref, ab_ref, bbb_ref, zb_ref, dyab_ref, dybb_ref, dbg_ref, dcw_ref, dvgb_ref, dws_ref, dbs_ref,
             vnb_ref, mixed_ref, dmixb_ref, dvn_ref):
        @pl.when(pl.program_id(0) == 0)
        def _():
            for r in (dbg_ref, dcw_ref, dvgb_ref, dws_ref, dbs_ref):
                r[...] = jnp.zeros_like(r)

        has_prev, has_next = _end_masks(nt)
        ca, ha, ch, ch_m1, ch_p1, cv = _conv_fwd(p_ref, prev_ref, next_ref, cw_ref, tm, has_prev, has_next)
        ba = _pcols(p_ref, 0, OFF_CA)
        ab_ref[...] = (ba * cv).astype(BF)
        vb_pre, tv, xhv, rstdv = _spatial_fwd(p_ref, vg_ref, vb_ref, ws_ref, bsf_ref, vnb_ref, mixed_ref, tm)
        ub = _pcols(p_ref, OFF_UB, OFF_VB)
        gu, tu = _gelu(ub)
        bbb_ref[...] = (gu * mixed_ref[...]).astype(BF)
        bga = bg_ref[:, 0:D]
        ga = jax.nn.sigmoid(_pcols(p_ref, OFF_GA, OFF_GB) + bga)
        gb = jax.nn.sigmoid(_pcols(p_ref, OFF_GB, N_PROJ) + bg_ref[:, D:2 * D])
        ya = ya_ref[...]
        yb = yb_ref[...]
        zb_ref[...] = (ga * ya + gb * yb).astype(BF)

        dr1_ext = jnp.concatenate([dprev_ref[...] * has_prev, dr1_ref[...], dnext_ref[...] * has_next], axis=0)
        dz_ext = _dot_nt(dr1_ext.astype(BF), wo_ref[...])
        ga_ext = jnp.concatenate([jax.nn.sigmoid(prev_ref[:, OFF_GA:OFF_GB] + bga), ga,
                                  jax.nn.sigmoid(next_ref[:, OFF_GA:OFF_GB] + bga)], axis=0)
        dya_ext = dz_ext * ga_ext
        dyab_ref[...] = dya_ext[mid].astype(BF)
        da_ext = _dot_nt(dya_ext.astype(BF), wpa_ref[...])
        ba_ext = jnp.concatenate([prev_ref[:, 0:OFF_CA], ba, next_ref[:, 0:OFF_CA]], axis=0)
        dcv_ext = da_ext * ba_ext
        dcv = dcv_ext[mid]
        dch = (cw_ref[0:1, :] * pltpu.roll(dcv_ext, te - 1, 0)[mid] + cw_ref[1:2, :] * dcv
               + cw_ref[2:3, :] * pltpu.roll(dcv_ext, 1, 0)[mid])
        dp_ref[:, 0:OFF_CA] = (da_ext[mid] * cv).astype(BF)
        dp_ref[:, OFF_CA:OFF_HA] = (dch * ha).astype(BF)
        dp_ref[:, OFF_HA:OFF_UB] = (dch * ca).astype(BF)
        dcw_ref[0:1, :] += _colsum(dcv * ch_m1)
        dcw_ref[1:2, :] += _colsum(dcv * ch)
        dcw_ref[2:3, :] += _colsum(dcv * ch_p1)

        dz = dz_ext[mid]
        dga = dz * ya * ga * (1.0 - ga)
        dgb = dz * yb * gb * (1.0 - gb)
        dp_ref[:, OFF_GA:OFF_GB] = dga.astype(BF)
        dp_ref[:, OFF_GB:N_PROJ] = dgb.astype(BF)
        dbg_ref[0:1, 0:D] += _colsum(dga)
        dbg_ref[0:1, D:2 * D] += _colsum(dgb)

        dybb_ref[...] = (dz * gb).astype(BF)
        dbb = _dot_nt(dybb_ref[...], wpb_ref[...])
        dp_ref[:, OFF_UB:OFF_VB] = (dbb * mixed_ref[...] * _gelu_grad(ub, tu)).astype(BF)
        dmixed = dbb * gu
        dmixb_ref[...] = dmixed.astype(BF)
        for c in range(tm // CHUNK):
            rows = slice(c * CHUNK, (c + 1) * CHUNK)
            dbs_ref[...] += dmixed[rows]
            for h in range(N_HEAD):
                cols = slice(h * CHUNK, (h + 1) * CHUNK)
                dws_ref[h] += _dot_nt(dmixb_ref[rows, cols], vnb_ref[rows, cols])
                dvn_ref[rows, cols] = _dot(wst_ref[h], dmixb_ref[rows, cols])
        dvn = dvn_ref[...]
        dvgb_ref[0:1, :] += _colsum(dvn * xhv)
        dvgb_ref[1:2, :] += _colsum(dvn)
        dgv = _ln_bwd(dvn, vg_ref[...], xhv, rstdv)
        dp_ref[:, OFF_VB:OFF_GA] = (dgv * _gelu_grad(vb_pre, tv)).astype(BF)

    tile = lambda w: pl.BlockSpec((tm, w), lambda i: (i, 0))
    acc = lambda *s: pl.BlockSpec(s, lambda i: (0,) * len(s))
    return _host_call(
        body, comm, name="mix_bwd", grid=(nt,),
        in_specs=[tile(N_PROJ), *_edge_specs(t, tm, N_PROJ), tile(D), *_edge_specs(t, tm, D), tile(D), tile(D),
                  _resident((W_A, D)), _resident((W_B, D)), _resident((D, D)), _resident((N_HEAD, CHUNK, CHUNK)),
                  _resident((N_HEAD, CHUNK, CHUNK)), _resident((CHUNK, W_B)), _resident((1, 2 * D)),
                  _resident((3, W_A)), _resident((1, W_B)), _resident((1, W_B))],
        out_specs=[tile(N_PROJ), tile(W_A), tile(W_B), tile(D), tile(D), tile(D),
                   acc(8, 2 * D), acc(8, W_A), acc(8, W_B), acc(N_HEAD, CHUNK, CHUNK), acc(CHUNK, W_B)],
        out_shape=[jax.ShapeDtypeStruct((t, N_PROJ), BF), jax.ShapeDtypeStruct((t, W_A), BF),
                   jax.ShapeDtypeStruct((t, W_B), BF), jax.ShapeDtypeStruct((t, D), BF), jax.ShapeDtypeStruct((t, D), BF),
                   jax.ShapeDtypeStruct((t, D), BF), jax.ShapeDtypeStruct((8, 2 * D), F32),
                   jax.ShapeDtypeStruct((8, W_A), F32), jax.ShapeDtypeStruct((8, W_B), F32),
                   jax.ShapeDtypeStruct((N_HEAD, CHUNK, CHUNK), F32), jax.ShapeDtypeStruct((CHUNK, W_B), F32)],
        scratch_shapes=[pltpu.VMEM((tm, W_B), BF), pltpu.VMEM((tm, W_B), F32), pltpu.VMEM((tm, W_B), BF),
                        pltpu.VMEM((tm, W_B), F32)],
        args=(p, pedge, pedge, dr1, dedge, dedge, ya, yb, wpa, wpb, wo, wsb, wstb, bsf, bg, cw, vg, vb))


def _add_own_half(full4, recv4, place, rb, name, after=None):
    n, rh, cols = recv4.shape
    nb = rh // rb

    def body(pl_ref, a_ref, b_ref, *rest):
        own_ref, ob_ref = rest[-2:]
        s = a_ref[...] + b_ref[...]
        ob_ref[...] = s.astype(BF)

        @pl.when(pl.program_id(1) == pl_ref[0])
        def _():
            own_ref[...] = s

    blk = (None, rb, cols)
    return _pc(
        body, name=name,
        grid_spec=pltpu.PrefetchScalarGridSpec(
            num_scalar_prefetch=1, grid=(nb, n),
            in_specs=[pl.BlockSpec(blk, lambda i, k, s: (k, s[1] * nb + i, 0)), pl.BlockSpec(blk, lambda i, k, s: (k, i, 0))]
            + ([] if after is None else [pl.BlockSpec(memory_space=pl.ANY)]),
            out_specs=[pl.BlockSpec((rb, cols), lambda i, k, s: (i, 0)), pl.BlockSpec(blk, lambda i, k, s: (k, i, 0))]),
        out_shape=[jax.ShapeDtypeStruct((rh, cols), F32), jax.ShapeDtypeStruct(recv4.shape, BF)],
        compiler_params=_params("arbitrary", "arbitrary"),
    )(place, full4, recv4, *(() if after is None else (after,)))


def _add_chips(own, r3, place, rb, name, after=None):
    _, rh, cols = r3.shape
    nb = rh // rb

    def body(pl_ref, s_ref, r_ref, *rest):
        rest[-1][...] = ((s_ref[...] + r_ref[0].astype(F32)) + r_ref[1].astype(F32)) + r_ref[2].astype(F32)

    return _pc(
        body, name=name,
        grid_spec=pltpu.PrefetchScalarGridSpec(
            num_scalar_prefetch=1, grid=(nb,),
            in_specs=[pl.BlockSpec((rb, cols), lambda i, s: (i, 0)), pl.BlockSpec((3, rb, cols), lambda i, s: (0, i, 0))]
            + ([] if after is None else [pl.BlockSpec(memory_space=pl.ANY)]),
            out_specs=pl.BlockSpec((rb, cols), lambda i, s: (s[1] * nb + i, 0))),
        out_shape=jax.ShapeDtypeStruct((2 * rh, cols), F32),
        compiler_params=_params("arbitrary"),
    )(place, own, r3, *(() if after is None else (after,)))


def _add_small(a, b):
    def body(a_ref, b_ref, o_ref):
        o_ref[...] = a_ref[...] + b_ref[...]

    return _pc(body, name="add_small_cores", out_shape=jax.ShapeDtypeStruct(a.shape, F32))(a, b)


def _sum_small_chips(own, slots, place):
    def body(pl_ref, own_ref, s_ref, o_ref):
        j = pl_ref[0]

        def term(k):
            return jnp.where(j == k, own_ref[...], s_ref[jnp.maximum((j ^ k) - 1, 0)])

        o_ref[...] = ((term(0) + term(1)) + term(2)) + term(3)

    vmem = pl.BlockSpec(memory_space=pltpu.VMEM)
    return _pc(body, name="sum_small_chips", in_specs=[pl.BlockSpec(memory_space=pltpu.SMEM), vmem, vmem], out_specs=vmem,
               out_shape=jax.ShapeDtypeStruct(own.shape, F32))(place, own, slots)


def _adamw_step(w, g, m, v):
    m2 = ADAM_B1 * m + (1.0 - ADAM_B1) * g
    v2 = ADAM_B2 * v + (1.0 - ADAM_B2) * (g * g)
    m_hat = m2 / (1.0 - ADAM_B1 ** ADAM_STEP)
    v_hat = v2 / (1.0 - ADAM_B2 ** ADAM_STEP)
    return -ADAM_LR * (m_hat / (jnp.sqrt(v_hat) + ADAM_EPS) + ADAM_WD * w), m2, v2


def _adamw(w, g, m, v, rb, name):
    rows, cols = w.shape

    def body(w_ref, g_ref, m_ref, v_ref, d_ref, m2_ref, v2_ref):
        d_ref[...], m2_ref[...], v2_ref[...] = _adamw_step(w_ref[...], g_ref[...], m_ref[...], v_ref[...])

    blk = pl.BlockSpec((rb, cols), lambda i: (i, 0))
    return _pc(body, name=name, grid=(rows // rb,), in_specs=[blk] * 4, out_specs=[blk] * 3,
               out_shape=[jax.ShapeDtypeStruct((rows, cols), F32)] * 3, compiler_params=_params("arbitrary"))(w, g, m, v)


SC_TILES = 32
SC_LANES = 16
SC_ROWS = 16


def _adamw_sc(ws, gs, ms, vs, name):
    n = len(ws)
    rows, cols = ws[0].shape
    per_tile = rows // SC_TILES
    slab_rows = min(SC_ROWS, per_tile)

    def body(*refs):
        ins, outs, (wb, gb, mb, vb, db) = refs[:4 * n], refs[4 * n:7 * n], refs[7 * n:]
        tile = lax.axis_index("sc_subcore") * 2 + lax.axis_index("sc_core")
        for i in range(n):
            for ps in range(per_tile // slab_rows):
                slab = pl.ds(tile * per_tile + ps * slab_rows, slab_rows)
                for k, buf in enumerate((wb, gb, mb, vb)):
                    pltpu.sync_copy(ins[k * n + i].at[slab, :], buf)

                @pl.loop(0, slab_rows)
                def _(r):
                    @pl.loop(0, cols, step=SC_LANES)
                    def _(c):
                        at = (r, pl.ds(c, SC_LANES))
                        db[at], mb[at], vb[at] = _adamw_step(wb[at], gb[at], mb[at], vb[at])

                for k, buf in enumerate((db, mb, vb)):
                    pltpu.sync_copy(buf, outs[k * n + i].at[slab, :])

    outs = pl.kernel(
        body, name=name, out_type=[jax.ShapeDtypeStruct((rows, cols), F32)] * (3 * n),
        mesh=plsc.VectorSubcoreMesh(core_axis_name="sc_core", subcore_axis_name="sc_subcore"),
        scratch_types=[pltpu.VMEM((slab_rows, cols), F32)] * 5,
    )(*ws, *gs, *ms, *vs)
    return outs[:n], outs[n:2 * n], outs[2 * n:]


def _adamw_small(ws, gs, ms, vs):
    n = len(ws)

    def body(*refs):
        ins, outs = refs[:4 * n], refs[4 * n:]
        for i in range(n):
            outs[i][...], outs[n + i][...], outs[2 * n + i][...] = _adamw_step(*(ins[k * n + i][...] for k in range(4)))

    outs = _pc(body, name="adamw_small", out_shape=[jax.ShapeDtypeStruct(w.shape, F32) for w in ws] * 3)(*ws, *gs, *ms, *vs)
    return outs[:n], outs[n:2 * n], outs[2 * n:]


LANES = 128
SMALL_GRADS = (("b_gate", 2 * D), ("conv_w", 3 * W_A), ("v_norm_g", W_B), ("v_norm_b", W_B),
               ("w_s", N_HEAD * CHUNK * CHUNK), ("b_s", N_HEAD * CHUNK), ("ln1_g", D), ("ln1_b", D), ("ln2_g", D), ("ln2_b", D),
               ("loss", 1))


def _pack_rows(parts):
    rows = []
    for a in parts:
        a = a.reshape(-1)
        a = jnp.pad(a, (0, (-a.shape[0]) % LANES))
        rows.append(a.reshape(-1, LANES))
    out = jnp.concatenate(rows, axis=0)
    return jnp.pad(out, ((0, (-out.shape[0]) % 8), (0, 0)))


def _unpack_rows(buf, sizes):
    out, r = [], 0
    for n in sizes:
        nr = -(-n // LANES)
        out.append(buf[r:r + nr].reshape(-1)[:n])
        r += nr
    return out


TM_PROJ = 1024
TM_MIX = 256
TM_DX = 512
DX_PAIR = 6
TK_DW = 4096
TK_DW_IN = 2048
TK_DW_PROJ = 1024
ADD_BLOCK_BYTES = 3 * 1024 * 1024
RB_ADAM = 128
CONV_ROWS = 8


def _add_rows(rows, cols):
    while rows * cols * 4 > ADD_BLOCK_BYTES and rows % 32 == 0:
        rows //= 2
    return rows


def _reduce_adds_1(grads, recvs, place, tag, after=None):
    out = [_add_own_half(g, r, place, _add_rows(*r.shape[1:]), f"add_cores_{tag}{a}", after) for a, (g, r) in enumerate(zip(grads, recvs))]
    return [o[0] for o in out], [o[1] for o in out]


def _reduce_adds_2(sums, recvs, place, tag, after=None):
    return [_add_chips(s, r, place, _add_rows(*r.shape[1:]), f"add_chips_{tag}{a}", after) for a, (s, r) in enumerate(zip(sums, recvs))]


def kernel(x, w_in, b_gate, conv_w, v_norm_g, v_norm_b, w_s, b_s, w_pa, w_pb, w_o, ln1_g, ln1_b, w_ff1, w_ff2, ln2_g, ln2_b, loss_target, m_w_in, m_b_gate, m_conv_w, m_v_norm_g, m_v_norm_b, m_w_s, m_b_s, m_w_pa, m_w_pb, m_w_o, m_ln1_g, m_ln1_b, m_w_ff1, m_w_ff2, m_ln2_g, m_ln2_b, v_w_in, v_b_gate, v_conv_w, v_v_norm_g, v_v_norm_b, v_w_s, v_b_s, v_w_pa, v_w_pb, v_w_o, v_ln1_g, v_ln1_b, v_w_ff1, v_w_ff2, v_ln2_g, v_ln2_b):
    t = x.shape[1]
    core = lax.axis_index("c").astype(jnp.int32).reshape(1)
    chip_idx = 2 * lax.axis_index("x") + lax.axis_index("y")
    chip = chip_idx.astype(jnp.int32).reshape(1)
    place = jnp.concatenate([chip, core])
    x2 = x.reshape(t, D)
    tgt = loss_target.reshape(t, D)

    win4, proj4, ff14, ff24 = _cast_shards(w_in[0], w_pa[0], w_pb[0], w_o[0], w_ff1[0], w_ff2[0], chip)
    conv4 = lax.dynamic_update_slice(jnp.zeros((N_CHIP, CONV_ROWS, W_A // N_CHIP), F32),
                                     jnp.pad(conv_w[0], ((0, CONV_ROWS - 3), (0, 0)))[None], (chip_idx, 0, 0))
    (p, pedge), (win4, proj4, ff14, ff24, conv4) = _proj_fwd(
        x2, chip, TM_PROJ, _gather_comm([win4, proj4, ff14, ff24], conv4, eager=1))

    def full(name, rows_total):
        off, rows = PROJ_OFF[name]
        return proj4[:, off:off + rows, :].reshape(rows_total, D)

    wpa, wpb, wo = full("w_pa", W_A), full("w_pb", W_B), full("w_o", D)
    wff2 = ff24.reshape(D_FF, D)
    cw = jnp.transpose(conv4[:, :3, :], (1, 0, 2)).reshape(3, W_A)
    wsb = w_s[0].astype(BF)
    wstb = jnp.swapaxes(w_s[0], 1, 2).astype(BF)
    bsf = jnp.repeat(jnp.transpose(b_s[0]), CHUNK, axis=1)

    r1, ya, yb = _mix_fwd(p, pedge, x2, wpa, wpb, wo, wsb, bsf, b_gate, cw, v_norm_g, v_norm_b, TM_MIX)
    dr1, dr1b, dedge, x1b, hidb, dh1b, dr2b, acc = _ffn_fwd_bwd(r1, tgt, ff14, wff2, ln1_g, ln1_b, ln2_g, ln2_b, TM_MIX)
    g_ff = [_dw(x1b, dh1b, N_CHIP, D, FF_SHARD, False, True, TK_DW, "dw_ff1"),
            _dw(hidb, dr2b, N_CHIP, FF_SHARD, D, True, False, TK_DW, "dw_ff2")]
    r_ff = _sequencer_call(_pair_comm(g_ff), "pair_ff", 3, _sibling_peer)
    (dp, ab, bbb, zb, dyab, dybb, dbg, dcw, dvgb, dws, dbs_sum), _ = _mix_bwd(
        p, pedge, dr1, dedge, ya, yb, wpa, wpb, wo, wsb, wstb, bsf, b_gate, cw, v_norm_g, v_norm_b, TM_MIX, None)
    s_ff, sb_ff = _reduce_adds_1(g_ff, r_ff, place, "ff", after=dp)
    c_ff = _sequencer_call(_chips_comm(sb_ff), "chips_ff", 4, _chip_peers)
    dwin4 = _dw(x2, dp, N_CHIP, D, NP_SHARD, False, True, TK_DW_IN, "dw_in", after=sb_ff[0])
    f_ff = _reduce_adds_2(s_ff, c_ff, place, "ff", after=dwin4)
    dproj4, (g_ff1, g_ff2) = _dw_proj(ab, dyab, bbb, dybb, zb, dr1b, TK_DW_PROJ, _join_comm(f_ff))
    dbs = jnp.transpose(jnp.sum(dbs_sum.reshape(CHUNK, N_HEAD, CHUNK), axis=-1))
    small = _pack_rows([dbg[0], dcw[0:3], dvgb[0], dvgb[1], dws, dbs, acc[0], acc[1], acc[2], acc[3],
                        0.5 * jnp.sum(acc[4]) / D])
    g_rest = [dwin4, dproj4]
    nblk = t // TM_DX
    n_a = max(1, min(DX_PAIR, nblk // 4))
    r_rest = _sequencer_call(_pair_comm(g_rest, small), "pair_rest", 1, _sibling_peer)
    dx, _ = _dx(dp, win4, dr1, TM_DX, 0, n_a, None, "dx_a", None)
    s_rest, sb_rest = _reduce_adds_1(g_rest, r_rest[:2], place, "rest", after=dx)
    csmall = _add_small(small, r_rest[2])
    c_rest = _sequencer_call(_chips_comm(sb_rest, csmall), "chips_rest", 2, _chip_peers)
    dx, _ = _dx(dp, win4, dr1, TM_DX, n_a, nblk - n_a, dx, "dx_b", None, after=sb_rest[0])
    f_rest = _reduce_adds_2(s_rest, c_rest[:2], place, "rest")
    gsmall = _sum_small_chips(csmall, c_rest[2], place)
    g_in, g_proj = _comm_call(_join_comm(f_rest), "join_rest")

    grads = {"w_in": g_in, "w_ff1": g_ff1, "w_ff2": g_ff2}
    for name, _ in PROJ_ROWS:
        off, rows = PROJ_OFF[name]
        grads[name] = g_proj[off:off + rows, :]
    for (name, n), flat in zip(SMALL_GRADS, _unpack_rows(gsmall, [n for _, n in SMALL_GRADS])):
        grads[name] = flat
    loss = grads.pop("loss").reshape(())
    grads["conv_w"] = lax.dynamic_slice(grads["conv_w"].reshape(3, W_A), (0, chip_idx * (W_A // N_CHIP)), (3, W_A // N_CHIP))

    weights = dict(w_in=w_in, b_gate=b_gate, conv_w=conv_w, v_norm_g=v_norm_g, v_norm_b=v_norm_b, w_s=w_s, b_s=b_s,
                   w_pa=w_pa, w_pb=w_pb, w_o=w_o, ln1_g=ln1_g, ln1_b=ln1_b, w_ff1=w_ff1, w_ff2=w_ff2, ln2_g=ln2_g, ln2_b=ln2_b)
    mom1 = dict(w_in=m_w_in, b_gate=m_b_gate, conv_w=m_conv_w, v_norm_g=m_v_norm_g, v_norm_b=m_v_norm_b, w_s=m_w_s,
                b_s=m_b_s, w_pa=m_w_pa, w_pb=m_w_pb, w_o=m_w_o, ln1_g=m_ln1_g, ln1_b=m_ln1_b, w_ff1=m_w_ff1,
                w_ff2=m_w_ff2, ln2_g=m_ln2_g, ln2_b=m_ln2_b)
    mom2 = dict(w_in=v_w_in, b_gate=v_b_gate, conv_w=v_conv_w, v_norm_g=v_v_norm_g, v_norm_b=v_v_norm_b, w_s=v_w_s,
                b_s=v_b_s, w_pa=v_w_pa, w_pb=v_w_pb, w_o=v_w_o, ln1_g=v_ln1_g, ln1_b=v_ln1_b, w_ff1=v_w_ff1,
                w_ff2=v_w_ff2, ln2_g=v_ln2_g, ln2_b=v_ln2_b)
    order = list(weights)
    big = ("w_in", "w_pa", "w_pb", "w_o", "w_ff1", "w_ff2")
    delta, new_m, new_v = {}, {}, {}
    early = ("w_ff1", "w_ff2")
    late = ("w_pb", "w_o")
    for group, tag in ((early, "adamw_sc"), (late, "adamw_sc_late")):
        ds, ms, vs = _adamw_sc(*([d[n][0] if d is not grads else d[n] for n in group] for d in (weights, grads, mom1, mom2)), tag)
        for name, d_, m_, v_ in zip(group, ds, ms, vs):
            delta[name], new_m[name], new_v[name] = d_, m_, v_
    for name in big:
        if name in early or name in late:
            continue
        w2 = weights[name][0]
        delta[name], new_m[name], new_v[name] = _adamw(w2, grads[name], mom1[name][0], mom2[name][0], RB_ADAM, "adamw_" + name)
    little = [n for n in order if n not in big]
    flat2d = lambda a: a.reshape(-1, a.shape[-1])
    ds, ms, vs = _adamw_small(*([flat2d(d[n].reshape(weights[n].shape)) for n in little] for d in (weights, grads, mom1, mom2)))
    for name, d_, m_, v_ in zip(little, ds, ms, vs):
        delta[name], new_m[name], new_v[name] = d_, m_, v_

    shaped = lambda d: [d[n].reshape(weights[n].shape) for n in order]
    return (loss, dx.reshape(x.shape), *shaped(grads), *shaped(delta), *shaped(new_m), *shaped(new_v))
```
